```python
import functools
import jax, jax.numpy as jnp
from jax import lax
import numpy as np

D_MODEL = 2048
BATCH = 4
SEQ = 2048
DEPTH = 1
DEC_BATCH = 8
DEC_SEQ = 8
PAST_LEN = 16384
PAGE_SIZE = 128

HEAD_DIM = 128
N_HEADS = D_MODEL // HEAD_DIM
N_KV_HEADS = 4
GROUP = N_HEADS // N_KV_HEADS
KV_W = N_KV_HEADS * HEAD_DIM
CMP_LEN = 32
CMP_STRIDE = 16
CMP_SUB = CMP_LEN // CMP_STRIDE
CMP_HID = HEAD_DIM
SLC_LEN = 64
SLC_TOP = 16
N_LOCAL_BLOCKS = 2
WINDOW = 512
WIN_BLK = 128
SLC_Q_CHUNK = 64
CONV_DIM = D_MODEL
CONV_W = 3
D_FF = 4 * D_MODEL
N_KV_KINDS = 4
RMS_EPS = 1e-6
NEG = -1e30
SCALE = HEAD_DIM ** -0.5
W_IN_SPLITS = (N_HEADS * HEAD_DIM,) + (KV_W,) * 6 + (3 * N_HEADS,) + (CONV_DIM,) * 3 + (D_MODEL,) * 2
W_IN_COLS = sum(W_IN_SPLITS)

kernel_name = "nsa_shortconv_gated_hybrid_step"


def rmsnorm(x, g):
    xf = x.astype(jnp.float32)
    y = xf * lax.rsqrt(jnp.mean(xf * xf, axis=-1, keepdims=True) + RMS_EPS)
    return (y * g.astype(jnp.float32)).astype(x.dtype)


def compress(k, w1, w2, pe):
    B, T = k.shape[:2]
    n_chunk = T // CMP_STRIDE
    chunks = k[:, :n_chunk * CMP_STRIDE].reshape(B, n_chunk, CMP_STRIDE, N_KV_HEADS, HEAD_DIM)
    w1s = w1.reshape(CMP_SUB, CMP_STRIDE, HEAD_DIM, CMP_HID)
    part = jnp.einsum('bcrkd,srdf->bcskf', chunks, w1s)
    nc = n_chunk - CMP_SUB + 1
    h = part[:, 0:nc, 0]
    for s in range(1, CMP_SUB):
        h = h + part[:, s:s + nc, s]
    h = h + jnp.einsum('srd,srdf->f', pe.reshape(CMP_SUB, CMP_STRIDE, HEAD_DIM), w1s)
    return jax.nn.gelu(h) @ w2


def cmp_attend(q, kc, vc, q_pos):
    nc = kc.shape[1]
    blk_end = jnp.arange(nc) * CMP_STRIDE + CMP_LEN - 1
    mask = blk_end[None, :] <= q_pos[:, None]
    s = jnp.einsum('bskgd,bckd->bkgsc', q, kc).astype(jnp.float32) * SCALE
    s = jnp.where(mask, s, NEG)
    p = jax.nn.softmax(s, axis=-1) * mask
    o = jnp.einsum('bkgsc,bckd->bskgd', p.astype(vc.dtype), vc)
    return o, p


def select_blocks(p_cmp, q_pos, n_blocks):
    pg = jnp.sum(p_cmp, axis=2)
    nc = pg.shape[-1]
    cs = jnp.arange(nc) * CMP_STRIDE
    sb = jnp.arange(n_blocks) * SLC_LEN
    ov = jnp.clip(jnp.minimum(cs[:, None] + CMP_LEN, sb[None, :] + SLC_LEN)
                  - jnp.maximum(cs[:, None], sb[None, :]), 0, None)
    m = (ov / CMP_STRIDE).astype(jnp.float32)
    score = jnp.einsum('bksc,cn->bksn', pg, m)
    cur = q_pos // SLC_LEN
    j = jnp.arange(n_blocks)
    valid = j[None, :] <= cur[:, None]
    forced = (j[None, :] == 0) | (j[None, :] > cur[:, None] - N_LOCAL_BLOCKS)
    score = jnp.where(valid, jnp.where(forced, jnp.inf, score), -jnp.inf)
    n_top = min(SLC_TOP, n_blocks)
    vals, idx = lax.top_k(score, n_top)
    return idx, vals > -jnp.inf


def slc_attend(q, kb, vb, idx, valid, q_pos):
    gather = jax.vmap(jax.vmap(lambda blocks, ix: blocks[ix]))
    kg = gather(kb, idx)
    vg = gather(vb, idx)
    kpos = idx[..., None] * SLC_LEN + jnp.arange(SLC_LEN)
    mask = valid[..., None] & (kpos <= q_pos[None, None, :, None, None])
    s = jnp.einsum('bskgd,bksnld->bksgnl', q, kg).astype(jnp.float32) * SCALE
    s = jnp.where(mask[:, :, :, None], s, NEG)
    p = jax.nn.softmax(s, axis=(-2, -1))
    return jnp.einsum('bksgnl,bksnld->bskgd', p.astype(vg.dtype), vg)


def nsa_global(q, k_cmp, v_cmp, k_slc, v_slc, q_pos, cw, chunked):
    kc = compress(k_cmp, cw[0], cw[1], cw[2])
    vc = compress(v_cmp, cw[3], cw[4], cw[5])
    o_cmp, p_cmp = cmp_attend(q, kc, vc, q_pos)
    B, T = k_slc.shape[:2]
    ns = -(-T // SLC_LEN)
    idx, valid = select_blocks(p_cmp, q_pos, ns)
    padw = ((0, 0), (0, ns * SLC_LEN - T), (0, 0), (0, 0))
    kb = jnp.pad(k_slc, padw).reshape(B, ns, SLC_LEN, N_KV_HEADS, HEAD_DIM).transpose(0, 3, 1, 2, 4)
    vb = jnp.pad(v_slc, padw).reshape(B, ns, SLC_LEN, N_KV_HEADS, HEAD_DIM).transpose(0, 3, 1, 2, 4)
    if chunked:
        S = q.shape[1]
        qc = min(SLC_Q_CHUNK, S)
        nq = S // qc
        n_top = idx.shape[-1]
        qs = q.reshape(B, nq, qc, N_KV_HEADS, GROUP, HEAD_DIM).swapaxes(0, 1)
        ids = idx.reshape(B, N_KV_HEADS, nq, qc, n_top).transpose(2, 0, 1, 3, 4)
        vls = valid.reshape(B, N_KV_HEADS, nq, qc, n_top).transpose(2, 0, 1, 3, 4)
        ps = q_pos.reshape(nq, qc)
        o = lax.map(lambda a: slc_attend(a[0], kb, vb, a[1], a[2], a[3]), (qs, ids, vls, ps))
        o_slc = o.swapaxes(0, 1).reshape(B, S, N_KV_HEADS, GROUP, HEAD_DIM)
    else:
        o_slc = slc_attend(q, kb, vb, idx, valid, q_pos)
    return o_cmp, o_slc


def window_prompt(q, k, v):
    B, S = k.shape[:2]
    blk = min(WIN_BLK, S)
    nb = S // blk
    nprev = -(-WINDOW // blk)
    span = (nprev + 1) * blk
    padw = ((0, 0), (nprev * blk, 0), (0, 0), (0, 0))
    kp = jnp.pad(k, padw).reshape(B, nb + nprev, blk, N_KV_HEADS, HEAD_DIM)
    vp = jnp.pad(v, padw).reshape(B, nb + nprev, blk, N_KV_HEADS, HEAD_DIM)
    kband = jnp.concatenate([kp[:, j:j + nb] for j in range(nprev + 1)], axis=2)
    vband = jnp.concatenate([vp[:, j:j + nb] for j in range(nprev + 1)], axis=2)
    qb = q.reshape(B, nb, blk, N_KV_HEADS, GROUP, HEAD_DIM)
    qpos = jnp.arange(S).reshape(nb, blk)
    kpos = (jnp.arange(nb) * blk - nprev * blk)[:, None] + jnp.arange(span)[None, :]
    diff = qpos[:, :, None] - kpos[:, None, :]
    mask = (kpos[:, None, :] >= 0) & (diff >= 0) & (diff < WINDOW)
    s = jnp.einsum('bnqkgd,bnckd->bnkgqc', qb, kband).astype(jnp.float32) * SCALE
    s = jnp.where(mask[None, :, None, None], s, NEG)
    p = jax.nn.softmax(s, axis=-1)
    o = jnp.einsum('bnkgqc,bnckd->bnqkgd', p.astype(vband.dtype), vband)
    return o.reshape(B, S, N_KV_HEADS, GROUP, HEAD_DIM)


def local_attend(q, k, v, mask):
    s = jnp.einsum('bskgd,btkd->bkgst', q, k).astype(jnp.float32) * SCALE
    s = jnp.where(mask, s, NEG)
    p = jax.nn.softmax(s, axis=-1)
    return jnp.einsum('bkgst,btkd->bskgd', p.astype(v.dtype), v)


def nsa_prompt(q, kv, cw):
    k_cmp, v_cmp, k_slc, v_slc, k_win, v_win = kv
    S = q.shape[1]
    q_pos = jnp.arange(S)
    o_cmp, o_slc = nsa_global(q, k_cmp, v_cmp, k_slc, v_slc, q_pos, cw, True)
    o_win = window_prompt(q, k_win, v_win)
    kv_rows = jnp.stack([k_cmp, v_cmp, k_slc, v_slc], axis=2)
    wb = min(WINDOW, S)
    win_state = jnp.stack([k_win, v_win], axis=2)[:, S - wb:]
    return o_cmp, o_slc, o_win, kv_rows, win_state


def nsa_sample(q, kv, cw, cache, page_table, win):
    k_cmp, v_cmp, k_slc, v_slc, k_win, v_win = kv
    DB, DS = q.shape[:2]
    past_len = page_table.shape[1] * cache.shape[1]
    kv_rows = jnp.stack([k_cmp, v_cmp, k_slc, v_slc], axis=2)
    past = cache[page_table].reshape(DB, past_len, N_KV_KINDS, N_KV_HEADS, HEAD_DIM)
    full = jnp.concatenate([past, kv_rows.astype(past.dtype)], axis=1)
    q_pos = past_len + jnp.arange(DS)
    o_cmp, o_slc = nsa_global(q, full[:, :, 0], full[:, :, 1], full[:, :, 2], full[:, :, 3], q_pos, cw, False)
    wb = win.shape[1]
    wk = jnp.concatenate([win, jnp.stack([k_win, v_win], axis=2).astype(win.dtype)], axis=1)
    kpos = past_len - wb + jnp.arange(wb + DS)
    diff = q_pos[:, None] - kpos[None, :]
    o_win = local_attend(q, wk[:, :, 0], wk[:, :, 1], (diff >= 0) & (diff < WINDOW))
    win_state = wk[:, DS:]
    return o_cmp, o_slc, o_win, kv_rows, win_state


def short_conv(u, prev, w, b):
    S = u.shape[1]
    up = jnp.concatenate([prev.astype(u.dtype), u], axis=1)
    y = b
    for j in range(CONV_W):
        y = y + up[:, j:j + S] * w[j]
    return y, up[:, S:]


def layer_forward(x, c, lw, cw, nsa_fn, conv_prev):
    B, S = x.shape[:2]
    ada = jax.nn.silu(c) @ lw['w_ada'] + lw['b_ada']
    sh1, sc1, g1, sh2, sc2, g2 = jnp.split(ada[:, None, :], 6, axis=-1)
    h = rmsnorm(x, lw['norm1_g']) * (1 + sc1) + sh1
    parts = jnp.split(h @ lw['w_in'], list(np.cumsum(W_IN_SPLITS)[:-1]), axis=-1)
    q, k_cmp, v_cmp, k_slc, v_slc, k_win, v_win, g_nsa, u_in, b_gate, c_gate, m_attn, m_conv = parts
    q = q.reshape(B, S, N_KV_HEADS, GROUP, HEAD_DIM)
    kv = tuple(t.reshape(B, S, N_KV_HEADS, HEAD_DIM) for t in (k_cmp, v_cmp, k_slc, v_slc, k_win, v_win))
    o_cmp, o_slc, o_win, kv_rows, win_state = nsa_fn(q, kv, cw)
    g = jax.nn.sigmoid(g_nsa).reshape(B, S, 3, N_KV_HEADS, GROUP, 1)
    o_attn = (g[:, :, 0] * o_cmp + g[:, :, 1] * o_slc + g[:, :, 2] * o_win).reshape(B, S, N_HEADS * HEAD_DIM)
    conv_y, conv_state = short_conv(c_gate * u_in, conv_prev, lw['conv_w'], lw['conv_b'])
    z = b_gate * conv_y
    mixed = (jax.nn.sigmoid(m_attn) * (o_attn @ lw['w_attn_proj'])
             + jax.nn.sigmoid(m_conv) * (z @ lw['w_conv_proj']))
    x = x + g1 * (mixed @ lw['w_out'])
    h2 = rmsnorm(x, lw['norm2_g']) * (1 + sc2) + sh2
    x = x + g2 * (jnp.square(jax.nn.relu(h2 @ lw['w_mlp1'])) @ lw['w_mlp2'])
    return x, kv_rows, win_state, conv_state


def setup_inputs(seed: int = 0) -> dict:
    key = jax.random.key(seed)
    ks = jax.random.split(key, 32)
    f32 = jnp.float32
    n_pages = PAST_LEN // PAGE_SIZE
    n_used = DEC_BATCH * n_pages
    n_pool = n_used + max(1, n_used // 4)
    win_buf = min(WINDOW, PAST_LEN)

    def nrm(k, shape, scale):
        return jax.random.normal(k, shape, f32) * scale

    page_table = jax.random.permutation(ks[0], n_pool)[:n_used].reshape(DEC_BATCH, n_pages).astype(jnp.int32)
    return {
        'x_prompt': nrm(ks[1], (BATCH, SEQ, D_MODEL), 1.0),
        'x_sample': nrm(ks[2], (DEC_BATCH, DEC_SEQ, D_MODEL), 1.0),
        'c_prompt': nrm(ks[3], (BATCH, D_MODEL), 1.0),
        'c_sample': nrm(ks[4], (DEC_BATCH, D_MODEL), 1.0),
        'cache_nsa_kv': nrm(ks[5], (DEPTH, n_pool, PAGE_SIZE, N_KV_KINDS, N_KV_HEADS, HEAD_DIM), 1.0),
        'page_table': page_table,
        'state_win_kv': nrm(ks[6], (DEPTH, DEC_BATCH, win_buf, 2, N_KV_HEADS, HEAD_DIM), 1.0),
        'state_conv': nrm(ks[7], (DEPTH, DEC_BATCH, CONV_W - 1, CONV_DIM), 1.0),
        'w_ada': nrm(ks[8], (DEPTH, D_MODEL, 6 * D_MODEL), 0.5 * D_MODEL ** -0.5),
        'b_ada': nrm(ks[9], (DEPTH, 6 * D_MODEL), 0.02),
        'norm1_g': 1.0 + nrm(ks[10], (DEPTH, D_MODEL), 0.05),
        'norm2_g': 1.0 + nrm(ks[11], (DEPTH, D_MODEL), 0.05),
        'w_in': nrm(ks[12], (DEPTH, D_MODEL, W_IN_COLS), D_MODEL ** -0.5),
        'w_cmp_k1': nrm(ks[13], (DEPTH, CMP_LEN * HEAD_DIM, CMP_HID), (CMP_LEN * HEAD_DIM) ** -0.5),
        'w_cmp_k2': nrm(ks[14], (DEPTH, CMP_HID, HEAD_DIM), 1.5 * CMP_HID ** -0.5),
        'pe_cmp_k': nrm(ks[15], (DEPTH, CMP_LEN, HEAD_DIM), 0.1),
        'w_cmp_v1': nrm(ks[16], (DEPTH, CMP_LEN * HEAD_DIM, CMP_HID), (CMP_LEN * HEAD_DIM) ** -0.5),
        'w_cmp_v2': nrm(ks[17], (DEPTH, CMP_HID, HEAD_DIM), 1.5 * CMP_HID ** -0.5),
        'pe_cmp_v': nrm(ks[18], (DEPTH, CMP_LEN, HEAD_DIM), 0.1),
        'conv_w': nrm(ks[19], (DEPTH, CONV_W, CONV_DIM), CONV_W ** -0.5),
        'conv_b': nrm(ks[20], (DEPTH, CONV_DIM), 0.02),
        'w_attn_proj': nrm(ks[21], (DEPTH, N_HEADS * HEAD_DIM, D_MODEL), (N_HEADS * HEAD_DIM) ** -0.5),
        'w_conv_proj': nrm(ks[22], (DEPTH, CONV_DIM, D_MODEL), CONV_DIM ** -0.5),
        'w_out': nrm(ks[23], (DEPTH, D_MODEL, D_MODEL), D_MODEL ** -0.5),
        'w_mlp1': nrm(ks[24], (DEPTH, D_MODEL, D_FF), D_MODEL ** -0.5),
        'w_mlp2': nrm(ks[25], (DEPTH, D_FF, D_MODEL), D_FF ** -0.5),
        'normf_g': 1.0 + nrm(ks[26], (D_MODEL,), 0.05),
    }


def reference(x_prompt, x_sample, c_prompt, c_sample, cache_nsa_kv, page_table, state_win_kv, state_conv,
              w_ada, b_ada, norm1_g, norm2_g, w_in, w_cmp_k1, w_cmp_k2, pe_cmp_k, w_cmp_v1, w_cmp_v2, pe_cmp_v,
              conv_w, conv_b, w_attn_proj, w_conv_proj, w_out, w_mlp1, w_mlp2, normf_g):
    yp, ys = x_prompt, x_sample
    kv_p, kv_s, win_p, win_s, conv_p, conv_s = [], [], [], [], [], []
    for l in range(DEPTH):
        lw = {'w_ada': w_ada[l], 'b_ada': b_ada[l], 'norm1_g': norm1_g[l], 'norm2_g': norm2_g[l],
              'w_in': w_in[l], 'conv_w': conv_w[l], 'conv_b': conv_b[l], 'w_attn_proj': w_attn_proj[l],
              'w_conv_proj': w_conv_proj[l], 'w_out': w_out[l], 'w_mlp1': w_mlp1[l], 'w_mlp2': w_mlp2[l]}
        cw = (w_cmp_k1[l], w_cmp_k2[l], pe_cmp_k[l], w_cmp_v1[l], w_cmp_v2[l], pe_cmp_v[l])
        conv0 = jnp.zeros((yp.shape[0], CONV_W - 1, CONV_DIM), yp.dtype)
        yp, a, b, c = layer_forward(yp, c_prompt, lw, cw, nsa_prompt, conv0)
        kv_p.append(a); win_p.append(b); conv_p.append(c)
        sample_fn = functools.partial(nsa_sample, cache=cache_nsa_kv[l], page_table=page_table, win=state_win_kv[l])
        ys, a, b, c = layer_forward(ys, c_sample, lw, cw, sample_fn, state_conv[l])
        kv_s.append(a); win_s.append(b); conv_s.append(c)
    y_prompt = rmsnorm(yp, normf_g)
    y_sample = rmsnorm(ys, normf_g)
    return (y_prompt, y_sample, jnp.stack(kv_p), jnp.stack(kv_s), jnp.stack(win_p), jnp.stack(win_s),
            jnp.stack(conv_p), jnp.stack(conv_s))
```

```python
import functools

import numpy as np
import jax
import jax.numpy as jnp
from jax import lax
from jax.experimental import pallas as pl
from jax.experimental.pallas import tpu as pltpu

F32 = jnp.float32
BF16 = jnp.bfloat16

D_MODEL = 2048
HEAD_DIM = 128
N_HEADS = 16
N_KV = 4
GROUP = 4
KV_W = N_KV * HEAD_DIM
CMP_LEN = 32
CMP_STRIDE = 16
SLC_LEN = 64
SLC_TOP = 16
N_LOCAL = 2
WINDOW = 512
CONV_W = 3
RMS_EPS = 1e-6
NEG = -1e30
SCALE = HEAD_DIM ** -0.5
PAGE = 128
PAGES_PER_STEP = 16
CHUNK_ROWS = PAGES_PER_STEP * PAGE
BLK_PER_CHUNK = CHUNK_ROWS // SLC_LEN
VMEM_LIMIT = 56 * 1024 * 1024

_NT = (((1,), (1,)), ((), ()))
_TN = (((0,), (0,)), ((), ()))


def _cparams(sem):
    return pltpu.CompilerParams(dimension_semantics=sem, vmem_limit_bytes=VMEM_LIMIT)


def _ada_kernel(c_ref, w_ref, b_ref, o_ref):
    c = c_ref[...]
    a = (c * jax.nn.sigmoid(c)).astype(BF16)
    o_ref[...] = jnp.dot(a, w_ref[...].astype(BF16), preferred_element_type=F32) + b_ref[...]


def _ada(c, w, b, tn=1024):
    m, k = c.shape
    n = w.shape[1]
    return pl.pallas_call(
        _ada_kernel,
        grid=(n // tn,),
        in_specs=[pl.BlockSpec((m, k), lambda j: (0, 0)),
                  pl.BlockSpec((k, tn), lambda j: (0, j)),
                  pl.BlockSpec((1, tn), lambda j: (0, j))],
        out_specs=pl.BlockSpec((m, tn), lambda j: (0, j)),
        out_shape=jax.ShapeDtypeStruct((m, n), F32),
        compiler_params=_cparams(("arbitrary",)),
        name="ada",
    )(c, w, b)


def _rowspec(p, tm, tn, seq, col=True):
    gr = p.shape[1]
    if gr == 1:
        if col:
            return pl.BlockSpec((1, 1, tn), lambda i, j: ((i * tm) // seq, 0, j))
        return pl.BlockSpec((1, 1, tn), lambda i: ((i * tm) // seq, 0, 0))
    if col:
        return pl.BlockSpec((1, tm, tn), lambda i, j: (0, i, j))
    return pl.BlockSpec((1, tm, tn), lambda i: (0, i, 0))


def _norm_kernel(x_ref, g_ref, sc_ref, sh_ref, o_ref):
    x = x_ref[...]
    r = lax.rsqrt(jnp.mean(x * x, axis=-1, keepdims=True) + RMS_EPS)
    y = (x * r) * g_ref[...]
    o_ref[...] = (y * (1.0 + sc_ref[0]) + sh_ref[0]).astype(o_ref.dtype)


def _norm_mod(x, g, sc, sh, *, tm, seq):
    t, d = x.shape
    return pl.pallas_call(
        _norm_kernel,
        grid=(t // tm,),
        in_specs=[pl.BlockSpec((tm, d), lambda i: (i, 0)),
                  pl.BlockSpec((1, d), lambda i: (0, 0)),
                  _rowspec(sc, tm, d, seq, col=False),
                  _rowspec(sh, tm, d, seq, col=False)],
        out_specs=pl.BlockSpec((tm, d), lambda i: (i, 0)),
        out_shape=jax.ShapeDtypeStruct((t, d), BF16),
        compiler_params=_cparams(("arbitrary",)),
        name="norm_mod",
    )(x, g, sc, sh)


def _mm_kernel(x_ref, w_ref, o_ref, *, act):
    acc = jnp.dot(x_ref[...].astype(BF16), w_ref[...], preferred_element_type=F32)
    if act == "sigmoid":
        acc = jax.nn.sigmoid(acc)
    o_ref[...] = acc.astype(o_ref.dtype)


def _mm(x, w, *, tm, tn, out_dtype, act=None, name="mm"):
    t, k = x.shape
    n = w.shape[1]
    return pl.pallas_call(
        functools.partial(_mm_kernel, act=act),
        grid=(t // tm, n // tn),
        in_specs=[pl.BlockSpec((tm, k), lambda i, j: (i, 0)),
                  pl.BlockSpec((k, tn), lambda i, j: (0, j))],
        out_specs=pl.BlockSpec((tm, tn), lambda i, j: (i, j)),
        out_shape=jax.ShapeDtypeStruct((t, n), out_dtype),
        compiler_params=_cparams(("arbitrary", "arbitrary")),
        name=name,
    )(x, w)


def _conv_taps(cu, prev1, prev2, bgate, cw_ref, cb_ref):
    y = cb_ref[...] + prev2 * cw_ref[0:1, :]
    y = y + prev1 * cw_ref[1:2, :]
    y = y + cu * cw_ref[2:3, :]
    return bgate * y


def _mm_conv_kernel(x_ref, w_ref, cw_ref, cb_ref, z_ref, st_ref, carry_ref, *, tm, tc, seq):
    i = pl.program_id(0)
    j = pl.program_id(1)
    acc = jnp.dot(x_ref[...], w_ref[...], preferred_element_type=F32)
    u = acc[:, 0:tc]
    bgate = acc[:, tc:2 * tc]
    cu = acc[:, 2 * tc:3 * tc] * u
    @pl.when((i * tm) % seq == 0)
    def _():
        carry_ref[j] = jnp.zeros((8, tc), F32)

    car = carry_ref[j]
    p0 = car[0:1, :]
    p1 = car[1:2, :]
    rows = lax.broadcasted_iota(jnp.int32, (tm, tc), 0)
    r1 = pltpu.roll(cu, 1, 0)
    r2 = pltpu.roll(cu, 2, 0)
    prev1 = jnp.where(rows == 0, p1, r1)
    prev2 = jnp.where(rows == 0, p0, jnp.where(rows == 1, p1, r2))
    z_ref[...] = _conv_taps(cu, prev1, prev2, bgate, cw_ref, cb_ref).astype(z_ref.dtype)
    tail = cu[tm - 8:tm, :]
    carry_ref[j] = jnp.concatenate([tail[6:8, :], tail[0:6, :]], axis=0)
    st_ref[0] = tail[6:8, :]


def _mm_conv(h, w_il, conv_w, conv_b, *, tm, tc, seq):
    t, k = h.shape
    d = w_il.shape[1] // 3
    nj = d // tc
    nb = t // seq
    return pl.pallas_call(
        functools.partial(_mm_conv_kernel, tm=tm, tc=tc, seq=seq),
        grid=(t // tm, nj),
        in_specs=[pl.BlockSpec((tm, k), lambda i, j: (i, 0)),
                  pl.BlockSpec((k, 3 * tc), lambda i, j: (0, j)),
                  pl.BlockSpec((CONV_W, tc), lambda i, j: (0, j)),
                  pl.BlockSpec((1, tc), lambda i, j: (0, j))],
        out_specs=[pl.BlockSpec((tm, tc), lambda i, j: (i, j)),
                   pl.BlockSpec((1, CONV_W - 1, tc), lambda i, j: ((i * tm) // seq, 0, j))],
        out_shape=[jax.ShapeDtypeStruct((t, d), BF16),
                   jax.ShapeDtypeStruct((nb, CONV_W - 1, d), F32)],
        scratch_shapes=[pltpu.VMEM((nj, 8, tc), F32)],
        compiler_params=_cparams(("arbitrary", "arbitrary")),
        name="mm_conv",
    )(h, w_il, conv_w, conv_b)


def _sample_conv_kernel(a_ref, st0_ref, st1_ref, cw_ref, cb_ref, z_ref, cu_ref, *, tc, seq):
    acc = a_ref[...]
    t = acc.shape[0]
    u = acc[:, 0:tc]
    bgate = acc[:, tc:2 * tc]
    cu = acc[:, 2 * tc:3 * tc] * u
    s = lax.rem(lax.broadcasted_iota(jnp.int32, (t, tc), 0), jnp.int32(seq))
    r1 = pltpu.roll(cu, 1, 0)
    r2 = pltpu.roll(cu, 2, 0)
    prev1 = jnp.where(s == 0, st1_ref[...], r1)
    prev2 = jnp.where(s == 0, st0_ref[...], jnp.where(s == 1, st1_ref[...], r2))
    z_ref[...] = _conv_taps(cu, prev1, prev2, bgate, cw_ref, cb_ref)
    cu_ref[...] = cu


def _sample_conv(ubc, st0, st1, conv_w, conv_b, *, tc, seq):
    t = ubc.shape[0]
    d = ubc.shape[1] // 3
    return pl.pallas_call(
        functools.partial(_sample_conv_kernel, tc=tc, seq=seq),
        grid=(d // tc,),
        in_specs=[pl.BlockSpec((t, 3 * tc), lambda j: (0, j)),
                  pl.BlockSpec((t, tc), lambda j: (0, j)),
                  pl.BlockSpec((t, tc), lambda j: (0, j)),
                  pl.BlockSpec((CONV_W, tc), lambda j: (0, j)),
                  pl.BlockSpec((1, tc), lambda j: (0, j))],
        out_specs=[pl.BlockSpec((t, tc), lambda j: (0, j)),
                   pl.BlockSpec((t, tc), lambda j: (0, j))],
        out_shape=[jax.ShapeDtypeStruct((t, d), F32), jax.ShapeDtypeStruct((t, d), F32)],
        compiler_params=_cparams(("arbitrary",)),
        name="sample_conv",
    )(ubc, st0, st1, conv_w, conv_b)


def _mix_kernel(o_ref, z_ref, wa_ref, wc_ref, ga_ref, gc_ref, m_ref):
    a = jnp.dot(o_ref[...].astype(BF16), wa_ref[...], preferred_element_type=F32)
    c = jnp.dot(z_ref[...].astype(BF16), wc_ref[...], preferred_element_type=F32)
    m_ref[...] = (ga_ref[...].astype(F32) * a + gc_ref[...].astype(F32) * c).astype(m_ref.dtype)


def _mix(o, z, wa, wc, gates, *, tm, tn):
    t, k = o.shape
    n = wa.shape[1]
    nj = n // tn
    return pl.pallas_call(
        _mix_kernel,
        grid=(t // tm, nj),
        in_specs=[pl.BlockSpec((tm, k), lambda i, j: (i, 0)),
                  pl.BlockSpec((tm, k), lambda i, j: (i, 0)),
                  pl.BlockSpec((k, tn), lambda i, j: (0, j)),
                  pl.BlockSpec((k, tn), lambda i, j: (0, j)),
                  pl.BlockSpec((tm, tn), lambda i, j: (i, j)),
                  pl.BlockSpec((tm, tn), lambda i, j: (i, j + nj))],
        out_specs=pl.BlockSpec((tm, tn), lambda i, j: (i, j)),
        out_shape=jax.ShapeDtypeStruct((t, n), BF16),
        compiler_params=_cparams(("arbitrary", "arbitrary")),
        name="mix",
    )(o, z, wa, wc, gates, gates)


def _resid_kernel(a_ref, w_ref, x_ref, g_ref, o_ref):
    acc = jnp.dot(a_ref[...], w_ref[...], preferred_element_type=F32)
    o_ref[...] = x_ref[...] + g_ref[0] * acc


def _mm_resid(a, w, x, gate, *, tm, tn, seq):
    t, k = a.shape
    n = w.shape[1]
    return pl.pallas_call(
        _resid_kernel,
        grid=(t // tm, n // tn),
        in_specs=[pl.BlockSpec((tm, k), lambda i, j: (i, 0)),
                  pl.BlockSpec((k, tn), lambda i, j: (0, j)),
                  pl.BlockSpec((tm, tn), lambda i, j: (i, j)),
                  _rowspec(gate, tm, tn, seq)],
        out_specs=pl.BlockSpec((tm, tn), lambda i, j: (i, j)),
        out_shape=jax.ShapeDtypeStruct((t, n), F32),
        compiler_params=_cparams(("arbitrary", "arbitrary")),
        name="mm_resid",
    )(a, w, x, gate)


def _mlp_kernel(h_ref, w1_ref, w2_ref, x_ref, g_ref, nf_ref, o_ref, acc_ref):
    k = pl.program_id(1)

    @pl.when(k == 0)
    def _():
        acc_ref[...] = jnp.zeros_like(acc_ref)

    a = jnp.dot(h_ref[...], w1_ref[...], preferred_element_type=F32)
    a = jnp.square(jnp.maximum(a, 0.0)).astype(BF16)
    acc_ref[...] += jnp.dot(a, w2_ref[...], preferred_element_type=F32)

    @pl.when(k == pl.num_programs(1) - 1)
    def _():
        x2 = x_ref[...] + g_ref[0] * acc_ref[...]
        r = lax.rsqrt(jnp.mean(x2 * x2, axis=-1, keepdims=True) + RMS_EPS)
        o_ref[...] = (x2 * r) * nf_ref[...]


def _mlp_final(h2, w1, w2, x1, gate, normf, *, tm, tf, seq):
    t, d = h2.shape
    f = w1.shape[1]
    return pl.pallas_call(
        _mlp_kernel,
        grid=(t // tm, f // tf),
        in_specs=[pl.BlockSpec((tm, d), lambda i, k: (i, 0)),
                  pl.BlockSpec((d, tf), lambda i, k: (0, k)),
                  pl.BlockSpec((tf, d), lambda i, k: (k, 0)),
                  pl.BlockSpec((tm, d), lambda i, k: (i, 0)),
                  _rowspec_k(gate, tm, d, seq),
                  pl.BlockSpec((1, d), lambda i, k: (0, 0))],
        out_specs=pl.BlockSpec((tm, d), lambda i, k: (i, 0)),
        out_shape=jax.ShapeDtypeStruct((t, d), F32),
        scratch_shapes=[pltpu.VMEM((tm, d), F32)],
        compiler_params=_cparams(("arbitrary", "arbitrary")),
        name="mlp",
    )(h2, w1, w2, x1, gate, normf)


def _rowspec_k(p, tm, d, seq):
    if p.shape[1] == 1:
        return pl.BlockSpec((1, 1, d), lambda i, k: ((i * tm) // seq, 0, 0))
    return pl.BlockSpec((1, tm, d), lambda i, k: (0, i, 0))


def _cmp_part_kernel(pt_ref, *refs):
    pages = refs[:PAGES_PER_STEP]
    w_ref = refs[PAGES_PER_STEP]
    o_ref = refs[PAGES_PER_STEP + 1]
    stage = refs[PAGES_PER_STEP + 2]
    nchunk = CHUNK_ROWS // CMP_STRIDE
    for n, pg in enumerate(pages):
        for cb in range(2 * N_KV):
            stage[cb, n * PAGE:(n + 1) * PAGE, :] = pg[:, cb * HEAD_DIM:(cb + 1) * HEAD_DIM]
    for kind in range(2):
        acc = None
        for rp in range(CMP_STRIDE // 2):
            halves = []
            for r in (2 * rp, 2 * rp + 1):
                halves.append(jnp.concatenate(
                    [stage[kind * N_KV + h, pl.ds(r, nchunk, stride=CMP_STRIDE), :] for h in range(N_KV)],
                    axis=0))
            lhs = jnp.concatenate(halves, axis=1).astype(BF16)
            d = jnp.dot(lhs, w_ref[kind, rp], preferred_element_type=F32)
            acc = d if acc is None else acc + d
        for h in range(N_KV):
            o_ref[kind, h] = acc[h * nchunk:(h + 1) * nchunk, :]


def _cmp_part(pages3, page_table, w1pairs, col_block):
    nb, npg = page_table.shape
    steps = npg // PAGES_PER_STEP
    nchunk = PAGES_PER_STEP * (PAGE // CMP_STRIDE)

    def page_spec(n):
        return pl.BlockSpec((None, PAGE, 2 * KV_W),
                            lambda b, p, pt, n=n: (pt[b, p * PAGES_PER_STEP + n], 0, col_block))

    grid_spec = pltpu.PrefetchScalarGridSpec(
        num_scalar_prefetch=1,
        grid=(nb, steps),
        in_specs=[page_spec(n) for n in range(PAGES_PER_STEP)]
        + [pl.BlockSpec((2, CMP_STRIDE // 2, 2 * HEAD_DIM, 2 * HEAD_DIM), lambda b, p, pt: (0, 0, 0, 0))],
        out_specs=pl.BlockSpec((None, 2, N_KV, nchunk, 2 * HEAD_DIM), lambda b, p, pt: (b, 0, 0, p, 0)),
        scratch_shapes=[pltpu.VMEM((2 * N_KV, CHUNK_ROWS, HEAD_DIM), F32)],
    )
    return pl.pallas_call(
        _cmp_part_kernel,
        grid_spec=grid_spec,
        out_shape=jax.ShapeDtypeStruct((nb, 2, N_KV, steps * nchunk, 2 * HEAD_DIM), F32),
        compiler_params=_cparams(("arbitrary", "arbitrary")),
        name="cmp_part",
    )(page_table, *([pages3] * PAGES_PER_STEP), w1pairs)


def _gelu_tanh(x):
    c = np.sqrt(2.0 / np.pi).astype(np.float32)
    return 0.5 * x * (1.0 + jnp.tanh(c * (x + 0.044715 * (x * x * x))))


def _cmp_finish_kernel(part_ref, pe_ref, w1_ref, w2_ref, o_ref):
    part = part_ref[...]
    n = part.shape[0]
    bias = jnp.dot(pe_ref[...], w1_ref[...], preferred_element_type=F32,
                   precision=lax.Precision.HIGHEST)[0:1, :]
    h = part[:, 0:HEAD_DIM] + pltpu.roll(part[:, HEAD_DIM:], n - 1, 0)
    h = h + bias
    o_ref[...] = jnp.dot(_gelu_tanh(h).astype(BF16), w2_ref[...].astype(BF16), preferred_element_type=F32)


def _cmp_finish(part, pe8, w1, w2):
    nb, _, _, nchunk, _ = part.shape
    return pl.pallas_call(
        _cmp_finish_kernel,
        grid=(nb, 2, N_KV),
        in_specs=[pl.BlockSpec((None, None, None, nchunk, 2 * HEAD_DIM), lambda b, k, h: (b, k, h, 0, 0)),
                  pl.BlockSpec((None, 8, CMP_LEN * HEAD_DIM), lambda b, k, h: (k, 0, 0)),
                  pl.BlockSpec((None, CMP_LEN * HEAD_DIM, HEAD_DIM), lambda b, k, h: (k, 0, 0)),
                  pl.BlockSpec((None, HEAD_DIM, HEAD_DIM), lambda b, k, h: (k, 0, 0))],
        out_specs=pl.BlockSpec((None, None, None, nchunk, HEAD_DIM), lambda b, k, h: (b, k, h, 0, 0)),
        out_shape=jax.ShapeDtypeStruct((nb, 2, N_KV, nchunk, HEAD_DIM), F32),
        compiler_params=_cparams(("arbitrary", "arbitrary", "arbitrary")),
        name="cmp_finish",
    )(part, pe8, w1, w2)


def _select_topk(score, valid, forced, blk, axis):
    sc = jnp.where(valid, jnp.where(forced, jnp.inf, score), -jnp.inf)
    sel = jnp.zeros(score.shape, F32)
    big = jnp.int32(1 << 20)
    for _ in range(SLC_TOP):
        mx = jnp.max(sc, axis=axis, keepdims=True)
        idx = jnp.min(jnp.where(sc == mx, blk, big), axis=axis, keepdims=True)
        hit = blk == idx
        sel = jnp.where(hit & (mx > -jnp.inf), 1.0, sel)
        sc = jnp.where(hit, -jnp.inf, sc)
    return sel


def _overlap_matrix(nc_pad, n_blocks_pad, nc, n_blocks):
    cs = np.arange(nc_pad) * CMP_STRIDE
    sb = np.arange(n_blocks_pad) * SLC_LEN
    ov = np.clip(np.minimum(cs[:, None] + CMP_LEN, sb[None, :] + SLC_LEN)
                 - np.maximum(cs[:, None], sb[None, :]), 0, None)
    m = (ov / CMP_STRIDE).astype(np.float32)
    m[nc:, :] = 0.0
    m[:, n_blocks:] = 0.0
    return m


def _attn_prompt_kernel(q_ref, kc_ref, vc_ref, ks_ref, vs_ref, kw_ref, vw_ref, g_ref, mt_ref, e_ref,
                        o_ref, kcb, vcb, ksb, vsb, kwb, vwb, *, tq, seq, ck):
    qi = pl.program_id(2)
    nc = seq // CMP_STRIDE - 1
    nsb = seq // SLC_LEN

    @pl.when(qi == 0)
    def _():
        kcb[...] = kc_ref[...].astype(BF16)
        vcb[...] = vc_ref[...].astype(BF16)
        ksb[...] = ks_ref[...].astype(BF16)
        vsb[...] = vs_ref[...].astype(BF16)
        kwb[...] = kw_ref[...].astype(BF16)
        vwb[...] = vw_ref[...].astype(BF16)

    q = q_ref[...]
    qq = jnp.concatenate([q[:, g * HEAD_DIM:(g + 1) * HEAD_DIM] for g in range(GROUP)], axis=0)
    rq = GROUP * tq
    q0 = qi * tq

    ncp = kcb.shape[0]
    s = lax.dot_general(qq, kcb[...], _NT, preferred_element_type=F32) * SCALE
    qpos = q0 + (lax.broadcasted_iota(jnp.int32, (rq, ncp), 0) & (tq - 1))
    col = lax.broadcasted_iota(jnp.int32, (rq, ncp), 1)
    vis = (col * CMP_STRIDE + (CMP_LEN - 1) <= qpos) & (col < nc)
    s = jnp.where(vis, s, NEG)
    mx = jnp.max(s, axis=1, keepdims=True)
    e = jnp.where(vis, jnp.exp(s - mx), 0.0)
    den = jnp.sum(e, axis=1, keepdims=True)
    p = e / jnp.where(den > 0.0, den, 1.0)
    o_cmp = jnp.dot(p.astype(BF16), vcb[...], preferred_element_type=F32)
    pg = p[0:tq]
    for g in range(1, GROUP):
        pg = pg + p[g * tq:(g + 1) * tq]

    sct = lax.dot_general(mt_ref[...], pg, _NT, preferred_element_type=F32,
                          precision=lax.Precision.HIGHEST)
    blk = lax.broadcasted_iota(jnp.int32, (nsb, tq), 0)
    cur = lax.div(q0 + lax.broadcasted_iota(jnp.int32, (nsb, tq), 1), jnp.int32(SLC_LEN))
    valid = blk <= cur
    forced = (blk == 0) | (blk > cur - N_LOCAL)
    need_topk = (q0 + tq - 1) // SLC_LEN + 1 > SLC_TOP
    sel_t = lax.cond(need_topk,
                     lambda: _select_topk(sct, valid, forced, blk, 0),
                     lambda: valid.astype(F32)).astype(BF16)

    def slc_body(c, carry):
        m_i, l_i, acc = carry
        k0 = pl.multiple_of(c * ck, ck)
        kch = ksb[pl.ds(k0, ck), :]
        vch = vsb[pl.ds(k0, ck), :]
        sc_ = lax.dot_general(qq, kch, _NT, preferred_element_type=F32) * SCALE
        selq = lax.dot_general(sel_t, e_ref[c], _TN, preferred_element_type=F32)
        kpos = k0 + lax.broadcasted_iota(jnp.int32, (tq, ck), 1)
        qp = q0 + lax.broadcasted_iota(jnp.int32, (tq, ck), 0)
        okf = jnp.where((selq > 0.5) & (kpos <= qp), 1.0, 0.0)
        ok = jnp.concatenate([okf] * GROUP, axis=0) > 0.5
        sc_ = jnp.where(ok, sc_, NEG)
        m_new = jnp.maximum(m_i, jnp.max(sc_, axis=1, keepdims=True))
        alpha = jnp.exp(m_i - m_new)
        pp = jnp.exp(sc_ - m_new)
        l_new = alpha * l_i + jnp.sum(pp, axis=1, keepdims=True)
        acc = alpha * acc + jnp.dot(pp.astype(BF16), vch, preferred_element_type=F32)
        return m_new, l_new, acc

    n_ck = (q0 + tq + ck - 1) // ck
    m_f, l_f, acc_f = lax.fori_loop(
        0, n_ck, slc_body,
        (jnp.full((rq, 1), NEG, F32), jnp.zeros((rq, 1), F32), jnp.zeros((rq, HEAD_DIM), F32)))
    o_slc = acc_f / l_f

    band = WINDOW + tq
    ks0 = pl.multiple_of(jnp.maximum(q0 - WINDOW, 0), tq)
    kw = kwb[pl.ds(ks0, band), :]
    vw = vwb[pl.ds(ks0, band), :]
    sw = lax.dot_general(qq, kw, _NT, preferred_element_type=F32) * SCALE
    qpw = q0 + (lax.broadcasted_iota(jnp.int32, (rq, band), 0) & (tq - 1))
    dlt = qpw - (ks0 + lax.broadcasted_iota(jnp.int32, (rq, band), 1))
    sw = jnp.where((dlt >= 0) & (dlt < WINDOW), sw, NEG)
    ew = jnp.exp(sw - jnp.max(sw, axis=1, keepdims=True))
    o_win = jnp.dot(ew.astype(BF16), vw, preferred_element_type=F32) / jnp.sum(ew, axis=1, keepdims=True)

    gt = jax.nn.sigmoid(g_ref[...])
    for g in range(GROUP):
        r0, r1 = g * tq, (g + 1) * tq
        og = gt[:, g:g + 1] * o_cmp[r0:r1] + gt[:, GROUP + g:GROUP + g + 1] * o_slc[r0:r1]
        og = og + gt[:, 2 * GROUP + g:2 * GROUP + g + 1] * o_win[r0:r1]
        o_ref[:, g * HEAD_DIM:(g + 1) * HEAD_DIM] = og.astype(o_ref.dtype)


def _attn_prompt(q, kcv, kv, win, g_re, *, nb, seq, tq=128, ck=512):
    t = q.shape[0]
    ncp = seq // CMP_STRIDE
    nsb = seq // SLC_LEN
    nq = seq // tq
    mt = jnp.asarray(_overlap_matrix(ncp, nsb, ncp - 1, nsb).T)
    kk = np.arange(seq)
    e3 = (kk[None, :] // SLC_LEN == np.arange(nsb)[:, None]).astype(np.float32)
    e3 = jnp.asarray(e3.reshape(nsb, seq // ck, ck).transpose(1, 0, 2), dtype=BF16)
    slab = lambda cb: pl.BlockSpec((seq, HEAD_DIM), lambda b, h, i, cb=cb: (b, cb + h))
    return pl.pallas_call(
        functools.partial(_attn_prompt_kernel, tq=tq, seq=seq, ck=ck),
        grid=(nb, N_KV, nq),
        in_specs=[pl.BlockSpec((tq, GROUP * HEAD_DIM), lambda b, h, i: (b * nq + i, h)),
                  pl.BlockSpec((None, None, None, ncp, HEAD_DIM), lambda b, h, i: (b, 0, h, 0, 0)),
                  pl.BlockSpec((None, None, None, ncp, HEAD_DIM), lambda b, h, i: (b, 1, h, 0, 0)),
                  slab(2 * N_KV), slab(3 * N_KV), slab(0), slab(N_KV),
                  pl.BlockSpec((None, None, tq, 4 * GROUP), lambda b, h, i: (b, h, i, 0)),
                  pl.BlockSpec((nsb, ncp), lambda b, h, i: (0, 0)),
                  pl.BlockSpec((seq // ck, nsb, ck), lambda b, h, i: (0, 0, 0))],
        out_specs=pl.BlockSpec((tq, GROUP * HEAD_DIM), lambda b, h, i: (b * nq + i, h)),
        out_shape=jax.ShapeDtypeStruct((t, N_HEADS * HEAD_DIM), BF16),
        scratch_shapes=[pltpu.VMEM((ncp, HEAD_DIM), BF16), pltpu.VMEM((ncp, HEAD_DIM), BF16),
                        pltpu.VMEM((seq, HEAD_DIM), BF16), pltpu.VMEM((seq, HEAD_DIM), BF16),
                        pltpu.VMEM((seq, HEAD_DIM), BF16), pltpu.VMEM((seq, HEAD_DIM), BF16)],
        compiler_params=_cparams(("arbitrary", "arbitrary", "arbitrary")),
        name="attn_prompt",
    )(q, kcv, kcv, kv, kv, win, win, g_re, mt, e3)


def _attn_sample_kernel(pt_ref, *refs, past, ds):
    pages = refs[:PAGES_PER_STEP]
    (q_ref, kcv_ref, kvn_ref, wn_ref, st_ref, g_ref, ms_ref, e_ref, o_ref,
     m_scr, l_scr, acc_scr, sel_scr, ocmp_scr) = refs[PAGES_PER_STEP:]
    p = pl.program_id(1)
    n_steps = pl.num_programs(1)
    rq = GROUP * ds
    ncp = kcv_ref.shape[2]
    nc = ncp - 1
    n_sel_chunks = sel_scr.shape[1]
    lanes = n_sel_chunks * HEAD_DIM

    def q_rows(h):
        c0 = h * GROUP * HEAD_DIM
        return jnp.concatenate(
            [q_ref[:, c0 + g * HEAD_DIM:c0 + (g + 1) * HEAD_DIM] for g in range(GROUP)], axis=0).astype(BF16)

    def online_update(h, sc_, vals):
        m_i = m_scr[h]
        m_new = jnp.maximum(m_i, jnp.max(sc_, axis=1, keepdims=True))
        alpha = jnp.exp(m_i - m_new)
        pp = jnp.exp(sc_ - m_new)
        l_scr[h] = alpha * l_scr[h] + jnp.sum(pp, axis=1, keepdims=True)
        acc_scr[h] = alpha * acc_scr[h] + jnp.dot(pp.astype(BF16), vals, preferred_element_type=F32)
        m_scr[h] = m_new

    @pl.when(p == 0)
    def _():
        for h in range(N_KV):
            qq = q_rows(h)
            kc = kcv_ref[0, h].astype(BF16)
            vc = kcv_ref[1, h].astype(BF16)
            s = lax.dot_general(qq, kc, _NT, preferred_element_type=F32) * SCALE
            qpos = past + (lax.broadcasted_iota(jnp.int32, (rq, ncp), 0) & (ds - 1))
            col = lax.broadcasted_iota(jnp.int32, (rq, ncp), 1)
            vis = (col * CMP_STRIDE + (CMP_LEN - 1) <= qpos) & (col < nc)
            s = jnp.where(vis, s, NEG)
            mx = jnp.max(s, axis=1, keepdims=True)
            e = jnp.where(vis, jnp.exp(s - mx), 0.0)
            den = jnp.sum(e, axis=1, keepdims=True)
            pr = e / jnp.where(den > 0.0, den, 1.0)
            ocmp_scr[h] = jnp.dot(pr.astype(BF16), vc, preferred_element_type=F32)
            pg = pr[0:ds]
            for g in range(1, GROUP):
                pg = pg + pr[g * ds:(g + 1) * ds]
            score = jnp.dot(pg, ms_ref[...], preferred_element_type=F32,
                            precision=lax.Precision.HIGHEST)
            lane = lax.broadcasted_iota(jnp.int32, (ds, lanes), 1)
            lane_in = lax.rem(lane, jnp.int32(HEAD_DIM))
            real = lane_in < BLK_PER_CHUNK
            blk = jnp.where(real, lax.div(lane, jnp.int32(HEAD_DIM)) * BLK_PER_CHUNK + lane_in, (1 << 19) + lane)
            cur = lax.div(past + lax.broadcasted_iota(jnp.int32, (ds, lanes), 0), jnp.int32(SLC_LEN))
            valid = real & (blk <= cur)
            forced = (blk == 0) | (blk > cur - N_LOCAL)
            sel = _select_topk(score, valid, forced, blk, 1)
            for c in range(n_sel_chunks):
                sel_scr[h, c] = sel[:, c * HEAD_DIM:(c + 1) * HEAD_DIM]
            m_scr[h] = jnp.full((rq, 1), NEG, F32)
            l_scr[h] = jnp.zeros((rq, 1), F32)
            acc_scr[h] = jnp.zeros((rq, HEAD_DIM), F32)

    k0 = p * CHUNK_ROWS
    for h in range(N_KV):
        qq = q_rows(h)
        kch = jnp.concatenate([pg_[:, h * HEAD_DIM:(h + 1) * HEAD_DIM] for pg_ in pages], axis=0).astype(BF16)
        vch = jnp.concatenate([pg_[:, KV_W + h * HEAD_DIM:KV_W + (h + 1) * HEAD_DIM] for pg_ in pages],
                              axis=0).astype(BF16)
        sc_ = lax.dot_general(qq, kch, _NT, preferred_element_type=F32) * SCALE
        selq = jnp.dot(sel_scr[h, p].astype(BF16), e_ref[...], preferred_element_type=F32)
        kpos = k0 + lax.broadcasted_iota(jnp.int32, (ds, CHUNK_ROWS), 1)
        qp = past + lax.broadcasted_iota(jnp.int32, (ds, CHUNK_ROWS), 0)
        okf = jnp.where((selq > 0.5) & (kpos <= qp), 1.0, 0.0)
        ok = jnp.concatenate([okf] * GROUP, axis=0) > 0.5
        online_update(h, jnp.where(ok, sc_, NEG), vch)

    @pl.when(p == n_steps - 1)
    def _():
        gt = jax.nn.sigmoid(g_ref[...])
        zpad = jnp.zeros((HEAD_DIM - ds, HEAD_DIM), F32)
        for h in range(N_KV):
            qq = q_rows(h)
            c_k = 2 * KV_W + h * HEAD_DIM
            c_v = 3 * KV_W + h * HEAD_DIM
            kn = jnp.concatenate([kvn_ref[:, c_k:c_k + HEAD_DIM], zpad], axis=0).astype(BF16)
            vn = jnp.concatenate([kvn_ref[:, c_v:c_v + HEAD_DIM], zpad], axis=0).astype(BF16)
            sn = lax.dot_general(qq, kn, _NT, preferred_element_type=F32) * SCALE
            last_sel = sel_scr[h, n_sel_chunks - 1]
            nb_last = (past // SLC_LEN) % BLK_PER_CHUNK
            seln = jnp.sum(jnp.where(lax.broadcasted_iota(jnp.int32, (ds, HEAD_DIM), 1) == nb_last,
                                     last_sel, 0.0), axis=1, keepdims=True)
            seln = jnp.concatenate([seln] * GROUP, axis=0)
            srow = lax.broadcasted_iota(jnp.int32, (rq, HEAD_DIM), 0) & (ds - 1)
            kcol = lax.broadcasted_iota(jnp.int32, (rq, HEAD_DIM), 1)
            okn = (seln > 0.5) & (kcol <= srow) & (kcol < ds)
            online_update(h, jnp.where(okn, sn, NEG), vn)
            o_slc = acc_scr[h] / l_scr[h]
            wb = st_ref.shape[0]
            band = wb + HEAD_DIM
            kw = jnp.concatenate([st_ref[:, h * HEAD_DIM:(h + 1) * HEAD_DIM],
                                  wn_ref[:, h * HEAD_DIM:(h + 1) * HEAD_DIM], zpad], axis=0).astype(BF16)
            vw = jnp.concatenate([st_ref[:, KV_W + h * HEAD_DIM:KV_W + (h + 1) * HEAD_DIM],
                                  wn_ref[:, KV_W + h * HEAD_DIM:KV_W + (h + 1) * HEAD_DIM], zpad],
                                 axis=0).astype(BF16)
            sw = lax.dot_general(qq, kw, _NT, preferred_element_type=F32) * SCALE
            srw = lax.broadcasted_iota(jnp.int32, (rq, band), 0) & (ds - 1)
            idx = lax.broadcasted_iota(jnp.int32, (rq, band), 1)
            dlt = (wb + srw) - idx
            sw = jnp.where((dlt >= 0) & (dlt < WINDOW) & (idx < wb + ds), sw, NEG)
            ew = jnp.exp(sw - jnp.max(sw, axis=1, keepdims=True))
            o_win = jnp.dot(ew.astype(BF16), vw, preferred_element_type=F32) / jnp.sum(ew, axis=1, keepdims=True)
            o_cmp = ocmp_scr[h]
            gh = gt[h]
            for g in range(GROUP):
                r0, r1 = g * ds, (g + 1) * ds
                og = gh[:, g:g + 1] * o_cmp[r0:r1] + gh[:, GROUP + g:GROUP + g + 1] * o_slc[r0:r1]
                og = og + gh[:, 2 * GROUP + g:2 * GROUP + g + 1] * o_win[r0:r1]
                c0 = (h * GROUP + g) * HEAD_DIM
                o_ref[:, c0:c0 + HEAD_DIM] = og


def _attn_sample(q, kcv, kv_new, win_new, win_state, g_re, cache3, page_table, *, ds):
    nb, npg = page_table.shape
    past = npg * PAGE
    steps = npg // PAGES_PER_STEP
    ncp = kcv.shape[3]
    wb = win_state.shape[1]
    n_blocks = past // SLC_LEN + 1
    n_sel_chunks = -(-n_blocks // BLK_PER_CHUNK)
    m = _overlap_matrix(ncp, n_sel_chunks * BLK_PER_CHUNK, ncp - 1, n_blocks)
    ms = np.zeros((ncp, n_sel_chunks, HEAD_DIM), np.float32)
    ms[:, :, :BLK_PER_CHUNK] = m.reshape(ncp, n_sel_chunks, BLK_PER_CHUNK)
    ms = jnp.asarray(ms.reshape(ncp, n_sel_chunks * HEAD_DIM))
    ee = np.zeros((HEAD_DIM, CHUNK_ROWS), np.float32)
    ee[:BLK_PER_CHUNK] = np.arange(CHUNK_ROWS)[None, :] // SLC_LEN == np.arange(BLK_PER_CHUNK)[:, None]
    ee = jnp.asarray(ee, dtype=BF16)
    rq = GROUP * ds

    def page_spec(n):
        return pl.BlockSpec((None, PAGE, 2 * KV_W),
                            lambda b, p, pt, n=n: (pt[b, p * PAGES_PER_STEP + n], 0, 1))

    grid_spec = pltpu.PrefetchScalarGridSpec(
        num_scalar_prefetch=1,
        grid=(nb, steps),
        in_specs=[page_spec(n) for n in range(PAGES_PER_STEP)] + [
            pl.BlockSpec((ds, N_HEADS * HEAD_DIM), lambda b, p, pt: (b, 0)),
            pl.BlockSpec((None, 2, N_KV, ncp, HEAD_DIM), lambda b, p, pt: (b, 0, 0, 0, 0)),
            pl.BlockSpec((ds, 4 * KV_W), lambda b, p, pt: (b, 0)),
            pl.BlockSpec((ds, 2 * KV_W), lambda b, p, pt: (b, 0)),
            pl.BlockSpec((None, wb, 2 * KV_W), lambda b, p, pt: (b, 0, 0)),
            pl.BlockSpec((None, N_KV, ds, 4 * GROUP), lambda b, p, pt: (b, 0, 0, 0)),
            pl.BlockSpec(ms.shape, lambda b, p, pt: (0, 0)),
            pl.BlockSpec(ee.shape, lambda b, p, pt: (0, 0))],
        out_specs=pl.BlockSpec((ds, N_HEADS * HEAD_DIM), lambda b, p, pt: (b, 0)),
        scratch_shapes=[pltpu.VMEM((N_KV, rq, 1), F32), pltpu.VMEM((N_KV, rq, 1), F32),
                        pltpu.VMEM((N_KV, rq, HEAD_DIM), F32),
                        pltpu.VMEM((N_KV, n_sel_chunks, ds, HEAD_DIM), F32),
                        pltpu.VMEM((N_KV, rq, HEAD_DIM), F32)],
    )
    return pl.pallas_call(
        functools.partial(_attn_sample_kernel, past=past, ds=ds),
        grid_spec=grid_spec,
        out_shape=jax.ShapeDtypeStruct((nb * ds, N_HEADS * HEAD_DIM), F32),
        compiler_params=_cparams(("arbitrary", "arbitrary")),
        name="attn_sample",
    )(page_table, *([cache3] * PAGES_PER_STEP), q, kcv, kv_new, win_new, win_state, g_re, ms, ee)


def _split_w_in(w_in, tc):
    d = D_MODEL
    o = 0
    parts = {}
    for name, width in (("q", N_HEADS * HEAD_DIM), ("kv", 4 * KV_W), ("win", 2 * KV_W), ("g", 3 * N_HEADS),
                        ("u", d), ("b", d), ("c", d), ("ma", d), ("mc", d)):
        parts[name] = w_in[:, o:o + width]
        o += width
    nj = d // tc
    ubc = jnp.stack([parts[n].reshape(d, nj, tc) for n in ("u", "b", "c")], axis=2).reshape(d, 3 * d)
    g_pad = jnp.pad(parts["g"], ((0, 0), (0, HEAD_DIM - 3 * N_HEADS)))
    gates = jnp.concatenate([parts["ma"], parts["mc"]], axis=1)
    return {"q": parts["q"].astype(BF16), "kv": parts["kv"].astype(BF16), "win": parts["win"].astype(BF16),
            "g": g_pad.astype(BF16), "ubc": ubc.astype(BF16), "gates": gates.astype(BF16)}


def _gate_layout(g_logits, nb, seq):
    g = g_logits[:, :3 * N_HEADS].reshape(nb, seq, 3, N_KV, GROUP)
    g = g.transpose(0, 3, 1, 2, 4).reshape(nb, N_KV, seq, 3 * GROUP)
    return jnp.pad(g, ((0, 0), (0, 0), (0, 0), (0, GROUP)))


def _cmp_weights(w_k1, w_k2, pe_k, w_v1, w_v2, pe_v):
    def pairs(w1):
        w = w1.reshape(2, CMP_STRIDE, HEAD_DIM, HEAD_DIM).transpose(1, 2, 0, 3)
        return w.reshape(CMP_STRIDE // 2, 2 * HEAD_DIM, 2 * HEAD_DIM)
    w1pairs = jnp.stack([pairs(w_k1), pairs(w_v1)]).astype(BF16)
    pe8 = jnp.stack([jnp.broadcast_to(pe_k.reshape(1, -1), (8, CMP_LEN * HEAD_DIM)),
                     jnp.broadcast_to(pe_v.reshape(1, -1), (8, CMP_LEN * HEAD_DIM))])
    return w1pairs, pe8, jnp.stack([w_k1, w_v1]), jnp.stack([w_k2, w_v2])


def kernel(x_prompt, x_sample, c_prompt, c_sample, cache_nsa_kv, page_table, state_win_kv, state_conv, w_ada, b_ada, norm1_g, norm2_g, w_in, w_cmp_k1, w_cmp_k2, pe_cmp_k, w_cmp_v1, w_cmp_v2, pe_cmp_v, conv_w, conv_b, w_attn_proj, w_conv_proj, w_out, w_mlp1, w_mlp2, normf_g):
    d = D_MODEL
    nbp, seq, _ = x_prompt.shape
    nbs, ds, _ = x_sample.shape
    tp = nbp * seq
    ts = nbs * ds
    tc = 256
    depth = w_in.shape[0]
    assert depth == 1

    xp = x_prompt.reshape(tp, d)
    xs = x_sample.reshape(ts, d)
    normf = normf_g.reshape(1, d)

    l = 0
    c_all = jnp.concatenate([c_prompt, c_sample, jnp.zeros((16 - nbp - nbs, d), F32)], axis=0)
    ada = _ada(c_all, w_ada[l], b_ada[l].reshape(1, -1)).reshape(16, 6, d)
    ada_p = [ada[:nbp, k][:, None, :] for k in range(6)]
    ada_s = [jnp.repeat(ada[nbp:nbp + nbs, k], ds, axis=0)[None] for k in range(6)]

    w = _split_w_in(w_in[l], tc)
    wa = w_attn_proj[l].astype(BF16)
    wc = w_conv_proj[l].astype(BF16)
    wo = w_out[l].astype(BF16)
    w1 = w_mlp1[l].astype(BF16)
    w2 = w_mlp2[l].astype(BF16)
    g1n = norm1_g[l].reshape(1, d)
    g2n = norm2_g[l].reshape(1, d)
    cw = conv_w[l]
    cb = conv_b[l].reshape(1, d)
    w1pairs, pe8, w1s, w2s = _cmp_weights(w_cmp_k1[l], w_cmp_k2[l], pe_cmp_k[l],
                                          w_cmp_v1[l], w_cmp_v2[l], pe_cmp_v[l])

    tm = 512
    hp = _norm_mod(xp, g1n, ada_p[1], ada_p[0], tm=256, seq=seq)
    q_p = _mm(hp, w["q"], tm=tm, tn=512, out_dtype=BF16, name="mm_q")
    kv_p = _mm(hp, w["kv"], tm=tm, tn=512, out_dtype=F32, name="mm_kv")
    win_p = _mm(hp, w["win"], tm=tm, tn=512, out_dtype=F32, name="mm_win")
    g_p = _mm(hp, w["g"], tm=tm, tn=HEAD_DIM, out_dtype=F32, name="mm_g")
    gates_p = _mm(hp, w["gates"], tm=tm, tn=512, out_dtype=BF16, act="sigmoid", name="mm_gates")
    z_p, conv_p = _mm_conv(hp, w["ubc"], cw, cb, tm=tm, tc=tc, seq=seq)

    pt_p = jnp.arange(tp // PAGE, dtype=jnp.int32).reshape(nbp, seq // PAGE)
    part_p = _cmp_part(kv_p.reshape(tp // PAGE, PAGE, 4 * KV_W), pt_p, w1pairs, 0)
    kcv_p = _cmp_finish(part_p, pe8, w1s, w2s)
    o_p = _attn_prompt(q_p, kcv_p, kv_p, win_p, _gate_layout(g_p, nbp, seq), nb=nbp, seq=seq)

    mixed_p = _mix(o_p, z_p, wa, wc, gates_p, tm=tm, tn=512)
    x1_p = _mm_resid(mixed_p, wo, xp, ada_p[2], tm=tm, tn=512, seq=seq)
    h2_p = _norm_mod(x1_p, g2n, ada_p[4], ada_p[3], tm=256, seq=seq)
    y_p = _mlp_final(h2_p, w1, w2, x1_p, ada_p[5], normf, tm=tm, tf=512, seq=seq)

    hs = _norm_mod(xs, g1n, ada_s[1], ada_s[0], tm=ts, seq=ds)
    q_s = _mm(hs, w["q"], tm=ts, tn=512, out_dtype=F32, name="mm_q_s")
    kv_s = _mm(hs, w["kv"], tm=ts, tn=512, out_dtype=F32, name="mm_kv_s")
    win_s = _mm(hs, w["win"], tm=ts, tn=512, out_dtype=F32, name="mm_win_s")
    g_s = _mm(hs, w["g"], tm=ts, tn=HEAD_DIM, out_dtype=F32, name="mm_g_s")
    gates_s = _mm(hs, w["gates"], tm=ts, tn=512, out_dtype=F32, act="sigmoid", name="mm_gates_s")
    ubc_s = _mm(hs, w["ubc"], tm=ts, tn=3 * tc, out_dtype=F32, name="mm_ubc_s")
    st = state_conv[l]
    z_s, cu_s = _sample_conv(ubc_s, jnp.repeat(st[:, 0], ds, axis=0), jnp.repeat(st[:, 1], ds, axis=0),
                             cw, cb, tc=tc, seq=ds)

    cache3 = cache_nsa_kv[l].reshape(cache_nsa_kv.shape[1], PAGE, 4 * KV_W)
    part_s = _cmp_part(cache3, page_table, w1pairs, 0)
    kcv_s = _cmp_finish(part_s, pe8, w1s, w2s)
    wst = state_win_kv[l].reshape(nbs, -1, 2 * KV_W)
    o_s = _attn_sample(q_s, kcv_s, kv_s, win_s, wst, _gate_layout(g_s, nbs, ds), cache3, page_table, ds=ds)

    mixed_s = _mix(o_s, z_s, wa, wc, gates_s, tm=ts, tn=512)
    x1_s = _mm_resid(mixed_s, wo, xs, ada_s[2], tm=ts, tn=512, seq=ds)
    h2_s = _norm_mod(x1_s, g2n, ada_s[4], ada_s[3], tm=ts, seq=ds)
    y_s = _mlp_final(h2_s, w1, w2, x1_s, ada_s[5], normf, tm=ts, tf=512, seq=ds)

    wbp = min(WINDOW, seq)
    wbs = wst.shape[1]
    kv_prompt = kv_p.reshape(1, nbp, seq, 4, N_KV, HEAD_DIM)
    kv_sample = kv_s.reshape(1, nbs, ds, 4, N_KV, HEAD_DIM)
    win_prompt = win_p.reshape(nbp, seq, 2, N_KV, HEAD_DIM)[None, :, seq - wbp:]
    win_sample = jnp.concatenate([wst, win_s.reshape(nbs, ds, 2 * KV_W)], axis=1)[:, ds:]
    win_sample = win_sample.reshape(1, nbs, wbs, 2, N_KV, HEAD_DIM)
    conv_prompt = conv_p[None]
    conv_sample = cu_s.reshape(nbs, ds, d)[None, :, ds - (CONV_W - 1):]
    return (y_p.reshape(nbp, seq, d), y_s.reshape(nbs, ds, d), kv_prompt, kv_sample,
            win_prompt, win_sample, conv_prompt, conv_sample)
```

```python
import functools

import numpy as np
import jax
import jax.numpy as jnp
from jax import lax
from jax.experimental import pallas as pl
from jax.experimental.pallas import tpu as pltpu

F32 = jnp.float32
BF16 = jnp.bfloat16

D_MODEL = 2048
HEAD_DIM = 128
N_HEADS = 16
N_KV = 4
GROUP = 4
KV_W = N_KV * HEAD_DIM
CMP_LEN = 32
CMP_STRIDE = 16
SLC_LEN = 64
SLC_TOP = 16
N_LOCAL = 2
WINDOW = 512
CONV_W = 3
RMS_EPS = 1e-6
NEG = -1e30
SCALE = HEAD_DIM ** -0.5
PAGE = 128
PAGES_PER_STEP = 16
CHUNK_ROWS = PAGES_PER_STEP * PAGE
BLK_PER_CHUNK = CHUNK_ROWS // SLC_LEN
VMEM_LIMIT = 56 * 1024 * 1024

_NT = (((1,), (1,)), ((), ()))
_TN = (((0,), (0,)), ((), ()))


def _cparams(sem):
    return pltpu.CompilerParams(dimension_semantics=sem, vmem_limit_bytes=VMEM_LIMIT)


def _ada_kernel(c_ref, w_ref, b_ref, o_ref):
    c = c_ref[...]
    a = (c * jax.nn.sigmoid(c)).astype(BF16)
    o_ref[...] = jnp.dot(a, w_ref[...].astype(BF16), preferred_element_type=F32) + b_ref[...]


def _ada(c, w, b, tn=1024):
    m, k = c.shape
    n = w.shape[1]
    return pl.pallas_call(
        _ada_kernel,
        grid=(n // tn,),
        in_specs=[pl.BlockSpec((m, k), lambda j: (0, 0)),
                  pl.BlockSpec((k, tn), lambda j: (0, j)),
                  pl.BlockSpec((1, tn), lambda j: (0, j))],
        out_specs=pl.BlockSpec((m, tn), lambda j: (0, j)),
        out_shape=jax.ShapeDtypeStruct((m, n), F32),
        compiler_params=_cparams(("arbitrary",)),
        name="ada",
    )(c, w, b)


def _rowspec(p, tm, tn, seq, col=True):
    gr = p.shape[1]
    if gr == 1:
        if col:
            return pl.BlockSpec((1, 1, tn), lambda i, j: ((i * tm) // seq, 0, j))
        return pl.BlockSpec((1, 1, tn), lambda i: ((i * tm) // seq, 0, 0))
    if col:
        return pl.BlockSpec((1, tm, tn), lambda i, j: (0, i, j))
    return pl.BlockSpec((1, tm, tn), lambda i: (0, i, 0))


def _norm_kernel(x_ref, g_ref, sc_ref, sh_ref, o_ref):
    x = x_ref[...]
    r = lax.rsqrt(jnp.mean(x * x, axis=-1, keepdims=True) + RMS_EPS)
    y = (x * r) * g_ref[...]
    o_ref[...] = (y * (1.0 + sc_ref[0]) + sh_ref[0]).astype(o_ref.dtype)


def _norm_mod(x, g, sc, sh, *, tm, seq):
    t, d = x.shape
    return pl.pallas_call(
        _norm_kernel,
        grid=(t // tm,),
        in_specs=[pl.BlockSpec((tm, d), lambda i: (i, 0)),
                  pl.BlockSpec((1, d), lambda i: (0, 0)),
                  _rowspec(sc, tm, d, seq, col=False),
                  _rowspec(sh, tm, d, seq, col=False)],
        out_specs=pl.BlockSpec((tm, d), lambda i: (i, 0)),
        out_shape=jax.ShapeDtypeStruct((t, d), BF16),
        compiler_params=_cparams(("arbitrary",)),
        name="norm_mod",
    )(x, g, sc, sh)


def _mm_kernel(x_ref, w_ref, o_ref, *, act):
    acc = jnp.dot(x_ref[...].astype(BF16), w_ref[...], preferred_element_type=F32)
    if act == "sigmoid":
        acc = jax.nn.sigmoid(acc)
    o_ref[...] = acc.astype(o_ref.dtype)


def _mm(x, w, *, tm, tn, out_dtype, act=None, name="mm"):
    t, k = x.shape
    n = w.shape[1]
    return pl.pallas_call(
        functools.partial(_mm_kernel, act=act),
        grid=(t // tm, n // tn),
        in_specs=[pl.BlockSpec((tm, k), lambda i, j: (i, 0)),
                  pl.BlockSpec((k, tn), lambda i, j: (0, j))],
        out_specs=pl.BlockSpec((tm, tn), lambda i, j: (i, j)),
        out_shape=jax.ShapeDtypeStruct((t, n), out_dtype),
        compiler_params=_cparams(("arbitrary", "arbitrary")),
        name=name,
    )(x, w)


def _store_head_major(o_ref, acc, tm):
    flat = o_ref.reshape(tm * 8, HEAD_DIM)
    for cb in range(8):
        flat[pl.ds(cb, tm, stride=8), :] = acc[:, cb * HEAD_DIM:(cb + 1) * HEAD_DIM]


def _mm_kv_kernel(x_ref, w_ref, o6_ref, ob_ref, *, tm):
    acc = jnp.dot(x_ref[...], w_ref[...], preferred_element_type=F32)
    _store_head_major(o6_ref, acc, tm)

    @pl.when(pl.program_id(1) == 1)
    def _():
        ob_ref[...] = acc.astype(BF16)


def _mm_kv(h, w, *, tm):
    t, k = h.shape
    tn = 2 * KV_W
    return pl.pallas_call(
        functools.partial(_mm_kv_kernel, tm=tm),
        grid=(t // tm, 2),
        in_specs=[pl.BlockSpec((tm, k), lambda i, j: (i, 0)),
                  pl.BlockSpec((k, tn), lambda i, j: (0, j))],
        out_specs=[pl.BlockSpec((tm, 8, HEAD_DIM), lambda i, j: (i, j, 0)),
                   pl.BlockSpec((tm, tn), lambda i, j: (i, 0))],
        out_shape=[jax.ShapeDtypeStruct((t, 16, HEAD_DIM), F32),
                   jax.ShapeDtypeStruct((t, tn), BF16)],
        compiler_params=_cparams(("arbitrary", "arbitrary")),
        name="mm_kv",
    )(h, w)


def _mm_win_kernel(x_ref, w_ref, ob_ref, o6_ref, *, tm, seq):
    acc = jnp.dot(x_ref[...], w_ref[...], preferred_element_type=F32)
    ob_ref[...] = acc.astype(BF16)

    @pl.when(((pl.program_id(0) + 1) * tm) % seq == 0)
    def _():
        _store_head_major(o6_ref, acc, tm)


def _mm_win(h, w, *, tm, seq):
    t, k = h.shape
    tn = 2 * KV_W
    return pl.pallas_call(
        functools.partial(_mm_win_kernel, tm=tm, seq=seq),
        grid=(t // tm,),
        in_specs=[pl.BlockSpec((tm, k), lambda i: (i, 0)),
                  pl.BlockSpec((k, tn), lambda i: (0, 0))],
        out_specs=[pl.BlockSpec((tm, tn), lambda i: (i, 0)),
                   pl.BlockSpec((tm, 8, HEAD_DIM), lambda i: ((i * tm) // seq, 0, 0))],
        out_shape=[jax.ShapeDtypeStruct((t, tn), BF16),
                   jax.ShapeDtypeStruct((t // seq * tm, 8, HEAD_DIM), F32)],
        compiler_params=_cparams(("arbitrary",)),
        name="mm_win",
    )(h, w)


def _conv_taps(cu, prev1, prev2, bgate, cw_ref, cb_ref):
    y = cb_ref[...] + prev2 * cw_ref[0:1, :]
    y = y + prev1 * cw_ref[1:2, :]
    y = y + cu * cw_ref[2:3, :]
    return bgate * y


def _mm_conv_kernel(x_ref, w_ref, cw_ref, cb_ref, z_ref, st_ref, carry_ref, *, tm, tc, seq):
    i = pl.program_id(0)
    j = pl.program_id(1)
    acc = jnp.dot(x_ref[...], w_ref[...], preferred_element_type=F32)
    u = acc[:, 0:tc]
    bgate = acc[:, tc:2 * tc]
    cu = acc[:, 2 * tc:3 * tc] * u
    @pl.when((i * tm) % seq == 0)
    def _():
        carry_ref[j] = jnp.zeros((8, tc), F32)

    car = carry_ref[j]
    p0 = car[0:1, :]
    p1 = car[1:2, :]
    rows = lax.broadcasted_iota(jnp.int32, (tm, tc), 0)
    r1 = pltpu.roll(cu, 1, 0)
    r2 = pltpu.roll(cu, 2, 0)
    prev1 = jnp.where(rows == 0, p1, r1)
    prev2 = jnp.where(rows == 0, p0, jnp.where(rows == 1, p1, r2))
    z_ref[...] = _conv_taps(cu, prev1, prev2, bgate, cw_ref, cb_ref).astype(z_ref.dtype)
    tail = cu[tm - 8:tm, :]
    carry_ref[j] = jnp.concatenate([tail[6:8, :], tail[0:6, :]], axis=0)
    st_ref[0] = tail[6:8, :]


def _mm_conv(h, w_il, conv_w, conv_b, *, tm, tc, seq):
    t, k = h.shape
    d = w_il.shape[1] // 3
    nj = d // tc
    return pl.pallas_call(
        functools.partial(_mm_conv_kernel, tm=tm, tc=tc, seq=seq),
        grid=(t // tm, nj),
        in_specs=[pl.BlockSpec((tm, k), lambda i, j: (i, 0)),
                  pl.BlockSpec((k, 3 * tc), lambda i, j: (0, j)),
                  pl.BlockSpec((CONV_W, tc), lambda i, j: (0, j)),
                  pl.BlockSpec((1, tc), lambda i, j: (0, j))],
        out_specs=[pl.BlockSpec((tm, tc), lambda i, j: (i, j)),
                   pl.BlockSpec((1, CONV_W - 1, tc), lambda i, j: (i, 0, j))],
        out_shape=[jax.ShapeDtypeStruct((t, d), BF16),
                   jax.ShapeDtypeStruct((t // tm, CONV_W - 1, d), F32)],
        scratch_shapes=[pltpu.VMEM((nj, 8, tc), F32)],
        compiler_params=_cparams(("arbitrary", "arbitrary")),
        name="mm_conv",
    )(h, w_il, conv_w, conv_b)


def _sample_conv_kernel(a_ref, st0_ref, st1_ref, cw_ref, cb_ref, z_ref, cu_ref, *, tc, seq):
    acc = a_ref[...]
    t = acc.shape[0]
    u = acc[:, 0:tc]
    bgate = acc[:, tc:2 * tc]
    cu = acc[:, 2 * tc:3 * tc] * u
    s = lax.rem(lax.broadcasted_iota(jnp.int32, (t, tc), 0), jnp.int32(seq))
    r1 = pltpu.roll(cu, 1, 0)
    r2 = pltpu.roll(cu, 2, 0)
    prev1 = jnp.where(s == 0, st1_ref[...], r1)
    prev2 = jnp.where(s == 0, st0_ref[...], jnp.where(s == 1, st1_ref[...], r2))
    z_ref[...] = _conv_taps(cu, prev1, prev2, bgate, cw_ref, cb_ref)
    cu_ref[...] = cu


def _sample_conv(ubc, st0, st1, conv_w, conv_b, *, tc, seq):
    t = ubc.shape[0]
    d = ubc.shape[1] // 3
    return pl.pallas_call(
        functools.partial(_sample_conv_kernel, tc=tc, seq=seq),
        grid=(d // tc,),
        in_specs=[pl.BlockSpec((t, 3 * tc), lambda j: (0, j)),
                  pl.BlockSpec((t, tc), lambda j: (0, j)),
                  pl.BlockSpec((t, tc), lambda j: (0, j)),
                  pl.BlockSpec((CONV_W, tc), lambda j: (0, j)),
                  pl.BlockSpec((1, tc), lambda j: (0, j))],
        out_specs=[pl.BlockSpec((t, tc), lambda j: (0, j)),
                   pl.BlockSpec((t, tc), lambda j: (0, j))],
        out_shape=[jax.ShapeDtypeStruct((t, d), F32), jax.ShapeDtypeStruct((t, d), F32)],
        compiler_params=_cparams(("arbitrary",)),
        name="sample_conv",
    )(ubc, st0, st1, conv_w, conv_b)


def _mix_kernel(o_ref, z_ref, wa_ref, wc_ref, ga_ref, gc_ref, m_ref):
    a = jnp.dot(o_ref[...].astype(BF16), wa_ref[...], preferred_element_type=F32)
    c = jnp.dot(z_ref[...].astype(BF16), wc_ref[...], preferred_element_type=F32)
    m_ref[...] = (ga_ref[...].astype(F32) * a + gc_ref[...].astype(F32) * c).astype(m_ref.dtype)


def _mix(o, z, wa, wc, gates, *, tm, tn):
    t, k = o.shape
    n = wa.shape[1]
    nj = n // tn
    return pl.pallas_call(
        _mix_kernel,
        grid=(t // tm, nj),
        in_specs=[pl.BlockSpec((tm, k), lambda i, j: (i, 0)),
                  pl.BlockSpec((tm, k), lambda i, j: (i, 0)),
                  pl.BlockSpec((k, tn), lambda i, j: (0, j)),
                  pl.BlockSpec((k, tn), lambda i, j: (0, j)),
                  pl.BlockSpec((tm, tn), lambda i, j: (i, j)),
                  pl.BlockSpec((tm, tn), lambda i, j: (i, j + nj))],
        out_specs=pl.BlockSpec((tm, tn), lambda i, j: (i, j)),
        out_shape=jax.ShapeDtypeStruct((t, n), BF16),
        compiler_params=_cparams(("arbitrary", "arbitrary")),
        name="mix",
    )(o, z, wa, wc, gates, gates)


def _resid_kernel(a_ref, w_ref, x_ref, g_ref, o_ref):
    acc = jnp.dot(a_ref[...], w_ref[...], preferred_element_type=F32)
    o_ref[...] = x_ref[...] + g_ref[0] * acc


def _mm_resid(a, w, x, gate, *, tm, tn, seq):
    t, k = a.shape
    n = w.shape[1]
    return pl.pallas_call(
        _resid_kernel,
        grid=(t // tm, n // tn),
        in_specs=[pl.BlockSpec((tm, k), lambda i, j: (i, 0)),
                  pl.BlockSpec((k, tn), lambda i, j: (0, j)),
                  pl.BlockSpec((tm, tn), lambda i, j: (i, j)),
                  _rowspec(gate, tm, tn, seq)],
        out_specs=pl.BlockSpec((tm, tn), lambda i, j: (i, j)),
        out_shape=jax.ShapeDtypeStruct((t, n), F32),
        compiler_params=_cparams(("arbitrary", "arbitrary")),
        name="mm_resid",
    )(a, w, x, gate)


def _mlp_kernel(h_ref, w1_ref, w2_ref, x_ref, g_ref, nf_ref, o_ref, acc_ref):
    k = pl.program_id(1)

    @pl.when(k == 0)
    def _():
        acc_ref[...] = jnp.zeros_like(acc_ref)

    a = jnp.dot(h_ref[...], w1_ref[...], preferred_element_type=F32)
    a = jnp.square(jnp.maximum(a, 0.0)).astype(BF16)
    acc_ref[...] += jnp.dot(a, w2_ref[...], preferred_element_type=F32)

    @pl.when(k == pl.num_programs(1) - 1)
    def _():
        x2 = x_ref[...] + g_ref[0] * acc_ref[...]
        r = lax.rsqrt(jnp.mean(x2 * x2, axis=-1, keepdims=True) + RMS_EPS)
        o_ref[...] = (x2 * r) * nf_ref[...]


def _mlp_final(h2, w1, w2, x1, gate, normf, *, tm, tf, seq):
    t, d = h2.shape
    f = w1.shape[1]
    return pl.pallas_call(
        _mlp_kernel,
        grid=(t // tm, f // tf),
        in_specs=[pl.BlockSpec((tm, d), lambda i, k: (i, 0)),
                  pl.BlockSpec((d, tf), lambda i, k: (0, k)),
                  pl.BlockSpec((tf, d), lambda i, k: (k, 0)),
                  pl.BlockSpec((tm, d), lambda i, k: (i, 0)),
                  _rowspec_k(gate, tm, d, seq),
                  pl.BlockSpec((1, d), lambda i, k: (0, 0))],
        out_specs=pl.BlockSpec((tm, d), lambda i, k: (i, 0)),
        out_shape=jax.ShapeDtypeStruct((t, d), F32),
        scratch_shapes=[pltpu.VMEM((tm, d), F32)],
        compiler_params=_cparams(("arbitrary", "arbitrary")),
        name="mlp",
    )(h2, w1, w2, x1, gate, normf)


def _rowspec_k(p, tm, d, seq):
    if p.shape[1] == 1:
        return pl.BlockSpec((1, 1, d), lambda i, k: ((i * tm) // seq, 0, 0))
    return pl.BlockSpec((1, tm, d), lambda i, k: (0, i, 0))


def _cmp_part_kernel(pt_ref, *refs):
    pages = refs[:PAGES_PER_STEP]
    w_ref = refs[PAGES_PER_STEP]
    o_ref = refs[PAGES_PER_STEP + 1]
    nchunk = CHUNK_ROWS // CMP_STRIDE
    cpp = PAGE // CMP_STRIDE
    flat = [pg.reshape(PAGE * 8, HEAD_DIM) for pg in pages]
    for kind in range(2):
        acc = None
        for rp in range(CMP_STRIDE // 2):
            halves = []
            for r in (2 * rp, 2 * rp + 1):
                halves.append(jnp.concatenate(
                    [fp[pl.ds(r * 8 + kind * N_KV + h, cpp, stride=CMP_STRIDE * 8), :]
                     for h in range(N_KV) for fp in flat], axis=0))
            lhs = jnp.concatenate(halves, axis=1).astype(BF16)
            d = jnp.dot(lhs, w_ref[kind, rp], preferred_element_type=F32)
            acc = d if acc is None else acc + d
        for h in range(N_KV):
            o_ref[kind, h] = acc[h * nchunk:(h + 1) * nchunk, :]


def _page_spec(n, kind_pair):
    return pl.BlockSpec((None, PAGE, 8, HEAD_DIM),
                        lambda b, p, pt, n=n: (pt[b, p * PAGES_PER_STEP + n], 0, kind_pair, 0))


def _cmp_part(pages4, page_table, w1pairs):
    nb, npg = page_table.shape
    steps = npg // PAGES_PER_STEP
    nchunk = PAGES_PER_STEP * (PAGE // CMP_STRIDE)
    grid_spec = pltpu.PrefetchScalarGridSpec(
        num_scalar_prefetch=1,
        grid=(nb, steps),
        in_specs=[_page_spec(n, 0) for n in range(PAGES_PER_STEP)]
        + [pl.BlockSpec((2, CMP_STRIDE // 2, 2 * HEAD_DIM, 2 * HEAD_DIM), lambda b, p, pt: (0, 0, 0, 0))],
        out_specs=pl.BlockSpec((None, 2, N_KV, nchunk, 2 * HEAD_DIM), lambda b, p, pt: (b, 0, 0, p, 0)),
    )
    return pl.pallas_call(
        _cmp_part_kernel,
        grid_spec=grid_spec,
        out_shape=jax.ShapeDtypeStruct((nb, 2, N_KV, steps * nchunk, 2 * HEAD_DIM), F32),
        compiler_params=_cparams(("arbitrary", "arbitrary")),
        name="cmp_part",
    )(page_table, *([pages4] * PAGES_PER_STEP), w1pairs)


def _gelu_tanh(x):
    c = np.sqrt(2.0 / np.pi).astype(np.float32)
    return 0.5 * x * (1.0 + jnp.tanh(c * (x + 0.044715 * (x * x * x))))


def _cmp_finish_kernel(part_ref, pe_ref, w1_ref, w2_ref, o_ref, bias_ref):
    @pl.when((pl.program_id(1) == 0) & (pl.program_id(2) == 0))
    def _():
        bias_ref[...] = jnp.dot(pe_ref[...], w1_ref[...], preferred_element_type=F32,
                                precision=lax.Precision.HIGHEST)

    part = part_ref[...]
    n = part.shape[0]
    h = part[:, 0:HEAD_DIM] + pltpu.roll(part[:, HEAD_DIM:], n - 1, 0)
    h = h + bias_ref[0:1, :]
    o_ref[...] = jnp.dot(_gelu_tanh(h).astype(BF16), w2_ref[...].astype(BF16),
                         preferred_element_type=F32).astype(o_ref.dtype)


def _cmp_finish(part, pe8, w1, w2):
    nb, _, _, nchunk, _ = part.shape
    return pl.pallas_call(
        _cmp_finish_kernel,
        grid=(2, nb, N_KV),
        in_specs=[pl.BlockSpec((None, None, None, nchunk, 2 * HEAD_DIM), lambda k, b, h: (b, k, h, 0, 0)),
                  pl.BlockSpec((None, 8, CMP_LEN * HEAD_DIM), lambda k, b, h: (k, 0, 0)),
                  pl.BlockSpec((None, CMP_LEN * HEAD_DIM, HEAD_DIM), lambda k, b, h: (k, 0, 0)),
                  pl.BlockSpec((None, HEAD_DIM, HEAD_DIM), lambda k, b, h: (k, 0, 0))],
        out_specs=pl.BlockSpec((None, None, None, nchunk, HEAD_DIM), lambda k, b, h: (b, k, h, 0, 0)),
        out_shape=jax.ShapeDtypeStruct((nb, 2, N_KV, nchunk, HEAD_DIM), BF16),
        scratch_shapes=[pltpu.VMEM((8, HEAD_DIM), F32)],
        compiler_params=_cparams(("arbitrary", "arbitrary", "arbitrary")),
        name="cmp_finish",
    )(part, pe8, w1, w2)


def _select_topk(score, valid, forced, blk, axis):
    sc = jnp.where(valid, jnp.where(forced, jnp.inf, score), -jnp.inf)
    sel = jnp.zeros(score.shape, F32)
    big = jnp.int32(1 << 20)
    for _ in range(SLC_TOP):
        mx = jnp.max(sc, axis=axis, keepdims=True)
        idx = jnp.min(jnp.where(sc == mx, blk, big), axis=axis, keepdims=True)
        hit = blk == idx
        sel = jnp.where(hit & (mx > -jnp.inf), 1.0, sel)
        sc = jnp.where(hit, -jnp.inf, sc)
    return sel


def _overlap_matrix(nc_pad, n_blocks_pad, nc, n_blocks):
    cs = np.arange(nc_pad) * CMP_STRIDE
    sb = np.arange(n_blocks_pad) * SLC_LEN
    ov = np.clip(np.minimum(cs[:, None] + CMP_LEN, sb[None, :] + SLC_LEN)
                 - np.maximum(cs[:, None], sb[None, :]), 0, None)
    m = (ov / CMP_STRIDE).astype(np.float32)
    m[nc:, :] = 0.0
    m[:, n_blocks:] = 0.0
    return m


def _attn_prompt_kernel(q_ref, kcb, vcb, ksb, vsb, kwb, vwb, g_ref, mt_ref, e_ref, o_ref, *, tq, seq, ck):
    qi = pl.program_id(2)
    nc = seq // CMP_STRIDE - 1
    nsb = seq // SLC_LEN

    q = q_ref[...]
    qq = jnp.concatenate([q[:, g * HEAD_DIM:(g + 1) * HEAD_DIM] for g in range(GROUP)], axis=0)
    rq = GROUP * tq
    q0 = qi * tq

    ncp = kcb.shape[0]
    s = lax.dot_general(qq, kcb[...], _NT, preferred_element_type=F32) * SCALE
    qpos = q0 + (lax.broadcasted_iota(jnp.int32, (rq, ncp), 0) & (tq - 1))
    col = lax.broadcasted_iota(jnp.int32, (rq, ncp), 1)
    vis = (col * CMP_STRIDE + (CMP_LEN - 1) <= qpos) & (col < nc)
    s = jnp.where(vis, s, NEG)
    mx = jnp.max(s, axis=1, keepdims=True)
    e = jnp.where(vis, jnp.exp(s - mx), 0.0)
    den = jnp.sum(e, axis=1, keepdims=True)
    p = e / jnp.where(den > 0.0, den, 1.0)
    o_cmp = jnp.dot(p.astype(BF16), vcb[...], preferred_element_type=F32)
    pg = p[0:tq]
    for g in range(1, GROUP):
        pg = pg + p[g * tq:(g + 1) * tq]

    sct = lax.dot_general(mt_ref[...], pg, _NT, preferred_element_type=F32,
                          precision=lax.Precision.HIGHEST)
    blk = lax.broadcasted_iota(jnp.int32, (nsb, tq), 0)
    cur = lax.div(q0 + lax.broadcasted_iota(jnp.int32, (nsb, tq), 1), jnp.int32(SLC_LEN))
    valid = blk <= cur
    forced = (blk == 0) | (blk > cur - N_LOCAL)
    need_topk = (q0 + tq - 1) // SLC_LEN + 1 > SLC_TOP
    sel_t = lax.cond(need_topk,
                     lambda: _select_topk(sct, valid, forced, blk, 0),
                     lambda: valid.astype(F32)).astype(BF16)

    def slc_body(c, carry):
        m_i, l_i, acc = carry
        k0 = pl.multiple_of(c * ck, ck)
        kch = ksb[pl.ds(k0, ck), :]
        vch = vsb[pl.ds(k0, ck), :]
        sc_ = lax.dot_general(qq, kch, _NT, preferred_element_type=F32) * SCALE
        selq = lax.dot_general(sel_t, e_ref[c], _TN, preferred_element_type=F32)
        kpos = k0 + lax.broadcasted_iota(jnp.int32, (tq, ck), 1)
        qp = q0 + lax.broadcasted_iota(jnp.int32, (tq, ck), 0)
        okf = jnp.where((selq > 0.5) & (kpos <= qp), 1.0, 0.0)
        ok = jnp.concatenate([okf] * GROUP, axis=0) > 0.5
        sc_ = jnp.where(ok, sc_, NEG)
        m_new = jnp.maximum(m_i, jnp.max(sc_, axis=1, keepdims=True))
        alpha = jnp.exp(m_i - m_new)
        pp = jnp.exp(sc_ - m_new)
        l_new = alpha * l_i + jnp.sum(pp, axis=1, keepdims=True)
        acc = alpha * acc + jnp.dot(pp.astype(BF16), vch, preferred_element_type=F32)
        return m_new, l_new, acc

    n_ck = (q0 + tq + ck - 1) // ck
    m_f, l_f, acc_f = lax.fori_loop(
        0, n_ck, slc_body,
        (jnp.full((rq, 1), NEG, F32), jnp.zeros((rq, 1), F32), jnp.zeros((rq, HEAD_DIM), F32)))
    o_slc = acc_f / l_f

    band = WINDOW + tq
    ks0 = pl.multiple_of(jnp.maximum(q0 - WINDOW, 0), tq)
    kw = kwb[pl.ds(ks0, band), :]
    vw = vwb[pl.ds(ks0, band), :]
    sw = lax.dot_general(qq, kw, _NT, preferred_element_type=F32) * SCALE
    qpw = q0 + (lax.broadcasted_iota(jnp.int32, (rq, band), 0) & (tq - 1))
    dlt = qpw - (ks0 + lax.broadcasted_iota(jnp.int32, (rq, band), 1))
    sw = jnp.where((dlt >= 0) & (dlt < WINDOW), sw, NEG)
    ew = jnp.exp(sw - jnp.max(sw, axis=1, keepdims=True))
    o_win = jnp.dot(ew.astype(BF16), vw, preferred_element_type=F32) / jnp.sum(ew, axis=1, keepdims=True)

    gt = jax.nn.sigmoid(g_ref[...])
    for g in range(GROUP):
        r0, r1 = g * tq, (g + 1) * tq
        og = gt[:, g:g + 1] * o_cmp[r0:r1] + gt[:, GROUP + g:GROUP + g + 1] * o_slc[r0:r1]
        og = og + gt[:, 2 * GROUP + g:2 * GROUP + g + 1] * o_win[r0:r1]
        o_ref[:, g * HEAD_DIM:(g + 1) * HEAD_DIM] = og.astype(o_ref.dtype)


def _attn_prompt(q, kcv, kv, win, g_re, *, nb, seq, tq=128, ck=512):
    assert tq & (tq - 1) == 0 and seq % ck == 0 and WINDOW % tq == 0
    t = q.shape[0]
    ncp = seq // CMP_STRIDE
    nsb = seq // SLC_LEN
    nq = seq // tq
    mt = jnp.asarray(_overlap_matrix(ncp, nsb, ncp - 1, nsb).T)
    kk = np.arange(seq)
    e3 = (kk[None, :] // SLC_LEN == np.arange(nsb)[:, None]).astype(np.float32)
    e3 = jnp.asarray(e3.reshape(nsb, seq // ck, ck).transpose(1, 0, 2), dtype=BF16)
    slab = lambda cb: pl.BlockSpec((seq, HEAD_DIM), lambda b, h, i, cb=cb: (b, cb + h))
    return pl.pallas_call(
        functools.partial(_attn_prompt_kernel, tq=tq, seq=seq, ck=ck),
        grid=(nb, N_KV, nq),
        in_specs=[pl.BlockSpec((tq, GROUP * HEAD_DIM), lambda b, h, i: (b * nq + i, h)),
                  pl.BlockSpec((None, None, None, ncp, HEAD_DIM), lambda b, h, i: (b, 0, h, 0, 0)),
                  pl.BlockSpec((None, None, None, ncp, HEAD_DIM), lambda b, h, i: (b, 1, h, 0, 0)),
                  slab(0), slab(N_KV), slab(0), slab(N_KV),
                  pl.BlockSpec((None, None, tq, 4 * GROUP), lambda b, h, i: (b, h, i, 0)),
                  pl.BlockSpec((nsb, ncp), lambda b, h, i: (0, 0)),
                  pl.BlockSpec((seq // ck, nsb, ck), lambda b, h, i: (0, 0, 0))],
        out_specs=pl.BlockSpec((tq, GROUP * HEAD_DIM), lambda b, h, i: (b * nq + i, h)),
        out_shape=jax.ShapeDtypeStruct((t, N_HEADS * HEAD_DIM), BF16),
        compiler_params=_cparams(("arbitrary", "arbitrary", "arbitrary")),
        name="attn_prompt",
    )(q, kcv, kcv, kv, kv, win, win, g_re, mt, e3)


def _attn_sample_kernel(pt_ref, *refs, past, ds):
    pages = refs[:PAGES_PER_STEP]
    (q_ref, kcv_ref, kvn_ref, wn_ref, st_ref, g_ref, ms_ref, e_ref, o_ref,
     m_scr, l_scr, acc_scr, sel_scr, ocmp_scr) = refs[PAGES_PER_STEP:]
    p = pl.program_id(1)
    n_steps = pl.num_programs(1)
    rq = GROUP * ds
    ncp = kcv_ref.shape[2]
    nc = ncp - 1
    n_sel_chunks = sel_scr.shape[1]
    lanes = n_sel_chunks * HEAD_DIM

    def q_rows(h):
        c0 = h * GROUP * HEAD_DIM
        return jnp.concatenate(
            [q_ref[:, c0 + g * HEAD_DIM:c0 + (g + 1) * HEAD_DIM] for g in range(GROUP)], axis=0).astype(BF16)

    def online_update(h, sc_, vals):
        m_i = m_scr[h]
        m_new = jnp.maximum(m_i, jnp.max(sc_, axis=1, keepdims=True))
        alpha = jnp.exp(m_i - m_new)
        pp = jnp.exp(sc_ - m_new)
        l_scr[h] = alpha * l_scr[h] + jnp.sum(pp, axis=1, keepdims=True)
        acc_scr[h] = alpha * acc_scr[h] + jnp.dot(pp.astype(BF16), vals, preferred_element_type=F32)
        m_scr[h] = m_new

    @pl.when(p == 0)
    def _():
        for h in range(N_KV):
            qq = q_rows(h)
            kc = kcv_ref[0, h]
            vc = kcv_ref[1, h]
            s = lax.dot_general(qq, kc, _NT, preferred_element_type=F32) * SCALE
            qpos = past + (lax.broadcasted_iota(jnp.int32, (rq, ncp), 0) & (ds - 1))
            col = lax.broadcasted_iota(jnp.int32, (rq, ncp), 1)
            vis = (col * CMP_STRIDE + (CMP_LEN - 1) <= qpos) & (col < nc)
            s = jnp.where(vis, s, NEG)
            mx = jnp.max(s, axis=1, keepdims=True)
            e = jnp.where(vis, jnp.exp(s - mx), 0.0)
            den = jnp.sum(e, axis=1, keepdims=True)
            pr = e / jnp.where(den > 0.0, den, 1.0)
            ocmp_scr[h] = jnp.dot(pr.astype(BF16), vc, preferred_element_type=F32)
            pg = pr[0:ds]
            for g in range(1, GROUP):
                pg = pg + pr[g * ds:(g + 1) * ds]
            score = jnp.dot(pg, ms_ref[...], preferred_element_type=F32,
                            precision=lax.Precision.HIGHEST)
            lane = lax.broadcasted_iota(jnp.int32, (ds, lanes), 1)
            lane_in = lax.rem(lane, jnp.int32(HEAD_DIM))
            real = lane_in < BLK_PER_CHUNK
            blk = jnp.where(real, lax.div(lane, jnp.int32(HEAD_DIM)) * BLK_PER_CHUNK + lane_in, (1 << 19) + lane)
            cur = lax.div(past + lax.broadcasted_iota(jnp.int32, (ds, lanes), 0), jnp.int32(SLC_LEN))
            valid = real & (blk <= cur)
            forced = (blk == 0) | (blk > cur - N_LOCAL)
            sel = _select_topk(score, valid, forced, blk, 1)
            for c in range(n_sel_chunks):
                sel_scr[h, c] = sel[:, c * HEAD_DIM:(c + 1) * HEAD_DIM]
            m_scr[h] = jnp.full((rq, 1), NEG, F32)
            l_scr[h] = jnp.zeros((rq, 1), F32)
            acc_scr[h] = jnp.zeros((rq, HEAD_DIM), F32)

    k0 = p * CHUNK_ROWS
    flat = [pg_.reshape(PAGE * 8, HEAD_DIM) for pg_ in pages]
    for h in range(N_KV):
        qq = q_rows(h)
        kch = jnp.concatenate([fp[pl.ds(h, PAGE, stride=8), :] for fp in flat], axis=0).astype(BF16)
        vch = jnp.concatenate([fp[pl.ds(N_KV + h, PAGE, stride=8), :] for fp in flat], axis=0).astype(BF16)
        sc_ = lax.dot_general(qq, kch, _NT, preferred_element_type=F32) * SCALE
        selq = jnp.dot(sel_scr[h, p].astype(BF16), e_ref[...], preferred_element_type=F32)
        kpos = k0 + lax.broadcasted_iota(jnp.int32, (ds, CHUNK_ROWS), 1)
        qp = past + lax.broadcasted_iota(jnp.int32, (ds, CHUNK_ROWS), 0)
        okf = jnp.where((selq > 0.5) & (kpos <= qp), 1.0, 0.0)
        ok = jnp.concatenate([okf] * GROUP, axis=0) > 0.5
        online_update(h, jnp.where(ok, sc_, NEG), vch)

    @pl.when(p == n_steps - 1)
    def _():
        gt = jax.nn.sigmoid(g_ref[...])
        zpad = jnp.zeros((HEAD_DIM - ds, HEAD_DIM), F32)
        for h in range(N_KV):
            qq = q_rows(h)
            c_k = 2 * KV_W + h * HEAD_DIM
            c_v = 3 * KV_W + h * HEAD_DIM
            kn = jnp.concatenate([kvn_ref[:, c_k:c_k + HEAD_DIM], zpad], axis=0).astype(BF16)
            vn = jnp.concatenate([kvn_ref[:, c_v:c_v + HEAD_DIM], zpad], axis=0).astype(BF16)
            sn = lax.dot_general(qq, kn, _NT, preferred_element_type=F32) * SCALE
            last_sel = sel_scr[h, n_sel_chunks - 1]
            nb_last = (past // SLC_LEN) % BLK_PER_CHUNK
            seln = jnp.sum(jnp.where(lax.broadcasted_iota(jnp.int32, (ds, HEAD_DIM), 1) == nb_last,
                                     last_sel, 0.0), axis=1, keepdims=True)
            seln = jnp.concatenate([seln] * GROUP, axis=0)
            srow = lax.broadcasted_iota(jnp.int32, (rq, HEAD_DIM), 0) & (ds - 1)
            kcol = lax.broadcasted_iota(jnp.int32, (rq, HEAD_DIM), 1)
            okn = (seln > 0.5) & (kcol <= srow) & (kcol < ds)
            online_update(h, jnp.where(okn, sn, NEG), vn)
            o_slc = acc_scr[h] / l_scr[h]
            wb = st_ref.shape[0]
            band = wb + HEAD_DIM
            st_flat = st_ref.reshape(wb * 8, HEAD_DIM)
            kw = jnp.concatenate([st_flat[pl.ds(h, wb, stride=8), :],
                                  wn_ref[:, h * HEAD_DIM:(h + 1) * HEAD_DIM], zpad], axis=0).astype(BF16)
            vw = jnp.concatenate([st_flat[pl.ds(N_KV + h, wb, stride=8), :],
                                  wn_ref[:, KV_W + h * HEAD_DIM:KV_W + (h + 1) * HEAD_DIM], zpad],
                                 axis=0).astype(BF16)
            sw = lax.dot_general(qq, kw, _NT, preferred_element_type=F32) * SCALE
            srw = lax.broadcasted_iota(jnp.int32, (rq, band), 0) & (ds - 1)
            idx = lax.broadcasted_iota(jnp.int32, (rq, band), 1)
            dlt = (wb + srw) - idx
            sw = jnp.where((dlt >= 0) & (dlt < WINDOW) & (idx < wb + ds), sw, NEG)
            ew = jnp.exp(sw - jnp.max(sw, axis=1, keepdims=True))
            o_win = jnp.dot(ew.astype(BF16), vw, preferred_element_type=F32) / jnp.sum(ew, axis=1, keepdims=True)
            o_cmp = ocmp_scr[h]
            gh = gt[h]
            for g in range(GROUP):
                r0, r1 = g * ds, (g + 1) * ds
                og = gh[:, g:g + 1] * o_cmp[r0:r1] + gh[:, GROUP + g:GROUP + g + 1] * o_slc[r0:r1]
                og = og + gh[:, 2 * GROUP + g:2 * GROUP + g + 1] * o_win[r0:r1]
                c0 = (h * GROUP + g) * HEAD_DIM
                o_ref[:, c0:c0 + HEAD_DIM] = og


def _attn_sample(q, kcv, kv_new, win_new, win_state, g_re, cache4, page_table, *, ds):
    nb, npg = page_table.shape
    past = npg * PAGE
    assert ds & (ds - 1) == 0 and ds <= SLC_LEN and past % SLC_LEN == 0 and npg % PAGES_PER_STEP == 0
    steps = npg // PAGES_PER_STEP
    ncp = kcv.shape[3]
    wb = win_state.shape[1]
    n_blocks = past // SLC_LEN + 1
    n_sel_chunks = -(-n_blocks // BLK_PER_CHUNK)
    m = _overlap_matrix(ncp, n_sel_chunks * BLK_PER_CHUNK, ncp - 1, n_blocks)
    ms = np.zeros((ncp, n_sel_chunks, HEAD_DIM), np.float32)
    ms[:, :, :BLK_PER_CHUNK] = m.reshape(ncp, n_sel_chunks, BLK_PER_CHUNK)
    ms = jnp.asarray(ms.reshape(ncp, n_sel_chunks * HEAD_DIM))
    ee = np.zeros((HEAD_DIM, CHUNK_ROWS), np.float32)
    ee[:BLK_PER_CHUNK] = np.arange(CHUNK_ROWS)[None, :] // SLC_LEN == np.arange(BLK_PER_CHUNK)[:, None]
    ee = jnp.asarray(ee, dtype=BF16)
    rq = GROUP * ds

    grid_spec = pltpu.PrefetchScalarGridSpec(
        num_scalar_prefetch=1,
        grid=(nb, steps),
        in_specs=[_page_spec(n, 1) for n in range(PAGES_PER_STEP)] + [
            pl.BlockSpec((ds, N_HEADS * HEAD_DIM), lambda b, p, pt: (b, 0)),
            pl.BlockSpec((None, 2, N_KV, ncp, HEAD_DIM), lambda b, p, pt: (b, 0, 0, 0, 0)),
            pl.BlockSpec((ds, 4 * KV_W), lambda b, p, pt: (b, 0)),
            pl.BlockSpec((ds, 2 * KV_W), lambda b, p, pt: (b, 0)),
            pl.BlockSpec((None, wb, 8, HEAD_DIM), lambda b, p, pt: (b, 0, 0, 0)),
            pl.BlockSpec((None, N_KV, ds, 4 * GROUP), lambda b, p, pt: (b, 0, 0, 0)),
            pl.BlockSpec(ms.shape, lambda b, p, pt: (0, 0)),
            pl.BlockSpec(ee.shape, lambda b, p, pt: (0, 0))],
        out_specs=pl.BlockSpec((ds, N_HEADS * HEAD_DIM), lambda b, p, pt: (b, 0)),
        scratch_shapes=[pltpu.VMEM((N_KV, rq, 1), F32), pltpu.VMEM((N_KV, rq, 1), F32),
                        pltpu.VMEM((N_KV, rq, HEAD_DIM), F32),
                        pltpu.VMEM((N_KV, n_sel_chunks, ds, HEAD_DIM), F32),
                        pltpu.VMEM((N_KV, rq, HEAD_DIM), F32)],
    )
    return pl.pallas_call(
        functools.partial(_attn_sample_kernel, past=past, ds=ds),
        grid_spec=grid_spec,
        out_shape=jax.ShapeDtypeStruct((nb * ds, N_HEADS * HEAD_DIM), F32),
        compiler_params=_cparams(("arbitrary", "arbitrary")),
        name="attn_sample",
    )(page_table, *([cache4] * PAGES_PER_STEP), q, kcv, kv_new, win_new, win_state, g_re, ms, ee)


def _split_w_in(w_in, tc):
    d = D_MODEL
    o = 0
    parts = {}
    for name, width in (("q", N_HEADS * HEAD_DIM), ("kv", 4 * KV_W), ("win", 2 * KV_W), ("g", 3 * N_HEADS),
                        ("u", d), ("b", d), ("c", d), ("ma", d), ("mc", d)):
        parts[name] = w_in[:, o:o + width]
        o += width
    nj = d // tc
    ubc = jnp.stack([parts[n].reshape(d, nj, tc) for n in ("u", "b", "c")], axis=2).reshape(d, 3 * d)
    g_pad = jnp.pad(parts["g"], ((0, 0), (0, HEAD_DIM - 3 * N_HEADS)))
    gates = jnp.concatenate([parts["ma"], parts["mc"]], axis=1)
    return {"q": parts["q"].astype(BF16), "kv": parts["kv"].astype(BF16), "win": parts["win"].astype(BF16),
            "g": g_pad.astype(BF16), "ubc": ubc.astype(BF16), "gates": gates.astype(BF16)}


def _gate_layout(g_logits, nb, seq):
    g = g_logits[:, :3 * N_HEADS].reshape(nb, seq, 3, N_KV, GROUP)
    g = g.transpose(0, 3, 1, 2, 4).reshape(nb, N_KV, seq, 3 * GROUP)
    return jnp.pad(g, ((0, 0), (0, 0), (0, 0), (0, GROUP)))


def _cmp_weights(w_k1, w_k2, pe_k, w_v1, w_v2, pe_v):
    def pairs(w1):
        w = w1.reshape(2, CMP_STRIDE, HEAD_DIM, HEAD_DIM).transpose(1, 2, 0, 3)
        return w.reshape(CMP_STRIDE // 2, 2 * HEAD_DIM, 2 * HEAD_DIM)
    w1pairs = jnp.stack([pairs(w_k1), pairs(w_v1)]).astype(BF16)
    pe8 = jnp.stack([jnp.broadcast_to(pe_k.reshape(1, -1), (8, CMP_LEN * HEAD_DIM)),
                     jnp.broadcast_to(pe_v.reshape(1, -1), (8, CMP_LEN * HEAD_DIM))])
    return w1pairs, pe8, jnp.stack([w_k1, w_v1]), jnp.stack([w_k2, w_v2])


def kernel(x_prompt, x_sample, c_prompt, c_sample, cache_nsa_kv, page_table, state_win_kv, state_conv, w_ada, b_ada, norm1_g, norm2_g, w_in, w_cmp_k1, w_cmp_k2, pe_cmp_k, w_cmp_v1, w_cmp_v2, pe_cmp_v, conv_w, conv_b, w_attn_proj, w_conv_proj, w_out, w_mlp1, w_mlp2, normf_g):
    d = D_MODEL
    nbp, seq, _ = x_prompt.shape
    nbs, ds, _ = x_sample.shape
    tp = nbp * seq
    ts = nbs * ds
    tc = 256
    depth = w_in.shape[0]
    assert depth == 1

    xp = x_prompt.reshape(tp, d)
    xs = x_sample.reshape(ts, d)
    normf = normf_g.reshape(1, d)

    l = 0
    c_all = jnp.concatenate([c_prompt, c_sample, jnp.zeros((16 - nbp - nbs, d), F32)], axis=0)
    ada = _ada(c_all, w_ada[l], b_ada[l].reshape(1, -1)).reshape(16, 6, d)
    ada_p = [ada[:nbp, k][:, None, :] for k in range(6)]
    ada_s = [jnp.repeat(ada[nbp:nbp + nbs, k], ds, axis=0)[None] for k in range(6)]

    w = _split_w_in(w_in[l], tc)
    wa = w_attn_proj[l].astype(BF16)
    wc = w_conv_proj[l].astype(BF16)
    wo = w_out[l].astype(BF16)
    w1 = w_mlp1[l].astype(BF16)
    w2 = w_mlp2[l].astype(BF16)
    g1n = norm1_g[l].reshape(1, d)
    g2n = norm2_g[l].reshape(1, d)
    cw = conv_w[l]
    cb = conv_b[l].reshape(1, d)
    w1pairs, pe8, w1s, w2s = _cmp_weights(w_cmp_k1[l], w_cmp_k2[l], pe_cmp_k[l],
                                          w_cmp_v1[l], w_cmp_v2[l], pe_cmp_v[l])

    tm = 512
    hp = _norm_mod(xp, g1n, ada_p[1], ada_p[0], tm=256, seq=seq)
    q_p = _mm(hp, w["q"], tm=tm, tn=512, out_dtype=BF16, name="mm_q")
    wbp = min(WINDOW, seq)
    assert wbp == tm
    kv6_p, kvb_p = _mm_kv(hp, w["kv"], tm=tm)
    winb_p, win6_p = _mm_win(hp, w["win"], tm=tm, seq=seq)
    g_p = _mm(hp, w["g"], tm=tm, tn=HEAD_DIM, out_dtype=F32, name="mm_g")
    gates_p = _mm(hp, w["gates"], tm=tm, tn=512, out_dtype=BF16, act="sigmoid", name="mm_gates")
    z_p, conv_tiles = _mm_conv(hp, w["ubc"], cw, cb, tm=tm, tc=tc, seq=seq)

    pt_p = jnp.arange(tp // PAGE, dtype=jnp.int32).reshape(nbp, seq // PAGE)
    part_p = _cmp_part(kv6_p.reshape(tp // PAGE, PAGE, 16, HEAD_DIM), pt_p, w1pairs)
    kcv_p = _cmp_finish(part_p, pe8, w1s, w2s)
    o_p = _attn_prompt(q_p, kcv_p, kvb_p, winb_p, _gate_layout(g_p, nbp, seq), nb=nbp, seq=seq)

    mixed_p = _mix(o_p, z_p, wa, wc, gates_p, tm=tm, tn=512)
    x1_p = _mm_resid(mixed_p, wo, xp, ada_p[2], tm=tm, tn=512, seq=seq)
    h2_p = _norm_mod(x1_p, g2n, ada_p[4], ada_p[3], tm=256, seq=seq)
    y_p = _mlp_final(h2_p, w1, w2, x1_p, ada_p[5], normf, tm=tm, tf=512, seq=seq)

    hs = _norm_mod(xs, g1n, ada_s[1], ada_s[0], tm=ts, seq=ds)
    q_s = _mm(hs, w["q"], tm=ts, tn=512, out_dtype=F32, name="mm_q_s")
    kv_s = _mm(hs, w["kv"], tm=ts, tn=512, out_dtype=F32, name="mm_kv_s")
    win_s = _mm(hs, w["win"], tm=ts, tn=512, out_dtype=F32, name="mm_win_s")
    g_s = _mm(hs, w["g"], tm=ts, tn=HEAD_DIM, out_dtype=F32, name="mm_g_s")
    gates_s = _mm(hs, w["gates"], tm=ts, tn=512, out_dtype=F32, act="sigmoid", name="mm_gates_s")
    ubc_s = _mm(hs, w["ubc"], tm=ts, tn=3 * tc, out_dtype=F32, name="mm_ubc_s")
    st = state_conv[l]
    z_s, cu_s = _sample_conv(ubc_s, jnp.repeat(st[:, 0], ds, axis=0), jnp.repeat(st[:, 1], ds, axis=0),
                             cw, cb, tc=tc, seq=ds)

    cache4 = cache_nsa_kv[l].reshape(cache_nsa_kv.shape[1], PAGE, 16, HEAD_DIM)
    part_s = _cmp_part(cache4, page_table, w1pairs)
    kcv_s = _cmp_finish(part_s, pe8, w1s, w2s)
    wbs = state_win_kv.shape[2]
    wst = state_win_kv[l].reshape(nbs, wbs, 8, HEAD_DIM)
    o_s = _attn_sample(q_s, kcv_s, kv_s, win_s, wst, _gate_layout(g_s, nbs, ds), cache4, page_table, ds=ds)

    mixed_s = _mix(o_s, z_s, wa, wc, gates_s, tm=ts, tn=512)
    x1_s = _mm_resid(mixed_s, wo, xs, ada_s[2], tm=ts, tn=512, seq=ds)
    h2_s = _norm_mod(x1_s, g2n, ada_s[4], ada_s[3], tm=ts, seq=ds)
    y_s = _mlp_final(h2_s, w1, w2, x1_s, ada_s[5], normf, tm=ts, tf=512, seq=ds)

    kv_prompt = kv6_p.reshape(1, nbp, seq, 4, N_KV, HEAD_DIM)
    kv_sample = kv_s.reshape(1, nbs, ds, 4, N_KV, HEAD_DIM)
    win_prompt = win6_p.reshape(1, nbp, wbp, 2, N_KV, HEAD_DIM)
    win_sample = jnp.concatenate([state_win_kv[l][:, ds:], win_s.reshape(nbs, ds, 2, N_KV, HEAD_DIM)], axis=1)[None]
    tiles_per_seq = seq // tm
    conv_prompt = conv_tiles[tiles_per_seq - 1::tiles_per_seq][None]
    conv_sample = cu_s.reshape(nbs, ds, d)[None, :, ds - (CONV_W - 1):]
    return (y_p.reshape(nbp, seq, d), y_s.reshape(nbs, ds, d), kv_prompt, kv_sample,
            win_prompt, win_sample, conv_prompt, conv_sample)
```

```python
import functools

import numpy as np
import jax
import jax.numpy as jnp
from jax import lax
from jax.experimental import pallas as pl
from jax.experimental.pallas import tpu as pltpu

F32 = jnp.float32
BF16 = jnp.bfloat16

D_MODEL = 2048
HEAD_DIM = 128
N_HEADS = 16
N_KV = 4
GROUP = 4
KV_W = N_KV * HEAD_DIM
CMP_LEN = 32
CMP_STRIDE = 16
SLC_LEN = 64
SLC_TOP = 16
N_LOCAL = 2
WINDOW = 512
CONV_W = 3
RMS_EPS = 1e-6
NEG = -1e30
SCALE = HEAD_DIM ** -0.5
PAGE = 128
PAGES_PER_STEP = 16
CHUNK_ROWS = PAGES_PER_STEP * PAGE
BLK_PER_CHUNK = CHUNK_ROWS // SLC_LEN
VMEM_LIMIT = 56 * 1024 * 1024

_NT = (((1,), (1,)), ((), ()))
_TN = (((0,), (0,)), ((), ()))


def _cparams(sem):
    return pltpu.CompilerParams(dimension_semantics=sem, vmem_limit_bytes=VMEM_LIMIT)


def _ada_kernel(c_ref, w_ref, b_ref, o_ref):
    c = c_ref[...]
    a = (c * jax.nn.sigmoid(c)).astype(BF16)
    o_ref[...] = jnp.dot(a, w_ref[...].astype(BF16), preferred_element_type=F32) + b_ref[...]


def _ada(c, w, b, tn=1024):
    m, k = c.shape
    n = w.shape[1]
    return pl.pallas_call(
        _ada_kernel,
        grid=(n // tn,),
        in_specs=[pl.BlockSpec((m, k), lambda j: (0, 0)),
                  pl.BlockSpec((k, tn), lambda j: (0, j)),
                  pl.BlockSpec((1, tn), lambda j: (0, j))],
        out_specs=pl.BlockSpec((m, tn), lambda j: (0, j)),
        out_shape=jax.ShapeDtypeStruct((m, n), F32),
        compiler_params=_cparams(("arbitrary",)),
        name="ada",
    )(c, w, b)


def _rowspec(p, tm, tn, seq, col=True):
    gr = p.shape[1]
    if gr == 1:
        if col:
            return pl.BlockSpec((1, 1, tn), lambda i, j: ((i * tm) // seq, 0, j))
        return pl.BlockSpec((1, 1, tn), lambda i: ((i * tm) // seq, 0, 0))
    if col:
        return pl.BlockSpec((1, tm, tn), lambda i, j: (0, i, j))
    return pl.BlockSpec((1, tm, tn), lambda i: (0, i, 0))


def _norm_kernel(x_ref, g_ref, sc_ref, sh_ref, o_ref):
    x = x_ref[...]
    r = lax.rsqrt(jnp.mean(x * x, axis=-1, keepdims=True) + RMS_EPS)
    y = (x * r) * g_ref[...]
    o_ref[...] = (y * (1.0 + sc_ref[0]) + sh_ref[0]).astype(o_ref.dtype)


def _norm_mod(x, g, sc, sh, *, tm, seq):
    t, d = x.shape
    return pl.pallas_call(
        _norm_kernel,
        grid=(t // tm,),
        in_specs=[pl.BlockSpec((tm, d), lambda i: (i, 0)),
                  pl.BlockSpec((1, d), lambda i: (0, 0)),
                  _rowspec(sc, tm, d, seq, col=False),
                  _rowspec(sh, tm, d, seq, col=False)],
        out_specs=pl.BlockSpec((tm, d), lambda i: (i, 0)),
        out_shape=jax.ShapeDtypeStruct((t, d), BF16),
        compiler_params=_cparams(("arbitrary",)),
        name="norm_mod",
    )(x, g, sc, sh)


def _mm_kernel(x_ref, w_ref, o_ref, *, act):
    acc = jnp.dot(x_ref[...].astype(BF16), w_ref[...], preferred_element_type=F32)
    if act == "sigmoid":
        acc = jax.nn.sigmoid(acc)
    o_ref[...] = acc.astype(o_ref.dtype)


def _mm(x, w, *, tm, tn, out_dtype, act=None, name="mm"):
    t, k = x.shape
    n = w.shape[1]
    return pl.pallas_call(
        functools.partial(_mm_kernel, act=act),
        grid=(t // tm, n // tn),
        in_specs=[pl.BlockSpec((tm, k), lambda i, j: (i, 0)),
                  pl.BlockSpec((k, tn), lambda i, j: (0, j))],
        out_specs=pl.BlockSpec((tm, tn), lambda i, j: (i, j)),
        out_shape=jax.ShapeDtypeStruct((t, n), out_dtype),
        compiler_params=_cparams(("arbitrary", "arbitrary")),
        name=name,
    )(x, w)


def _store_head_major(o_ref, acc, tm):
    flat = o_ref.reshape(tm * 8, HEAD_DIM)
    for cb in range(8):
        flat[pl.ds(cb, tm, stride=8), :] = acc[:, cb * HEAD_DIM:(cb + 1) * HEAD_DIM]


def _mm_kv_kernel(x_ref, w_ref, o6_ref, ob_ref, *, tm):
    acc = jnp.dot(x_ref[...], w_ref[...], preferred_element_type=F32)
    _store_head_major(o6_ref, acc, tm)

    @pl.when(pl.program_id(1) == 1)
    def _():
        ob_ref[...] = acc.astype(BF16)


def _mm_kv(h, w, *, tm):
    t, k = h.shape
    tn = 2 * KV_W
    return pl.pallas_call(
        functools.partial(_mm_kv_kernel, tm=tm),
        grid=(t // tm, 2),
        in_specs=[pl.BlockSpec((tm, k), lambda i, j: (i, 0)),
                  pl.BlockSpec((k, tn), lambda i, j: (0, j))],
        out_specs=[pl.BlockSpec((tm, 8, HEAD_DIM), lambda i, j: (i, j, 0)),
                   pl.BlockSpec((tm, tn), lambda i, j: (i, 0))],
        out_shape=[jax.ShapeDtypeStruct((t, 16, HEAD_DIM), F32),
                   jax.ShapeDtypeStruct((t, tn), BF16)],
        compiler_params=_cparams(("arbitrary", "arbitrary")),
        name="mm_kv",
    )(h, w)


def _mm_win_kernel(x_ref, w_ref, ob_ref, o6_ref, *, tm, seq):
    acc = jnp.dot(x_ref[...], w_ref[...], preferred_element_type=F32)
    ob_ref[...] = acc.astype(BF16)

    @pl.when(((pl.program_id(0) + 1) * tm) % seq == 0)
    def _():
        _store_head_major(o6_ref, acc, tm)


def _mm_win(h, w, *, tm, seq):
    t, k = h.shape
    tn = 2 * KV_W
    return pl.pallas_call(
        functools.partial(_mm_win_kernel, tm=tm, seq=seq),
        grid=(t // tm,),
        in_specs=[pl.BlockSpec((tm, k), lambda i: (i, 0)),
                  pl.BlockSpec((k, tn), lambda i: (0, 0))],
        out_specs=[pl.BlockSpec((tm, tn), lambda i: (i, 0)),
                   pl.BlockSpec((tm, 8, HEAD_DIM), lambda i: ((i * tm) // seq, 0, 0))],
        out_shape=[jax.ShapeDtypeStruct((t, tn), BF16),
                   jax.ShapeDtypeStruct((t // seq * tm, 8, HEAD_DIM), F32)],
        compiler_params=_cparams(("arbitrary",)),
        name="mm_win",
    )(h, w)


def _conv_taps(cu, prev1, prev2, bgate, cw_ref, cb_ref):
    y = cb_ref[...] + prev2 * cw_ref[0:1, :]
    y = y + prev1 * cw_ref[1:2, :]
    y = y + cu * cw_ref[2:3, :]
    return bgate * y


def _mm_conv_kernel(x_ref, w_ref, cw_ref, cb_ref, z_ref, st_ref, carry_ref, *, tm, tc, seq):
    i = pl.program_id(0)
    j = pl.program_id(1)
    acc = jnp.dot(x_ref[...], w_ref[...], preferred_element_type=F32)
    u = acc[:, 0:tc]
    bgate = acc[:, tc:2 * tc]
    cu = acc[:, 2 * tc:3 * tc] * u
    @pl.when((i * tm) % seq == 0)
    def _():
        carry_ref[j] = jnp.zeros((8, tc), F32)

    car = carry_ref[j]
    p0 = car[0:1, :]
    p1 = car[1:2, :]
    rows = lax.broadcasted_iota(jnp.int32, (tm, tc), 0)
    r1 = pltpu.roll(cu, 1, 0)
    r2 = pltpu.roll(cu, 2, 0)
    prev1 = jnp.where(rows == 0, p1, r1)
    prev2 = jnp.where(rows == 0, p0, jnp.where(rows == 1, p1, r2))
    z_ref[...] = _conv_taps(cu, prev1, prev2, bgate, cw_ref, cb_ref).astype(z_ref.dtype)
    tail = cu[tm - 8:tm, :]
    carry_ref[j] = jnp.concatenate([tail[6:8, :], tail[0:6, :]], axis=0)
    st_ref[0] = tail[6:8, :]


def _mm_conv(h, w_il, conv_w, conv_b, *, tm, tc, seq):
    t, k = h.shape
    d = w_il.shape[1] // 3
    nj = d // tc
    return pl.pallas_call(
        functools.partial(_mm_conv_kernel, tm=tm, tc=tc, seq=seq),
        grid=(t // tm, nj),
        in_specs=[pl.BlockSpec((tm, k), lambda i, j: (i, 0)),
                  pl.BlockSpec((k, 3 * tc), lambda i, j: (0, j)),
                  pl.BlockSpec((CONV_W, tc), lambda i, j: (0, j)),
                  pl.BlockSpec((1, tc), lambda i, j: (0, j))],
        out_specs=[pl.BlockSpec((tm, tc), lambda i, j: (i, j)),
                   pl.BlockSpec((1, CONV_W - 1, tc), lambda i, j: (i, 0, j))],
        out_shape=[jax.ShapeDtypeStruct((t, d), BF16),
                   jax.ShapeDtypeStruct((t // tm, CONV_W - 1, d), F32)],
        scratch_shapes=[pltpu.VMEM((nj, 8, tc), F32)],
        compiler_params=_cparams(("arbitrary", "arbitrary")),
        name="mm_conv",
    )(h, w_il, conv_w, conv_b)


def _sample_conv_kernel(a_ref, st0_ref, st1_ref, cw_ref, cb_ref, z_ref, cu_ref, *, tc, seq):
    acc = a_ref[...]
    t = acc.shape[0]
    u = acc[:, 0:tc]
    bgate = acc[:, tc:2 * tc]
    cu = acc[:, 2 * tc:3 * tc] * u
    s = lax.rem(lax.broadcasted_iota(jnp.int32, (t, tc), 0), jnp.int32(seq))
    r1 = pltpu.roll(cu, 1, 0)
    r2 = pltpu.roll(cu, 2, 0)
    prev1 = jnp.where(s == 0, st1_ref[...], r1)
    prev2 = jnp.where(s == 0, st0_ref[...], jnp.where(s == 1, st1_ref[...], r2))
    z_ref[...] = _conv_taps(cu, prev1, prev2, bgate, cw_ref, cb_ref)
    cu_ref[...] = cu


def _sample_conv(ubc, st0, st1, conv_w, conv_b, *, tc, seq):
    t = ubc.shape[0]
    d = ubc.shape[1] // 3
    return pl.pallas_call(
        functools.partial(_sample_conv_kernel, tc=tc, seq=seq),
        grid=(d // tc,),
        in_specs=[pl.BlockSpec((t, 3 * tc), lambda j: (0, j)),
                  pl.BlockSpec((t, tc), lambda j: (0, j)),
                  pl.BlockSpec((t, tc), lambda j: (0, j)),
                  pl.BlockSpec((CONV_W, tc), lambda j: (0, j)),
                  pl.BlockSpec((1, tc), lambda j: (0, j))],
        out_specs=[pl.BlockSpec((t, tc), lambda j: (0, j)),
                   pl.BlockSpec((t, tc), lambda j: (0, j))],
        out_shape=[jax.ShapeDtypeStruct((t, d), F32), jax.ShapeDtypeStruct((t, d), F32)],
        compiler_params=_cparams(("arbitrary",)),
        name="sample_conv",
    )(ubc, st0, st1, conv_w, conv_b)


def _mix_kernel(o_ref, z_ref, wa_ref, wc_ref, ga_ref, gc_ref, m_ref):
    a = jnp.dot(o_ref[...].astype(BF16), wa_ref[...], preferred_element_type=F32)
    c = jnp.dot(z_ref[...].astype(BF16), wc_ref[...], preferred_element_type=F32)
    m_ref[...] = (ga_ref[...].astype(F32) * a + gc_ref[...].astype(F32) * c).astype(m_ref.dtype)


def _mix(o, z, wa, wc, gates, *, tm, tn):
    t, k = o.shape
    n = wa.shape[1]
    nj = n // tn
    return pl.pallas_call(
        _mix_kernel,
        grid=(t // tm, nj),
        in_specs=[pl.BlockSpec((tm, k), lambda i, j: (i, 0)),
                  pl.BlockSpec((tm, k), lambda i, j: (i, 0)),
                  pl.BlockSpec((k, tn), lambda i, j: (0, j)),
                  pl.BlockSpec((k, tn), lambda i, j: (0, j)),
                  pl.BlockSpec((tm, tn), lambda i, j: (i, j)),
                  pl.BlockSpec((tm, tn), lambda i, j: (i, j + nj))],
        out_specs=pl.BlockSpec((tm, tn), lambda i, j: (i, j)),
        out_shape=jax.ShapeDtypeStruct((t, n), BF16),
        compiler_params=_cparams(("arbitrary", "arbitrary")),
        name="mix",
    )(o, z, wa, wc, gates, gates)


def _resid_kernel(a_ref, w_ref, x_ref, g_ref, o_ref):
    acc = jnp.dot(a_ref[...], w_ref[...], preferred_element_type=F32)
    o_ref[...] = x_ref[...] + g_ref[0] * acc


def _mm_resid(a, w, x, gate, *, tm, tn, seq):
    t, k = a.shape
    n = w.shape[1]
    return pl.pallas_call(
        _resid_kernel,
        grid=(t // tm, n // tn),
        in_specs=[pl.BlockSpec((tm, k), lambda i, j: (i, 0)),
                  pl.BlockSpec((k, tn), lambda i, j: (0, j)),
                  pl.BlockSpec((tm, tn), lambda i, j: (i, j)),
                  _rowspec(gate, tm, tn, seq)],
        out_specs=pl.BlockSpec((tm, tn), lambda i, j: (i, j)),
        out_shape=jax.ShapeDtypeStruct((t, n), F32),
        compiler_params=_cparams(("arbitrary", "arbitrary")),
        name="mm_resid",
    )(a, w, x, gate)


def _mlp_kernel(h_ref, w1_ref, w2_ref, x_ref, g_ref, nf_ref, o_ref, acc_ref):
    k = pl.program_id(1)

    @pl.when(k == 0)
    def _():
        acc_ref[...] = jnp.zeros_like(acc_ref)

    a = jnp.dot(h_ref[...], w1_ref[...], preferred_element_type=F32)
    a = jnp.square(jnp.maximum(a, 0.0)).astype(BF16)
    acc_ref[...] += jnp.dot(a, w2_ref[...], preferred_element_type=F32)

    @pl.when(k == pl.num_programs(1) - 1)
    def _():
        x2 = x_ref[...] + g_ref[0] * acc_ref[...]
        r = lax.rsqrt(jnp.mean(x2 * x2, axis=-1, keepdims=True) + RMS_EPS)
        o_ref[...] = (x2 * r) * nf_ref[...]


def _mlp_final(h2, w1, w2, x1, gate, normf, *, tm, tf, seq):
    t, d = h2.shape
    f = w1.shape[1]
    return pl.pallas_call(
        _mlp_kernel,
        grid=(t // tm, f // tf),
        in_specs=[pl.BlockSpec((tm, d), lambda i, k: (i, 0)),
                  pl.BlockSpec((d, tf), lambda i, k: (0, k)),
                  pl.BlockSpec((tf, d), lambda i, k: (k, 0)),
                  pl.BlockSpec((tm, d), lambda i, k: (i, 0)),
                  _rowspec_k(gate, tm, d, seq),
                  pl.BlockSpec((1, d), lambda i, k: (0, 0))],
        out_specs=pl.BlockSpec((tm, d), lambda i, k: (i, 0)),
        out_shape=jax.ShapeDtypeStruct((t, d), F32),
        scratch_shapes=[pltpu.VMEM((tm, d), F32)],
        compiler_params=_cparams(("arbitrary", "arbitrary")),
        name="mlp",
    )(h2, w1, w2, x1, gate, normf)


def _rowspec_k(p, tm, d, seq):
    if p.shape[1] == 1:
        return pl.BlockSpec((1, 1, d), lambda i, k: ((i * tm) // seq, 0, 0))
    return pl.BlockSpec((1, tm, d), lambda i, k: (0, i, 0))


def _cmp_part_kernel(pt_ref, *refs):
    pages = refs[:PAGES_PER_STEP]
    w_ref = refs[PAGES_PER_STEP]
    o_ref = refs[PAGES_PER_STEP + 1]
    nchunk = CHUNK_ROWS // CMP_STRIDE
    cpp = PAGE // CMP_STRIDE
    flat = [pg.reshape(PAGE * 8, HEAD_DIM) for pg in pages]
    for kind in range(2):
        acc = None
        for rp in range(CMP_STRIDE // 2):
            halves = []
            for r in (2 * rp, 2 * rp + 1):
                halves.append(jnp.concatenate(
                    [fp[pl.ds(r * 8 + kind * N_KV + h, cpp, stride=CMP_STRIDE * 8), :]
                     for h in range(N_KV) for fp in flat], axis=0))
            lhs = jnp.concatenate(halves, axis=1).astype(BF16)
            d = jnp.dot(lhs, w_ref[kind, rp], preferred_element_type=F32)
            acc = d if acc is None else acc + d
        for h in range(N_KV):
            o_ref[kind, h] = acc[h * nchunk:(h + 1) * nchunk, :]


def _page_spec(n, kind_pair):
    return pl.BlockSpec((None, PAGE, 8, HEAD_DIM),
                        lambda b, p, pt, n=n: (pt[b, p * PAGES_PER_STEP + n], 0, kind_pair, 0))


def _cmp_part(pages4, page_table, w1pairs):
    nb, npg = page_table.shape
    steps = npg // PAGES_PER_STEP
    nchunk = PAGES_PER_STEP * (PAGE // CMP_STRIDE)
    grid_spec = pltpu.PrefetchScalarGridSpec(
        num_scalar_prefetch=1,
        grid=(nb, steps),
        in_specs=[_page_spec(n, 0) for n in range(PAGES_PER_STEP)]
        + [pl.BlockSpec((2, CMP_STRIDE // 2, 2 * HEAD_DIM, 2 * HEAD_DIM), lambda b, p, pt: (0, 0, 0, 0))],
        out_specs=pl.BlockSpec((None, 2, N_KV, nchunk, 2 * HEAD_DIM), lambda b, p, pt: (b, 0, 0, p, 0)),
    )
    return pl.pallas_call(
        _cmp_part_kernel,
        grid_spec=grid_spec,
        out_shape=jax.ShapeDtypeStruct((nb, 2, N_KV, steps * nchunk, 2 * HEAD_DIM), F32),
        compiler_params=_cparams(("arbitrary", "arbitrary")),
        name="cmp_part",
    )(page_table, *([pages4] * PAGES_PER_STEP), w1pairs)


def _gelu_tanh(x):
    c = np.sqrt(2.0 / np.pi).astype(np.float32)
    return 0.5 * x * (1.0 + jnp.tanh(c * (x + 0.044715 * (x * x * x))))


def _cmp_finish_kernel(part_ref, pe_ref, w1_ref, w2_ref, o_ref, bias_ref):
    @pl.when((pl.program_id(1) == 0) & (pl.program_id(2) == 0))
    def _():
        bias_ref[...] = jnp.dot(pe_ref[...], w1_ref[...], preferred_element_type=F32,
                                precision=lax.Precision.HIGHEST)

    part = part_ref[...]
    n = part.shape[0]
    h = part[:, 0:HEAD_DIM] + pltpu.roll(part[:, HEAD_DIM:], n - 1, 0)
    h = h + bias_ref[0:1, :]
    o_ref[...] = jnp.dot(_gelu_tanh(h).astype(BF16), w2_ref[...].astype(BF16),
                         preferred_element_type=F32).astype(o_ref.dtype)


def _cmp_finish(part, pe8, w1, w2):
    nb, _, _, nchunk, _ = part.shape
    return pl.pallas_call(
        _cmp_finish_kernel,
        grid=(2, nb, N_KV),
        in_specs=[pl.BlockSpec((None, None, None, nchunk, 2 * HEAD_DIM), lambda k, b, h: (b, k, h, 0, 0)),
                  pl.BlockSpec((None, 8, CMP_LEN * HEAD_DIM), lambda k, b, h: (k, 0, 0)),
                  pl.BlockSpec((None, CMP_LEN * HEAD_DIM, HEAD_DIM), lambda k, b, h: (k, 0, 0)),
                  pl.BlockSpec((None, HEAD_DIM, HEAD_DIM), lambda k, b, h: (k, 0, 0))],
        out_specs=pl.BlockSpec((None, None, None, nchunk, HEAD_DIM), lambda k, b, h: (b, k, h, 0, 0)),
        out_shape=jax.ShapeDtypeStruct((nb, 2, N_KV, nchunk, HEAD_DIM), BF16),
        scratch_shapes=[pltpu.VMEM((8, HEAD_DIM), F32)],
        compiler_params=_cparams(("arbitrary", "arbitrary", "arbitrary")),
        name="cmp_finish",
    )(part, pe8, w1, w2)


def _select_topk(score, valid, forced, blk, axis):
    sc = jnp.where(valid, jnp.where(forced, jnp.inf, score), -jnp.inf)
    sel = jnp.zeros(score.shape, F32)
    big = jnp.int32(1 << 20)
    for _ in range(SLC_TOP):
        mx = jnp.max(sc, axis=axis, keepdims=True)
        idx = jnp.min(jnp.where(sc == mx, blk, big), axis=axis, keepdims=True)
        hit = blk == idx
        sel = jnp.where(hit & (mx > -jnp.inf), 1.0, sel)
        sc = jnp.where(hit, -jnp.inf, sc)
    return sel


def _overlap_matrix(nc_pad, n_blocks_pad, nc, n_blocks):
    cs = np.arange(nc_pad) * CMP_STRIDE
    sb = np.arange(n_blocks_pad) * SLC_LEN
    ov = np.clip(np.minimum(cs[:, None] + CMP_LEN, sb[None, :] + SLC_LEN)
                 - np.maximum(cs[:, None], sb[None, :]), 0, None)
    m = (ov / CMP_STRIDE).astype(np.float32)
    m[nc:, :] = 0.0
    m[:, n_blocks:] = 0.0
    return m


def _attn_prompt_kernel(q_ref, kcb, vcb, ksb, vsb, kwb, vwb, g_ref, mt_ref, e_ref, o_ref,
                        m_scr, l_scr, acc_scr, bias_scr, gate_scr, *, tq, seq, ck):
    qi = pl.program_id(2)
    nc = seq // CMP_STRIDE - 1
    nsb = seq // SLC_LEN
    rb = min(tq, 128)

    q = q_ref[...]
    qq = jnp.concatenate([q[:, g * HEAD_DIM:(g + 1) * HEAD_DIM] for g in range(GROUP)], axis=0)
    rq = GROUP * tq
    q0 = qi * tq

    gt = jax.nn.sigmoid(g_ref[...])
    for c in range(3 * GROUP):
        gate_scr[c] = jnp.broadcast_to(gt[:, c:c + 1], (tq, HEAD_DIM))

    ncp = kcb.shape[0]
    s = lax.dot_general(qq, kcb[...], _NT, preferred_element_type=F32) * SCALE
    qpos = q0 + (lax.broadcasted_iota(jnp.int32, (rq, ncp), 0) & (tq - 1))
    col = lax.broadcasted_iota(jnp.int32, (rq, ncp), 1)
    vis = (col * CMP_STRIDE + (CMP_LEN - 1) <= qpos) & (col < nc)
    s = jnp.where(vis, s, NEG)
    mx = jnp.max(s, axis=1, keepdims=True)
    e = jnp.where(vis, jnp.exp(s - mx), 0.0)
    den = jnp.sum(e, axis=1, keepdims=True)
    p = e / jnp.where(den > 0.0, den, 1.0)
    o_cmp = jnp.dot(p.astype(BF16), vcb[...], preferred_element_type=F32)
    pg = p[0:tq]
    for g in range(1, GROUP):
        pg = pg + p[g * tq:(g + 1) * tq]

    sct = lax.dot_general(mt_ref[...], pg, _NT, preferred_element_type=F32,
                          precision=lax.Precision.HIGHEST)
    blk = lax.broadcasted_iota(jnp.int32, (nsb, tq), 0)
    cur = lax.div(q0 + lax.broadcasted_iota(jnp.int32, (nsb, tq), 1), jnp.int32(SLC_LEN))
    valid = blk <= cur
    forced = (blk == 0) | (blk > cur - N_LOCAL)
    need_topk = (q0 + tq - 1) // SLC_LEN + 1 > SLC_TOP
    sel_t = lax.cond(need_topk,
                     lambda: _select_topk(sct, valid, forced, blk, 0),
                     lambda: valid.astype(F32)).astype(BF16)

    def sweep(k_ref, v_ref, k_start, n_chunks, bias_fn):
        m_scr[...] = jnp.full((rq, HEAD_DIM), NEG, F32)
        l_scr[...] = jnp.zeros((rq, HEAD_DIM), F32)
        acc_scr[...] = jnp.zeros((rq, HEAD_DIM), F32)
        nl = ck // HEAD_DIM

        def scores(c):
            k0 = pl.multiple_of(k_start + c * ck, 128)
            return lax.dot_general(qq, k_ref[pl.ds(k0, ck), :], _NT, preferred_element_type=F32)

        def chunk_inputs(c):
            k0 = pl.multiple_of(k_start + c * ck, 128)
            return scores(c), bias_fn(c, k0)

        c_exp = np.float32(SCALE * np.log2(np.e))

        def reduce_chunk(c, sb):
            sc_, bias = sb
            k0 = pl.multiple_of(k_start + c * ck, 128)
            vch = v_ref[pl.ds(k0, ck), :]
            alphas, ps = [], []
            nblk = rq // rb
            for r in range(nblk):
                rows = slice(r * rb, (r + 1) * rb)
                b0 = (r * rb) % tq
                sg = [sc_[rows, j * HEAD_DIM:(j + 1) * HEAD_DIM]
                      + bias[b0:b0 + rb, j * HEAD_DIM:(j + 1) * HEAD_DIM] for j in range(nl)]
                mx = functools.reduce(jnp.maximum, sg)
                m_old = m_scr[rows]
                m_new = jnp.maximum(m_old, jnp.max(mx, axis=1, keepdims=True))
                alpha = jnp.exp2((m_old - m_new) * c_exp)
                pj = [jnp.exp2((x - m_new) * c_exp) for x in sg]
                psum = functools.reduce(lambda a, b: a + b, pj)
                l_scr[rows] = alpha * l_scr[rows] + jnp.sum(psum, axis=1, keepdims=True)
                m_scr[rows] = m_new
                alphas.append(alpha)
                ps.append(jnp.concatenate([x.astype(BF16) for x in pj], axis=1))
                if (r + 1) % (nblk // 2) == 0:
                    half = slice((r + 1 - nblk // 2) * rb, (r + 1) * rb)
                    pv = jnp.dot(jnp.concatenate(ps, axis=0), vch, preferred_element_type=F32)
                    acc_scr[half] = jnp.concatenate(alphas, axis=0) * acc_scr[half] + pv
                    alphas, ps = [], []

        def body(c, carry):
            reduce_chunk(c, chunk_inputs(c))
            return carry

        lax.fori_loop(0, n_chunks, body, 0)
        return acc_scr[...] / l_scr[...]

    qp = q0 + lax.broadcasted_iota(jnp.int32, (tq, ck), 0)
    kcol = lax.broadcasted_iota(jnp.int32, (tq, ck), 1)

    eye = (lax.broadcasted_iota(jnp.int32, (nsb, HEAD_DIM), 0)
           == lax.broadcasted_iota(jnp.int32, (nsb, HEAD_DIM), 1)).astype(BF16)
    sel_q = lax.dot_general(sel_t, eye, _TN, preferred_element_type=F32).astype(BF16)

    n_slc = (q0 + tq + ck - 1) // ck

    def put_mask(c):
        selq = jnp.dot(sel_q, e_ref[c], preferred_element_type=F32)
        bias_scr[c] = jnp.where((selq > 0.5) & (c * ck + kcol <= qp), 0.0, NEG)

    def mask_body(i, carry):
        put_mask(2 * i)
        put_mask(jnp.minimum(2 * i + 1, seq // ck - 1))
        return carry

    lax.fori_loop(0, (n_slc + 1) // 2, mask_body, 0)
    o_slc = sweep(ksb, vsb, 0, n_slc, lambda c, k0: bias_scr[c])

    span = -(-(WINDOW + tq) // ck) * ck

    def win_bias(c, k0):
        dlt = qp - (k0 + kcol)
        return jnp.where((dlt >= 0) & (dlt < WINDOW), 0.0, NEG)

    o_win = sweep(kwb, vwb, jnp.maximum(q0 + tq - span, 0), span // ck, win_bias)

    for g in range(GROUP):
        r0, r1 = g * tq, (g + 1) * tq
        og = gate_scr[g] * o_cmp[r0:r1] + gate_scr[GROUP + g] * o_slc[r0:r1]
        og = og + gate_scr[2 * GROUP + g] * o_win[r0:r1]
        o_ref[:, g * HEAD_DIM:(g + 1) * HEAD_DIM] = og.astype(o_ref.dtype)


def _attn_prompt(q, kcv, kv, win, g_re, *, nb, seq, tq=256, ck=512):
    assert tq & (tq - 1) == 0 and seq % ck == 0 and WINDOW % tq == 0
    t = q.shape[0]
    ncp = seq // CMP_STRIDE
    nsb = seq // SLC_LEN
    nq = seq // tq
    mt = jnp.asarray(_overlap_matrix(ncp, nsb, ncp - 1, nsb).T)
    kk = np.arange(seq)
    assert nsb <= HEAD_DIM
    e3 = (kk[None, :] // SLC_LEN == np.arange(HEAD_DIM)[:, None]).astype(np.float32)
    e3 = jnp.asarray(e3.reshape(HEAD_DIM, seq // ck, ck).transpose(1, 0, 2), dtype=BF16)
    slab = lambda cb: pl.BlockSpec((seq, HEAD_DIM), lambda b, h, i, cb=cb: (b, cb + h))
    return pl.pallas_call(
        functools.partial(_attn_prompt_kernel, tq=tq, seq=seq, ck=ck),
        grid=(nb, N_KV, nq),
        in_specs=[pl.BlockSpec((tq, GROUP * HEAD_DIM), lambda b, h, i: (b * nq + i, h)),
                  pl.BlockSpec((None, None, None, ncp, HEAD_DIM), lambda b, h, i: (b, 0, h, 0, 0)),
                  pl.BlockSpec((None, None, None, ncp, HEAD_DIM), lambda b, h, i: (b, 1, h, 0, 0)),
                  slab(0), slab(N_KV), slab(0), slab(N_KV),
                  pl.BlockSpec((None, None, tq, 4 * GROUP), lambda b, h, i: (b, h, i, 0)),
                  pl.BlockSpec((nsb, ncp), lambda b, h, i: (0, 0)),
                  pl.BlockSpec((seq // ck, HEAD_DIM, ck), lambda b, h, i: (0, 0, 0))],
        out_specs=pl.BlockSpec((tq, GROUP * HEAD_DIM), lambda b, h, i: (b * nq + i, h)),
        out_shape=jax.ShapeDtypeStruct((t, N_HEADS * HEAD_DIM), BF16),
        scratch_shapes=[pltpu.VMEM((GROUP * tq, HEAD_DIM), F32)] * 3
        + [pltpu.VMEM((seq // ck, tq, ck), F32), pltpu.VMEM((3 * GROUP, tq, HEAD_DIM), F32)],
        compiler_params=_cparams(("arbitrary", "arbitrary", "arbitrary")),
        name="attn_prompt",
    )(q, kcv, kcv, kv, kv, win, win, g_re, mt, e3)


def _attn_sample_kernel(pt_ref, *refs, past, ds):
    pages = refs[:PAGES_PER_STEP]
    (q_ref, kcv_ref, kvn_ref, wn_ref, st_ref, g_ref, ms_ref, e_ref, o_ref,
     m_scr, l_scr, acc_scr, sel_scr, ocmp_scr) = refs[PAGES_PER_STEP:]
    p = pl.program_id(1)
    n_steps = pl.num_programs(1)
    rq = GROUP * ds
    ncp = kcv_ref.shape[2]
    nc = ncp - 1
    n_sel_chunks = sel_scr.shape[1]
    lanes = n_sel_chunks * HEAD_DIM

    def q_rows(h):
        c0 = h * GROUP * HEAD_DIM
        return jnp.concatenate(
            [q_ref[:, c0 + g * HEAD_DIM:c0 + (g + 1) * HEAD_DIM] for g in range(GROUP)], axis=0).astype(BF16)

    def online_update(h, sc_, vals):
        m_i = m_scr[h]
        m_new = jnp.maximum(m_i, jnp.max(sc_, axis=1, keepdims=True))
        alpha = jnp.exp(m_i - m_new)
        pp = jnp.exp(sc_ - m_new)
        l_scr[h] = alpha * l_scr[h] + jnp.sum(pp, axis=1, keepdims=True)
        acc_scr[h] = alpha * acc_scr[h] + jnp.dot(pp.astype(BF16), vals, preferred_element_type=F32)
        m_scr[h] = m_new

    @pl.when(p == 0)
    def _():
        for h in range(N_KV):
            qq = q_rows(h)
            kc = kcv_ref[0, h]
            vc = kcv_ref[1, h]
            s = lax.dot_general(qq, kc, _NT, preferred_element_type=F32) * SCALE
            qpos = past + (lax.broadcasted_iota(jnp.int32, (rq, ncp), 0) & (ds - 1))
            col = lax.broadcasted_iota(jnp.int32, (rq, ncp), 1)
            vis = (col * CMP_STRIDE + (CMP_LEN - 1) <= qpos) & (col < nc)
            s = jnp.where(vis, s, NEG)
            mx = jnp.max(s, axis=1, keepdims=True)
            e = jnp.where(vis, jnp.exp(s - mx), 0.0)
            den = jnp.sum(e, axis=1, keepdims=True)
            pr = e / jnp.where(den > 0.0, den, 1.0)
            ocmp_scr[h] = jnp.dot(pr.astype(BF16), vc, preferred_element_type=F32)
            pg = pr[0:ds]
            for g in range(1, GROUP):
                pg = pg + pr[g * ds:(g + 1) * ds]
            score = jnp.dot(pg, ms_ref[...], preferred_element_type=F32,
                            precision=lax.Precision.HIGHEST)
            lane = lax.broadcasted_iota(jnp.int32, (ds, lanes), 1)
            lane_in = lax.rem(lane, jnp.int32(HEAD_DIM))
            real = lane_in < BLK_PER_CHUNK
            blk = jnp.where(real, lax.div(lane, jnp.int32(HEAD_DIM)) * BLK_PER_CHUNK + lane_in, (1 << 19) + lane)
            cur = lax.div(past + lax.broadcasted_iota(jnp.int32, (ds, lanes), 0), jnp.int32(SLC_LEN))
            valid = real & (blk <= cur)
            forced = (blk == 0) | (blk > cur - N_LOCAL)
            sel = _select_topk(score, valid, forced, blk, 1)
            for c in range(n_sel_chunks):
                sel_scr[h, c] = sel[:, c * HEAD_DIM:(c + 1) * HEAD_DIM]
            m_scr[h] = jnp.full((rq, 1), NEG, F32)
            l_scr[h] = jnp.zeros((rq, 1), F32)
            acc_scr[h] = jnp.zeros((rq, HEAD_DIM), F32)

    k0 = p * CHUNK_ROWS
    flat = [pg_.reshape(PAGE * 8, HEAD_DIM) for pg_ in pages]
    for h in range(N_KV):
        qq = q_rows(h)
        kch = jnp.concatenate([fp[pl.ds(h, PAGE, stride=8), :] for fp in flat], axis=0).astype(BF16)
        vch = jnp.concatenate([fp[pl.ds(N_KV + h, PAGE, stride=8), :] for fp in flat], axis=0).astype(BF16)
        sc_ = lax.dot_general(qq, kch, _NT, preferred_element_type=F32) * SCALE
        selq = jnp.dot(sel_scr[h, p].astype(BF16), e_ref[...], preferred_element_type=F32)
        kpos = k0 + lax.broadcasted_iota(jnp.int32, (ds, CHUNK_ROWS), 1)
        qp = past + lax.broadcasted_iota(jnp.int32, (ds, CHUNK_ROWS), 0)
        okf = jnp.where((selq > 0.5) & (kpos <= qp), 1.0, 0.0)
        ok = jnp.concatenate([okf] * GROUP, axis=0) > 0.5
        online_update(h, jnp.where(ok, sc_, NEG), vch)

    @pl.when(p == n_steps - 1)
    def _():
        gt = jax.nn.sigmoid(g_ref[...])
        zpad = jnp.zeros((HEAD_DIM - ds, HEAD_DIM), F32)
        for h in range(N_KV):
            qq = q_rows(h)
            c_k = 2 * KV_W + h * HEAD_DIM
            c_v = 3 * KV_W + h * HEAD_DIM
            kn = jnp.concatenate([kvn_ref[:, c_k:c_k + HEAD_DIM], zpad], axis=0).astype(BF16)
            vn = jnp.concatenate([kvn_ref[:, c_v:c_v + HEAD_DIM], zpad], axis=0).astype(BF16)
            sn = lax.dot_general(qq, kn, _NT, preferred_element_type=F32) * SCALE
            last_sel = sel_scr[h, n_sel_chunks - 1]
            nb_last = (past // SLC_LEN) % BLK_PER_CHUNK
            seln = jnp.sum(jnp.where(lax.broadcasted_iota(jnp.int32, (ds, HEAD_DIM), 1) == nb_last,
                                     last_sel, 0.0), axis=1, keepdims=True)
            seln = jnp.concatenate([seln] * GROUP, axis=0)
            srow = lax.broadcasted_iota(jnp.int32, (rq, HEAD_DIM), 0) & (ds - 1)
            kcol = lax.broadcasted_iota(jnp.int32, (rq, HEAD_DIM), 1)
            okn = (seln > 0.5) & (kcol <= srow) & (kcol < ds)
            online_update(h, jnp.where(okn, sn, NEG), vn)
            o_slc = acc_scr[h] / l_scr[h]
            wb = st_ref.shape[0]
            band = wb + HEAD_DIM
            st_flat = st_ref.reshape(wb * 8, HEAD_DIM)
            kw = jnp.concatenate([st_flat[pl.ds(h, wb, stride=8), :],
                                  wn_ref[:, h * HEAD_DIM:(h + 1) * HEAD_DIM], zpad], axis=0).astype(BF16)
            vw = jnp.concatenate([st_flat[pl.ds(N_KV + h, wb, stride=8), :],
                                  wn_ref[:, KV_W + h * HEAD_DIM:KV_W + (h + 1) * HEAD_DIM], zpad],
                                 axis=0).astype(BF16)
            sw = lax.dot_general(qq, kw, _NT, preferred_element_type=F32) * SCALE
            srw = lax.broadcasted_iota(jnp.int32, (rq, band), 0) & (ds - 1)
            idx = lax.broadcasted_iota(jnp.int32, (rq, band), 1)
            dlt = (wb + srw) - idx
            sw = jnp.where((dlt >= 0) & (dlt < WINDOW) & (idx < wb + ds), sw, NEG)
            ew = jnp.exp(sw - jnp.max(sw, axis=1, keepdims=True))
            o_win = jnp.dot(ew.astype(BF16), vw, preferred_element_type=F32) / jnp.sum(ew, axis=1, keepdims=True)
            o_cmp = ocmp_scr[h]
            gh = gt[h]
            for g in range(GROUP):
                r0, r1 = g * ds, (g + 1) * ds
                og = gh[:, g:g + 1] * o_cmp[r0:r1] + gh[:, GROUP + g:GROUP + g + 1] * o_slc[r0:r1]
                og = og + gh[:, 2 * GROUP + g:2 * GROUP + g + 1] * o_win[r0:r1]
                c0 = (h * GROUP + g) * HEAD_DIM
                o_ref[:, c0:c0 + HEAD_DIM] = og


def _attn_sample(q, kcv, kv_new, win_new, win_state, g_re, cache4, page_table, *, ds):
    nb, npg = page_table.shape
    past = npg * PAGE
    assert ds & (ds - 1) == 0 and ds <= SLC_LEN and past % SLC_LEN == 0 and npg % PAGES_PER_STEP == 0
    steps = npg // PAGES_PER_STEP
    ncp = kcv.shape[3]
    wb = win_state.shape[1]
    n_blocks = past // SLC_LEN + 1
    n_sel_chunks = -(-n_blocks // BLK_PER_CHUNK)
    m = _overlap_matrix(ncp, n_sel_chunks * BLK_PER_CHUNK, ncp - 1, n_blocks)
    ms = np.zeros((ncp, n_sel_chunks, HEAD_DIM), np.float32)
    ms[:, :, :BLK_PER_CHUNK] = m.reshape(ncp, n_sel_chunks, BLK_PER_CHUNK)
    ms = jnp.asarray(ms.reshape(ncp, n_sel_chunks * HEAD_DIM))
    ee = np.zeros((HEAD_DIM, CHUNK_ROWS), np.float32)
    ee[:BLK_PER_CHUNK] = np.arange(CHUNK_ROWS)[None, :] // SLC_LEN == np.arange(BLK_PER_CHUNK)[:, None]
    ee = jnp.asarray(ee, dtype=BF16)
    rq = GROUP * ds

    grid_spec = pltpu.PrefetchScalarGridSpec(
        num_scalar_prefetch=1,
        grid=(nb, steps),
        in_specs=[_page_spec(n, 1) for n in range(PAGES_PER_STEP)] + [
            pl.BlockSpec((ds, N_HEADS * HEAD_DIM), lambda b, p, pt: (b, 0)),
            pl.BlockSpec((None, 2, N_KV, ncp, HEAD_DIM), lambda b, p, pt: (b, 0, 0, 0, 0)),
            pl.BlockSpec((ds, 4 * KV_W), lambda b, p, pt: (b, 0)),
            pl.BlockSpec((ds, 2 * KV_W), lambda b, p, pt: (b, 0)),
            pl.BlockSpec((None, wb, 8, HEAD_DIM), lambda b, p, pt: (b, 0, 0, 0)),
            pl.BlockSpec((None, N_KV, ds, 4 * GROUP), lambda b, p, pt: (b, 0, 0, 0)),
            pl.BlockSpec(ms.shape, lambda b, p, pt: (0, 0)),
            pl.BlockSpec(ee.shape, lambda b, p, pt: (0, 0))],
        out_specs=pl.BlockSpec((ds, N_HEADS * HEAD_DIM), lambda b, p, pt: (b, 0)),
        scratch_shapes=[pltpu.VMEM((N_KV, rq, 1), F32), pltpu.VMEM((N_KV, rq, 1), F32),
                        pltpu.VMEM((N_KV, rq, HEAD_DIM), F32),
                        pltpu.VMEM((N_KV, n_sel_chunks, ds, HEAD_DIM), F32),
                        pltpu.VMEM((N_KV, rq, HEAD_DIM), F32)],
    )
    return pl.pallas_call(
        functools.partial(_attn_sample_kernel, past=past, ds=ds),
        grid_spec=grid_spec,
        out_shape=jax.ShapeDtypeStruct((nb * ds, N_HEADS * HEAD_DIM), F32),
        compiler_params=_cparams(("arbitrary", "arbitrary")),
        name="attn_sample",
    )(page_table, *([cache4] * PAGES_PER_STEP), q, kcv, kv_new, win_new, win_state, g_re, ms, ee)


def _split_w_in(w_in, tc):
    d = D_MODEL
    o = 0
    parts = {}
    for name, width in (("q", N_HEADS * HEAD_DIM), ("kv", 4 * KV_W), ("win", 2 * KV_W), ("g", 3 * N_HEADS),
                        ("u", d), ("b", d), ("c", d), ("ma", d), ("mc", d)):
        parts[name] = w_in[:, o:o + width]
        o += width
    nj = d // tc
    ubc = jnp.stack([parts[n].reshape(d, nj, tc) for n in ("u", "b", "c")], axis=2).reshape(d, 3 * d)
    g_pad = jnp.pad(parts["g"], ((0, 0), (0, HEAD_DIM - 3 * N_HEADS)))
    gates = jnp.concatenate([parts["ma"], parts["mc"]], axis=1)
    return {"q": parts["q"].astype(BF16), "kv": parts["kv"].astype(BF16), "win": parts["win"].astype(BF16),
            "g": g_pad.astype(BF16), "ubc": ubc.astype(BF16), "gates": gates.astype(BF16)}


def _gate_layout(g_logits, nb, seq):
    g = g_logits[:, :3 * N_HEADS].reshape(nb, seq, 3, N_KV, GROUP)
    g = g.transpose(0, 3, 1, 2, 4).reshape(nb, N_KV, seq, 3 * GROUP)
    return jnp.pad(g, ((0, 0), (0, 0), (0, 0), (0, GROUP)))


def _cmp_weights(w_k1, w_k2, pe_k, w_v1, w_v2, pe_v):
    def pairs(w1):
        w = w1.reshape(2, CMP_STRIDE, HEAD_DIM, HEAD_DIM).transpose(1, 2, 0, 3)
        return w.reshape(CMP_STRIDE // 2, 2 * HEAD_DIM, 2 * HEAD_DIM)
    w1pairs = jnp.stack([pairs(w_k1), pairs(w_v1)]).astype(BF16)
    pe8 = jnp.stack([jnp.broadcast_to(pe_k.reshape(1, -1), (8, CMP_LEN * HEAD_DIM)),
                     jnp.broadcast_to(pe_v.reshape(1, -1), (8, CMP_LEN * HEAD_DIM))])
    return w1pairs, pe8, jnp.stack([w_k1, w_v1]), jnp.stack([w_k2, w_v2])


def kernel(x_prompt, x_sample, c_prompt, c_sample, cache_nsa_kv, page_table, state_win_kv, state_conv, w_ada, b_ada, norm1_g, norm2_g, w_in, w_cmp_k1, w_cmp_k2, pe_cmp_k, w_cmp_v1, w_cmp_v2, pe_cmp_v, conv_w, conv_b, w_attn_proj, w_conv_proj, w_out, w_mlp1, w_mlp2, normf_g):
    d = D_MODEL
    nbp, seq, _ = x_prompt.shape
    nbs, ds, _ = x_sample.shape
    tp = nbp * seq
    ts = nbs * ds
    tc = 256
    depth = w_in.shape[0]
    assert depth == 1

    xp = x_prompt.reshape(tp, d)
    xs = x_sample.reshape(ts, d)
    normf = normf_g.reshape(1, d)

    l = 0
    c_all = jnp.concatenate([c_prompt, c_sample, jnp.zeros((16 - nbp - nbs, d), F32)], axis=0)
    ada = _ada(c_all, w_ada[l], b_ada[l].reshape(1, -1)).reshape(16, 6, d)
    ada_p = [ada[:nbp, k][:, None, :] for k in range(6)]
    ada_s = [jnp.repeat(ada[nbp:nbp + nbs, k], ds, axis=0)[None] for k in range(6)]

    w = _split_w_in(w_in[l], tc)
    wa = w_attn_proj[l].astype(BF16)
    wc = w_conv_proj[l].astype(BF16)
    wo = w_out[l].astype(BF16)
    w1 = w_mlp1[l].astype(BF16)
    w2 = w_mlp2[l].astype(BF16)
    g1n = norm1_g[l].reshape(1, d)
    g2n = norm2_g[l].reshape(1, d)
    cw = conv_w[l]
    cb = conv_b[l].reshape(1, d)
    w1pairs, pe8, w1s, w2s = _cmp_weights(w_cmp_k1[l], w_cmp_k2[l], pe_cmp_k[l],
                                          w_cmp_v1[l], w_cmp_v2[l], pe_cmp_v[l])

    tm = 1024
    hp = _norm_mod(xp, g1n, ada_p[1], ada_p[0], tm=256, seq=seq)
    q_p = _mm(hp, w["q"], tm=tm, tn=1024, out_dtype=BF16, name="mm_q")
    wbp = min(WINDOW, seq)
    kv6_p, kvb_p = _mm_kv(hp, w["kv"], tm=tm)
    winb_p, win6_p = _mm_win(hp, w["win"], tm=wbp, seq=seq)
    g_p = _mm(hp, w["g"], tm=tm, tn=HEAD_DIM, out_dtype=F32, name="mm_g")
    gates_p = _mm(hp, w["gates"], tm=tm, tn=1024, out_dtype=BF16, act="sigmoid", name="mm_gates")
    z_p, conv_tiles = _mm_conv(hp, w["ubc"], cw, cb, tm=tm, tc=tc, seq=seq)

    pt_p = jnp.arange(tp // PAGE, dtype=jnp.int32).reshape(nbp, seq // PAGE)
    part_p = _cmp_part(kv6_p.reshape(tp // PAGE, PAGE, 16, HEAD_DIM), pt_p, w1pairs)
    kcv_p = _cmp_finish(part_p, pe8, w1s, w2s)
    o_p = _attn_prompt(q_p, kcv_p, kvb_p, winb_p, _gate_layout(g_p, nbp, seq), nb=nbp, seq=seq)

    mixed_p = _mix(o_p, z_p, wa, wc, gates_p, tm=tm, tn=512)
    x1_p = _mm_resid(mixed_p, wo, xp, ada_p[2], tm=tm, tn=1024, seq=seq)
    h2_p = _norm_mod(x1_p, g2n, ada_p[4], ada_p[3], tm=256, seq=seq)
    y_p = _mlp_final(h2_p, w1, w2, x1_p, ada_p[5], normf, tm=512, tf=512, seq=seq)

    hs = _norm_mod(xs, g1n, ada_s[1], ada_s[0], tm=ts, seq=ds)
    q_s = _mm(hs, w["q"], tm=ts, tn=512, out_dtype=F32, name="mm_q_s")
    kv_s = _mm(hs, w["kv"], tm=ts, tn=512, out_dtype=F32, name="mm_kv_s")
    win_s = _mm(hs, w["win"], tm=ts, tn=512, out_dtype=F32, name="mm_win_s")
    g_s = _mm(hs, w["g"], tm=ts, tn=HEAD_DIM, out_dtype=F32, name="mm_g_s")
    gates_s = _mm(hs, w["gates"], tm=ts, tn=512, out_dtype=F32, act="sigmoid", name="mm_gates_s")
    ubc_s = _mm(hs, w["ubc"], tm=ts, tn=3 * tc, out_dtype=F32, name="mm_ubc_s")
    st = state_conv[l]
    z_s, cu_s = _sample_conv(ubc_s, jnp.repeat(st[:, 0], ds, axis=0), jnp.repeat(st[:, 1], ds, axis=0),
                             cw, cb, tc=tc, seq=ds)

    cache4 = cache_nsa_kv[l].reshape(cache_nsa_kv.shape[1], PAGE, 16, HEAD_DIM)
    part_s = _cmp_part(cache4, page_table, w1pairs)
    kcv_s = _cmp_finish(part_s, pe8, w1s, w2s)
    wbs = state_win_kv.shape[2]
    wst = state_win_kv[l].reshape(nbs, wbs, 8, HEAD_DIM)
    o_s = _attn_sample(q_s, kcv_s, kv_s, win_s, wst, _gate_layout(g_s, nbs, ds), cache4, page_table, ds=ds)

    mixed_s = _mix(o_s, z_s, wa, wc, gates_s, tm=ts, tn=512)
    x1_s = _mm_resid(mixed_s, wo, xs, ada_s[2], tm=ts, tn=512, seq=ds)
    h2_s = _norm_mod(x1_s, g2n, ada_s[4], ada_s[3], tm=ts, seq=ds)
    y_s = _mlp_final(h2_s, w1, w2, x1_s, ada_s[5], normf, tm=ts, tf=512, seq=ds)

    kv_prompt = kv6_p.reshape(1, nbp, seq, 4, N_KV, HEAD_DIM)
    kv_sample = kv_s.reshape(1, nbs, ds, 4, N_KV, HEAD_DIM)
    win_prompt = win6_p.reshape(1, nbp, wbp, 2, N_KV, HEAD_DIM)
    win_sample = jnp.concatenate([state_win_kv[l][:, ds:], win_s.reshape(nbs, ds, 2, N_KV, HEAD_DIM)], axis=1)[None]
    tiles_per_seq = seq // tm
    conv_prompt = conv_tiles[tiles_per_seq - 1::tiles_per_seq][None]
    conv_sample = cu_s.reshape(nbs, ds, d)[None, :, ds - (CONV_W - 1):]
    return (y_p.reshape(nbp, seq, d), y_s.reshape(nbs, ds, d), kv_prompt, kv_sample,
            win_prompt, win_sample, conv_prompt, conv_sample)
```

```python
import functools

import numpy as np
import jax
import jax.numpy as jnp
from jax import lax
from jax.experimental import pallas as pl
from jax.experimental.pallas import tpu as pltpu

F32 = jnp.float32
BF16 = jnp.bfloat16

D_MODEL = 2048
HEAD_DIM = 128
N_HEADS = 16
N_KV = 4
GROUP = 4
KV_W = N_KV * HEAD_DIM
CMP_LEN = 32
CMP_STRIDE = 16
SLC_LEN = 64
SLC_TOP = 16
N_LOCAL = 2
WINDOW = 512
CONV_W = 3
RMS_EPS = 1e-6
NEG = -1e30
SCALE = HEAD_DIM ** -0.5
PAGE = 128
PAGES_PER_STEP = 16
CHUNK_ROWS = PAGES_PER_STEP * PAGE
BLK_PER_CHUNK = CHUNK_ROWS // SLC_LEN
VMEM_LIMIT = 56 * 1024 * 1024

_NT = (((1,), (1,)), ((), ()))
_TN = (((0,), (0,)), ((), ()))


def _cparams(sem):
    return pltpu.CompilerParams(dimension_semantics=sem, vmem_limit_bytes=VMEM_LIMIT)


def _ada_kernel(c_ref, w_ref, b_ref, o_ref):
    c = c_ref[...]
    a = (c * jax.nn.sigmoid(c)).astype(BF16)
    o_ref[...] = jnp.dot(a, w_ref[...].astype(BF16), preferred_element_type=F32) + b_ref[...]


def _ada(c, w, b, tn=1024):
    m, k = c.shape
    n = w.shape[1]
    return pl.pallas_call(
        _ada_kernel,
        grid=(n // tn,),
        in_specs=[pl.BlockSpec((m, k), lambda j: (0, 0)),
                  pl.BlockSpec((k, tn), lambda j: (0, j)),
                  pl.BlockSpec((1, tn), lambda j: (0, j))],
        out_specs=pl.BlockSpec((m, tn), lambda j: (0, j)),
        out_shape=jax.ShapeDtypeStruct((m, n), F32),
        compiler_params=_cparams(("arbitrary",)),
        name="ada",
    )(c, w, b)


def _rowspec(p, tm, tn, seq, col=True):
    gr = p.shape[1]
    if gr == 1:
        if col:
            return pl.BlockSpec((1, 1, tn), lambda i, j: ((i * tm) // seq, 0, j))
        return pl.BlockSpec((1, 1, tn), lambda i: ((i * tm) // seq, 0, 0))
    if col:
        return pl.BlockSpec((1, tm, tn), lambda i, j: (0, i, j))
    return pl.BlockSpec((1, tm, tn), lambda i: (0, i, 0))


def _norm_kernel(x_ref, g_ref, sc_ref, sh_ref, o_ref):
    x = x_ref[...]
    r = lax.rsqrt(jnp.mean(x * x, axis=-1, keepdims=True) + RMS_EPS)
    y = (x * r) * g_ref[...]
    o_ref[...] = (y * (1.0 + sc_ref[0]) + sh_ref[0]).astype(o_ref.dtype)


def _norm_mod(x, g, sc, sh, *, tm, seq):
    t, d = x.shape
    return pl.pallas_call(
        _norm_kernel,
        grid=(t // tm,),
        in_specs=[pl.BlockSpec((tm, d), lambda i: (i, 0)),
                  pl.BlockSpec((1, d), lambda i: (0, 0)),
                  _rowspec(sc, tm, d, seq, col=False),
                  _rowspec(sh, tm, d, seq, col=False)],
        out_specs=pl.BlockSpec((tm, d), lambda i: (i, 0)),
        out_shape=jax.ShapeDtypeStruct((t, d), BF16),
        compiler_params=_cparams(("arbitrary",)),
        name="norm_mod",
    )(x, g, sc, sh)


def _mm_kernel(x_ref, w_ref, o_ref, *, act):
    acc = jnp.dot(x_ref[...].astype(BF16), w_ref[...], preferred_element_type=F32)
    if act == "sigmoid":
        acc = jax.nn.sigmoid(acc)
    o_ref[...] = acc.astype(o_ref.dtype)


def _mm(x, w, *, tm, tn, out_dtype, act=None, name="mm"):
    t, k = x.shape
    n = w.shape[1]
    return pl.pallas_call(
        functools.partial(_mm_kernel, act=act),
        grid=(t // tm, n // tn),
        in_specs=[pl.BlockSpec((tm, k), lambda i, j: (i, 0)),
                  pl.BlockSpec((k, tn), lambda i, j: (0, j))],
        out_specs=pl.BlockSpec((tm, tn), lambda i, j: (i, j)),
        out_shape=jax.ShapeDtypeStruct((t, n), out_dtype),
        compiler_params=_cparams(("arbitrary", "arbitrary")),
        name=name,
    )(x, w)


def _store_head_major(o_ref, acc, tm):
    flat = o_ref.reshape(tm * 8, HEAD_DIM)
    for cb in range(8):
        flat[pl.ds(cb, tm, stride=8), :] = acc[:, cb * HEAD_DIM:(cb + 1) * HEAD_DIM]


def _mm_kv_kernel(x_ref, w_ref, o6_ref, ob_ref, *, tm):
    acc = jnp.dot(x_ref[...], w_ref[...], preferred_element_type=F32)
    _store_head_major(o6_ref, acc, tm)

    @pl.when(pl.program_id(1) == 1)
    def _():
        ob_ref[...] = acc.astype(BF16)


def _mm_kv(h, w, *, tm):
    t, k = h.shape
    tn = 2 * KV_W
    return pl.pallas_call(
        functools.partial(_mm_kv_kernel, tm=tm),
        grid=(t // tm, 2),
        in_specs=[pl.BlockSpec((tm, k), lambda i, j: (i, 0)),
                  pl.BlockSpec((k, tn), lambda i, j: (0, j))],
        out_specs=[pl.BlockSpec((tm, 8, HEAD_DIM), lambda i, j: (i, j, 0)),
                   pl.BlockSpec((tm, tn), lambda i, j: (i, 0))],
        out_shape=[jax.ShapeDtypeStruct((t, 16, HEAD_DIM), F32),
                   jax.ShapeDtypeStruct((t, tn), BF16)],
        compiler_params=_cparams(("arbitrary", "arbitrary")),
        name="mm_kv",
    )(h, w)


def _mm_win_kernel(x_ref, w_ref, ob_ref, o6_ref, *, tm, seq):
    acc = jnp.dot(x_ref[...], w_ref[...], preferred_element_type=F32)
    ob_ref[...] = acc.astype(BF16)

    @pl.when(((pl.program_id(0) + 1) * tm) % seq == 0)
    def _():
        _store_head_major(o6_ref, acc, tm)


def _mm_win(h, w, *, tm, seq):
    t, k = h.shape
    tn = 2 * KV_W
    return pl.pallas_call(
        functools.partial(_mm_win_kernel, tm=tm, seq=seq),
        grid=(t // tm,),
        in_specs=[pl.BlockSpec((tm, k), lambda i: (i, 0)),
                  pl.BlockSpec((k, tn), lambda i: (0, 0))],
        out_specs=[pl.BlockSpec((tm, tn), lambda i: (i, 0)),
                   pl.BlockSpec((tm, 8, HEAD_DIM), lambda i: ((i * tm) // seq, 0, 0))],
        out_shape=[jax.ShapeDtypeStruct((t, tn), BF16),
                   jax.ShapeDtypeStruct((t // seq * tm, 8, HEAD_DIM), F32)],
        compiler_params=_cparams(("arbitrary",)),
        name="mm_win",
    )(h, w)


def _conv_taps(cu, prev1, prev2, bgate, cw_ref, cb_ref):
    y = cb_ref[...] + prev2 * cw_ref[0:1, :]
    y = y + prev1 * cw_ref[1:2, :]
    y = y + cu * cw_ref[2:3, :]
    return bgate * y


def _mm_conv_kernel(x_ref, w_ref, cw_ref, cb_ref, z_ref, st_ref, carry_ref, *, tm, tc, seq):
    i = pl.program_id(0)
    j = pl.program_id(1)
    acc = jnp.dot(x_ref[...], w_ref[...], preferred_element_type=F32)
    u = acc[:, 0:tc]
    bgate = acc[:, tc:2 * tc]
    cu = acc[:, 2 * tc:3 * tc] * u
    @pl.when((i * tm) % seq == 0)
    def _():
        carry_ref[j] = jnp.zeros((8, tc), F32)

    car = carry_ref[j]
    p0 = car[0:1, :]
    p1 = car[1:2, :]
    rows = lax.broadcasted_iota(jnp.int32, (tm, tc), 0)
    r1 = pltpu.roll(cu, 1, 0)
    r2 = pltpu.roll(cu, 2, 0)
    prev1 = jnp.where(rows == 0, p1, r1)
    prev2 = jnp.where(rows == 0, p0, jnp.where(rows == 1, p1, r2))
    z_ref[...] = _conv_taps(cu, prev1, prev2, bgate, cw_ref, cb_ref).astype(z_ref.dtype)
    tail = cu[tm - 8:tm, :]
    carry_ref[j] = jnp.concatenate([tail[6:8, :], tail[0:6, :]], axis=0)
    st_ref[0] = tail[6:8, :]


def _mm_conv(h, w_il, conv_w, conv_b, *, tm, tc, seq):
    t, k = h.shape
    d = w_il.shape[1] // 3
    nj = d // tc
    return pl.pallas_call(
        functools.partial(_mm_conv_kernel, tm=tm, tc=tc, seq=seq),
        grid=(t // tm, nj),
        in_specs=[pl.BlockSpec((tm, k), lambda i, j: (i, 0)),
                  pl.BlockSpec((k, 3 * tc), lambda i, j: (0, j)),
                  pl.BlockSpec((CONV_W, tc), lambda i, j: (0, j)),
                  pl.BlockSpec((1, tc), lambda i, j: (0, j))],
        out_specs=[pl.BlockSpec((tm, tc), lambda i, j: (i, j)),
                   pl.BlockSpec((1, CONV_W - 1, tc), lambda i, j: (i, 0, j))],
        out_shape=[jax.ShapeDtypeStruct((t, d), BF16),
                   jax.ShapeDtypeStruct((t // tm, CONV_W - 1, d), F32)],
        scratch_shapes=[pltpu.VMEM((nj, 8, tc), F32)],
        compiler_params=_cparams(("arbitrary", "arbitrary")),
        name="mm_conv",
    )(h, w_il, conv_w, conv_b)


def _sample_conv_kernel(a_ref, st0_ref, st1_ref, cw_ref, cb_ref, z_ref, cu_ref, *, tc, seq):
    acc = a_ref[...]
    t = acc.shape[0]
    u = acc[:, 0:tc]
    bgate = acc[:, tc:2 * tc]
    cu = acc[:, 2 * tc:3 * tc] * u
    s = lax.rem(lax.broadcasted_iota(jnp.int32, (t, tc), 0), jnp.int32(seq))
    r1 = pltpu.roll(cu, 1, 0)
    r2 = pltpu.roll(cu, 2, 0)
    prev1 = jnp.where(s == 0, st1_ref[...], r1)
    prev2 = jnp.where(s == 0, st0_ref[...], jnp.where(s == 1, st1_ref[...], r2))
    z_ref[...] = _conv_taps(cu, prev1, prev2, bgate, cw_ref, cb_ref)
    cu_ref[...] = cu


def _sample_conv(ubc, st0, st1, conv_w, conv_b, *, tc, seq):
    t = ubc.shape[0]
    d = ubc.shape[1] // 3
    return pl.pallas_call(
        functools.partial(_sample_conv_kernel, tc=tc, seq=seq),
        grid=(d // tc,),
        in_specs=[pl.BlockSpec((t, 3 * tc), lambda j: (0, j)),
                  pl.BlockSpec((t, tc), lambda j: (0, j)),
                  pl.BlockSpec((t, tc), lambda j: (0, j)),
                  pl.BlockSpec((CONV_W, tc), lambda j: (0, j)),
                  pl.BlockSpec((1, tc), lambda j: (0, j))],
        out_specs=[pl.BlockSpec((t, tc), lambda j: (0, j)),
                   pl.BlockSpec((t, tc), lambda j: (0, j))],
        out_shape=[jax.ShapeDtypeStruct((t, d), F32), jax.ShapeDtypeStruct((t, d), F32)],
        compiler_params=_cparams(("arbitrary",)),
        name="sample_conv",
    )(ubc, st0, st1, conv_w, conv_b)


def _mix_kernel(o_ref, z_ref, wa_ref, wc_ref, ga_ref, gc_ref, m_ref):
    a = jnp.dot(o_ref[...].astype(BF16), wa_ref[...], preferred_element_type=F32)
    c = jnp.dot(z_ref[...].astype(BF16), wc_ref[...], preferred_element_type=F32)
    m_ref[...] = (ga_ref[...].astype(F32) * a + gc_ref[...].astype(F32) * c).astype(m_ref.dtype)


def _mix(o, z, wa, wc, gates, *, tm, tn):
    t, k = o.shape
    n = wa.shape[1]
    nj = n // tn
    return pl.pallas_call(
        _mix_kernel,
        grid=(t // tm, nj),
        in_specs=[pl.BlockSpec((tm, k), lambda i, j: (i, 0)),
                  pl.BlockSpec((tm, k), lambda i, j: (i, 0)),
                  pl.BlockSpec((k, tn), lambda i, j: (0, j)),
                  pl.BlockSpec((k, tn), lambda i, j: (0, j)),
                  pl.BlockSpec((tm, tn), lambda i, j: (i, j)),
                  pl.BlockSpec((tm, tn), lambda i, j: (i, j + nj))],
        out_specs=pl.BlockSpec((tm, tn), lambda i, j: (i, j)),
        out_shape=jax.ShapeDtypeStruct((t, n), BF16),
        compiler_params=_cparams(("arbitrary", "arbitrary")),
        name="mix",
    )(o, z, wa, wc, gates, gates)


def _resid_kernel(a_ref, w_ref, x_ref, g_ref, o_ref):
    acc = jnp.dot(a_ref[...], w_ref[...], preferred_element_type=F32)
    o_ref[...] = x_ref[...] + g_ref[0] * acc


def _mm_resid(a, w, x, gate, *, tm, tn, seq):
    t, k = a.shape
    n = w.shape[1]
    return pl.pallas_call(
        _resid_kernel,
        grid=(t // tm, n // tn),
        in_specs=[pl.BlockSpec((tm, k), lambda i, j: (i, 0)),
                  pl.BlockSpec((k, tn), lambda i, j: (0, j)),
                  pl.BlockSpec((tm, tn), lambda i, j: (i, j)),
                  _rowspec(gate, tm, tn, seq)],
        out_specs=pl.BlockSpec((tm, tn), lambda i, j: (i, j)),
        out_shape=jax.ShapeDtypeStruct((t, n), F32),
        compiler_params=_cparams(("arbitrary", "arbitrary")),
        name="mm_resid",
    )(a, w, x, gate)


def _mlp_kernel(h_ref, w1_ref, w2_ref, x_ref, g_ref, nf_ref, o_ref, acc_ref):
    k = pl.program_id(1)

    @pl.when(k == 0)
    def _():
        acc_ref[...] = jnp.zeros_like(acc_ref)

    a = jnp.dot(h_ref[...], w1_ref[...], preferred_element_type=F32)
    a = jnp.square(jnp.maximum(a, 0.0)).astype(BF16)
    acc_ref[...] += jnp.dot(a, w2_ref[...], preferred_element_type=F32)

    @pl.when(k == pl.num_programs(1) - 1)
    def _():
        x2 = x_ref[...] + g_ref[0] * acc_ref[...]
        r = lax.rsqrt(jnp.mean(x2 * x2, axis=-1, keepdims=True) + RMS_EPS)
        o_ref[...] = (x2 * r) * nf_ref[...]


def _mlp_final(h2, w1, w2, x1, gate, normf, *, tm, tf, seq):
    t, d = h2.shape
    f = w1.shape[1]
    return pl.pallas_call(
        _mlp_kernel,
        grid=(t // tm, f // tf),
        in_specs=[pl.BlockSpec((tm, d), lambda i, k: (i, 0)),
                  pl.BlockSpec((d, tf), lambda i, k: (0, k)),
                  pl.BlockSpec((tf, d), lambda i, k: (k, 0)),
                  pl.BlockSpec((tm, d), lambda i, k: (i, 0)),
                  _rowspec_k(gate, tm, d, seq),
                  pl.BlockSpec((1, d), lambda i, k: (0, 0))],
        out_specs=pl.BlockSpec((tm, d), lambda i, k: (i, 0)),
        out_shape=jax.ShapeDtypeStruct((t, d), F32),
        scratch_shapes=[pltpu.VMEM((tm, d), F32)],
        compiler_params=_cparams(("arbitrary", "arbitrary")),
        name="mlp",
    )(h2, w1, w2, x1, gate, normf)


def _rowspec_k(p, tm, d, seq):
    if p.shape[1] == 1:
        return pl.BlockSpec((1, 1, d), lambda i, k: ((i * tm) // seq, 0, 0))
    return pl.BlockSpec((1, tm, d), lambda i, k: (0, i, 0))


def _cmp_part_kernel(pt_ref, *refs):
    pages = refs[:PAGES_PER_STEP]
    w_ref = refs[PAGES_PER_STEP]
    o_ref = refs[PAGES_PER_STEP + 1]
    stage = refs[PAGES_PER_STEP + 2]
    nchunk = CHUNK_ROWS // CMP_STRIDE
    cpp = PAGE // CMP_STRIDE
    nslab = 2 * N_KV
    cols = [jnp.concatenate([pg[pl.ds(r, cpp, stride=CMP_STRIDE)].reshape(cpp * nslab, HEAD_DIM)
                             for pg in pages], axis=0).astype(BF16) for r in range(CMP_STRIDE)]
    lhs = jnp.concatenate(cols, axis=1)
    acc = jnp.dot(lhs, w_ref[...], preferred_element_type=F32)
    slab = lax.broadcasted_iota(jnp.int32, (nchunk * nslab, 2 * HEAD_DIM), 0) & (nslab - 1)
    sel = jnp.where(slab < N_KV, acc[:, :2 * HEAD_DIM], acc[:, 2 * HEAD_DIM:])
    stage[0] = sel[:, :HEAD_DIM]
    stage[1] = sel[:, HEAD_DIM:]
    for cb in range(nslab):
        o_ref[cb // N_KV, cb % N_KV] = jnp.concatenate(
            [stage[0, pl.ds(cb, nchunk, stride=nslab), :], stage[1, pl.ds(cb, nchunk, stride=nslab), :]], axis=1)


def _page_spec(n, kind_pair):
    return pl.BlockSpec((None, PAGE, 8, HEAD_DIM),
                        lambda b, p, pt, n=n: (pt[b, p * PAGES_PER_STEP + n], 0, kind_pair, 0))


def _cmp_part(pages4, page_table, w1pairs):
    nb, npg = page_table.shape
    steps = npg // PAGES_PER_STEP
    nchunk = PAGES_PER_STEP * (PAGE // CMP_STRIDE)
    grid_spec = pltpu.PrefetchScalarGridSpec(
        num_scalar_prefetch=1,
        grid=(nb, steps),
        in_specs=[_page_spec(n, 0) for n in range(PAGES_PER_STEP)]
        + [pl.BlockSpec((CMP_STRIDE * HEAD_DIM, 4 * HEAD_DIM), lambda b, p, pt: (0, 0))],
        out_specs=pl.BlockSpec((None, 2, N_KV, nchunk, 2 * HEAD_DIM), lambda b, p, pt: (b, 0, 0, p, 0)),
        scratch_shapes=[pltpu.VMEM((2, nchunk * 2 * N_KV, HEAD_DIM), F32)],
    )
    return pl.pallas_call(
        _cmp_part_kernel,
        grid_spec=grid_spec,
        out_shape=jax.ShapeDtypeStruct((nb, 2, N_KV, steps * nchunk, 2 * HEAD_DIM), F32),
        compiler_params=_cparams(("arbitrary", "arbitrary")),
        name="cmp_part",
    )(page_table, *([pages4] * PAGES_PER_STEP), w1pairs)


def _gelu_tanh(x):
    c = np.sqrt(2.0 / np.pi).astype(np.float32)
    return 0.5 * x * (1.0 + jnp.tanh(c * (x + 0.044715 * (x * x * x))))


def _cmp_finish_kernel(part_ref, pe_ref, w1_ref, w2_ref, o_ref, bias_ref):
    @pl.when((pl.program_id(1) == 0) & (pl.program_id(2) == 0))
    def _():
        bias_ref[...] = jnp.dot(pe_ref[...], w1_ref[...], preferred_element_type=F32,
                                precision=lax.Precision.HIGHEST)

    part = part_ref[...]
    n = part.shape[0]
    h = part[:, 0:HEAD_DIM] + pltpu.roll(part[:, HEAD_DIM:], n - 1, 0)
    h = h + bias_ref[0:1, :]
    o_ref[...] = jnp.dot(_gelu_tanh(h).astype(BF16), w2_ref[...].astype(BF16),
                         preferred_element_type=F32).astype(o_ref.dtype)


def _cmp_finish(part, pe8, w1, w2):
    nb, _, _, nchunk, _ = part.shape
    return pl.pallas_call(
        _cmp_finish_kernel,
        grid=(2, nb, N_KV),
        in_specs=[pl.BlockSpec((None, None, None, nchunk, 2 * HEAD_DIM), lambda k, b, h: (b, k, h, 0, 0)),
                  pl.BlockSpec((None, 8, CMP_LEN * HEAD_DIM), lambda k, b, h: (k, 0, 0)),
                  pl.BlockSpec((None, CMP_LEN * HEAD_DIM, HEAD_DIM), lambda k, b, h: (k, 0, 0)),
                  pl.BlockSpec((None, HEAD_DIM, HEAD_DIM), lambda k, b, h: (k, 0, 0))],
        out_specs=pl.BlockSpec((None, None, None, nchunk, HEAD_DIM), lambda k, b, h: (b, k, h, 0, 0)),
        out_shape=jax.ShapeDtypeStruct((nb, 2, N_KV, nchunk, HEAD_DIM), BF16),
        scratch_shapes=[pltpu.VMEM((8, HEAD_DIM), F32)],
        compiler_params=_cparams(("arbitrary", "arbitrary", "arbitrary")),
        name="cmp_finish",
    )(part, pe8, w1, w2)


def _select_topk(score, valid, forced, blk, axis):
    sc = jnp.where(valid, jnp.where(forced, jnp.inf, score), -jnp.inf)
    sel = jnp.zeros(score.shape, F32)
    big = jnp.int32(1 << 20)
    for _ in range(SLC_TOP):
        mx = jnp.max(sc, axis=axis, keepdims=True)
        idx = jnp.min(jnp.where(sc == mx, blk, big), axis=axis, keepdims=True)
        hit = blk == idx
        sel = jnp.where(hit & (mx > -jnp.inf), 1.0, sel)
        sc = jnp.where(hit, -jnp.inf, sc)
    return sel


def _overlap_matrix(nc_pad, n_blocks_pad, nc, n_blocks):
    cs = np.arange(nc_pad) * CMP_STRIDE
    sb = np.arange(n_blocks_pad) * SLC_LEN
    ov = np.clip(np.minimum(cs[:, None] + CMP_LEN, sb[None, :] + SLC_LEN)
                 - np.maximum(cs[:, None], sb[None, :]), 0, None)
    m = (ov / CMP_STRIDE).astype(np.float32)
    m[nc:, :] = 0.0
    m[:, n_blocks:] = 0.0
    return m


def _attn_prompt_kernel(q_ref, kcb, vcb, ksb, vsb, kwb, vwb, g_ref, mt_ref, e_ref, o_ref,
                        m_scr, l_scr, acc_scr, bias_scr, gate_scr, *, tq, seq, ck):
    qi = pl.program_id(2)
    nc = seq // CMP_STRIDE - 1
    nsb = seq // SLC_LEN
    rb = min(tq, 128)

    q = q_ref[...]
    qq = jnp.concatenate([q[:, g * HEAD_DIM:(g + 1) * HEAD_DIM] for g in range(GROUP)], axis=0)
    rq = GROUP * tq
    q0 = qi * tq

    gt = jax.nn.sigmoid(g_ref[...])
    for c in range(3 * GROUP):
        gate_scr[c] = jnp.broadcast_to(gt[:, c:c + 1], (tq, HEAD_DIM))

    ncp = kcb.shape[0]
    s = lax.dot_general(qq, kcb[...], _NT, preferred_element_type=F32) * SCALE
    qpos = q0 + (lax.broadcasted_iota(jnp.int32, (rq, ncp), 0) & (tq - 1))
    col = lax.broadcasted_iota(jnp.int32, (rq, ncp), 1)
    vis = (col * CMP_STRIDE + (CMP_LEN - 1) <= qpos) & (col < nc)
    s = jnp.where(vis, s, NEG)
    mx = jnp.max(s, axis=1, keepdims=True)
    e = jnp.where(vis, jnp.exp(s - mx), 0.0)
    den = jnp.sum(e, axis=1, keepdims=True)
    p = e / jnp.where(den > 0.0, den, 1.0)
    o_cmp = jnp.dot(p.astype(BF16), vcb[...], preferred_element_type=F32)
    pg = p[0:tq]
    for g in range(1, GROUP):
        pg = pg + p[g * tq:(g + 1) * tq]

    sct = lax.dot_general(mt_ref[...], pg, _NT, preferred_element_type=F32,
                          precision=lax.Precision.HIGHEST)
    blk = lax.broadcasted_iota(jnp.int32, (nsb, tq), 0)
    cur = lax.div(q0 + lax.broadcasted_iota(jnp.int32, (nsb, tq), 1), jnp.int32(SLC_LEN))
    valid = blk <= cur
    forced = (blk == 0) | (blk > cur - N_LOCAL)
    need_topk = (q0 + tq - 1) // SLC_LEN + 1 > SLC_TOP
    sel_t = lax.cond(need_topk,
                     lambda: _select_topk(sct, valid, forced, blk, 0),
                     lambda: valid.astype(F32)).astype(BF16)

    def sweep(k_ref, v_ref, k_start, n_chunks, bias_fn):
        m_scr[...] = jnp.full((rq, HEAD_DIM), NEG, F32)
        l_scr[...] = jnp.zeros((rq, HEAD_DIM), F32)
        acc_scr[...] = jnp.zeros((rq, HEAD_DIM), F32)
        nl = ck // HEAD_DIM

        def scores(c):
            k0 = pl.multiple_of(k_start + c * ck, 128)
            return lax.dot_general(qq, k_ref[pl.ds(k0, ck), :], _NT, preferred_element_type=F32)

        def chunk_inputs(c):
            k0 = pl.multiple_of(k_start + c * ck, 128)
            return scores(c), bias_fn(c, k0)

        c_exp = np.float32(SCALE * np.log2(np.e))

        def reduce_chunk(c, sb):
            sc_, bias = sb
            k0 = pl.multiple_of(k_start + c * ck, 128)
            vch = v_ref[pl.ds(k0, ck), :]
            alphas, ps = [], []
            nblk = rq // rb
            for r in range(nblk):
                rows = slice(r * rb, (r + 1) * rb)
                b0 = (r * rb) % tq
                sg = [sc_[rows, j * HEAD_DIM:(j + 1) * HEAD_DIM]
                      + bias[b0:b0 + rb, j * HEAD_DIM:(j + 1) * HEAD_DIM] for j in range(nl)]
                mx = functools.reduce(jnp.maximum, sg)
                m_old = m_scr[rows]
                m_new = jnp.maximum(m_old, jnp.max(mx, axis=1, keepdims=True))
                alpha = jnp.exp2((m_old - m_new) * c_exp)
                pj = [jnp.exp2((x - m_new) * c_exp) for x in sg]
                psum = functools.reduce(lambda a, b: a + b, pj)
                l_scr[rows] = alpha * l_scr[rows] + jnp.sum(psum, axis=1, keepdims=True)
                m_scr[rows] = m_new
                alphas.append(alpha)
                ps.append(jnp.concatenate([x.astype(BF16) for x in pj], axis=1))
                if (r + 1) % (nblk // 2) == 0:
                    half = slice((r + 1 - nblk // 2) * rb, (r + 1) * rb)
                    pv = jnp.dot(jnp.concatenate(ps, axis=0), vch, preferred_element_type=F32)
                    acc_scr[half] = jnp.concatenate(alphas, axis=0) * acc_scr[half] + pv
                    alphas, ps = [], []

        def body(c, carry):
            reduce_chunk(c, chunk_inputs(c))
            return carry

        lax.fori_loop(0, n_chunks, body, 0)
        return acc_scr[...] / l_scr[...]

    qp = q0 + lax.broadcasted_iota(jnp.int32, (tq, ck), 0)
    kcol = lax.broadcasted_iota(jnp.int32, (tq, ck), 1)

    eye = (lax.broadcasted_iota(jnp.int32, (nsb, HEAD_DIM), 0)
           == lax.broadcasted_iota(jnp.int32, (nsb, HEAD_DIM), 1)).astype(BF16)
    sel_q = lax.dot_general(sel_t, eye, _TN, preferred_element_type=F32).astype(BF16)

    n_slc = (q0 + tq + ck - 1) // ck

    def put_mask(c):
        selq = jnp.dot(sel_q, e_ref[c], preferred_element_type=F32)
        bias_scr[c] = jnp.where((selq > 0.5) & (c * ck + kcol <= qp), 0.0, NEG)

    def mask_body(i, carry):
        put_mask(2 * i)
        put_mask(jnp.minimum(2 * i + 1, seq // ck - 1))
        return carry

    lax.fori_loop(0, (n_slc + 1) // 2, mask_body, 0)
    o_slc = sweep(ksb, vsb, 0, n_slc, lambda c, k0: bias_scr[c])

    span = -(-(WINDOW + tq) // ck) * ck

    def win_bias(c, k0):
        dlt = qp - (k0 + kcol)
        return jnp.where((dlt >= 0) & (dlt < WINDOW), 0.0, NEG)

    o_win = sweep(kwb, vwb, jnp.maximum(q0 + tq - span, 0), span // ck, win_bias)

    for g in range(GROUP):
        r0, r1 = g * tq, (g + 1) * tq
        og = gate_scr[g] * o_cmp[r0:r1] + gate_scr[GROUP + g] * o_slc[r0:r1]
        og = og + gate_scr[2 * GROUP + g] * o_win[r0:r1]
        o_ref[:, g * HEAD_DIM:(g + 1) * HEAD_DIM] = og.astype(o_ref.dtype)


def _attn_prompt(q, kcv, kv, win, g_re, *, nb, seq, tq=256, ck=512):
    assert tq & (tq - 1) == 0 and seq % ck == 0 and WINDOW % tq == 0
    t = q.shape[0]
    ncp = seq // CMP_STRIDE
    nsb = seq // SLC_LEN
    nq = seq // tq
    mt = jnp.asarray(_overlap_matrix(ncp, nsb, ncp - 1, nsb).T)
    kk = np.arange(seq)
    assert nsb <= HEAD_DIM
    e3 = (kk[None, :] // SLC_LEN == np.arange(HEAD_DIM)[:, None]).astype(np.float32)
    e3 = jnp.asarray(e3.reshape(HEAD_DIM, seq // ck, ck).transpose(1, 0, 2), dtype=BF16)
    slab = lambda cb: pl.BlockSpec((seq, HEAD_DIM), lambda b, h, i, cb=cb: (b, cb + h))
    return pl.pallas_call(
        functools.partial(_attn_prompt_kernel, tq=tq, seq=seq, ck=ck),
        grid=(nb, N_KV, nq),
        in_specs=[pl.BlockSpec((tq, GROUP * HEAD_DIM), lambda b, h, i: (b * nq + i, h)),
                  pl.BlockSpec((None, None, None, ncp, HEAD_DIM), lambda b, h, i: (b, 0, h, 0, 0)),
                  pl.BlockSpec((None, None, None, ncp, HEAD_DIM), lambda b, h, i: (b, 1, h, 0, 0)),
                  slab(0), slab(N_KV), slab(0), slab(N_KV),
                  pl.BlockSpec((None, None, tq, 4 * GROUP), lambda b, h, i: (b, h, i, 0)),
                  pl.BlockSpec((nsb, ncp), lambda b, h, i: (0, 0)),
                  pl.BlockSpec((seq // ck, HEAD_DIM, ck), lambda b, h, i: (0, 0, 0))],
        out_specs=pl.BlockSpec((tq, GROUP * HEAD_DIM), lambda b, h, i: (b * nq + i, h)),
        out_shape=jax.ShapeDtypeStruct((t, N_HEADS * HEAD_DIM), BF16),
        scratch_shapes=[pltpu.VMEM((GROUP * tq, HEAD_DIM), F32)] * 3
        + [pltpu.VMEM((seq // ck, tq, ck), F32), pltpu.VMEM((3 * GROUP, tq, HEAD_DIM), F32)],
        compiler_params=_cparams(("arbitrary", "arbitrary", "arbitrary")),
        name="attn_prompt",
    )(q, kcv, kcv, kv, kv, win, win, g_re, mt, e3)


def _attn_sample_kernel(pt_ref, cache_ref, q_ref, kcv_ref, kvn_ref, wn_ref, st_ref, g_ref, ms_ref, e_ref, o_ref,
                        m_scr, l_scr, acc_scr, sel_scr, ocmp_scr, kv_buf, sem, *, past, ds):
    b = pl.program_id(0)
    p = pl.program_id(1)
    n_steps = pl.num_programs(1)

    def page_copies(bb, pp, slot):
        cps = []
        for n in range(PAGES_PER_STEP):
            page = pt_ref[bb, pp * PAGES_PER_STEP + n]
            for cb in range(2 * N_KV):
                cps.append(pltpu.make_async_copy(cache_ref.at[page, :, 2 * N_KV + cb, :],
                                                 kv_buf.at[slot, cb, pl.ds(n * PAGE, PAGE), :],
                                                 sem.at[slot]))
        return cps

    t = b * n_steps + p
    slot = lax.rem(t, 2)

    @pl.when(t == 0)
    def _():
        for cp in page_copies(b, p, slot):
            cp.start()

    @pl.when(t + 1 < pl.num_programs(0) * n_steps)
    def _():
        wrap = p + 1 == n_steps
        for cp in page_copies(jnp.where(wrap, b + 1, b), jnp.where(wrap, 0, p + 1), 1 - slot):
            cp.start()
    rq = GROUP * ds
    ncp = kcv_ref.shape[2]
    nc = ncp - 1
    n_sel_chunks = sel_scr.shape[1]
    lanes = n_sel_chunks * HEAD_DIM

    def q_rows(h):
        c0 = h * GROUP * HEAD_DIM
        return jnp.concatenate(
            [q_ref[:, c0 + g * HEAD_DIM:c0 + (g + 1) * HEAD_DIM] for g in range(GROUP)], axis=0).astype(BF16)

    def online_update(h, sc_, vals):
        m_i = m_scr[h]
        m_new = jnp.maximum(m_i, jnp.max(sc_, axis=1, keepdims=True))
        alpha = jnp.exp(m_i - m_new)
        pp = jnp.exp(sc_ - m_new)
        l_scr[h] = alpha * l_scr[h] + jnp.sum(pp, axis=1, keepdims=True)
        acc_scr[h] = alpha * acc_scr[h] + jnp.dot(pp.astype(BF16), vals, preferred_element_type=F32)
        m_scr[h] = m_new

    @pl.when(p == 0)
    def _():
        pgs = []
        for h in range(N_KV):
            qq = q_rows(h)
            kc = kcv_ref[0, h]
            vc = kcv_ref[1, h]
            s = lax.dot_general(qq, kc, _NT, preferred_element_type=F32) * SCALE
            qpos = past + (lax.broadcasted_iota(jnp.int32, (rq, ncp), 0) & (ds - 1))
            col = lax.broadcasted_iota(jnp.int32, (rq, ncp), 1)
            vis = (col * CMP_STRIDE + (CMP_LEN - 1) <= qpos) & (col < nc)
            s = jnp.where(vis, s, NEG)
            mx = jnp.max(s, axis=1, keepdims=True)
            e = jnp.where(vis, jnp.exp(s - mx), 0.0)
            den = jnp.sum(e, axis=1, keepdims=True)
            pr = e / jnp.where(den > 0.0, den, 1.0)
            ocmp_scr[h] = jnp.dot(pr.astype(BF16), vc, preferred_element_type=F32)
            pg = pr[0:ds]
            for g in range(1, GROUP):
                pg = pg + pr[g * ds:(g + 1) * ds]
            pgs.append(pg)
            m_scr[h] = jnp.full((rq, 1), NEG, F32)
            l_scr[h] = jnp.zeros((rq, 1), F32)
            acc_scr[h] = jnp.zeros((rq, HEAD_DIM), F32)
        nr = N_KV * ds
        pg_all = jnp.concatenate(pgs, axis=0)
        hi = pg_all.astype(BF16)
        r1 = pg_all - hi.astype(F32)
        mid = r1.astype(BF16)
        lo = (r1 - mid.astype(F32)).astype(BF16)
        sc3 = jnp.dot(jnp.concatenate([hi, mid, lo], axis=0), ms_ref[...], preferred_element_type=F32)
        score = (sc3[0:nr] + sc3[nr:2 * nr]) + sc3[2 * nr:3 * nr]
        lane = lax.broadcasted_iota(jnp.int32, (nr, lanes), 1)
        lane_in = lax.rem(lane, jnp.int32(HEAD_DIM))
        real = lane_in < BLK_PER_CHUNK
        blk = jnp.where(real, lax.div(lane, jnp.int32(HEAD_DIM)) * BLK_PER_CHUNK + lane_in, (1 << 19) + lane)
        qrow = lax.broadcasted_iota(jnp.int32, (nr, lanes), 0) & (ds - 1)
        cur = lax.div(past + qrow, jnp.int32(SLC_LEN))
        valid = real & (blk <= cur)
        forced = (blk == 0) | (blk > cur - N_LOCAL)
        sel = _select_topk(score, valid, forced, blk, 1)
        for h in range(N_KV):
            for c in range(n_sel_chunks):
                sel_scr[h, c] = sel[h * ds:(h + 1) * ds, c * HEAD_DIM:(c + 1) * HEAD_DIM]

    for cp in page_copies(b, p, slot):
        cp.wait()

    k0 = p * CHUNK_ROWS
    for h in range(N_KV):
        qq = q_rows(h)
        kch = kv_buf[slot, h].astype(BF16)
        vch = kv_buf[slot, N_KV + h].astype(BF16)
        sc_ = lax.dot_general(qq, kch, _NT, preferred_element_type=F32) * SCALE
        selq = jnp.dot(sel_scr[h, p].astype(BF16), e_ref[...], preferred_element_type=F32)
        kpos = k0 + lax.broadcasted_iota(jnp.int32, (ds, CHUNK_ROWS), 1)
        qp = past + lax.broadcasted_iota(jnp.int32, (ds, CHUNK_ROWS), 0)
        okf = jnp.where((selq > 0.5) & (kpos <= qp), 1.0, 0.0)
        ok = jnp.concatenate([okf] * GROUP, axis=0) > 0.5
        online_update(h, jnp.where(ok, sc_, NEG), vch)

    @pl.when(p == n_steps - 1)
    def _():
        gt = jax.nn.sigmoid(g_ref[...])
        zpad = jnp.zeros((HEAD_DIM - ds, HEAD_DIM), F32)
        for h in range(N_KV):
            qq = q_rows(h)
            c_k = 2 * KV_W + h * HEAD_DIM
            c_v = 3 * KV_W + h * HEAD_DIM
            kn = jnp.concatenate([kvn_ref[:, c_k:c_k + HEAD_DIM], zpad], axis=0).astype(BF16)
            vn = jnp.concatenate([kvn_ref[:, c_v:c_v + HEAD_DIM], zpad], axis=0).astype(BF16)
            sn = lax.dot_general(qq, kn, _NT, preferred_element_type=F32) * SCALE
            last_sel = sel_scr[h, n_sel_chunks - 1]
            nb_last = (past // SLC_LEN) % BLK_PER_CHUNK
            seln = jnp.sum(jnp.where(lax.broadcasted_iota(jnp.int32, (ds, HEAD_DIM), 1) == nb_last,
                                     last_sel, 0.0), axis=1, keepdims=True)
            seln = jnp.concatenate([seln] * GROUP, axis=0)
            srow = lax.broadcasted_iota(jnp.int32, (rq, HEAD_DIM), 0) & (ds - 1)
            kcol = lax.broadcasted_iota(jnp.int32, (rq, HEAD_DIM), 1)
            okn = (seln > 0.5) & (kcol <= srow) & (kcol < ds)
            online_update(h, jnp.where(okn, sn, NEG), vn)
            o_slc = acc_scr[h] / l_scr[h]
            wb = st_ref.shape[0]
            band = wb + HEAD_DIM
            st_flat = st_ref.reshape(wb * 8, HEAD_DIM)
            kw = jnp.concatenate([st_flat[pl.ds(h, wb, stride=8), :],
                                  wn_ref[:, h * HEAD_DIM:(h + 1) * HEAD_DIM], zpad], axis=0).astype(BF16)
            vw = jnp.concatenate([st_flat[pl.ds(N_KV + h, wb, stride=8), :],
                                  wn_ref[:, KV_W + h * HEAD_DIM:KV_W + (h + 1) * HEAD_DIM], zpad],
                                 axis=0).astype(BF16)
            sw = lax.dot_general(qq, kw, _NT, preferred_element_type=F32) * SCALE
            srw = lax.broadcasted_iota(jnp.int32, (rq, band), 0) & (ds - 1)
            idx = lax.broadcasted_iota(jnp.int32, (rq, band), 1)
            dlt = (wb + srw) - idx
            sw = jnp.where((dlt >= 0) & (dlt < WINDOW) & (idx < wb + ds), sw, NEG)
            ew = jnp.exp(sw - jnp.max(sw, axis=1, keepdims=True))
            o_win = jnp.dot(ew.astype(BF16), vw, preferred_element_type=F32) / jnp.sum(ew, axis=1, keepdims=True)
            o_cmp = ocmp_scr[h]
            gh = gt[h]
            for g in range(GROUP):
                r0, r1 = g * ds, (g + 1) * ds
                og = gh[:, g:g + 1] * o_cmp[r0:r1] + gh[:, GROUP + g:GROUP + g + 1] * o_slc[r0:r1]
                og = og + gh[:, 2 * GROUP + g:2 * GROUP + g + 1] * o_win[r0:r1]
                c0 = (h * GROUP + g) * HEAD_DIM
                o_ref[:, c0:c0 + HEAD_DIM] = og


def _attn_sample(q, kcv, kv_new, win_new, win_state, g_re, cache4, page_table, *, ds):
    nb, npg = page_table.shape
    past = npg * PAGE
    assert ds & (ds - 1) == 0 and ds <= SLC_LEN and past % SLC_LEN == 0 and npg % PAGES_PER_STEP == 0
    steps = npg // PAGES_PER_STEP
    ncp = kcv.shape[3]
    wb = win_state.shape[1]
    n_blocks = past // SLC_LEN + 1
    n_sel_chunks = -(-n_blocks // BLK_PER_CHUNK)
    m = _overlap_matrix(ncp, n_sel_chunks * BLK_PER_CHUNK, ncp - 1, n_blocks)
    ms = np.zeros((ncp, n_sel_chunks, HEAD_DIM), np.float32)
    ms[:, :, :BLK_PER_CHUNK] = m.reshape(ncp, n_sel_chunks, BLK_PER_CHUNK)
    ms = jnp.asarray(ms.reshape(ncp, n_sel_chunks * HEAD_DIM), dtype=BF16)
    ee = np.zeros((HEAD_DIM, CHUNK_ROWS), np.float32)
    ee[:BLK_PER_CHUNK] = np.arange(CHUNK_ROWS)[None, :] // SLC_LEN == np.arange(BLK_PER_CHUNK)[:, None]
    ee = jnp.asarray(ee, dtype=BF16)
    rq = GROUP * ds

    grid_spec = pltpu.PrefetchScalarGridSpec(
        num_scalar_prefetch=1,
        grid=(nb, steps),
        in_specs=[
            pl.BlockSpec(memory_space=pl.ANY),
            pl.BlockSpec((ds, N_HEADS * HEAD_DIM), lambda b, p, pt: (b, 0)),
            pl.BlockSpec((None, 2, N_KV, ncp, HEAD_DIM), lambda b, p, pt: (b, 0, 0, 0, 0)),
            pl.BlockSpec((ds, 4 * KV_W), lambda b, p, pt: (b, 0)),
            pl.BlockSpec((ds, 2 * KV_W), lambda b, p, pt: (b, 0)),
            pl.BlockSpec((None, wb, 8, HEAD_DIM), lambda b, p, pt: (b, 0, 0, 0)),
            pl.BlockSpec((None, N_KV, ds, 4 * GROUP), lambda b, p, pt: (b, 0, 0, 0)),
            pl.BlockSpec(ms.shape, lambda b, p, pt: (0, 0)),
            pl.BlockSpec(ee.shape, lambda b, p, pt: (0, 0))],
        out_specs=pl.BlockSpec((ds, N_HEADS * HEAD_DIM), lambda b, p, pt: (b, 0)),
        scratch_shapes=[pltpu.VMEM((N_KV, rq, 1), F32), pltpu.VMEM((N_KV, rq, 1), F32),
                        pltpu.VMEM((N_KV, rq, HEAD_DIM), F32),
                        pltpu.VMEM((N_KV, n_sel_chunks, ds, HEAD_DIM), F32),
                        pltpu.VMEM((N_KV, rq, HEAD_DIM), F32),
                        pltpu.VMEM((2, 2 * N_KV, CHUNK_ROWS, HEAD_DIM), F32),
                        pltpu.SemaphoreType.DMA((2,))],
    )
    return pl.pallas_call(
        functools.partial(_attn_sample_kernel, past=past, ds=ds),
        grid_spec=grid_spec,
        out_shape=jax.ShapeDtypeStruct((nb * ds, N_HEADS * HEAD_DIM), F32),
        compiler_params=_cparams(("arbitrary", "arbitrary")),
        name="attn_sample",
    )(page_table, cache4, q, kcv, kv_new, win_new, win_state, g_re, ms, ee)


def _split_w_in(w_in, tc):
    d = D_MODEL
    o = 0
    parts = {}
    for name, width in (("q", N_HEADS * HEAD_DIM), ("kv", 4 * KV_W), ("win", 2 * KV_W), ("g", 3 * N_HEADS),
                        ("u", d), ("b", d), ("c", d), ("ma", d), ("mc", d)):
        parts[name] = w_in[:, o:o + width]
        o += width
    nj = d // tc
    ubc = jnp.stack([parts[n].reshape(d, nj, tc) for n in ("u", "b", "c")], axis=2).reshape(d, 3 * d)
    g_pad = jnp.pad(parts["g"], ((0, 0), (0, HEAD_DIM - 3 * N_HEADS)))
    gates = jnp.concatenate([parts["ma"], parts["mc"]], axis=1)
    return {"q": parts["q"].astype(BF16), "kv": parts["kv"].astype(BF16), "win": parts["win"].astype(BF16),
            "g": g_pad.astype(BF16), "ubc": ubc.astype(BF16), "gates": gates.astype(BF16)}


def _gate_layout(g_logits, nb, seq):
    g = g_logits[:, :3 * N_HEADS].reshape(nb, seq, 3, N_KV, GROUP)
    g = g.transpose(0, 3, 1, 2, 4).reshape(nb, N_KV, seq, 3 * GROUP)
    return jnp.pad(g, ((0, 0), (0, 0), (0, 0), (0, GROUP)))


def _cmp_weights(w_k1, w_k2, pe_k, w_v1, w_v2, pe_v):
    def by_row(w1):
        w = w1.reshape(2, CMP_STRIDE, HEAD_DIM, HEAD_DIM).transpose(1, 2, 0, 3)
        return w.reshape(CMP_STRIDE * HEAD_DIM, 2 * HEAD_DIM)
    w1pairs = jnp.concatenate([by_row(w_k1), by_row(w_v1)], axis=1).astype(BF16)
    pe8 = jnp.stack([jnp.broadcast_to(pe_k.reshape(1, -1), (8, CMP_LEN * HEAD_DIM)),
                     jnp.broadcast_to(pe_v.reshape(1, -1), (8, CMP_LEN * HEAD_DIM))])
    return w1pairs, pe8, jnp.stack([w_k1, w_v1]), jnp.stack([w_k2, w_v2])


def kernel(x_prompt, x_sample, c_prompt, c_sample, cache_nsa_kv, page_table, state_win_kv, state_conv, w_ada, b_ada, norm1_g, norm2_g, w_in, w_cmp_k1, w_cmp_k2, pe_cmp_k, w_cmp_v1, w_cmp_v2, pe_cmp_v, conv_w, conv_b, w_attn_proj, w_conv_proj, w_out, w_mlp1, w_mlp2, normf_g):
    d = D_MODEL
    nbp, seq, _ = x_prompt.shape
    nbs, ds, _ = x_sample.shape
    tp = nbp * seq
    ts = nbs * ds
    tc = 256
    depth = w_in.shape[0]
    assert depth == 1

    xp = x_prompt.reshape(tp, d)
    xs = x_sample.reshape(ts, d)
    normf = normf_g.reshape(1, d)

    l = 0
    c_all = jnp.concatenate([c_prompt, c_sample, jnp.zeros((16 - nbp - nbs, d), F32)], axis=0)
    ada = _ada(c_all, w_ada[l], b_ada[l].reshape(1, -1)).reshape(16, 6, d)
    ada_p = [ada[:nbp, k][:, None, :] for k in range(6)]
    ada_s = [jnp.repeat(ada[nbp:nbp + nbs, k], ds, axis=0)[None] for k in range(6)]

    w = _split_w_in(w_in[l], tc)
    wa = w_attn_proj[l].astype(BF16)
    wc = w_conv_proj[l].astype(BF16)
    wo = w_out[l].astype(BF16)
    w1 = w_mlp1[l].astype(BF16)
    w2 = w_mlp2[l].astype(BF16)
    g1n = norm1_g[l].reshape(1, d)
    g2n = norm2_g[l].reshape(1, d)
    cw = conv_w[l]
    cb = conv_b[l].reshape(1, d)
    w1pairs, pe8, w1s, w2s = _cmp_weights(w_cmp_k1[l], w_cmp_k2[l], pe_cmp_k[l],
                                          w_cmp_v1[l], w_cmp_v2[l], pe_cmp_v[l])

    tm = 1024
    hp = _norm_mod(xp, g1n, ada_p[1], ada_p[0], tm=256, seq=seq)
    q_p = _mm(hp, w["q"], tm=tm, tn=1024, out_dtype=BF16, name="mm_q")
    wbp = min(WINDOW, seq)
    kv6_p, kvb_p = _mm_kv(hp, w["kv"], tm=tm)
    winb_p, win6_p = _mm_win(hp, w["win"], tm=wbp, seq=seq)
    g_p = _mm(hp, w["g"], tm=tm, tn=HEAD_DIM, out_dtype=F32, name="mm_g")
    gates_p = _mm(hp, w["gates"], tm=tm, tn=1024, out_dtype=BF16, act="sigmoid", name="mm_gates")
    z_p, conv_tiles = _mm_conv(hp, w["ubc"], cw, cb, tm=tm, tc=tc, seq=seq)

    pt_p = jnp.arange(tp // PAGE, dtype=jnp.int32).reshape(nbp, seq // PAGE)
    part_p = _cmp_part(kv6_p.reshape(tp // PAGE, PAGE, 16, HEAD_DIM), pt_p, w1pairs)
    kcv_p = _cmp_finish(part_p, pe8, w1s, w2s)
    o_p = _attn_prompt(q_p, kcv_p, kvb_p, winb_p, _gate_layout(g_p, nbp, seq), nb=nbp, seq=seq)

    mixed_p = _mix(o_p, z_p, wa, wc, gates_p, tm=tm, tn=512)
    x1_p = _mm_resid(mixed_p, wo, xp, ada_p[2], tm=tm, tn=1024, seq=seq)
    h2_p = _norm_mod(x1_p, g2n, ada_p[4], ada_p[3], tm=256, seq=seq)
    y_p = _mlp_final(h2_p, w1, w2, x1_p, ada_p[5], normf, tm=512, tf=1024, seq=seq)

    hs = _norm_mod(xs, g1n, ada_s[1], ada_s[0], tm=ts, seq=ds)
    q_s = _mm(hs, w["q"], tm=ts, tn=512, out_dtype=F32, name="mm_q_s")
    kv_s = _mm(hs, w["kv"], tm=ts, tn=512, out_dtype=F32, name="mm_kv_s")
    win_s = _mm(hs, w["win"], tm=ts, tn=512, out_dtype=F32, name="mm_win_s")
    g_s = _mm(hs, w["g"], tm=ts, tn=HEAD_DIM, out_dtype=F32, name="mm_g_s")
    gates_s = _mm(hs, w["gates"], tm=ts, tn=512, out_dtype=F32, act="sigmoid", name="mm_gates_s")
    ubc_s = _mm(hs, w["ubc"], tm=ts, tn=3 * tc, out_dtype=F32, name="mm_ubc_s")
    st = state_conv[l]
    z_s, cu_s = _sample_conv(ubc_s, jnp.repeat(st[:, 0], ds, axis=0), jnp.repeat(st[:, 1], ds, axis=0),
                             cw, cb, tc=tc, seq=ds)

    cache4 = cache_nsa_kv[l].reshape(cache_nsa_kv.shape[1], PAGE, 16, HEAD_DIM)
    part_s = _cmp_part(cache4, page_table, w1pairs)
    kcv_s = _cmp_finish(part_s, pe8, w1s, w2s)
    wbs = state_win_kv.shape[2]
    wst = state_win_kv[l].reshape(nbs, wbs, 8, HEAD_DIM)
    o_s = _attn_sample(q_s, kcv_s, kv_s, win_s, wst, _gate_layout(g_s, nbs, ds), cache4, page_table, ds=ds)

    mixed_s = _mix(o_s, z_s, wa, wc, gates_s, tm=ts, tn=512)
    x1_s = _mm_resid(mixed_s, wo, xs, ada_s[2], tm=ts, tn=512, seq=ds)
    h2_s = _norm_mod(x1_s, g2n, ada_s[4], ada_s[3], tm=ts, seq=ds)
    y_s = _mlp_final(h2_s, w1, w2, x1_s, ada_s[5], normf, tm=ts, tf=512, seq=ds)

    kv_prompt = kv6_p.reshape(1, nbp, seq, 4, N_KV, HEAD_DIM)
    kv_sample = kv_s.reshape(1, nbs, ds, 4, N_KV, HEAD_DIM)
    win_prompt = win6_p.reshape(1, nbp, wbp, 2, N_KV, HEAD_DIM)
    win_sample = jnp.concatenate([state_win_kv[l][:, ds:], win_s.reshape(nbs, ds, 2, N_KV, HEAD_DIM)], axis=1)[None]
    tiles_per_seq = seq // tm
    conv_prompt = conv_tiles[tiles_per_seq - 1::tiles_per_seq][None]
    conv_sample = cu_s.reshape(nbs, ds, d)[None, :, ds - (CONV_W - 1):]
    return (y_p.reshape(nbp, seq, d), y_s.reshape(nbs, ds, d), kv_prompt, kv_sample,
            win_prompt, win_sample, conv_prompt, conv_sample)
```

```python
import functools

import numpy as np
import jax
import jax.numpy as jnp
from jax import lax
from jax.experimental import pallas as pl
from jax.experimental.pallas import tpu as pltpu

F32 = jnp.float32
BF16 = jnp.bfloat16

D_MODEL = 2048
HEAD_DIM = 128
N_HEADS = 16
N_KV = 4
GROUP = 4
KV_W = N_KV * HEAD_DIM
CMP_LEN = 32
CMP_STRIDE = 16
SLC_LEN = 64
SLC_TOP = 16
N_LOCAL = 2
WINDOW = 512
CONV_W = 3
RMS_EPS = 1e-6
NEG = -1e30
SCALE = HEAD_DIM ** -0.5
PAGE = 128
PAGES_PER_STEP = 16
CHUNK_ROWS = PAGES_PER_STEP * PAGE
BLK_PER_CHUNK = CHUNK_ROWS // SLC_LEN
VMEM_LIMIT = 56 * 1024 * 1024

_NT = (((1,), (1,)), ((), ()))
_TN = (((0,), (0,)), ((), ()))


def _cparams(sem):
    return pltpu.CompilerParams(dimension_semantics=sem, vmem_limit_bytes=VMEM_LIMIT)


def _ada_kernel(c_ref, w_ref, b_ref, o_ref):
    c = c_ref[...]
    a = (c * jax.nn.sigmoid(c)).astype(BF16)
    o_ref[...] = jnp.dot(a, w_ref[...].astype(BF16), preferred_element_type=F32) + b_ref[...]


def _ada(c, w, b, tn=1024):
    m, k = c.shape
    n = w.shape[1]
    return pl.pallas_call(
        _ada_kernel,
        grid=(n // tn,),
        in_specs=[pl.BlockSpec((m, k), lambda j: (0, 0)),
                  pl.BlockSpec((k, tn), lambda j: (0, j)),
                  pl.BlockSpec((1, tn), lambda j: (0, j))],
        out_specs=pl.BlockSpec((m, tn), lambda j: (0, j)),
        out_shape=jax.ShapeDtypeStruct((m, n), F32),
        compiler_params=_cparams(("arbitrary",)),
        name="ada",
    )(c, w, b)


def _rowspec(p, tm, tn, seq, col=True):
    gr = p.shape[1]
    if gr == 1:
        if col:
            return pl.BlockSpec((1, 1, tn), lambda i, j: ((i * tm) // seq, 0, j))
        return pl.BlockSpec((1, 1, tn), lambda i: ((i * tm) // seq, 0, 0))
    if col:
        return pl.BlockSpec((1, tm, tn), lambda i, j: (0, i, j))
    return pl.BlockSpec((1, tm, tn), lambda i: (0, i, 0))


def _norm_kernel(x_ref, g_ref, sc_ref, sh_ref, o_ref):
    x = x_ref[...]
    r = lax.rsqrt(jnp.mean(x * x, axis=-1, keepdims=True) + RMS_EPS)
    y = (x * r) * g_ref[...]
    o_ref[...] = (y * (1.0 + sc_ref[0]) + sh_ref[0]).astype(o_ref.dtype)


def _norm_mod(x, g, sc, sh, *, tm, seq):
    t, d = x.shape
    return pl.pallas_call(
        _norm_kernel,
        grid=(t // tm,),
        in_specs=[pl.BlockSpec((tm, d), lambda i: (i, 0)),
                  pl.BlockSpec((1, d), lambda i: (0, 0)),
                  _rowspec(sc, tm, d, seq, col=False),
                  _rowspec(sh, tm, d, seq, col=False)],
        out_specs=pl.BlockSpec((tm, d), lambda i: (i, 0)),
        out_shape=jax.ShapeDtypeStruct((t, d), BF16),
        compiler_params=_cparams(("arbitrary",)),
        name="norm_mod",
    )(x, g, sc, sh)


def _mm_kernel(x_ref, w_ref, o_ref, *, act):
    acc = jnp.dot(x_ref[...].astype(BF16), w_ref[...], preferred_element_type=F32)
    if act == "sigmoid":
        acc = jax.nn.sigmoid(acc)
    o_ref[...] = acc.astype(o_ref.dtype)


def _mm(x, w, *, tm, tn, out_dtype, act=None, name="mm", col0=0, n=None):
    t, k = x.shape
    n = w.shape[1] if n is None else n
    assert col0 % tn == 0 and n % tn == 0
    cb0 = col0 // tn
    return pl.pallas_call(
        functools.partial(_mm_kernel, act=act),
        grid=(t // tm, n // tn),
        in_specs=[pl.BlockSpec((tm, k), lambda i, j: (i, 0)),
                  pl.BlockSpec((k, tn), lambda i, j: (0, cb0 + j))],
        out_specs=pl.BlockSpec((tm, tn), lambda i, j: (i, j)),
        out_shape=jax.ShapeDtypeStruct((t, n), out_dtype),
        compiler_params=_cparams(("arbitrary", "arbitrary")),
        name=name,
    )(x, w)


def _store_head_major(o_ref, acc, tm):
    flat = o_ref.reshape(tm * 8, HEAD_DIM)
    for cb in range(8):
        flat[pl.ds(cb, tm, stride=8), :] = acc[:, cb * HEAD_DIM:(cb + 1) * HEAD_DIM]


def _mm_kv_kernel(x_ref, w_ref, o6_ref, ob_ref, *, tm):
    acc = jnp.dot(x_ref[...], w_ref[...], preferred_element_type=F32)
    _store_head_major(o6_ref, acc, tm)

    @pl.when(pl.program_id(1) == 1)
    def _():
        ob_ref[...] = acc.astype(BF16)


def _mm_kv(h, w, *, tm, col0):
    t, k = h.shape
    tn = 2 * KV_W
    cb0 = col0 // tn
    return pl.pallas_call(
        functools.partial(_mm_kv_kernel, tm=tm),
        grid=(t // tm, 2),
        in_specs=[pl.BlockSpec((tm, k), lambda i, j: (i, 0)),
                  pl.BlockSpec((k, tn), lambda i, j: (0, cb0 + j))],
        out_specs=[pl.BlockSpec((tm, 8, HEAD_DIM), lambda i, j: (i, j, 0)),
                   pl.BlockSpec((tm, tn), lambda i, j: (i, 0))],
        out_shape=[jax.ShapeDtypeStruct((t, 16, HEAD_DIM), F32),
                   jax.ShapeDtypeStruct((t, tn), BF16)],
        compiler_params=_cparams(("arbitrary", "arbitrary")),
        name="mm_kv",
    )(h, w)


def _mm_win_kernel(x_ref, w_ref, ob_ref, o6_ref, *, tm, seq):
    acc = jnp.dot(x_ref[...], w_ref[...], preferred_element_type=F32)
    ob_ref[...] = acc.astype(BF16)

    @pl.when(((pl.program_id(0) + 1) * tm) % seq == 0)
    def _():
        _store_head_major(o6_ref, acc, tm)


def _mm_win(h, w, *, tm, seq, col0):
    t, k = h.shape
    tn = 2 * KV_W
    cb0 = col0 // tn
    return pl.pallas_call(
        functools.partial(_mm_win_kernel, tm=tm, seq=seq),
        grid=(t // tm,),
        in_specs=[pl.BlockSpec((tm, k), lambda i: (i, 0)),
                  pl.BlockSpec((k, tn), lambda i: (0, cb0))],
        out_specs=[pl.BlockSpec((tm, tn), lambda i: (i, 0)),
                   pl.BlockSpec((tm, 8, HEAD_DIM), lambda i: ((i * tm) // seq, 0, 0))],
        out_shape=[jax.ShapeDtypeStruct((t, tn), BF16),
                   jax.ShapeDtypeStruct((t // seq * tm, 8, HEAD_DIM), F32)],
        compiler_params=_cparams(("arbitrary",)),
        name="mm_win",
    )(h, w)


def _conv_taps(cu, prev1, prev2, bgate, cw_ref, cb_ref):
    y = cb_ref[...] + prev2 * cw_ref[0:1, :]
    y = y + prev1 * cw_ref[1:2, :]
    y = y + cu * cw_ref[2:3, :]
    return bgate * y


def _mm_conv_kernel(x_ref, wu_ref, wb_ref, wc_ref, cw_ref, cb_ref, z_ref, st_ref, carry_ref, *, tm, tc, seq):
    i = pl.program_id(0)
    j = pl.program_id(1)
    w3 = jnp.concatenate([wu_ref[...], wb_ref[...], wc_ref[...]], axis=1)
    acc = jnp.dot(x_ref[...], w3, preferred_element_type=F32)
    u = acc[:, 0:tc]
    bgate = acc[:, tc:2 * tc]
    cu = acc[:, 2 * tc:3 * tc] * u
    @pl.when((i * tm) % seq == 0)
    def _():
        carry_ref[j] = jnp.zeros((8, tc), F32)

    car = carry_ref[j]
    p0 = car[0:1, :]
    p1 = car[1:2, :]
    rows = lax.broadcasted_iota(jnp.int32, (tm, tc), 0)
    r1 = pltpu.roll(cu, 1, 0)
    r2 = pltpu.roll(cu, 2, 0)
    prev1 = jnp.where(rows == 0, p1, r1)
    prev2 = jnp.where(rows == 0, p0, jnp.where(rows == 1, p1, r2))
    z_ref[...] = _conv_taps(cu, prev1, prev2, bgate, cw_ref, cb_ref).astype(z_ref.dtype)
    tail = cu[tm - 8:tm, :]
    carry_ref[j] = jnp.concatenate([tail[6:8, :], tail[0:6, :]], axis=0)
    st_ref[0] = tail[6:8, :]


def _mm_conv(h, w, conv_w, conv_b, *, tm, tc, seq, col0):
    t, k = h.shape
    d = conv_w.shape[1]
    nj = d // tc
    cb0 = col0 // tc
    wspec = lambda part: pl.BlockSpec((k, tc), lambda i, j: (0, cb0 + part * nj + j))
    return pl.pallas_call(
        functools.partial(_mm_conv_kernel, tm=tm, tc=tc, seq=seq),
        grid=(t // tm, nj),
        in_specs=[pl.BlockSpec((tm, k), lambda i, j: (i, 0)),
                  wspec(0), wspec(1), wspec(2),
                  pl.BlockSpec((CONV_W, tc), lambda i, j: (0, j)),
                  pl.BlockSpec((1, tc), lambda i, j: (0, j))],
        out_specs=[pl.BlockSpec((tm, tc), lambda i, j: (i, j)),
                   pl.BlockSpec((1, CONV_W - 1, tc), lambda i, j: (i, 0, j))],
        out_shape=[jax.ShapeDtypeStruct((t, d), BF16),
                   jax.ShapeDtypeStruct((t // tm, CONV_W - 1, d), F32)],
        scratch_shapes=[pltpu.VMEM((nj, 8, tc), F32)],
        compiler_params=_cparams(("arbitrary", "arbitrary")),
        name="mm_conv",
    )(h, w, w, w, conv_w, conv_b)


def _sample_conv_kernel(u_ref, b_ref, c_ref, st0_ref, st1_ref, cw_ref, cb_ref, z_ref, cu_ref, *, tc, seq):
    t = u_ref.shape[0]
    bgate = b_ref[...]
    cu = c_ref[...] * u_ref[...]
    s = lax.rem(lax.broadcasted_iota(jnp.int32, (t, tc), 0), jnp.int32(seq))
    r1 = pltpu.roll(cu, 1, 0)
    r2 = pltpu.roll(cu, 2, 0)
    prev1 = jnp.where(s == 0, st1_ref[...], r1)
    prev2 = jnp.where(s == 0, st0_ref[...], jnp.where(s == 1, st1_ref[...], r2))
    z_ref[...] = _conv_taps(cu, prev1, prev2, bgate, cw_ref, cb_ref)
    cu_ref[...] = cu


def _sample_conv(ubc, st0, st1, conv_w, conv_b, *, tc, seq):
    t = ubc.shape[0]
    d = ubc.shape[1] // 3
    nj = d // tc
    return pl.pallas_call(
        functools.partial(_sample_conv_kernel, tc=tc, seq=seq),
        grid=(nj,),
        in_specs=[pl.BlockSpec((t, tc), lambda j: (0, j)),
                  pl.BlockSpec((t, tc), lambda j: (0, nj + j)),
                  pl.BlockSpec((t, tc), lambda j: (0, 2 * nj + j)),
                  pl.BlockSpec((t, tc), lambda j: (0, j)),
                  pl.BlockSpec((t, tc), lambda j: (0, j)),
                  pl.BlockSpec((CONV_W, tc), lambda j: (0, j)),
                  pl.BlockSpec((1, tc), lambda j: (0, j))],
        out_specs=[pl.BlockSpec((t, tc), lambda j: (0, j)),
                   pl.BlockSpec((t, tc), lambda j: (0, j))],
        out_shape=[jax.ShapeDtypeStruct((t, d), F32), jax.ShapeDtypeStruct((t, d), F32)],
        compiler_params=_cparams(("arbitrary",)),
        name="sample_conv",
    )(ubc, ubc, ubc, st0, st1, conv_w, conv_b)


def _mix_kernel(o_ref, z_ref, wa_ref, wc_ref, ga_ref, gc_ref, m_ref):
    a = jnp.dot(o_ref[...].astype(BF16), wa_ref[...], preferred_element_type=F32)
    c = jnp.dot(z_ref[...].astype(BF16), wc_ref[...], preferred_element_type=F32)
    m_ref[...] = (ga_ref[...].astype(F32) * a + gc_ref[...].astype(F32) * c).astype(m_ref.dtype)


def _mix(o, z, wa, wc, gates, *, tm, tn):
    t, k = o.shape
    n = wa.shape[1]
    nj = n // tn
    return pl.pallas_call(
        _mix_kernel,
        grid=(t // tm, nj),
        in_specs=[pl.BlockSpec((tm, k), lambda i, j: (i, 0)),
                  pl.BlockSpec((tm, k), lambda i, j: (i, 0)),
                  pl.BlockSpec((k, tn), lambda i, j: (0, j)),
                  pl.BlockSpec((k, tn), lambda i, j: (0, j)),
                  pl.BlockSpec((tm, tn), lambda i, j: (i, j)),
                  pl.BlockSpec((tm, tn), lambda i, j: (i, j + nj))],
        out_specs=pl.BlockSpec((tm, tn), lambda i, j: (i, j)),
        out_shape=jax.ShapeDtypeStruct((t, n), BF16),
        compiler_params=_cparams(("arbitrary", "arbitrary")),
        name="mix",
    )(o, z, wa, wc, gates, gates)


def _resid_kernel(a_ref, w_ref, x_ref, g_ref, o_ref):
    acc = jnp.dot(a_ref[...], w_ref[...], preferred_element_type=F32)
    o_ref[...] = x_ref[...] + g_ref[0] * acc


def _mm_resid(a, w, x, gate, *, tm, tn, seq):
    t, k = a.shape
    n = w.shape[1]
    return pl.pallas_call(
        _resid_kernel,
        grid=(t // tm, n // tn),
        in_specs=[pl.BlockSpec((tm, k), lambda i, j: (i, 0)),
                  pl.BlockSpec((k, tn), lambda i, j: (0, j)),
                  pl.BlockSpec((tm, tn), lambda i, j: (i, j)),
                  _rowspec(gate, tm, tn, seq)],
        out_specs=pl.BlockSpec((tm, tn), lambda i, j: (i, j)),
        out_shape=jax.ShapeDtypeStruct((t, n), F32),
        compiler_params=_cparams(("arbitrary", "arbitrary")),
        name="mm_resid",
    )(a, w, x, gate)


def _mlp_kernel(h_ref, w1_ref, w2_ref, x_ref, g_ref, nf_ref, o_ref):
    k = pl.program_id(1)

    @pl.when(k == 0)
    def _():
        o_ref[...] = jnp.zeros_like(o_ref)

    a = jnp.dot(h_ref[...], w1_ref[...], preferred_element_type=F32)
    a = jnp.square(jnp.maximum(a, 0.0)).astype(BF16)
    o_ref[...] += jnp.dot(a, w2_ref[...], preferred_element_type=F32)

    @pl.when(k == pl.num_programs(1) - 1)
    def _():
        x2 = x_ref[...] + g_ref[0] * o_ref[...]
        r = lax.rsqrt(jnp.mean(x2 * x2, axis=-1, keepdims=True) + RMS_EPS)
        o_ref[...] = (x2 * r) * nf_ref[...]


def _mlp_final(h2, w1, w2, x1, gate, normf, *, tm, tf, seq):
    t, d = h2.shape
    f = w1.shape[1]
    return pl.pallas_call(
        _mlp_kernel,
        grid=(t // tm, f // tf),
        in_specs=[pl.BlockSpec((tm, d), lambda i, k: (i, 0)),
                  pl.BlockSpec((d, tf), lambda i, k: (0, k)),
                  pl.BlockSpec((tf, d), lambda i, k: (k, 0)),
                  pl.BlockSpec((tm, d), lambda i, k: (i, 0), pipeline_mode=pl.Buffered(1)),
                  _rowspec_k(gate, tm, d, seq),
                  pl.BlockSpec((1, d), lambda i, k: (0, 0))],
        out_specs=pl.BlockSpec((tm, d), lambda i, k: (i, 0)),
        out_shape=jax.ShapeDtypeStruct((t, d), F32),
        compiler_params=_cparams(("arbitrary", "arbitrary")),
        name="mlp",
    )(h2, w1, w2, x1, gate, normf)


def _rowspec_k(p, tm, d, seq):
    if p.shape[1] == 1:
        return pl.BlockSpec((1, 1, d), lambda i, k: ((i * tm) // seq, 0, 0))
    return pl.BlockSpec((1, tm, d), lambda i, k: (0, i, 0))


def _cmp_part_kernel(pt_ref, *refs):
    pages = refs[:PAGES_PER_STEP]
    w_ref = refs[PAGES_PER_STEP]
    o_ref = refs[PAGES_PER_STEP + 1]
    stage = refs[PAGES_PER_STEP + 2]
    nchunk = CHUNK_ROWS // CMP_STRIDE
    cpp = PAGE // CMP_STRIDE
    nslab = 2 * N_KV
    cols = [jnp.concatenate([pg[pl.ds(r, cpp, stride=CMP_STRIDE)].reshape(cpp * nslab, HEAD_DIM)
                             for pg in pages], axis=0).astype(BF16) for r in range(CMP_STRIDE)]
    lhs = jnp.concatenate(cols, axis=1)
    acc = jnp.dot(lhs, w_ref[...], preferred_element_type=F32)
    slab = lax.broadcasted_iota(jnp.int32, (nchunk * nslab, 2 * HEAD_DIM), 0) & (nslab - 1)
    sel = jnp.where(slab < N_KV, acc[:, :2 * HEAD_DIM], acc[:, 2 * HEAD_DIM:])
    stage[0] = sel[:, :HEAD_DIM]
    stage[1] = sel[:, HEAD_DIM:]
    for cb in range(nslab):
        o_ref[cb // N_KV, cb % N_KV] = jnp.concatenate(
            [stage[0, pl.ds(cb, nchunk, stride=nslab), :], stage[1, pl.ds(cb, nchunk, stride=nslab), :]], axis=1)


def _page_spec(n, kind_pair):
    return pl.BlockSpec((None, PAGE, 8, HEAD_DIM),
                        lambda b, p, pt, n=n: (pt[b, p * PAGES_PER_STEP + n], 0, kind_pair, 0))


def _cmp_part(pages4, page_table, w1pairs):
    nb, npg = page_table.shape
    steps = npg // PAGES_PER_STEP
    nchunk = PAGES_PER_STEP * (PAGE // CMP_STRIDE)
    grid_spec = pltpu.PrefetchScalarGridSpec(
        num_scalar_prefetch=1,
        grid=(nb, steps),
        in_specs=[_page_spec(n, 0) for n in range(PAGES_PER_STEP)]
        + [pl.BlockSpec((CMP_STRIDE * HEAD_DIM, 4 * HEAD_DIM), lambda b, p, pt: (0, 0))],
        out_specs=pl.BlockSpec((None, 2, N_KV, nchunk, 2 * HEAD_DIM), lambda b, p, pt: (b, 0, 0, p, 0)),
        scratch_shapes=[pltpu.VMEM((2, nchunk * 2 * N_KV, HEAD_DIM), F32)],
    )
    return pl.pallas_call(
        _cmp_part_kernel,
        grid_spec=grid_spec,
        out_shape=jax.ShapeDtypeStruct((nb, 2, N_KV, steps * nchunk, 2 * HEAD_DIM), F32),
        compiler_params=_cparams(("arbitrary", "arbitrary")),
        name="cmp_part",
    )(page_table, *([pages4] * PAGES_PER_STEP), w1pairs)


def _gelu_tanh(x):
    c = np.sqrt(2.0 / np.pi).astype(np.float32)
    return 0.5 * x * (1.0 + jnp.tanh(c * (x + 0.044715 * (x * x * x))))


def _cmp_finish_kernel(part_ref, pe_ref, w1_ref, w2_ref, o_ref, bias_ref):
    @pl.when((pl.program_id(1) == 0) & (pl.program_id(2) == 0))
    def _():
        bias_ref[...] = jnp.dot(pe_ref[...], w1_ref[...], preferred_element_type=F32,
                                precision=lax.Precision.HIGHEST)

    part = part_ref[...]
    n = part.shape[0]
    h = part[:, 0:HEAD_DIM] + pltpu.roll(part[:, HEAD_DIM:], n - 1, 0)
    h = h + bias_ref[0:1, :]
    o_ref[...] = jnp.dot(_gelu_tanh(h).astype(BF16), w2_ref[...].astype(BF16),
                         preferred_element_type=F32).astype(o_ref.dtype)


def _cmp_finish(part, pe8, w1, w2):
    nb, _, _, nchunk, _ = part.shape
    return pl.pallas_call(
        _cmp_finish_kernel,
        grid=(2, nb, N_KV),
        in_specs=[pl.BlockSpec((None, None, None, nchunk, 2 * HEAD_DIM), lambda k, b, h: (b, k, h, 0, 0)),
                  pl.BlockSpec((None, 8, CMP_LEN * HEAD_DIM), lambda k, b, h: (k, 0, 0)),
                  pl.BlockSpec((None, CMP_LEN * HEAD_DIM, HEAD_DIM), lambda k, b, h: (k, 0, 0)),
                  pl.BlockSpec((None, HEAD_DIM, HEAD_DIM), lambda k, b, h: (k, 0, 0))],
        out_specs=pl.BlockSpec((None, None, None, nchunk, HEAD_DIM), lambda k, b, h: (b, k, h, 0, 0)),
        out_shape=jax.ShapeDtypeStruct((nb, 2, N_KV, nchunk, HEAD_DIM), BF16),
        scratch_shapes=[pltpu.VMEM((8, HEAD_DIM), F32)],
        compiler_params=_cparams(("arbitrary", "arbitrary", "arbitrary")),
        name="cmp_finish",
    )(part, pe8, w1, w2)


def _select_topk(score, valid, forced, blk, axis):
    sc = jnp.where(valid, jnp.where(forced, jnp.inf, score), -jnp.inf)
    sel = jnp.zeros(score.shape, F32)
    big = jnp.int32(1 << 20)
    for _ in range(SLC_TOP):
        mx = jnp.max(sc, axis=axis, keepdims=True)
        idx = jnp.min(jnp.where(sc == mx, blk, big), axis=axis, keepdims=True)
        hit = blk == idx
        sel = jnp.where(hit & (mx > -jnp.inf), 1.0, sel)
        sc = jnp.where(hit, -jnp.inf, sc)
    return sel


def _overlap_matrix(nc_pad, n_blocks_pad, nc, n_blocks):
    cs = np.arange(nc_pad) * CMP_STRIDE
    sb = np.arange(n_blocks_pad) * SLC_LEN
    ov = np.clip(np.minimum(cs[:, None] + CMP_LEN, sb[None, :] + SLC_LEN)
                 - np.maximum(cs[:, None], sb[None, :]), 0, None)
    m = (ov / CMP_STRIDE).astype(np.float32)
    m[nc:, :] = 0.0
    m[:, n_blocks:] = 0.0
    return m


def _attn_prompt_kernel(q_ref, kcb, vcb, ksb, vsb, kwb, vwb, g_ref, mt_ref, e_ref, o_ref,
                        m_scr, l_scr, acc_scr, bias_scr, gate_scr, *, tq, seq, ck):
    qi = pl.program_id(2)
    nc = seq // CMP_STRIDE - 1
    nsb = seq // SLC_LEN
    rb = min(tq, 128)

    q = q_ref[...]
    qq = jnp.concatenate([q[:, g * HEAD_DIM:(g + 1) * HEAD_DIM] for g in range(GROUP)], axis=0)
    rq = GROUP * tq
    q0 = qi * tq

    gt = jax.nn.sigmoid(g_ref[...])
    for c in range(3 * GROUP):
        gate_scr[c] = jnp.broadcast_to(gt[:, c:c + 1], (tq, HEAD_DIM))

    ncp = kcb.shape[0]
    s = lax.dot_general(qq, kcb[...], _NT, preferred_element_type=F32) * SCALE
    qpos = q0 + (lax.broadcasted_iota(jnp.int32, (rq, ncp), 0) & (tq - 1))
    col = lax.broadcasted_iota(jnp.int32, (rq, ncp), 1)
    vis = (col * CMP_STRIDE + (CMP_LEN - 1) <= qpos) & (col < nc)
    s = jnp.where(vis, s, NEG)
    mx = jnp.max(s, axis=1, keepdims=True)
    e = jnp.where(vis, jnp.exp(s - mx), 0.0)
    den = jnp.sum(e, axis=1, keepdims=True)
    p = e / jnp.where(den > 0.0, den, 1.0)
    o_cmp = jnp.dot(p.astype(BF16), vcb[...], preferred_element_type=F32)
    pg = p[0:tq]
    for g in range(1, GROUP):
        pg = pg + p[g * tq:(g + 1) * tq]

    sct = lax.dot_general(mt_ref[...], pg, _NT, preferred_element_type=F32,
                          precision=lax.Precision.HIGHEST)
    blk = lax.broadcasted_iota(jnp.int32, (nsb, tq), 0)
    cur = lax.div(q0 + lax.broadcasted_iota(jnp.int32, (nsb, tq), 1), jnp.int32(SLC_LEN))
    valid = blk <= cur
    forced = (blk == 0) | (blk > cur - N_LOCAL)
    need_topk = (q0 + tq - 1) // SLC_LEN + 1 > SLC_TOP
    sel_t = lax.cond(need_topk,
                     lambda: _select_topk(sct, valid, forced, blk, 0),
                     lambda: valid.astype(F32)).astype(BF16)

    def sweep(k_ref, v_ref, k_start, n_chunks, bias_fn):
        m_scr[...] = jnp.full((rq, HEAD_DIM), NEG, F32)
        l_scr[...] = jnp.zeros((rq, HEAD_DIM), F32)
        acc_scr[...] = jnp.zeros((rq, HEAD_DIM), F32)
        nl = ck // HEAD_DIM

        def scores(c):
            k0 = pl.multiple_of(k_start + c * ck, 128)
            return lax.dot_general(qq, k_ref[pl.ds(k0, ck), :], _NT, preferred_element_type=F32)

        def chunk_inputs(c):
            k0 = pl.multiple_of(k_start + c * ck, 128)
            return scores(c), bias_fn(c, k0)

        c_exp = np.float32(SCALE * np.log2(np.e))

        def reduce_chunk(c, sb):
            sc_, bias = sb
            k0 = pl.multiple_of(k_start + c * ck, 128)
            vch = v_ref[pl.ds(k0, ck), :]
            alphas, ps = [], []
            nblk = rq // rb
            for r in range(nblk):
                rows = slice(r * rb, (r + 1) * rb)
                b0 = (r * rb) % tq
                sg = [sc_[rows, j * HEAD_DIM:(j + 1) * HEAD_DIM]
                      + bias[b0:b0 + rb, j * HEAD_DIM:(j + 1) * HEAD_DIM] for j in range(nl)]
                mx = functools.reduce(jnp.maximum, sg)
                m_old = m_scr[rows]
                m_new = jnp.maximum(m_old, jnp.max(mx, axis=1, keepdims=True))
                alpha = jnp.exp2((m_old - m_new) * c_exp)
                pj = [jnp.exp2((x - m_new) * c_exp) for x in sg]
                psum = functools.reduce(lambda a, b: a + b, pj)
                l_scr[rows] = alpha * l_scr[rows] + jnp.sum(psum, axis=1, keepdims=True)
                m_scr[rows] = m_new
                alphas.append(alpha)
                ps.append(jnp.concatenate([x.astype(BF16) for x in pj], axis=1))
                if (r + 1) % (nblk // 2) == 0:
                    half = slice((r + 1 - nblk // 2) * rb, (r + 1) * rb)
                    pv = jnp.dot(jnp.concatenate(ps, axis=0), vch, preferred_element_type=F32)
                    acc_scr[half] = jnp.concatenate(alphas, axis=0) * acc_scr[half] + pv
                    alphas, ps = [], []

        def body(c, carry):
            reduce_chunk(c, chunk_inputs(c))
            return carry

        lax.fori_loop(0, n_chunks, body, 0)
        return acc_scr[...] / l_scr[...]

    qp = q0 + lax.broadcasted_iota(jnp.int32, (tq, ck), 0)
    kcol = lax.broadcasted_iota(jnp.int32, (tq, ck), 1)

    eye = (lax.broadcasted_iota(jnp.int32, (nsb, HEAD_DIM), 0)
           == lax.broadcasted_iota(jnp.int32, (nsb, HEAD_DIM), 1)).astype(BF16)
    sel_q = lax.dot_general(sel_t, eye, _TN, preferred_element_type=F32).astype(BF16)

    n_slc = (q0 + tq + ck - 1) // ck

    def put_mask(c):
        selq = jnp.dot(sel_q, e_ref[c], preferred_element_type=F32)
        bias_scr[c] = jnp.where((selq > 0.5) & (c * ck + kcol <= qp), 0.0, NEG)

    def mask_body(i, carry):
        put_mask(2 * i)
        put_mask(jnp.minimum(2 * i + 1, seq // ck - 1))
        return carry

    lax.fori_loop(0, (n_slc + 1) // 2, mask_body, 0)
    o_slc = sweep(ksb, vsb, 0, n_slc, lambda c, k0: bias_scr[c])

    span = -(-(WINDOW + tq) // ck) * ck

    def win_bias(c, k0):
        dlt = qp - (k0 + kcol)
        return jnp.where((dlt >= 0) & (dlt < WINDOW), 0.0, NEG)

    o_win = sweep(kwb, vwb, jnp.maximum(q0 + tq - span, 0), span // ck, win_bias)

    for g in range(GROUP):
        r0, r1 = g * tq, (g + 1) * tq
        og = gate_scr[g] * o_cmp[r0:r1] + gate_scr[GROUP + g] * o_slc[r0:r1]
        og = og + gate_scr[2 * GROUP + g] * o_win[r0:r1]
        o_ref[:, g * HEAD_DIM:(g + 1) * HEAD_DIM] = og.astype(o_ref.dtype)


def _attn_prompt(q, kcv, kv, win, g_re, *, nb, seq, tq=256, ck=512):
    assert tq & (tq - 1) == 0 and seq % ck == 0 and WINDOW % tq == 0
    t = q.shape[0]
    ncp = seq // CMP_STRIDE
    nsb = seq // SLC_LEN
    nq = seq // tq
    mt = jnp.asarray(_overlap_matrix(ncp, nsb, ncp - 1, nsb).T)
    kk = np.arange(seq)
    assert nsb <= HEAD_DIM
    e3 = (kk[None, :] // SLC_LEN == np.arange(HEAD_DIM)[:, None]).astype(np.float32)
    e3 = jnp.asarray(e3.reshape(HEAD_DIM, seq // ck, ck).transpose(1, 0, 2), dtype=BF16)
    slab = lambda cb: pl.BlockSpec((seq, HEAD_DIM), lambda b, h, i, cb=cb: (b, cb + h))
    return pl.pallas_call(
        functools.partial(_attn_prompt_kernel, tq=tq, seq=seq, ck=ck),
        grid=(nb, N_KV, nq),
        in_specs=[pl.BlockSpec((tq, GROUP * HEAD_DIM), lambda b, h, i: (b * nq + i, h)),
                  pl.BlockSpec((None, None, None, ncp, HEAD_DIM), lambda b, h, i: (b, 0, h, 0, 0)),
                  pl.BlockSpec((None, None, None, ncp, HEAD_DIM), lambda b, h, i: (b, 1, h, 0, 0)),
                  slab(0), slab(N_KV), slab(0), slab(N_KV),
                  pl.BlockSpec((None, None, tq, 4 * GROUP), lambda b, h, i: (b, h, i, 0)),
                  pl.BlockSpec((nsb, ncp), lambda b, h, i: (0, 0)),
                  pl.BlockSpec((seq // ck, HEAD_DIM, ck), lambda b, h, i: (0, 0, 0))],
        out_specs=pl.BlockSpec((tq, GROUP * HEAD_DIM), lambda b, h, i: (b * nq + i, h)),
        out_shape=jax.ShapeDtypeStruct((t, N_HEADS * HEAD_DIM), BF16),
        scratch_shapes=[pltpu.VMEM((GROUP * tq, HEAD_DIM), F32)] * 3
        + [pltpu.VMEM((seq // ck, tq, ck), F32), pltpu.VMEM((3 * GROUP, tq, HEAD_DIM), F32)],
        compiler_params=_cparams(("arbitrary", "arbitrary", "arbitrary")),
        name="attn_prompt",
    )(q, kcv, kcv, kv, kv, win, win, g_re, mt, e3)


def _attn_sample_kernel(pt_ref, cache_ref, q_ref, kcv_ref, kvn_ref, wn_ref, st_ref, g_ref, ms_ref, e_ref, o_ref,
                        wso_ref, m_scr, l_scr, acc_scr, sel_scr, ocmp_scr, kv_buf, sem, *, past, ds):
    b = pl.program_id(0)
    p = pl.program_id(1)
    n_steps = pl.num_programs(1)

    def page_copies(bb, pp, slot):
        cps = []
        for n in range(PAGES_PER_STEP):
            page = pt_ref[bb, pp * PAGES_PER_STEP + n]
            for cb in range(2 * N_KV):
                cps.append(pltpu.make_async_copy(cache_ref.at[page, :, 2 * N_KV + cb, :],
                                                 kv_buf.at[slot, cb, pl.ds(n * PAGE, PAGE), :],
                                                 sem.at[slot]))
        return cps

    t = b * n_steps + p
    slot = lax.rem(t, 2)

    @pl.when(t == 0)
    def _():
        for cp in page_copies(b, p, slot):
            cp.start()

    @pl.when(t + 1 < pl.num_programs(0) * n_steps)
    def _():
        wrap = p + 1 == n_steps
        for cp in page_copies(jnp.where(wrap, b + 1, b), jnp.where(wrap, 0, p + 1), 1 - slot):
            cp.start()
    rq = GROUP * ds
    ncp = kcv_ref.shape[2]
    nc = ncp - 1
    n_sel_chunks = sel_scr.shape[1]
    lanes = n_sel_chunks * HEAD_DIM

    def q_rows(h):
        c0 = h * GROUP * HEAD_DIM
        return jnp.concatenate(
            [q_ref[:, c0 + g * HEAD_DIM:c0 + (g + 1) * HEAD_DIM] for g in range(GROUP)], axis=0).astype(BF16)

    def online_update(h, sc_, vals):
        m_i = m_scr[h]
        m_new = jnp.maximum(m_i, jnp.max(sc_, axis=1, keepdims=True))
        alpha = jnp.exp(m_i - m_new)
        pp = jnp.exp(sc_ - m_new)
        l_scr[h] = alpha * l_scr[h] + jnp.sum(pp, axis=1, keepdims=True)
        acc_scr[h] = alpha * acc_scr[h] + jnp.dot(pp.astype(BF16), vals, preferred_element_type=F32)
        m_scr[h] = m_new

    @pl.when(p == 0)
    def _():
        pgs = []
        for h in range(N_KV):
            qq = q_rows(h)
            kc = kcv_ref[0, h]
            vc = kcv_ref[1, h]
            s = lax.dot_general(qq, kc, _NT, preferred_element_type=F32) * SCALE
            qpos = past + (lax.broadcasted_iota(jnp.int32, (rq, ncp), 0) & (ds - 1))
            col = lax.broadcasted_iota(jnp.int32, (rq, ncp), 1)
            vis = (col * CMP_STRIDE + (CMP_LEN - 1) <= qpos) & (col < nc)
            s = jnp.where(vis, s, NEG)
            mx = jnp.max(s, axis=1, keepdims=True)
            e = jnp.where(vis, jnp.exp(s - mx), 0.0)
            den = jnp.sum(e, axis=1, keepdims=True)
            pr = e / jnp.where(den > 0.0, den, 1.0)
            ocmp_scr[h] = jnp.dot(pr.astype(BF16), vc, preferred_element_type=F32)
            pg = pr[0:ds]
            for g in range(1, GROUP):
                pg = pg + pr[g * ds:(g + 1) * ds]
            pgs.append(pg)
            m_scr[h] = jnp.full((rq, 1), NEG, F32)
            l_scr[h] = jnp.zeros((rq, 1), F32)
            acc_scr[h] = jnp.zeros((rq, HEAD_DIM), F32)
        nr = N_KV * ds
        pg_all = jnp.concatenate(pgs, axis=0)
        hi = pg_all.astype(BF16)
        r1 = pg_all - hi.astype(F32)
        mid = r1.astype(BF16)
        lo = (r1 - mid.astype(F32)).astype(BF16)
        sc3 = jnp.dot(jnp.concatenate([hi, mid, lo], axis=0), ms_ref[...], preferred_element_type=F32)
        score = (sc3[0:nr] + sc3[nr:2 * nr]) + sc3[2 * nr:3 * nr]
        lane = lax.broadcasted_iota(jnp.int32, (nr, lanes), 1)
        lane_in = lax.rem(lane, jnp.int32(HEAD_DIM))
        real = lane_in < BLK_PER_CHUNK
        blk = jnp.where(real, lax.div(lane, jnp.int32(HEAD_DIM)) * BLK_PER_CHUNK + lane_in, (1 << 19) + lane)
        qrow = lax.broadcasted_iota(jnp.int32, (nr, lanes), 0) & (ds - 1)
        cur = lax.div(past + qrow, jnp.int32(SLC_LEN))
        valid = real & (blk <= cur)
        forced = (blk == 0) | (blk > cur - N_LOCAL)
        sel = _select_topk(score, valid, forced, blk, 1)
        for h in range(N_KV):
            for c in range(n_sel_chunks):
                sel_scr[h, c] = sel[h * ds:(h + 1) * ds, c * HEAD_DIM:(c + 1) * HEAD_DIM]

    for cp in page_copies(b, p, slot):
        cp.wait()

    k0 = p * CHUNK_ROWS
    for h in range(N_KV):
        qq = q_rows(h)
        kch = kv_buf[slot, h].astype(BF16)
        vch = kv_buf[slot, N_KV + h].astype(BF16)
        sc_ = lax.dot_general(qq, kch, _NT, preferred_element_type=F32) * SCALE
        selq = jnp.dot(sel_scr[h, p].astype(BF16), e_ref[...], preferred_element_type=F32)
        kpos = k0 + lax.broadcasted_iota(jnp.int32, (ds, CHUNK_ROWS), 1)
        qp = past + lax.broadcasted_iota(jnp.int32, (ds, CHUNK_ROWS), 0)
        okf = jnp.where((selq > 0.5) & (kpos <= qp), 1.0, 0.0)
        ok = jnp.concatenate([okf] * GROUP, axis=0) > 0.5
        online_update(h, jnp.where(ok, sc_, NEG), vch)

    @pl.when(p == n_steps - 1)
    def _():
        gt = jax.nn.sigmoid(g_ref[...])
        zpad = jnp.zeros((HEAD_DIM - ds, HEAD_DIM), F32)
        for h in range(N_KV):
            qq = q_rows(h)
            c_k = 2 * KV_W + h * HEAD_DIM
            c_v = 3 * KV_W + h * HEAD_DIM
            kn = jnp.concatenate([kvn_ref[:, c_k:c_k + HEAD_DIM], zpad], axis=0).astype(BF16)
            vn = jnp.concatenate([kvn_ref[:, c_v:c_v + HEAD_DIM], zpad], axis=0).astype(BF16)
            sn = lax.dot_general(qq, kn, _NT, preferred_element_type=F32) * SCALE
            last_sel = sel_scr[h, n_sel_chunks - 1]
            nb_last = (past // SLC_LEN) % BLK_PER_CHUNK
            seln = jnp.sum(jnp.where(lax.broadcasted_iota(jnp.int32, (ds, HEAD_DIM), 1) == nb_last,
                                     last_sel, 0.0), axis=1, keepdims=True)
            seln = jnp.concatenate([seln] * GROUP, axis=0)
            srow = lax.broadcasted_iota(jnp.int32, (rq, HEAD_DIM), 0) & (ds - 1)
            kcol = lax.broadcasted_iota(jnp.int32, (rq, HEAD_DIM), 1)
            okn = (seln > 0.5) & (kcol <= srow) & (kcol < ds)
            online_update(h, jnp.where(okn, sn, NEG), vn)
            o_slc = acc_scr[h] / l_scr[h]
            wb = st_ref.shape[0]
            band = wb + HEAD_DIM
            st_flat = st_ref.reshape(wb * 8, HEAD_DIM)
            kw = jnp.concatenate([st_flat[pl.ds(h, wb, stride=8), :],
                                  wn_ref[:, h * HEAD_DIM:(h + 1) * HEAD_DIM], zpad], axis=0).astype(BF16)
            vw = jnp.concatenate([st_flat[pl.ds(N_KV + h, wb, stride=8), :],
                                  wn_ref[:, KV_W + h * HEAD_DIM:KV_W + (h + 1) * HEAD_DIM], zpad],
                                 axis=0).astype(BF16)
            sw = lax.dot_general(qq, kw, _NT, preferred_element_type=F32) * SCALE
            srw = lax.broadcasted_iota(jnp.int32, (rq, band), 0) & (ds - 1)
            idx = lax.broadcasted_iota(jnp.int32, (rq, band), 1)
            dlt = (wb + srw) - idx
            sw = jnp.where((dlt >= 0) & (dlt < WINDOW) & (idx < wb + ds), sw, NEG)
            ew = jnp.exp(sw - jnp.max(sw, axis=1, keepdims=True))
            o_win = jnp.dot(ew.astype(BF16), vw, preferred_element_type=F32) / jnp.sum(ew, axis=1, keepdims=True)
            o_cmp = ocmp_scr[h]
            gh = gt[h]
            for g in range(GROUP):
                r0, r1 = g * ds, (g + 1) * ds
                og = gh[:, g:g + 1] * o_cmp[r0:r1] + gh[:, GROUP + g:GROUP + g + 1] * o_slc[r0:r1]
                og = og + gh[:, 2 * GROUP + g:2 * GROUP + g + 1] * o_win[r0:r1]
                c0 = (h * GROUP + g) * HEAD_DIM
                o_ref[:, c0:c0 + HEAD_DIM] = og
        wb = st_ref.shape[0]
        wso_ref[0:wb - ds] = st_ref[ds:wb]
        wso_flat = wso_ref.reshape(wb * 8, HEAD_DIM)
        for cb in range(2 * N_KV):
            wso_flat[pl.ds((wb - ds) * 8 + cb, ds, stride=8), :] = wn_ref[:, cb * HEAD_DIM:(cb + 1) * HEAD_DIM]


def _attn_sample(q, kcv, kv_new, win_new, win_state, g_re, cache4, page_table, *, ds):
    nb, npg = page_table.shape
    past = npg * PAGE
    assert ds & (ds - 1) == 0 and ds <= SLC_LEN and past % SLC_LEN == 0 and npg % PAGES_PER_STEP == 0
    steps = npg // PAGES_PER_STEP
    ncp = kcv.shape[3]
    wb = win_state.shape[1]
    n_blocks = past // SLC_LEN + 1
    n_sel_chunks = -(-n_blocks // BLK_PER_CHUNK)
    m = _overlap_matrix(ncp, n_sel_chunks * BLK_PER_CHUNK, ncp - 1, n_blocks)
    ms = np.zeros((ncp, n_sel_chunks, HEAD_DIM), np.float32)
    ms[:, :, :BLK_PER_CHUNK] = m.reshape(ncp, n_sel_chunks, BLK_PER_CHUNK)
    ms = jnp.asarray(ms.reshape(ncp, n_sel_chunks * HEAD_DIM), dtype=BF16)
    ee = np.zeros((HEAD_DIM, CHUNK_ROWS), np.float32)
    ee[:BLK_PER_CHUNK] = np.arange(CHUNK_ROWS)[None, :] // SLC_LEN == np.arange(BLK_PER_CHUNK)[:, None]
    ee = jnp.asarray(ee, dtype=BF16)
    rq = GROUP * ds

    grid_spec = pltpu.PrefetchScalarGridSpec(
        num_scalar_prefetch=1,
        grid=(nb, steps),
        in_specs=[
            pl.BlockSpec(memory_space=pl.ANY),
            pl.BlockSpec((ds, N_HEADS * HEAD_DIM), lambda b, p, pt: (b, 0)),
            pl.BlockSpec((None, 2, N_KV, ncp, HEAD_DIM), lambda b, p, pt: (b, 0, 0, 0, 0)),
            pl.BlockSpec((ds, 4 * KV_W), lambda b, p, pt: (b, 0)),
            pl.BlockSpec((ds, 2 * KV_W), lambda b, p, pt: (b, 0)),
            pl.BlockSpec((None, wb, 8, HEAD_DIM), lambda b, p, pt: (b, 0, 0, 0)),
            pl.BlockSpec((None, N_KV, ds, 4 * GROUP), lambda b, p, pt: (b, 0, 0, 0)),
            pl.BlockSpec(ms.shape, lambda b, p, pt: (0, 0)),
            pl.BlockSpec(ee.shape, lambda b, p, pt: (0, 0))],
        out_specs=[pl.BlockSpec((ds, N_HEADS * HEAD_DIM), lambda b, p, pt: (b, 0)),
                   pl.BlockSpec((None, wb, 8, HEAD_DIM), lambda b, p, pt: (b, 0, 0, 0))],
        scratch_shapes=[pltpu.VMEM((N_KV, rq, 1), F32), pltpu.VMEM((N_KV, rq, 1), F32),
                        pltpu.VMEM((N_KV, rq, HEAD_DIM), F32),
                        pltpu.VMEM((N_KV, n_sel_chunks, ds, HEAD_DIM), F32),
                        pltpu.VMEM((N_KV, rq, HEAD_DIM), F32),
                        pltpu.VMEM((2, 2 * N_KV, CHUNK_ROWS, HEAD_DIM), F32),
                        pltpu.SemaphoreType.DMA((2,))],
    )
    return pl.pallas_call(
        functools.partial(_attn_sample_kernel, past=past, ds=ds),
        grid_spec=grid_spec,
        out_shape=[jax.ShapeDtypeStruct((nb * ds, N_HEADS * HEAD_DIM), F32),
                   jax.ShapeDtypeStruct((nb, wb, 8, HEAD_DIM), F32)],
        compiler_params=_cparams(("arbitrary", "arbitrary")),
        name="attn_sample",
    )(page_table, cache4, q, kcv, kv_new, win_new, win_state, g_re, ms, ee)


_Q0, _KV0, _WIN0, _G0 = 0, 2048, 4096, 5120
_REST0 = _G0 + 3 * N_HEADS


def _split_w_in_kernel(w_ref, head_ref, tail_ref):
    w = w_ref[...]
    head_ref[...] = w[:, :_G0 + HEAD_DIM].astype(BF16)
    tail_ref[...] = w[:, _REST0:].astype(BF16)


def _split_w_in(w_in, tr=256):
    k, n = w_in.shape
    nh, nt = _G0 + HEAD_DIM, n - _REST0
    return pl.pallas_call(
        _split_w_in_kernel,
        grid=(k // tr,),
        in_specs=[pl.BlockSpec((tr, n), lambda i: (i, 0))],
        out_specs=[pl.BlockSpec((tr, nh), lambda i: (i, 0)), pl.BlockSpec((tr, nt), lambda i: (i, 0))],
        out_shape=[jax.ShapeDtypeStruct((k, nh), BF16), jax.ShapeDtypeStruct((k, nt), BF16)],
        compiler_params=_cparams(("arbitrary",)),
        name="split_w_in",
    )(w_in)


def _gate_layout(g_logits, nb, seq):
    g = g_logits[:, :3 * N_HEADS].reshape(nb, seq, 3, N_KV, GROUP)
    g = g.transpose(0, 3, 1, 2, 4).reshape(nb, N_KV, seq, 3 * GROUP)
    return jnp.pad(g, ((0, 0), (0, 0), (0, 0), (0, GROUP)))


def _cmp_weights(w_k1, w_k2, pe_k, w_v1, w_v2, pe_v):
    def by_row(w1):
        w = w1.reshape(2, CMP_STRIDE, HEAD_DIM, HEAD_DIM).transpose(1, 2, 0, 3)
        return w.reshape(CMP_STRIDE * HEAD_DIM, 2 * HEAD_DIM)
    w1pairs = jnp.concatenate([by_row(w_k1), by_row(w_v1)], axis=1).astype(BF16)
    pe8 = jnp.stack([jnp.broadcast_to(pe_k.reshape(1, -1), (8, CMP_LEN * HEAD_DIM)),
                     jnp.broadcast_to(pe_v.reshape(1, -1), (8, CMP_LEN * HEAD_DIM))])
    return w1pairs, pe8, jnp.stack([w_k1, w_v1]), jnp.stack([w_k2, w_v2])


def kernel(x_prompt, x_sample, c_prompt, c_sample, cache_nsa_kv, page_table, state_win_kv, state_conv, w_ada, b_ada, norm1_g, norm2_g, w_in, w_cmp_k1, w_cmp_k2, pe_cmp_k, w_cmp_v1, w_cmp_v2, pe_cmp_v, conv_w, conv_b, w_attn_proj, w_conv_proj, w_out, w_mlp1, w_mlp2, normf_g):
    d = D_MODEL
    nbp, seq, _ = x_prompt.shape
    nbs, ds, _ = x_sample.shape
    tp = nbp * seq
    ts = nbs * ds
    tc = 256
    depth = w_in.shape[0]
    assert depth == 1

    xp = x_prompt.reshape(tp, d)
    xs = x_sample.reshape(ts, d)
    normf = normf_g.reshape(1, d)

    l = 0
    c_all = jnp.concatenate([c_prompt, c_sample, jnp.zeros((16 - nbp - nbs, d), F32)], axis=0)
    ada = _ada(c_all, w_ada[l], b_ada[l].reshape(1, -1)).reshape(16, 6, d)
    ada_p = [ada[:nbp, k][:, None, :] for k in range(6)]
    ada_s = [jnp.repeat(ada[nbp:nbp + nbs, k], ds, axis=0)[None] for k in range(6)]

    w_head, w_tail = _split_w_in(w_in[l])
    gates0 = 3 * d
    wa = w_attn_proj[l].astype(BF16)
    wc = w_conv_proj[l].astype(BF16)
    wo = w_out[l].astype(BF16)
    w1 = w_mlp1[l].astype(BF16)
    w2 = w_mlp2[l].astype(BF16)
    g1n = norm1_g[l].reshape(1, d)
    g2n = norm2_g[l].reshape(1, d)
    cw = conv_w[l]
    cb = conv_b[l].reshape(1, d)
    w1pairs, pe8, w1s, w2s = _cmp_weights(w_cmp_k1[l], w_cmp_k2[l], pe_cmp_k[l],
                                          w_cmp_v1[l], w_cmp_v2[l], pe_cmp_v[l])

    tm = 1024
    hp = _norm_mod(xp, g1n, ada_p[1], ada_p[0], tm=512, seq=seq)
    q_p = _mm(hp, w_head, tm=tm, tn=1024, out_dtype=BF16, name="mm_q", col0=_Q0, n=N_HEADS * HEAD_DIM)
    wbp = min(WINDOW, seq)
    kv6_p, kvb_p = _mm_kv(hp, w_head, tm=tm, col0=_KV0)
    winb_p, win6_p = _mm_win(hp, w_head, tm=wbp, seq=seq, col0=_WIN0)
    g_p = _mm(hp, w_head, tm=tm, tn=HEAD_DIM, out_dtype=F32, name="mm_g", col0=_G0, n=HEAD_DIM)
    gates_p = _mm(hp, w_tail, tm=tm, tn=1024, out_dtype=BF16, act="sigmoid", name="mm_gates",
                  col0=gates0, n=2 * d)
    z_p, conv_tiles = _mm_conv(hp, w_tail, cw, cb, tm=tm, tc=tc, seq=seq, col0=0)

    pt_p = jnp.arange(tp // PAGE, dtype=jnp.int32).reshape(nbp, seq // PAGE)
    part_p = _cmp_part(kv6_p.reshape(tp // PAGE, PAGE, 16, HEAD_DIM), pt_p, w1pairs)
    kcv_p = _cmp_finish(part_p, pe8, w1s, w2s)
    o_p = _attn_prompt(q_p, kcv_p, kvb_p, winb_p, _gate_layout(g_p, nbp, seq), nb=nbp, seq=seq)

    mixed_p = _mix(o_p, z_p, wa, wc, gates_p, tm=tm, tn=512)
    x1_p = _mm_resid(mixed_p, wo, xp, ada_p[2], tm=tm, tn=1024, seq=seq)
    h2_p = _norm_mod(x1_p, g2n, ada_p[4], ada_p[3], tm=512, seq=seq)
    y_p = _mlp_final(h2_p, w1, w2, x1_p, ada_p[5], normf, tm=512, tf=1024, seq=seq)

    hs = _norm_mod(xs, g1n, ada_s[1], ada_s[0], tm=ts, seq=ds)
    q_s = _mm(hs, w_head, tm=ts, tn=512, out_dtype=F32, name="mm_q_s", col0=_Q0, n=N_HEADS * HEAD_DIM)
    kv_s = _mm(hs, w_head, tm=ts, tn=512, out_dtype=F32, name="mm_kv_s", col0=_KV0, n=4 * KV_W)
    win_s = _mm(hs, w_head, tm=ts, tn=512, out_dtype=F32, name="mm_win_s", col0=_WIN0, n=2 * KV_W)
    g_s = _mm(hs, w_head, tm=ts, tn=HEAD_DIM, out_dtype=F32, name="mm_g_s", col0=_G0, n=HEAD_DIM)
    gates_s = _mm(hs, w_tail, tm=ts, tn=512, out_dtype=F32, act="sigmoid", name="mm_gates_s",
                  col0=gates0, n=2 * d)
    ubc_s = _mm(hs, w_tail, tm=ts, tn=512, out_dtype=F32, name="mm_ubc_s", col0=0, n=3 * d)
    st = state_conv[l]
    z_s, cu_s = _sample_conv(ubc_s, jnp.repeat(st[:, 0], ds, axis=0), jnp.repeat(st[:, 1], ds, axis=0),
                             cw, cb, tc=tc, seq=ds)

    cache4 = cache_nsa_kv[l].reshape(cache_nsa_kv.shape[1], PAGE, 16, HEAD_DIM)
    part_s = _cmp_part(cache4, page_table, w1pairs)
    kcv_s = _cmp_finish(part_s, pe8, w1s, w2s)
    wbs = state_win_kv.shape[2]
    wst = state_win_kv[l].reshape(nbs, wbs, 8, HEAD_DIM)
    o_s, wst_next = _attn_sample(q_s, kcv_s, kv_s, win_s, wst, _gate_layout(g_s, nbs, ds), cache4, page_table,
                                 ds=ds)

    mixed_s = _mix(o_s, z_s, wa, wc, gates_s, tm=ts, tn=512)
    x1_s = _mm_resid(mixed_s, wo, xs, ada_s[2], tm=ts, tn=512, seq=ds)
    h2_s = _norm_mod(x1_s, g2n, ada_s[4], ada_s[3], tm=ts, seq=ds)
    y_s = _mlp_final(h2_s, w1, w2, x1_s, ada_s[5], normf, tm=ts, tf=512, seq=ds)

    kv_prompt = kv6_p.reshape(1, nbp, seq, 4, N_KV, HEAD_DIM)
    kv_sample = kv_s.reshape(1, nbs, ds, 4, N_KV, HEAD_DIM)
    win_prompt = win6_p.reshape(1, nbp, wbp, 2, N_KV, HEAD_DIM)
    win_sample = wst_next.reshape(1, nbs, wbs, 2, N_KV, HEAD_DIM)
    tiles_per_seq = seq // tm
    conv_prompt = conv_tiles[tiles_per_seq - 1::tiles_per_seq][None]
    conv_sample = cu_s.reshape(nbs, ds, d)[None, :, ds - (CONV_W - 1):]
    return (y_p.reshape(nbp, seq, d), y_s.reshape(nbs, ds, d), kv_prompt, kv_sample,
            win_prompt, win_sample, conv_prompt, conv_sample)
```

```python
import functools

import numpy as np
import jax
import jax.numpy as jnp
from jax import lax
from jax.experimental import pallas as pl
from jax.experimental.pallas import tpu as pltpu

F32 = jnp.float32
BF16 = jnp.bfloat16

D_MODEL = 2048
HEAD_DIM = 128
N_HEADS = 16
N_KV = 4
GROUP = 4
KV_W = N_KV * HEAD_DIM
CMP_LEN = 32
CMP_STRIDE = 16
SLC_LEN = 64
SLC_TOP = 16
N_LOCAL = 2
WINDOW = 512
CONV_W = 3
RMS_EPS = 1e-6
NEG = -1e30
SCALE = HEAD_DIM ** -0.5
PAGE = 128
PAGES_PER_STEP = 16
CHUNK_ROWS = PAGES_PER_STEP * PAGE
BLK_PER_CHUNK = CHUNK_ROWS // SLC_LEN
VMEM_LIMIT = 56 * 1024 * 1024

_NT = (((1,), (1,)), ((), ()))
_TN = (((0,), (0,)), ((), ()))


def _cparams(sem):
    return pltpu.CompilerParams(dimension_semantics=sem, vmem_limit_bytes=VMEM_LIMIT)


def _ada_kernel(c_ref, w_ref, b_ref, o_ref):
    c = c_ref[...]
    a = (c * jax.nn.sigmoid(c)).astype(BF16)
    o_ref[...] = jnp.dot(a, w_ref[...].astype(BF16), preferred_element_type=F32) + b_ref[...]


def _ada(c, w, b, tn=1024):
    m, k = c.shape
    n = w.shape[1]
    return pl.pallas_call(
        _ada_kernel,
        grid=(n // tn,),
        in_specs=[pl.BlockSpec((m, k), lambda j: (0, 0)),
                  pl.BlockSpec((k, tn), lambda j: (0, j)),
                  pl.BlockSpec((1, tn), lambda j: (0, j))],
        out_specs=pl.BlockSpec((m, tn), lambda j: (0, j)),
        out_shape=jax.ShapeDtypeStruct((m, n), F32),
        compiler_params=_cparams(("arbitrary",)),
        name="ada",
    )(c, w, b)


def _rowspec(p, tm, tn, seq, col=True):
    gr = p.shape[1]
    if gr == 1:
        if col:
            return pl.BlockSpec((1, 1, tn), lambda i, j: ((i * tm) // seq, 0, j))
        return pl.BlockSpec((1, 1, tn), lambda i: ((i * tm) // seq, 0, 0))
    if col:
        return pl.BlockSpec((1, tm, tn), lambda i, j: (0, i, j))
    return pl.BlockSpec((1, tm, tn), lambda i: (0, i, 0))


def _norm_kernel(x_ref, g_ref, sc_ref, sh_ref, o_ref):
    x = x_ref[...]
    r = lax.rsqrt(jnp.mean(x * x, axis=-1, keepdims=True) + RMS_EPS)
    y = (x * r) * g_ref[...]
    o_ref[...] = (y * (1.0 + sc_ref[0]) + sh_ref[0]).astype(o_ref.dtype)


def _norm_mod(x, g, sc, sh, *, tm, seq):
    t, d = x.shape
    return pl.pallas_call(
        _norm_kernel,
        grid=(t // tm,),
        in_specs=[pl.BlockSpec((tm, d), lambda i: (i, 0)),
                  pl.BlockSpec((1, d), lambda i: (0, 0)),
                  _rowspec(sc, tm, d, seq, col=False),
                  _rowspec(sh, tm, d, seq, col=False)],
        out_specs=pl.BlockSpec((tm, d), lambda i: (i, 0)),
        out_shape=jax.ShapeDtypeStruct((t, d), BF16),
        compiler_params=_cparams(("arbitrary",)),
        name="norm_mod",
    )(x, g, sc, sh)


def _xwt(x_ref, wt_ref):
    return lax.dot_general(x_ref[...].astype(BF16), wt_ref[...], _NT, preferred_element_type=F32)


def _mm_kernel(x_ref, w_ref, o_ref, *, act):
    acc = _xwt(x_ref, w_ref)
    if act == "sigmoid":
        acc = jax.nn.sigmoid(acc)
    o_ref[...] = acc.astype(o_ref.dtype)


def _mm(x, w, *, tm, tn, out_dtype, act=None, name="mm", col0, n):
    t, k = x.shape
    assert col0 % tn == 0 and n % tn == 0
    cb0 = col0 // tn
    return pl.pallas_call(
        functools.partial(_mm_kernel, act=act),
        grid=(t // tm, n // tn),
        in_specs=[pl.BlockSpec((tm, k), lambda i, j: (i, 0)),
                  pl.BlockSpec((tn, k), lambda i, j: (cb0 + j, 0))],
        out_specs=pl.BlockSpec((tm, tn), lambda i, j: (i, j)),
        out_shape=jax.ShapeDtypeStruct((t, n), out_dtype),
        compiler_params=_cparams(("arbitrary", "arbitrary")),
        name=name,
    )(x, w)


def _store_head_major(o_ref, acc, tm):
    flat = o_ref.reshape(tm * 8, HEAD_DIM)
    for cb in range(8):
        flat[pl.ds(cb, tm, stride=8), :] = acc[:, cb * HEAD_DIM:(cb + 1) * HEAD_DIM]


def _mm_kv_kernel(x_ref, w_ref, o6_ref, ob_ref, *, tm):
    acc = _xwt(x_ref, w_ref)
    _store_head_major(o6_ref, acc, tm)

    @pl.when(pl.program_id(1) == 1)
    def _():
        ob_ref[...] = acc.astype(BF16)


def _mm_kv(h, w, *, tm, col0):
    t, k = h.shape
    tn = 2 * KV_W
    cb0 = col0 // tn
    return pl.pallas_call(
        functools.partial(_mm_kv_kernel, tm=tm),
        grid=(t // tm, 2),
        in_specs=[pl.BlockSpec((tm, k), lambda i, j: (i, 0)),
                  pl.BlockSpec((tn, k), lambda i, j: (cb0 + j, 0))],
        out_specs=[pl.BlockSpec((tm, 8, HEAD_DIM), lambda i, j: (i, j, 0)),
                   pl.BlockSpec((tm, tn), lambda i, j: (i, 0))],
        out_shape=[jax.ShapeDtypeStruct((t, 16, HEAD_DIM), F32),
                   jax.ShapeDtypeStruct((t, tn), BF16)],
        compiler_params=_cparams(("arbitrary", "arbitrary")),
        name="mm_kv",
    )(h, w)


def _mm_win_kernel(x_ref, w_ref, ob_ref, o6_ref, *, tm, seq):
    acc = _xwt(x_ref, w_ref)
    ob_ref[...] = acc.astype(BF16)

    @pl.when(((pl.program_id(0) + 1) * tm) % seq == 0)
    def _():
        _store_head_major(o6_ref, acc, tm)


def _mm_win(h, w, *, tm, seq, col0):
    t, k = h.shape
    tn = 2 * KV_W
    cb0 = col0 // tn
    return pl.pallas_call(
        functools.partial(_mm_win_kernel, tm=tm, seq=seq),
        grid=(t // tm,),
        in_specs=[pl.BlockSpec((tm, k), lambda i: (i, 0)),
                  pl.BlockSpec((tn, k), lambda i: (cb0, 0))],
        out_specs=[pl.BlockSpec((tm, tn), lambda i: (i, 0)),
                   pl.BlockSpec((tm, 8, HEAD_DIM), lambda i: ((i * tm) // seq, 0, 0))],
        out_shape=[jax.ShapeDtypeStruct((t, tn), BF16),
                   jax.ShapeDtypeStruct((t // seq * tm, 8, HEAD_DIM), F32)],
        compiler_params=_cparams(("arbitrary",)),
        name="mm_win",
    )(h, w)


def _conv_taps(cu, prev1, prev2, bgate, cw_ref, cb_ref):
    y = cb_ref[...] + prev2 * cw_ref[0:1, :]
    y = y + prev1 * cw_ref[1:2, :]
    y = y + cu * cw_ref[2:3, :]
    return bgate * y


def _mm_conv_kernel(x_ref, wu_ref, wb_ref, wc_ref, cw_ref, cb_ref, z_ref, st_ref, carry_ref, *, tm, tc, seq):
    i = pl.program_id(0)
    j = pl.program_id(1)
    w3 = jnp.concatenate([wu_ref[...], wb_ref[...], wc_ref[...]], axis=0)
    acc = lax.dot_general(x_ref[...], w3, _NT, preferred_element_type=F32)
    u = acc[:, 0:tc]
    bgate = acc[:, tc:2 * tc]
    cu = acc[:, 2 * tc:3 * tc] * u
    @pl.when((i * tm) % seq == 0)
    def _():
        carry_ref[j] = jnp.zeros((8, tc), F32)

    car = carry_ref[j]
    p0 = car[0:1, :]
    p1 = car[1:2, :]
    rows = lax.broadcasted_iota(jnp.int32, (tm, tc), 0)
    r1 = pltpu.roll(cu, 1, 0)
    r2 = pltpu.roll(cu, 2, 0)
    prev1 = jnp.where(rows == 0, p1, r1)
    prev2 = jnp.where(rows == 0, p0, jnp.where(rows == 1, p1, r2))
    z_ref[...] = _conv_taps(cu, prev1, prev2, bgate, cw_ref, cb_ref).astype(z_ref.dtype)
    tail = cu[tm - 8:tm, :]
    carry_ref[j] = jnp.concatenate([tail[6:8, :], tail[0:6, :]], axis=0)
    st_ref[0] = tail[6:8, :]


def _mm_conv(h, w, conv_w, conv_b, *, tm, tc, seq, col0):
    t, k = h.shape
    d = conv_w.shape[1]
    nj = d // tc
    cb0 = col0 // tc
    wspec = lambda part: pl.BlockSpec((tc, k), lambda i, j: (cb0 + part * nj + j, 0))
    return pl.pallas_call(
        functools.partial(_mm_conv_kernel, tm=tm, tc=tc, seq=seq),
        grid=(t // tm, nj),
        in_specs=[pl.BlockSpec((tm, k), lambda i, j: (i, 0)),
                  wspec(0), wspec(1), wspec(2),
                  pl.BlockSpec((CONV_W, tc), lambda i, j: (0, j)),
                  pl.BlockSpec((1, tc), lambda i, j: (0, j))],
        out_specs=[pl.BlockSpec((tm, tc), lambda i, j: (i, j)),
                   pl.BlockSpec((1, CONV_W - 1, tc), lambda i, j: (i, 0, j))],
        out_shape=[jax.ShapeDtypeStruct((t, d), BF16),
                   jax.ShapeDtypeStruct((t // tm, CONV_W - 1, d), F32)],
        scratch_shapes=[pltpu.VMEM((nj, 8, tc), F32)],
        compiler_params=_cparams(("arbitrary", "arbitrary")),
        name="mm_conv",
    )(h, w, w, w, conv_w, conv_b)


def _sample_conv_kernel(u_ref, b_ref, c_ref, st0_ref, st1_ref, cw_ref, cb_ref, z_ref, cu_ref, *, tc, seq):
    t = u_ref.shape[0]
    bgate = b_ref[...]
    cu = c_ref[...] * u_ref[...]
    s = lax.rem(lax.broadcasted_iota(jnp.int32, (t, tc), 0), jnp.int32(seq))
    r1 = pltpu.roll(cu, 1, 0)
    r2 = pltpu.roll(cu, 2, 0)
    prev1 = jnp.where(s == 0, st1_ref[...], r1)
    prev2 = jnp.where(s == 0, st0_ref[...], jnp.where(s == 1, st1_ref[...], r2))
    z_ref[...] = _conv_taps(cu, prev1, prev2, bgate, cw_ref, cb_ref)
    cu_ref[...] = cu


def _sample_conv(ubc, st0, st1, conv_w, conv_b, *, tc, seq):
    t = ubc.shape[0]
    d = ubc.shape[1] // 3
    nj = d // tc
    return pl.pallas_call(
        functools.partial(_sample_conv_kernel, tc=tc, seq=seq),
        grid=(nj,),
        in_specs=[pl.BlockSpec((t, tc), lambda j: (0, j)),
                  pl.BlockSpec((t, tc), lambda j: (0, nj + j)),
                  pl.BlockSpec((t, tc), lambda j: (0, 2 * nj + j)),
                  pl.BlockSpec((t, tc), lambda j: (0, j)),
                  pl.BlockSpec((t, tc), lambda j: (0, j)),
                  pl.BlockSpec((CONV_W, tc), lambda j: (0, j)),
                  pl.BlockSpec((1, tc), lambda j: (0, j))],
        out_specs=[pl.BlockSpec((t, tc), lambda j: (0, j)),
                   pl.BlockSpec((t, tc), lambda j: (0, j))],
        out_shape=[jax.ShapeDtypeStruct((t, d), F32), jax.ShapeDtypeStruct((t, d), F32)],
        compiler_params=_cparams(("arbitrary",)),
        name="sample_conv",
    )(ubc, ubc, ubc, st0, st1, conv_w, conv_b)


def _mix_kernel(o_ref, z_ref, wa_ref, wc_ref, ga_ref, gc_ref, m_ref):
    a = jnp.dot(o_ref[...].astype(BF16), wa_ref[...].astype(BF16), preferred_element_type=F32)
    c = jnp.dot(z_ref[...].astype(BF16), wc_ref[...].astype(BF16), preferred_element_type=F32)
    m_ref[...] = (ga_ref[...].astype(F32) * a + gc_ref[...].astype(F32) * c).astype(m_ref.dtype)


def _mix(o, z, wa, wc, gates, *, tm, tn):
    t, k = o.shape
    n = wa.shape[1]
    nj = n // tn
    return pl.pallas_call(
        _mix_kernel,
        grid=(t // tm, nj),
        in_specs=[pl.BlockSpec((tm, k), lambda i, j: (i, 0)),
                  pl.BlockSpec((tm, k), lambda i, j: (i, 0)),
                  pl.BlockSpec((k, tn), lambda i, j: (0, j)),
                  pl.BlockSpec((k, tn), lambda i, j: (0, j)),
                  pl.BlockSpec((tm, tn), lambda i, j: (i, j)),
                  pl.BlockSpec((tm, tn), lambda i, j: (i, j + nj))],
        out_specs=pl.BlockSpec((tm, tn), lambda i, j: (i, j)),
        out_shape=jax.ShapeDtypeStruct((t, n), BF16),
        compiler_params=_cparams(("arbitrary", "arbitrary")),
        name="mix",
    )(o, z, wa, wc, gates, gates)


def _resid_norm_kernel(a_ref, w_ref, x_ref, g_ref, ng_ref, sc_ref, sh_ref, x1_ref, h2_ref, wbf_ref):
    @pl.when(pl.program_id(0) == 0)
    def _():
        wbf_ref[...] = w_ref[...].astype(BF16)

    acc = jnp.dot(a_ref[...], wbf_ref[...], preferred_element_type=F32)
    x1 = x_ref[...] + g_ref[0] * acc
    x1_ref[...] = x1
    r = lax.rsqrt(jnp.mean(x1 * x1, axis=-1, keepdims=True) + RMS_EPS)
    y = (x1 * r) * ng_ref[...]
    h2_ref[...] = (y * (1.0 + sc_ref[0]) + sh_ref[0]).astype(h2_ref.dtype)


def _mm_resid_norm(a, w, x, gate, ng, sc, sh, *, tm, seq):
    t, k = a.shape
    n = w.shape[1]
    row = lambda p: _rowspec(p, tm, n, seq, col=False)
    return pl.pallas_call(
        _resid_norm_kernel,
        grid=(t // tm,),
        in_specs=[pl.BlockSpec((tm, k), lambda i: (i, 0)),
                  pl.BlockSpec((k, n), lambda i: (0, 0), pipeline_mode=pl.Buffered(1)),
                  pl.BlockSpec((tm, n), lambda i: (i, 0)),
                  row(gate),
                  pl.BlockSpec((1, n), lambda i: (0, 0)),
                  row(sc), row(sh)],
        out_specs=[pl.BlockSpec((tm, n), lambda i: (i, 0)), pl.BlockSpec((tm, n), lambda i: (i, 0))],
        out_shape=[jax.ShapeDtypeStruct((t, n), F32), jax.ShapeDtypeStruct((t, n), BF16)],
        scratch_shapes=[pltpu.VMEM((k, n), BF16)],
        compiler_params=_cparams(("arbitrary",)),
        name="mm_resid_norm",
    )(a, w, x, gate, ng, sc, sh)


def _mlp_kernel(h_ref, w1_ref, w2_ref, x_ref, g_ref, nf_ref, o_ref, acc_ref):
    k = pl.program_id(1)

    @pl.when(k == 0)
    def _():
        acc_ref[...] = jnp.zeros_like(acc_ref)

    a = jnp.dot(h_ref[...], w1_ref[...], preferred_element_type=F32)
    a = jnp.square(jnp.maximum(a, 0.0)).astype(BF16)
    acc_ref[...] += jnp.dot(a, w2_ref[...], preferred_element_type=F32)

    @pl.when(k == pl.num_programs(1) - 1)
    def _():
        x2 = x_ref[...] + g_ref[0] * acc_ref[...]
        r = lax.rsqrt(jnp.mean(x2 * x2, axis=-1, keepdims=True) + RMS_EPS)
        o_ref[...] = (x2 * r) * nf_ref[...]


def _mlp_final(h2, w1, w2, x1, gate, normf, *, tm, tf, seq):
    t, d = h2.shape
    f = w1.shape[1]
    return pl.pallas_call(
        _mlp_kernel,
        grid=(t // tm, f // tf),
        in_specs=[pl.BlockSpec((tm, d), lambda i, k: (i, 0)),
                  pl.BlockSpec((d, tf), lambda i, k: (0, k)),
                  pl.BlockSpec((tf, d), lambda i, k: (k, 0)),
                  pl.BlockSpec((tm, d), lambda i, k: (i, 0)),
                  _rowspec_k(gate, tm, d, seq),
                  pl.BlockSpec((1, d), lambda i, k: (0, 0))],
        out_specs=pl.BlockSpec((tm, d), lambda i, k: (i, 0)),
        out_shape=jax.ShapeDtypeStruct((t, d), F32),
        scratch_shapes=[pltpu.VMEM((tm, d), F32)],
        compiler_params=_cparams(("arbitrary", "arbitrary")),
        name="mlp",
    )(h2, w1, w2, x1, gate, normf)


def _rowspec_k(p, tm, d, seq):
    if p.shape[1] == 1:
        return pl.BlockSpec((1, 1, d), lambda i, k: ((i * tm) // seq, 0, 0))
    return pl.BlockSpec((1, tm, d), lambda i, k: (0, i, 0))


def _cmp_part_kernel(pt_ref, *refs):
    pages = refs[:PAGES_PER_STEP]
    w_ref = refs[PAGES_PER_STEP]
    o_ref = refs[PAGES_PER_STEP + 1]
    stage = refs[PAGES_PER_STEP + 2]
    nchunk = CHUNK_ROWS // CMP_STRIDE
    cpp = PAGE // CMP_STRIDE
    nslab = 2 * N_KV
    cols = [jnp.concatenate([pg[pl.ds(r, cpp, stride=CMP_STRIDE)].reshape(cpp * nslab, HEAD_DIM)
                             for pg in pages], axis=0).astype(BF16) for r in range(CMP_STRIDE)]
    lhs = jnp.concatenate(cols, axis=1)
    acc = jnp.dot(lhs, w_ref[...], preferred_element_type=F32)
    slab = lax.broadcasted_iota(jnp.int32, (nchunk * nslab, 2 * HEAD_DIM), 0) & (nslab - 1)
    sel = jnp.where(slab < N_KV, acc[:, :2 * HEAD_DIM], acc[:, 2 * HEAD_DIM:])
    stage[0] = sel[:, :HEAD_DIM]
    stage[1] = sel[:, HEAD_DIM:]
    for cb in range(nslab):
        o_ref[cb // N_KV, cb % N_KV] = jnp.concatenate(
            [stage[0, pl.ds(cb, nchunk, stride=nslab), :], stage[1, pl.ds(cb, nchunk, stride=nslab), :]], axis=1)


def _page_spec(n, kind_pair):
    return pl.BlockSpec((None, PAGE, 8, HEAD_DIM),
                        lambda b, p, pt, n=n: (pt[b, p * PAGES_PER_STEP + n], 0, kind_pair, 0))


def _cmp_part(pages4, page_table, w1pairs):
    nb, npg = page_table.shape
    steps = npg // PAGES_PER_STEP
    nchunk = PAGES_PER_STEP * (PAGE // CMP_STRIDE)
    grid_spec = pltpu.PrefetchScalarGridSpec(
        num_scalar_prefetch=1,
        grid=(nb, steps),
        in_specs=[_page_spec(n, 0) for n in range(PAGES_PER_STEP)]
        + [pl.BlockSpec((CMP_STRIDE * HEAD_DIM, 4 * HEAD_DIM), lambda b, p, pt: (0, 0))],
        out_specs=pl.BlockSpec((None, 2, N_KV, nchunk, 2 * HEAD_DIM), lambda b, p, pt: (b, 0, 0, p, 0)),
        scratch_shapes=[pltpu.VMEM((2, nchunk * 2 * N_KV, HEAD_DIM), F32)],
    )
    return pl.pallas_call(
        _cmp_part_kernel,
        grid_spec=grid_spec,
        out_shape=jax.ShapeDtypeStruct((nb, 2, N_KV, steps * nchunk, 2 * HEAD_DIM), F32),
        compiler_params=_cparams(("arbitrary", "arbitrary")),
        name="cmp_part",
    )(page_table, *([pages4] * PAGES_PER_STEP), w1pairs)


def _gelu_tanh(x):
    c = np.sqrt(2.0 / np.pi).astype(np.float32)
    return 0.5 * x * (1.0 + jnp.tanh(c * (x + 0.044715 * (x * x * x))))


def _cmp_finish_kernel(part_ref, pe_ref, w1_ref, w2_ref, o_ref, bias_ref):
    @pl.when(pl.program_id(1) == 0)
    def _():
        bias_ref[...] = jnp.dot(pe_ref[...], w1_ref[...], preferred_element_type=F32,
                                precision=lax.Precision.HIGHEST)

    w2 = w2_ref[...].astype(BF16)
    for hd in range(N_KV):
        part = part_ref[hd]
        n = part.shape[0]
        h = part[:, 0:HEAD_DIM] + pltpu.roll(part[:, HEAD_DIM:], n - 1, 0)
        h = h + bias_ref[0:1, :]
        o_ref[hd] = jnp.dot(_gelu_tanh(h).astype(BF16), w2, preferred_element_type=F32).astype(o_ref.dtype)


def _cmp_finish(part, pe8, w1, w2):
    nb, _, _, nchunk, _ = part.shape
    return pl.pallas_call(
        _cmp_finish_kernel,
        grid=(2, nb),
        in_specs=[pl.BlockSpec((None, None, N_KV, nchunk, 2 * HEAD_DIM), lambda k, b: (b, k, 0, 0, 0)),
                  pl.BlockSpec((None, 8, CMP_LEN * HEAD_DIM), lambda k, b: (k, 0, 0)),
                  pl.BlockSpec((None, CMP_LEN * HEAD_DIM, HEAD_DIM), lambda k, b: (k, 0, 0)),
                  pl.BlockSpec((None, HEAD_DIM, HEAD_DIM), lambda k, b: (k, 0, 0))],
        out_specs=pl.BlockSpec((None, None, N_KV, nchunk, HEAD_DIM), lambda k, b: (b, k, 0, 0, 0)),
        out_shape=jax.ShapeDtypeStruct((nb, 2, N_KV, nchunk, HEAD_DIM), BF16),
        scratch_shapes=[pltpu.VMEM((8, HEAD_DIM), F32)],
        compiler_params=_cparams(("arbitrary", "arbitrary")),
        name="cmp_finish",
    )(part, pe8, w1, w2)


def _select_topk(score, valid, forced, blk, axis):
    sc = jnp.where(valid, jnp.where(forced, jnp.inf, score), -jnp.inf)
    sel = jnp.zeros(score.shape, F32)
    big = jnp.int32(1 << 20)
    for _ in range(SLC_TOP):
        mx = jnp.max(sc, axis=axis, keepdims=True)
        idx = jnp.min(jnp.where(sc == mx, blk, big), axis=axis, keepdims=True)
        hit = blk == idx
        sel = jnp.where(hit & (mx > -jnp.inf), 1.0, sel)
        sc = jnp.where(hit, -jnp.inf, sc)
    return sel


def _overlap_matrix(nc_pad, n_blocks_pad, nc, n_blocks):
    cs = np.arange(nc_pad) * CMP_STRIDE
    sb = np.arange(n_blocks_pad) * SLC_LEN
    ov = np.clip(np.minimum(cs[:, None] + CMP_LEN, sb[None, :] + SLC_LEN)
                 - np.maximum(cs[:, None], sb[None, :]), 0, None)
    m = (ov / CMP_STRIDE).astype(np.float32)
    m[nc:, :] = 0.0
    m[:, n_blocks:] = 0.0
    return m


def _attn_prompt_kernel(q_ref, kcb, vcb, ksb, vsb, kwb, vwb, g_ref, mt_ref, e_ref, o_ref,
                        m_scr, l_scr, acc_scr, bias_scr, gate_scr, *, tq, seq, ck):
    qi = pl.program_id(2)
    nc = seq // CMP_STRIDE - 1
    nsb = seq // SLC_LEN
    rb = min(tq, 128)

    q = q_ref[...]
    qq = jnp.concatenate([q[:, g * HEAD_DIM:(g + 1) * HEAD_DIM] for g in range(GROUP)], axis=0)
    rq = GROUP * tq
    q0 = qi * tq

    gt = jax.nn.sigmoid(g_ref[...])
    for c in range(3 * GROUP):
        gate_scr[c] = jnp.broadcast_to(gt[:, c:c + 1], (tq, HEAD_DIM))

    ncp = kcb.shape[0]
    s = lax.dot_general(qq, kcb[...], _NT, preferred_element_type=F32) * SCALE
    qpos = q0 + (lax.broadcasted_iota(jnp.int32, (rq, ncp), 0) & (tq - 1))
    col = lax.broadcasted_iota(jnp.int32, (rq, ncp), 1)
    vis = (col * CMP_STRIDE + (CMP_LEN - 1) <= qpos) & (col < nc)
    s = jnp.where(vis, s, NEG)
    mx = jnp.max(s, axis=1, keepdims=True)
    e = jnp.where(vis, jnp.exp(s - mx), 0.0)
    den = jnp.sum(e, axis=1, keepdims=True)
    p = e / jnp.where(den > 0.0, den, 1.0)
    o_cmp = jnp.dot(p.astype(BF16), vcb[...], preferred_element_type=F32)
    pg = p[0:tq]
    for g in range(1, GROUP):
        pg = pg + p[g * tq:(g + 1) * tq]

    sct = lax.dot_general(mt_ref[...], pg, _NT, preferred_element_type=F32,
                          precision=lax.Precision.HIGHEST)
    blk = lax.broadcasted_iota(jnp.int32, (nsb, tq), 0)
    cur = lax.div(q0 + lax.broadcasted_iota(jnp.int32, (nsb, tq), 1), jnp.int32(SLC_LEN))
    valid = blk <= cur
    forced = (blk == 0) | (blk > cur - N_LOCAL)
    need_topk = (q0 + tq - 1) // SLC_LEN + 1 > SLC_TOP
    sel_t = lax.cond(need_topk,
                     lambda: _select_topk(sct, valid, forced, blk, 0),
                     lambda: valid.astype(F32)).astype(BF16)

    def sweep(k_ref, v_ref, k_start, n_chunks, bias_fn):
        m_scr[...] = jnp.full((rq, HEAD_DIM), NEG, F32)
        l_scr[...] = jnp.zeros((rq, HEAD_DIM), F32)
        acc_scr[...] = jnp.zeros((rq, HEAD_DIM), F32)
        nl = ck // HEAD_DIM

        def scores(c):
            k0 = pl.multiple_of(k_start + c * ck, 128)
            return lax.dot_general(qq, k_ref[pl.ds(k0, ck), :], _NT, preferred_element_type=F32)

        def chunk_inputs(c):
            k0 = pl.multiple_of(k_start + c * ck, 128)
            return scores(c), bias_fn(c, k0)

        c_exp = np.float32(SCALE * np.log2(np.e))

        def reduce_chunk(c, sb):
            sc_, bias = sb
            k0 = pl.multiple_of(k_start + c * ck, 128)
            vch = v_ref[pl.ds(k0, ck), :]
            alphas, ps = [], []
            nblk = rq // rb
            for r in range(nblk):
                rows = slice(r * rb, (r + 1) * rb)
                b0 = (r * rb) % tq
                sg = [sc_[rows, j * HEAD_DIM:(j + 1) * HEAD_DIM]
                      + bias[b0:b0 + rb, j * HEAD_DIM:(j + 1) * HEAD_DIM] for j in range(nl)]
                mx = functools.reduce(jnp.maximum, sg)
                m_old = m_scr[rows]
                m_new = jnp.maximum(m_old, jnp.max(mx, axis=1, keepdims=True))
                alpha = jnp.exp2((m_old - m_new) * c_exp)
                pj = [jnp.exp2((x - m_new) * c_exp) for x in sg]
                psum = functools.reduce(lambda a, b: a + b, pj)
                l_scr[rows] = alpha * l_scr[rows] + jnp.sum(psum, axis=1, keepdims=True)
                m_scr[rows] = m_new
                alphas.append(alpha)
                ps.append(jnp.concatenate([x.astype(BF16) for x in pj], axis=1))
                if (r + 1) % (nblk // 2) == 0:
                    half = slice((r + 1 - nblk // 2) * rb, (r + 1) * rb)
                    pv = jnp.dot(jnp.concatenate(ps, axis=0), vch, preferred_element_type=F32)
                    acc_scr[half] = jnp.concatenate(alphas, axis=0) * acc_scr[half] + pv
                    alphas, ps = [], []

        def body(c, carry):
            reduce_chunk(c, chunk_inputs(c))
            return carry

        lax.fori_loop(0, n_chunks, body, 0)
        return acc_scr[...] / l_scr[...]

    qp = q0 + lax.broadcasted_iota(jnp.int32, (tq, ck), 0)
    kcol = lax.broadcasted_iota(jnp.int32, (tq, ck), 1)

    eye = (lax.broadcasted_iota(jnp.int32, (nsb, HEAD_DIM), 0)
           == lax.broadcasted_iota(jnp.int32, (nsb, HEAD_DIM), 1)).astype(BF16)
    sel_q = lax.dot_general(sel_t, eye, _TN, preferred_element_type=F32).astype(BF16)

    n_slc = (q0 + tq + ck - 1) // ck

    def put_mask(c):
        selq = jnp.dot(sel_q, e_ref[c], preferred_element_type=F32)
        bias_scr[c] = jnp.where((selq > 0.5) & (c * ck + kcol <= qp), 0.0, NEG)

    def mask_body(i, carry):
        put_mask(2 * i)
        put_mask(jnp.minimum(2 * i + 1, seq // ck - 1))
        return carry

    lax.fori_loop(0, (n_slc + 1) // 2, mask_body, 0)
    o_slc = sweep(ksb, vsb, 0, n_slc, lambda c, k0: bias_scr[c])

    span = -(-(WINDOW + tq) // ck) * ck

    def win_bias(c, k0):
        dlt = qp - (k0 + kcol)
        return jnp.where((dlt >= 0) & (dlt < WINDOW), 0.0, NEG)

    o_win = sweep(kwb, vwb, jnp.maximum(q0 + tq - span, 0), span // ck, win_bias)

    for g in range(GROUP):
        r0, r1 = g * tq, (g + 1) * tq
        og = gate_scr[g] * o_cmp[r0:r1] + gate_scr[GROUP + g] * o_slc[r0:r1]
        og = og + gate_scr[2 * GROUP + g] * o_win[r0:r1]
        o_ref[:, g * HEAD_DIM:(g + 1) * HEAD_DIM] = og.astype(o_ref.dtype)


def _attn_prompt(q, kcv, kv, win, g_re, *, nb, seq, tq=512, ck=512):
    assert tq & (tq - 1) == 0 and seq % ck == 0 and WINDOW % tq == 0
    t = q.shape[0]
    ncp = seq // CMP_STRIDE
    nsb = seq // SLC_LEN
    nq = seq // tq
    mt = jnp.asarray(_overlap_matrix(ncp, nsb, ncp - 1, nsb).T)
    kk = np.arange(seq)
    assert nsb <= HEAD_DIM
    e3 = (kk[None, :] // SLC_LEN == np.arange(HEAD_DIM)[:, None]).astype(np.float32)
    e3 = jnp.asarray(e3.reshape(HEAD_DIM, seq // ck, ck).transpose(1, 0, 2), dtype=BF16)
    slab = lambda cb: pl.BlockSpec((seq, HEAD_DIM), lambda b, h, i, cb=cb: (b, cb + h))
    return pl.pallas_call(
        functools.partial(_attn_prompt_kernel, tq=tq, seq=seq, ck=ck),
        grid=(nb, N_KV, nq),
        in_specs=[pl.BlockSpec((tq, GROUP * HEAD_DIM), lambda b, h, i: (b * nq + i, h)),
                  pl.BlockSpec((None, None, None, ncp, HEAD_DIM), lambda b, h, i: (b, 0, h, 0, 0)),
                  pl.BlockSpec((None, None, None, ncp, HEAD_DIM), lambda b, h, i: (b, 1, h, 0, 0)),
                  slab(0), slab(N_KV), slab(0), slab(N_KV),
                  pl.BlockSpec((None, None, tq, 4 * GROUP), lambda b, h, i: (b, h, i, 0)),
                  pl.BlockSpec((nsb, ncp), lambda b, h, i: (0, 0)),
                  pl.BlockSpec((seq // ck, HEAD_DIM, ck), lambda b, h, i: (0, 0, 0))],
        out_specs=pl.BlockSpec((tq, GROUP * HEAD_DIM), lambda b, h, i: (b * nq + i, h)),
        out_shape=jax.ShapeDtypeStruct((t, N_HEADS * HEAD_DIM), BF16),
        scratch_shapes=[pltpu.VMEM((GROUP * tq, HEAD_DIM), F32)] * 3
        + [pltpu.VMEM((seq // ck, tq, ck), F32), pltpu.VMEM((3 * GROUP, tq, HEAD_DIM), F32)],
        compiler_params=_cparams(("arbitrary", "arbitrary", "arbitrary")),
        name="attn_prompt",
    )(q, kcv, kcv, kv, kv, win, win, g_re, mt, e3)


def _attn_sample_kernel(pt_ref, cache_ref, q_ref, kcv_ref, kvn_ref, wn_ref, st_ref, g_ref, ms_ref, e_ref, o_ref,
                        wso_ref, m_scr, l_scr, acc_scr, sel_scr, ocmp_scr, kv_buf, sem, *, past, ds):
    b = pl.program_id(0)
    p = pl.program_id(1)
    n_steps = pl.num_programs(1)

    def page_copies(bb, pp, slot):
        cps = []
        for n in range(PAGES_PER_STEP):
            page = pt_ref[bb, pp * PAGES_PER_STEP + n]
            for cb in range(2 * N_KV):
                cps.append(pltpu.make_async_copy(cache_ref.at[page, :, 2 * N_KV + cb, :],
                                                 kv_buf.at[slot, cb, pl.ds(n * PAGE, PAGE), :],
                                                 sem.at[slot]))
        return cps

    t = b * n_steps + p
    slot = lax.rem(t, 2)

    @pl.when(t == 0)
    def _():
        for cp in page_copies(b, p, slot):
            cp.start()

    @pl.when(t + 1 < pl.num_programs(0) * n_steps)
    def _():
        wrap = p + 1 == n_steps
        for cp in page_copies(jnp.where(wrap, b + 1, b), jnp.where(wrap, 0, p + 1), 1 - slot):
            cp.start()
    rq = GROUP * ds
    ncp = kcv_ref.shape[2]
    nc = ncp - 1
    n_sel_chunks = sel_scr.shape[1]
    lanes = n_sel_chunks * HEAD_DIM

    def q_rows(h):
        c0 = h * GROUP * HEAD_DIM
        return jnp.concatenate(
            [q_ref[:, c0 + g * HEAD_DIM:c0 + (g + 1) * HEAD_DIM] for g in range(GROUP)], axis=0).astype(BF16)

    def online_update(h, sc_, vals):
        m_i = m_scr[h]
        m_new = jnp.maximum(m_i, jnp.max(sc_, axis=1, keepdims=True))
        alpha = jnp.exp(m_i - m_new)
        pp = jnp.exp(sc_ - m_new)
        l_scr[h] = alpha * l_scr[h] + jnp.sum(pp, axis=1, keepdims=True)
        acc_scr[h] = alpha * acc_scr[h] + jnp.dot(pp.astype(BF16), vals, preferred_element_type=F32)
        m_scr[h] = m_new

    @pl.when(p == 0)
    def _():
        pgs = []
        for h in range(N_KV):
            qq = q_rows(h)
            kc = kcv_ref[0, h]
            vc = kcv_ref[1, h]
            s = lax.dot_general(qq, kc, _NT, preferred_element_type=F32) * SCALE
            qpos = past + (lax.broadcasted_iota(jnp.int32, (rq, ncp), 0) & (ds - 1))
            col = lax.broadcasted_iota(jnp.int32, (rq, ncp), 1)
            vis = (col * CMP_STRIDE + (CMP_LEN - 1) <= qpos) & (col < nc)
            s = jnp.where(vis, s, NEG)
            mx = jnp.max(s, axis=1, keepdims=True)
            e = jnp.where(vis, jnp.exp(s - mx), 0.0)
            den = jnp.sum(e, axis=1, keepdims=True)
            pr = e / jnp.where(den > 0.0, den, 1.0)
            ocmp_scr[h] = jnp.dot(pr.astype(BF16), vc, preferred_element_type=F32)
            pg = pr[0:ds]
            for g in range(1, GROUP):
                pg = pg + pr[g * ds:(g + 1) * ds]
            pgs.append(pg)
            m_scr[h] = jnp.full((rq, 1), NEG, F32)
            l_scr[h] = jnp.zeros((rq, 1), F32)
            acc_scr[h] = jnp.zeros((rq, HEAD_DIM), F32)
        nr = N_KV * ds
        pg_all = jnp.concatenate(pgs, axis=0)
        hi = pg_all.astype(BF16)
        r1 = pg_all - hi.astype(F32)
        mid = r1.astype(BF16)
        lo = (r1 - mid.astype(F32)).astype(BF16)
        sc3 = jnp.dot(jnp.concatenate([hi, mid, lo], axis=0), ms_ref[...], preferred_element_type=F32)
        score = (sc3[0:nr] + sc3[nr:2 * nr]) + sc3[2 * nr:3 * nr]
        lane = lax.broadcasted_iota(jnp.int32, (nr, lanes), 1)
        lane_in = lax.rem(lane, jnp.int32(HEAD_DIM))
        real = lane_in < BLK_PER_CHUNK
        blk = jnp.where(real, lax.div(lane, jnp.int32(HEAD_DIM)) * BLK_PER_CHUNK + lane_in, (1 << 19) + lane)
        qrow = lax.broadcasted_iota(jnp.int32, (nr, lanes), 0) & (ds - 1)
        cur = lax.div(past + qrow, jnp.int32(SLC_LEN))
        valid = real & (blk <= cur)
        forced = (blk == 0) | (blk > cur - N_LOCAL)
        sel = _select_topk(score, valid, forced, blk, 1)
        for h in range(N_KV):
            for c in range(n_sel_chunks):
                sel_scr[h, c] = sel[h * ds:(h + 1) * ds, c * HEAD_DIM:(c + 1) * HEAD_DIM]

    for cp in page_copies(b, p, slot):
        cp.wait()

    k0 = p * CHUNK_ROWS
    for h in range(N_KV):
        qq = q_rows(h)
        kch = kv_buf[slot, h].astype(BF16)
        vch = kv_buf[slot, N_KV + h].astype(BF16)
        sc_ = lax.dot_general(qq, kch, _NT, preferred_element_type=F32) * SCALE
        selq = jnp.dot(sel_scr[h, p].astype(BF16), e_ref[...], preferred_element_type=F32)
        kpos = k0 + lax.broadcasted_iota(jnp.int32, (ds, CHUNK_ROWS), 1)
        qp = past + lax.broadcasted_iota(jnp.int32, (ds, CHUNK_ROWS), 0)
        okf = jnp.where((selq > 0.5) & (kpos <= qp), 1.0, 0.0)
        ok = jnp.concatenate([okf] * GROUP, axis=0) > 0.5
        online_update(h, jnp.where(ok, sc_, NEG), vch)

    @pl.when(p == n_steps - 1)
    def _():
        gt = jax.nn.sigmoid(g_ref[...])
        zpad = jnp.zeros((HEAD_DIM - ds, HEAD_DIM), F32)
        for h in range(N_KV):
            qq = q_rows(h)
            c_k = 2 * KV_W + h * HEAD_DIM
            c_v = 3 * KV_W + h * HEAD_DIM
            kn = jnp.concatenate([kvn_ref[:, c_k:c_k + HEAD_DIM], zpad], axis=0).astype(BF16)
            vn = jnp.concatenate([kvn_ref[:, c_v:c_v + HEAD_DIM], zpad], axis=0).astype(BF16)
            sn = lax.dot_general(qq, kn, _NT, preferred_element_type=F32) * SCALE
            last_sel = sel_scr[h, n_sel_chunks - 1]
            nb_last = (past // SLC_LEN) % BLK_PER_CHUNK
            seln = jnp.sum(jnp.where(lax.broadcasted_iota(jnp.int32, (ds, HEAD_DIM), 1) == nb_last,
                                     last_sel, 0.0), axis=1, keepdims=True)
            seln = jnp.concatenate([seln] * GROUP, axis=0)
            srow = lax.broadcasted_iota(jnp.int32, (rq, HEAD_DIM), 0) & (ds - 1)
            kcol = lax.broadcasted_iota(jnp.int32, (rq, HEAD_DIM), 1)
            okn = (seln > 0.5) & (kcol <= srow) & (kcol < ds)
            online_update(h, jnp.where(okn, sn, NEG), vn)
            o_slc = acc_scr[h] / l_scr[h]
            wb = st_ref.shape[0]
            band = wb + HEAD_DIM
            st_flat = st_ref.reshape(wb * 8, HEAD_DIM)
            kw = jnp.concatenate([st_flat[pl.ds(h, wb, stride=8), :],
                                  wn_ref[:, h * HEAD_DIM:(h + 1) * HEAD_DIM], zpad], axis=0).astype(BF16)
            vw = jnp.concatenate([st_flat[pl.ds(N_KV + h, wb, stride=8), :],
                                  wn_ref[:, KV_W + h * HEAD_DIM:KV_W + (h + 1) * HEAD_DIM], zpad],
                                 axis=0).astype(BF16)
            sw = lax.dot_general(qq, kw, _NT, preferred_element_type=F32) * SCALE
            srw = lax.broadcasted_iota(jnp.int32, (rq, band), 0) & (ds - 1)
            idx = lax.broadcasted_iota(jnp.int32, (rq, band), 1)
            dlt = (wb + srw) - idx
            sw = jnp.where((dlt >= 0) & (dlt < WINDOW) & (idx < wb + ds), sw, NEG)
            ew = jnp.exp(sw - jnp.max(sw, axis=1, keepdims=True))
            o_win = jnp.dot(ew.astype(BF16), vw, preferred_element_type=F32) / jnp.sum(ew, axis=1, keepdims=True)
            o_cmp = ocmp_scr[h]
            gh = gt[h]
            for g in range(GROUP):
                r0, r1 = g * ds, (g + 1) * ds
                og = gh[:, g:g + 1] * o_cmp[r0:r1] + gh[:, GROUP + g:GROUP + g + 1] * o_slc[r0:r1]
                og = og + gh[:, 2 * GROUP + g:2 * GROUP + g + 1] * o_win[r0:r1]
                c0 = (h * GROUP + g) * HEAD_DIM
                o_ref[:, c0:c0 + HEAD_DIM] = og
        wb = st_ref.shape[0]
        wso_ref[0:wb - ds] = st_ref[ds:wb]
        wso_flat = wso_ref.reshape(wb * 8, HEAD_DIM)
        for cb in range(2 * N_KV):
            wso_flat[pl.ds((wb - ds) * 8 + cb, ds, stride=8), :] = wn_ref[:, cb * HEAD_DIM:(cb + 1) * HEAD_DIM]


def _attn_sample(q, kcv, kv_new, win_new, win_state, g_re, cache4, page_table, *, ds):
    nb, npg = page_table.shape
    past = npg * PAGE
    assert ds & (ds - 1) == 0 and ds <= SLC_LEN and past % SLC_LEN == 0 and npg % PAGES_PER_STEP == 0
    steps = npg // PAGES_PER_STEP
    ncp = kcv.shape[3]
    wb = win_state.shape[1]
    n_blocks = past // SLC_LEN + 1
    n_sel_chunks = -(-n_blocks // BLK_PER_CHUNK)
    m = _overlap_matrix(ncp, n_sel_chunks * BLK_PER_CHUNK, ncp - 1, n_blocks)
    ms = np.zeros((ncp, n_sel_chunks, HEAD_DIM), np.float32)
    ms[:, :, :BLK_PER_CHUNK] = m.reshape(ncp, n_sel_chunks, BLK_PER_CHUNK)
    ms = jnp.asarray(ms.reshape(ncp, n_sel_chunks * HEAD_DIM), dtype=BF16)
    ee = np.zeros((HEAD_DIM, CHUNK_ROWS), np.float32)
    ee[:BLK_PER_CHUNK] = np.arange(CHUNK_ROWS)[None, :] // SLC_LEN == np.arange(BLK_PER_CHUNK)[:, None]
    ee = jnp.asarray(ee, dtype=BF16)
    rq = GROUP * ds

    grid_spec = pltpu.PrefetchScalarGridSpec(
        num_scalar_prefetch=1,
        grid=(nb, steps),
        in_specs=[
            pl.BlockSpec(memory_space=pl.ANY),
            pl.BlockSpec((ds, N_HEADS * HEAD_DIM), lambda b, p, pt: (b, 0)),
            pl.BlockSpec((None, 2, N_KV, ncp, HEAD_DIM), lambda b, p, pt: (b, 0, 0, 0, 0)),
            pl.BlockSpec((ds, 4 * KV_W), lambda b, p, pt: (b, 0)),
            pl.BlockSpec((ds, 2 * KV_W), lambda b, p, pt: (b, 0)),
            pl.BlockSpec((None, wb, 8, HEAD_DIM), lambda b, p, pt: (b, 0, 0, 0)),
            pl.BlockSpec((None, N_KV, ds, 4 * GROUP), lambda b, p, pt: (b, 0, 0, 0)),
            pl.BlockSpec(ms.shape, lambda b, p, pt: (0, 0)),
            pl.BlockSpec(ee.shape, lambda b, p, pt: (0, 0))],
        out_specs=[pl.BlockSpec((ds, N_HEADS * HEAD_DIM), lambda b, p, pt: (b, 0)),
                   pl.BlockSpec((None, wb, 8, HEAD_DIM), lambda b, p, pt: (b, 0, 0, 0))],
        scratch_shapes=[pltpu.VMEM((N_KV, rq, 1), F32), pltpu.VMEM((N_KV, rq, 1), F32),
                        pltpu.VMEM((N_KV, rq, HEAD_DIM), F32),
                        pltpu.VMEM((N_KV, n_sel_chunks, ds, HEAD_DIM), F32),
                        pltpu.VMEM((N_KV, rq, HEAD_DIM), F32),
                        pltpu.VMEM((2, 2 * N_KV, CHUNK_ROWS, HEAD_DIM), F32),
                        pltpu.SemaphoreType.DMA((2,))],
    )
    return pl.pallas_call(
        functools.partial(_attn_sample_kernel, past=past, ds=ds),
        grid_spec=grid_spec,
        out_shape=[jax.ShapeDtypeStruct((nb * ds, N_HEADS * HEAD_DIM), F32),
                   jax.ShapeDtypeStruct((nb, wb, 8, HEAD_DIM), F32)],
        compiler_params=_cparams(("arbitrary", "arbitrary")),
        name="attn_sample",
    )(page_table, cache4, q, kcv, kv_new, win_new, win_state, g_re, ms, ee)


_Q0, _KV0, _WIN0, _G0 = 0, 2048, 4096, 5120
_REST0 = _G0 + 3 * N_HEADS


def _split_w_in(w_in):
    wt = w_in.T.astype(BF16)
    return wt, wt[_REST0:]


def _gate_layout(g_logits, nb, seq):
    g = g_logits[:, :3 * N_HEADS].reshape(nb, seq, 3, N_KV, GROUP)
    g = g.transpose(0, 3, 1, 2, 4).reshape(nb, N_KV, seq, 3 * GROUP)
    return jnp.pad(g, ((0, 0), (0, 0), (0, 0), (0, GROUP)))


def _cmp_weights(w_k1, w_k2, pe_k, w_v1, w_v2, pe_v):
    def by_row(w1):
        w = w1.reshape(2, CMP_STRIDE, HEAD_DIM, HEAD_DIM).transpose(1, 2, 0, 3)
        return w.reshape(CMP_STRIDE * HEAD_DIM, 2 * HEAD_DIM)
    w1pairs = jnp.concatenate([by_row(w_k1), by_row(w_v1)], axis=1).astype(BF16)
    pe8 = jnp.stack([jnp.broadcast_to(pe_k.reshape(1, -1), (8, CMP_LEN * HEAD_DIM)),
                     jnp.broadcast_to(pe_v.reshape(1, -1), (8, CMP_LEN * HEAD_DIM))])
    return w1pairs, pe8, jnp.stack([w_k1, w_v1]), jnp.stack([w_k2, w_v2])


def kernel(x_prompt, x_sample, c_prompt, c_sample, cache_nsa_kv, page_table, state_win_kv, state_conv, w_ada, b_ada, norm1_g, norm2_g, w_in, w_cmp_k1, w_cmp_k2, pe_cmp_k, w_cmp_v1, w_cmp_v2, pe_cmp_v, conv_w, conv_b, w_attn_proj, w_conv_proj, w_out, w_mlp1, w_mlp2, normf_g):
    d = D_MODEL
    nbp, seq, _ = x_prompt.shape
    nbs, ds, _ = x_sample.shape
    tp = nbp * seq
    ts = nbs * ds
    tc = 256
    depth = w_in.shape[0]
    assert depth == 1

    xp = x_prompt.reshape(tp, d)
    xs = x_sample.reshape(ts, d)
    normf = normf_g.reshape(1, d)

    l = 0
    c_all = jnp.concatenate([c_prompt, c_sample, jnp.zeros((16 - nbp - nbs, d), F32)], axis=0)
    ada = _ada(c_all, w_ada[l], b_ada[l].reshape(1, -1)).reshape(16, 6, d)
    ada_p = [ada[:nbp, k][:, None, :] for k in range(6)]
    ada_s = [jnp.repeat(ada[nbp:nbp + nbs, k], ds, axis=0)[None] for k in range(6)]

    w_head, w_tail = _split_w_in(w_in[l])
    gates0 = 3 * d
    wa = w_attn_proj[l]
    wc = w_conv_proj[l]
    wo = w_out[l]
    w1 = w_mlp1[l].astype(BF16)
    w2 = w_mlp2[l].astype(BF16)
    g1n = norm1_g[l].reshape(1, d)
    g2n = norm2_g[l].reshape(1, d)
    cw = conv_w[l]
    cb = conv_b[l].reshape(1, d)
    w1pairs, pe8, w1s, w2s = _cmp_weights(w_cmp_k1[l], w_cmp_k2[l], pe_cmp_k[l],
                                          w_cmp_v1[l], w_cmp_v2[l], pe_cmp_v[l])

    tm = 1024
    hp = _norm_mod(xp, g1n, ada_p[1], ada_p[0], tm=512, seq=seq)
    q_p = _mm(hp, w_head, tm=tm, tn=1024, out_dtype=BF16, name="mm_q", col0=_Q0, n=N_HEADS * HEAD_DIM)
    wbp = min(WINDOW, seq)
    kv6_p, kvb_p = _mm_kv(hp, w_head, tm=tm, col0=_KV0)
    winb_p, win6_p = _mm_win(hp, w_head, tm=wbp, seq=seq, col0=_WIN0)
    g_p = _mm(hp, w_head, tm=tm, tn=HEAD_DIM, out_dtype=F32, name="mm_g", col0=_G0, n=HEAD_DIM)
    gates_p = _mm(hp, w_tail, tm=tm, tn=1024, out_dtype=BF16, act="sigmoid", name="mm_gates",
                  col0=gates0, n=2 * d)
    z_p, conv_tiles = _mm_conv(hp, w_tail, cw, cb, tm=tm, tc=tc, seq=seq, col0=0)

    pt_p = jnp.arange(tp // PAGE, dtype=jnp.int32).reshape(nbp, seq // PAGE)
    part_p = _cmp_part(kv6_p.reshape(tp // PAGE, PAGE, 16, HEAD_DIM), pt_p, w1pairs)
    kcv_p = _cmp_finish(part_p, pe8, w1s, w2s)
    o_p = _attn_prompt(q_p, kcv_p, kvb_p, winb_p, _gate_layout(g_p, nbp, seq), nb=nbp, seq=seq)

    mixed_p = _mix(o_p, z_p, wa, wc, gates_p, tm=tm, tn=512)
    x1_p, h2_p = _mm_resid_norm(mixed_p, wo, xp, ada_p[2], g2n, ada_p[4], ada_p[3], tm=512, seq=seq)
    y_p = _mlp_final(h2_p, w1, w2, x1_p, ada_p[5], normf, tm=512, tf=1024, seq=seq)

    hs = _norm_mod(xs, g1n, ada_s[1], ada_s[0], tm=ts, seq=ds)
    q_s = _mm(hs, w_head, tm=ts, tn=512, out_dtype=F32, name="mm_q_s", col0=_Q0, n=N_HEADS * HEAD_DIM)
    kv_s = _mm(hs, w_head, tm=ts, tn=512, out_dtype=F32, name="mm_kv_s", col0=_KV0, n=4 * KV_W)
    win_s = _mm(hs, w_head, tm=ts, tn=512, out_dtype=F32, name="mm_win_s", col0=_WIN0, n=2 * KV_W)
    g_s = _mm(hs, w_head, tm=ts, tn=HEAD_DIM, out_dtype=F32, name="mm_g_s", col0=_G0, n=HEAD_DIM)
    gates_s = _mm(hs, w_tail, tm=ts, tn=512, out_dtype=F32, act="sigmoid", name="mm_gates_s",
                  col0=gates0, n=2 * d)
    ubc_s = _mm(hs, w_tail, tm=ts, tn=512, out_dtype=F32, name="mm_ubc_s", col0=0, n=3 * d)
    st = state_conv[l]
    z_s, cu_s = _sample_conv(ubc_s, jnp.repeat(st[:, 0], ds, axis=0), jnp.repeat(st[:, 1], ds, axis=0),
                             cw, cb, tc=tc, seq=ds)

    cache4 = cache_nsa_kv[l].reshape(cache_nsa_kv.shape[1], PAGE, 16, HEAD_DIM)
    part_s = _cmp_part(cache4, page_table, w1pairs)
    kcv_s = _cmp_finish(part_s, pe8, w1s, w2s)
    wbs = state_win_kv.shape[2]
    wst = state_win_kv[l].reshape(nbs, wbs, 8, HEAD_DIM)
    o_s, wst_next = _attn_sample(q_s, kcv_s, kv_s, win_s, wst, _gate_layout(g_s, nbs, ds), cache4, page_table,
                                 ds=ds)

    mixed_s = _mix(o_s, z_s, wa, wc, gates_s, tm=ts, tn=512)
    x1_s, h2_s = _mm_resid_norm(mixed_s, wo, xs, ada_s[2], g2n, ada_s[4], ada_s[3], tm=ts, seq=ds)
    y_s = _mlp_final(h2_s, w1, w2, x1_s, ada_s[5], normf, tm=ts, tf=512, seq=ds)

    kv_prompt = kv6_p.reshape(1, nbp, seq, 4, N_KV, HEAD_DIM)
    kv_sample = kv_s.reshape(1, nbs, ds, 4, N_KV, HEAD_DIM)
    win_prompt = win6_p.reshape(1, nbp, wbp, 2, N_KV, HEAD_DIM)
    win_sample = wst_next.reshape(1, nbs, wbs, 2, N_KV, HEAD_DIM)
    tiles_per_seq = seq // tm
    conv_prompt = conv_tiles[tiles_per_seq - 1::tiles_per_seq][None]
    conv_sample = cu_s.reshape(nbs, ds, d)[None, :, ds - (CONV_W - 1):]
    return (y_p.reshape(nbp, seq, d), y_s.reshape(nbs, ds, d), kv_prompt, kv_sample,
            win_prompt, win_sample, conv_prompt, conv_sample)
```

```python
import functools

import numpy as np
import jax
import jax.numpy as jnp
from jax import lax
from jax.experimental import pallas as pl
from jax.experimental.pallas import tpu as pltpu

F32 = jnp.float32
BF16 = jnp.bfloat16

D_MODEL = 2048
HEAD_DIM = 128
N_HEADS = 16
N_KV = 4
GROUP = 4
KV_W = N_KV * HEAD_DIM
CMP_LEN = 32
CMP_STRIDE = 16
SLC_LEN = 64
SLC_TOP = 16
N_LOCAL = 2
WINDOW = 512
CONV_W = 3
RMS_EPS = 1e-6
NEG = -1e30
SCALE = HEAD_DIM ** -0.5
PAGE = 128
PAGES_PER_STEP = 16
CHUNK_ROWS = PAGES_PER_STEP * PAGE
BLK_PER_CHUNK = CHUNK_ROWS // SLC_LEN
VMEM_LIMIT = 56 * 1024 * 1024

_NT = (((1,), (1,)), ((), ()))
_TN = (((0,), (0,)), ((), ()))


def _cparams(sem):
    return pltpu.CompilerParams(dimension_semantics=sem, vmem_limit_bytes=VMEM_LIMIT)


def _ada_kernel(c_ref, w_ref, b_ref, o_ref):
    c = c_ref[...]
    a = (c * jax.nn.sigmoid(c)).astype(BF16)
    o_ref[...] = jnp.dot(a, w_ref[...].astype(BF16), preferred_element_type=F32) + b_ref[...]


def _ada(c, w, b, tn=1024):
    m, k = c.shape
    n = w.shape[1]
    return pl.pallas_call(
        _ada_kernel,
        grid=(n // tn,),
        in_specs=[pl.BlockSpec((m, k), lambda j: (0, 0)),
                  pl.BlockSpec((k, tn), lambda j: (0, j)),
                  pl.BlockSpec((1, tn), lambda j: (0, j))],
        out_specs=pl.BlockSpec((m, tn), lambda j: (0, j)),
        out_shape=jax.ShapeDtypeStruct((m, n), F32),
        compiler_params=_cparams(("arbitrary",)),
        name="ada",
    )(c, w, b)


def _rowspec(p, tm, tn, seq, col=True):
    gr = p.shape[1]
    if gr == 1:
        if col:
            return pl.BlockSpec((1, 1, tn), lambda i, j: ((i * tm) // seq, 0, j))
        return pl.BlockSpec((1, 1, tn), lambda i: ((i * tm) // seq, 0, 0))
    if col:
        return pl.BlockSpec((1, tm, tn), lambda i, j: (0, i, j))
    return pl.BlockSpec((1, tm, tn), lambda i: (0, i, 0))


def _norm_kernel(x_ref, g_ref, sc_ref, sh_ref, o_ref):
    x = x_ref[...]
    r = lax.rsqrt(jnp.mean(x * x, axis=-1, keepdims=True) + RMS_EPS)
    y = (x * r) * g_ref[...]
    o_ref[...] = (y * (1.0 + sc_ref[0]) + sh_ref[0]).astype(o_ref.dtype)


def _norm_mod(x, g, sc, sh, *, tm, seq):
    t, d = x.shape
    return pl.pallas_call(
        _norm_kernel,
        grid=(t // tm,),
        in_specs=[pl.BlockSpec((tm, d), lambda i: (i, 0)),
                  pl.BlockSpec((1, d), lambda i: (0, 0)),
                  _rowspec(sc, tm, d, seq, col=False),
                  _rowspec(sh, tm, d, seq, col=False)],
        out_specs=pl.BlockSpec((tm, d), lambda i: (i, 0)),
        out_shape=jax.ShapeDtypeStruct((t, d), BF16),
        compiler_params=_cparams(("arbitrary",)),
        name="norm_mod",
    )(x, g, sc, sh)


def _xwt(x_ref, wt_ref):
    return lax.dot_general(x_ref[...].astype(BF16), wt_ref[...], _NT, preferred_element_type=F32)


def _mm_kernel(x_ref, w_ref, o_ref, *, act):
    acc = _xwt(x_ref, w_ref)
    if act == "sigmoid":
        acc = jax.nn.sigmoid(acc)
    o_ref[...] = acc.astype(o_ref.dtype)


def _mm_wstat_kernel(x_ref, w_ref, o_ref, wbf_ref, *, act):
    @pl.when(pl.program_id(1) == 0)
    def _():
        wbf_ref[...] = w_ref[...].astype(BF16)

    acc = _xwt(x_ref, wbf_ref)
    if act == "sigmoid":
        acc = jax.nn.sigmoid(acc)
    o_ref[...] = acc.astype(o_ref.dtype)


def _wrows(tn, k, row0):
    return pl.BlockSpec((pl.Element(tn), pl.Element(k)), lambda j, i: (pl.multiple_of(row0 + j * tn, 8), 0))


def _mm_wstat(x, w, *, tm, tn, out_dtype, act=None, name="mm", col0, n):
    t, k = x.shape
    assert col0 % 8 == 0 and n % tn == 0
    return pl.pallas_call(
        functools.partial(_mm_wstat_kernel, act=act),
        grid=(n // tn, t // tm),
        in_specs=[pl.BlockSpec((tm, k), lambda j, i: (i, 0)),
                  _wrows(tn, k, col0)],
        out_specs=pl.BlockSpec((tm, tn), lambda j, i: (i, j)),
        out_shape=jax.ShapeDtypeStruct((t, n), out_dtype),
        scratch_shapes=[pltpu.VMEM((tn, k), BF16)],
        compiler_params=_cparams(("arbitrary", "arbitrary")),
        name=name,
    )(x, w)


def _mm(x, w, *, tm, tn, out_dtype, act=None, name="mm", col0, n):
    t, k = x.shape
    assert col0 % tn == 0 and n % tn == 0
    cb0 = col0 // tn
    return pl.pallas_call(
        functools.partial(_mm_kernel, act=act),
        grid=(t // tm, n // tn),
        in_specs=[pl.BlockSpec((tm, k), lambda i, j: (i, 0)),
                  pl.BlockSpec((tn, k), lambda i, j: (cb0 + j, 0))],
        out_specs=pl.BlockSpec((tm, tn), lambda i, j: (i, j)),
        out_shape=jax.ShapeDtypeStruct((t, n), out_dtype),
        compiler_params=_cparams(("arbitrary", "arbitrary")),
        name=name,
    )(x, w)


def _store_head_major(o_ref, acc, tm):
    flat = o_ref.reshape(tm * 8, HEAD_DIM)
    for cb in range(8):
        flat[pl.ds(cb, tm, stride=8), :] = acc[:, cb * HEAD_DIM:(cb + 1) * HEAD_DIM]


def _mm_kv_kernel(x_ref, w_ref, o6_ref, ob_ref, *, tm):
    acc = _xwt(x_ref, w_ref)
    _store_head_major(o6_ref, acc, tm)

    @pl.when(pl.program_id(1) == 1)
    def _():
        ob_ref[...] = acc.astype(BF16)


def _mm_kv(h, w, *, tm, col0):
    t, k = h.shape
    tn = 2 * KV_W
    cb0 = col0 // tn
    return pl.pallas_call(
        functools.partial(_mm_kv_kernel, tm=tm),
        grid=(t // tm, 2),
        in_specs=[pl.BlockSpec((tm, k), lambda i, j: (i, 0)),
                  pl.BlockSpec((tn, k), lambda i, j: (cb0 + j, 0))],
        out_specs=[pl.BlockSpec((tm, 8, HEAD_DIM), lambda i, j: (i, j, 0)),
                   pl.BlockSpec((tm, tn), lambda i, j: (i, 0))],
        out_shape=[jax.ShapeDtypeStruct((t, 16, HEAD_DIM), F32),
                   jax.ShapeDtypeStruct((t, tn), BF16)],
        compiler_params=_cparams(("arbitrary", "arbitrary")),
        name="mm_kv",
    )(h, w)


def _mm_win_kernel(x_ref, w_ref, wg_ref, ob_ref, o6_ref, og_ref, *, tm, seq):
    acc = _xwt(x_ref, w_ref)
    ob_ref[...] = acc.astype(BF16)
    og_ref[...] = _xwt(x_ref, wg_ref)

    @pl.when(((pl.program_id(0) + 1) * tm) % seq == 0)
    def _():
        _store_head_major(o6_ref, acc, tm)


def _mm_win(h, w, *, tm, seq, col0, gcol0):
    t, k = h.shape
    tn = 2 * KV_W
    cb0 = col0 // tn
    gb0 = gcol0 // HEAD_DIM
    return pl.pallas_call(
        functools.partial(_mm_win_kernel, tm=tm, seq=seq),
        grid=(t // tm,),
        in_specs=[pl.BlockSpec((tm, k), lambda i: (i, 0)),
                  pl.BlockSpec((tn, k), lambda i: (cb0, 0)),
                  pl.BlockSpec((HEAD_DIM, k), lambda i: (gb0, 0))],
        out_specs=[pl.BlockSpec((tm, tn), lambda i: (i, 0)),
                   pl.BlockSpec((tm, 8, HEAD_DIM), lambda i: ((i * tm) // seq, 0, 0)),
                   pl.BlockSpec((tm, HEAD_DIM), lambda i: (i, 0))],
        out_shape=[jax.ShapeDtypeStruct((t, tn), BF16),
                   jax.ShapeDtypeStruct((t // seq * tm, 8, HEAD_DIM), F32),
                   jax.ShapeDtypeStruct((t, HEAD_DIM), F32)],
        compiler_params=_cparams(("arbitrary",)),
        name="mm_win",
    )(h, w, w)


def _conv_taps(cu, prev1, prev2, bgate, cw_ref, cb_ref):
    y = cb_ref[...] + prev2 * cw_ref[0:1, :]
    y = y + prev1 * cw_ref[1:2, :]
    y = y + cu * cw_ref[2:3, :]
    return bgate * y


def _mm_conv_kernel(x_ref, wu_ref, wb_ref, wc_ref, cw_ref, cb_ref, z_ref, st_ref, w3_ref, carry_ref, *, tm, tc, seq):
    i = pl.program_id(1)

    @pl.when(i == 0)
    def _():
        w3_ref[0:tc] = wu_ref[...].astype(BF16)
        w3_ref[tc:2 * tc] = wb_ref[...].astype(BF16)
        w3_ref[2 * tc:3 * tc] = wc_ref[...].astype(BF16)

    acc = _xwt(x_ref, w3_ref)
    u = acc[:, 0:tc]
    bgate = acc[:, tc:2 * tc]
    cu = acc[:, 2 * tc:3 * tc] * u

    @pl.when((i * tm) % seq == 0)
    def _():
        carry_ref[...] = jnp.zeros((8, tc), F32)

    car = carry_ref[...]
    p0 = car[0:1, :]
    p1 = car[1:2, :]
    rows = lax.broadcasted_iota(jnp.int32, (tm, tc), 0)
    r1 = pltpu.roll(cu, 1, 0)
    r2 = pltpu.roll(cu, 2, 0)
    prev1 = jnp.where(rows == 0, p1, r1)
    prev2 = jnp.where(rows == 0, p0, jnp.where(rows == 1, p1, r2))
    z_ref[...] = _conv_taps(cu, prev1, prev2, bgate, cw_ref, cb_ref).astype(z_ref.dtype)
    tail = cu[tm - 8:tm, :]
    carry_ref[...] = jnp.concatenate([tail[6:8, :], tail[0:6, :]], axis=0)
    st_ref[0] = tail[6:8, :]


def _mm_conv(h, w, conv_w, conv_b, *, tm, tc, seq, col0):
    t, k = h.shape
    d = conv_w.shape[1]
    nj = d // tc
    return pl.pallas_call(
        functools.partial(_mm_conv_kernel, tm=tm, tc=tc, seq=seq),
        grid=(nj, t // tm),
        in_specs=[pl.BlockSpec((tm, k), lambda j, i: (i, 0)),
                  _wrows(tc, k, col0), _wrows(tc, k, col0 + d), _wrows(tc, k, col0 + 2 * d),
                  pl.BlockSpec((CONV_W, tc), lambda j, i: (0, j)),
                  pl.BlockSpec((1, tc), lambda j, i: (0, j))],
        out_specs=[pl.BlockSpec((tm, tc), lambda j, i: (i, j)),
                   pl.BlockSpec((1, CONV_W - 1, tc), lambda j, i: (i, 0, j))],
        out_shape=[jax.ShapeDtypeStruct((t, d), BF16),
                   jax.ShapeDtypeStruct((t // tm, CONV_W - 1, d), F32)],
        scratch_shapes=[pltpu.VMEM((3 * tc, k), BF16), pltpu.VMEM((8, tc), F32)],
        compiler_params=_cparams(("arbitrary", "arbitrary")),
        name="mm_conv",
    )(h, w, w, w, conv_w, conv_b)


def _sample_conv_kernel(u_ref, b_ref, c_ref, st0_ref, st1_ref, cw_ref, cb_ref, z_ref, cu_ref, *, tc, seq):
    t = u_ref.shape[0]
    bgate = b_ref[...]
    cu = c_ref[...] * u_ref[...]
    s = lax.rem(lax.broadcasted_iota(jnp.int32, (t, tc), 0), jnp.int32(seq))
    r1 = pltpu.roll(cu, 1, 0)
    r2 = pltpu.roll(cu, 2, 0)
    prev1 = jnp.where(s == 0, st1_ref[...], r1)
    prev2 = jnp.where(s == 0, st0_ref[...], jnp.where(s == 1, st1_ref[...], r2))
    z_ref[...] = _conv_taps(cu, prev1, prev2, bgate, cw_ref, cb_ref)
    cu_ref[...] = cu


def _sample_conv(ubc, st0, st1, conv_w, conv_b, *, tc, seq):
    t = ubc.shape[0]
    d = ubc.shape[1] // 3
    nj = d // tc
    return pl.pallas_call(
        functools.partial(_sample_conv_kernel, tc=tc, seq=seq),
        grid=(nj,),
        in_specs=[pl.BlockSpec((t, tc), lambda j: (0, j)),
                  pl.BlockSpec((t, tc), lambda j: (0, nj + j)),
                  pl.BlockSpec((t, tc), lambda j: (0, 2 * nj + j)),
                  pl.BlockSpec((t, tc), lambda j: (0, j)),
                  pl.BlockSpec((t, tc), lambda j: (0, j)),
                  pl.BlockSpec((CONV_W, tc), lambda j: (0, j)),
                  pl.BlockSpec((1, tc), lambda j: (0, j))],
        out_specs=[pl.BlockSpec((t, tc), lambda j: (0, j)),
                   pl.BlockSpec((t, tc), lambda j: (0, j))],
        out_shape=[jax.ShapeDtypeStruct((t, d), F32), jax.ShapeDtypeStruct((t, d), F32)],
        compiler_params=_cparams(("arbitrary",)),
        name="sample_conv",
    )(ubc, ubc, ubc, st0, st1, conv_w, conv_b)


def _mix_kernel(o_ref, z_ref, wa_ref, wc_ref, ga_ref, gc_ref, m_ref):
    a = jnp.dot(o_ref[...].astype(BF16), wa_ref[...].astype(BF16), preferred_element_type=F32)
    c = jnp.dot(z_ref[...].astype(BF16), wc_ref[...].astype(BF16), preferred_element_type=F32)
    m_ref[...] = (ga_ref[...].astype(F32) * a + gc_ref[...].astype(F32) * c).astype(m_ref.dtype)


def _mix(o, z, wa, wc, gates, *, tm, tn):
    t, k = o.shape
    n = wa.shape[1]
    nj = n // tn
    return pl.pallas_call(
        _mix_kernel,
        grid=(t // tm, nj),
        in_specs=[pl.BlockSpec((tm, k), lambda i, j: (i, 0)),
                  pl.BlockSpec((tm, k), lambda i, j: (i, 0)),
                  pl.BlockSpec((k, tn), lambda i, j: (0, j)),
                  pl.BlockSpec((k, tn), lambda i, j: (0, j)),
                  pl.BlockSpec((tm, tn), lambda i, j: (i, j)),
                  pl.BlockSpec((tm, tn), lambda i, j: (i, j + nj))],
        out_specs=pl.BlockSpec((tm, tn), lambda i, j: (i, j)),
        out_shape=jax.ShapeDtypeStruct((t, n), BF16),
        compiler_params=_cparams(("arbitrary", "arbitrary")),
        name="mix",
    )(o, z, wa, wc, gates, gates)


def _resid_norm_kernel(a_ref, w_ref, x_ref, g_ref, ng_ref, sc_ref, sh_ref, x1_ref, h2_ref, wbf_ref):
    @pl.when(pl.program_id(0) == 0)
    def _():
        wbf_ref[...] = w_ref[...].astype(BF16)

    acc = jnp.dot(a_ref[...], wbf_ref[...], preferred_element_type=F32)
    x1 = x_ref[...] + g_ref[0] * acc
    x1_ref[...] = x1
    r = lax.rsqrt(jnp.mean(x1 * x1, axis=-1, keepdims=True) + RMS_EPS)
    y = (x1 * r) * ng_ref[...]
    h2_ref[...] = (y * (1.0 + sc_ref[0]) + sh_ref[0]).astype(h2_ref.dtype)


def _mm_resid_norm(a, w, x, gate, ng, sc, sh, *, tm, seq):
    t, k = a.shape
    n = w.shape[1]
    row = lambda p: _rowspec(p, tm, n, seq, col=False)
    return pl.pallas_call(
        _resid_norm_kernel,
        grid=(t // tm,),
        in_specs=[pl.BlockSpec((tm, k), lambda i: (i, 0)),
                  pl.BlockSpec((k, n), lambda i: (0, 0), pipeline_mode=pl.Buffered(1)),
                  pl.BlockSpec((tm, n), lambda i: (i, 0)),
                  row(gate),
                  pl.BlockSpec((1, n), lambda i: (0, 0)),
                  row(sc), row(sh)],
        out_specs=[pl.BlockSpec((tm, n), lambda i: (i, 0)), pl.BlockSpec((tm, n), lambda i: (i, 0))],
        out_shape=[jax.ShapeDtypeStruct((t, n), F32), jax.ShapeDtypeStruct((t, n), BF16)],
        scratch_shapes=[pltpu.VMEM((k, n), BF16)],
        compiler_params=_cparams(("arbitrary",)),
        name="mm_resid_norm",
    )(a, w, x, gate, ng, sc, sh)


def _mlp_kernel(h_ref, w1_ref, w2_ref, x_ref, g_ref, nf_ref, o_ref, acc_ref):
    k = pl.program_id(1)

    @pl.when(k == 0)
    def _():
        acc_ref[...] = jnp.zeros_like(acc_ref)

    a = jnp.dot(h_ref[...], w1_ref[...], preferred_element_type=F32)
    a = jnp.square(jnp.maximum(a, 0.0)).astype(BF16)
    acc_ref[...] += jnp.dot(a, w2_ref[...], preferred_element_type=F32)

    @pl.when(k == pl.num_programs(1) - 1)
    def _():
        x2 = x_ref[...] + g_ref[0] * acc_ref[...]
        r = lax.rsqrt(jnp.mean(x2 * x2, axis=-1, keepdims=True) + RMS_EPS)
        o_ref[...] = (x2 * r) * nf_ref[...]


def _mlp_final(h2, w1, w2, x1, gate, normf, *, tm, tf, seq):
    t, d = h2.shape
    f = w1.shape[1]
    return pl.pallas_call(
        _mlp_kernel,
        grid=(t // tm, f // tf),
        in_specs=[pl.BlockSpec((tm, d), lambda i, k: (i, 0)),
                  pl.BlockSpec((d, tf), lambda i, k: (0, k)),
                  pl.BlockSpec((tf, d), lambda i, k: (k, 0)),
                  pl.BlockSpec((tm, d), lambda i, k: (i, 0)),
                  _rowspec_k(gate, tm, d, seq),
                  pl.BlockSpec((1, d), lambda i, k: (0, 0))],
        out_specs=pl.BlockSpec((tm, d), lambda i, k: (i, 0)),
        out_shape=jax.ShapeDtypeStruct((t, d), F32),
        scratch_shapes=[pltpu.VMEM((tm, d), F32)],
        compiler_params=_cparams(("arbitrary", "arbitrary")),
        name="mlp",
    )(h2, w1, w2, x1, gate, normf)


def _rowspec_k(p, tm, d, seq):
    if p.shape[1] == 1:
        return pl.BlockSpec((1, 1, d), lambda i, k: ((i * tm) // seq, 0, 0))
    return pl.BlockSpec((1, tm, d), lambda i, k: (0, i, 0))


def _cmp_part_kernel(pt_ref, *refs):
    pages = refs[:PAGES_PER_STEP]
    w_ref = refs[PAGES_PER_STEP]
    o_ref = refs[PAGES_PER_STEP + 1]
    stage = refs[PAGES_PER_STEP + 2]
    nchunk = CHUNK_ROWS // CMP_STRIDE
    cpp = PAGE // CMP_STRIDE
    nslab = 2 * N_KV
    cols = [jnp.concatenate([pg[pl.ds(r, cpp, stride=CMP_STRIDE)].reshape(cpp * nslab, HEAD_DIM)
                             for pg in pages], axis=0).astype(BF16) for r in range(CMP_STRIDE)]
    lhs = jnp.concatenate(cols, axis=1)
    acc = jnp.dot(lhs, w_ref[...], preferred_element_type=F32)
    slab = lax.broadcasted_iota(jnp.int32, (nchunk * nslab, 2 * HEAD_DIM), 0) & (nslab - 1)
    sel = jnp.where(slab < N_KV, acc[:, :2 * HEAD_DIM], acc[:, 2 * HEAD_DIM:])
    stage[0] = sel[:, :HEAD_DIM]
    stage[1] = sel[:, HEAD_DIM:]
    for cb in range(nslab):
        o_ref[cb // N_KV, cb % N_KV] = jnp.concatenate(
            [stage[0, pl.ds(cb, nchunk, stride=nslab), :], stage[1, pl.ds(cb, nchunk, stride=nslab), :]], axis=1)


def _page_spec(n, kind_pair):
    return pl.BlockSpec((None, PAGE, 8, HEAD_DIM),
                        lambda b, p, pt, n=n: (pt[b, p * PAGES_PER_STEP + n], 0, kind_pair, 0))


def _cmp_part(pages4, page_table, w1pairs):
    nb, npg = page_table.shape
    steps = npg // PAGES_PER_STEP
    nchunk = PAGES_PER_STEP * (PAGE // CMP_STRIDE)
    grid_spec = pltpu.PrefetchScalarGridSpec(
        num_scalar_prefetch=1,
        grid=(nb, steps),
        in_specs=[_page_spec(n, 0) for n in range(PAGES_PER_STEP)]
        + [pl.BlockSpec((CMP_STRIDE * HEAD_DIM, 4 * HEAD_DIM), lambda b, p, pt: (0, 0))],
        out_specs=pl.BlockSpec((None, 2, N_KV, nchunk, 2 * HEAD_DIM), lambda b, p, pt: (b, 0, 0, p, 0)),
        scratch_shapes=[pltpu.VMEM((2, nchunk * 2 * N_KV, HEAD_DIM), F32)],
    )
    return pl.pallas_call(
        _cmp_part_kernel,
        grid_spec=grid_spec,
        out_shape=jax.ShapeDtypeStruct((nb, 2, N_KV, steps * nchunk, 2 * HEAD_DIM), F32),
        compiler_params=_cparams(("arbitrary", "arbitrary")),
        name="cmp_part",
    )(page_table, *([pages4] * PAGES_PER_STEP), w1pairs)


def _gelu_tanh(x):
    c = np.sqrt(2.0 / np.pi).astype(np.float32)
    return 0.5 * x * (1.0 + jnp.tanh(c * (x + 0.044715 * (x * x * x))))


def _cmp_finish_kernel(part_ref, pe_ref, w1_ref, w2_ref, o_ref, bias_ref):
    @pl.when(pl.program_id(1) == 0)
    def _():
        bias_ref[...] = jnp.dot(pe_ref[...], w1_ref[...], preferred_element_type=F32,
                                precision=lax.Precision.HIGHEST)

    w2 = w2_ref[...].astype(BF16)
    for hd in range(N_KV):
        part = part_ref[hd]
        n = part.shape[0]
        h = part[:, 0:HEAD_DIM] + pltpu.roll(part[:, HEAD_DIM:], n - 1, 0)
        h = h + bias_ref[0:1, :]
        o_ref[hd] = jnp.dot(_gelu_tanh(h).astype(BF16), w2, preferred_element_type=F32).astype(o_ref.dtype)


def _cmp_finish(part, pe8, w1, w2):
    nb, _, _, nchunk, _ = part.shape
    return pl.pallas_call(
        _cmp_finish_kernel,
        grid=(2, nb),
        in_specs=[pl.BlockSpec((None, None, N_KV, nchunk, 2 * HEAD_DIM), lambda k, b: (b, k, 0, 0, 0)),
                  pl.BlockSpec((None, 8, CMP_LEN * HEAD_DIM), lambda k, b: (k, 0, 0)),
                  pl.BlockSpec((None, CMP_LEN * HEAD_DIM, HEAD_DIM), lambda k, b: (k, 0, 0)),
                  pl.BlockSpec((None, HEAD_DIM, HEAD_DIM), lambda k, b: (k, 0, 0))],
        out_specs=pl.BlockSpec((None, None, N_KV, nchunk, HEAD_DIM), lambda k, b: (b, k, 0, 0, 0)),
        out_shape=jax.ShapeDtypeStruct((nb, 2, N_KV, nchunk, HEAD_DIM), BF16),
        scratch_shapes=[pltpu.VMEM((8, HEAD_DIM), F32)],
        compiler_params=_cparams(("arbitrary", "arbitrary")),
        name="cmp_finish",
    )(part, pe8, w1, w2)


def _select_topk(score, valid, forced, blk, axis):
    sc = jnp.where(valid, jnp.where(forced, jnp.inf, score), -jnp.inf)
    sel = jnp.zeros(score.shape, F32)
    big = jnp.int32(1 << 20)
    for _ in range(SLC_TOP):
        mx = jnp.max(sc, axis=axis, keepdims=True)
        idx = jnp.min(jnp.where(sc == mx, blk, big), axis=axis, keepdims=True)
        hit = blk == idx
        sel = jnp.where(hit & (mx > -jnp.inf), 1.0, sel)
        sc = jnp.where(hit, -jnp.inf, sc)
    return sel


def _overlap_matrix(nc_pad, n_blocks_pad, nc, n_blocks):
    cs = np.arange(nc_pad) * CMP_STRIDE
    sb = np.arange(n_blocks_pad) * SLC_LEN
    ov = np.clip(np.minimum(cs[:, None] + CMP_LEN, sb[None, :] + SLC_LEN)
                 - np.maximum(cs[:, None], sb[None, :]), 0, None)
    m = (ov / CMP_STRIDE).astype(np.float32)
    m[nc:, :] = 0.0
    m[:, n_blocks:] = 0.0
    return m


def _attn_prompt_kernel(q_ref, kcb, vcb, ksb, vsb, kwb, vwb, g_ref, mt_ref, e_ref, o_ref,
                        m_scr, l_scr, acc_scr, bias_scr, gate_scr, *, tq, seq, ck):
    qi = pl.program_id(2)
    nc = seq // CMP_STRIDE - 1
    nsb = seq // SLC_LEN
    rb = min(tq, 128)

    q = q_ref[...]
    qq = jnp.concatenate([q[:, g * HEAD_DIM:(g + 1) * HEAD_DIM] for g in range(GROUP)], axis=0)
    rq = GROUP * tq
    q0 = qi * tq

    gt = jax.nn.sigmoid(g_ref[...])
    for c in range(3 * GROUP):
        gate_scr[c] = jnp.broadcast_to(gt[:, c:c + 1], (tq, HEAD_DIM))

    ncp = kcb.shape[0]
    s = lax.dot_general(qq, kcb[...], _NT, preferred_element_type=F32) * SCALE
    qpos = q0 + (lax.broadcasted_iota(jnp.int32, (rq, ncp), 0) & (tq - 1))
    col = lax.broadcasted_iota(jnp.int32, (rq, ncp), 1)
    vis = (col * CMP_STRIDE + (CMP_LEN - 1) <= qpos) & (col < nc)
    s = jnp.where(vis, s, NEG)
    mx = jnp.max(s, axis=1, keepdims=True)
    e = jnp.where(vis, jnp.exp(s - mx), 0.0)
    den = jnp.sum(e, axis=1, keepdims=True)
    p = e / jnp.where(den > 0.0, den, 1.0)
    o_cmp = jnp.dot(p.astype(BF16), vcb[...], preferred_element_type=F32)
    pg = p[0:tq]
    for g in range(1, GROUP):
        pg = pg + p[g * tq:(g + 1) * tq]

    sct = lax.dot_general(mt_ref[...], pg, _NT, preferred_element_type=F32,
                          precision=lax.Precision.HIGHEST)
    blk = lax.broadcasted_iota(jnp.int32, (nsb, tq), 0)
    cur = lax.div(q0 + lax.broadcasted_iota(jnp.int32, (nsb, tq), 1), jnp.int32(SLC_LEN))
    valid = blk <= cur
    forced = (blk == 0) | (blk > cur - N_LOCAL)
    need_topk = (q0 + tq - 1) // SLC_LEN + 1 > SLC_TOP
    sel_t = lax.cond(need_topk,
                     lambda: _select_topk(sct, valid, forced, blk, 0),
                     lambda: valid.astype(F32)).astype(BF16)

    def sweep(k_ref, v_ref, k_start, n_chunks, bias_fn):
        m_scr[...] = jnp.full((rq, HEAD_DIM), NEG, F32)
        l_scr[...] = jnp.zeros((rq, HEAD_DIM), F32)
        acc_scr[...] = jnp.zeros((rq, HEAD_DIM), F32)
        nl = ck // HEAD_DIM

        def scores(c):
            k0 = pl.multiple_of(k_start + c * ck, 128)
            return lax.dot_general(qq, k_ref[pl.ds(k0, ck), :], _NT, preferred_element_type=F32)

        def chunk_inputs(c):
            k0 = pl.multiple_of(k_start + c * ck, 128)
            return scores(c), bias_fn(c, k0)

        c_exp = np.float32(SCALE * np.log2(np.e))

        def reduce_chunk(c, sb):
            sc_, bias = sb
            k0 = pl.multiple_of(k_start + c * ck, 128)
            vch = v_ref[pl.ds(k0, ck), :]
            alphas, ps = [], []
            nblk = rq // rb
            for r in range(nblk):
                rows = slice(r * rb, (r + 1) * rb)
                b0 = (r * rb) % tq
                sg = [sc_[rows, j * HEAD_DIM:(j + 1) * HEAD_DIM]
                      + bias[b0:b0 + rb, j * HEAD_DIM:(j + 1) * HEAD_DIM] for j in range(nl)]
                mx = functools.reduce(jnp.maximum, sg)
                m_old = m_scr[rows]
                m_new = jnp.maximum(m_old, jnp.max(mx, axis=1, keepdims=True))
                alpha = jnp.exp2((m_old - m_new) * c_exp)
                pj = [jnp.exp2((x - m_new) * c_exp) for x in sg]
                psum = functools.reduce(lambda a, b: a + b, pj)
                l_scr[rows] = alpha * l_scr[rows] + jnp.sum(psum, axis=1, keepdims=True)
                m_scr[rows] = m_new
                alphas.append(alpha)
                ps.append(jnp.concatenate([x.astype(BF16) for x in pj], axis=1))
                if (r + 1) % (nblk // 2) == 0:
                    half = slice((r + 1 - nblk // 2) * rb, (r + 1) * rb)
                    pv = jnp.dot(jnp.concatenate(ps, axis=0), vch, preferred_element_type=F32)
                    acc_scr[half] = jnp.concatenate(alphas, axis=0) * acc_scr[half] + pv
                    alphas, ps = [], []

        def body(c, carry):
            reduce_chunk(c, chunk_inputs(c))
            return carry

        lax.fori_loop(0, n_chunks, body, 0)
        return acc_scr[...] / l_scr[...]

    qp = q0 + lax.broadcasted_iota(jnp.int32, (tq, ck), 0)
    kcol = lax.broadcasted_iota(jnp.int32, (tq, ck), 1)

    eye = (lax.broadcasted_iota(jnp.int32, (nsb, HEAD_DIM), 0)
           == lax.broadcasted_iota(jnp.int32, (nsb, HEAD_DIM), 1)).astype(BF16)
    sel_q = lax.dot_general(sel_t, eye, _TN, preferred_element_type=F32).astype(BF16)

    n_slc = (q0 + tq + ck - 1) // ck

    def put_mask(c):
        selq = jnp.dot(sel_q, e_ref[c], preferred_element_type=F32)
        bias_scr[c] = jnp.where((selq > 0.5) & (c * ck + kcol <= qp), 0.0, NEG)

    def mask_body(i, carry):
        put_mask(2 * i)
        put_mask(jnp.minimum(2 * i + 1, seq // ck - 1))
        return carry

    lax.fori_loop(0, (n_slc + 1) // 2, mask_body, 0)
    o_slc = sweep(ksb, vsb, 0, n_slc, lambda c, k0: bias_scr[c])

    span = -(-(WINDOW + tq) // ck) * ck

    def win_bias(c, k0):
        dlt = qp - (k0 + kcol)
        return jnp.where((dlt >= 0) & (dlt < WINDOW), 0.0, NEG)

    o_win = sweep(kwb, vwb, jnp.maximum(q0 + tq - span, 0), span // ck, win_bias)

    for g in range(GROUP):
        r0, r1 = g * tq, (g + 1) * tq
        og = gate_scr[g] * o_cmp[r0:r1] + gate_scr[GROUP + g] * o_slc[r0:r1]
        og = og + gate_scr[2 * GROUP + g] * o_win[r0:r1]
        o_ref[:, g * HEAD_DIM:(g + 1) * HEAD_DIM] = og.astype(o_ref.dtype)


def _attn_prompt(q, kcv, kv, win, g_re, *, nb, seq, tq=512, ck=512):
    assert tq & (tq - 1) == 0 and seq % ck == 0 and WINDOW % tq == 0
    t = q.shape[0]
    ncp = seq // CMP_STRIDE
    nsb = seq // SLC_LEN
    nq = seq // tq
    mt = jnp.asarray(_overlap_matrix(ncp, nsb, ncp - 1, nsb).T)
    kk = np.arange(seq)
    assert nsb <= HEAD_DIM
    e3 = (kk[None, :] // SLC_LEN == np.arange(HEAD_DIM)[:, None]).astype(np.float32)
    e3 = jnp.asarray(e3.reshape(HEAD_DIM, seq // ck, ck).transpose(1, 0, 2), dtype=BF16)
    slab = lambda cb: pl.BlockSpec((seq, HEAD_DIM), lambda b, h, i, cb=cb: (b, cb + h))
    return pl.pallas_call(
        functools.partial(_attn_prompt_kernel, tq=tq, seq=seq, ck=ck),
        grid=(nb, N_KV, nq),
        in_specs=[pl.BlockSpec((tq, GROUP * HEAD_DIM), lambda b, h, i: (b * nq + i, h)),
                  pl.BlockSpec((None, None, None, ncp, HEAD_DIM), lambda b, h, i: (b, 0, h, 0, 0)),
                  pl.BlockSpec((None, None, None, ncp, HEAD_DIM), lambda b, h, i: (b, 1, h, 0, 0)),
                  slab(0), slab(N_KV), slab(0), slab(N_KV),
                  pl.BlockSpec((None, None, tq, 4 * GROUP), lambda b, h, i: (b, h, i, 0)),
                  pl.BlockSpec((nsb, ncp), lambda b, h, i: (0, 0)),
                  pl.BlockSpec((seq // ck, HEAD_DIM, ck), lambda b, h, i: (0, 0, 0))],
        out_specs=pl.BlockSpec((tq, GROUP * HEAD_DIM), lambda b, h, i: (b * nq + i, h)),
        out_shape=jax.ShapeDtypeStruct((t, N_HEADS * HEAD_DIM), BF16),
        scratch_shapes=[pltpu.VMEM((GROUP * tq, HEAD_DIM), F32)] * 3
        + [pltpu.VMEM((seq // ck, tq, ck), F32), pltpu.VMEM((3 * GROUP, tq, HEAD_DIM), F32)],
        compiler_params=_cparams(("arbitrary", "arbitrary", "arbitrary")),
        name="attn_prompt",
    )(q, kcv, kcv, kv, kv, win, win, g_re, mt, e3)


def _attn_sample_kernel(pt_ref, cache_ref, q_ref, kcv_ref, kvn_ref, wn_ref, st_ref, g_ref, ms_ref, e_ref, o_ref,
                        wso_ref, m_scr, l_scr, acc_scr, sel_scr, ocmp_scr, kv_buf, sem, *, past, ds):
    b = pl.program_id(0)
    p = pl.program_id(1)
    n_steps = pl.num_programs(1)

    def page_copies(bb, pp, slot):
        cps = []
        for n in range(PAGES_PER_STEP):
            page = pt_ref[bb, pp * PAGES_PER_STEP + n]
            for cb in range(2 * N_KV):
                cps.append(pltpu.make_async_copy(cache_ref.at[page, :, 2 * N_KV + cb, :],
                                                 kv_buf.at[slot, cb, pl.ds(n * PAGE, PAGE), :],
                                                 sem.at[slot]))
        return cps

    t = b * n_steps + p
    slot = lax.rem(t, 2)

    @pl.when(t == 0)
    def _():
        for cp in page_copies(b, p, slot):
            cp.start()

    @pl.when(t + 1 < pl.num_programs(0) * n_steps)
    def _():
        wrap = p + 1 == n_steps
        for cp in page_copies(jnp.where(wrap, b + 1, b), jnp.where(wrap, 0, p + 1), 1 - slot):
            cp.start()
    rq = GROUP * ds
    ncp = kcv_ref.shape[2]
    nc = ncp - 1
    n_sel_chunks = sel_scr.shape[1]
    lanes = n_sel_chunks * HEAD_DIM

    def q_rows(h):
        c0 = h * GROUP * HEAD_DIM
        return jnp.concatenate(
            [q_ref[:, c0 + g * HEAD_DIM:c0 + (g + 1) * HEAD_DIM] for g in range(GROUP)], axis=0).astype(BF16)

    def online_update(h, sc_, vals):
        m_i = m_scr[h]
        m_new = jnp.maximum(m_i, jnp.max(sc_, axis=1, keepdims=True))
        alpha = jnp.exp(m_i - m_new)
        pp = jnp.exp(sc_ - m_new)
        l_scr[h] = alpha * l_scr[h] + jnp.sum(pp, axis=1, keepdims=True)
        acc_scr[h] = alpha * acc_scr[h] + jnp.dot(pp.astype(BF16), vals, preferred_element_type=F32)
        m_scr[h] = m_new

    @pl.when(p == 0)
    def _():
        pgs = []
        for h in range(N_KV):
            qq = q_rows(h)
            kc = kcv_ref[0, h]
            vc = kcv_ref[1, h]
            s = lax.dot_general(qq, kc, _NT, preferred_element_type=F32) * SCALE
            qpos = past + (lax.broadcasted_iota(jnp.int32, (rq, ncp), 0) & (ds - 1))
            col = lax.broadcasted_iota(jnp.int32, (rq, ncp), 1)
            vis = (col * CMP_STRIDE + (CMP_LEN - 1) <= qpos) & (col < nc)
            s = jnp.where(vis, s, NEG)
            mx = jnp.max(s, axis=1, keepdims=True)
            e = jnp.where(vis, jnp.exp(s - mx), 0.0)
            den = jnp.sum(e, axis=1, keepdims=True)
            pr = e / jnp.where(den > 0.0, den, 1.0)
            ocmp_scr[h] = jnp.dot(pr.astype(BF16), vc, preferred_element_type=F32)
            pg = pr[0:ds]
            for g in range(1, GROUP):
                pg = pg + pr[g * ds:(g + 1) * ds]
            pgs.append(pg)
            m_scr[h] = jnp.full((rq, 1), NEG, F32)
            l_scr[h] = jnp.zeros((rq, 1), F32)
            acc_scr[h] = jnp.zeros((rq, HEAD_DIM), F32)
        nr = N_KV * ds
        pg_all = jnp.concatenate(pgs, axis=0)
        hi = pg_all.astype(BF16)
        r1 = pg_all - hi.astype(F32)
        mid = r1.astype(BF16)
        lo = (r1 - mid.astype(F32)).astype(BF16)
        sc3 = jnp.dot(jnp.concatenate([hi, mid, lo], axis=0), ms_ref[...], preferred_element_type=F32)
        score = (sc3[0:nr] + sc3[nr:2 * nr]) + sc3[2 * nr:3 * nr]
        lane = lax.broadcasted_iota(jnp.int32, (nr, lanes), 1)
        lane_in = lax.rem(lane, jnp.int32(HEAD_DIM))
        real = lane_in < BLK_PER_CHUNK
        blk = jnp.where(real, lax.div(lane, jnp.int32(HEAD_DIM)) * BLK_PER_CHUNK + lane_in, (1 << 19) + lane)
        qrow = lax.broadcasted_iota(jnp.int32, (nr, lanes), 0) & (ds - 1)
        cur = lax.div(past + qrow, jnp.int32(SLC_LEN))
        valid = real & (blk <= cur)
        forced = (blk == 0) | (blk > cur - N_LOCAL)
        sel = _select_topk(score, valid, forced, blk, 1)
        for h in range(N_KV):
            for c in range(n_sel_chunks):
                sel_scr[h, c] = sel[h * ds:(h + 1) * ds, c * HEAD_DIM:(c + 1) * HEAD_DIM]

    for cp in page_copies(b, p, slot):
        cp.wait()

    k0 = p * CHUNK_ROWS
    for h in range(N_KV):
        qq = q_rows(h)
        kch = kv_buf[slot, h].astype(BF16)
        vch = kv_buf[slot, N_KV + h].astype(BF16)
        sc_ = lax.dot_general(qq, kch, _NT, preferred_element_type=F32) * SCALE
        selq = jnp.dot(sel_scr[h, p].astype(BF16), e_ref[...], preferred_element_type=F32)
        kpos = k0 + lax.broadcasted_iota(jnp.int32, (ds, CHUNK_ROWS), 1)
        qp = past + lax.broadcasted_iota(jnp.int32, (ds, CHUNK_ROWS), 0)
        okf = jnp.where((selq > 0.5) & (kpos <= qp), 1.0, 0.0)
        ok = jnp.concatenate([okf] * GROUP, axis=0) > 0.5
        online_update(h, jnp.where(ok, sc_, NEG), vch)

    @pl.when(p == n_steps - 1)
    def _():
        gt = jax.nn.sigmoid(g_ref[...])
        zpad = jnp.zeros((HEAD_DIM - ds, HEAD_DIM), F32)
        for h in range(N_KV):
            qq = q_rows(h)
            c_k = 2 * KV_W + h * HEAD_DIM
            c_v = 3 * KV_W + h * HEAD_DIM
            kn = jnp.concatenate([kvn_ref[:, c_k:c_k + HEAD_DIM], zpad], axis=0).astype(BF16)
            vn = jnp.concatenate([kvn_ref[:, c_v:c_v + HEAD_DIM], zpad], axis=0).astype(BF16)
            sn = lax.dot_general(qq, kn, _NT, preferred_element_type=F32) * SCALE
            last_sel = sel_scr[h, n_sel_chunks - 1]
            nb_last = (past // SLC_LEN) % BLK_PER_CHUNK
            seln = jnp.sum(jnp.where(lax.broadcasted_iota(jnp.int32, (ds, HEAD_DIM), 1) == nb_last,
                                     last_sel, 0.0), axis=1, keepdims=True)
            seln = jnp.concatenate([seln] * GROUP, axis=0)
            srow = lax.broadcasted_iota(jnp.int32, (rq, HEAD_DIM), 0) & (ds - 1)
            kcol = lax.broadcasted_iota(jnp.int32, (rq, HEAD_DIM), 1)
            okn = (seln > 0.5) & (kcol <= srow) & (kcol < ds)
            online_update(h, jnp.where(okn, sn, NEG), vn)
            o_slc = acc_scr[h] / l_scr[h]
            wb = st_ref.shape[0]
            band = wb + HEAD_DIM
            st_flat = st_ref.reshape(wb * 8, HEAD_DIM)
            kw = jnp.concatenate([st_flat[pl.ds(h, wb, stride=8), :],
                                  wn_ref[:, h * HEAD_DIM:(h + 1) * HEAD_DIM], zpad], axis=0).astype(BF16)
            vw = jnp.concatenate([st_flat[pl.ds(N_KV + h, wb, stride=8), :],
                                  wn_ref[:, KV_W + h * HEAD_DIM:KV_W + (h + 1) * HEAD_DIM], zpad],
                                 axis=0).astype(BF16)
            sw = lax.dot_general(qq, kw, _NT, preferred_element_type=F32) * SCALE
            srw = lax.broadcasted_iota(jnp.int32, (rq, band), 0) & (ds - 1)
            idx = lax.broadcasted_iota(jnp.int32, (rq, band), 1)
            dlt = (wb + srw) - idx
            sw = jnp.where((dlt >= 0) & (dlt < WINDOW) & (idx < wb + ds), sw, NEG)
            ew = jnp.exp(sw - jnp.max(sw, axis=1, keepdims=True))
            o_win = jnp.dot(ew.astype(BF16), vw, preferred_element_type=F32) / jnp.sum(ew, axis=1, keepdims=True)
            o_cmp = ocmp_scr[h]
            gh = gt[h]
            for g in range(GROUP):
                r0, r1 = g * ds, (g + 1) * ds
                og = gh[:, g:g + 1] * o_cmp[r0:r1] + gh[:, GROUP + g:GROUP + g + 1] * o_slc[r0:r1]
                og = og + gh[:, 2 * GROUP + g:2 * GROUP + g + 1] * o_win[r0:r1]
                c0 = (h * GROUP + g) * HEAD_DIM
                o_ref[:, c0:c0 + HEAD_DIM] = og
        wb = st_ref.shape[0]
        wso_ref[0:wb - ds] = st_ref[ds:wb]
        wso_flat = wso_ref.reshape(wb * 8, HEAD_DIM)
        for cb in range(2 * N_KV):
            wso_flat[pl.ds((wb - ds) * 8 + cb, ds, stride=8), :] = wn_ref[:, cb * HEAD_DIM:(cb + 1) * HEAD_DIM]


def _attn_sample(q, kcv, kv_new, win_new, win_state, g_re, cache4, page_table, *, ds):
    nb, npg = page_table.shape
    past = npg * PAGE
    assert ds & (ds - 1) == 0 and ds <= SLC_LEN and past % SLC_LEN == 0 and npg % PAGES_PER_STEP == 0
    steps = npg // PAGES_PER_STEP
    ncp = kcv.shape[3]
    wb = win_state.shape[1]
    n_blocks = past // SLC_LEN + 1
    n_sel_chunks = -(-n_blocks // BLK_PER_CHUNK)
    m = _overlap_matrix(ncp, n_sel_chunks * BLK_PER_CHUNK, ncp - 1, n_blocks)
    ms = np.zeros((ncp, n_sel_chunks, HEAD_DIM), np.float32)
    ms[:, :, :BLK_PER_CHUNK] = m.reshape(ncp, n_sel_chunks, BLK_PER_CHUNK)
    ms = jnp.asarray(ms.reshape(ncp, n_sel_chunks * HEAD_DIM), dtype=BF16)
    ee = np.zeros((HEAD_DIM, CHUNK_ROWS), np.float32)
    ee[:BLK_PER_CHUNK] = np.arange(CHUNK_ROWS)[None, :] // SLC_LEN == np.arange(BLK_PER_CHUNK)[:, None]
    ee = jnp.asarray(ee, dtype=BF16)
    rq = GROUP * ds

    grid_spec = pltpu.PrefetchScalarGridSpec(
        num_scalar_prefetch=1,
        grid=(nb, steps),
        in_specs=[
            pl.BlockSpec(memory_space=pl.ANY),
            pl.BlockSpec((ds, N_HEADS * HEAD_DIM), lambda b, p, pt: (b, 0)),
            pl.BlockSpec((None, 2, N_KV, ncp, HEAD_DIM), lambda b, p, pt: (b, 0, 0, 0, 0)),
            pl.BlockSpec((ds, 4 * KV_W), lambda b, p, pt: (b, 0)),
            pl.BlockSpec((ds, 2 * KV_W), lambda b, p, pt: (b, 0)),
            pl.BlockSpec((None, wb, 8, HEAD_DIM), lambda b, p, pt: (b, 0, 0, 0)),
            pl.BlockSpec((None, N_KV, ds, 4 * GROUP), lambda b, p, pt: (b, 0, 0, 0)),
            pl.BlockSpec(ms.shape, lambda b, p, pt: (0, 0)),
            pl.BlockSpec(ee.shape, lambda b, p, pt: (0, 0))],
        out_specs=[pl.BlockSpec((ds, N_HEADS * HEAD_DIM), lambda b, p, pt: (b, 0)),
                   pl.BlockSpec((None, wb, 8, HEAD_DIM), lambda b, p, pt: (b, 0, 0, 0))],
        scratch_shapes=[pltpu.VMEM((N_KV, rq, 1), F32), pltpu.VMEM((N_KV, rq, 1), F32),
                        pltpu.VMEM((N_KV, rq, HEAD_DIM), F32),
                        pltpu.VMEM((N_KV, n_sel_chunks, ds, HEAD_DIM), F32),
                        pltpu.VMEM((N_KV, rq, HEAD_DIM), F32),
                        pltpu.VMEM((2, 2 * N_KV, CHUNK_ROWS, HEAD_DIM), F32),
                        pltpu.SemaphoreType.DMA((2,))],
    )
    return pl.pallas_call(
        functools.partial(_attn_sample_kernel, past=past, ds=ds),
        grid_spec=grid_spec,
        out_shape=[jax.ShapeDtypeStruct((nb * ds, N_HEADS * HEAD_DIM), F32),
                   jax.ShapeDtypeStruct((nb, wb, 8, HEAD_DIM), F32)],
        compiler_params=_cparams(("arbitrary", "arbitrary")),
        name="attn_sample",
    )(page_table, cache4, q, kcv, kv_new, win_new, win_state, g_re, ms, ee)


_Q0, _KV0, _WIN0, _G0 = 0, 2048, 4096, 5120
_REST0 = _G0 + 3 * N_HEADS


def _split_w_in(w_in):
    wt = w_in.T
    return wt, wt[:_G0 + HEAD_DIM].astype(BF16)


def _gate_layout(g_logits, nb, seq):
    g = g_logits[:, :3 * N_HEADS].reshape(nb, seq, 3, N_KV, GROUP)
    g = g.transpose(0, 3, 1, 2, 4).reshape(nb, N_KV, seq, 3 * GROUP)
    return jnp.pad(g, ((0, 0), (0, 0), (0, 0), (0, GROUP)))


def _cmp_weights(w_k1, w_k2, pe_k, w_v1, w_v2, pe_v):
    def by_row(w1):
        w = w1.reshape(2, CMP_STRIDE, HEAD_DIM, HEAD_DIM).transpose(1, 2, 0, 3)
        return w.reshape(CMP_STRIDE * HEAD_DIM, 2 * HEAD_DIM)
    w1pairs = jnp.concatenate([by_row(w_k1), by_row(w_v1)], axis=1).astype(BF16)
    pe8 = jnp.stack([jnp.broadcast_to(pe_k.reshape(1, -1), (8, CMP_LEN * HEAD_DIM)),
                     jnp.broadcast_to(pe_v.reshape(1, -1), (8, CMP_LEN * HEAD_DIM))])
    return w1pairs, pe8, jnp.stack([w_k1, w_v1]), jnp.stack([w_k2, w_v2])


def kernel(x_prompt, x_sample, c_prompt, c_sample, cache_nsa_kv, page_table, state_win_kv, state_conv, w_ada, b_ada, norm1_g, norm2_g, w_in, w_cmp_k1, w_cmp_k2, pe_cmp_k, w_cmp_v1, w_cmp_v2, pe_cmp_v, conv_w, conv_b, w_attn_proj, w_conv_proj, w_out, w_mlp1, w_mlp2, normf_g):
    d = D_MODEL
    nbp, seq, _ = x_prompt.shape
    nbs, ds, _ = x_sample.shape
    tp = nbp * seq
    ts = nbs * ds
    tc = 256
    depth = w_in.shape[0]
    assert depth == 1

    xp = x_prompt.reshape(tp, d)
    xs = x_sample.reshape(ts, d)
    normf = normf_g.reshape(1, d)

    l = 0
    c_all = jnp.concatenate([c_prompt, c_sample, jnp.zeros((16 - nbp - nbs, d), F32)], axis=0)
    ada = _ada(c_all, w_ada[l], b_ada[l].reshape(1, -1)).reshape(16, 6, d)
    ada_p = [ada[:nbp, k][:, None, :] for k in range(6)]
    ada_s = [jnp.repeat(ada[nbp:nbp + nbs, k], ds, axis=0)[None] for k in range(6)]

    w_t, w_head = _split_w_in(w_in[l])
    gates0 = _REST0 + 3 * d
    wa = w_attn_proj[l]
    wc = w_conv_proj[l]
    wo = w_out[l]
    w1 = w_mlp1[l].astype(BF16)
    w2 = w_mlp2[l].astype(BF16)
    g1n = norm1_g[l].reshape(1, d)
    g2n = norm2_g[l].reshape(1, d)
    cw = conv_w[l]
    cb = conv_b[l].reshape(1, d)
    w1pairs, pe8, w1s, w2s = _cmp_weights(w_cmp_k1[l], w_cmp_k2[l], pe_cmp_k[l],
                                          w_cmp_v1[l], w_cmp_v2[l], pe_cmp_v[l])

    tm = 1024
    hp = _norm_mod(xp, g1n, ada_p[1], ada_p[0], tm=512, seq=seq)
    q_p = _mm(hp, w_head, tm=tm, tn=1024, out_dtype=BF16, name="mm_q", col0=_Q0, n=N_HEADS * HEAD_DIM)
    wbp = min(WINDOW, seq)
    kv6_p, kvb_p = _mm_kv(hp, w_head, tm=tm, col0=_KV0)
    winb_p, win6_p, g_p = _mm_win(hp, w_head, tm=wbp, seq=seq, col0=_WIN0, gcol0=_G0)
    gates_p = _mm_wstat(hp, w_t, tm=tm, tn=1024, out_dtype=BF16, act="sigmoid", name="mm_gates",
                        col0=gates0, n=2 * d)
    z_p, conv_tiles = _mm_conv(hp, w_t, cw, cb, tm=tm, tc=tc, seq=seq, col0=_REST0)

    pt_p = jnp.arange(tp // PAGE, dtype=jnp.int32).reshape(nbp, seq // PAGE)
    part_p = _cmp_part(kv6_p.reshape(tp // PAGE, PAGE, 16, HEAD_DIM), pt_p, w1pairs)
    kcv_p = _cmp_finish(part_p, pe8, w1s, w2s)
    o_p = _attn_prompt(q_p, kcv_p, kvb_p, winb_p, _gate_layout(g_p, nbp, seq), nb=nbp, seq=seq)

    mixed_p = _mix(o_p, z_p, wa, wc, gates_p, tm=tm, tn=512)
    x1_p, h2_p = _mm_resid_norm(mixed_p, wo, xp, ada_p[2], g2n, ada_p[4], ada_p[3], tm=512, seq=seq)
    y_p = _mlp_final(h2_p, w1, w2, x1_p, ada_p[5], normf, tm=512, tf=1024, seq=seq)

    hs = _norm_mod(xs, g1n, ada_s[1], ada_s[0], tm=ts, seq=ds)
    q_s = _mm(hs, w_head, tm=ts, tn=512, out_dtype=F32, name="mm_q_s", col0=_Q0, n=N_HEADS * HEAD_DIM)
    kv_s = _mm(hs, w_head, tm=ts, tn=512, out_dtype=F32, name="mm_kv_s", col0=_KV0, n=4 * KV_W)
    win_s = _mm(hs, w_head, tm=ts, tn=512, out_dtype=F32, name="mm_win_s", col0=_WIN0, n=2 * KV_W)
    g_s = _mm(hs, w_head, tm=ts, tn=HEAD_DIM, out_dtype=F32, name="mm_g_s", col0=_G0, n=HEAD_DIM)
    gates_s = _mm_wstat(hs, w_t, tm=ts, tn=512, out_dtype=F32, act="sigmoid", name="mm_gates_s",
                        col0=gates0, n=2 * d)
    ubc_s = _mm_wstat(hs, w_t, tm=ts, tn=512, out_dtype=F32, name="mm_ubc_s", col0=_REST0, n=3 * d)
    st = state_conv[l]
    z_s, cu_s = _sample_conv(ubc_s, jnp.repeat(st[:, 0], ds, axis=0), jnp.repeat(st[:, 1], ds, axis=0),
                             cw, cb, tc=tc, seq=ds)

    cache4 = cache_nsa_kv[l].reshape(cache_nsa_kv.shape[1], PAGE, 16, HEAD_DIM)
    part_s = _cmp_part(cache4, page_table, w1pairs)
    kcv_s = _cmp_finish(part_s, pe8, w1s, w2s)
    wbs = state_win_kv.shape[2]
    wst = state_win_kv[l].reshape(nbs, wbs, 8, HEAD_DIM)
    o_s, wst_next = _attn_sample(q_s, kcv_s, kv_s, win_s, wst, _gate_layout(g_s, nbs, ds), cache4, page_table,
                                 ds=ds)

    mixed_s = _mix(o_s, z_s, wa, wc, gates_s, tm=ts, tn=512)
    x1_s, h2_s = _mm_resid_norm(mixed_s, wo, xs, ada_s[2], g2n, ada_s[4], ada_s[3], tm=ts, seq=ds)
    y_s = _mlp_final(h2_s, w1, w2, x1_s, ada_s[5], normf, tm=ts, tf=512, seq=ds)

    kv_prompt = kv6_p.reshape(1, nbp, seq, 4, N_KV, HEAD_DIM)
    kv_sample = kv_s.reshape(1, nbs, ds, 4, N_KV, HEAD_DIM)
    win_prompt = win6_p.reshape(1, nbp, wbp, 2, N_KV, HEAD_DIM)
    win_sample = wst_next.reshape(1, nbs, wbs, 2, N_KV, HEAD_DIM)
    tiles_per_seq = seq // tm
    conv_prompt = conv_tiles[tiles_per_seq - 1::tiles_per_seq][None]
    conv_sample = cu_s.reshape(nbs, ds, d)[None, :, ds - (CONV_W - 1):]
    return (y_p.reshape(nbp, seq, d), y_s.reshape(nbs, ds, d), kv_prompt, kv_sample,
            win_prompt, win_sample, conv_prompt, conv_sample)
```

```python
import functools

import numpy as np
import jax
import jax.numpy as jnp
from jax import lax
from jax.experimental import pallas as pl
from jax.experimental.pallas import tpu as pltpu

F32 = jnp.float32
BF16 = jnp.bfloat16

D_MODEL = 2048
HEAD_DIM = 128
N_HEADS = 16
N_KV = 4
GROUP = 4
KV_W = N_KV * HEAD_DIM
CMP_LEN = 32
CMP_STRIDE = 16
SLC_LEN = 64
SLC_TOP = 16
N_LOCAL = 2
WINDOW = 512
CONV_W = 3
RMS_EPS = 1e-6
NEG = -1e30
SCALE = HEAD_DIM ** -0.5
PAGE = 128
PAGES_PER_STEP = 16
CHUNK_ROWS = PAGES_PER_STEP * PAGE
BLK_PER_CHUNK = CHUNK_ROWS // SLC_LEN
VMEM_LIMIT = 56 * 1024 * 1024

_NT = (((1,), (1,)), ((), ()))
_TN = (((0,), (0,)), ((), ()))


def _cparams(sem):
    return pltpu.CompilerParams(dimension_semantics=sem, vmem_limit_bytes=VMEM_LIMIT)


def _ada_kernel(c_ref, w_ref, b_ref, o_ref):
    c = c_ref[...]
    a = (c * jax.nn.sigmoid(c)).astype(BF16)
    o_ref[...] = jnp.dot(a, w_ref[...].astype(BF16), preferred_element_type=F32) + b_ref[...]


def _ada(c, w, b, tn=1024):
    m, k = c.shape
    n = w.shape[1]
    return pl.pallas_call(
        _ada_kernel,
        grid=(n // tn,),
        in_specs=[pl.BlockSpec((m, k), lambda j: (0, 0)),
                  pl.BlockSpec((k, tn), lambda j: (0, j)),
                  pl.BlockSpec((1, tn), lambda j: (0, j))],
        out_specs=pl.BlockSpec((m, tn), lambda j: (0, j)),
        out_shape=jax.ShapeDtypeStruct((m, n), F32),
        compiler_params=_cparams(("arbitrary",)),
        name="ada",
    )(c, w, b)


def _rowspec(p, tm, tn, seq, col=True):
    gr = p.shape[1]
    if gr == 1:
        if col:
            return pl.BlockSpec((1, 1, tn), lambda i, j: ((i * tm) // seq, 0, j))
        return pl.BlockSpec((1, 1, tn), lambda i: ((i * tm) // seq, 0, 0))
    if col:
        return pl.BlockSpec((1, tm, tn), lambda i, j: (0, i, j))
    return pl.BlockSpec((1, tm, tn), lambda i: (0, i, 0))


def _norm_kernel(x_ref, g_ref, sc_ref, sh_ref, o_ref):
    x = x_ref[...]
    r = lax.rsqrt(jnp.mean(x * x, axis=-1, keepdims=True) + RMS_EPS)
    y = (x * r) * g_ref[...]
    o_ref[...] = (y * (1.0 + sc_ref[0]) + sh_ref[0]).astype(o_ref.dtype)


def _norm_mod(x, g, sc, sh, *, tm, seq):
    t, d = x.shape
    return pl.pallas_call(
        _norm_kernel,
        grid=(t // tm,),
        in_specs=[pl.BlockSpec((tm, d), lambda i: (i, 0)),
                  pl.BlockSpec((1, d), lambda i: (0, 0)),
                  _rowspec(sc, tm, d, seq, col=False),
                  _rowspec(sh, tm, d, seq, col=False)],
        out_specs=pl.BlockSpec((tm, d), lambda i: (i, 0)),
        out_shape=jax.ShapeDtypeStruct((t, d), BF16),
        compiler_params=_cparams(("arbitrary",)),
        name="norm_mod",
    )(x, g, sc, sh)


def _xwt(x_ref, wt_ref):
    return lax.dot_general(x_ref[...].astype(BF16), wt_ref[...], _NT, preferred_element_type=F32)


def _mm_kernel(x_ref, w_ref, o_ref, *, act):
    acc = _xwt(x_ref, w_ref)
    if act == "sigmoid":
        acc = jax.nn.sigmoid(acc)
    o_ref[...] = acc.astype(o_ref.dtype)


def _mm_wstat_kernel(x_ref, w_ref, o_ref, wbf_ref, *, act):
    @pl.when(pl.program_id(1) == 0)
    def _():
        wbf_ref[...] = w_ref[...].astype(BF16)

    acc = _xwt(x_ref, wbf_ref)
    if act == "sigmoid":
        acc = jax.nn.sigmoid(acc)
    o_ref[...] = acc.astype(o_ref.dtype)


def _wrows(tn, k, row0):
    return pl.BlockSpec((pl.Element(tn), pl.Element(k)), lambda j, i: (pl.multiple_of(row0 + j * tn, 8), 0))


def _mm_wstat(x, w, *, tm, tn, out_dtype, act=None, name="mm", col0, n):
    t, k = x.shape
    assert col0 % 8 == 0 and n % tn == 0
    return pl.pallas_call(
        functools.partial(_mm_wstat_kernel, act=act),
        grid=(n // tn, t // tm),
        in_specs=[pl.BlockSpec((tm, k), lambda j, i: (i, 0)),
                  _wrows(tn, k, col0)],
        out_specs=pl.BlockSpec((tm, tn), lambda j, i: (i, j)),
        out_shape=jax.ShapeDtypeStruct((t, n), out_dtype),
        scratch_shapes=[pltpu.VMEM((tn, k), BF16)],
        compiler_params=_cparams(("arbitrary", "arbitrary")),
        name=name,
    )(x, w)


def _mm(x, w, *, tm, tn, out_dtype, act=None, name="mm", col0, n):
    t, k = x.shape
    assert col0 % tn == 0 and n % tn == 0
    cb0 = col0 // tn
    return pl.pallas_call(
        functools.partial(_mm_kernel, act=act),
        grid=(t // tm, n // tn),
        in_specs=[pl.BlockSpec((tm, k), lambda i, j: (i, 0)),
                  pl.BlockSpec((tn, k), lambda i, j: (cb0 + j, 0))],
        out_specs=pl.BlockSpec((tm, tn), lambda i, j: (i, j)),
        out_shape=jax.ShapeDtypeStruct((t, n), out_dtype),
        compiler_params=_cparams(("arbitrary", "arbitrary")),
        name=name,
    )(x, w)


def _store_head_major(o_ref, acc, tm):
    flat = o_ref.reshape(tm * 8, HEAD_DIM)
    for cb in range(8):
        flat[pl.ds(cb, tm, stride=8), :] = acc[:, cb * HEAD_DIM:(cb + 1) * HEAD_DIM]


def _mm_kv_kernel(x_ref, w_ref, o6_ref, ob_ref, *, tm):
    acc = _xwt(x_ref, w_ref)
    _store_head_major(o6_ref, acc, tm)

    @pl.when(pl.program_id(1) == 1)
    def _():
        ob_ref[...] = acc.astype(BF16)


def _mm_kv(h, w, *, tm, col0):
    t, k = h.shape
    tn = 2 * KV_W
    cb0 = col0 // tn
    return pl.pallas_call(
        functools.partial(_mm_kv_kernel, tm=tm),
        grid=(t // tm, 2),
        in_specs=[pl.BlockSpec((tm, k), lambda i, j: (i, 0)),
                  pl.BlockSpec((tn, k), lambda i, j: (cb0 + j, 0))],
        out_specs=[pl.BlockSpec((tm, 8, HEAD_DIM), lambda i, j: (i, j, 0)),
                   pl.BlockSpec((tm, tn), lambda i, j: (i, 0))],
        out_shape=[jax.ShapeDtypeStruct((t, 16, HEAD_DIM), F32),
                   jax.ShapeDtypeStruct((t, tn), BF16)],
        compiler_params=_cparams(("arbitrary", "arbitrary")),
        name="mm_kv",
    )(h, w)


def _mm_win_kernel(x_ref, w_ref, wg_ref, ob_ref, o6_ref, og_ref, *, tm, seq):
    acc = _xwt(x_ref, w_ref)
    ob_ref[...] = acc.astype(BF16)
    og_ref[...] = _xwt(x_ref, wg_ref)

    @pl.when(((pl.program_id(0) + 1) * tm) % seq == 0)
    def _():
        _store_head_major(o6_ref, acc, tm)


def _mm_win(h, w, *, tm, seq, col0, gcol0):
    t, k = h.shape
    tn = 2 * KV_W
    cb0 = col0 // tn
    gb0 = gcol0 // HEAD_DIM
    return pl.pallas_call(
        functools.partial(_mm_win_kernel, tm=tm, seq=seq),
        grid=(t // tm,),
        in_specs=[pl.BlockSpec((tm, k), lambda i: (i, 0)),
                  pl.BlockSpec((tn, k), lambda i: (cb0, 0)),
                  pl.BlockSpec((HEAD_DIM, k), lambda i: (gb0, 0))],
        out_specs=[pl.BlockSpec((tm, tn), lambda i: (i, 0)),
                   pl.BlockSpec((tm, 8, HEAD_DIM), lambda i: ((i * tm) // seq, 0, 0)),
                   pl.BlockSpec((tm, HEAD_DIM), lambda i: (i, 0))],
        out_shape=[jax.ShapeDtypeStruct((t, tn), BF16),
                   jax.ShapeDtypeStruct((t // seq * tm, 8, HEAD_DIM), F32),
                   jax.ShapeDtypeStruct((t, HEAD_DIM), F32)],
        compiler_params=_cparams(("arbitrary",)),
        name="mm_win",
    )(h, w, w)


def _conv_taps(cu, prev1, prev2, bgate, cw_ref, cb_ref):
    y = cb_ref[...] + prev2 * cw_ref[0:1, :]
    y = y + prev1 * cw_ref[1:2, :]
    y = y + cu * cw_ref[2:3, :]
    return bgate * y


def _mm_conv_kernel(x_ref, wu_ref, wb_ref, wc_ref, cw_ref, cb_ref, z_ref, st_ref, w3_ref, carry_ref, *, tm, tc, seq):
    i = pl.program_id(1)

    @pl.when(i == 0)
    def _():
        w3_ref[0:tc] = wu_ref[...].astype(BF16)
        w3_ref[tc:2 * tc] = wb_ref[...].astype(BF16)
        w3_ref[2 * tc:3 * tc] = wc_ref[...].astype(BF16)

    acc = _xwt(x_ref, w3_ref)
    u = acc[:, 0:tc]
    bgate = acc[:, tc:2 * tc]
    cu = acc[:, 2 * tc:3 * tc] * u

    @pl.when((i * tm) % seq == 0)
    def _():
        carry_ref[...] = jnp.zeros((8, tc), F32)

    car = carry_ref[...]
    p0 = car[0:1, :]
    p1 = car[1:2, :]
    rows = lax.broadcasted_iota(jnp.int32, (tm, tc), 0)
    r1 = pltpu.roll(cu, 1, 0)
    r2 = pltpu.roll(cu, 2, 0)
    prev1 = jnp.where(rows == 0, p1, r1)
    prev2 = jnp.where(rows == 0, p0, jnp.where(rows == 1, p1, r2))
    z_ref[...] = _conv_taps(cu, prev1, prev2, bgate, cw_ref, cb_ref).astype(z_ref.dtype)
    tail = cu[tm - 8:tm, :]
    carry_ref[...] = jnp.concatenate([tail[6:8, :], tail[0:6, :]], axis=0)
    st_ref[0] = tail[6:8, :]


def _mm_conv(h, w, conv_w, conv_b, *, tm, tc, seq, col0):
    t, k = h.shape
    d = conv_w.shape[1]
    nj = d // tc
    return pl.pallas_call(
        functools.partial(_mm_conv_kernel, tm=tm, tc=tc, seq=seq),
        grid=(nj, t // tm),
        in_specs=[pl.BlockSpec((tm, k), lambda j, i: (i, 0)),
                  _wrows(tc, k, col0), _wrows(tc, k, col0 + d), _wrows(tc, k, col0 + 2 * d),
                  pl.BlockSpec((CONV_W, tc), lambda j, i: (0, j)),
                  pl.BlockSpec((1, tc), lambda j, i: (0, j))],
        out_specs=[pl.BlockSpec((tm, tc), lambda j, i: (i, j)),
                   pl.BlockSpec((1, CONV_W - 1, tc), lambda j, i: (i, 0, j))],
        out_shape=[jax.ShapeDtypeStruct((t, d), BF16),
                   jax.ShapeDtypeStruct((t // tm, CONV_W - 1, d), F32)],
        scratch_shapes=[pltpu.VMEM((3 * tc, k), BF16), pltpu.VMEM((8, tc), F32)],
        compiler_params=_cparams(("arbitrary", "arbitrary")),
        name="mm_conv",
    )(h, w, w, w, conv_w, conv_b)


def _sample_conv_kernel(u_ref, b_ref, c_ref, st0_ref, st1_ref, cw_ref, cb_ref, z_ref, cu_ref, *, tc, seq):
    t = u_ref.shape[0]
    bgate = b_ref[...]
    cu = c_ref[...] * u_ref[...]
    s = lax.rem(lax.broadcasted_iota(jnp.int32, (t, tc), 0), jnp.int32(seq))
    r1 = pltpu.roll(cu, 1, 0)
    r2 = pltpu.roll(cu, 2, 0)
    prev1 = jnp.where(s == 0, st1_ref[...], r1)
    prev2 = jnp.where(s == 0, st0_ref[...], jnp.where(s == 1, st1_ref[...], r2))
    z_ref[...] = _conv_taps(cu, prev1, prev2, bgate, cw_ref, cb_ref)
    cu_ref[...] = cu


def _sample_conv(ubc, st0, st1, conv_w, conv_b, *, tc, seq):
    t = ubc.shape[0]
    d = ubc.shape[1] // 3
    nj = d // tc
    return pl.pallas_call(
        functools.partial(_sample_conv_kernel, tc=tc, seq=seq),
        grid=(nj,),
        in_specs=[pl.BlockSpec((t, tc), lambda j: (0, j)),
                  pl.BlockSpec((t, tc), lambda j: (0, nj + j)),
                  pl.BlockSpec((t, tc), lambda j: (0, 2 * nj + j)),
                  pl.BlockSpec((t, tc), lambda j: (0, j)),
                  pl.BlockSpec((t, tc), lambda j: (0, j)),
                  pl.BlockSpec((CONV_W, tc), lambda j: (0, j)),
                  pl.BlockSpec((1, tc), lambda j: (0, j))],
        out_specs=[pl.BlockSpec((t, tc), lambda j: (0, j)),
                   pl.BlockSpec((t, tc), lambda j: (0, j))],
        out_shape=[jax.ShapeDtypeStruct((t, d), F32), jax.ShapeDtypeStruct((t, d), F32)],
        compiler_params=_cparams(("arbitrary",)),
        name="sample_conv",
    )(ubc, ubc, ubc, st0, st1, conv_w, conv_b)


def _mix_kernel(o_ref, z_ref, wa_ref, wc_ref, ga_ref, gc_ref, m_ref):
    a = jnp.dot(o_ref[...].astype(BF16), wa_ref[...].astype(BF16), preferred_element_type=F32)
    c = jnp.dot(z_ref[...].astype(BF16), wc_ref[...].astype(BF16), preferred_element_type=F32)
    m_ref[...] = (ga_ref[...].astype(F32) * a + gc_ref[...].astype(F32) * c).astype(m_ref.dtype)


def _mix(o, z, wa, wc, gates, *, tm, tn):
    t, k = o.shape
    n = wa.shape[1]
    nj = n // tn
    return pl.pallas_call(
        _mix_kernel,
        grid=(t // tm, nj),
        in_specs=[pl.BlockSpec((tm, k), lambda i, j: (i, 0)),
                  pl.BlockSpec((tm, k), lambda i, j: (i, 0)),
                  pl.BlockSpec((k, tn), lambda i, j: (0, j)),
                  pl.BlockSpec((k, tn), lambda i, j: (0, j)),
                  pl.BlockSpec((tm, tn), lambda i, j: (i, j)),
                  pl.BlockSpec((tm, tn), lambda i, j: (i, j + nj))],
        out_specs=pl.BlockSpec((tm, tn), lambda i, j: (i, j)),
        out_shape=jax.ShapeDtypeStruct((t, n), BF16),
        compiler_params=_cparams(("arbitrary", "arbitrary")),
        name="mix",
    )(o, z, wa, wc, gates, gates)


def _resid_norm_kernel(a_ref, w_ref, x_ref, g_ref, ng_ref, sc_ref, sh_ref, x1_ref, h2_ref, wbf_ref):
    @pl.when(pl.program_id(0) == 0)
    def _():
        wbf_ref[...] = w_ref[...].astype(BF16)

    acc = jnp.dot(a_ref[...], wbf_ref[...], preferred_element_type=F32)
    x1 = x_ref[...] + g_ref[0] * acc
    x1_ref[...] = x1
    r = lax.rsqrt(jnp.mean(x1 * x1, axis=-1, keepdims=True) + RMS_EPS)
    y = (x1 * r) * ng_ref[...]
    h2_ref[...] = (y * (1.0 + sc_ref[0]) + sh_ref[0]).astype(h2_ref.dtype)


def _mm_resid_norm(a, w, x, gate, ng, sc, sh, *, tm, seq):
    t, k = a.shape
    n = w.shape[1]
    row = lambda p: _rowspec(p, tm, n, seq, col=False)
    return pl.pallas_call(
        _resid_norm_kernel,
        grid=(t // tm,),
        in_specs=[pl.BlockSpec((tm, k), lambda i: (i, 0)),
                  pl.BlockSpec((k, n), lambda i: (0, 0), pipeline_mode=pl.Buffered(1)),
                  pl.BlockSpec((tm, n), lambda i: (i, 0)),
                  row(gate),
                  pl.BlockSpec((1, n), lambda i: (0, 0)),
                  row(sc), row(sh)],
        out_specs=[pl.BlockSpec((tm, n), lambda i: (i, 0)), pl.BlockSpec((tm, n), lambda i: (i, 0))],
        out_shape=[jax.ShapeDtypeStruct((t, n), F32), jax.ShapeDtypeStruct((t, n), BF16)],
        scratch_shapes=[pltpu.VMEM((k, n), BF16)],
        compiler_params=_cparams(("arbitrary",)),
        name="mm_resid_norm",
    )(a, w, x, gate, ng, sc, sh)


def _mlp_kernel(h_ref, w1_ref, w2_ref, x_ref, g_ref, nf_ref, o_ref, acc_ref, a_ref):
    k = pl.program_id(1)
    nk = pl.num_programs(1) - 1

    def up():
        a = jnp.dot(h_ref[...], w1_ref[...], preferred_element_type=F32)
        a_ref[k % 2] = jnp.square(jnp.maximum(a, 0.0)).astype(BF16)

    def down():
        return jnp.dot(a_ref[(k + 1) % 2], w2_ref[...], preferred_element_type=F32)

    @pl.when(k == 0)
    def _():
        up()

    @pl.when(k == 1)
    def _():
        up()
        acc_ref[...] = down()

    @pl.when((k > 1) & (k < nk))
    def _():
        up()
        acc_ref[...] += down()

    @pl.when(k == nk)
    def _():
        x2 = x_ref[...] + g_ref[0] * (acc_ref[...] + down())
        r = lax.rsqrt(jnp.mean(x2 * x2, axis=-1, keepdims=True) + RMS_EPS)
        o_ref[...] = (x2 * r) * nf_ref[...]


def _mlp_final(h2, w1, w2, x1, gate, normf, *, tm, tf, seq):
    t, d = h2.shape
    f = w1.shape[1]
    nk = f // tf
    assert nk >= 2
    return pl.pallas_call(
        _mlp_kernel,
        grid=(t // tm, nk + 1),
        in_specs=[pl.BlockSpec((tm, d), lambda i, k: (i, 0)),
                  pl.BlockSpec((d, tf), lambda i, k: (0, jnp.minimum(k, nk - 1))),
                  pl.BlockSpec((tf, d), lambda i, k: (jnp.maximum(k - 1, 0), 0)),
                  pl.BlockSpec((tm, d), lambda i, k: (i, 0)),
                  _rowspec_k(gate, tm, d, seq),
                  pl.BlockSpec((1, d), lambda i, k: (0, 0))],
        out_specs=pl.BlockSpec((tm, d), lambda i, k: (i, 0)),
        out_shape=jax.ShapeDtypeStruct((t, d), F32),
        scratch_shapes=[pltpu.VMEM((tm, d), F32), pltpu.VMEM((2, tm, tf), BF16)],
        compiler_params=_cparams(("arbitrary", "arbitrary")),
        name="mlp",
    )(h2, w1, w2, x1, gate, normf)


def _rowspec_k(p, tm, d, seq):
    if p.shape[1] == 1:
        return pl.BlockSpec((1, 1, d), lambda i, k: ((i * tm) // seq, 0, 0))
    return pl.BlockSpec((1, tm, d), lambda i, k: (0, i, 0))


def _cmp_part_kernel(pt_ref, *refs):
    pages = refs[:PAGES_PER_STEP]
    w_ref = refs[PAGES_PER_STEP]
    o_ref = refs[PAGES_PER_STEP + 1]
    stage = refs[PAGES_PER_STEP + 2]
    nchunk = CHUNK_ROWS // CMP_STRIDE
    cpp = PAGE // CMP_STRIDE
    nslab = 2 * N_KV
    cols = [jnp.concatenate([pg[pl.ds(r, cpp, stride=CMP_STRIDE)].reshape(cpp * nslab, HEAD_DIM)
                             for pg in pages], axis=0).astype(BF16) for r in range(CMP_STRIDE)]
    lhs = jnp.concatenate(cols, axis=1)
    acc = jnp.dot(lhs, w_ref[...], preferred_element_type=F32)
    slab = lax.broadcasted_iota(jnp.int32, (nchunk * nslab, 2 * HEAD_DIM), 0) & (nslab - 1)
    sel = jnp.where(slab < N_KV, acc[:, :2 * HEAD_DIM], acc[:, 2 * HEAD_DIM:])
    stage[0] = sel[:, :HEAD_DIM]
    stage[1] = sel[:, HEAD_DIM:]
    for cb in range(nslab):
        o_ref[cb // N_KV, cb % N_KV] = jnp.concatenate(
            [stage[0, pl.ds(cb, nchunk, stride=nslab), :], stage[1, pl.ds(cb, nchunk, stride=nslab), :]], axis=1)


def _page_spec(n, kind_pair):
    return pl.BlockSpec((None, PAGE, 8, HEAD_DIM),
                        lambda b, p, pt, n=n: (pt[b, p * PAGES_PER_STEP + n], 0, kind_pair, 0))


def _cmp_part(pages4, page_table, w1pairs):
    nb, npg = page_table.shape
    steps = npg // PAGES_PER_STEP
    nchunk = PAGES_PER_STEP * (PAGE // CMP_STRIDE)
    grid_spec = pltpu.PrefetchScalarGridSpec(
        num_scalar_prefetch=1,
        grid=(nb, steps),
        in_specs=[_page_spec(n, 0) for n in range(PAGES_PER_STEP)]
        + [pl.BlockSpec((CMP_STRIDE * HEAD_DIM, 4 * HEAD_DIM), lambda b, p, pt: (0, 0))],
        out_specs=pl.BlockSpec((None, 2, N_KV, nchunk, 2 * HEAD_DIM), lambda b, p, pt: (b, 0, 0, p, 0)),
        scratch_shapes=[pltpu.VMEM((2, nchunk * 2 * N_KV, HEAD_DIM), F32)],
    )
    return pl.pallas_call(
        _cmp_part_kernel,
        grid_spec=grid_spec,
        out_shape=jax.ShapeDtypeStruct((nb, 2, N_KV, steps * nchunk, 2 * HEAD_DIM), F32),
        compiler_params=_cparams(("arbitrary", "arbitrary")),
        name="cmp_part",
    )(page_table, *([pages4] * PAGES_PER_STEP), w1pairs)


def _gelu_tanh(x):
    c = np.sqrt(2.0 / np.pi).astype(np.float32)
    return 0.5 * x * (1.0 + jnp.tanh(c * (x + 0.044715 * (x * x * x))))


def _cmp_finish_kernel(part_ref, pe_ref, w1_ref, w2_ref, o_ref, bias_ref):
    @pl.when(pl.program_id(1) == 0)
    def _():
        bias_ref[...] = jnp.dot(pe_ref[...], w1_ref[...], preferred_element_type=F32,
                                precision=lax.Precision.HIGHEST)

    w2 = w2_ref[...].astype(BF16)
    for hd in range(N_KV):
        part = part_ref[hd]
        n = part.shape[0]
        h = part[:, 0:HEAD_DIM] + pltpu.roll(part[:, HEAD_DIM:], n - 1, 0)
        h = h + bias_ref[0:1, :]
        o_ref[hd] = jnp.dot(_gelu_tanh(h).astype(BF16), w2, preferred_element_type=F32).astype(o_ref.dtype)


def _cmp_finish(part, pe8, w1, w2):
    nb, _, _, nchunk, _ = part.shape
    return pl.pallas_call(
        _cmp_finish_kernel,
        grid=(2, nb),
        in_specs=[pl.BlockSpec((None, None, N_KV, nchunk, 2 * HEAD_DIM), lambda k, b: (b, k, 0, 0, 0)),
                  pl.BlockSpec((None, 8, CMP_LEN * HEAD_DIM), lambda k, b: (k, 0, 0)),
                  pl.BlockSpec((None, CMP_LEN * HEAD_DIM, HEAD_DIM), lambda k, b: (k, 0, 0)),
                  pl.BlockSpec((None, HEAD_DIM, HEAD_DIM), lambda k, b: (k, 0, 0))],
        out_specs=pl.BlockSpec((None, None, N_KV, nchunk, HEAD_DIM), lambda k, b: (b, k, 0, 0, 0)),
        out_shape=jax.ShapeDtypeStruct((nb, 2, N_KV, nchunk, HEAD_DIM), BF16),
        scratch_shapes=[pltpu.VMEM((8, HEAD_DIM), F32)],
        compiler_params=_cparams(("arbitrary", "arbitrary")),
        name="cmp_finish",
    )(part, pe8, w1, w2)


def _select_topk(score, valid, forced, blk, axis):
    sc = jnp.where(valid, jnp.where(forced, jnp.inf, score), -jnp.inf)
    sel = jnp.zeros(score.shape, F32)
    big = jnp.int32(1 << 20)
    for _ in range(SLC_TOP):
        mx = jnp.max(sc, axis=axis, keepdims=True)
        idx = jnp.min(jnp.where(sc == mx, blk, big), axis=axis, keepdims=True)
        hit = blk == idx
        sel = jnp.where(hit & (mx > -jnp.inf), 1.0, sel)
        sc = jnp.where(hit, -jnp.inf, sc)
    return sel


def _overlap_matrix(nc_pad, n_blocks_pad, nc, n_blocks):
    cs = np.arange(nc_pad) * CMP_STRIDE
    sb = np.arange(n_blocks_pad) * SLC_LEN
    ov = np.clip(np.minimum(cs[:, None] + CMP_LEN, sb[None, :] + SLC_LEN)
                 - np.maximum(cs[:, None], sb[None, :]), 0, None)
    m = (ov / CMP_STRIDE).astype(np.float32)
    m[nc:, :] = 0.0
    m[:, n_blocks:] = 0.0
    return m


def _attn_prompt_kernel(q_ref, kcb, vcb, ksb, vsb, kwb, vwb, g_ref, mt_ref, e_ref, o_ref,
                        m_scr, l_scr, acc_scr, bias_scr, gate_scr, *, tq, seq, ck):
    qi = pl.program_id(2)
    nc = seq // CMP_STRIDE - 1
    nsb = seq // SLC_LEN
    rb = min(tq, 128)

    q = q_ref[...]
    qq = jnp.concatenate([q[:, g * HEAD_DIM:(g + 1) * HEAD_DIM] for g in range(GROUP)], axis=0)
    rq = GROUP * tq
    q0 = qi * tq

    gt = jax.nn.sigmoid(g_ref[...])
    for c in range(3 * GROUP):
        gate_scr[c] = jnp.broadcast_to(gt[:, c:c + 1], (tq, HEAD_DIM))

    ncp = kcb.shape[0]
    s = lax.dot_general(qq, kcb[...], _NT, preferred_element_type=F32) * SCALE
    qpos = q0 + (lax.broadcasted_iota(jnp.int32, (rq, ncp), 0) & (tq - 1))
    col = lax.broadcasted_iota(jnp.int32, (rq, ncp), 1)
    vis = (col * CMP_STRIDE + (CMP_LEN - 1) <= qpos) & (col < nc)
    s = jnp.where(vis, s, NEG)
    mx = jnp.max(s, axis=1, keepdims=True)
    e = jnp.where(vis, jnp.exp(s - mx), 0.0)
    den = jnp.sum(e, axis=1, keepdims=True)
    p = e / jnp.where(den > 0.0, den, 1.0)
    o_cmp = jnp.dot(p.astype(BF16), vcb[...], preferred_element_type=F32)
    pg = p[0:tq]
    for g in range(1, GROUP):
        pg = pg + p[g * tq:(g + 1) * tq]

    sct = lax.dot_general(mt_ref[...], pg, _NT, preferred_element_type=F32,
                          precision=lax.Precision.HIGHEST)
    blk = lax.broadcasted_iota(jnp.int32, (nsb, tq), 0)
    cur = lax.div(q0 + lax.broadcasted_iota(jnp.int32, (nsb, tq), 1), jnp.int32(SLC_LEN))
    valid = blk <= cur
    forced = (blk == 0) | (blk > cur - N_LOCAL)
    need_topk = (q0 + tq - 1) // SLC_LEN + 1 > SLC_TOP
    sel_t = lax.cond(need_topk,
                     lambda: _select_topk(sct, valid, forced, blk, 0),
                     lambda: valid.astype(F32)).astype(BF16)

    def sweep(k_ref, v_ref, k_start, n_chunks, bias_fn):
        m_scr[...] = jnp.full((rq, HEAD_DIM), NEG, F32)
        l_scr[...] = jnp.zeros((rq, HEAD_DIM), F32)
        acc_scr[...] = jnp.zeros((rq, HEAD_DIM), F32)
        nl = ck // HEAD_DIM

        def scores(c):
            k0 = pl.multiple_of(k_start + c * ck, 128)
            return lax.dot_general(qq, k_ref[pl.ds(k0, ck), :], _NT, preferred_element_type=F32)

        def chunk_inputs(c):
            k0 = pl.multiple_of(k_start + c * ck, 128)
            return scores(c), bias_fn(c, k0)

        c_exp = np.float32(SCALE * np.log2(np.e))

        def reduce_chunk(c, sb):
            sc_, bias = sb
            k0 = pl.multiple_of(k_start + c * ck, 128)
            vch = v_ref[pl.ds(k0, ck), :]
            alphas, ps = [], []
            nblk = rq // rb
            for r in range(nblk):
                rows = slice(r * rb, (r + 1) * rb)
                b0 = (r * rb) % tq
                sg = [sc_[rows, j * HEAD_DIM:(j + 1) * HEAD_DIM]
                      + bias[b0:b0 + rb, j * HEAD_DIM:(j + 1) * HEAD_DIM] for j in range(nl)]
                mx = functools.reduce(jnp.maximum, sg)
                m_old = m_scr[rows]
                m_new = jnp.maximum(m_old, jnp.max(mx, axis=1, keepdims=True))
                alpha = jnp.exp2((m_old - m_new) * c_exp)
                pj = [jnp.exp2((x - m_new) * c_exp) for x in sg]
                psum = functools.reduce(lambda a, b: a + b, pj)
                l_scr[rows] = alpha * l_scr[rows] + jnp.sum(psum, axis=1, keepdims=True)
                m_scr[rows] = m_new
                alphas.append(alpha)
                ps.append(jnp.concatenate([x.astype(BF16) for x in pj], axis=1))
                if (r + 1) % (nblk // 2) == 0:
                    half = slice((r + 1 - nblk // 2) * rb, (r + 1) * rb)
                    pv = jnp.dot(jnp.concatenate(ps, axis=0), vch, preferred_element_type=F32)
                    acc_scr[half] = jnp.concatenate(alphas, axis=0) * acc_scr[half] + pv
                    alphas, ps = [], []

        def body(c, carry):
            reduce_chunk(c, chunk_inputs(c))
            return carry

        lax.fori_loop(0, n_chunks, body, 0)
        return acc_scr[...] / l_scr[...]

    qp = q0 + lax.broadcasted_iota(jnp.int32, (tq, ck), 0)
    kcol = lax.broadcasted_iota(jnp.int32, (tq, ck), 1)

    eye = (lax.broadcasted_iota(jnp.int32, (nsb, HEAD_DIM), 0)
           == lax.broadcasted_iota(jnp.int32, (nsb, HEAD_DIM), 1)).astype(BF16)
    sel_q = lax.dot_general(sel_t, eye, _TN, preferred_element_type=F32).astype(BF16)

    n_slc = (q0 + tq + ck - 1) // ck

    def put_mask(c):
        selq = jnp.dot(sel_q, e_ref[c], preferred_element_type=F32)
        bias_scr[c] = jnp.where((selq > 0.5) & (c * ck + kcol <= qp), 0.0, NEG)

    def mask_body(i, carry):
        put_mask(2 * i)
        put_mask(jnp.minimum(2 * i + 1, seq // ck - 1))
        return carry

    lax.fori_loop(0, (n_slc + 1) // 2, mask_body, 0)
    o_slc = sweep(ksb, vsb, 0, n_slc, lambda c, k0: bias_scr[c])

    span = -(-(WINDOW + tq) // ck) * ck

    def win_bias(c, k0):
        dlt = qp - (k0 + kcol)
        return jnp.where((dlt >= 0) & (dlt < WINDOW), 0.0, NEG)

    o_win = sweep(kwb, vwb, jnp.maximum(q0 + tq - span, 0), span // ck, win_bias)

    for g in range(GROUP):
        r0, r1 = g * tq, (g + 1) * tq
        og = gate_scr[g] * o_cmp[r0:r1] + gate_scr[GROUP + g] * o_slc[r0:r1]
        og = og + gate_scr[2 * GROUP + g] * o_win[r0:r1]
        o_ref[:, g * HEAD_DIM:(g + 1) * HEAD_DIM] = og.astype(o_ref.dtype)


def _attn_prompt(q, kcv, kv, win, g_re, *, nb, seq, tq=512, ck=512):
    assert tq & (tq - 1) == 0 and seq % ck == 0 and WINDOW % tq == 0
    t = q.shape[0]
    ncp = seq // CMP_STRIDE
    nsb = seq // SLC_LEN
    nq = seq // tq
    mt = jnp.asarray(_overlap_matrix(ncp, nsb, ncp - 1, nsb).T)
    kk = np.arange(seq)
    assert nsb <= HEAD_DIM
    e3 = (kk[None, :] // SLC_LEN == np.arange(HEAD_DIM)[:, None]).astype(np.float32)
    e3 = jnp.asarray(e3.reshape(HEAD_DIM, seq // ck, ck).transpose(1, 0, 2), dtype=BF16)
    slab = lambda cb: pl.BlockSpec((seq, HEAD_DIM), lambda b, h, i, cb=cb: (b, cb + h))
    return pl.pallas_call(
        functools.partial(_attn_prompt_kernel, tq=tq, seq=seq, ck=ck),
        grid=(nb, N_KV, nq),
        in_specs=[pl.BlockSpec((tq, GROUP * HEAD_DIM), lambda b, h, i: (b * nq + i, h)),
                  pl.BlockSpec((None, None, None, ncp, HEAD_DIM), lambda b, h, i: (b, 0, h, 0, 0)),
                  pl.BlockSpec((None, None, None, ncp, HEAD_DIM), lambda b, h, i: (b, 1, h, 0, 0)),
                  slab(0), slab(N_KV), slab(0), slab(N_KV),
                  pl.BlockSpec((None, None, tq, 4 * GROUP), lambda b, h, i: (b, h, i, 0)),
                  pl.BlockSpec((nsb, ncp), lambda b, h, i: (0, 0)),
                  pl.BlockSpec((seq // ck, HEAD_DIM, ck), lambda b, h, i: (0, 0, 0))],
        out_specs=pl.BlockSpec((tq, GROUP * HEAD_DIM), lambda b, h, i: (b * nq + i, h)),
        out_shape=jax.ShapeDtypeStruct((t, N_HEADS * HEAD_DIM), BF16),
        scratch_shapes=[pltpu.VMEM((GROUP * tq, HEAD_DIM), F32)] * 3
        + [pltpu.VMEM((seq // ck, tq, ck), F32), pltpu.VMEM((3 * GROUP, tq, HEAD_DIM), F32)],
        compiler_params=_cparams(("arbitrary", "arbitrary", "arbitrary")),
        name="attn_prompt",
    )(q, kcv, kcv, kv, kv, win, win, g_re, mt, e3)


def _attn_sample_kernel(pt_ref, cache_ref, q_ref, kcv_ref, kvn_ref, wn_ref, st_ref, g_ref, ms_ref, e_ref, o_ref,
                        wso_ref, m_scr, l_scr, acc_scr, sel_scr, ocmp_scr, kv_buf, sem, *, past, ds):
    b = pl.program_id(0)
    p = pl.program_id(1)
    n_steps = pl.num_programs(1)

    def page_copies(bb, pp, slot):
        cps = []
        for n in range(PAGES_PER_STEP):
            page = pt_ref[bb, pp * PAGES_PER_STEP + n]
            for cb in range(2 * N_KV):
                cps.append(pltpu.make_async_copy(cache_ref.at[page, :, 2 * N_KV + cb, :],
                                                 kv_buf.at[slot, cb, pl.ds(n * PAGE, PAGE), :],
                                                 sem.at[slot]))
        return cps

    t = b * n_steps + p
    slot = lax.rem(t, 2)

    @pl.when(t == 0)
    def _():
        for cp in page_copies(b, p, slot):
            cp.start()

    @pl.when(t + 1 < pl.num_programs(0) * n_steps)
    def _():
        wrap = p + 1 == n_steps
        for cp in page_copies(jnp.where(wrap, b + 1, b), jnp.where(wrap, 0, p + 1), 1 - slot):
            cp.start()
    rq = GROUP * ds
    ncp = kcv_ref.shape[2]
    nc = ncp - 1
    n_sel_chunks = sel_scr.shape[1]
    lanes = n_sel_chunks * HEAD_DIM

    def q_rows(h):
        c0 = h * GROUP * HEAD_DIM
        return jnp.concatenate(
            [q_ref[:, c0 + g * HEAD_DIM:c0 + (g + 1) * HEAD_DIM] for g in range(GROUP)], axis=0).astype(BF16)

    def online_update(h, sc_, vals):
        m_i = m_scr[h]
        m_new = jnp.maximum(m_i, jnp.max(sc_, axis=1, keepdims=True))
        alpha = jnp.exp(m_i - m_new)
        pp = jnp.exp(sc_ - m_new)
        l_scr[h] = alpha * l_scr[h] + jnp.sum(pp, axis=1, keepdims=True)
        acc_scr[h] = alpha * acc_scr[h] + jnp.dot(pp.astype(BF16), vals, preferred_element_type=F32)
        m_scr[h] = m_new

    @pl.when(p == 0)
    def _():
        pgs = []
        for h in range(N_KV):
            qq = q_rows(h)
            kc = kcv_ref[0, h]
            vc = kcv_ref[1, h]
            s = lax.dot_general(qq, kc, _NT, preferred_element_type=F32) * SCALE
            qpos = past + (lax.broadcasted_iota(jnp.int32, (rq, ncp), 0) & (ds - 1))
            col = lax.broadcasted_iota(jnp.int32, (rq, ncp), 1)
            vis = (col * CMP_STRIDE + (CMP_LEN - 1) <= qpos) & (col < nc)
            s = jnp.where(vis, s, NEG)
            mx = jnp.max(s, axis=1, keepdims=True)
            e = jnp.where(vis, jnp.exp(s - mx), 0.0)
            den = jnp.sum(e, axis=1, keepdims=True)
            pr = e / jnp.where(den > 0.0, den, 1.0)
            ocmp_scr[h] = jnp.dot(pr.astype(BF16), vc, preferred_element_type=F32)
            pg = pr[0:ds]
            for g in range(1, GROUP):
                pg = pg + pr[g * ds:(g + 1) * ds]
            pgs.append(pg)
            m_scr[h] = jnp.full((rq, 1), NEG, F32)
            l_scr[h] = jnp.zeros((rq, 1), F32)
            acc_scr[h] = jnp.zeros((rq, HEAD_DIM), F32)
        nr = N_KV * ds
        pg_all = jnp.concatenate(pgs, axis=0)
        hi = pg_all.astype(BF16)
        r1 = pg_all - hi.astype(F32)
        mid = r1.astype(BF16)
        lo = (r1 - mid.astype(F32)).astype(BF16)
        sc3 = jnp.dot(jnp.concatenate([hi, mid, lo], axis=0), ms_ref[...], preferred_element_type=F32)
        score = (sc3[0:nr] + sc3[nr:2 * nr]) + sc3[2 * nr:3 * nr]
        lane = lax.broadcasted_iota(jnp.int32, (nr, lanes), 1)
        lane_in = lax.rem(lane, jnp.int32(HEAD_DIM))
        real = lane_in < BLK_PER_CHUNK
        blk = jnp.where(real, lax.div(lane, jnp.int32(HEAD_DIM)) * BLK_PER_CHUNK + lane_in, (1 << 19) + lane)
        qrow = lax.broadcasted_iota(jnp.int32, (nr, lanes), 0) & (ds - 1)
        cur = lax.div(past + qrow, jnp.int32(SLC_LEN))
        valid = real & (blk <= cur)
        forced = (blk == 0) | (blk > cur - N_LOCAL)
        sel = _select_topk(score, valid, forced, blk, 1)
        for h in range(N_KV):
            for c in range(n_sel_chunks):
                sel_scr[h, c] = sel[h * ds:(h + 1) * ds, c * HEAD_DIM:(c + 1) * HEAD_DIM]

    for cp in page_copies(b, p, slot):
        cp.wait()

    k0 = p * CHUNK_ROWS
    for h in range(N_KV):
        qq = q_rows(h)
        kch = kv_buf[slot, h].astype(BF16)
        vch = kv_buf[slot, N_KV + h].astype(BF16)
        sc_ = lax.dot_general(qq, kch, _NT, preferred_element_type=F32) * SCALE
        selq = jnp.dot(sel_scr[h, p].astype(BF16), e_ref[...], preferred_element_type=F32)
        kpos = k0 + lax.broadcasted_iota(jnp.int32, (ds, CHUNK_ROWS), 1)
        qp = past + lax.broadcasted_iota(jnp.int32, (ds, CHUNK_ROWS), 0)
        okf = jnp.where((selq > 0.5) & (kpos <= qp), 1.0, 0.0)
        ok = jnp.concatenate([okf] * GROUP, axis=0) > 0.5
        online_update(h, jnp.where(ok, sc_, NEG), vch)

    @pl.when(p == n_steps - 1)
    def _():
        gt = jax.nn.sigmoid(g_ref[...])
        zpad = jnp.zeros((HEAD_DIM - ds, HEAD_DIM), F32)
        for h in range(N_KV):
            qq = q_rows(h)
            c_k = 2 * KV_W + h * HEAD_DIM
            c_v = 3 * KV_W + h * HEAD_DIM
            kn = jnp.concatenate([kvn_ref[:, c_k:c_k + HEAD_DIM], zpad], axis=0).astype(BF16)
            vn = jnp.concatenate([kvn_ref[:, c_v:c_v + HEAD_DIM], zpad], axis=0).astype(BF16)
            sn = lax.dot_general(qq, kn, _NT, preferred_element_type=F32) * SCALE
            last_sel = sel_scr[h, n_sel_chunks - 1]
            nb_last = (past // SLC_LEN) % BLK_PER_CHUNK
            seln = jnp.sum(jnp.where(lax.broadcasted_iota(jnp.int32, (ds, HEAD_DIM), 1) == nb_last,
                                     last_sel, 0.0), axis=1, keepdims=True)
            seln = jnp.concatenate([seln] * GROUP, axis=0)
            srow = lax.broadcasted_iota(jnp.int32, (rq, HEAD_DIM), 0) & (ds - 1)
            kcol = lax.broadcasted_iota(jnp.int32, (rq, HEAD_DIM), 1)
            okn = (seln > 0.5) & (kcol <= srow) & (kcol < ds)
            online_update(h, jnp.where(okn, sn, NEG), vn)
            o_slc = acc_scr[h] / l_scr[h]
            wb = st_ref.shape[0]
            band = wb + HEAD_DIM
            st_flat = st_ref.reshape(wb * 8, HEAD_DIM)
            kw = jnp.concatenate([st_flat[pl.ds(h, wb, stride=8), :],
                                  wn_ref[:, h * HEAD_DIM:(h + 1) * HEAD_DIM], zpad], axis=0).astype(BF16)
            vw = jnp.concatenate([st_flat[pl.ds(N_KV + h, wb, stride=8), :],
                                  wn_ref[:, KV_W + h * HEAD_DIM:KV_W + (h + 1) * HEAD_DIM], zpad],
                                 axis=0).astype(BF16)
            sw = lax.dot_general(qq, kw, _NT, preferred_element_type=F32) * SCALE
            srw = lax.broadcasted_iota(jnp.int32, (rq, band), 0) & (ds - 1)
            idx = lax.broadcasted_iota(jnp.int32, (rq, band), 1)
            dlt = (wb + srw) - idx
            sw = jnp.where((dlt >= 0) & (dlt < WINDOW) & (idx < wb + ds), sw, NEG)
            ew = jnp.exp(sw - jnp.max(sw, axis=1, keepdims=True))
            o_win = jnp.dot(ew.astype(BF16), vw, preferred_element_type=F32) / jnp.sum(ew, axis=1, keepdims=True)
            o_cmp = ocmp_scr[h]
            gh = gt[h]
            for g in range(GROUP):
                r0, r1 = g * ds, (g + 1) * ds
                og = gh[:, g:g + 1] * o_cmp[r0:r1] + gh[:, GROUP + g:GROUP + g + 1] * o_slc[r0:r1]
                og = og + gh[:, 2 * GROUP + g:2 * GROUP + g + 1] * o_win[r0:r1]
                c0 = (h * GROUP + g) * HEAD_DIM
                o_ref[:, c0:c0 + HEAD_DIM] = og
        wb = st_ref.shape[0]
        wso_ref[0:wb - ds] = st_ref[ds:wb]
        wso_flat = wso_ref.reshape(wb * 8, HEAD_DIM)
        for cb in range(2 * N_KV):
            wso_flat[pl.ds((wb - ds) * 8 + cb, ds, stride=8), :] = wn_ref[:, cb * HEAD_DIM:(cb + 1) * HEAD_DIM]


def _attn_sample(q, kcv, kv_new, win_new, win_state, g_re, cache4, page_table, *, ds):
    nb, npg = page_table.shape
    past = npg * PAGE
    assert ds & (ds - 1) == 0 and ds <= SLC_LEN and past % SLC_LEN == 0 and npg % PAGES_PER_STEP == 0
    steps = npg // PAGES_PER_STEP
    ncp = kcv.shape[3]
    wb = win_state.shape[1]
    n_blocks = past // SLC_LEN + 1
    n_sel_chunks = -(-n_blocks // BLK_PER_CHUNK)
    m = _overlap_matrix(ncp, n_sel_chunks * BLK_PER_CHUNK, ncp - 1, n_blocks)
    ms = np.zeros((ncp, n_sel_chunks, HEAD_DIM), np.float32)
    ms[:, :, :BLK_PER_CHUNK] = m.reshape(ncp, n_sel_chunks, BLK_PER_CHUNK)
    ms = jnp.asarray(ms.reshape(ncp, n_sel_chunks * HEAD_DIM), dtype=BF16)
    ee = np.zeros((HEAD_DIM, CHUNK_ROWS), np.float32)
    ee[:BLK_PER_CHUNK] = np.arange(CHUNK_ROWS)[None, :] // SLC_LEN == np.arange(BLK_PER_CHUNK)[:, None]
    ee = jnp.asarray(ee, dtype=BF16)
    rq = GROUP * ds

    grid_spec = pltpu.PrefetchScalarGridSpec(
        num_scalar_prefetch=1,
        grid=(nb, steps),
        in_specs=[
            pl.BlockSpec(memory_space=pl.ANY),
            pl.BlockSpec((ds, N_HEADS * HEAD_DIM), lambda b, p, pt: (b, 0)),
            pl.BlockSpec((None, 2, N_KV, ncp, HEAD_DIM), lambda b, p, pt: (b, 0, 0, 0, 0)),
            pl.BlockSpec((ds, 4 * KV_W), lambda b, p, pt: (b, 0)),
            pl.BlockSpec((ds, 2 * KV_W), lambda b, p, pt: (b, 0)),
            pl.BlockSpec((None, wb, 8, HEAD_DIM), lambda b, p, pt: (b, 0, 0, 0)),
            pl.BlockSpec((None, N_KV, ds, 4 * GROUP), lambda b, p, pt: (b, 0, 0, 0)),
            pl.BlockSpec(ms.shape, lambda b, p, pt: (0, 0)),
            pl.BlockSpec(ee.shape, lambda b, p, pt: (0, 0))],
        out_specs=[pl.BlockSpec((ds, N_HEADS * HEAD_DIM), lambda b, p, pt: (b, 0)),
                   pl.BlockSpec((None, wb, 8, HEAD_DIM), lambda b, p, pt: (b, 0, 0, 0))],
        scratch_shapes=[pltpu.VMEM((N_KV, rq, 1), F32), pltpu.VMEM((N_KV, rq, 1), F32),
                        pltpu.VMEM((N_KV, rq, HEAD_DIM), F32),
                        pltpu.VMEM((N_KV, n_sel_chunks, ds, HEAD_DIM), F32),
                        pltpu.VMEM((N_KV, rq, HEAD_DIM), F32),
                        pltpu.VMEM((2, 2 * N_KV, CHUNK_ROWS, HEAD_DIM), F32),
                        pltpu.SemaphoreType.DMA((2,))],
    )
    return pl.pallas_call(
        functools.partial(_attn_sample_kernel, past=past, ds=ds),
        grid_spec=grid_spec,
        out_shape=[jax.ShapeDtypeStruct((nb * ds, N_HEADS * HEAD_DIM), F32),
                   jax.ShapeDtypeStruct((nb, wb, 8, HEAD_DIM), F32)],
        compiler_params=_cparams(("arbitrary", "arbitrary")),
        name="attn_sample",
    )(page_table, cache4, q, kcv, kv_new, win_new, win_state, g_re, ms, ee)


_Q0, _KV0, _WIN0, _G0 = 0, 2048, 4096, 5120
_REST0 = _G0 + 3 * N_HEADS


def _split_w_in(w_in):
    wt = w_in.T
    return wt, wt[:_G0 + HEAD_DIM].astype(BF16)


def _gate_layout(g_logits, nb, seq):
    g = g_logits[:, :3 * N_HEADS].reshape(nb, seq, 3, N_KV, GROUP)
    g = g.transpose(0, 3, 1, 2, 4).reshape(nb, N_KV, seq, 3 * GROUP)
    return jnp.pad(g, ((0, 0), (0, 0), (0, 0), (0, GROUP)))


def _cmp_weights(w_k1, w_k2, pe_k, w_v1, w_v2, pe_v):
    def by_row(w1):
        w = w1.reshape(2, CMP_STRIDE, HEAD_DIM, HEAD_DIM).transpose(1, 2, 0, 3)
        return w.reshape(CMP_STRIDE * HEAD_DIM, 2 * HEAD_DIM)
    w1pairs = jnp.concatenate([by_row(w_k1), by_row(w_v1)], axis=1).astype(BF16)
    pe8 = jnp.stack([jnp.broadcast_to(pe_k.reshape(1, -1), (8, CMP_LEN * HEAD_DIM)),
                     jnp.broadcast_to(pe_v.reshape(1, -1), (8, CMP_LEN * HEAD_DIM))])
    return w1pairs, pe8, jnp.stack([w_k1, w_v1]), jnp.stack([w_k2, w_v2])


def kernel(x_prompt, x_sample, c_prompt, c_sample, cache_nsa_kv, page_table, state_win_kv, state_conv, w_ada, b_ada, norm1_g, norm2_g, w_in, w_cmp_k1, w_cmp_k2, pe_cmp_k, w_cmp_v1, w_cmp_v2, pe_cmp_v, conv_w, conv_b, w_attn_proj, w_conv_proj, w_out, w_mlp1, w_mlp2, normf_g):
    d = D_MODEL
    nbp, seq, _ = x_prompt.shape
    nbs, ds, _ = x_sample.shape
    tp = nbp * seq
    ts = nbs * ds
    tc = 256
    depth = w_in.shape[0]
    assert depth == 1

    xp = x_prompt.reshape(tp, d)
    xs = x_sample.reshape(ts, d)
    normf = normf_g.reshape(1, d)

    l = 0
    c_all = jnp.concatenate([c_prompt, c_sample, jnp.zeros((16 - nbp - nbs, d), F32)], axis=0)
    ada = _ada(c_all, w_ada[l], b_ada[l].reshape(1, -1)).reshape(16, 6, d)
    ada_p = [ada[:nbp, k][:, None, :] for k in range(6)]
    ada_s = [jnp.repeat(ada[nbp:nbp + nbs, k], ds, axis=0)[None] for k in range(6)]

    w_t, w_head = _split_w_in(w_in[l])
    gates0 = _REST0 + 3 * d
    wa = w_attn_proj[l]
    wc = w_conv_proj[l]
    wo = w_out[l]
    w1 = w_mlp1[l].astype(BF16)
    w2 = w_mlp2[l].astype(BF16)
    g1n = norm1_g[l].reshape(1, d)
    g2n = norm2_g[l].reshape(1, d)
    cw = conv_w[l]
    cb = conv_b[l].reshape(1, d)
    w1pairs, pe8, w1s, w2s = _cmp_weights(w_cmp_k1[l], w_cmp_k2[l], pe_cmp_k[l],
                                          w_cmp_v1[l], w_cmp_v2[l], pe_cmp_v[l])

    tm = 1024
    hp = _norm_mod(xp, g1n, ada_p[1], ada_p[0], tm=512, seq=seq)
    q_p = _mm(hp, w_head, tm=tm, tn=1024, out_dtype=BF16, name="mm_q", col0=_Q0, n=N_HEADS * HEAD_DIM)
    wbp = min(WINDOW, seq)
    kv6_p, kvb_p = _mm_kv(hp, w_head, tm=tm, col0=_KV0)
    winb_p, win6_p, g_p = _mm_win(hp, w_head, tm=wbp, seq=seq, col0=_WIN0, gcol0=_G0)
    gates_p = _mm_wstat(hp, w_t, tm=tm, tn=1024, out_dtype=BF16, act="sigmoid", name="mm_gates",
                        col0=gates0, n=2 * d)
    z_p, conv_tiles = _mm_conv(hp, w_t, cw, cb, tm=tm, tc=tc, seq=seq, col0=_REST0)

    pt_p = jnp.arange(tp // PAGE, dtype=jnp.int32).reshape(nbp, seq // PAGE)
    part_p = _cmp_part(kv6_p.reshape(tp // PAGE, PAGE, 16, HEAD_DIM), pt_p, w1pairs)
    kcv_p = _cmp_finish(part_p, pe8, w1s, w2s)
    o_p = _attn_prompt(q_p, kcv_p, kvb_p, winb_p, _gate_layout(g_p, nbp, seq), nb=nbp, seq=seq)

    mixed_p = _mix(o_p, z_p, wa, wc, gates_p, tm=tm, tn=512)
    x1_p, h2_p = _mm_resid_norm(mixed_p, wo, xp, ada_p[2], g2n, ada_p[4], ada_p[3], tm=512, seq=seq)
    y_p = _mlp_final(h2_p, w1, w2, x1_p, ada_p[5], normf, tm=512, tf=1024, seq=seq)

    hs = _norm_mod(xs, g1n, ada_s[1], ada_s[0], tm=ts, seq=ds)
    q_s = _mm(hs, w_head, tm=ts, tn=512, out_dtype=F32, name="mm_q_s", col0=_Q0, n=N_HEADS * HEAD_DIM)
    kv_s = _mm(hs, w_head, tm=ts, tn=512, out_dtype=F32, name="mm_kv_s", col0=_KV0, n=4 * KV_W)
    win_s = _mm(hs, w_head, tm=ts, tn=512, out_dtype=F32, name="mm_win_s", col0=_WIN0, n=2 * KV_W)
    g_s = _mm(hs, w_head, tm=ts, tn=HEAD_DIM, out_dtype=F32, name="mm_g_s", col0=_G0, n=HEAD_DIM)
    gates_s = _mm_wstat(hs, w_t, tm=ts, tn=512, out_dtype=F32, act="sigmoid", name="mm_gates_s",
                        col0=gates0, n=2 * d)
    ubc_s = _mm_wstat(hs, w_t, tm=ts, tn=512, out_dtype=F32, name="mm_ubc_s", col0=_REST0, n=3 * d)
    st = state_conv[l]
    z_s, cu_s = _sample_conv(ubc_s, jnp.repeat(st[:, 0], ds, axis=0), jnp.repeat(st[:, 1], ds, axis=0),
                             cw, cb, tc=tc, seq=ds)

    cache4 = cache_nsa_kv[l].reshape(cache_nsa_kv.shape[1], PAGE, 16, HEAD_DIM)
    part_s = _cmp_part(cache4, page_table, w1pairs)
    kcv_s = _cmp_finish(part_s, pe8, w1s, w2s)
    wbs = state_win_kv.shape[2]
    wst = state_win_kv[l].reshape(nbs, wbs, 8, HEAD_DIM)
    o_s, wst_next = _attn_sample(q_s, kcv_s, kv_s, win_s, wst, _gate_layout(g_s, nbs, ds), cache4, page_table,
                                 ds=ds)

    mixed_s = _mix(o_s, z_s, wa, wc, gates_s, tm=ts, tn=512)
    x1_s, h2_s = _mm_resid_norm(mixed_s, wo, xs, ada_s[2], g2n, ada_s[4], ada_s[3], tm=ts, seq=ds)
    y_s = _mlp_final(h2_s, w1, w2, x1_s, ada_s[5], normf, tm=ts, tf=512, seq=ds)

    kv_prompt = kv6_p.reshape(1, nbp, seq, 4, N_KV, HEAD_DIM)
    kv_sample = kv_s.reshape(1, nbs, ds, 4, N_KV, HEAD_DIM)
    win_prompt = win6_p.reshape(1, nbp, wbp, 2, N_KV, HEAD_DIM)
    win_sample = wst_next.reshape(1, nbs, wbs, 2, N_KV, HEAD_DIM)
    tiles_per_seq = seq // tm
    conv_prompt = conv_tiles[tiles_per_seq - 1::tiles_per_seq][None]
    conv_sample = cu_s.reshape(nbs, ds, d)[None, :, ds - (CONV_W - 1):]
    return (y_p.reshape(nbp, seq, d), y_s.reshape(nbs, ds, d), kv_prompt, kv_sample,
            win_prompt, win_sample, conv_prompt, conv_sample)
```

```python
import functools

import numpy as np
import jax
import jax.numpy as jnp
from jax import lax
from jax.experimental import pallas as pl
from jax.experimental.pallas import tpu as pltpu

F32 = jnp.float32
BF16 = jnp.bfloat16

D_MODEL = 2048
HEAD_DIM = 128
N_HEADS = 16
N_KV = 4
GROUP = 4
KV_W = N_KV * HEAD_DIM
CMP_LEN = 32
CMP_STRIDE = 16
SLC_LEN = 64
SLC_TOP = 16
N_LOCAL = 2
WINDOW = 512
CONV_W = 3
RMS_EPS = 1e-6
NEG = -1e30
SCALE = HEAD_DIM ** -0.5
PAGE = 128
PAGES_PER_STEP = 16
CHUNK_ROWS = PAGES_PER_STEP * PAGE
BLK_PER_CHUNK = CHUNK_ROWS // SLC_LEN
VMEM_LIMIT = 56 * 1024 * 1024

_NT = (((1,), (1,)), ((), ()))
_TN = (((0,), (0,)), ((), ()))


def _cparams(sem):
    return pltpu.CompilerParams(dimension_semantics=sem, vmem_limit_bytes=VMEM_LIMIT)


def _ada_kernel(c_ref, w_ref, b_ref, o_ref):
    c = c_ref[...]
    a = (c * jax.nn.sigmoid(c)).astype(BF16)
    o_ref[...] = jnp.dot(a, w_ref[...].astype(BF16), preferred_element_type=F32) + b_ref[...]


def _ada(c, w, b, tn=1024):
    m, k = c.shape
    n = w.shape[1]
    return pl.pallas_call(
        _ada_kernel,
        grid=(n // tn,),
        in_specs=[pl.BlockSpec((m, k), lambda j: (0, 0)),
                  pl.BlockSpec((k, tn), lambda j: (0, j)),
                  pl.BlockSpec((1, tn), lambda j: (0, j))],
        out_specs=pl.BlockSpec((m, tn), lambda j: (0, j)),
        out_shape=jax.ShapeDtypeStruct((m, n), F32),
        compiler_params=_cparams(("arbitrary",)),
        name="ada",
    )(c, w, b)


def _rowspec(p, tm, tn, seq, col=True):
    gr = p.shape[1]
    if gr == 1:
        if col:
            return pl.BlockSpec((1, 1, tn), lambda i, j: ((i * tm) // seq, 0, j))
        return pl.BlockSpec((1, 1, tn), lambda i: ((i * tm) // seq, 0, 0))
    if col:
        return pl.BlockSpec((1, tm, tn), lambda i, j: (0, i, j))
    return pl.BlockSpec((1, tm, tn), lambda i: (0, i, 0))


def _xwt(x_ref, wt_ref):
    return lax.dot_general(x_ref[...].astype(BF16), wt_ref[...], _NT, preferred_element_type=F32)


def _mm_kernel(x_ref, w_ref, o_ref, *, act):
    acc = _xwt(x_ref, w_ref)
    if act == "sigmoid":
        acc = jax.nn.sigmoid(acc)
    o_ref[...] = acc.astype(o_ref.dtype)


def _norm_mm_kernel(x_ref, g_ref, sc_ref, sh_ref, w_ref, h_ref, o_ref):
    @pl.when(pl.program_id(1) == 0)
    def _():
        x = x_ref[...]
        r = lax.rsqrt(jnp.mean(x * x, axis=-1, keepdims=True) + RMS_EPS)
        y = (x * r) * g_ref[...]
        h_ref[...] = (y * (1.0 + sc_ref[0]) + sh_ref[0]).astype(h_ref.dtype)

    o_ref[...] = _xwt(h_ref, w_ref).astype(o_ref.dtype)


def _norm_mm(x, g, sc, sh, w, *, tm, tn, seq, out_dtype, col0, n, name):
    t, d = x.shape
    assert col0 % tn == 0 and n % tn == 0
    cb0 = col0 // tn
    row = lambda p: (pl.BlockSpec((1, 1, d), lambda i, j: ((i * tm) // seq, 0, 0)) if p.shape[1] == 1
                     else pl.BlockSpec((1, tm, d), lambda i, j: (0, i, 0)))
    return pl.pallas_call(
        _norm_mm_kernel,
        grid=(t // tm, n // tn),
        in_specs=[pl.BlockSpec((tm, d), lambda i, j: (i, 0)),
                  pl.BlockSpec((1, d), lambda i, j: (0, 0)),
                  row(sc), row(sh),
                  pl.BlockSpec((tn, d), lambda i, j: (cb0 + j, 0))],
        out_specs=[pl.BlockSpec((tm, d), lambda i, j: (i, 0)),
                   pl.BlockSpec((tm, tn), lambda i, j: (i, j))],
        out_shape=[jax.ShapeDtypeStruct((t, d), BF16), jax.ShapeDtypeStruct((t, n), out_dtype)],
        compiler_params=_cparams(("arbitrary", "arbitrary")),
        name=name,
    )(x, g, sc, sh, w)


def _mm_wstat_kernel(x_ref, w_ref, o_ref, wbf_ref, *, act):
    @pl.when(pl.program_id(1) == 0)
    def _():
        wbf_ref[...] = w_ref[...].astype(BF16)

    acc = _xwt(x_ref, wbf_ref)
    if act == "sigmoid":
        acc = jax.nn.sigmoid(acc)
    o_ref[...] = acc.astype(o_ref.dtype)


def _wrows(tn, k, row0):
    return pl.BlockSpec((pl.Element(tn), pl.Element(k)), lambda j, i: (pl.multiple_of(row0 + j * tn, 8), 0))


def _mm_wstat(x, w, *, tm, tn, out_dtype, act=None, name="mm", col0, n):
    t, k = x.shape
    assert col0 % 8 == 0 and n % tn == 0
    return pl.pallas_call(
        functools.partial(_mm_wstat_kernel, act=act),
        grid=(n // tn, t // tm),
        in_specs=[pl.BlockSpec((tm, k), lambda j, i: (i, 0)),
                  _wrows(tn, k, col0)],
        out_specs=pl.BlockSpec((tm, tn), lambda j, i: (i, j)),
        out_shape=jax.ShapeDtypeStruct((t, n), out_dtype),
        scratch_shapes=[pltpu.VMEM((tn, k), BF16)],
        compiler_params=_cparams(("arbitrary", "arbitrary")),
        name=name,
    )(x, w)


def _mm(x, w, *, tm, tn, out_dtype, act=None, name="mm", col0, n):
    t, k = x.shape
    assert col0 % tn == 0 and n % tn == 0
    cb0 = col0 // tn
    return pl.pallas_call(
        functools.partial(_mm_kernel, act=act),
        grid=(t // tm, n // tn),
        in_specs=[pl.BlockSpec((tm, k), lambda i, j: (i, 0)),
                  pl.BlockSpec((tn, k), lambda i, j: (cb0 + j, 0))],
        out_specs=pl.BlockSpec((tm, tn), lambda i, j: (i, j)),
        out_shape=jax.ShapeDtypeStruct((t, n), out_dtype),
        compiler_params=_cparams(("arbitrary", "arbitrary")),
        name=name,
    )(x, w)


def _store_head_major(o_ref, acc, tm):
    flat = o_ref.reshape(tm * 8, HEAD_DIM)
    for cb in range(8):
        flat[pl.ds(cb, tm, stride=8), :] = acc[:, cb * HEAD_DIM:(cb + 1) * HEAD_DIM]


def _mm_kv_kernel(x_ref, w_ref, o6_ref, ob_ref, *, tm):
    acc = _xwt(x_ref, w_ref)
    _store_head_major(o6_ref, acc, tm)

    @pl.when(pl.program_id(1) == 1)
    def _():
        ob_ref[...] = acc.astype(BF16)


def _mm_kv(h, w, *, tm, col0):
    t, k = h.shape
    tn = 2 * KV_W
    cb0 = col0 // tn
    return pl.pallas_call(
        functools.partial(_mm_kv_kernel, tm=tm),
        grid=(t // tm, 2),
        in_specs=[pl.BlockSpec((tm, k), lambda i, j: (i, 0)),
                  pl.BlockSpec((tn, k), lambda i, j: (cb0 + j, 0))],
        out_specs=[pl.BlockSpec((tm, 8, HEAD_DIM), lambda i, j: (i, j, 0)),
                   pl.BlockSpec((tm, tn), lambda i, j: (i, 0))],
        out_shape=[jax.ShapeDtypeStruct((t, 16, HEAD_DIM), F32),
                   jax.ShapeDtypeStruct((t, tn), BF16)],
        compiler_params=_cparams(("arbitrary", "arbitrary")),
        name="mm_kv",
    )(h, w)


def _mm_win_kernel(x_ref, w_ref, wg_ref, ob_ref, o6_ref, og_ref, *, tm, seq):
    acc = _xwt(x_ref, w_ref)
    ob_ref[...] = acc.astype(BF16)
    og_ref[...] = _xwt(x_ref, wg_ref)

    @pl.when(((pl.program_id(0) + 1) * tm) % seq == 0)
    def _():
        _store_head_major(o6_ref, acc, tm)


def _mm_win(h, w, *, tm, seq, col0, gcol0):
    t, k = h.shape
    tn = 2 * KV_W
    cb0 = col0 // tn
    gb0 = gcol0 // HEAD_DIM
    return pl.pallas_call(
        functools.partial(_mm_win_kernel, tm=tm, seq=seq),
        grid=(t // tm,),
        in_specs=[pl.BlockSpec((tm, k), lambda i: (i, 0)),
                  pl.BlockSpec((tn, k), lambda i: (cb0, 0)),
                  pl.BlockSpec((HEAD_DIM, k), lambda i: (gb0, 0))],
        out_specs=[pl.BlockSpec((tm, tn), lambda i: (i, 0)),
                   pl.BlockSpec((tm, 8, HEAD_DIM), lambda i: ((i * tm) // seq, 0, 0)),
                   pl.BlockSpec((tm, HEAD_DIM), lambda i: (i, 0))],
        out_shape=[jax.ShapeDtypeStruct((t, tn), BF16),
                   jax.ShapeDtypeStruct((t // seq * tm, 8, HEAD_DIM), F32),
                   jax.ShapeDtypeStruct((t, HEAD_DIM), F32)],
        compiler_params=_cparams(("arbitrary",)),
        name="mm_win",
    )(h, w, w)


def _conv_taps(cu, prev1, prev2, bgate, cw_ref, cb_ref):
    y = cb_ref[...] + prev2 * cw_ref[0:1, :]
    y = y + prev1 * cw_ref[1:2, :]
    y = y + cu * cw_ref[2:3, :]
    return bgate * y


def _mm_conv_kernel(x_ref, wu_ref, wb_ref, wc_ref, cw_ref, cb_ref, z_ref, st_ref, w3_ref, carry_ref, *, tm, tc, seq):
    i = pl.program_id(1)

    @pl.when(i == 0)
    def _():
        w3_ref[0:tc] = wu_ref[...].astype(BF16)
        w3_ref[tc:2 * tc] = wb_ref[...].astype(BF16)
        w3_ref[2 * tc:3 * tc] = wc_ref[...].astype(BF16)

    acc = _xwt(x_ref, w3_ref)
    u = acc[:, 0:tc]
    bgate = acc[:, tc:2 * tc]
    cu = acc[:, 2 * tc:3 * tc] * u

    @pl.when((i * tm) % seq == 0)
    def _():
        carry_ref[...] = jnp.zeros((8, tc), F32)

    car = carry_ref[...]
    p0 = car[0:1, :]
    p1 = car[1:2, :]
    rows = lax.broadcasted_iota(jnp.int32, (tm, tc), 0)
    r1 = pltpu.roll(cu, 1, 0)
    r2 = pltpu.roll(cu, 2, 0)
    prev1 = jnp.where(rows == 0, p1, r1)
    prev2 = jnp.where(rows == 0, p0, jnp.where(rows == 1, p1, r2))
    z_ref[...] = _conv_taps(cu, prev1, prev2, bgate, cw_ref, cb_ref).astype(z_ref.dtype)
    tail = cu[tm - 8:tm, :]
    carry_ref[...] = jnp.concatenate([tail[6:8, :], tail[0:6, :]], axis=0)
    st_ref[0] = tail[6:8, :]


def _mm_conv(h, w, conv_w, conv_b, *, tm, tc, seq, col0):
    t, k = h.shape
    d = conv_w.shape[1]
    nj = d // tc
    return pl.pallas_call(
        functools.partial(_mm_conv_kernel, tm=tm, tc=tc, seq=seq),
        grid=(nj, t // tm),
        in_specs=[pl.BlockSpec((tm, k), lambda j, i: (i, 0)),
                  _wrows(tc, k, col0), _wrows(tc, k, col0 + d), _wrows(tc, k, col0 + 2 * d),
                  pl.BlockSpec((CONV_W, tc), lambda j, i: (0, j)),
                  pl.BlockSpec((1, tc), lambda j, i: (0, j))],
        out_specs=[pl.BlockSpec((tm, tc), lambda j, i: (i, j)),
                   pl.BlockSpec((1, CONV_W - 1, tc), lambda j, i: (i, 0, j))],
        out_shape=[jax.ShapeDtypeStruct((t, d), BF16),
                   jax.ShapeDtypeStruct((t // tm, CONV_W - 1, d), F32)],
        scratch_shapes=[pltpu.VMEM((3 * tc, k), BF16), pltpu.VMEM((8, tc), F32)],
        compiler_params=_cparams(("arbitrary", "arbitrary")),
        name="mm_conv",
    )(h, w, w, w, conv_w, conv_b)


def _sample_conv_kernel(u_ref, b_ref, c_ref, st0_ref, st1_ref, cw_ref, cb_ref, z_ref, cu_ref, *, tc, seq):
    t = u_ref.shape[0]
    bgate = b_ref[...]
    cu = c_ref[...] * u_ref[...]
    s = lax.rem(lax.broadcasted_iota(jnp.int32, (t, tc), 0), jnp.int32(seq))
    r1 = pltpu.roll(cu, 1, 0)
    r2 = pltpu.roll(cu, 2, 0)
    prev1 = jnp.where(s == 0, st1_ref[...], r1)
    prev2 = jnp.where(s == 0, st0_ref[...], jnp.where(s == 1, st1_ref[...], r2))
    z_ref[...] = _conv_taps(cu, prev1, prev2, bgate, cw_ref, cb_ref)
    cu_ref[...] = cu


def _sample_conv(ubc, st0, st1, conv_w, conv_b, *, tc, seq):
    t = ubc.shape[0]
    d = ubc.shape[1] // 3
    nj = d // tc
    return pl.pallas_call(
        functools.partial(_sample_conv_kernel, tc=tc, seq=seq),
        grid=(nj,),
        in_specs=[pl.BlockSpec((t, tc), lambda j: (0, j)),
                  pl.BlockSpec((t, tc), lambda j: (0, nj + j)),
                  pl.BlockSpec((t, tc), lambda j: (0, 2 * nj + j)),
                  pl.BlockSpec((t, tc), lambda j: (0, j)),
                  pl.BlockSpec((t, tc), lambda j: (0, j)),
                  pl.BlockSpec((CONV_W, tc), lambda j: (0, j)),
                  pl.BlockSpec((1, tc), lambda j: (0, j))],
        out_specs=[pl.BlockSpec((t, tc), lambda j: (0, j)),
                   pl.BlockSpec((t, tc), lambda j: (0, j))],
        out_shape=[jax.ShapeDtypeStruct((t, d), F32), jax.ShapeDtypeStruct((t, d), F32)],
        compiler_params=_cparams(("arbitrary",)),
        name="sample_conv",
    )(ubc, ubc, ubc, st0, st1, conv_w, conv_b)


def _mix_kernel(o_ref, z_ref, wa_ref, wc_ref, ga_ref, gc_ref, m_ref):
    a = jnp.dot(o_ref[...].astype(BF16), wa_ref[...].astype(BF16), preferred_element_type=F32)
    c = jnp.dot(z_ref[...].astype(BF16), wc_ref[...].astype(BF16), preferred_element_type=F32)
    m_ref[...] = (ga_ref[...].astype(F32) * a + gc_ref[...].astype(F32) * c).astype(m_ref.dtype)


def _mix(o, z, wa, wc, gates, *, tm, tn):
    t, k = o.shape
    n = wa.shape[1]
    nj = n // tn
    return pl.pallas_call(
        _mix_kernel,
        grid=(t // tm, nj),
        in_specs=[pl.BlockSpec((tm, k), lambda i, j: (i, 0)),
                  pl.BlockSpec((tm, k), lambda i, j: (i, 0)),
                  pl.BlockSpec((k, tn), lambda i, j: (0, j)),
                  pl.BlockSpec((k, tn), lambda i, j: (0, j)),
                  pl.BlockSpec((tm, tn), lambda i, j: (i, j)),
                  pl.BlockSpec((tm, tn), lambda i, j: (i, j + nj))],
        out_specs=pl.BlockSpec((tm, tn), lambda i, j: (i, j)),
        out_shape=jax.ShapeDtypeStruct((t, n), BF16),
        compiler_params=_cparams(("arbitrary", "arbitrary")),
        name="mix",
    )(o, z, wa, wc, gates, gates)


def _resid_norm_kernel(a_ref, w_ref, x_ref, g_ref, ng_ref, sc_ref, sh_ref, x1_ref, h2_ref, wbf_ref):
    @pl.when(pl.program_id(0) == 0)
    def _():
        wbf_ref[...] = w_ref[...].astype(BF16)

    acc = jnp.dot(a_ref[...], wbf_ref[...], preferred_element_type=F32)
    x1 = x_ref[...] + g_ref[0] * acc
    x1_ref[...] = x1
    r = lax.rsqrt(jnp.mean(x1 * x1, axis=-1, keepdims=True) + RMS_EPS)
    y = (x1 * r) * ng_ref[...]
    h2_ref[...] = (y * (1.0 + sc_ref[0]) + sh_ref[0]).astype(h2_ref.dtype)


def _mm_resid_norm(a, w, x, gate, ng, sc, sh, *, tm, seq):
    t, k = a.shape
    n = w.shape[1]
    row = lambda p: _rowspec(p, tm, n, seq, col=False)
    return pl.pallas_call(
        _resid_norm_kernel,
        grid=(t // tm,),
        in_specs=[pl.BlockSpec((tm, k), lambda i: (i, 0)),
                  pl.BlockSpec((k, n), lambda i: (0, 0), pipeline_mode=pl.Buffered(1)),
                  pl.BlockSpec((tm, n), lambda i: (i, 0)),
                  row(gate),
                  pl.BlockSpec((1, n), lambda i: (0, 0)),
                  row(sc), row(sh)],
        out_specs=[pl.BlockSpec((tm, n), lambda i: (i, 0)), pl.BlockSpec((tm, n), lambda i: (i, 0))],
        out_shape=[jax.ShapeDtypeStruct((t, n), F32), jax.ShapeDtypeStruct((t, n), BF16)],
        scratch_shapes=[pltpu.VMEM((k, n), BF16)],
        compiler_params=_cparams(("arbitrary",)),
        name="mm_resid_norm",
    )(a, w, x, gate, ng, sc, sh)


def _mlp_kernel(h_ref, w1_ref, w2_ref, x_ref, g_ref, nf_ref, o_ref, acc_ref):
    k = pl.program_id(1)

    @pl.when(k == 0)
    def _():
        acc_ref[...] = jnp.zeros_like(acc_ref)

    a = jnp.dot(h_ref[...], w1_ref[...], preferred_element_type=F32)
    a = jnp.square(jnp.maximum(a, 0.0)).astype(BF16)
    acc_ref[...] += jnp.dot(a, w2_ref[...], preferred_element_type=F32)

    @pl.when(k == pl.num_programs(1) - 1)
    def _():
        x2 = x_ref[...] + g_ref[0] * acc_ref[...]
        r = lax.rsqrt(jnp.mean(x2 * x2, axis=-1, keepdims=True) + RMS_EPS)
        o_ref[...] = (x2 * r) * nf_ref[...]


def _mlp_final(h2, w1, w2, x1, gate, normf, *, tm, tf, seq):
    t, d = h2.shape
    f = w1.shape[1]
    return pl.pallas_call(
        _mlp_kernel,
        grid=(t // tm, f // tf),
        in_specs=[pl.BlockSpec((tm, d), lambda i, k: (i, 0)),
                  pl.BlockSpec((d, tf), lambda i, k: (0, k)),
                  pl.BlockSpec((tf, d), lambda i, k: (k, 0)),
                  pl.BlockSpec((tm, d), lambda i, k: (i, 0)),
                  _rowspec_k(gate, tm, d, seq),
                  pl.BlockSpec((1, d), lambda i, k: (0, 0))],
        out_specs=pl.BlockSpec((tm, d), lambda i, k: (i, 0)),
        out_shape=jax.ShapeDtypeStruct((t, d), F32),
        scratch_shapes=[pltpu.VMEM((tm, d), F32)],
        compiler_params=_cparams(("arbitrary", "arbitrary")),
        name="mlp",
    )(h2, w1, w2, x1, gate, normf)


def _rowspec_k(p, tm, d, seq):
    if p.shape[1] == 1:
        return pl.BlockSpec((1, 1, d), lambda i, k: ((i * tm) // seq, 0, 0))
    return pl.BlockSpec((1, tm, d), lambda i, k: (0, i, 0))


def _cmp_part_kernel(pt_ref, *refs):
    pages = refs[:PAGES_PER_STEP]
    w_ref = refs[PAGES_PER_STEP]
    o_ref = refs[PAGES_PER_STEP + 1]
    stage = refs[PAGES_PER_STEP + 2]
    nchunk = CHUNK_ROWS // CMP_STRIDE
    cpp = PAGE // CMP_STRIDE
    nslab = 2 * N_KV
    cols = [jnp.concatenate([pg[pl.ds(r, cpp, stride=CMP_STRIDE)].reshape(cpp * nslab, HEAD_DIM)
                             for pg in pages], axis=0).astype(BF16) for r in range(CMP_STRIDE)]
    lhs = jnp.concatenate(cols, axis=1)
    acc = jnp.dot(lhs, w_ref[...], preferred_element_type=F32)
    slab = lax.broadcasted_iota(jnp.int32, (nchunk * nslab, 2 * HEAD_DIM), 0) & (nslab - 1)
    sel = jnp.where(slab < N_KV, acc[:, :2 * HEAD_DIM], acc[:, 2 * HEAD_DIM:])
    stage[0] = sel[:, :HEAD_DIM]
    stage[1] = sel[:, HEAD_DIM:]
    for cb in range(nslab):
        o_ref[cb // N_KV, cb % N_KV] = jnp.concatenate(
            [stage[0, pl.ds(cb, nchunk, stride=nslab), :], stage[1, pl.ds(cb, nchunk, stride=nslab), :]], axis=1)


def _page_spec(n, kind_pair):
    return pl.BlockSpec((None, PAGE, 8, HEAD_DIM),
                        lambda b, p, pt, n=n: (pt[b, p * PAGES_PER_STEP + n], 0, kind_pair, 0))


def _cmp_part(pages4, page_table, w1pairs):
    nb, npg = page_table.shape
    steps = npg // PAGES_PER_STEP
    nchunk = PAGES_PER_STEP * (PAGE // CMP_STRIDE)
    grid_spec = pltpu.PrefetchScalarGridSpec(
        num_scalar_prefetch=1,
        grid=(nb, steps),
        in_specs=[_page_spec(n, 0) for n in range(PAGES_PER_STEP)]
        + [pl.BlockSpec((CMP_STRIDE * HEAD_DIM, 4 * HEAD_DIM), lambda b, p, pt: (0, 0))],
        out_specs=pl.BlockSpec((None, 2, N_KV, nchunk, 2 * HEAD_DIM), lambda b, p, pt: (b, 0, 0, p, 0)),
        scratch_shapes=[pltpu.VMEM((2, nchunk * 2 * N_KV, HEAD_DIM), F32)],
    )
    return pl.pallas_call(
        _cmp_part_kernel,
        grid_spec=grid_spec,
        out_shape=jax.ShapeDtypeStruct((nb, 2, N_KV, steps * nchunk, 2 * HEAD_DIM), F32),
        compiler_params=_cparams(("arbitrary", "arbitrary")),
        name="cmp_part",
    )(page_table, *([pages4] * PAGES_PER_STEP), w1pairs)


def _gelu_tanh(x):
    c = np.sqrt(2.0 / np.pi).astype(np.float32)
    return 0.5 * x * (1.0 + jnp.tanh(c * (x + 0.044715 * (x * x * x))))


def _cmp_finish_kernel(part_ref, pe_ref, w1_ref, w2_ref, o_ref, bias_ref):
    @pl.when(pl.program_id(1) == 0)
    def _():
        bias_ref[...] = jnp.dot(pe_ref[...], w1_ref[...], preferred_element_type=F32,
                                precision=lax.Precision.HIGHEST)

    w2 = w2_ref[...].astype(BF16)
    for hd in range(N_KV):
        part = part_ref[hd]
        n = part.shape[0]
        h = part[:, 0:HEAD_DIM] + pltpu.roll(part[:, HEAD_DIM:], n - 1, 0)
        h = h + bias_ref[0:1, :]
        o_ref[hd] = jnp.dot(_gelu_tanh(h).astype(BF16), w2, preferred_element_type=F32).astype(o_ref.dtype)


def _cmp_finish(part, pe8, w1, w2):
    nb, _, _, nchunk, _ = part.shape
    return pl.pallas_call(
        _cmp_finish_kernel,
        grid=(2, nb),
        in_specs=[pl.BlockSpec((None, None, N_KV, nchunk, 2 * HEAD_DIM), lambda k, b: (b, k, 0, 0, 0)),
                  pl.BlockSpec((None, 8, CMP_LEN * HEAD_DIM), lambda k, b: (k, 0, 0)),
                  pl.BlockSpec((None, CMP_LEN * HEAD_DIM, HEAD_DIM), lambda k, b: (k, 0, 0)),
                  pl.BlockSpec((None, HEAD_DIM, HEAD_DIM), lambda k, b: (k, 0, 0))],
        out_specs=pl.BlockSpec((None, None, N_KV, nchunk, HEAD_DIM), lambda k, b: (b, k, 0, 0, 0)),
        out_shape=jax.ShapeDtypeStruct((nb, 2, N_KV, nchunk, HEAD_DIM), BF16),
        scratch_shapes=[pltpu.VMEM((8, HEAD_DIM), F32)],
        compiler_params=_cparams(("arbitrary", "arbitrary")),
        name="cmp_finish",
    )(part, pe8, w1, w2)


def _select_topk(score, valid, forced, blk, axis):
    sc = jnp.where(valid, jnp.where(forced, jnp.inf, score), -jnp.inf)
    sel = jnp.zeros(score.shape, F32)
    big = jnp.int32(1 << 20)
    for _ in range(SLC_TOP):
        mx = jnp.max(sc, axis=axis, keepdims=True)
        idx = jnp.min(jnp.where(sc == mx, blk, big), axis=axis, keepdims=True)
        hit = blk == idx
        sel = jnp.where(hit & (mx > -jnp.inf), 1.0, sel)
        sc = jnp.where(hit, -jnp.inf, sc)
    return sel


def _overlap_matrix(nc_pad, n_blocks_pad, nc, n_blocks):
    cs = np.arange(nc_pad) * CMP_STRIDE
    sb = np.arange(n_blocks_pad) * SLC_LEN
    ov = np.clip(np.minimum(cs[:, None] + CMP_LEN, sb[None, :] + SLC_LEN)
                 - np.maximum(cs[:, None], sb[None, :]), 0, None)
    m = (ov / CMP_STRIDE).astype(np.float32)
    m[nc:, :] = 0.0
    m[:, n_blocks:] = 0.0
    return m


def _attn_prompt_kernel(q_ref, kcb, vcb, ksb, vsb, kwb, vwb, g_ref, mt_ref, e_ref, o_ref,
                        m_scr, l_scr, acc_scr, bias_scr, gate_scr, *, tq, seq, ck):
    qi = pl.program_id(2)
    nc = seq // CMP_STRIDE - 1
    nsb = seq // SLC_LEN
    rb = min(tq, 128)

    q = q_ref[...]
    qq = jnp.concatenate([q[:, g * HEAD_DIM:(g + 1) * HEAD_DIM] for g in range(GROUP)], axis=0)
    rq = GROUP * tq
    q0 = qi * tq

    gt = jax.nn.sigmoid(g_ref[...])
    ngc = 4 * GROUP
    hi = gt.astype(BF16)
    r1 = gt - hi.astype(F32)
    mid = r1.astype(BF16)
    lo = (r1 - mid.astype(F32)).astype(BF16)
    g3 = jnp.concatenate([hi, mid, lo, jnp.zeros((tq, HEAD_DIM - 3 * ngc), BF16)], axis=1)
    srow = lax.broadcasted_iota(jnp.int32, (HEAD_DIM, 3 * GROUP * HEAD_DIM), 0)
    scol = lax.broadcasted_iota(jnp.int32, (HEAD_DIM, 3 * GROUP * HEAD_DIM), 1)
    spread = ((srow < 3 * ngc) & ((srow & (ngc - 1)) == lax.div(scol, jnp.int32(HEAD_DIM)))).astype(BF16)
    gl = jnp.dot(g3, spread, preferred_element_type=F32)
    for c in range(3 * GROUP):
        gate_scr[c] = gl[:, c * HEAD_DIM:(c + 1) * HEAD_DIM]

    ncp = kcb.shape[0]
    s = lax.dot_general(qq, kcb[...], _NT, preferred_element_type=F32) * SCALE
    qpos = q0 + (lax.broadcasted_iota(jnp.int32, (rq, ncp), 0) & (tq - 1))
    col = lax.broadcasted_iota(jnp.int32, (rq, ncp), 1)
    vis = (col * CMP_STRIDE + (CMP_LEN - 1) <= qpos) & (col < nc)
    s = jnp.where(vis, s, NEG)
    mx = jnp.max(s, axis=1, keepdims=True)
    e = jnp.where(vis, jnp.exp(s - mx), 0.0)
    den = jnp.sum(e, axis=1, keepdims=True)
    p = e / jnp.where(den > 0.0, den, 1.0)
    o_cmp = jnp.dot(p.astype(BF16), vcb[...], preferred_element_type=F32)
    pg = p[0:tq]
    for g in range(1, GROUP):
        pg = pg + p[g * tq:(g + 1) * tq]

    sct = lax.dot_general(mt_ref[...], pg, _NT, preferred_element_type=F32,
                          precision=lax.Precision.HIGHEST)
    blk = lax.broadcasted_iota(jnp.int32, (nsb, tq), 0)
    cur = lax.div(q0 + lax.broadcasted_iota(jnp.int32, (nsb, tq), 1), jnp.int32(SLC_LEN))
    valid = blk <= cur
    forced = (blk == 0) | (blk > cur - N_LOCAL)
    need_topk = (q0 + tq - 1) // SLC_LEN + 1 > SLC_TOP
    sel_t = lax.cond(need_topk,
                     lambda: _select_topk(sct, valid, forced, blk, 0),
                     lambda: valid.astype(F32)).astype(BF16)

    def sweep(k_ref, v_ref, k_start, n_chunks, bias_fn):
        m_scr[...] = jnp.full((rq, HEAD_DIM), NEG, F32)
        l_scr[...] = jnp.zeros((rq, HEAD_DIM), F32)
        acc_scr[...] = jnp.zeros((rq, HEAD_DIM), F32)
        nl = ck // HEAD_DIM

        def scores(c):
            k0 = pl.multiple_of(k_start + c * ck, 128)
            return lax.dot_general(qq, k_ref[pl.ds(k0, ck), :], _NT, preferred_element_type=F32)

        def chunk_inputs(c):
            k0 = pl.multiple_of(k_start + c * ck, 128)
            return scores(c), bias_fn(c, k0)

        c_exp = np.float32(SCALE * np.log2(np.e))

        def reduce_chunk(c, sb):
            sc_, bias = sb
            k0 = pl.multiple_of(k_start + c * ck, 128)
            vch = jnp.concatenate([v_ref[pl.ds(k0, ck), :], jnp.ones((ck, HEAD_DIM), BF16)], axis=1)
            alphas, ps = [], []
            nblk = rq // rb
            for r in range(nblk):
                rows = slice(r * rb, (r + 1) * rb)
                b0 = (r * rb) % tq
                sg = [sc_[rows, j * HEAD_DIM:(j + 1) * HEAD_DIM]
                      + bias[b0:b0 + rb, j * HEAD_DIM:(j + 1) * HEAD_DIM] for j in range(nl)]
                mx = functools.reduce(jnp.maximum, sg)
                m_old = m_scr[rows]
                m_new = jnp.maximum(m_old, jnp.max(mx, axis=1, keepdims=True))
                alpha = jnp.exp2((m_old - m_new) * c_exp)
                pj = [jnp.exp2((x - m_new) * c_exp) for x in sg]
                m_scr[rows] = m_new
                alphas.append(alpha)
                ps.append(jnp.concatenate([x.astype(BF16) for x in pj], axis=1))
                if (r + 1) % (nblk // 2) == 0:
                    half = slice((r + 1 - nblk // 2) * rb, (r + 1) * rb)
                    pv = jnp.dot(jnp.concatenate(ps, axis=0), vch, preferred_element_type=F32)
                    al = jnp.concatenate(alphas, axis=0)
                    acc_scr[half] = al * acc_scr[half] + pv[:, :HEAD_DIM]
                    l_scr[half] = al * l_scr[half] + pv[:, HEAD_DIM:]
                    alphas, ps = [], []

        def body(c, carry):
            reduce_chunk(c, chunk_inputs(c))
            return carry

        lax.fori_loop(0, n_chunks, body, 0)
        return acc_scr[...] / l_scr[...]

    qp = q0 + lax.broadcasted_iota(jnp.int32, (tq, ck), 0)
    kcol = lax.broadcasted_iota(jnp.int32, (tq, ck), 1)

    eye = (lax.broadcasted_iota(jnp.int32, (nsb, HEAD_DIM), 0)
           == lax.broadcasted_iota(jnp.int32, (nsb, HEAD_DIM), 1)).astype(BF16)
    sel_q = lax.dot_general(sel_t, eye, _TN, preferred_element_type=F32).astype(BF16)

    n_slc = (q0 + tq + ck - 1) // ck

    def put_mask(c):
        selq = jnp.dot(sel_q, e_ref[c], preferred_element_type=F32)
        bias_scr[c] = jnp.where((selq > 0.5) & (c * ck + kcol <= qp), 0.0, NEG)

    def mask_body(i, carry):
        put_mask(2 * i)
        put_mask(jnp.minimum(2 * i + 1, seq // ck - 1))
        return carry

    lax.fori_loop(0, (n_slc + 1) // 2, mask_body, 0)
    o_slc = sweep(ksb, vsb, 0, n_slc, lambda c, k0: bias_scr[c])

    span = -(-(WINDOW + tq) // ck) * ck

    def win_bias(c, k0):
        dlt = qp - (k0 + kcol)
        return jnp.where((dlt >= 0) & (dlt < WINDOW), 0.0, NEG)

    o_win = sweep(kwb, vwb, jnp.maximum(q0 + tq - span, 0), span // ck, win_bias)

    for g in range(GROUP):
        r0, r1 = g * tq, (g + 1) * tq
        og = gate_scr[g] * o_cmp[r0:r1] + gate_scr[GROUP + g] * o_slc[r0:r1]
        og = og + gate_scr[2 * GROUP + g] * o_win[r0:r1]
        o_ref[:, g * HEAD_DIM:(g + 1) * HEAD_DIM] = og.astype(o_ref.dtype)


def _attn_prompt(q, kcv, kv, win, g_re, *, nb, seq, tq=512, ck=512):
    assert tq & (tq - 1) == 0 and seq % ck == 0 and WINDOW % tq == 0
    t = q.shape[0]
    ncp = seq // CMP_STRIDE
    nsb = seq // SLC_LEN
    nq = seq // tq
    mt = jnp.asarray(_overlap_matrix(ncp, nsb, ncp - 1, nsb).T)
    kk = np.arange(seq)
    assert nsb <= HEAD_DIM
    e3 = (kk[None, :] // SLC_LEN == np.arange(HEAD_DIM)[:, None]).astype(np.float32)
    e3 = jnp.asarray(e3.reshape(HEAD_DIM, seq // ck, ck).transpose(1, 0, 2), dtype=BF16)
    slab = lambda cb: pl.BlockSpec((seq, HEAD_DIM), lambda b, h, i, cb=cb: (b, cb + h))
    return pl.pallas_call(
        functools.partial(_attn_prompt_kernel, tq=tq, seq=seq, ck=ck),
        grid=(nb, N_KV, nq),
        in_specs=[pl.BlockSpec((tq, GROUP * HEAD_DIM), lambda b, h, i: (b * nq + i, h)),
                  pl.BlockSpec((None, None, None, ncp, HEAD_DIM), lambda b, h, i: (b, 0, h, 0, 0)),
                  pl.BlockSpec((None, None, None, ncp, HEAD_DIM), lambda b, h, i: (b, 1, h, 0, 0)),
                  slab(0), slab(N_KV), slab(0), slab(N_KV),
                  pl.BlockSpec((None, None, tq, 4 * GROUP), lambda b, h, i: (b, h, i, 0)),
                  pl.BlockSpec((nsb, ncp), lambda b, h, i: (0, 0)),
                  pl.BlockSpec((seq // ck, HEAD_DIM, ck), lambda b, h, i: (0, 0, 0))],
        out_specs=pl.BlockSpec((tq, GROUP * HEAD_DIM), lambda b, h, i: (b * nq + i, h)),
        out_shape=jax.ShapeDtypeStruct((t, N_HEADS * HEAD_DIM), BF16),
        scratch_shapes=[pltpu.VMEM((GROUP * tq, HEAD_DIM), F32)] * 3
        + [pltpu.VMEM((seq // ck, tq, ck), F32), pltpu.VMEM((3 * GROUP, tq, HEAD_DIM), F32)],
        compiler_params=_cparams(("arbitrary", "arbitrary", "arbitrary")),
        name="attn_prompt",
    )(q, kcv, kcv, kv, kv, win, win, g_re, mt, e3)


def _attn_sample_kernel(pt_ref, cache_ref, q_ref, kcv_ref, kvn_ref, wn_ref, st_ref, g_ref, ms_ref, e_ref, o_ref,
                        wso_ref, m_scr, l_scr, acc_scr, sel_scr, ocmp_scr, kv_buf, sem, *, past, ds):
    b = pl.program_id(0)
    p = pl.program_id(1)
    n_steps = pl.num_programs(1)

    def page_copies(bb, pp, slot):
        cps = []
        for n in range(PAGES_PER_STEP):
            page = pt_ref[bb, pp * PAGES_PER_STEP + n]
            for cb in range(2 * N_KV):
                cps.append(pltpu.make_async_copy(cache_ref.at[page, :, 2 * N_KV + cb, :],
                                                 kv_buf.at[slot, cb, pl.ds(n * PAGE, PAGE), :],
                                                 sem.at[slot]))
        return cps

    t = b * n_steps + p
    slot = lax.rem(t, 2)

    @pl.when(t == 0)
    def _():
        for cp in page_copies(b, p, slot):
            cp.start()

    @pl.when(t + 1 < pl.num_programs(0) * n_steps)
    def _():
        wrap = p + 1 == n_steps
        for cp in page_copies(jnp.where(wrap, b + 1, b), jnp.where(wrap, 0, p + 1), 1 - slot):
            cp.start()
    rq = GROUP * ds
    ncp = kcv_ref.shape[2]
    nc = ncp - 1
    n_sel_chunks = sel_scr.shape[1]
    lanes = n_sel_chunks * HEAD_DIM

    def q_rows(h):
        c0 = h * GROUP * HEAD_DIM
        return jnp.concatenate(
            [q_ref[:, c0 + g * HEAD_DIM:c0 + (g + 1) * HEAD_DIM] for g in range(GROUP)], axis=0).astype(BF16)

    def online_update(h, sc_, vals):
        m_i = m_scr[h]
        m_new = jnp.maximum(m_i, jnp.max(sc_, axis=1, keepdims=True))
        alpha = jnp.exp(m_i - m_new)
        pp = jnp.exp(sc_ - m_new)
        l_scr[h] = alpha * l_scr[h] + jnp.sum(pp, axis=1, keepdims=True)
        acc_scr[h] = alpha * acc_scr[h] + jnp.dot(pp.astype(BF16), vals, preferred_element_type=F32)
        m_scr[h] = m_new

    @pl.when(p == 0)
    def _():
        pgs = []
        for h in range(N_KV):
            qq = q_rows(h)
            kc = kcv_ref[0, h]
            vc = kcv_ref[1, h]
            s = lax.dot_general(qq, kc, _NT, preferred_element_type=F32) * SCALE
            qpos = past + (lax.broadcasted_iota(jnp.int32, (rq, ncp), 0) & (ds - 1))
            col = lax.broadcasted_iota(jnp.int32, (rq, ncp), 1)
            vis = (col * CMP_STRIDE + (CMP_LEN - 1) <= qpos) & (col < nc)
            s = jnp.where(vis, s, NEG)
            mx = jnp.max(s, axis=1, keepdims=True)
            e = jnp.where(vis, jnp.exp(s - mx), 0.0)
            den = jnp.sum(e, axis=1, keepdims=True)
            pr = e / jnp.where(den > 0.0, den, 1.0)
            ocmp_scr[h] = jnp.dot(pr.astype(BF16), vc, preferred_element_type=F32)
            pg = pr[0:ds]
            for g in range(1, GROUP):
                pg = pg + pr[g * ds:(g + 1) * ds]
            pgs.append(pg)
            m_scr[h] = jnp.full((rq, 1), NEG, F32)
            l_scr[h] = jnp.zeros((rq, 1), F32)
            acc_scr[h] = jnp.zeros((rq, HEAD_DIM), F32)
        nr = N_KV * ds
        pg_all = jnp.concatenate(pgs, axis=0)
        hi = pg_all.astype(BF16)
        r1 = pg_all - hi.astype(F32)
        mid = r1.astype(BF16)
        lo = (r1 - mid.astype(F32)).astype(BF16)
        sc3 = jnp.dot(jnp.concatenate([hi, mid, lo], axis=0), ms_ref[...], preferred_element_type=F32)
        score = (sc3[0:nr] + sc3[nr:2 * nr]) + sc3[2 * nr:3 * nr]
        lane = lax.broadcasted_iota(jnp.int32, (nr, lanes), 1)
        lane_in = lax.rem(lane, jnp.int32(HEAD_DIM))
        real = lane_in < BLK_PER_CHUNK
        blk = jnp.where(real, lax.div(lane, jnp.int32(HEAD_DIM)) * BLK_PER_CHUNK + lane_in, (1 << 19) + lane)
        qrow = lax.broadcasted_iota(jnp.int32, (nr, lanes), 0) & (ds - 1)
        cur = lax.div(past + qrow, jnp.int32(SLC_LEN))
        valid = real & (blk <= cur)
        forced = (blk == 0) | (blk > cur - N_LOCAL)
        sel = _select_topk(score, valid, forced, blk, 1)
        for h in range(N_KV):
            for c in range(n_sel_chunks):
                sel_scr[h, c] = sel[h * ds:(h + 1) * ds, c * HEAD_DIM:(c + 1) * HEAD_DIM]

    for cp in page_copies(b, p, slot):
        cp.wait()

    k0 = p * CHUNK_ROWS
    for h in range(N_KV):
        qq = q_rows(h)
        kch = kv_buf[slot, h].astype(BF16)
        vch = kv_buf[slot, N_KV + h].astype(BF16)
        sc_ = lax.dot_general(qq, kch, _NT, preferred_element_type=F32) * SCALE
        selq = jnp.dot(sel_scr[h, p].astype(BF16), e_ref[...], preferred_element_type=F32)
        kpos = k0 + lax.broadcasted_iota(jnp.int32, (ds, CHUNK_ROWS), 1)
        qp = past + lax.broadcasted_iota(jnp.int32, (ds, CHUNK_ROWS), 0)
        okf = jnp.where((selq > 0.5) & (kpos <= qp), 1.0, 0.0)
        ok = jnp.concatenate([okf] * GROUP, axis=0) > 0.5
        online_update(h, jnp.where(ok, sc_, NEG), vch)

    @pl.when(p == n_steps - 1)
    def _():
        gt = jax.nn.sigmoid(g_ref[...])
        zpad = jnp.zeros((HEAD_DIM - ds, HEAD_DIM), F32)
        for h in range(N_KV):
            qq = q_rows(h)
            c_k = 2 * KV_W + h * HEAD_DIM
            c_v = 3 * KV_W + h * HEAD_DIM
            kn = jnp.concatenate([kvn_ref[:, c_k:c_k + HEAD_DIM], zpad], axis=0).astype(BF16)
            vn = jnp.concatenate([kvn_ref[:, c_v:c_v + HEAD_DIM], zpad], axis=0).astype(BF16)
            sn = lax.dot_general(qq, kn, _NT, preferred_element_type=F32) * SCALE
            last_sel = sel_scr[h, n_sel_chunks - 1]
            nb_last = (past // SLC_LEN) % BLK_PER_CHUNK
            seln = jnp.sum(jnp.where(lax.broadcasted_iota(jnp.int32, (ds, HEAD_DIM), 1) == nb_last,
                                     last_sel, 0.0), axis=1, keepdims=True)
            seln = jnp.concatenate([seln] * GROUP, axis=0)
            srow = lax.broadcasted_iota(jnp.int32, (rq, HEAD_DIM), 0) & (ds - 1)
            kcol = lax.broadcasted_iota(jnp.int32, (rq, HEAD_DIM), 1)
            okn = (seln > 0.5) & (kcol <= srow) & (kcol < ds)
            online_update(h, jnp.where(okn, sn, NEG), vn)
            o_slc = acc_scr[h] / l_scr[h]
            wb = st_ref.shape[0]
            band = wb + HEAD_DIM
            st_flat = st_ref.reshape(wb * 8, HEAD_DIM)
            kw = jnp.concatenate([st_flat[pl.ds(h, wb, stride=8), :],
                                  wn_ref[:, h * HEAD_DIM:(h + 1) * HEAD_DIM], zpad], axis=0).astype(BF16)
            vw = jnp.concatenate([st_flat[pl.ds(N_KV + h, wb, stride=8), :],
                                  wn_ref[:, KV_W + h * HEAD_DIM:KV_W + (h + 1) * HEAD_DIM], zpad],
                                 axis=0).astype(BF16)
            sw = lax.dot_general(qq, kw, _NT, preferred_element_type=F32) * SCALE
            srw = lax.broadcasted_iota(jnp.int32, (rq, band), 0) & (ds - 1)
            idx = lax.broadcasted_iota(jnp.int32, (rq, band), 1)
            dlt = (wb + srw) - idx
            sw = jnp.where((dlt >= 0) & (dlt < WINDOW) & (idx < wb + ds), sw, NEG)
            ew = jnp.exp(sw - jnp.max(sw, axis=1, keepdims=True))
            o_win = jnp.dot(ew.astype(BF16), vw, preferred_element_type=F32) / jnp.sum(ew, axis=1, keepdims=True)
            o_cmp = ocmp_scr[h]
            gh = gt[h]
            for g in range(GROUP):
                r0, r1 = g * ds, (g + 1) * ds
                og = gh[:, g:g + 1] * o_cmp[r0:r1] + gh[:, GROUP + g:GROUP + g + 1] * o_slc[r0:r1]
                og = og + gh[:, 2 * GROUP + g:2 * GROUP + g + 1] * o_win[r0:r1]
                c0 = (h * GROUP + g) * HEAD_DIM
                o_ref[:, c0:c0 + HEAD_DIM] = og
        wb = st_ref.shape[0]
        wso_ref[0:wb - ds] = st_ref[ds:wb]
        wso_flat = wso_ref.reshape(wb * 8, HEAD_DIM)
        for cb in range(2 * N_KV):
            wso_flat[pl.ds((wb - ds) * 8 + cb, ds, stride=8), :] = wn_ref[:, cb * HEAD_DIM:(cb + 1) * HEAD_DIM]


def _attn_sample(q, kcv, kv_new, win_new, win_state, g_re, cache4, page_table, *, ds):
    nb, npg = page_table.shape
    past = npg * PAGE
    assert ds & (ds - 1) == 0 and ds <= SLC_LEN and past % SLC_LEN == 0 and npg % PAGES_PER_STEP == 0
    steps = npg // PAGES_PER_STEP
    ncp = kcv.shape[3]
    wb = win_state.shape[1]
    n_blocks = past // SLC_LEN + 1
    n_sel_chunks = -(-n_blocks // BLK_PER_CHUNK)
    m = _overlap_matrix(ncp, n_sel_chunks * BLK_PER_CHUNK, ncp - 1, n_blocks)
    ms = np.zeros((ncp, n_sel_chunks, HEAD_DIM), np.float32)
    ms[:, :, :BLK_PER_CHUNK] = m.reshape(ncp, n_sel_chunks, BLK_PER_CHUNK)
    ms = jnp.asarray(ms.reshape(ncp, n_sel_chunks * HEAD_DIM), dtype=BF16)
    ee = np.zeros((HEAD_DIM, CHUNK_ROWS), np.float32)
    ee[:BLK_PER_CHUNK] = np.arange(CHUNK_ROWS)[None, :] // SLC_LEN == np.arange(BLK_PER_CHUNK)[:, None]
    ee = jnp.asarray(ee, dtype=BF16)
    rq = GROUP * ds

    grid_spec = pltpu.PrefetchScalarGridSpec(
        num_scalar_prefetch=1,
        grid=(nb, steps),
        in_specs=[
            pl.BlockSpec(memory_space=pl.ANY),
            pl.BlockSpec((ds, N_HEADS * HEAD_DIM), lambda b, p, pt: (b, 0)),
            pl.BlockSpec((None, 2, N_KV, ncp, HEAD_DIM), lambda b, p, pt: (b, 0, 0, 0, 0)),
            pl.BlockSpec((ds, 4 * KV_W), lambda b, p, pt: (b, 0)),
            pl.BlockSpec((ds, 2 * KV_W), lambda b, p, pt: (b, 0)),
            pl.BlockSpec((None, wb, 8, HEAD_DIM), lambda b, p, pt: (b, 0, 0, 0)),
            pl.BlockSpec((None, N_KV, ds, 4 * GROUP), lambda b, p, pt: (b, 0, 0, 0)),
            pl.BlockSpec(ms.shape, lambda b, p, pt: (0, 0)),
            pl.BlockSpec(ee.shape, lambda b, p, pt: (0, 0))],
        out_specs=[pl.BlockSpec((ds, N_HEADS * HEAD_DIM), lambda b, p, pt: (b, 0)),
                   pl.BlockSpec((None, wb, 8, HEAD_DIM), lambda b, p, pt: (b, 0, 0, 0))],
        scratch_shapes=[pltpu.VMEM((N_KV, rq, 1), F32), pltpu.VMEM((N_KV, rq, 1), F32),
                        pltpu.VMEM((N_KV, rq, HEAD_DIM), F32),
                        pltpu.VMEM((N_KV, n_sel_chunks, ds, HEAD_DIM), F32),
                        pltpu.VMEM((N_KV, rq, HEAD_DIM), F32),
                        pltpu.VMEM((2, 2 * N_KV, CHUNK_ROWS, HEAD_DIM), F32),
                        pltpu.SemaphoreType.DMA((2,))],
    )
    return pl.pallas_call(
        functools.partial(_attn_sample_kernel, past=past, ds=ds),
        grid_spec=grid_spec,
        out_shape=[jax.ShapeDtypeStruct((nb * ds, N_HEADS * HEAD_DIM), F32),
                   jax.ShapeDtypeStruct((nb, wb, 8, HEAD_DIM), F32)],
        compiler_params=_cparams(("arbitrary", "arbitrary")),
        name="attn_sample",
    )(page_table, cache4, q, kcv, kv_new, win_new, win_state, g_re, ms, ee)


_Q0, _KV0, _WIN0, _G0 = 0, 2048, 4096, 5120
_REST0 = _G0 + 3 * N_HEADS


def _split_w_in(w_in):
    wt = w_in.T
    return wt, wt[:_G0 + HEAD_DIM].astype(BF16)


def _gate_layout(g_logits, nb, seq):
    g = g_logits[:, :3 * N_HEADS].reshape(nb, seq, 3, N_KV, GROUP)
    g = g.transpose(0, 3, 1, 2, 4).reshape(nb, N_KV, seq, 3 * GROUP)
    return jnp.pad(g, ((0, 0), (0, 0), (0, 0), (0, GROUP)))


def _cmp_weights(w_k1, w_k2, pe_k, w_v1, w_v2, pe_v):
    def by_row(w1):
        w = w1.reshape(2, CMP_STRIDE, HEAD_DIM, HEAD_DIM).transpose(1, 2, 0, 3)
        return w.reshape(CMP_STRIDE * HEAD_DIM, 2 * HEAD_DIM)
    w1pairs = jnp.concatenate([by_row(w_k1), by_row(w_v1)], axis=1).astype(BF16)
    pe8 = jnp.stack([jnp.broadcast_to(pe_k.reshape(1, -1), (8, CMP_LEN * HEAD_DIM)),
                     jnp.broadcast_to(pe_v.reshape(1, -1), (8, CMP_LEN * HEAD_DIM))])
    return w1pairs, pe8, jnp.stack([w_k1, w_v1]), jnp.stack([w_k2, w_v2])


def kernel(x_prompt, x_sample, c_prompt, c_sample, cache_nsa_kv, page_table, state_win_kv, state_conv, w_ada, b_ada, norm1_g, norm2_g, w_in, w_cmp_k1, w_cmp_k2, pe_cmp_k, w_cmp_v1, w_cmp_v2, pe_cmp_v, conv_w, conv_b, w_attn_proj, w_conv_proj, w_out, w_mlp1, w_mlp2, normf_g):
    d = D_MODEL
    nbp, seq, _ = x_prompt.shape
    nbs, ds, _ = x_sample.shape
    tp = nbp * seq
    ts = nbs * ds
    tc = 256
    depth = w_in.shape[0]
    assert depth == 1

    xp = x_prompt.reshape(tp, d)
    xs = x_sample.reshape(ts, d)
    normf = normf_g.reshape(1, d)

    l = 0
    c_all = jnp.concatenate([c_prompt, c_sample, jnp.zeros((16 - nbp - nbs, d), F32)], axis=0)
    ada = _ada(c_all, w_ada[l], b_ada[l].reshape(1, -1)).reshape(16, 6, d)
    ada_p = [ada[:nbp, k][:, None, :] for k in range(6)]
    ada_s = [jnp.repeat(ada[nbp:nbp + nbs, k], ds, axis=0)[None] for k in range(6)]

    w_t, w_head = _split_w_in(w_in[l])
    gates0 = _REST0 + 3 * d
    wa = w_attn_proj[l]
    wc = w_conv_proj[l]
    wo = w_out[l]
    w1 = w_mlp1[l].astype(BF16)
    w2 = w_mlp2[l].astype(BF16)
    g1n = norm1_g[l].reshape(1, d)
    g2n = norm2_g[l].reshape(1, d)
    cw = conv_w[l]
    cb = conv_b[l].reshape(1, d)
    w1pairs, pe8, w1s, w2s = _cmp_weights(w_cmp_k1[l], w_cmp_k2[l], pe_cmp_k[l],
                                          w_cmp_v1[l], w_cmp_v2[l], pe_cmp_v[l])

    tm = 1024
    hp, q_p = _norm_mm(xp, g1n, ada_p[1], ada_p[0], w_head, tm=tm, tn=1024, seq=seq, out_dtype=BF16,
                       col0=_Q0, n=N_HEADS * HEAD_DIM, name="norm_mm_q")
    wbp = min(WINDOW, seq)
    kv6_p, kvb_p = _mm_kv(hp, w_head, tm=tm, col0=_KV0)
    winb_p, win6_p, g_p = _mm_win(hp, w_head, tm=wbp, seq=seq, col0=_WIN0, gcol0=_G0)
    gates_p = _mm_wstat(hp, w_t, tm=tm, tn=1024, out_dtype=BF16, act="sigmoid", name="mm_gates",
                        col0=gates0, n=2 * d)
    z_p, conv_tiles = _mm_conv(hp, w_t, cw, cb, tm=tm, tc=tc, seq=seq, col0=_REST0)

    pt_p = jnp.arange(tp // PAGE, dtype=jnp.int32).reshape(nbp, seq // PAGE)
    part_p = _cmp_part(kv6_p.reshape(tp // PAGE, PAGE, 16, HEAD_DIM), pt_p, w1pairs)
    kcv_p = _cmp_finish(part_p, pe8, w1s, w2s)
    o_p = _attn_prompt(q_p, kcv_p, kvb_p, winb_p, _gate_layout(g_p, nbp, seq), nb=nbp, seq=seq)

    mixed_p = _mix(o_p, z_p, wa, wc, gates_p, tm=tm, tn=512)
    x1_p, h2_p = _mm_resid_norm(mixed_p, wo, xp, ada_p[2], g2n, ada_p[4], ada_p[3], tm=512, seq=seq)
    y_p = _mlp_final(h2_p, w1, w2, x1_p, ada_p[5], normf, tm=512, tf=1024, seq=seq)

    hs, q_s = _norm_mm(xs, g1n, ada_s[1], ada_s[0], w_head, tm=ts, tn=512, seq=ds, out_dtype=F32,
                       col0=_Q0, n=N_HEADS * HEAD_DIM, name="norm_mm_q_s")
    kv_s = _mm(hs, w_head, tm=ts, tn=512, out_dtype=F32, name="mm_kv_s", col0=_KV0, n=4 * KV_W)
    win_s = _mm(hs, w_head, tm=ts, tn=512, out_dtype=F32, name="mm_win_s", col0=_WIN0, n=2 * KV_W)
    g_s = _mm(hs, w_head, tm=ts, tn=HEAD_DIM, out_dtype=F32, name="mm_g_s", col0=_G0, n=HEAD_DIM)
    gates_s = _mm_wstat(hs, w_t, tm=ts, tn=512, out_dtype=F32, act="sigmoid", name="mm_gates_s",
                        col0=gates0, n=2 * d)
    ubc_s = _mm_wstat(hs, w_t, tm=ts, tn=512, out_dtype=F32, name="mm_ubc_s", col0=_REST0, n=3 * d)
    st = state_conv[l]
    z_s, cu_s = _sample_conv(ubc_s, jnp.repeat(st[:, 0], ds, axis=0), jnp.repeat(st[:, 1], ds, axis=0),
                             cw, cb, tc=tc, seq=ds)

    cache4 = cache_nsa_kv[l].reshape(cache_nsa_kv.shape[1], PAGE, 16, HEAD_DIM)
    part_s = _cmp_part(cache4, page_table, w1pairs)
    kcv_s = _cmp_finish(part_s, pe8, w1s, w2s)
    wbs = state_win_kv.shape[2]
    wst = state_win_kv[l].reshape(nbs, wbs, 8, HEAD_DIM)
    o_s, wst_next = _attn_sample(q_s, kcv_s, kv_s, win_s, wst, _gate_layout(g_s, nbs, ds), cache4, page_table,
                                 ds=ds)

    mixed_s = _mix(o_s, z_s, wa, wc, gates_s, tm=ts, tn=512)
    x1_s, h2_s = _mm_resid_norm(mixed_s, wo, xs, ada_s[2], g2n, ada_s[4], ada_s[3], tm=ts, seq=ds)
    y_s = _mlp_final(h2_s, w1, w2, x1_s, ada_s[5], normf, tm=ts, tf=512, seq=ds)

    kv_prompt = kv6_p.reshape(1, nbp, seq, 4, N_KV, HEAD_DIM)
    kv_sample = kv_s.reshape(1, nbs, ds, 4, N_KV, HEAD_DIM)
    win_prompt = win6_p.reshape(1, nbp, wbp, 2, N_KV, HEAD_DIM)
    win_sample = wst_next.reshape(1, nbs, wbs, 2, N_KV, HEAD_DIM)
    tiles_per_seq = seq // tm
    conv_prompt = conv_tiles[tiles_per_seq - 1::tiles_per_seq][None]
    conv_sample = cu_s.reshape(nbs, ds, d)[None, :, ds - (CONV_W - 1):]
    return (y_p.reshape(nbp, seq, d), y_s.reshape(nbs, ds, d), kv_prompt, kv_sample,
            win_prompt, win_sample, conv_prompt, conv_sample)
```

```python
import functools

import numpy as np
import jax
import jax.numpy as jnp
from jax import lax
from jax.experimental import pallas as pl
from jax.experimental.pallas import tpu as pltpu

F32 = jnp.float32
BF16 = jnp.bfloat16

D_MODEL = 2048
HEAD_DIM = 128
N_HEADS = 16
N_KV = 4
GROUP = 4
KV_W = N_KV * HEAD_DIM
CMP_LEN = 32
CMP_STRIDE = 16
SLC_LEN = 64
SLC_TOP = 16
N_LOCAL = 2
WINDOW = 512
CONV_W = 3
RMS_EPS = 1e-6
NEG = -1e30
SCALE = HEAD_DIM ** -0.5
PAGE = 128
PAGES_PER_STEP = 16
CHUNK_ROWS = PAGES_PER_STEP * PAGE
BLK_PER_CHUNK = CHUNK_ROWS // SLC_LEN
VMEM_LIMIT = 56 * 1024 * 1024

_NT = (((1,), (1,)), ((), ()))
_TN = (((0,), (0,)), ((), ()))


def _cparams(sem):
    return pltpu.CompilerParams(dimension_semantics=sem, vmem_limit_bytes=VMEM_LIMIT)


def _ada_kernel(c_ref, w_ref, b_ref, o_ref):
    c = c_ref[...]
    a = (c * jax.nn.sigmoid(c)).astype(BF16)
    o_ref[...] = jnp.dot(a, w_ref[...].astype(BF16), preferred_element_type=F32) + b_ref[...]


def _ada(c, w, b, tn=1024):
    m, k = c.shape
    n = w.shape[1]
    return pl.pallas_call(
        _ada_kernel,
        grid=(n // tn,),
        in_specs=[pl.BlockSpec((m, k), lambda j: (0, 0)),
                  pl.BlockSpec((k, tn), lambda j: (0, j)),
                  pl.BlockSpec((1, tn), lambda j: (0, j))],
        out_specs=pl.BlockSpec((m, tn), lambda j: (0, j)),
        out_shape=jax.ShapeDtypeStruct((m, n), F32),
        compiler_params=_cparams(("arbitrary",)),
        name="ada",
    )(c, w, b)


def _rowspec(p, tm, tn, seq, col=True):
    gr = p.shape[1]
    if gr == 1:
        if col:
            return pl.BlockSpec((1, 1, tn), lambda i, j: ((i * tm) // seq, 0, j))
        return pl.BlockSpec((1, 1, tn), lambda i: ((i * tm) // seq, 0, 0))
    if col:
        return pl.BlockSpec((1, tm, tn), lambda i, j: (0, i, j))
    return pl.BlockSpec((1, tm, tn), lambda i: (0, i, 0))


def _xwt(x_ref, wt_ref):
    return lax.dot_general(x_ref[...].astype(BF16), wt_ref[...], _NT, preferred_element_type=F32)


def _mm_kernel(x_ref, w_ref, o_ref, *, act):
    acc = _xwt(x_ref, w_ref)
    if act == "sigmoid":
        acc = jax.nn.sigmoid(acc)
    o_ref[...] = acc.astype(o_ref.dtype)


def _norm_mm_kernel(x_ref, g_ref, sc_ref, sh_ref, w_ref, h_ref, o_ref):
    @pl.when(pl.program_id(1) == 0)
    def _():
        x = x_ref[...]
        r = lax.rsqrt(jnp.mean(x * x, axis=-1, keepdims=True) + RMS_EPS)
        y = (x * r) * g_ref[...]
        h_ref[...] = (y * (1.0 + sc_ref[0]) + sh_ref[0]).astype(h_ref.dtype)

    o_ref[...] = _xwt(h_ref, w_ref).astype(o_ref.dtype)


def _norm_mm(x, g, sc, sh, w, *, tm, tn, seq, out_dtype, col0, n, name):
    t, d = x.shape
    assert col0 % tn == 0 and n % tn == 0
    cb0 = col0 // tn
    row = lambda p: (pl.BlockSpec((1, 1, d), lambda i, j: ((i * tm) // seq, 0, 0)) if p.shape[1] == 1
                     else pl.BlockSpec((1, tm, d), lambda i, j: (0, i, 0)))
    return pl.pallas_call(
        _norm_mm_kernel,
        grid=(t // tm, n // tn),
        in_specs=[pl.BlockSpec((tm, d), lambda i, j: (i, 0)),
                  pl.BlockSpec((1, d), lambda i, j: (0, 0)),
                  row(sc), row(sh),
                  pl.BlockSpec((tn, d), lambda i, j: (cb0 + j, 0))],
        out_specs=[pl.BlockSpec((tm, d), lambda i, j: (i, 0)),
                   pl.BlockSpec((tm, tn), lambda i, j: (i, j))],
        out_shape=[jax.ShapeDtypeStruct((t, d), BF16), jax.ShapeDtypeStruct((t, n), out_dtype)],
        compiler_params=_cparams(("arbitrary", "arbitrary")),
        name=name,
    )(x, g, sc, sh, w)


def _mm_wstat_kernel(x_ref, w_ref, o_ref, wbf_ref, *, act):
    @pl.when(pl.program_id(1) == 0)
    def _():
        wbf_ref[...] = w_ref[...].astype(BF16)

    acc = _xwt(x_ref, wbf_ref)
    if act == "sigmoid":
        acc = jax.nn.sigmoid(acc)
    o_ref[...] = acc.astype(o_ref.dtype)


def _wrows(tn, k, row0):
    return pl.BlockSpec((pl.Element(tn), pl.Element(k)), lambda j, i: (pl.multiple_of(row0 + j * tn, 8), 0))


def _mm_wstat(x, w, *, tm, tn, out_dtype, act=None, name="mm", col0, n):
    t, k = x.shape
    assert col0 % 8 == 0 and n % tn == 0
    return pl.pallas_call(
        functools.partial(_mm_wstat_kernel, act=act),
        grid=(n // tn, t // tm),
        in_specs=[pl.BlockSpec((tm, k), lambda j, i: (i, 0)),
                  _wrows(tn, k, col0)],
        out_specs=pl.BlockSpec((tm, tn), lambda j, i: (i, j)),
        out_shape=jax.ShapeDtypeStruct((t, n), out_dtype),
        scratch_shapes=[pltpu.VMEM((tn, k), BF16)],
        compiler_params=_cparams(("arbitrary", "arbitrary")),
        name=name,
    )(x, w)


def _mm(x, w, *, tm, tn, out_dtype, act=None, name="mm", col0, n):
    t, k = x.shape
    assert col0 % tn == 0 and n % tn == 0
    cb0 = col0 // tn
    return pl.pallas_call(
        functools.partial(_mm_kernel, act=act),
        grid=(t // tm, n // tn),
        in_specs=[pl.BlockSpec((tm, k), lambda i, j: (i, 0)),
                  pl.BlockSpec((tn, k), lambda i, j: (cb0 + j, 0))],
        out_specs=pl.BlockSpec((tm, tn), lambda i, j: (i, j)),
        out_shape=jax.ShapeDtypeStruct((t, n), out_dtype),
        compiler_params=_cparams(("arbitrary", "arbitrary")),
        name=name,
    )(x, w)


def _store_head_major(o_ref, acc, tm):
    flat = o_ref.reshape(tm * 8, HEAD_DIM)
    for cb in range(8):
        flat[pl.ds(cb, tm, stride=8), :] = acc[:, cb * HEAD_DIM:(cb + 1) * HEAD_DIM]


def _mm_kv_kernel(x_ref, w_ref, o6_ref, ob_ref, *, tm):
    acc = _xwt(x_ref, w_ref)
    _store_head_major(o6_ref, acc, tm)

    @pl.when(pl.program_id(1) == 1)
    def _():
        ob_ref[...] = acc.astype(BF16)


def _mm_kv(h, w, *, tm, col0):
    t, k = h.shape
    tn = 2 * KV_W
    cb0 = col0 // tn
    return pl.pallas_call(
        functools.partial(_mm_kv_kernel, tm=tm),
        grid=(t // tm, 2),
        in_specs=[pl.BlockSpec((tm, k), lambda i, j: (i, 0)),
                  pl.BlockSpec((tn, k), lambda i, j: (cb0 + j, 0))],
        out_specs=[pl.BlockSpec((tm, 8, HEAD_DIM), lambda i, j: (i, j, 0)),
                   pl.BlockSpec((tm, tn), lambda i, j: (i, 0))],
        out_shape=[jax.ShapeDtypeStruct((t, 16, HEAD_DIM), F32),
                   jax.ShapeDtypeStruct((t, tn), BF16)],
        compiler_params=_cparams(("arbitrary", "arbitrary")),
        name="mm_kv",
    )(h, w)


def _mm_win_kernel(x_ref, w_ref, wg_ref, ob_ref, o6_ref, og_ref, *, tm, seq):
    acc = _xwt(x_ref, w_ref)
    ob_ref[...] = acc.astype(BF16)
    gl = _xwt(x_ref, wg_ref)
    for hd in range(N_KV):
        og_ref[hd] = jnp.concatenate(
            [gl[:, br * N_HEADS + hd * GROUP:br * N_HEADS + (hd + 1) * GROUP] for br in range(3)]
            + [jnp.zeros((tm, GROUP), F32)], axis=1)

    @pl.when(((pl.program_id(0) + 1) * tm) % seq == 0)
    def _():
        _store_head_major(o6_ref, acc, tm)


def _mm_win(h, w, *, tm, seq, col0, gcol0):
    t, k = h.shape
    tn = 2 * KV_W
    cb0 = col0 // tn
    gb0 = gcol0 // HEAD_DIM
    return pl.pallas_call(
        functools.partial(_mm_win_kernel, tm=tm, seq=seq),
        grid=(t // tm,),
        in_specs=[pl.BlockSpec((tm, k), lambda i: (i, 0)),
                  pl.BlockSpec((tn, k), lambda i: (cb0, 0)),
                  pl.BlockSpec((HEAD_DIM, k), lambda i: (gb0, 0))],
        out_specs=[pl.BlockSpec((tm, tn), lambda i: (i, 0)),
                   pl.BlockSpec((tm, 8, HEAD_DIM), lambda i: ((i * tm) // seq, 0, 0)),
                   pl.BlockSpec((None, N_KV, tm, 4 * GROUP), lambda i: ((i * tm) // seq, 0, (i * tm % seq) // tm, 0))],
        out_shape=[jax.ShapeDtypeStruct((t, tn), BF16),
                   jax.ShapeDtypeStruct((t // seq * tm, 8, HEAD_DIM), F32),
                   jax.ShapeDtypeStruct((t // seq, N_KV, seq, 4 * GROUP), F32)],
        compiler_params=_cparams(("arbitrary",)),
        name="mm_win",
    )(h, w, w)


def _conv_taps(cu, prev1, prev2, bgate, cw_ref, cb_ref):
    y = cb_ref[...] + prev2 * cw_ref[0:1, :]
    y = y + prev1 * cw_ref[1:2, :]
    y = y + cu * cw_ref[2:3, :]
    return bgate * y


def _mm_conv_kernel(x_ref, wu_ref, wb_ref, wc_ref, cw_ref, cb_ref, ca_ref, cb2_ref, z_ref, st_ref, oa_ref, ob_ref,
                    w3_ref, carry_ref, *, tm, tc, seq):
    i = pl.program_id(1)
    oa_ref[...] = ca_ref[...].astype(BF16)
    ob_ref[...] = cb2_ref[...].astype(BF16)

    @pl.when(i == 0)
    def _():
        w3_ref[0:tc] = wu_ref[...].astype(BF16)
        w3_ref[tc:2 * tc] = wb_ref[...].astype(BF16)
        w3_ref[2 * tc:3 * tc] = wc_ref[...].astype(BF16)

    acc = _xwt(x_ref, w3_ref)
    u = acc[:, 0:tc]
    bgate = acc[:, tc:2 * tc]
    cu = acc[:, 2 * tc:3 * tc] * u

    @pl.when((i * tm) % seq == 0)
    def _():
        carry_ref[...] = jnp.zeros((8, tc), F32)

    car = carry_ref[...]
    p0 = car[0:1, :]
    p1 = car[1:2, :]
    rows = lax.broadcasted_iota(jnp.int32, (tm, tc), 0)
    r1 = pltpu.roll(cu, 1, 0)
    r2 = pltpu.roll(cu, 2, 0)
    prev1 = jnp.where(rows == 0, p1, r1)
    prev2 = jnp.where(rows == 0, p0, jnp.where(rows == 1, p1, r2))
    z_ref[...] = _conv_taps(cu, prev1, prev2, bgate, cw_ref, cb_ref).astype(z_ref.dtype)
    tail = cu[tm - 8:tm, :]
    carry_ref[...] = jnp.concatenate([tail[6:8, :], tail[0:6, :]], axis=0)
    st_ref[0] = tail[6:8, :]


def _mm_conv(h, w, conv_w, conv_b, cast_a, cast_b, *, tm, tc, seq, col0):
    t, k = h.shape
    d = conv_w.shape[1]
    nj = d // tc
    ni = t // tm
    steps = nj * ni
    ra, rb_ = cast_a.shape[0] // steps, cast_b.shape[0] // steps
    assert ra % 16 == 0 and rb_ % 16 == 0 and ra * steps == cast_a.shape[0] and rb_ * steps == cast_b.shape[0]
    slab = lambda r, m: pl.BlockSpec((r, m.shape[1]), lambda j, i: (j * ni + i, 0))
    return pl.pallas_call(
        functools.partial(_mm_conv_kernel, tm=tm, tc=tc, seq=seq),
        grid=(nj, ni),
        in_specs=[pl.BlockSpec((tm, k), lambda j, i: (i, 0)),
                  _wrows(tc, k, col0), _wrows(tc, k, col0 + d), _wrows(tc, k, col0 + 2 * d),
                  pl.BlockSpec((CONV_W, tc), lambda j, i: (0, j)),
                  pl.BlockSpec((1, tc), lambda j, i: (0, j)),
                  slab(ra, cast_a), slab(rb_, cast_b)],
        out_specs=[pl.BlockSpec((tm, tc), lambda j, i: (i, j)),
                   pl.BlockSpec((1, CONV_W - 1, tc), lambda j, i: (i, 0, j)),
                   slab(ra, cast_a), slab(rb_, cast_b)],
        out_shape=[jax.ShapeDtypeStruct((t, d), BF16),
                   jax.ShapeDtypeStruct((t // tm, CONV_W - 1, d), F32),
                   jax.ShapeDtypeStruct(cast_a.shape, BF16),
                   jax.ShapeDtypeStruct(cast_b.shape, BF16)],
        scratch_shapes=[pltpu.VMEM((3 * tc, k), BF16), pltpu.VMEM((8, tc), F32)],
        compiler_params=_cparams(("arbitrary", "arbitrary")),
        name="mm_conv",
    )(h, w, w, w, conv_w, conv_b, cast_a, cast_b)


def _sample_conv_kernel(u_ref, b_ref, c_ref, st0_ref, st1_ref, cw_ref, cb_ref, z_ref, cu_ref, *, tc, seq):
    t = u_ref.shape[0]
    bgate = b_ref[...]
    cu = c_ref[...] * u_ref[...]
    s = lax.rem(lax.broadcasted_iota(jnp.int32, (t, tc), 0), jnp.int32(seq))
    r1 = pltpu.roll(cu, 1, 0)
    r2 = pltpu.roll(cu, 2, 0)
    prev1 = jnp.where(s == 0, st1_ref[...], r1)
    prev2 = jnp.where(s == 0, st0_ref[...], jnp.where(s == 1, st1_ref[...], r2))
    z_ref[...] = _conv_taps(cu, prev1, prev2, bgate, cw_ref, cb_ref)
    cu_ref[...] = cu


def _sample_conv(ubc, st0, st1, conv_w, conv_b, *, tc, seq):
    t = ubc.shape[0]
    d = ubc.shape[1] // 3
    nj = d // tc
    return pl.pallas_call(
        functools.partial(_sample_conv_kernel, tc=tc, seq=seq),
        grid=(nj,),
        in_specs=[pl.BlockSpec((t, tc), lambda j: (0, j)),
                  pl.BlockSpec((t, tc), lambda j: (0, nj + j)),
                  pl.BlockSpec((t, tc), lambda j: (0, 2 * nj + j)),
                  pl.BlockSpec((t, tc), lambda j: (0, j)),
                  pl.BlockSpec((t, tc), lambda j: (0, j)),
                  pl.BlockSpec((CONV_W, tc), lambda j: (0, j)),
                  pl.BlockSpec((1, tc), lambda j: (0, j))],
        out_specs=[pl.BlockSpec((t, tc), lambda j: (0, j)),
                   pl.BlockSpec((t, tc), lambda j: (0, j))],
        out_shape=[jax.ShapeDtypeStruct((t, d), F32), jax.ShapeDtypeStruct((t, d), F32)],
        compiler_params=_cparams(("arbitrary",)),
        name="sample_conv",
    )(ubc, ubc, ubc, st0, st1, conv_w, conv_b)


def _mix_kernel(o_ref, z_ref, wa_ref, wc_ref, ga_ref, gc_ref, m_ref):
    a = jnp.dot(o_ref[...].astype(BF16), wa_ref[...].astype(BF16), preferred_element_type=F32)
    c = jnp.dot(z_ref[...].astype(BF16), wc_ref[...].astype(BF16), preferred_element_type=F32)
    m_ref[...] = (ga_ref[...].astype(F32) * a + gc_ref[...].astype(F32) * c).astype(m_ref.dtype)


def _mix(o, z, wa, wc, gates, *, tm, tn):
    t, k = o.shape
    n = wa.shape[1]
    nj = n // tn
    return pl.pallas_call(
        _mix_kernel,
        grid=(t // tm, nj),
        in_specs=[pl.BlockSpec((tm, k), lambda i, j: (i, 0)),
                  pl.BlockSpec((tm, k), lambda i, j: (i, 0)),
                  pl.BlockSpec((k, tn), lambda i, j: (0, j)),
                  pl.BlockSpec((k, tn), lambda i, j: (0, j)),
                  pl.BlockSpec((tm, tn), lambda i, j: (i, j)),
                  pl.BlockSpec((tm, tn), lambda i, j: (i, j + nj))],
        out_specs=pl.BlockSpec((tm, tn), lambda i, j: (i, j)),
        out_shape=jax.ShapeDtypeStruct((t, n), BF16),
        compiler_params=_cparams(("arbitrary", "arbitrary")),
        name="mix",
    )(o, z, wa, wc, gates, gates)


def _resid_norm_kernel(a_ref, w_ref, x_ref, g_ref, ng_ref, sc_ref, sh_ref, x1_ref, h2_ref, wbf_ref):
    @pl.when(pl.program_id(0) == 0)
    def _():
        wbf_ref[...] = w_ref[...].astype(BF16)

    acc = jnp.dot(a_ref[...], wbf_ref[...], preferred_element_type=F32)
    x1 = x_ref[...] + g_ref[0] * acc
    x1_ref[...] = x1
    r = lax.rsqrt(jnp.mean(x1 * x1, axis=-1, keepdims=True) + RMS_EPS)
    y = (x1 * r) * ng_ref[...]
    h2_ref[...] = (y * (1.0 + sc_ref[0]) + sh_ref[0]).astype(h2_ref.dtype)


def _mm_resid_norm(a, w, x, gate, ng, sc, sh, *, tm, seq):
    t, k = a.shape
    n = w.shape[1]
    row = lambda p: _rowspec(p, tm, n, seq, col=False)
    return pl.pallas_call(
        _resid_norm_kernel,
        grid=(t // tm,),
        in_specs=[pl.BlockSpec((tm, k), lambda i: (i, 0)),
                  pl.BlockSpec((k, n), lambda i: (0, 0), pipeline_mode=pl.Buffered(1)),
                  pl.BlockSpec((tm, n), lambda i: (i, 0)),
                  row(gate),
                  pl.BlockSpec((1, n), lambda i: (0, 0)),
                  row(sc), row(sh)],
        out_specs=[pl.BlockSpec((tm, n), lambda i: (i, 0)), pl.BlockSpec((tm, n), lambda i: (i, 0))],
        out_shape=[jax.ShapeDtypeStruct((t, n), F32), jax.ShapeDtypeStruct((t, n), BF16)],
        scratch_shapes=[pltpu.VMEM((k, n), BF16)],
        compiler_params=_cparams(("arbitrary",)),
        name="mm_resid_norm",
    )(a, w, x, gate, ng, sc, sh)


def _mlp_kernel(h_ref, w1_ref, w2_ref, x_ref, g_ref, nf_ref, o_ref, acc_ref):
    k = pl.program_id(1)

    @pl.when(k == 0)
    def _():
        acc_ref[...] = jnp.zeros_like(acc_ref)

    a = jnp.dot(h_ref[...], w1_ref[...], preferred_element_type=F32)
    a = jnp.square(jnp.maximum(a, 0.0)).astype(BF16)
    acc_ref[...] += jnp.dot(a, w2_ref[...], preferred_element_type=F32)

    @pl.when(k == pl.num_programs(1) - 1)
    def _():
        x2 = x_ref[...] + g_ref[0] * acc_ref[...]
        r = lax.rsqrt(jnp.mean(x2 * x2, axis=-1, keepdims=True) + RMS_EPS)
        o_ref[...] = (x2 * r) * nf_ref[...]


def _mlp_final(h2, w1, w2, x1, gate, normf, *, tm, tf, seq):
    t, d = h2.shape
    f = w1.shape[1]
    return pl.pallas_call(
        _mlp_kernel,
        grid=(t // tm, f // tf),
        in_specs=[pl.BlockSpec((tm, d), lambda i, k: (i, 0)),
                  pl.BlockSpec((d, tf), lambda i, k: (0, k)),
                  pl.BlockSpec((tf, d), lambda i, k: (k, 0)),
                  pl.BlockSpec((tm, d), lambda i, k: (i, 0)),
                  _rowspec_k(gate, tm, d, seq),
                  pl.BlockSpec((1, d), lambda i, k: (0, 0))],
        out_specs=pl.BlockSpec((tm, d), lambda i, k: (i, 0)),
        out_shape=jax.ShapeDtypeStruct((t, d), F32),
        scratch_shapes=[pltpu.VMEM((tm, d), F32)],
        compiler_params=_cparams(("arbitrary", "arbitrary")),
        name="mlp",
    )(h2, w1, w2, x1, gate, normf)


def _rowspec_k(p, tm, d, seq):
    if p.shape[1] == 1:
        return pl.BlockSpec((1, 1, d), lambda i, k: ((i * tm) // seq, 0, 0))
    return pl.BlockSpec((1, tm, d), lambda i, k: (0, i, 0))


def _cmp_part_kernel(pt_ref, *refs):
    pages = refs[:PAGES_PER_STEP]
    w_ref = refs[PAGES_PER_STEP]
    o_ref = refs[PAGES_PER_STEP + 1]
    stage = refs[PAGES_PER_STEP + 2]
    nchunk = CHUNK_ROWS // CMP_STRIDE
    cpp = PAGE // CMP_STRIDE
    nslab = 2 * N_KV
    cols = [jnp.concatenate([pg[pl.ds(r, cpp, stride=CMP_STRIDE)].reshape(cpp * nslab, HEAD_DIM)
                             for pg in pages], axis=0).astype(BF16) for r in range(CMP_STRIDE)]
    lhs = jnp.concatenate(cols, axis=1)
    acc = jnp.dot(lhs, w_ref[...], preferred_element_type=F32)
    slab = lax.broadcasted_iota(jnp.int32, (nchunk * nslab, 2 * HEAD_DIM), 0) & (nslab - 1)
    sel = jnp.where(slab < N_KV, acc[:, :2 * HEAD_DIM], acc[:, 2 * HEAD_DIM:])
    stage[0] = sel[:, :HEAD_DIM]
    stage[1] = sel[:, HEAD_DIM:]
    for cb in range(nslab):
        o_ref[cb // N_KV, cb % N_KV] = jnp.concatenate(
            [stage[0, pl.ds(cb, nchunk, stride=nslab), :], stage[1, pl.ds(cb, nchunk, stride=nslab), :]], axis=1)


def _page_spec(n, kind_pair):
    return pl.BlockSpec((None, PAGE, 8, HEAD_DIM),
                        lambda b, p, pt, n=n: (pt[b, p * PAGES_PER_STEP + n], 0, kind_pair, 0))


def _cmp_part(pages4, page_table, w1pairs):
    nb, npg = page_table.shape
    steps = npg // PAGES_PER_STEP
    nchunk = PAGES_PER_STEP * (PAGE // CMP_STRIDE)
    grid_spec = pltpu.PrefetchScalarGridSpec(
        num_scalar_prefetch=1,
        grid=(nb, steps),
        in_specs=[_page_spec(n, 0) for n in range(PAGES_PER_STEP)]
        + [pl.BlockSpec((CMP_STRIDE * HEAD_DIM, 4 * HEAD_DIM), lambda b, p, pt: (0, 0))],
        out_specs=pl.BlockSpec((None, 2, N_KV, nchunk, 2 * HEAD_DIM), lambda b, p, pt: (b, 0, 0, p, 0)),
        scratch_shapes=[pltpu.VMEM((2, nchunk * 2 * N_KV, HEAD_DIM), F32)],
    )
    return pl.pallas_call(
        _cmp_part_kernel,
        grid_spec=grid_spec,
        out_shape=jax.ShapeDtypeStruct((nb, 2, N_KV, steps * nchunk, 2 * HEAD_DIM), F32),
        compiler_params=_cparams(("arbitrary", "arbitrary")),
        name="cmp_part",
    )(page_table, *([pages4] * PAGES_PER_STEP), w1pairs)


def _gelu_tanh(x):
    c = np.sqrt(2.0 / np.pi).astype(np.float32)
    return 0.5 * x * (1.0 + jnp.tanh(c * (x + 0.044715 * (x * x * x))))


def _cmp_finish_kernel(part_ref, pe_ref, w1_ref, w2_ref, o_ref, bias_ref):
    @pl.when(pl.program_id(1) == 0)
    def _():
        bias_ref[...] = jnp.dot(pe_ref[...], w1_ref[...], preferred_element_type=F32,
                                precision=lax.Precision.HIGHEST)

    w2 = w2_ref[...].astype(BF16)
    for hd in range(N_KV):
        part = part_ref[hd]
        n = part.shape[0]
        h = part[:, 0:HEAD_DIM] + pltpu.roll(part[:, HEAD_DIM:], n - 1, 0)
        h = h + bias_ref[0:1, :]
        o_ref[hd] = jnp.dot(_gelu_tanh(h).astype(BF16), w2, preferred_element_type=F32).astype(o_ref.dtype)


def _cmp_finish(part, pe8, w1, w2):
    nb, _, _, nchunk, _ = part.shape
    return pl.pallas_call(
        _cmp_finish_kernel,
        grid=(2, nb),
        in_specs=[pl.BlockSpec((None, None, N_KV, nchunk, 2 * HEAD_DIM), lambda k, b: (b, k, 0, 0, 0)),
                  pl.BlockSpec((None, 8, CMP_LEN * HEAD_DIM), lambda k, b: (k, 0, 0)),
                  pl.BlockSpec((None, CMP_LEN * HEAD_DIM, HEAD_DIM), lambda k, b: (k, 0, 0)),
                  pl.BlockSpec((None, HEAD_DIM, HEAD_DIM), lambda k, b: (k, 0, 0))],
        out_specs=pl.BlockSpec((None, None, N_KV, nchunk, HEAD_DIM), lambda k, b: (b, k, 0, 0, 0)),
        out_shape=jax.ShapeDtypeStruct((nb, 2, N_KV, nchunk, HEAD_DIM), BF16),
        scratch_shapes=[pltpu.VMEM((8, HEAD_DIM), F32)],
        compiler_params=_cparams(("arbitrary", "arbitrary")),
        name="cmp_finish",
    )(part, pe8, w1, w2)


def _select_topk(score, valid, forced, blk, axis):
    sc = jnp.where(valid, jnp.where(forced, jnp.inf, score), -jnp.inf)
    sel = jnp.zeros(score.shape, F32)
    big = jnp.int32(1 << 20)
    for _ in range(SLC_TOP):
        mx = jnp.max(sc, axis=axis, keepdims=True)
        idx = jnp.min(jnp.where(sc == mx, blk, big), axis=axis, keepdims=True)
        hit = blk == idx
        sel = jnp.where(hit & (mx > -jnp.inf), 1.0, sel)
        sc = jnp.where(hit, -jnp.inf, sc)
    return sel


def _overlap_matrix(nc_pad, n_blocks_pad, nc, n_blocks):
    cs = np.arange(nc_pad) * CMP_STRIDE
    sb = np.arange(n_blocks_pad) * SLC_LEN
    ov = np.clip(np.minimum(cs[:, None] + CMP_LEN, sb[None, :] + SLC_LEN)
                 - np.maximum(cs[:, None], sb[None, :]), 0, None)
    m = (ov / CMP_STRIDE).astype(np.float32)
    m[nc:, :] = 0.0
    m[:, n_blocks:] = 0.0
    return m


def _attn_prompt_kernel(q_ref, kcb, vcb, ksb, vsb, kwb, vwb, g_ref, mt_ref, e_ref, o_ref,
                        m_scr, l_scr, acc_scr, bias_scr, gate_scr, *, tq, seq, ck):
    qi = pl.program_id(2)
    nc = seq // CMP_STRIDE - 1
    nsb = seq // SLC_LEN
    rb = min(tq, 128)

    q = q_ref[...]
    qq = jnp.concatenate([q[:, g * HEAD_DIM:(g + 1) * HEAD_DIM] for g in range(GROUP)], axis=0)
    rq = GROUP * tq
    q0 = qi * tq

    gt = jax.nn.sigmoid(g_ref[...])
    ngc = 4 * GROUP
    hi = gt.astype(BF16)
    r1 = gt - hi.astype(F32)
    mid = r1.astype(BF16)
    lo = (r1 - mid.astype(F32)).astype(BF16)
    g3 = jnp.concatenate([hi, mid, lo, jnp.zeros((tq, HEAD_DIM - 3 * ngc), BF16)], axis=1)
    srow = lax.broadcasted_iota(jnp.int32, (HEAD_DIM, 3 * GROUP * HEAD_DIM), 0)
    scol = lax.broadcasted_iota(jnp.int32, (HEAD_DIM, 3 * GROUP * HEAD_DIM), 1)
    spread = ((srow < 3 * ngc) & ((srow & (ngc - 1)) == lax.div(scol, jnp.int32(HEAD_DIM)))).astype(BF16)
    gl = jnp.dot(g3, spread, preferred_element_type=F32)
    for c in range(3 * GROUP):
        gate_scr[c] = gl[:, c * HEAD_DIM:(c + 1) * HEAD_DIM]

    ncp = kcb.shape[0]
    s = lax.dot_general(qq, kcb[...], _NT, preferred_element_type=F32) * SCALE
    qpos = q0 + (lax.broadcasted_iota(jnp.int32, (rq, ncp), 0) & (tq - 1))
    col = lax.broadcasted_iota(jnp.int32, (rq, ncp), 1)
    vis = (col * CMP_STRIDE + (CMP_LEN - 1) <= qpos) & (col < nc)
    s = jnp.where(vis, s, NEG)
    mx = jnp.max(s, axis=1, keepdims=True)
    e = jnp.where(vis, jnp.exp(s - mx), 0.0)
    den = jnp.sum(e, axis=1, keepdims=True)
    p = e / jnp.where(den > 0.0, den, 1.0)
    o_cmp = jnp.dot(p.astype(BF16), vcb[...], preferred_element_type=F32)
    pg = p[0:tq]
    for g in range(1, GROUP):
        pg = pg + p[g * tq:(g + 1) * tq]

    sct = lax.dot_general(mt_ref[...], pg, _NT, preferred_element_type=F32,
                          precision=lax.Precision.HIGHEST)
    blk = lax.broadcasted_iota(jnp.int32, (nsb, tq), 0)
    cur = lax.div(q0 + lax.broadcasted_iota(jnp.int32, (nsb, tq), 1), jnp.int32(SLC_LEN))
    valid = blk <= cur
    forced = (blk == 0) | (blk > cur - N_LOCAL)
    need_topk = (q0 + tq - 1) // SLC_LEN + 1 > SLC_TOP
    sel_t = lax.cond(need_topk,
                     lambda: _select_topk(sct, valid, forced, blk, 0),
                     lambda: valid.astype(F32)).astype(BF16)

    def sweep(k_ref, v_ref, k_start, n_chunks, bias_fn):
        m_scr[...] = jnp.full((rq, HEAD_DIM), NEG, F32)
        l_scr[...] = jnp.zeros((rq, HEAD_DIM), F32)
        acc_scr[...] = jnp.zeros((rq, HEAD_DIM), F32)
        nl = ck // HEAD_DIM

        def scores(c):
            k0 = pl.multiple_of(k_start + c * ck, 128)
            return lax.dot_general(qq, k_ref[pl.ds(k0, ck), :], _NT, preferred_element_type=F32)

        def chunk_inputs(c):
            k0 = pl.multiple_of(k_start + c * ck, 128)
            return scores(c), bias_fn(c, k0)

        c_exp = np.float32(SCALE * np.log2(np.e))

        def reduce_chunk(c, sb):
            sc_, bias = sb
            k0 = pl.multiple_of(k_start + c * ck, 128)
            vch = jnp.concatenate([v_ref[pl.ds(k0, ck), :], jnp.ones((ck, HEAD_DIM), BF16)], axis=1)
            alphas, ps = [], []
            nblk = rq // rb
            for r in range(nblk):
                rows = slice(r * rb, (r + 1) * rb)
                b0 = (r * rb) % tq
                sg = [sc_[rows, j * HEAD_DIM:(j + 1) * HEAD_DIM]
                      + bias[b0:b0 + rb, j * HEAD_DIM:(j + 1) * HEAD_DIM] for j in range(nl)]
                mx = functools.reduce(jnp.maximum, sg)
                m_old = m_scr[rows]
                m_new = jnp.maximum(m_old, jnp.max(mx, axis=1, keepdims=True))
                alpha = jnp.exp2((m_old - m_new) * c_exp)
                pj = [jnp.exp2((x - m_new) * c_exp) for x in sg]
                m_scr[rows] = m_new
                alphas.append(alpha)
                ps.append(jnp.concatenate([x.astype(BF16) for x in pj], axis=1))
                if (r + 1) % (nblk // 2) == 0:
                    half = slice((r + 1 - nblk // 2) * rb, (r + 1) * rb)
                    pv = jnp.dot(jnp.concatenate(ps, axis=0), vch, preferred_element_type=F32)
                    al = jnp.concatenate(alphas, axis=0)
                    acc_scr[half] = al * acc_scr[half] + pv[:, :HEAD_DIM]
                    l_scr[half] = al * l_scr[half] + pv[:, HEAD_DIM:]
                    alphas, ps = [], []

        def body(c, carry):
            reduce_chunk(c, chunk_inputs(c))
            return carry

        lax.fori_loop(0, n_chunks, body, 0)
        return acc_scr[...] / l_scr[...]

    qp = q0 + lax.broadcasted_iota(jnp.int32, (tq, ck), 0)
    kcol = lax.broadcasted_iota(jnp.int32, (tq, ck), 1)

    eye = (lax.broadcasted_iota(jnp.int32, (nsb, HEAD_DIM), 0)
           == lax.broadcasted_iota(jnp.int32, (nsb, HEAD_DIM), 1)).astype(BF16)
    sel_q = lax.dot_general(sel_t, eye, _TN, preferred_element_type=F32).astype(BF16)

    n_slc = (q0 + tq + ck - 1) // ck

    def put_mask(c):
        selq = jnp.dot(sel_q, e_ref[c], preferred_element_type=F32)
        bias_scr[c] = jnp.where((selq > 0.5) & (c * ck + kcol <= qp), 0.0, NEG)

    def mask_body(i, carry):
        put_mask(2 * i)
        put_mask(jnp.minimum(2 * i + 1, seq // ck - 1))
        return carry

    lax.fori_loop(0, (n_slc + 1) // 2, mask_body, 0)
    o_slc = sweep(ksb, vsb, 0, n_slc, lambda c, k0: bias_scr[c])

    span = -(-(WINDOW + tq) // ck) * ck

    def win_bias(c, k0):
        dlt = qp - (k0 + kcol)
        return jnp.where((dlt >= 0) & (dlt < WINDOW), 0.0, NEG)

    o_win = sweep(kwb, vwb, jnp.maximum(q0 + tq - span, 0), span // ck, win_bias)

    for g in range(GROUP):
        r0, r1 = g * tq, (g + 1) * tq
        og = gate_scr[g] * o_cmp[r0:r1] + gate_scr[GROUP + g] * o_slc[r0:r1]
        og = og + gate_scr[2 * GROUP + g] * o_win[r0:r1]
        o_ref[:, g * HEAD_DIM:(g + 1) * HEAD_DIM] = og.astype(o_ref.dtype)


def _attn_prompt(q, kcv, kv, win, g_re, *, nb, seq, tq=512, ck=512):
    assert tq & (tq - 1) == 0 and seq % ck == 0 and WINDOW % tq == 0
    t = q.shape[0]
    ncp = seq // CMP_STRIDE
    nsb = seq // SLC_LEN
    nq = seq // tq
    mt = jnp.asarray(_overlap_matrix(ncp, nsb, ncp - 1, nsb).T)
    kk = np.arange(seq)
    assert nsb <= HEAD_DIM
    e3 = (kk[None, :] // SLC_LEN == np.arange(HEAD_DIM)[:, None]).astype(np.float32)
    e3 = jnp.asarray(e3.reshape(HEAD_DIM, seq // ck, ck).transpose(1, 0, 2), dtype=BF16)
    slab = lambda cb: pl.BlockSpec((seq, HEAD_DIM), lambda b, h, i, cb=cb: (b, cb + h))
    return pl.pallas_call(
        functools.partial(_attn_prompt_kernel, tq=tq, seq=seq, ck=ck),
        grid=(nb, N_KV, nq),
        in_specs=[pl.BlockSpec((tq, GROUP * HEAD_DIM), lambda b, h, i: (b * nq + i, h)),
                  pl.BlockSpec((None, None, None, ncp, HEAD_DIM), lambda b, h, i: (b, 0, h, 0, 0)),
                  pl.BlockSpec((None, None, None, ncp, HEAD_DIM), lambda b, h, i: (b, 1, h, 0, 0)),
                  slab(0), slab(N_KV), slab(0), slab(N_KV),
                  pl.BlockSpec((None, None, tq, 4 * GROUP), lambda b, h, i: (b, h, i, 0)),
                  pl.BlockSpec((nsb, ncp), lambda b, h, i: (0, 0)),
                  pl.BlockSpec((seq // ck, HEAD_DIM, ck), lambda b, h, i: (0, 0, 0))],
        out_specs=pl.BlockSpec((tq, GROUP * HEAD_DIM), lambda b, h, i: (b * nq + i, h)),
        out_shape=jax.ShapeDtypeStruct((t, N_HEADS * HEAD_DIM), BF16),
        scratch_shapes=[pltpu.VMEM((GROUP * tq, HEAD_DIM), F32)] * 3
        + [pltpu.VMEM((seq // ck, tq, ck), F32), pltpu.VMEM((3 * GROUP, tq, HEAD_DIM), F32)],
        compiler_params=_cparams(("arbitrary", "arbitrary", "arbitrary")),
        name="attn_prompt",
    )(q, kcv, kcv, kv, kv, win, win, g_re, mt, e3)


def _attn_sample_kernel(pt_ref, cache_ref, q_ref, kcv_ref, kvn_ref, wn_ref, st_ref, g_ref, ms_ref, e_ref, o_ref,
                        wso_ref, m_scr, l_scr, acc_scr, sel_scr, ocmp_scr, kv_buf, sem, *, past, ds):
    b = pl.program_id(0)
    p = pl.program_id(1)
    n_steps = pl.num_programs(1)

    def page_copies(bb, pp, slot):
        cps = []
        for n in range(PAGES_PER_STEP):
            page = pt_ref[bb, pp * PAGES_PER_STEP + n]
            for cb in range(2 * N_KV):
                cps.append(pltpu.make_async_copy(cache_ref.at[page, :, 2 * N_KV + cb, :],
                                                 kv_buf.at[slot, cb, pl.ds(n * PAGE, PAGE), :],
                                                 sem.at[slot]))
        return cps

    t = b * n_steps + p
    slot = lax.rem(t, 2)

    @pl.when(t == 0)
    def _():
        for cp in page_copies(b, p, slot):
            cp.start()

    @pl.when(t + 1 < pl.num_programs(0) * n_steps)
    def _():
        wrap = p + 1 == n_steps
        for cp in page_copies(jnp.where(wrap, b + 1, b), jnp.where(wrap, 0, p + 1), 1 - slot):
            cp.start()
    rq = GROUP * ds
    ncp = kcv_ref.shape[2]
    nc = ncp - 1
    n_sel_chunks = sel_scr.shape[1]
    lanes = n_sel_chunks * HEAD_DIM

    def q_rows(h):
        c0 = h * GROUP * HEAD_DIM
        return jnp.concatenate(
            [q_ref[:, c0 + g * HEAD_DIM:c0 + (g + 1) * HEAD_DIM] for g in range(GROUP)], axis=0).astype(BF16)

    def online_update(h, sc_, vals):
        m_i = m_scr[h]
        m_new = jnp.maximum(m_i, jnp.max(sc_, axis=1, keepdims=True))
        alpha = jnp.exp(m_i - m_new)
        pp = jnp.exp(sc_ - m_new)
        l_scr[h] = alpha * l_scr[h] + jnp.sum(pp, axis=1, keepdims=True)
        acc_scr[h] = alpha * acc_scr[h] + jnp.dot(pp.astype(BF16), vals, preferred_element_type=F32)
        m_scr[h] = m_new

    @pl.when(p == 0)
    def _():
        pgs = []
        for h in range(N_KV):
            qq = q_rows(h)
            kc = kcv_ref[0, h]
            vc = kcv_ref[1, h]
            s = lax.dot_general(qq, kc, _NT, preferred_element_type=F32) * SCALE
            qpos = past + (lax.broadcasted_iota(jnp.int32, (rq, ncp), 0) & (ds - 1))
            col = lax.broadcasted_iota(jnp.int32, (rq, ncp), 1)
            vis = (col * CMP_STRIDE + (CMP_LEN - 1) <= qpos) & (col < nc)
            s = jnp.where(vis, s, NEG)
            mx = jnp.max(s, axis=1, keepdims=True)
            e = jnp.where(vis, jnp.exp(s - mx), 0.0)
            den = jnp.sum(e, axis=1, keepdims=True)
            pr = e / jnp.where(den > 0.0, den, 1.0)
            ocmp_scr[h] = jnp.dot(pr.astype(BF16), vc, preferred_element_type=F32)
            pg = pr[0:ds]
            for g in range(1, GROUP):
                pg = pg + pr[g * ds:(g + 1) * ds]
            pgs.append(pg)
            m_scr[h] = jnp.full((rq, 1), NEG, F32)
            l_scr[h] = jnp.zeros((rq, 1), F32)
            acc_scr[h] = jnp.zeros((rq, HEAD_DIM), F32)
        nr = N_KV * ds
        pg_all = jnp.concatenate(pgs, axis=0)
        hi = pg_all.astype(BF16)
        r1 = pg_all - hi.astype(F32)
        mid = r1.astype(BF16)
        lo = (r1 - mid.astype(F32)).astype(BF16)
        sc3 = jnp.dot(jnp.concatenate([hi, mid, lo], axis=0), ms_ref[...], preferred_element_type=F32)
        score = (sc3[0:nr] + sc3[nr:2 * nr]) + sc3[2 * nr:3 * nr]
        lane = lax.broadcasted_iota(jnp.int32, (nr, lanes), 1)
        lane_in = lax.rem(lane, jnp.int32(HEAD_DIM))
        real = lane_in < BLK_PER_CHUNK
        blk = jnp.where(real, lax.div(lane, jnp.int32(HEAD_DIM)) * BLK_PER_CHUNK + lane_in, (1 << 19) + lane)
        qrow = lax.broadcasted_iota(jnp.int32, (nr, lanes), 0) & (ds - 1)
        cur = lax.div(past + qrow, jnp.int32(SLC_LEN))
        valid = real & (blk <= cur)
        forced = (blk == 0) | (blk > cur - N_LOCAL)
        sel = _select_topk(score, valid, forced, blk, 1)
        for h in range(N_KV):
            for c in range(n_sel_chunks):
                sel_scr[h, c] = sel[h * ds:(h + 1) * ds, c * HEAD_DIM:(c + 1) * HEAD_DIM]

    for cp in page_copies(b, p, slot):
        cp.wait()

    k0 = p * CHUNK_ROWS
    for h in range(N_KV):
        qq = q_rows(h)
        kch = kv_buf[slot, h].astype(BF16)
        vch = kv_buf[slot, N_KV + h].astype(BF16)
        sc_ = lax.dot_general(qq, kch, _NT, preferred_element_type=F32) * SCALE
        selq = jnp.dot(sel_scr[h, p].astype(BF16), e_ref[...], preferred_element_type=F32)
        kpos = k0 + lax.broadcasted_iota(jnp.int32, (ds, CHUNK_ROWS), 1)
        qp = past + lax.broadcasted_iota(jnp.int32, (ds, CHUNK_ROWS), 0)
        okf = jnp.where((selq > 0.5) & (kpos <= qp), 1.0, 0.0)
        ok = jnp.concatenate([okf] * GROUP, axis=0) > 0.5
        online_update(h, jnp.where(ok, sc_, NEG), vch)

    @pl.when(p == n_steps - 1)
    def _():
        gt = jax.nn.sigmoid(g_ref[...])
        zpad = jnp.zeros((HEAD_DIM - ds, HEAD_DIM), F32)
        for h in range(N_KV):
            qq = q_rows(h)
            c_k = 2 * KV_W + h * HEAD_DIM
            c_v = 3 * KV_W + h * HEAD_DIM
            kn = jnp.concatenate([kvn_ref[:, c_k:c_k + HEAD_DIM], zpad], axis=0).astype(BF16)
            vn = jnp.concatenate([kvn_ref[:, c_v:c_v + HEAD_DIM], zpad], axis=0).astype(BF16)
            sn = lax.dot_general(qq, kn, _NT, preferred_element_type=F32) * SCALE
            last_sel = sel_scr[h, n_sel_chunks - 1]
            nb_last = (past // SLC_LEN) % BLK_PER_CHUNK
            seln = jnp.sum(jnp.where(lax.broadcasted_iota(jnp.int32, (ds, HEAD_DIM), 1) == nb_last,
                                     last_sel, 0.0), axis=1, keepdims=True)
            seln = jnp.concatenate([seln] * GROUP, axis=0)
            srow = lax.broadcasted_iota(jnp.int32, (rq, HEAD_DIM), 0) & (ds - 1)
            kcol = lax.broadcasted_iota(jnp.int32, (rq, HEAD_DIM), 1)
            okn = (seln > 0.5) & (kcol <= srow) & (kcol < ds)
            online_update(h, jnp.where(okn, sn, NEG), vn)
            o_slc = acc_scr[h] / l_scr[h]
            wb = st_ref.shape[0]
            band = wb + HEAD_DIM
            st_flat = st_ref.reshape(wb * 8, HEAD_DIM)
            kw = jnp.concatenate([st_flat[pl.ds(h, wb, stride=8), :],
                                  wn_ref[:, h * HEAD_DIM:(h + 1) * HEAD_DIM], zpad], axis=0).astype(BF16)
            vw = jnp.concatenate([st_flat[pl.ds(N_KV + h, wb, stride=8), :],
                                  wn_ref[:, KV_W + h * HEAD_DIM:KV_W + (h + 1) * HEAD_DIM], zpad],
                                 axis=0).astype(BF16)
            sw = lax.dot_general(qq, kw, _NT, preferred_element_type=F32) * SCALE
            srw = lax.broadcasted_iota(jnp.int32, (rq, band), 0) & (ds - 1)
            idx = lax.broadcasted_iota(jnp.int32, (rq, band), 1)
            dlt = (wb + srw) - idx
            sw = jnp.where((dlt >= 0) & (dlt < WINDOW) & (idx < wb + ds), sw, NEG)
            ew = jnp.exp(sw - jnp.max(sw, axis=1, keepdims=True))
            o_win = jnp.dot(ew.astype(BF16), vw, preferred_element_type=F32) / jnp.sum(ew, axis=1, keepdims=True)
            o_cmp = ocmp_scr[h]
            gh = gt[h]
            for g in range(GROUP):
                r0, r1 = g * ds, (g + 1) * ds
                og = gh[:, g:g + 1] * o_cmp[r0:r1] + gh[:, GROUP + g:GROUP + g + 1] * o_slc[r0:r1]
                og = og + gh[:, 2 * GROUP + g:2 * GROUP + g + 1] * o_win[r0:r1]
                c0 = (h * GROUP + g) * HEAD_DIM
                o_ref[:, c0:c0 + HEAD_DIM] = og
        wb = st_ref.shape[0]
        wso_ref[0:wb - ds] = st_ref[ds:wb]
        wso_flat = wso_ref.reshape(wb * 8, HEAD_DIM)
        for cb in range(2 * N_KV):
            wso_flat[pl.ds((wb - ds) * 8 + cb, ds, stride=8), :] = wn_ref[:, cb * HEAD_DIM:(cb + 1) * HEAD_DIM]


def _attn_sample(q, kcv, kv_new, win_new, win_state, g_re, cache4, page_table, *, ds):
    nb, npg = page_table.shape
    past = npg * PAGE
    assert ds & (ds - 1) == 0 and ds <= SLC_LEN and past % SLC_LEN == 0 and npg % PAGES_PER_STEP == 0
    steps = npg // PAGES_PER_STEP
    ncp = kcv.shape[3]
    wb = win_state.shape[1]
    n_blocks = past // SLC_LEN + 1
    n_sel_chunks = -(-n_blocks // BLK_PER_CHUNK)
    m = _overlap_matrix(ncp, n_sel_chunks * BLK_PER_CHUNK, ncp - 1, n_blocks)
    ms = np.zeros((ncp, n_sel_chunks, HEAD_DIM), np.float32)
    ms[:, :, :BLK_PER_CHUNK] = m.reshape(ncp, n_sel_chunks, BLK_PER_CHUNK)
    ms = jnp.asarray(ms.reshape(ncp, n_sel_chunks * HEAD_DIM), dtype=BF16)
    ee = np.zeros((HEAD_DIM, CHUNK_ROWS), np.float32)
    ee[:BLK_PER_CHUNK] = np.arange(CHUNK_ROWS)[None, :] // SLC_LEN == np.arange(BLK_PER_CHUNK)[:, None]
    ee = jnp.asarray(ee, dtype=BF16)
    rq = GROUP * ds

    grid_spec = pltpu.PrefetchScalarGridSpec(
        num_scalar_prefetch=1,
        grid=(nb, steps),
        in_specs=[
            pl.BlockSpec(memory_space=pl.ANY),
            pl.BlockSpec((ds, N_HEADS * HEAD_DIM), lambda b, p, pt: (b, 0)),
            pl.BlockSpec((None, 2, N_KV, ncp, HEAD_DIM), lambda b, p, pt: (b, 0, 0, 0, 0)),
            pl.BlockSpec((ds, 4 * KV_W), lambda b, p, pt: (b, 0)),
            pl.BlockSpec((ds, 2 * KV_W), lambda b, p, pt: (b, 0)),
            pl.BlockSpec((None, wb, 8, HEAD_DIM), lambda b, p, pt: (b, 0, 0, 0)),
            pl.BlockSpec((None, N_KV, ds, 4 * GROUP), lambda b, p, pt: (b, 0, 0, 0)),
            pl.BlockSpec(ms.shape, lambda b, p, pt: (0, 0)),
            pl.BlockSpec(ee.shape, lambda b, p, pt: (0, 0))],
        out_specs=[pl.BlockSpec((ds, N_HEADS * HEAD_DIM), lambda b, p, pt: (b, 0)),
                   pl.BlockSpec((None, wb, 8, HEAD_DIM), lambda b, p, pt: (b, 0, 0, 0))],
        scratch_shapes=[pltpu.VMEM((N_KV, rq, 1), F32), pltpu.VMEM((N_KV, rq, 1), F32),
                        pltpu.VMEM((N_KV, rq, HEAD_DIM), F32),
                        pltpu.VMEM((N_KV, n_sel_chunks, ds, HEAD_DIM), F32),
                        pltpu.VMEM((N_KV, rq, HEAD_DIM), F32),
                        pltpu.VMEM((2, 2 * N_KV, CHUNK_ROWS, HEAD_DIM), F32),
                        pltpu.SemaphoreType.DMA((2,))],
    )
    return pl.pallas_call(
        functools.partial(_attn_sample_kernel, past=past, ds=ds),
        grid_spec=grid_spec,
        out_shape=[jax.ShapeDtypeStruct((nb * ds, N_HEADS * HEAD_DIM), F32),
                   jax.ShapeDtypeStruct((nb, wb, 8, HEAD_DIM), F32)],
        compiler_params=_cparams(("arbitrary", "arbitrary")),
        name="attn_sample",
    )(page_table, cache4, q, kcv, kv_new, win_new, win_state, g_re, ms, ee)


_Q0, _KV0, _WIN0, _G0 = 0, 2048, 4096, 5120
_REST0 = _G0 + 3 * N_HEADS


def _split_w_in(w_in):
    wt = w_in.T
    return wt, wt[:_G0 + HEAD_DIM].astype(BF16)


def _gate_layout(g_logits, nb, seq):
    g = g_logits[:, :3 * N_HEADS].reshape(nb, seq, 3, N_KV, GROUP)
    g = g.transpose(0, 3, 1, 2, 4).reshape(nb, N_KV, seq, 3 * GROUP)
    return jnp.pad(g, ((0, 0), (0, 0), (0, 0), (0, GROUP)))


def _cmp_weights(w_k1, w_k2, pe_k, w_v1, w_v2, pe_v):
    def by_row(w1):
        w = w1.reshape(2, CMP_STRIDE, HEAD_DIM, HEAD_DIM).transpose(1, 2, 0, 3)
        return w.reshape(CMP_STRIDE * HEAD_DIM, 2 * HEAD_DIM)
    w1pairs = jnp.concatenate([by_row(w_k1), by_row(w_v1)], axis=1).astype(BF16)
    pe8 = jnp.stack([jnp.broadcast_to(pe_k.reshape(1, -1), (8, CMP_LEN * HEAD_DIM)),
                     jnp.broadcast_to(pe_v.reshape(1, -1), (8, CMP_LEN * HEAD_DIM))])
    return w1pairs, pe8, jnp.stack([w_k1, w_v1]), jnp.stack([w_k2, w_v2])


def kernel(x_prompt, x_sample, c_prompt, c_sample, cache_nsa_kv, page_table, state_win_kv, state_conv, w_ada, b_ada, norm1_g, norm2_g, w_in, w_cmp_k1, w_cmp_k2, pe_cmp_k, w_cmp_v1, w_cmp_v2, pe_cmp_v, conv_w, conv_b, w_attn_proj, w_conv_proj, w_out, w_mlp1, w_mlp2, normf_g):
    d = D_MODEL
    nbp, seq, _ = x_prompt.shape
    nbs, ds, _ = x_sample.shape
    tp = nbp * seq
    ts = nbs * ds
    tc = 256
    depth = w_in.shape[0]
    assert depth == 1

    xp = x_prompt.reshape(tp, d)
    xs = x_sample.reshape(ts, d)
    normf = normf_g.reshape(1, d)

    l = 0
    c_all = jnp.concatenate([c_prompt, c_sample, jnp.zeros((16 - nbp - nbs, d), F32)], axis=0)
    ada = _ada(c_all, w_ada[l], b_ada[l].reshape(1, -1)).reshape(16, 6, d)
    ada_p = [ada[:nbp, k][:, None, :] for k in range(6)]
    ada_s = [jnp.repeat(ada[nbp:nbp + nbs, k], ds, axis=0)[None] for k in range(6)]

    w_t, w_head = _split_w_in(w_in[l])
    gates0 = _REST0 + 3 * d
    wa = w_attn_proj[l]
    wc = w_conv_proj[l]
    wo = w_out[l]
    g1n = norm1_g[l].reshape(1, d)
    g2n = norm2_g[l].reshape(1, d)
    cw = conv_w[l]
    cb = conv_b[l].reshape(1, d)
    w1pairs, pe8, w1s, w2s = _cmp_weights(w_cmp_k1[l], w_cmp_k2[l], pe_cmp_k[l],
                                          w_cmp_v1[l], w_cmp_v2[l], pe_cmp_v[l])

    tm = 1024
    hp, q_p = _norm_mm(xp, g1n, ada_p[1], ada_p[0], w_head, tm=tm, tn=1024, seq=seq, out_dtype=BF16,
                       col0=_Q0, n=N_HEADS * HEAD_DIM, name="norm_mm_q")
    wbp = min(WINDOW, seq)
    kv6_p, kvb_p = _mm_kv(hp, w_head, tm=tm, col0=_KV0)
    winb_p, win6_p, g_p = _mm_win(hp, w_head, tm=wbp, seq=seq, col0=_WIN0, gcol0=_G0)
    gates_p = _mm_wstat(hp, w_t, tm=tm, tn=1024, out_dtype=BF16, act="sigmoid", name="mm_gates",
                        col0=gates0, n=2 * d)
    z_p, conv_tiles, w1, w2 = _mm_conv(hp, w_t, cw, cb, w_mlp1[l], w_mlp2[l], tm=tm, tc=tc, seq=seq, col0=_REST0)

    pt_p = jnp.arange(tp // PAGE, dtype=jnp.int32).reshape(nbp, seq // PAGE)
    part_p = _cmp_part(kv6_p.reshape(tp // PAGE, PAGE, 16, HEAD_DIM), pt_p, w1pairs)
    kcv_p = _cmp_finish(part_p, pe8, w1s, w2s)
    o_p = _attn_prompt(q_p, kcv_p, kvb_p, winb_p, g_p, nb=nbp, seq=seq)

    mixed_p = _mix(o_p, z_p, wa, wc, gates_p, tm=tm, tn=512)
    x1_p, h2_p = _mm_resid_norm(mixed_p, wo, xp, ada_p[2], g2n, ada_p[4], ada_p[3], tm=512, seq=seq)
    y_p = _mlp_final(h2_p, w1, w2, x1_p, ada_p[5], normf, tm=512, tf=1024, seq=seq)

    hs, q_s = _norm_mm(xs, g1n, ada_s[1], ada_s[0], w_head, tm=ts, tn=512, seq=ds, out_dtype=F32,
                       col0=_Q0, n=N_HEADS * HEAD_DIM, name="norm_mm_q_s")
    kv_s = _mm(hs, w_head, tm=ts, tn=512, out_dtype=F32, name="mm_kv_s", col0=_KV0, n=4 * KV_W)
    win_s = _mm(hs, w_head, tm=ts, tn=512, out_dtype=F32, name="mm_win_s", col0=_WIN0, n=2 * KV_W)
    g_s = _mm(hs, w_head, tm=ts, tn=HEAD_DIM, out_dtype=F32, name="mm_g_s", col0=_G0, n=HEAD_DIM)
    gates_s = _mm_wstat(hs, w_t, tm=ts, tn=512, out_dtype=F32, act="sigmoid", name="mm_gates_s",
                        col0=gates0, n=2 * d)
    ubc_s = _mm_wstat(hs, w_t, tm=ts, tn=512, out_dtype=F32, name="mm_ubc_s", col0=_REST0, n=3 * d)
    st = state_conv[l]
    z_s, cu_s = _sample_conv(ubc_s, jnp.repeat(st[:, 0], ds, axis=0), jnp.repeat(st[:, 1], ds, axis=0),
                             cw, cb, tc=tc, seq=ds)

    cache4 = cache_nsa_kv[l].reshape(cache_nsa_kv.shape[1], PAGE, 16, HEAD_DIM)
    part_s = _cmp_part(cache4, page_table, w1pairs)
    kcv_s = _cmp_finish(part_s, pe8, w1s, w2s)
    wbs = state_win_kv.shape[2]
    wst = state_win_kv[l].reshape(nbs, wbs, 8, HEAD_DIM)
    o_s, wst_next = _attn_sample(q_s, kcv_s, kv_s, win_s, wst, _gate_layout(g_s, nbs, ds), cache4, page_table,
                                 ds=ds)

    mixed_s = _mix(o_s, z_s, wa, wc, gates_s, tm=ts, tn=512)
    x1_s, h2_s = _mm_resid_norm(mixed_s, wo, xs, ada_s[2], g2n, ada_s[4], ada_s[3], tm=ts, seq=ds)
    y_s = _mlp_final(h2_s, w1, w2, x1_s, ada_s[5], normf, tm=ts, tf=512, seq=ds)

    kv_prompt = kv6_p.reshape(1, nbp, seq, 4, N_KV, HEAD_DIM)
    kv_sample = kv_s.reshape(1, nbs, ds, 4, N_KV, HEAD_DIM)
    win_prompt = win6_p.reshape(1, nbp, wbp, 2, N_KV, HEAD_DIM)
    win_sample = wst_next.reshape(1, nbs, wbs, 2, N_KV, HEAD_DIM)
    tiles_per_seq = seq // tm
    conv_prompt = conv_tiles[tiles_per_seq - 1::tiles_per_seq][None]
    conv_sample = cu_s.reshape(nbs, ds, d)[None, :, ds - (CONV_W - 1):]
    return (y_p.reshape(nbp, seq, d), y_s.reshape(nbs, ds, d), kv_prompt, kv_sample,
            win_prompt, win_sample, conv_prompt, conv_sample)
```

```python
import functools

import numpy as np
import jax
import jax.numpy as jnp
from jax import lax
from jax.experimental import pallas as pl
from jax.experimental.pallas import tpu as pltpu

F32 = jnp.float32
BF16 = jnp.bfloat16

D_MODEL = 2048
HEAD_DIM = 128
N_HEADS = 16
N_KV = 4
GROUP = 4
KV_W = N_KV * HEAD_DIM
CMP_LEN = 32
CMP_STRIDE = 16
SLC_LEN = 64
SLC_TOP = 16
N_LOCAL = 2
WINDOW = 512
CONV_W = 3
RMS_EPS = 1e-6
NEG = -1e30
SCALE = HEAD_DIM ** -0.5
PAGE = 128
PAGES_PER_STEP = 16
CHUNK_ROWS = PAGES_PER_STEP * PAGE
BLK_PER_CHUNK = CHUNK_ROWS // SLC_LEN
VMEM_LIMIT = 56 * 1024 * 1024

_NT = (((1,), (1,)), ((), ()))
_TN = (((0,), (0,)), ((), ()))


def _cparams(sem):
    return pltpu.CompilerParams(dimension_semantics=sem, vmem_limit_bytes=VMEM_LIMIT)


def _ada_kernel(c_ref, w_ref, b_ref, o_ref):
    c = c_ref[...]
    a = (c * jax.nn.sigmoid(c)).astype(BF16)
    o_ref[...] = jnp.dot(a, w_ref[...].astype(BF16), preferred_element_type=F32) + b_ref[...]


def _ada(c, w, b, tn=1024):
    m, k = c.shape
    n = w.shape[1]
    return pl.pallas_call(
        _ada_kernel,
        grid=(n // tn,),
        in_specs=[pl.BlockSpec((m, k), lambda j: (0, 0)),
                  pl.BlockSpec((k, tn), lambda j: (0, j)),
                  pl.BlockSpec((1, tn), lambda j: (0, j))],
        out_specs=pl.BlockSpec((m, tn), lambda j: (0, j)),
        out_shape=jax.ShapeDtypeStruct((m, n), F32),
        compiler_params=_cparams(("arbitrary",)),
        name="ada",
    )(c, w, b)


def _rowspec(p, tm, tn, seq, col=True):
    gr = p.shape[1]
    if gr == 1:
        if col:
            return pl.BlockSpec((1, 1, tn), lambda i, j: ((i * tm) // seq, 0, j))
        return pl.BlockSpec((1, 1, tn), lambda i: ((i * tm) // seq, 0, 0))
    if col:
        return pl.BlockSpec((1, tm, tn), lambda i, j: (0, i, j))
    return pl.BlockSpec((1, tm, tn), lambda i: (0, i, 0))


def _xwt(x_ref, wt_ref):
    return lax.dot_general(x_ref[...].astype(BF16), wt_ref[...], _NT, preferred_element_type=F32)


def _mm_kernel(x_ref, w_ref, o_ref, *, act):
    acc = _xwt(x_ref, w_ref)
    if act == "sigmoid":
        acc = jax.nn.sigmoid(acc)
    o_ref[...] = acc.astype(o_ref.dtype)


def _norm_mm_kernel(x_ref, g_ref, sc_ref, sh_ref, w_ref, h_ref, o_ref):
    @pl.when(pl.program_id(1) == 0)
    def _():
        x = x_ref[...]
        r = lax.rsqrt(jnp.mean(x * x, axis=-1, keepdims=True) + RMS_EPS)
        y = (x * r) * g_ref[...]
        h_ref[...] = (y * (1.0 + sc_ref[0]) + sh_ref[0]).astype(h_ref.dtype)

    o_ref[...] = _xwt(h_ref, w_ref).astype(o_ref.dtype)


def _norm_mm(x, g, sc, sh, w, *, tm, tn, seq, out_dtype, col0, n, name):
    t, d = x.shape
    assert col0 % tn == 0 and n % tn == 0
    cb0 = col0 // tn
    row = lambda p: (pl.BlockSpec((1, 1, d), lambda i, j: ((i * tm) // seq, 0, 0)) if p.shape[1] == 1
                     else pl.BlockSpec((1, tm, d), lambda i, j: (0, i, 0)))
    return pl.pallas_call(
        _norm_mm_kernel,
        grid=(t // tm, n // tn),
        in_specs=[pl.BlockSpec((tm, d), lambda i, j: (i, 0)),
                  pl.BlockSpec((1, d), lambda i, j: (0, 0)),
                  row(sc), row(sh),
                  pl.BlockSpec((tn, d), lambda i, j: (cb0 + j, 0))],
        out_specs=[pl.BlockSpec((tm, d), lambda i, j: (i, 0)),
                   pl.BlockSpec((tm, tn), lambda i, j: (i, j))],
        out_shape=[jax.ShapeDtypeStruct((t, d), BF16), jax.ShapeDtypeStruct((t, n), out_dtype)],
        compiler_params=_cparams(("arbitrary", "arbitrary")),
        name=name,
    )(x, g, sc, sh, w)


def _mm_wstat_kernel(x_ref, w_ref, o_ref, wbf_ref, *, act):
    @pl.when(pl.program_id(1) == 0)
    def _():
        wbf_ref[...] = w_ref[...].astype(BF16)

    acc = _xwt(x_ref, wbf_ref)
    if act == "sigmoid":
        acc = jax.nn.sigmoid(acc)
    o_ref[...] = acc.astype(o_ref.dtype)


def _wrows(tn, k, row0):
    return pl.BlockSpec((pl.Element(tn), pl.Element(k)), lambda j, i: (pl.multiple_of(row0 + j * tn, 8), 0))


def _mm_wstat(x, w, *, tm, tn, out_dtype, act=None, name="mm", col0, n):
    t, k = x.shape
    assert col0 % 8 == 0 and n % tn == 0
    return pl.pallas_call(
        functools.partial(_mm_wstat_kernel, act=act),
        grid=(n // tn, t // tm),
        in_specs=[pl.BlockSpec((tm, k), lambda j, i: (i, 0)),
                  _wrows(tn, k, col0)],
        out_specs=pl.BlockSpec((tm, tn), lambda j, i: (i, j)),
        out_shape=jax.ShapeDtypeStruct((t, n), out_dtype),
        scratch_shapes=[pltpu.VMEM((tn, k), BF16)],
        compiler_params=_cparams(("arbitrary", "arbitrary")),
        name=name,
    )(x, w)


def _mm(x, w, *, tm, tn, out_dtype, act=None, name="mm", col0, n):
    t, k = x.shape
    assert col0 % tn == 0 and n % tn == 0
    cb0 = col0 // tn
    return pl.pallas_call(
        functools.partial(_mm_kernel, act=act),
        grid=(t // tm, n // tn),
        in_specs=[pl.BlockSpec((tm, k), lambda i, j: (i, 0)),
                  pl.BlockSpec((tn, k), lambda i, j: (cb0 + j, 0))],
        out_specs=pl.BlockSpec((tm, tn), lambda i, j: (i, j)),
        out_shape=jax.ShapeDtypeStruct((t, n), out_dtype),
        compiler_params=_cparams(("arbitrary", "arbitrary")),
        name=name,
    )(x, w)


def _store_head_major(o_ref, acc, tm):
    flat = o_ref.reshape(tm * 8, HEAD_DIM)
    for cb in range(8):
        flat[pl.ds(cb, tm, stride=8), :] = acc[:, cb * HEAD_DIM:(cb + 1) * HEAD_DIM]


def _mm_kv_kernel(x_ref, w_ref, o6_ref, ob_ref, *, tm):
    acc = _xwt(x_ref, w_ref)
    _store_head_major(o6_ref, acc, tm)

    @pl.when(pl.program_id(1) == 1)
    def _():
        ob_ref[...] = acc.astype(BF16)


def _mm_kv(h, w, *, tm, col0):
    t, k = h.shape
    tn = 2 * KV_W
    cb0 = col0 // tn
    return pl.pallas_call(
        functools.partial(_mm_kv_kernel, tm=tm),
        grid=(t // tm, 2),
        in_specs=[pl.BlockSpec((tm, k), lambda i, j: (i, 0)),
                  pl.BlockSpec((tn, k), lambda i, j: (cb0 + j, 0))],
        out_specs=[pl.BlockSpec((tm, 8, HEAD_DIM), lambda i, j: (i, j, 0)),
                   pl.BlockSpec((tm, tn), lambda i, j: (i, 0))],
        out_shape=[jax.ShapeDtypeStruct((t, 16, HEAD_DIM), F32),
                   jax.ShapeDtypeStruct((t, tn), BF16)],
        compiler_params=_cparams(("arbitrary", "arbitrary")),
        name="mm_kv",
    )(h, w)


def _mm_win_kernel(x_ref, w_ref, wg_ref, ob_ref, o6_ref, og_ref, *, tm, seq):
    acc = _xwt(x_ref, w_ref)
    ob_ref[...] = acc.astype(BF16)
    gl = _xwt(x_ref, wg_ref)
    for hd in range(N_KV):
        og_ref[hd] = jnp.concatenate(
            [gl[:, br * N_HEADS + hd * GROUP:br * N_HEADS + (hd + 1) * GROUP] for br in range(3)]
            + [jnp.zeros((tm, GROUP), F32)], axis=1)

    @pl.when(((pl.program_id(0) + 1) * tm) % seq == 0)
    def _():
        _store_head_major(o6_ref, acc, tm)


def _mm_win(h, w, *, tm, seq, col0, gcol0):
    t, k = h.shape
    tn = 2 * KV_W
    cb0 = col0 // tn
    gb0 = gcol0 // HEAD_DIM
    return pl.pallas_call(
        functools.partial(_mm_win_kernel, tm=tm, seq=seq),
        grid=(t // tm,),
        in_specs=[pl.BlockSpec((tm, k), lambda i: (i, 0)),
                  pl.BlockSpec((tn, k), lambda i: (cb0, 0)),
                  pl.BlockSpec((HEAD_DIM, k), lambda i: (gb0, 0))],
        out_specs=[pl.BlockSpec((tm, tn), lambda i: (i, 0)),
                   pl.BlockSpec((tm, 8, HEAD_DIM), lambda i: ((i * tm) // seq, 0, 0)),
                   pl.BlockSpec((None, N_KV, tm, 4 * GROUP), lambda i: ((i * tm) // seq, 0, (i * tm % seq) // tm, 0))],
        out_shape=[jax.ShapeDtypeStruct((t, tn), BF16),
                   jax.ShapeDtypeStruct((t // seq * tm, 8, HEAD_DIM), F32),
                   jax.ShapeDtypeStruct((t // seq, N_KV, seq, 4 * GROUP), F32)],
        compiler_params=_cparams(("arbitrary",)),
        name="mm_win",
    )(h, w, w)


def _conv_taps(cu, prev1, prev2, bgate, cw_ref, cb_ref):
    y = cb_ref[...] + prev2 * cw_ref[0:1, :]
    y = y + prev1 * cw_ref[1:2, :]
    y = y + cu * cw_ref[2:3, :]
    return bgate * y


def _mm_conv_kernel(x_ref, wu_ref, wb_ref, wc_ref, cw_ref, cb_ref, ca_ref, cb2_ref, z_ref, st_ref, oa_ref, ob_ref,
                    w3_ref, carry_ref, *, tm, tc, seq):
    i = pl.program_id(1)
    oa_ref[...] = ca_ref[...].astype(BF16)
    ob_ref[...] = cb2_ref[...].astype(BF16)

    @pl.when(i == 0)
    def _():
        w3_ref[0:tc] = wu_ref[...].astype(BF16)
        w3_ref[tc:2 * tc] = wb_ref[...].astype(BF16)
        w3_ref[2 * tc:3 * tc] = wc_ref[...].astype(BF16)

    acc = _xwt(x_ref, w3_ref)
    u = acc[:, 0:tc]
    bgate = acc[:, tc:2 * tc]
    cu = acc[:, 2 * tc:3 * tc] * u

    @pl.when((i * tm) % seq == 0)
    def _():
        carry_ref[...] = jnp.zeros((8, tc), F32)

    car = carry_ref[...]
    p0 = car[0:1, :]
    p1 = car[1:2, :]
    rows = lax.broadcasted_iota(jnp.int32, (tm, tc), 0)
    r1 = pltpu.roll(cu, 1, 0)
    r2 = pltpu.roll(cu, 2, 0)
    prev1 = jnp.where(rows == 0, p1, r1)
    prev2 = jnp.where(rows == 0, p0, jnp.where(rows == 1, p1, r2))
    z_ref[...] = _conv_taps(cu, prev1, prev2, bgate, cw_ref, cb_ref).astype(z_ref.dtype)
    tail = cu[tm - 8:tm, :]
    carry_ref[...] = jnp.concatenate([tail[6:8, :], tail[0:6, :]], axis=0)
    st_ref[0] = tail[6:8, :]


def _mm_conv(h, w, conv_w, conv_b, cast_a, cast_b, *, tm, tc, seq, col0):
    t, k = h.shape
    d = conv_w.shape[1]
    nj = d // tc
    ni = t // tm
    steps = nj * ni
    ra, rb_ = cast_a.shape[0] // steps, cast_b.shape[0] // steps
    assert ra % 16 == 0 and rb_ % 16 == 0 and ra * steps == cast_a.shape[0] and rb_ * steps == cast_b.shape[0]
    slab = lambda r, m: pl.BlockSpec((r, m.shape[1]), lambda j, i: (j * ni + i, 0))
    return pl.pallas_call(
        functools.partial(_mm_conv_kernel, tm=tm, tc=tc, seq=seq),
        grid=(nj, ni),
        in_specs=[pl.BlockSpec((tm, k), lambda j, i: (i, 0)),
                  _wrows(tc, k, col0), _wrows(tc, k, col0 + d), _wrows(tc, k, col0 + 2 * d),
                  pl.BlockSpec((CONV_W, tc), lambda j, i: (0, j)),
                  pl.BlockSpec((1, tc), lambda j, i: (0, j)),
                  slab(ra, cast_a), slab(rb_, cast_b)],
        out_specs=[pl.BlockSpec((tm, tc), lambda j, i: (i, j)),
                   pl.BlockSpec((1, CONV_W - 1, tc), lambda j, i: (i, 0, j)),
                   slab(ra, cast_a), slab(rb_, cast_b)],
        out_shape=[jax.ShapeDtypeStruct((t, d), BF16),
                   jax.ShapeDtypeStruct((t // tm, CONV_W - 1, d), F32),
                   jax.ShapeDtypeStruct(cast_a.shape, BF16),
                   jax.ShapeDtypeStruct(cast_b.shape, BF16)],
        scratch_shapes=[pltpu.VMEM((3 * tc, k), BF16), pltpu.VMEM((8, tc), F32)],
        compiler_params=_cparams(("arbitrary", "arbitrary")),
        name="mm_conv",
    )(h, w, w, w, conv_w, conv_b, cast_a, cast_b)


def _sample_conv_kernel(u_ref, b_ref, c_ref, st0_ref, st1_ref, cw_ref, cb_ref, z_ref, cu_ref, *, tc, seq):
    t = u_ref.shape[0]
    bgate = b_ref[...]
    cu = c_ref[...] * u_ref[...]
    s = lax.rem(lax.broadcasted_iota(jnp.int32, (t, tc), 0), jnp.int32(seq))
    r1 = pltpu.roll(cu, 1, 0)
    r2 = pltpu.roll(cu, 2, 0)
    prev1 = jnp.where(s == 0, st1_ref[...], r1)
    prev2 = jnp.where(s == 0, st0_ref[...], jnp.where(s == 1, st1_ref[...], r2))
    z_ref[...] = _conv_taps(cu, prev1, prev2, bgate, cw_ref, cb_ref)
    cu_ref[...] = cu


def _sample_conv(ubc, st0, st1, conv_w, conv_b, *, tc, seq):
    t = ubc.shape[0]
    d = ubc.shape[1] // 3
    nj = d // tc
    return pl.pallas_call(
        functools.partial(_sample_conv_kernel, tc=tc, seq=seq),
        grid=(nj,),
        in_specs=[pl.BlockSpec((t, tc), lambda j: (0, j)),
                  pl.BlockSpec((t, tc), lambda j: (0, nj + j)),
                  pl.BlockSpec((t, tc), lambda j: (0, 2 * nj + j)),
                  pl.BlockSpec((t, tc), lambda j: (0, j)),
                  pl.BlockSpec((t, tc), lambda j: (0, j)),
                  pl.BlockSpec((CONV_W, tc), lambda j: (0, j)),
                  pl.BlockSpec((1, tc), lambda j: (0, j))],
        out_specs=[pl.BlockSpec((t, tc), lambda j: (0, j)),
                   pl.BlockSpec((t, tc), lambda j: (0, j))],
        out_shape=[jax.ShapeDtypeStruct((t, d), F32), jax.ShapeDtypeStruct((t, d), F32)],
        compiler_params=_cparams(("arbitrary",)),
        name="sample_conv",
    )(ubc, ubc, ubc, st0, st1, conv_w, conv_b)


def _mix_kernel(o_ref, z_ref, wa_ref, wc_ref, ga_ref, gc_ref, m_ref):
    a = jnp.dot(o_ref[...].astype(BF16), wa_ref[...].astype(BF16), preferred_element_type=F32)
    c = jnp.dot(z_ref[...].astype(BF16), wc_ref[...].astype(BF16), preferred_element_type=F32)
    m_ref[...] = (ga_ref[...].astype(F32) * a + gc_ref[...].astype(F32) * c).astype(m_ref.dtype)


def _mix(o, z, wa, wc, gates, *, tm, tn):
    t, k = o.shape
    n = wa.shape[1]
    nj = n // tn
    return pl.pallas_call(
        _mix_kernel,
        grid=(t // tm, nj),
        in_specs=[pl.BlockSpec((tm, k), lambda i, j: (i, 0)),
                  pl.BlockSpec((tm, k), lambda i, j: (i, 0)),
                  pl.BlockSpec((k, tn), lambda i, j: (0, j)),
                  pl.BlockSpec((k, tn), lambda i, j: (0, j)),
                  pl.BlockSpec((tm, tn), lambda i, j: (i, j)),
                  pl.BlockSpec((tm, tn), lambda i, j: (i, j + nj))],
        out_specs=pl.BlockSpec((tm, tn), lambda i, j: (i, j)),
        out_shape=jax.ShapeDtypeStruct((t, n), BF16),
        compiler_params=_cparams(("arbitrary", "arbitrary")),
        name="mix",
    )(o, z, wa, wc, gates, gates)


def _resid_norm_kernel(a_ref, w_ref, x_ref, g_ref, ng_ref, sc_ref, sh_ref, x1_ref, h2_ref, wbf_ref):
    @pl.when(pl.program_id(0) == 0)
    def _():
        wbf_ref[...] = w_ref[...].astype(BF16)

    acc = jnp.dot(a_ref[...], wbf_ref[...], preferred_element_type=F32)
    x1 = x_ref[...] + g_ref[0] * acc
    x1_ref[...] = x1
    r = lax.rsqrt(jnp.mean(x1 * x1, axis=-1, keepdims=True) + RMS_EPS)
    y = (x1 * r) * ng_ref[...]
    h2_ref[...] = (y * (1.0 + sc_ref[0]) + sh_ref[0]).astype(h2_ref.dtype)


def _mm_resid_norm(a, w, x, gate, ng, sc, sh, *, tm, seq):
    t, k = a.shape
    n = w.shape[1]
    row = lambda p: _rowspec(p, tm, n, seq, col=False)
    return pl.pallas_call(
        _resid_norm_kernel,
        grid=(t // tm,),
        in_specs=[pl.BlockSpec((tm, k), lambda i: (i, 0)),
                  pl.BlockSpec((k, n), lambda i: (0, 0), pipeline_mode=pl.Buffered(1)),
                  pl.BlockSpec((tm, n), lambda i: (i, 0)),
                  row(gate),
                  pl.BlockSpec((1, n), lambda i: (0, 0)),
                  row(sc), row(sh)],
        out_specs=[pl.BlockSpec((tm, n), lambda i: (i, 0)), pl.BlockSpec((tm, n), lambda i: (i, 0))],
        out_shape=[jax.ShapeDtypeStruct((t, n), F32), jax.ShapeDtypeStruct((t, n), BF16)],
        scratch_shapes=[pltpu.VMEM((k, n), BF16)],
        compiler_params=_cparams(("arbitrary",)),
        name="mm_resid_norm",
    )(a, w, x, gate, ng, sc, sh)


def _mlp_kernel(h_ref, w1_ref, w2_ref, x_ref, g_ref, nf_ref, o_ref, acc_ref):
    k = pl.program_id(1)

    @pl.when(k == 0)
    def _():
        acc_ref[...] = jnp.zeros_like(acc_ref)

    a = jnp.dot(h_ref[...], w1_ref[...], preferred_element_type=F32)
    a = jnp.square(jnp.maximum(a, 0.0)).astype(BF16)
    acc_ref[...] += jnp.dot(a, w2_ref[...], preferred_element_type=F32)

    @pl.when(k == pl.num_programs(1) - 1)
    def _():
        x2 = x_ref[...] + g_ref[0] * acc_ref[...]
        r = lax.rsqrt(jnp.mean(x2 * x2, axis=-1, keepdims=True) + RMS_EPS)
        o_ref[...] = (x2 * r) * nf_ref[...]


def _mlp_final(h2, w1, w2, x1, gate, normf, *, tm, tf, seq):
    t, d = h2.shape
    f = w1.shape[1]
    return pl.pallas_call(
        _mlp_kernel,
        grid=(t // tm, f // tf),
        in_specs=[pl.BlockSpec((tm, d), lambda i, k: (i, 0)),
                  pl.BlockSpec((d, tf), lambda i, k: (0, k)),
                  pl.BlockSpec((tf, d), lambda i, k: (k, 0)),
                  pl.BlockSpec((tm, d), lambda i, k: (i, 0)),
                  _rowspec_k(gate, tm, d, seq),
                  pl.BlockSpec((1, d), lambda i, k: (0, 0))],
        out_specs=pl.BlockSpec((tm, d), lambda i, k: (i, 0)),
        out_shape=jax.ShapeDtypeStruct((t, d), F32),
        scratch_shapes=[pltpu.VMEM((tm, d), F32)],
        compiler_params=_cparams(("arbitrary", "arbitrary")),
        name="mlp",
    )(h2, w1, w2, x1, gate, normf)


def _rowspec_k(p, tm, d, seq):
    if p.shape[1] == 1:
        return pl.BlockSpec((1, 1, d), lambda i, k: ((i * tm) // seq, 0, 0))
    return pl.BlockSpec((1, tm, d), lambda i, k: (0, i, 0))


def _cmp_part_kernel(pt_ref, *refs):
    pages = refs[:PAGES_PER_STEP]
    w_ref = refs[PAGES_PER_STEP]
    o_ref = refs[PAGES_PER_STEP + 1]
    stage = refs[PAGES_PER_STEP + 2]
    nchunk = CHUNK_ROWS // CMP_STRIDE
    cpp = PAGE // CMP_STRIDE
    nslab = 2 * N_KV
    cols = [jnp.concatenate([pg[pl.ds(r, cpp, stride=CMP_STRIDE)].reshape(cpp * nslab, HEAD_DIM)
                             for pg in pages], axis=0).astype(BF16) for r in range(CMP_STRIDE)]
    lhs = jnp.concatenate(cols, axis=1)
    acc = jnp.dot(lhs, w_ref[...], preferred_element_type=F32)
    slab = lax.broadcasted_iota(jnp.int32, (nchunk * nslab, 2 * HEAD_DIM), 0) & (nslab - 1)
    sel = jnp.where(slab < N_KV, acc[:, :2 * HEAD_DIM], acc[:, 2 * HEAD_DIM:])
    stage[0] = sel[:, :HEAD_DIM]
    stage[1] = sel[:, HEAD_DIM:]
    for cb in range(nslab):
        o_ref[cb // N_KV, cb % N_KV] = jnp.concatenate(
            [stage[0, pl.ds(cb, nchunk, stride=nslab), :], stage[1, pl.ds(cb, nchunk, stride=nslab), :]], axis=1)


def _page_spec(n, kind_pair):
    return pl.BlockSpec((None, PAGE, 8, HEAD_DIM),
                        lambda b, p, pt, n=n: (pt[b, p * PAGES_PER_STEP + n], 0, kind_pair, 0))


def _cmp_part(pages4, page_table, w1pairs):
    nb, npg = page_table.shape
    steps = npg // PAGES_PER_STEP
    nchunk = PAGES_PER_STEP * (PAGE // CMP_STRIDE)
    grid_spec = pltpu.PrefetchScalarGridSpec(
        num_scalar_prefetch=1,
        grid=(nb, steps),
        in_specs=[_page_spec(n, 0) for n in range(PAGES_PER_STEP)]
        + [pl.BlockSpec((CMP_STRIDE * HEAD_DIM, 4 * HEAD_DIM), lambda b, p, pt: (0, 0))],
        out_specs=pl.BlockSpec((None, 2, N_KV, nchunk, 2 * HEAD_DIM), lambda b, p, pt: (b, 0, 0, p, 0)),
        scratch_shapes=[pltpu.VMEM((2, nchunk * 2 * N_KV, HEAD_DIM), F32)],
    )
    return pl.pallas_call(
        _cmp_part_kernel,
        grid_spec=grid_spec,
        out_shape=jax.ShapeDtypeStruct((nb, 2, N_KV, steps * nchunk, 2 * HEAD_DIM), F32),
        compiler_params=_cparams(("arbitrary", "arbitrary")),
        name="cmp_part",
    )(page_table, *([pages4] * PAGES_PER_STEP), w1pairs)


def _gelu_tanh(x):
    c = np.sqrt(2.0 / np.pi).astype(np.float32)
    return 0.5 * x * (1.0 + jnp.tanh(c * (x + 0.044715 * (x * x * x))))


def _cmp_finish_kernel(part_ref, pe_ref, w1_ref, w2_ref, o_ref, bias_ref):
    @pl.when(pl.program_id(1) == 0)
    def _():
        bias_ref[...] = jnp.dot(pe_ref[...], w1_ref[...], preferred_element_type=F32,
                                precision=lax.Precision.HIGHEST)

    w2 = w2_ref[...].astype(BF16)
    for hd in range(N_KV):
        part = part_ref[hd]
        n = part.shape[0]
        h = part[:, 0:HEAD_DIM] + pltpu.roll(part[:, HEAD_DIM:], n - 1, 0)
        h = h + bias_ref[0:1, :]
        o_ref[hd] = jnp.dot(_gelu_tanh(h).astype(BF16), w2, preferred_element_type=F32).astype(o_ref.dtype)


def _cmp_finish(part, pe8, w1, w2):
    nb, _, _, nchunk, _ = part.shape
    return pl.pallas_call(
        _cmp_finish_kernel,
        grid=(2, nb),
        in_specs=[pl.BlockSpec((None, None, N_KV, nchunk, 2 * HEAD_DIM), lambda k, b: (b, k, 0, 0, 0)),
                  pl.BlockSpec((None, 8, CMP_LEN * HEAD_DIM), lambda k, b: (k, 0, 0)),
                  pl.BlockSpec((None, CMP_LEN * HEAD_DIM, HEAD_DIM), lambda k, b: (k, 0, 0)),
                  pl.BlockSpec((None, HEAD_DIM, HEAD_DIM), lambda k, b: (k, 0, 0))],
        out_specs=pl.BlockSpec((None, None, N_KV, nchunk, HEAD_DIM), lambda k, b: (b, k, 0, 0, 0)),
        out_shape=jax.ShapeDtypeStruct((nb, 2, N_KV, nchunk, HEAD_DIM), BF16),
        scratch_shapes=[pltpu.VMEM((8, HEAD_DIM), F32)],
        compiler_params=_cparams(("arbitrary", "arbitrary")),
        name="cmp_finish",
    )(part, pe8, w1, w2)


def _select_topk(score, valid, forced, blk, axis):
    sc = jnp.where(valid, jnp.where(forced, jnp.inf, score), -jnp.inf)
    sel = jnp.zeros(score.shape, F32)
    big = jnp.int32(1 << 20)
    for _ in range(SLC_TOP):
        mx = jnp.max(sc, axis=axis, keepdims=True)
        idx = jnp.min(jnp.where(sc == mx, blk, big), axis=axis, keepdims=True)
        hit = blk == idx
        sel = jnp.where(hit & (mx > -jnp.inf), 1.0, sel)
        sc = jnp.where(hit, -jnp.inf, sc)
    return sel


def _overlap_matrix(nc_pad, n_blocks_pad, nc, n_blocks):
    cs = np.arange(nc_pad) * CMP_STRIDE
    sb = np.arange(n_blocks_pad) * SLC_LEN
    ov = np.clip(np.minimum(cs[:, None] + CMP_LEN, sb[None, :] + SLC_LEN)
                 - np.maximum(cs[:, None], sb[None, :]), 0, None)
    m = (ov / CMP_STRIDE).astype(np.float32)
    m[nc:, :] = 0.0
    m[:, n_blocks:] = 0.0
    return m


def _attn_prompt_kernel(q_ref, kcb, vcb, ksb, vsb, kwb, vwb, g_ref, mt_ref, e_ref, o_ref,
                        m_scr, l_scr, acc_scr, bias_scr, gate_scr, *, tq, seq, ck):
    qi = pl.program_id(2)
    nc = seq // CMP_STRIDE - 1
    nsb = seq // SLC_LEN
    rb = min(tq, 128)

    q = q_ref[...]
    qq = jnp.concatenate([q[:, g * HEAD_DIM:(g + 1) * HEAD_DIM] for g in range(GROUP)], axis=0)
    rq = GROUP * tq
    q0 = qi * tq

    gt = jax.nn.sigmoid(g_ref[...])
    ngc = 4 * GROUP
    hi = gt.astype(BF16)
    r1 = gt - hi.astype(F32)
    mid = r1.astype(BF16)
    lo = (r1 - mid.astype(F32)).astype(BF16)
    g3 = jnp.concatenate([hi, mid, lo, jnp.zeros((tq, HEAD_DIM - 3 * ngc), BF16)], axis=1)
    srow = lax.broadcasted_iota(jnp.int32, (HEAD_DIM, 3 * GROUP * HEAD_DIM), 0)
    scol = lax.broadcasted_iota(jnp.int32, (HEAD_DIM, 3 * GROUP * HEAD_DIM), 1)
    spread = ((srow < 3 * ngc) & ((srow & (ngc - 1)) == lax.div(scol, jnp.int32(HEAD_DIM)))).astype(BF16)
    gl = jnp.dot(g3, spread, preferred_element_type=F32)
    for c in range(3 * GROUP):
        gate_scr[c] = gl[:, c * HEAD_DIM:(c + 1) * HEAD_DIM]

    ncp = kcb.shape[0]
    s = lax.dot_general(qq, kcb[...], _NT, preferred_element_type=F32) * SCALE
    qpos = q0 + (lax.broadcasted_iota(jnp.int32, (rq, ncp), 0) & (tq - 1))
    col = lax.broadcasted_iota(jnp.int32, (rq, ncp), 1)
    vis = (col * CMP_STRIDE + (CMP_LEN - 1) <= qpos) & (col < nc)
    s = jnp.where(vis, s, NEG)
    mx = jnp.max(s, axis=1, keepdims=True)
    e = jnp.where(vis, jnp.exp(s - mx), 0.0)
    den = jnp.sum(e, axis=1, keepdims=True)
    p = e / jnp.where(den > 0.0, den, 1.0)
    o_cmp = jnp.dot(p.astype(BF16), vcb[...], preferred_element_type=F32)
    pg = p[0:tq]
    for g in range(1, GROUP):
        pg = pg + p[g * tq:(g + 1) * tq]

    sct = lax.dot_general(mt_ref[...], pg, _NT, preferred_element_type=F32,
                          precision=lax.Precision.HIGHEST)
    blk = lax.broadcasted_iota(jnp.int32, (nsb, tq), 0)
    cur = lax.div(q0 + lax.broadcasted_iota(jnp.int32, (nsb, tq), 1), jnp.int32(SLC_LEN))
    valid = blk <= cur
    forced = (blk == 0) | (blk > cur - N_LOCAL)
    need_topk = (q0 + tq - 1) // SLC_LEN + 1 > SLC_TOP
    sel_t = lax.cond(need_topk,
                     lambda: _select_topk(sct, valid, forced, blk, 0),
                     lambda: valid.astype(F32)).astype(BF16)

    def sweep(k_ref, v_ref, k_start, n_chunks, bias_fn):
        nl = ck // HEAD_DIM

        def scores(c):
            k0 = pl.multiple_of(k_start + c * ck, 128)
            return lax.dot_general(qq, k_ref[pl.ds(k0, ck), :], _NT, preferred_element_type=F32)

        def chunk_inputs(c):
            k0 = pl.multiple_of(k_start + c * ck, 128)
            return scores(c), bias_fn(c, k0)

        c_exp = np.float32(SCALE * np.log2(np.e))

        def reduce_chunk(c, sb, first):
            sc_, bias = sb
            k0 = pl.multiple_of(k_start + c * ck, 128)
            vch = jnp.concatenate([v_ref[pl.ds(k0, ck), :], jnp.ones((ck, HEAD_DIM), BF16)], axis=1)
            alphas, ps = [], []
            nblk = rq // rb
            for r in range(nblk):
                rows = slice(r * rb, (r + 1) * rb)
                b0 = (r * rb) % tq
                sg = [sc_[rows, j * HEAD_DIM:(j + 1) * HEAD_DIM]
                      + bias[b0:b0 + rb, j * HEAD_DIM:(j + 1) * HEAD_DIM] for j in range(nl)]
                mx = functools.reduce(jnp.maximum, sg)
                row_max = jnp.max(mx, axis=1, keepdims=True)
                if first:
                    m_new = jnp.broadcast_to(row_max, (rb, HEAD_DIM))
                else:
                    m_old = m_scr[rows]
                    m_new = jnp.maximum(m_old, row_max)
                    alphas.append(jnp.exp2((m_old - m_new) * c_exp))
                pj = [jnp.exp2((x - m_new) * c_exp) for x in sg]
                m_scr[rows] = m_new
                ps.append(jnp.concatenate([x.astype(BF16) for x in pj], axis=1))
                if (r + 1) % (nblk // 2) == 0:
                    half = slice((r + 1 - nblk // 2) * rb, (r + 1) * rb)
                    pv = jnp.dot(jnp.concatenate(ps, axis=0), vch, preferred_element_type=F32)
                    if first:
                        acc_scr[half] = pv[:, :HEAD_DIM]
                        l_scr[half] = pv[:, HEAD_DIM:]
                    else:
                        al = jnp.concatenate(alphas, axis=0)
                        acc_scr[half] = al * acc_scr[half] + pv[:, :HEAD_DIM]
                        l_scr[half] = al * l_scr[half] + pv[:, HEAD_DIM:]
                    alphas, ps = [], []

        def body(c, carry):
            reduce_chunk(c, chunk_inputs(c), False)
            return carry

        reduce_chunk(0, chunk_inputs(0), True)
        lax.fori_loop(1, n_chunks, body, 0)
        return acc_scr[...] / l_scr[...]

    qp = q0 + lax.broadcasted_iota(jnp.int32, (tq, ck), 0)
    kcol = lax.broadcasted_iota(jnp.int32, (tq, ck), 1)

    eye = (lax.broadcasted_iota(jnp.int32, (nsb, HEAD_DIM), 0)
           == lax.broadcasted_iota(jnp.int32, (nsb, HEAD_DIM), 1)).astype(BF16)
    sel_q = lax.dot_general(sel_t, eye, _TN, preferred_element_type=F32).astype(BF16)

    n_slc = (q0 + tq + ck - 1) // ck

    def put_mask(c):
        selq = jnp.dot(sel_q, e_ref[c], preferred_element_type=F32)
        bias_scr[c] = jnp.where((selq > 0.5) & (c * ck + kcol <= qp), 0.0, NEG)

    def mask_body(i, carry):
        put_mask(2 * i)
        put_mask(jnp.minimum(2 * i + 1, seq // ck - 1))
        return carry

    lax.fori_loop(0, (n_slc + 1) // 2, mask_body, 0)
    o_slc = sweep(ksb, vsb, 0, n_slc, lambda c, k0: bias_scr[c])

    span = -(-(WINDOW + tq) // ck) * ck

    def win_bias(c, k0):
        dlt = qp - (k0 + kcol)
        return jnp.where((dlt >= 0) & (dlt < WINDOW), 0.0, NEG)

    o_win = sweep(kwb, vwb, jnp.maximum(q0 + tq - span, 0), span // ck, win_bias)

    for g in range(GROUP):
        r0, r1 = g * tq, (g + 1) * tq
        og = gate_scr[g] * o_cmp[r0:r1] + gate_scr[GROUP + g] * o_slc[r0:r1]
        og = og + gate_scr[2 * GROUP + g] * o_win[r0:r1]
        o_ref[:, g * HEAD_DIM:(g + 1) * HEAD_DIM] = og.astype(o_ref.dtype)


def _attn_prompt(q, kcv, kv, win, g_re, *, nb, seq, tq=512, ck=512):
    assert tq & (tq - 1) == 0 and seq % ck == 0 and WINDOW % tq == 0
    t = q.shape[0]
    ncp = seq // CMP_STRIDE
    nsb = seq // SLC_LEN
    nq = seq // tq
    mt = jnp.asarray(_overlap_matrix(ncp, nsb, ncp - 1, nsb).T)
    kk = np.arange(seq)
    assert nsb <= HEAD_DIM
    e3 = (kk[None, :] // SLC_LEN == np.arange(HEAD_DIM)[:, None]).astype(np.float32)
    e3 = jnp.asarray(e3.reshape(HEAD_DIM, seq // ck, ck).transpose(1, 0, 2), dtype=BF16)
    slab = lambda cb: pl.BlockSpec((seq, HEAD_DIM), lambda b, h, i, cb=cb: (b, cb + h))
    return pl.pallas_call(
        functools.partial(_attn_prompt_kernel, tq=tq, seq=seq, ck=ck),
        grid=(nb, N_KV, nq),
        in_specs=[pl.BlockSpec((tq, GROUP * HEAD_DIM), lambda b, h, i: (b * nq + i, h)),
                  pl.BlockSpec((None, None, None, ncp, HEAD_DIM), lambda b, h, i: (b, 0, h, 0, 0)),
                  pl.BlockSpec((None, None, None, ncp, HEAD_DIM), lambda b, h, i: (b, 1, h, 0, 0)),
                  slab(0), slab(N_KV), slab(0), slab(N_KV),
                  pl.BlockSpec((None, None, tq, 4 * GROUP), lambda b, h, i: (b, h, i, 0)),
                  pl.BlockSpec((nsb, ncp), lambda b, h, i: (0, 0)),
                  pl.BlockSpec((seq // ck, HEAD_DIM, ck), lambda b, h, i: (0, 0, 0))],
        out_specs=pl.BlockSpec((tq, GROUP * HEAD_DIM), lambda b, h, i: (b * nq + i, h)),
        out_shape=jax.ShapeDtypeStruct((t, N_HEADS * HEAD_DIM), BF16),
        scratch_shapes=[pltpu.VMEM((GROUP * tq, HEAD_DIM), F32)] * 3
        + [pltpu.VMEM((seq // ck, tq, ck), F32), pltpu.VMEM((3 * GROUP, tq, HEAD_DIM), F32)],
        compiler_params=_cparams(("arbitrary", "arbitrary", "arbitrary")),
        name="attn_prompt",
    )(q, kcv, kcv, kv, kv, win, win, g_re, mt, e3)


def _attn_sample_kernel(pt_ref, cache_ref, q_ref, kcv_ref, kvn_ref, wn_ref, st_ref, g_ref, ms_ref, e_ref, o_ref,
                        wso_ref, m_scr, l_scr, acc_scr, sel_scr, ocmp_scr, kv_buf, sem, *, past, ds):
    b = pl.program_id(0)
    p = pl.program_id(1)
    n_steps = pl.num_programs(1)

    def page_copies(bb, pp, slot):
        cps = []
        for n in range(PAGES_PER_STEP):
            page = pt_ref[bb, pp * PAGES_PER_STEP + n]
            for cb in range(2 * N_KV):
                cps.append(pltpu.make_async_copy(cache_ref.at[page, :, 2 * N_KV + cb, :],
                                                 kv_buf.at[slot, cb, pl.ds(n * PAGE, PAGE), :],
                                                 sem.at[slot]))
        return cps

    t = b * n_steps + p
    slot = lax.rem(t, 2)

    @pl.when(t == 0)
    def _():
        for cp in page_copies(b, p, slot):
            cp.start()

    @pl.when(t + 1 < pl.num_programs(0) * n_steps)
    def _():
        wrap = p + 1 == n_steps
        for cp in page_copies(jnp.where(wrap, b + 1, b), jnp.where(wrap, 0, p + 1), 1 - slot):
            cp.start()
    rq = GROUP * ds
    ncp = kcv_ref.shape[2]
    nc = ncp - 1
    n_sel_chunks = sel_scr.shape[1]
    lanes = n_sel_chunks * HEAD_DIM

    def q_rows(h):
        c0 = h * GROUP * HEAD_DIM
        return jnp.concatenate(
            [q_ref[:, c0 + g * HEAD_DIM:c0 + (g + 1) * HEAD_DIM] for g in range(GROUP)], axis=0).astype(BF16)

    def online_update(h, sc_, vals):
        m_i = m_scr[h]
        m_new = jnp.maximum(m_i, jnp.max(sc_, axis=1, keepdims=True))
        alpha = jnp.exp(m_i - m_new)
        pp = jnp.exp(sc_ - m_new)
        l_scr[h] = alpha * l_scr[h] + jnp.sum(pp, axis=1, keepdims=True)
        acc_scr[h] = alpha * acc_scr[h] + jnp.dot(pp.astype(BF16), vals, preferred_element_type=F32)
        m_scr[h] = m_new

    @pl.when(p == 0)
    def _():
        pgs = []
        for h in range(N_KV):
            qq = q_rows(h)
            kc = kcv_ref[0, h]
            vc = kcv_ref[1, h]
            s = lax.dot_general(qq, kc, _NT, preferred_element_type=F32) * SCALE
            qpos = past + (lax.broadcasted_iota(jnp.int32, (rq, ncp), 0) & (ds - 1))
            col = lax.broadcasted_iota(jnp.int32, (rq, ncp), 1)
            vis = (col * CMP_STRIDE + (CMP_LEN - 1) <= qpos) & (col < nc)
            s = jnp.where(vis, s, NEG)
            mx = jnp.max(s, axis=1, keepdims=True)
            e = jnp.where(vis, jnp.exp(s - mx), 0.0)
            den = jnp.sum(e, axis=1, keepdims=True)
            pr = e / jnp.where(den > 0.0, den, 1.0)
            ocmp_scr[h] = jnp.dot(pr.astype(BF16), vc, preferred_element_type=F32)
            pg = pr[0:ds]
            for g in range(1, GROUP):
                pg = pg + pr[g * ds:(g + 1) * ds]
            pgs.append(pg)
            m_scr[h] = jnp.full((rq, 1), NEG, F32)
            l_scr[h] = jnp.zeros((rq, 1), F32)
            acc_scr[h] = jnp.zeros((rq, HEAD_DIM), F32)
        nr = N_KV * ds
        pg_all = jnp.concatenate(pgs, axis=0)
        hi = pg_all.astype(BF16)
        r1 = pg_all - hi.astype(F32)
        mid = r1.astype(BF16)
        lo = (r1 - mid.astype(F32)).astype(BF16)
        sc3 = jnp.dot(jnp.concatenate([hi, mid, lo], axis=0), ms_ref[...], preferred_element_type=F32)
        score = (sc3[0:nr] + sc3[nr:2 * nr]) + sc3[2 * nr:3 * nr]
        lane = lax.broadcasted_iota(jnp.int32, (nr, lanes), 1)
        lane_in = lax.rem(lane, jnp.int32(HEAD_DIM))
        real = lane_in < BLK_PER_CHUNK
        blk = jnp.where(real, lax.div(lane, jnp.int32(HEAD_DIM)) * BLK_PER_CHUNK + lane_in, (1 << 19) + lane)
        qrow = lax.broadcasted_iota(jnp.int32, (nr, lanes), 0) & (ds - 1)
        cur = lax.div(past + qrow, jnp.int32(SLC_LEN))
        valid = real & (blk <= cur)
        forced = (blk == 0) | (blk > cur - N_LOCAL)
        sel = _select_topk(score, valid, forced, blk, 1)
        for h in range(N_KV):
            for c in range(n_sel_chunks):
                sel_scr[h, c] = sel[h * ds:(h + 1) * ds, c * HEAD_DIM:(c + 1) * HEAD_DIM]

    for cp in page_copies(b, p, slot):
        cp.wait()

    k0 = p * CHUNK_ROWS
    for h in range(N_KV):
        qq = q_rows(h)
        kch = kv_buf[slot, h].astype(BF16)
        vch = kv_buf[slot, N_KV + h].astype(BF16)
        sc_ = lax.dot_general(qq, kch, _NT, preferred_element_type=F32) * SCALE
        selq = jnp.dot(sel_scr[h, p].astype(BF16), e_ref[...], preferred_element_type=F32)
        kpos = k0 + lax.broadcasted_iota(jnp.int32, (ds, CHUNK_ROWS), 1)
        qp = past + lax.broadcasted_iota(jnp.int32, (ds, CHUNK_ROWS), 0)
        okf = jnp.where((selq > 0.5) & (kpos <= qp), 1.0, 0.0)
        ok = jnp.concatenate([okf] * GROUP, axis=0) > 0.5
        online_update(h, jnp.where(ok, sc_, NEG), vch)

    @pl.when(p == n_steps - 1)
    def _():
        gt = jax.nn.sigmoid(g_ref[...])
        zpad = jnp.zeros((HEAD_DIM - ds, HEAD_DIM), F32)
        for h in range(N_KV):
            qq = q_rows(h)
            c_k = 2 * KV_W + h * HEAD_DIM
            c_v = 3 * KV_W + h * HEAD_DIM
            kn = jnp.concatenate([kvn_ref[:, c_k:c_k + HEAD_DIM], zpad], axis=0).astype(BF16)
            vn = jnp.concatenate([kvn_ref[:, c_v:c_v + HEAD_DIM], zpad], axis=0).astype(BF16)
            sn = lax.dot_general(qq, kn, _NT, preferred_element_type=F32) * SCALE
            last_sel = sel_scr[h, n_sel_chunks - 1]
            nb_last = (past // SLC_LEN) % BLK_PER_CHUNK
            seln = jnp.sum(jnp.where(lax.broadcasted_iota(jnp.int32, (ds, HEAD_DIM), 1) == nb_last,
                                     last_sel, 0.0), axis=1, keepdims=True)
            seln = jnp.concatenate([seln] * GROUP, axis=0)
            srow = lax.broadcasted_iota(jnp.int32, (rq, HEAD_DIM), 0) & (ds - 1)
            kcol = lax.broadcasted_iota(jnp.int32, (rq, HEAD_DIM), 1)
            okn = (seln > 0.5) & (kcol <= srow) & (kcol < ds)
            online_update(h, jnp.where(okn, sn, NEG), vn)
            o_slc = acc_scr[h] / l_scr[h]
            wb = st_ref.shape[0]
            band = wb + HEAD_DIM
            st_flat = st_ref.reshape(wb * 8, HEAD_DIM)
            kw = jnp.concatenate([st_flat[pl.ds(h, wb, stride=8), :],
                                  wn_ref[:, h * HEAD_DIM:(h + 1) * HEAD_DIM], zpad], axis=0).astype(BF16)
            vw = jnp.concatenate([st_flat[pl.ds(N_KV + h, wb, stride=8), :],
                                  wn_ref[:, KV_W + h * HEAD_DIM:KV_W + (h + 1) * HEAD_DIM], zpad],
                                 axis=0).astype(BF16)
            sw = lax.dot_general(qq, kw, _NT, preferred_element_type=F32) * SCALE
            srw = lax.broadcasted_iota(jnp.int32, (rq, band), 0) & (ds - 1)
            idx = lax.broadcasted_iota(jnp.int32, (rq, band), 1)
            dlt = (wb + srw) - idx
            sw = jnp.where((dlt >= 0) & (dlt < WINDOW) & (idx < wb + ds), sw, NEG)
            ew = jnp.exp(sw - jnp.max(sw, axis=1, keepdims=True))
            o_win = jnp.dot(ew.astype(BF16), vw, preferred_element_type=F32) / jnp.sum(ew, axis=1, keepdims=True)
            o_cmp = ocmp_scr[h]
            gh = gt[h]
            for g in range(GROUP):
                r0, r1 = g * ds, (g + 1) * ds
                og = gh[:, g:g + 1] * o_cmp[r0:r1] + gh[:, GROUP + g:GROUP + g + 1] * o_slc[r0:r1]
                og = og + gh[:, 2 * GROUP + g:2 * GROUP + g + 1] * o_win[r0:r1]
                c0 = (h * GROUP + g) * HEAD_DIM
                o_ref[:, c0:c0 + HEAD_DIM] = og
        wb = st_ref.shape[0]
        wso_ref[0:wb - ds] = st_ref[ds:wb]
        wso_flat = wso_ref.reshape(wb * 8, HEAD_DIM)
        for cb in range(2 * N_KV):
            wso_flat[pl.ds((wb - ds) * 8 + cb, ds, stride=8), :] = wn_ref[:, cb * HEAD_DIM:(cb + 1) * HEAD_DIM]


def _attn_sample(q, kcv, kv_new, win_new, win_state, g_re, cache4, page_table, *, ds):
    nb, npg = page_table.shape
    past = npg * PAGE
    assert ds & (ds - 1) == 0 and ds <= SLC_LEN and past % SLC_LEN == 0 and npg % PAGES_PER_STEP == 0
    steps = npg // PAGES_PER_STEP
    ncp = kcv.shape[3]
    wb = win_state.shape[1]
    n_blocks = past // SLC_LEN + 1
    n_sel_chunks = -(-n_blocks // BLK_PER_CHUNK)
    m = _overlap_matrix(ncp, n_sel_chunks * BLK_PER_CHUNK, ncp - 1, n_blocks)
    ms = np.zeros((ncp, n_sel_chunks, HEAD_DIM), np.float32)
    ms[:, :, :BLK_PER_CHUNK] = m.reshape(ncp, n_sel_chunks, BLK_PER_CHUNK)
    ms = jnp.asarray(ms.reshape(ncp, n_sel_chunks * HEAD_DIM), dtype=BF16)
    ee = np.zeros((HEAD_DIM, CHUNK_ROWS), np.float32)
    ee[:BLK_PER_CHUNK] = np.arange(CHUNK_ROWS)[None, :] // SLC_LEN == np.arange(BLK_PER_CHUNK)[:, None]
    ee = jnp.asarray(ee, dtype=BF16)
    rq = GROUP * ds

    grid_spec = pltpu.PrefetchScalarGridSpec(
        num_scalar_prefetch=1,
        grid=(nb, steps),
        in_specs=[
            pl.BlockSpec(memory_space=pl.ANY),
            pl.BlockSpec((ds, N_HEADS * HEAD_DIM), lambda b, p, pt: (b, 0)),
            pl.BlockSpec((None, 2, N_KV, ncp, HEAD_DIM), lambda b, p, pt: (b, 0, 0, 0, 0)),
            pl.BlockSpec((ds, 4 * KV_W), lambda b, p, pt: (b, 0)),
            pl.BlockSpec((ds, 2 * KV_W), lambda b, p, pt: (b, 0)),
            pl.BlockSpec((None, wb, 8, HEAD_DIM), lambda b, p, pt: (b, 0, 0, 0)),
            pl.BlockSpec((None, N_KV, ds, 4 * GROUP), lambda b, p, pt: (b, 0, 0, 0)),
            pl.BlockSpec(ms.shape, lambda b, p, pt: (0, 0)),
            pl.BlockSpec(ee.shape, lambda b, p, pt: (0, 0))],
        out_specs=[pl.BlockSpec((ds, N_HEADS * HEAD_DIM), lambda b, p, pt: (b, 0)),
                   pl.BlockSpec((None, wb, 8, HEAD_DIM), lambda b, p, pt: (b, 0, 0, 0))],
        scratch_shapes=[pltpu.VMEM((N_KV, rq, 1), F32), pltpu.VMEM((N_KV, rq, 1), F32),
                        pltpu.VMEM((N_KV, rq, HEAD_DIM), F32),
                        pltpu.VMEM((N_KV, n_sel_chunks, ds, HEAD_DIM), F32),
                        pltpu.VMEM((N_KV, rq, HEAD_DIM), F32),
                        pltpu.VMEM((2, 2 * N_KV, CHUNK_ROWS, HEAD_DIM), F32),
                        pltpu.SemaphoreType.DMA((2,))],
    )
    return pl.pallas_call(
        functools.partial(_attn_sample_kernel, past=past, ds=ds),
        grid_spec=grid_spec,
        out_shape=[jax.ShapeDtypeStruct((nb * ds, N_HEADS * HEAD_DIM), F32),
                   jax.ShapeDtypeStruct((nb, wb, 8, HEAD_DIM), F32)],
        compiler_params=_cparams(("arbitrary", "arbitrary")),
        name="attn_sample",
    )(page_table, cache4, q, kcv, kv_new, win_new, win_state, g_re, ms, ee)


_Q0, _KV0, _WIN0, _G0 = 0, 2048, 4096, 5120
_REST0 = _G0 + 3 * N_HEADS


def _split_w_in(w_in):
    wt = w_in.T
    return wt, wt[:_G0 + HEAD_DIM].astype(BF16)


def _gate_layout(g_logits, nb, seq):
    g = g_logits[:, :3 * N_HEADS].reshape(nb, seq, 3, N_KV, GROUP)
    g = g.transpose(0, 3, 1, 2, 4).reshape(nb, N_KV, seq, 3 * GROUP)
    return jnp.pad(g, ((0, 0), (0, 0), (0, 0), (0, GROUP)))


def _cmp_weights(w_k1, w_k2, pe_k, w_v1, w_v2, pe_v):
    def by_row(w1):
        w = w1.reshape(2, CMP_STRIDE, HEAD_DIM, HEAD_DIM).transpose(1, 2, 0, 3)
        return w.reshape(CMP_STRIDE * HEAD_DIM, 2 * HEAD_DIM)
    w1pairs = jnp.concatenate([by_row(w_k1), by_row(w_v1)], axis=1).astype(BF16)
    pe8 = jnp.stack([jnp.broadcast_to(pe_k.reshape(1, -1), (8, CMP_LEN * HEAD_DIM)),
                     jnp.broadcast_to(pe_v.reshape(1, -1), (8, CMP_LEN * HEAD_DIM))])
    return w1pairs, pe8, jnp.stack([w_k1, w_v1]), jnp.stack([w_k2, w_v2])


def kernel(x_prompt, x_sample, c_prompt, c_sample, cache_nsa_kv, page_table, state_win_kv, state_conv, w_ada, b_ada, norm1_g, norm2_g, w_in, w_cmp_k1, w_cmp_k2, pe_cmp_k, w_cmp_v1, w_cmp_v2, pe_cmp_v, conv_w, conv_b, w_attn_proj, w_conv_proj, w_out, w_mlp1, w_mlp2, normf_g):
    d = D_MODEL
    nbp, seq, _ = x_prompt.shape
    nbs, ds, _ = x_sample.shape
    tp = nbp * seq
    ts = nbs * ds
    tc = 256
    depth = w_in.shape[0]
    assert depth == 1

    xp = x_prompt.reshape(tp, d)
    xs = x_sample.reshape(ts, d)
    normf = normf_g.reshape(1, d)

    l = 0
    c_all = jnp.concatenate([c_prompt, c_sample, jnp.zeros((16 - nbp - nbs, d), F32)], axis=0)
    ada = _ada(c_all, w_ada[l], b_ada[l].reshape(1, -1)).reshape(16, 6, d)
    ada_p = [ada[:nbp, k][:, None, :] for k in range(6)]
    ada_s = [jnp.repeat(ada[nbp:nbp + nbs, k], ds, axis=0)[None] for k in range(6)]

    w_t, w_head = _split_w_in(w_in[l])
    gates0 = _REST0 + 3 * d
    wa = w_attn_proj[l]
    wc = w_conv_proj[l]
    wo = w_out[l]
    g1n = norm1_g[l].reshape(1, d)
    g2n = norm2_g[l].reshape(1, d)
    cw = conv_w[l]
    cb = conv_b[l].reshape(1, d)
    w1pairs, pe8, w1s, w2s = _cmp_weights(w_cmp_k1[l], w_cmp_k2[l], pe_cmp_k[l],
                                          w_cmp_v1[l], w_cmp_v2[l], pe_cmp_v[l])

    tm = 1024
    hp, q_p = _norm_mm(xp, g1n, ada_p[1], ada_p[0], w_head, tm=tm, tn=1024, seq=seq, out_dtype=BF16,
                       col0=_Q0, n=N_HEADS * HEAD_DIM, name="norm_mm_q")
    wbp = min(WINDOW, seq)
    kv6_p, kvb_p = _mm_kv(hp, w_head, tm=tm, col0=_KV0)
    winb_p, win6_p, g_p = _mm_win(hp, w_head, tm=wbp, seq=seq, col0=_WIN0, gcol0=_G0)
    gates_p = _mm_wstat(hp, w_t, tm=tm, tn=1024, out_dtype=BF16, act="sigmoid", name="mm_gates",
                        col0=gates0, n=2 * d)
    z_p, conv_tiles, w1, w2 = _mm_conv(hp, w_t, cw, cb, w_mlp1[l], w_mlp2[l], tm=tm, tc=tc, seq=seq, col0=_REST0)

    pt_p = jnp.arange(tp // PAGE, dtype=jnp.int32).reshape(nbp, seq // PAGE)
    part_p = _cmp_part(kv6_p.reshape(tp // PAGE, PAGE, 16, HEAD_DIM), pt_p, w1pairs)
    kcv_p = _cmp_finish(part_p, pe8, w1s, w2s)
    o_p = _attn_prompt(q_p, kcv_p, kvb_p, winb_p, g_p, nb=nbp, seq=seq)

    mixed_p = _mix(o_p, z_p, wa, wc, gates_p, tm=tm, tn=512)
    x1_p, h2_p = _mm_resid_norm(mixed_p, wo, xp, ada_p[2], g2n, ada_p[4], ada_p[3], tm=512, seq=seq)
    y_p = _mlp_final(h2_p, w1, w2, x1_p, ada_p[5], normf, tm=512, tf=1024, seq=seq)

    hs, q_s = _norm_mm(xs, g1n, ada_s[1], ada_s[0], w_head, tm=ts, tn=512, seq=ds, out_dtype=F32,
                       col0=_Q0, n=N_HEADS * HEAD_DIM, name="norm_mm_q_s")
    kv_s = _mm(hs, w_head, tm=ts, tn=512, out_dtype=F32, name="mm_kv_s", col0=_KV0, n=4 * KV_W)
    win_s = _mm(hs, w_head, tm=ts, tn=512, out_dtype=F32, name="mm_win_s", col0=_WIN0, n=2 * KV_W)
    g_s = _mm(hs, w_head, tm=ts, tn=HEAD_DIM, out_dtype=F32, name="mm_g_s", col0=_G0, n=HEAD_DIM)
    gates_s = _mm_wstat(hs, w_t, tm=ts, tn=512, out_dtype=F32, act="sigmoid", name="mm_gates_s",
                        col0=gates0, n=2 * d)
    ubc_s = _mm_wstat(hs, w_t, tm=ts, tn=512, out_dtype=F32, name="mm_ubc_s", col0=_REST0, n=3 * d)
    st = state_conv[l]
    z_s, cu_s = _sample_conv(ubc_s, jnp.repeat(st[:, 0], ds, axis=0), jnp.repeat(st[:, 1], ds, axis=0),
                             cw, cb, tc=tc, seq=ds)

    cache4 = cache_nsa_kv[l].reshape(cache_nsa_kv.shape[1], PAGE, 16, HEAD_DIM)
    part_s = _cmp_part(cache4, page_table, w1pairs)
    kcv_s = _cmp_finish(part_s, pe8, w1s, w2s)
    wbs = state_win_kv.shape[2]
    wst = state_win_kv[l].reshape(nbs, wbs, 8, HEAD_DIM)
    o_s, wst_next = _attn_sample(q_s, kcv_s, kv_s, win_s, wst, _gate_layout(g_s, nbs, ds), cache4, page_table,
                                 ds=ds)

    mixed_s = _mix(o_s, z_s, wa, wc, gates_s, tm=ts, tn=512)
    x1_s, h2_s = _mm_resid_norm(mixed_s, wo, xs, ada_s[2], g2n, ada_s[4], ada_s[3], tm=ts, seq=ds)
    y_s = _mlp_final(h2_s, w1, w2, x1_s, ada_s[5], normf, tm=ts, tf=512, seq=ds)

    kv_prompt = kv6_p.reshape(1, nbp, seq, 4, N_KV, HEAD_DIM)
    kv_sample = kv_s.reshape(1, nbs, ds, 4, N_KV, HEAD_DIM)
    win_prompt = win6_p.reshape(1, nbp, wbp, 2, N_KV, HEAD_DIM)
    win_sample = wst_next.reshape(1, nbs, wbs, 2, N_KV, HEAD_DIM)
    tiles_per_seq = seq // tm
    conv_prompt = conv_tiles[tiles_per_seq - 1::tiles_per_seq][None]
    conv_sample = cu_s.reshape(nbs, ds, d)[None, :, ds - (CONV_W - 1):]
    return (y_p.reshape(nbp, seq, d), y_s.reshape(nbs, ds, d), kv_prompt, kv_sample,
            win_prompt, win_sample, conv_prompt, conv_sample)
```

```python
import functools

import numpy as np
import jax
import jax.numpy as jnp
from jax import lax
from jax.experimental import pallas as pl
from jax.experimental.pallas import tpu as pltpu

F32 = jnp.float32
BF16 = jnp.bfloat16

D_MODEL = 2048
HEAD_DIM = 128
N_HEADS = 16
N_KV = 4
GROUP = 4
KV_W = N_KV * HEAD_DIM
CMP_LEN = 32
CMP_STRIDE = 16
SLC_LEN = 64
SLC_TOP = 16
N_LOCAL = 2
WINDOW = 512
CONV_W = 3
RMS_EPS = 1e-6
NEG = -1e30
SCALE = HEAD_DIM ** -0.5
PAGE = 128
PAGES_PER_STEP = 16
CHUNK_ROWS = PAGES_PER_STEP * PAGE
BLK_PER_CHUNK = CHUNK_ROWS // SLC_LEN
VMEM_LIMIT = 56 * 1024 * 1024

_NT = (((1,), (1,)), ((), ()))
_TN = (((0,), (0,)), ((), ()))


def _cparams(sem):
    return pltpu.CompilerParams(dimension_semantics=sem, vmem_limit_bytes=VMEM_LIMIT)


def _ada_kernel(c_ref, w_ref, b_ref, o_ref):
    c = c_ref[...]
    a = (c * jax.nn.sigmoid(c)).astype(BF16)
    o_ref[...] = jnp.dot(a, w_ref[...].astype(BF16), preferred_element_type=F32) + b_ref[...]


def _ada(c, w, b, tn=1024):
    m, k = c.shape
    n = w.shape[1]
    return pl.pallas_call(
        _ada_kernel,
        grid=(n // tn,),
        in_specs=[pl.BlockSpec((m, k), lambda j: (0, 0)),
                  pl.BlockSpec((k, tn), lambda j: (0, j)),
                  pl.BlockSpec((1, tn), lambda j: (0, j))],
        out_specs=pl.BlockSpec((m, tn), lambda j: (0, j)),
        out_shape=jax.ShapeDtypeStruct((m, n), F32),
        compiler_params=_cparams(("arbitrary",)),
        name="ada",
    )(c, w, b)


def _rowspec(p, tm, tn, seq, col=True):
    gr = p.shape[1]
    if gr == 1:
        if col:
            return pl.BlockSpec((1, 1, tn), lambda i, j: ((i * tm) // seq, 0, j))
        return pl.BlockSpec((1, 1, tn), lambda i: ((i * tm) // seq, 0, 0))
    if col:
        return pl.BlockSpec((1, tm, tn), lambda i, j: (0, i, j))
    return pl.BlockSpec((1, tm, tn), lambda i: (0, i, 0))


def _xwt(x_ref, wt_ref):
    return lax.dot_general(x_ref[...].astype(BF16), wt_ref[...], _NT, preferred_element_type=F32)


def _mm_kernel(x_ref, w_ref, o_ref, *, act):
    acc = _xwt(x_ref, w_ref)
    if act == "sigmoid":
        acc = jax.nn.sigmoid(acc)
    o_ref[...] = acc.astype(o_ref.dtype)


def _norm_mm_kernel(x_ref, g_ref, sc_ref, sh_ref, w_ref, h_ref, o_ref):
    @pl.when(pl.program_id(1) == 0)
    def _():
        x = x_ref[...]
        r = lax.rsqrt(jnp.mean(x * x, axis=-1, keepdims=True) + RMS_EPS)
        y = (x * r) * g_ref[...]
        h_ref[...] = (y * (1.0 + sc_ref[0]) + sh_ref[0]).astype(h_ref.dtype)

    o_ref[...] = _xwt(h_ref, w_ref).astype(o_ref.dtype)


def _norm_mm(x, g, sc, sh, w, *, tm, tn, seq, out_dtype, col0, n, name):
    t, d = x.shape
    assert col0 % tn == 0 and n % tn == 0
    cb0 = col0 // tn
    row = lambda p: (pl.BlockSpec((1, 1, d), lambda i, j: ((i * tm) // seq, 0, 0)) if p.shape[1] == 1
                     else pl.BlockSpec((1, tm, d), lambda i, j: (0, i, 0)))
    return pl.pallas_call(
        _norm_mm_kernel,
        grid=(t // tm, n // tn),
        in_specs=[pl.BlockSpec((tm, d), lambda i, j: (i, 0)),
                  pl.BlockSpec((1, d), lambda i, j: (0, 0)),
                  row(sc), row(sh),
                  pl.BlockSpec((tn, d), lambda i, j: (cb0 + j, 0))],
        out_specs=[pl.BlockSpec((tm, d), lambda i, j: (i, 0)),
                   pl.BlockSpec((tm, tn), lambda i, j: (i, j))],
        out_shape=[jax.ShapeDtypeStruct((t, d), BF16), jax.ShapeDtypeStruct((t, n), out_dtype)],
        compiler_params=_cparams(("arbitrary", "arbitrary")),
        name=name,
    )(x, g, sc, sh, w)


def _mm_wstat_kernel(x_ref, w_ref, o_ref, wbf_ref, *, act):
    @pl.when(pl.program_id(1) == 0)
    def _():
        wbf_ref[...] = w_ref[...].astype(BF16)

    acc = _xwt(x_ref, wbf_ref)
    if act == "sigmoid":
        acc = jax.nn.sigmoid(acc)
    o_ref[...] = acc.astype(o_ref.dtype)


def _wrows(tn, k, row0):
    return pl.BlockSpec((pl.Element(tn), pl.Element(k)), lambda j, i: (pl.multiple_of(row0 + j * tn, 8), 0))


def _mm_wstat(x, w, *, tm, tn, out_dtype, act=None, name="mm", col0, n):
    t, k = x.shape
    assert col0 % 8 == 0 and n % tn == 0
    return pl.pallas_call(
        functools.partial(_mm_wstat_kernel, act=act),
        grid=(n // tn, t // tm),
        in_specs=[pl.BlockSpec((tm, k), lambda j, i: (i, 0)),
                  _wrows(tn, k, col0)],
        out_specs=pl.BlockSpec((tm, tn), lambda j, i: (i, j)),
        out_shape=jax.ShapeDtypeStruct((t, n), out_dtype),
        scratch_shapes=[pltpu.VMEM((tn, k), BF16)],
        compiler_params=_cparams(("arbitrary", "arbitrary")),
        name=name,
    )(x, w)


def _mm(x, w, *, tm, tn, out_dtype, act=None, name="mm", col0, n):
    t, k = x.shape
    assert col0 % tn == 0 and n % tn == 0
    cb0 = col0 // tn
    return pl.pallas_call(
        functools.partial(_mm_kernel, act=act),
        grid=(t // tm, n // tn),
        in_specs=[pl.BlockSpec((tm, k), lambda i, j: (i, 0)),
                  pl.BlockSpec((tn, k), lambda i, j: (cb0 + j, 0))],
        out_specs=pl.BlockSpec((tm, tn), lambda i, j: (i, j)),
        out_shape=jax.ShapeDtypeStruct((t, n), out_dtype),
        compiler_params=_cparams(("arbitrary", "arbitrary")),
        name=name,
    )(x, w)


def _store_head_major(o_ref, acc, tm):
    flat = o_ref.reshape(tm * 8, HEAD_DIM)
    for cb in range(8):
        flat[pl.ds(cb, tm, stride=8), :] = acc[:, cb * HEAD_DIM:(cb + 1) * HEAD_DIM]


def _mm_kv_kernel(x_ref, w_ref, o6_ref, ob_ref, *, tm):
    acc = _xwt(x_ref, w_ref)
    _store_head_major(o6_ref, acc, tm)

    @pl.when(pl.program_id(1) == 1)
    def _():
        ob_ref[...] = acc.astype(BF16)


def _mm_kv(h, w, *, tm, col0):
    t, k = h.shape
    tn = 2 * KV_W
    cb0 = col0 // tn
    return pl.pallas_call(
        functools.partial(_mm_kv_kernel, tm=tm),
        grid=(t // tm, 2),
        in_specs=[pl.BlockSpec((tm, k), lambda i, j: (i, 0)),
                  pl.BlockSpec((tn, k), lambda i, j: (cb0 + j, 0))],
        out_specs=[pl.BlockSpec((tm, 8, HEAD_DIM), lambda i, j: (i, j, 0)),
                   pl.BlockSpec((tm, tn), lambda i, j: (i, 0))],
        out_shape=[jax.ShapeDtypeStruct((t, 16, HEAD_DIM), F32),
                   jax.ShapeDtypeStruct((t, tn), BF16)],
        compiler_params=_cparams(("arbitrary", "arbitrary")),
        name="mm_kv",
    )(h, w)


def _mm_win_kernel(x_ref, w_ref, wg_ref, ob_ref, o6_ref, og_ref, *, tm, seq):
    acc = _xwt(x_ref, w_ref)
    ob_ref[...] = acc.astype(BF16)
    gl = _xwt(x_ref, wg_ref)
    for hd in range(N_KV):
        og_ref[hd] = jnp.concatenate(
            [gl[:, br * N_HEADS + hd * GROUP:br * N_HEADS + (hd + 1) * GROUP] for br in range(3)]
            + [jnp.zeros((tm, GROUP), F32)], axis=1)

    @pl.when(((pl.program_id(0) + 1) * tm) % seq == 0)
    def _():
        _store_head_major(o6_ref, acc, tm)


def _mm_win(h, w, *, tm, seq, col0, gcol0):
    t, k = h.shape
    tn = 2 * KV_W
    cb0 = col0 // tn
    gb0 = gcol0 // HEAD_DIM
    return pl.pallas_call(
        functools.partial(_mm_win_kernel, tm=tm, seq=seq),
        grid=(t // tm,),
        in_specs=[pl.BlockSpec((tm, k), lambda i: (i, 0)),
                  pl.BlockSpec((tn, k), lambda i: (cb0, 0)),
                  pl.BlockSpec((HEAD_DIM, k), lambda i: (gb0, 0))],
        out_specs=[pl.BlockSpec((tm, tn), lambda i: (i, 0)),
                   pl.BlockSpec((tm, 8, HEAD_DIM), lambda i: ((i * tm) // seq, 0, 0)),
                   pl.BlockSpec((None, N_KV, tm, 4 * GROUP), lambda i: ((i * tm) // seq, 0, (i * tm % seq) // tm, 0))],
        out_shape=[jax.ShapeDtypeStruct((t, tn), BF16),
                   jax.ShapeDtypeStruct((t // seq * tm, 8, HEAD_DIM), F32),
                   jax.ShapeDtypeStruct((t // seq, N_KV, seq, 4 * GROUP), F32)],
        compiler_params=_cparams(("arbitrary",)),
        name="mm_win",
    )(h, w, w)


def _conv_taps(cu, prev1, prev2, bgate, cw_ref, cb_ref):
    y = cb_ref[...] + prev2 * cw_ref[0:1, :]
    y = y + prev1 * cw_ref[1:2, :]
    y = y + cu * cw_ref[2:3, :]
    return bgate * y


def _mm_conv_kernel(x_ref, wu_ref, wb_ref, wc_ref, cw_ref, cb_ref, ca_ref, cb2_ref, z_ref, st_ref, oa_ref, ob_ref,
                    w3_ref, carry_ref, *, tm, tc, seq):
    i = pl.program_id(1)
    oa_ref[...] = ca_ref[...].astype(BF16)
    ob_ref[...] = cb2_ref[...].astype(BF16)

    @pl.when(i == 0)
    def _():
        w3_ref[0:tc] = wu_ref[...].astype(BF16)
        w3_ref[tc:2 * tc] = wb_ref[...].astype(BF16)
        w3_ref[2 * tc:3 * tc] = wc_ref[...].astype(BF16)

    acc = _xwt(x_ref, w3_ref)
    u = acc[:, 0:tc]
    bgate = acc[:, tc:2 * tc]
    cu = acc[:, 2 * tc:3 * tc] * u

    @pl.when((i * tm) % seq == 0)
    def _():
        carry_ref[...] = jnp.zeros((8, tc), F32)

    car = carry_ref[...]
    p0 = car[0:1, :]
    p1 = car[1:2, :]
    rows = lax.broadcasted_iota(jnp.int32, (tm, tc), 0)
    r1 = pltpu.roll(cu, 1, 0)
    r2 = pltpu.roll(cu, 2, 0)
    prev1 = jnp.where(rows == 0, p1, r1)
    prev2 = jnp.where(rows == 0, p0, jnp.where(rows == 1, p1, r2))
    z_ref[...] = _conv_taps(cu, prev1, prev2, bgate, cw_ref, cb_ref).astype(z_ref.dtype)
    tail = cu[tm - 8:tm, :]
    carry_ref[...] = jnp.concatenate([tail[6:8, :], tail[0:6, :]], axis=0)
    st_ref[0] = tail[6:8, :]


def _mm_conv(h, w, conv_w, conv_b, cast_a, cast_b, *, tm, tc, seq, col0):
    t, k = h.shape
    d = conv_w.shape[1]
    nj = d // tc
    ni = t // tm
    steps = nj * ni
    ra, rb_ = cast_a.shape[0] // steps, cast_b.shape[0] // steps
    assert ra % 16 == 0 and rb_ % 16 == 0 and ra * steps == cast_a.shape[0] and rb_ * steps == cast_b.shape[0]
    slab = lambda r, m: pl.BlockSpec((r, m.shape[1]), lambda j, i: (j * ni + i, 0))
    return pl.pallas_call(
        functools.partial(_mm_conv_kernel, tm=tm, tc=tc, seq=seq),
        grid=(nj, ni),
        in_specs=[pl.BlockSpec((tm, k), lambda j, i: (i, 0)),
                  _wrows(tc, k, col0), _wrows(tc, k, col0 + d), _wrows(tc, k, col0 + 2 * d),
                  pl.BlockSpec((CONV_W, tc), lambda j, i: (0, j)),
                  pl.BlockSpec((1, tc), lambda j, i: (0, j)),
                  slab(ra, cast_a), slab(rb_, cast_b)],
        out_specs=[pl.BlockSpec((tm, tc), lambda j, i: (i, j)),
                   pl.BlockSpec((1, CONV_W - 1, tc), lambda j, i: (i, 0, j)),
                   slab(ra, cast_a), slab(rb_, cast_b)],
        out_shape=[jax.ShapeDtypeStruct((t, d), BF16),
                   jax.ShapeDtypeStruct((t // tm, CONV_W - 1, d), F32),
                   jax.ShapeDtypeStruct(cast_a.shape, BF16),
                   jax.ShapeDtypeStruct(cast_b.shape, BF16)],
        scratch_shapes=[pltpu.VMEM((3 * tc, k), BF16), pltpu.VMEM((8, tc), F32)],
        compiler_params=_cparams(("arbitrary", "arbitrary")),
        name="mm_conv",
    )(h, w, w, w, conv_w, conv_b, cast_a, cast_b)


def _sample_conv_kernel(u_ref, b_ref, c_ref, st0_ref, st1_ref, cw_ref, cb_ref, z_ref, cu_ref, *, tc, seq):
    t = u_ref.shape[0]
    bgate = b_ref[...]
    cu = c_ref[...] * u_ref[...]
    s = lax.rem(lax.broadcasted_iota(jnp.int32, (t, tc), 0), jnp.int32(seq))
    r1 = pltpu.roll(cu, 1, 0)
    r2 = pltpu.roll(cu, 2, 0)
    prev1 = jnp.where(s == 0, st1_ref[...], r1)
    prev2 = jnp.where(s == 0, st0_ref[...], jnp.where(s == 1, st1_ref[...], r2))
    z_ref[...] = _conv_taps(cu, prev1, prev2, bgate, cw_ref, cb_ref)
    cu_ref[...] = cu


def _sample_conv(ubc, st0, st1, conv_w, conv_b, *, tc, seq):
    t = ubc.shape[0]
    d = ubc.shape[1] // 3
    nj = d // tc
    return pl.pallas_call(
        functools.partial(_sample_conv_kernel, tc=tc, seq=seq),
        grid=(nj,),
        in_specs=[pl.BlockSpec((t, tc), lambda j: (0, j)),
                  pl.BlockSpec((t, tc), lambda j: (0, nj + j)),
                  pl.BlockSpec((t, tc), lambda j: (0, 2 * nj + j)),
                  pl.BlockSpec((t, tc), lambda j: (0, j)),
                  pl.BlockSpec((t, tc), lambda j: (0, j)),
                  pl.BlockSpec((CONV_W, tc), lambda j: (0, j)),
                  pl.BlockSpec((1, tc), lambda j: (0, j))],
        out_specs=[pl.BlockSpec((t, tc), lambda j: (0, j)),
                   pl.BlockSpec((t, tc), lambda j: (0, j))],
        out_shape=[jax.ShapeDtypeStruct((t, d), F32), jax.ShapeDtypeStruct((t, d), F32)],
        compiler_params=_cparams(("arbitrary",)),
        name="sample_conv",
    )(ubc, ubc, ubc, st0, st1, conv_w, conv_b)


def _mix_kernel(o_ref, z_ref, wa_ref, wc_ref, ga_ref, gc_ref, m_ref):
    a = jnp.dot(o_ref[...].astype(BF16), wa_ref[...].astype(BF16), preferred_element_type=F32)
    c = jnp.dot(z_ref[...].astype(BF16), wc_ref[...].astype(BF16), preferred_element_type=F32)
    m_ref[...] = (ga_ref[...].astype(F32) * a + gc_ref[...].astype(F32) * c).astype(m_ref.dtype)


def _mix(o, z, wa, wc, gates, *, tm, tn):
    t, k = o.shape
    n = wa.shape[1]
    nj = n // tn
    return pl.pallas_call(
        _mix_kernel,
        grid=(t // tm, nj),
        in_specs=[pl.BlockSpec((tm, k), lambda i, j: (i, 0)),
                  pl.BlockSpec((tm, k), lambda i, j: (i, 0)),
                  pl.BlockSpec((k, tn), lambda i, j: (0, j)),
                  pl.BlockSpec((k, tn), lambda i, j: (0, j)),
                  pl.BlockSpec((tm, tn), lambda i, j: (i, j)),
                  pl.BlockSpec((tm, tn), lambda i, j: (i, j + nj))],
        out_specs=pl.BlockSpec((tm, tn), lambda i, j: (i, j)),
        out_shape=jax.ShapeDtypeStruct((t, n), BF16),
        compiler_params=_cparams(("arbitrary", "arbitrary")),
        name="mix",
    )(o, z, wa, wc, gates, gates)


def _resid_norm_kernel(a_ref, w_ref, x_ref, g_ref, ng_ref, sc_ref, sh_ref, x1_ref, h2_ref, wbf_ref):
    @pl.when(pl.program_id(0) == 0)
    def _():
        wbf_ref[...] = w_ref[...].astype(BF16)

    acc = jnp.dot(a_ref[...], wbf_ref[...], preferred_element_type=F32)
    x1 = x_ref[...] + g_ref[0] * acc
    x1_ref[...] = x1
    r = lax.rsqrt(jnp.mean(x1 * x1, axis=-1, keepdims=True) + RMS_EPS)
    y = (x1 * r) * ng_ref[...]
    h2_ref[...] = (y * (1.0 + sc_ref[0]) + sh_ref[0]).astype(h2_ref.dtype)


def _mm_resid_norm(a, w, x, gate, ng, sc, sh, *, tm, seq):
    t, k = a.shape
    n = w.shape[1]
    row = lambda p: _rowspec(p, tm, n, seq, col=False)
    return pl.pallas_call(
        _resid_norm_kernel,
        grid=(t // tm,),
        in_specs=[pl.BlockSpec((tm, k), lambda i: (i, 0)),
                  pl.BlockSpec((k, n), lambda i: (0, 0), pipeline_mode=pl.Buffered(1)),
                  pl.BlockSpec((tm, n), lambda i: (i, 0)),
                  row(gate),
                  pl.BlockSpec((1, n), lambda i: (0, 0)),
                  row(sc), row(sh)],
        out_specs=[pl.BlockSpec((tm, n), lambda i: (i, 0)), pl.BlockSpec((tm, n), lambda i: (i, 0))],
        out_shape=[jax.ShapeDtypeStruct((t, n), F32), jax.ShapeDtypeStruct((t, n), BF16)],
        scratch_shapes=[pltpu.VMEM((k, n), BF16)],
        compiler_params=_cparams(("arbitrary",)),
        name="mm_resid_norm",
    )(a, w, x, gate, ng, sc, sh)


def _mlp_kernel(h_ref, w1_ref, w2_ref, x_ref, g_ref, nf_ref, o_ref, acc_ref):
    k = pl.program_id(1)

    @pl.when(k == 0)
    def _():
        acc_ref[...] = jnp.zeros_like(acc_ref)

    a = jnp.dot(h_ref[...], w1_ref[...], preferred_element_type=F32)
    a = jnp.square(jnp.maximum(a, 0.0)).astype(BF16)
    acc_ref[...] += jnp.dot(a, w2_ref[...], preferred_element_type=F32)

    @pl.when(k == pl.num_programs(1) - 1)
    def _():
        x2 = x_ref[...] + g_ref[0] * acc_ref[...]
        r = lax.rsqrt(jnp.mean(x2 * x2, axis=-1, keepdims=True) + RMS_EPS)
        o_ref[...] = (x2 * r) * nf_ref[...]


def _mlp_final(h2, w1, w2, x1, gate, normf, *, tm, tf, seq):
    t, d = h2.shape
    f = w1.shape[1]
    return pl.pallas_call(
        _mlp_kernel,
        grid=(t // tm, f // tf),
        in_specs=[pl.BlockSpec((tm, d), lambda i, k: (i, 0)),
                  pl.BlockSpec((d, tf), lambda i, k: (0, k)),
                  pl.BlockSpec((tf, d), lambda i, k: (k, 0)),
                  pl.BlockSpec((tm, d), lambda i, k: (i, 0)),
                  _rowspec_k(gate, tm, d, seq),
                  pl.BlockSpec((1, d), lambda i, k: (0, 0))],
        out_specs=pl.BlockSpec((tm, d), lambda i, k: (i, 0)),
        out_shape=jax.ShapeDtypeStruct((t, d), F32),
        scratch_shapes=[pltpu.VMEM((tm, d), F32)],
        compiler_params=_cparams(("arbitrary", "arbitrary")),
        name="mlp",
    )(h2, w1, w2, x1, gate, normf)


def _rowspec_k(p, tm, d, seq):
    if p.shape[1] == 1:
        return pl.BlockSpec((1, 1, d), lambda i, k: ((i * tm) // seq, 0, 0))
    return pl.BlockSpec((1, tm, d), lambda i, k: (0, i, 0))


def _cmp_part_kernel(pt_ref, *refs):
    pages = refs[:PAGES_PER_STEP]
    w_ref = refs[PAGES_PER_STEP]
    o_ref = refs[PAGES_PER_STEP + 1]
    stage = refs[PAGES_PER_STEP + 2]
    nchunk = CHUNK_ROWS // CMP_STRIDE
    cpp = PAGE // CMP_STRIDE
    nslab = 2 * N_KV
    tiles = [[pg[pl.ds(r, cpp, stride=CMP_STRIDE)] for pg in pages] for r in range(CMP_STRIDE)]
    for kind in range(2):
        cols = [jnp.concatenate([x[:, kind * N_KV:(kind + 1) * N_KV, :].reshape(cpp * N_KV, HEAD_DIM)
                                 for x in tiles[r]], axis=0).astype(BF16) for r in range(CMP_STRIDE)]
        lhs = jnp.concatenate(cols, axis=1)
        acc = jnp.dot(lhs, w_ref[:, kind * 2 * HEAD_DIM:(kind + 1) * 2 * HEAD_DIM],
                      preferred_element_type=F32)
        stage[0, 0:nchunk * N_KV] = acc[:, :HEAD_DIM]
        stage[1, 0:nchunk * N_KV] = acc[:, HEAD_DIM:]
        for h in range(N_KV):
            o_ref[kind, h] = jnp.concatenate(
                [stage[0, pl.ds(h, nchunk, stride=N_KV), :], stage[1, pl.ds(h, nchunk, stride=N_KV), :]], axis=1)


def _page_spec(n, kind_pair):
    return pl.BlockSpec((None, PAGE, 8, HEAD_DIM),
                        lambda b, p, pt, n=n: (pt[b, p * PAGES_PER_STEP + n], 0, kind_pair, 0))


def _cmp_part(pages4, page_table, w1pairs):
    nb, npg = page_table.shape
    steps = npg // PAGES_PER_STEP
    nchunk = PAGES_PER_STEP * (PAGE // CMP_STRIDE)
    grid_spec = pltpu.PrefetchScalarGridSpec(
        num_scalar_prefetch=1,
        grid=(nb, steps),
        in_specs=[_page_spec(n, 0) for n in range(PAGES_PER_STEP)]
        + [pl.BlockSpec((CMP_STRIDE * HEAD_DIM, 4 * HEAD_DIM), lambda b, p, pt: (0, 0))],
        out_specs=pl.BlockSpec((None, 2, N_KV, nchunk, 2 * HEAD_DIM), lambda b, p, pt: (b, 0, 0, p, 0)),
        scratch_shapes=[pltpu.VMEM((2, nchunk * 2 * N_KV, HEAD_DIM), F32)],
    )
    return pl.pallas_call(
        _cmp_part_kernel,
        grid_spec=grid_spec,
        out_shape=jax.ShapeDtypeStruct((nb, 2, N_KV, steps * nchunk, 2 * HEAD_DIM), F32),
        compiler_params=_cparams(("arbitrary", "arbitrary")),
        name="cmp_part",
    )(page_table, *([pages4] * PAGES_PER_STEP), w1pairs)


def _gelu_tanh(x):
    c = np.sqrt(2.0 / np.pi).astype(np.float32)
    return 0.5 * x * (1.0 + jnp.tanh(c * (x + 0.044715 * (x * x * x))))


def _cmp_finish_kernel(part_ref, pe_ref, w1_ref, w2_ref, o_ref, bias_ref):
    @pl.when(pl.program_id(1) == 0)
    def _():
        bias_ref[...] = jnp.dot(pe_ref[...], w1_ref[...], preferred_element_type=F32,
                                precision=lax.Precision.HIGHEST)

    w2 = w2_ref[...].astype(BF16)
    for hd in range(N_KV):
        part = part_ref[hd]
        n = part.shape[0]
        h = part[:, 0:HEAD_DIM] + pltpu.roll(part[:, HEAD_DIM:], n - 1, 0)
        h = h + bias_ref[0:1, :]
        o_ref[hd] = jnp.dot(_gelu_tanh(h).astype(BF16), w2, preferred_element_type=F32).astype(o_ref.dtype)


def _cmp_finish(part, pe8, w1, w2):
    nb, _, _, nchunk, _ = part.shape
    return pl.pallas_call(
        _cmp_finish_kernel,
        grid=(2, nb),
        in_specs=[pl.BlockSpec((None, None, N_KV, nchunk, 2 * HEAD_DIM), lambda k, b: (b, k, 0, 0, 0)),
                  pl.BlockSpec((None, 8, CMP_LEN * HEAD_DIM), lambda k, b: (k, 0, 0)),
                  pl.BlockSpec((None, CMP_LEN * HEAD_DIM, HEAD_DIM), lambda k, b: (k, 0, 0)),
                  pl.BlockSpec((None, HEAD_DIM, HEAD_DIM), lambda k, b: (k, 0, 0))],
        out_specs=pl.BlockSpec((None, None, N_KV, nchunk, HEAD_DIM), lambda k, b: (b, k, 0, 0, 0)),
        out_shape=jax.ShapeDtypeStruct((nb, 2, N_KV, nchunk, HEAD_DIM), BF16),
        scratch_shapes=[pltpu.VMEM((8, HEAD_DIM), F32)],
        compiler_params=_cparams(("arbitrary", "arbitrary")),
        name="cmp_finish",
    )(part, pe8, w1, w2)


def _select_topk(score, valid, forced, blk, axis):
    sc = jnp.where(valid, jnp.where(forced, jnp.inf, score), -jnp.inf)
    sel = jnp.zeros(score.shape, F32)
    big = jnp.int32(1 << 20)
    for _ in range(SLC_TOP):
        mx = jnp.max(sc, axis=axis, keepdims=True)
        idx = jnp.min(jnp.where(sc == mx, blk, big), axis=axis, keepdims=True)
        hit = blk == idx
        sel = jnp.where(hit & (mx > -jnp.inf), 1.0, sel)
        sc = jnp.where(hit, -jnp.inf, sc)
    return sel


def _overlap_matrix(nc_pad, n_blocks_pad, nc, n_blocks):
    cs = np.arange(nc_pad) * CMP_STRIDE
    sb = np.arange(n_blocks_pad) * SLC_LEN
    ov = np.clip(np.minimum(cs[:, None] + CMP_LEN, sb[None, :] + SLC_LEN)
                 - np.maximum(cs[:, None], sb[None, :]), 0, None)
    m = (ov / CMP_STRIDE).astype(np.float32)
    m[nc:, :] = 0.0
    m[:, n_blocks:] = 0.0
    return m


def _attn_prompt_kernel(q_ref, kcb, vcb, ksb, vsb, kwb, vwb, g_ref, mt_ref, e_ref, o_ref,
                        m_scr, l_scr, acc_scr, bias_scr, gate_scr, *, tq, seq, ck):
    qi = pl.program_id(2)
    nc = seq // CMP_STRIDE - 1
    nsb = seq // SLC_LEN
    rb = min(tq, 128)

    q = q_ref[...]
    qq = jnp.concatenate([q[:, g * HEAD_DIM:(g + 1) * HEAD_DIM] for g in range(GROUP)], axis=0)
    rq = GROUP * tq
    q0 = qi * tq

    gt = jax.nn.sigmoid(g_ref[...])
    ngc = 4 * GROUP
    hi = gt.astype(BF16)
    r1 = gt - hi.astype(F32)
    mid = r1.astype(BF16)
    lo = (r1 - mid.astype(F32)).astype(BF16)
    g3 = jnp.concatenate([hi, mid, lo, jnp.zeros((tq, HEAD_DIM - 3 * ngc), BF16)], axis=1)
    srow = lax.broadcasted_iota(jnp.int32, (HEAD_DIM, 3 * GROUP * HEAD_DIM), 0)
    scol = lax.broadcasted_iota(jnp.int32, (HEAD_DIM, 3 * GROUP * HEAD_DIM), 1)
    spread = ((srow < 3 * ngc) & ((srow & (ngc - 1)) == lax.div(scol, jnp.int32(HEAD_DIM)))).astype(BF16)
    gl = jnp.dot(g3, spread, preferred_element_type=F32)
    for c in range(3 * GROUP):
        gate_scr[c] = gl[:, c * HEAD_DIM:(c + 1) * HEAD_DIM]

    ncp = kcb.shape[0]
    s = lax.dot_general(qq, kcb[...], _NT, preferred_element_type=F32) * SCALE
    qpos = q0 + (lax.broadcasted_iota(jnp.int32, (rq, ncp), 0) & (tq - 1))
    col = lax.broadcasted_iota(jnp.int32, (rq, ncp), 1)
    vis = (col * CMP_STRIDE + (CMP_LEN - 1) <= qpos) & (col < nc)
    s = jnp.where(vis, s, NEG)
    mx = jnp.max(s, axis=1, keepdims=True)
    e = jnp.where(vis, jnp.exp(s - mx), 0.0)
    den = jnp.sum(e, axis=1, keepdims=True)
    p = e / jnp.where(den > 0.0, den, 1.0)
    o_cmp = jnp.dot(p.astype(BF16), vcb[...], preferred_element_type=F32)
    pg = p[0:tq]
    for g in range(1, GROUP):
        pg = pg + p[g * tq:(g + 1) * tq]

    sct = lax.dot_general(mt_ref[...], pg, _NT, preferred_element_type=F32,
                          precision=lax.Precision.HIGHEST)
    blk = lax.broadcasted_iota(jnp.int32, (nsb, tq), 0)
    cur = lax.div(q0 + lax.broadcasted_iota(jnp.int32, (nsb, tq), 1), jnp.int32(SLC_LEN))
    valid = blk <= cur
    forced = (blk == 0) | (blk > cur - N_LOCAL)
    need_topk = (q0 + tq - 1) // SLC_LEN + 1 > SLC_TOP
    sel_t = lax.cond(need_topk,
                     lambda: _select_topk(sct, valid, forced, blk, 0),
                     lambda: valid.astype(F32)).astype(BF16)

    def sweep(k_ref, v_ref, k_start, n_chunks, bias_fn):
        nl = ck // HEAD_DIM

        def scores(c):
            k0 = pl.multiple_of(k_start + c * ck, 128)
            return lax.dot_general(qq, k_ref[pl.ds(k0, ck), :], _NT, preferred_element_type=F32)

        def chunk_inputs(c):
            k0 = pl.multiple_of(k_start + c * ck, 128)
            return scores(c), bias_fn(c, k0)

        c_exp = np.float32(SCALE * np.log2(np.e))

        def reduce_chunk(c, sb, first):
            sc_, bias = sb
            k0 = pl.multiple_of(k_start + c * ck, 128)
            vch = jnp.concatenate([v_ref[pl.ds(k0, ck), :], jnp.ones((ck, HEAD_DIM), BF16)], axis=1)
            alphas, ps = [], []
            nblk = rq // rb
            for r in range(nblk):
                rows = slice(r * rb, (r + 1) * rb)
                b0 = (r * rb) % tq
                sg = [sc_[rows, j * HEAD_DIM:(j + 1) * HEAD_DIM]
                      + bias[b0:b0 + rb, j * HEAD_DIM:(j + 1) * HEAD_DIM] for j in range(nl)]
                mx = functools.reduce(jnp.maximum, sg)
                row_max = jnp.max(mx, axis=1, keepdims=True)
                if first:
                    m_new = jnp.broadcast_to(row_max, (rb, HEAD_DIM))
                else:
                    m_old = m_scr[rows]
                    m_new = jnp.maximum(m_old, row_max)
                    alphas.append(jnp.exp2((m_old - m_new) * c_exp))
                pj = [jnp.exp2((x - m_new) * c_exp) for x in sg]
                m_scr[rows] = m_new
                ps.append(jnp.concatenate([x.astype(BF16) for x in pj], axis=1))
                if (r + 1) % (nblk // 2) == 0:
                    half = slice((r + 1 - nblk // 2) * rb, (r + 1) * rb)
                    pv = jnp.dot(jnp.concatenate(ps, axis=0), vch, preferred_element_type=F32)
                    if first:
                        acc_scr[half] = pv[:, :HEAD_DIM]
                        l_scr[half] = pv[:, HEAD_DIM:]
                    else:
                        al = jnp.concatenate(alphas, axis=0)
                        acc_scr[half] = al * acc_scr[half] + pv[:, :HEAD_DIM]
                        l_scr[half] = al * l_scr[half] + pv[:, HEAD_DIM:]
                    alphas, ps = [], []

        def body(c, carry):
            reduce_chunk(c, chunk_inputs(c), False)
            return carry

        reduce_chunk(0, chunk_inputs(0), True)
        lax.fori_loop(1, n_chunks, body, 0)
        return acc_scr[...] / l_scr[...]

    qp = q0 + lax.broadcasted_iota(jnp.int32, (tq, ck), 0)
    kcol = lax.broadcasted_iota(jnp.int32, (tq, ck), 1)

    eye = (lax.broadcasted_iota(jnp.int32, (nsb, HEAD_DIM), 0)
           == lax.broadcasted_iota(jnp.int32, (nsb, HEAD_DIM), 1)).astype(BF16)
    sel_q = lax.dot_general(sel_t, eye, _TN, preferred_element_type=F32).astype(BF16)

    n_slc = (q0 + tq + ck - 1) // ck

    def put_mask(c):
        selq = jnp.dot(sel_q, e_ref[c], preferred_element_type=F32)
        bias_scr[c] = jnp.where((selq > 0.5) & (c * ck + kcol <= qp), 0.0, NEG)

    def mask_body(i, carry):
        put_mask(2 * i)
        put_mask(jnp.minimum(2 * i + 1, seq // ck - 1))
        return carry

    lax.fori_loop(0, (n_slc + 1) // 2, mask_body, 0)
    o_slc = sweep(ksb, vsb, 0, n_slc, lambda c, k0: bias_scr[c])

    span = -(-(WINDOW + tq) // ck) * ck

    def win_bias(c, k0):
        dlt = qp - (k0 + kcol)
        return jnp.where((dlt >= 0) & (dlt < WINDOW), 0.0, NEG)

    o_win = sweep(kwb, vwb, jnp.maximum(q0 + tq - span, 0), span // ck, win_bias)

    for g in range(GROUP):
        r0, r1 = g * tq, (g + 1) * tq
        og = gate_scr[g] * o_cmp[r0:r1] + gate_scr[GROUP + g] * o_slc[r0:r1]
        og = og + gate_scr[2 * GROUP + g] * o_win[r0:r1]
        o_ref[:, g * HEAD_DIM:(g + 1) * HEAD_DIM] = og.astype(o_ref.dtype)


def _attn_prompt(q, kcv, kv, win, g_re, *, nb, seq, tq=512, ck=512):
    assert tq & (tq - 1) == 0 and seq % ck == 0 and WINDOW % tq == 0
    t = q.shape[0]
    ncp = seq // CMP_STRIDE
    nsb = seq // SLC_LEN
    nq = seq // tq
    mt = jnp.asarray(_overlap_matrix(ncp, nsb, ncp - 1, nsb).T)
    kk = np.arange(seq)
    assert nsb <= HEAD_DIM
    e3 = (kk[None, :] // SLC_LEN == np.arange(HEAD_DIM)[:, None]).astype(np.float32)
    e3 = jnp.asarray(e3.reshape(HEAD_DIM, seq // ck, ck).transpose(1, 0, 2), dtype=BF16)
    slab = lambda cb: pl.BlockSpec((seq, HEAD_DIM), lambda b, h, i, cb=cb: (b, cb + h))
    return pl.pallas_call(
        functools.partial(_attn_prompt_kernel, tq=tq, seq=seq, ck=ck),
        grid=(nb, N_KV, nq),
        in_specs=[pl.BlockSpec((tq, GROUP * HEAD_DIM), lambda b, h, i: (b * nq + i, h)),
                  pl.BlockSpec((None, None, None, ncp, HEAD_DIM), lambda b, h, i: (b, 0, h, 0, 0)),
                  pl.BlockSpec((None, None, None, ncp, HEAD_DIM), lambda b, h, i: (b, 1, h, 0, 0)),
                  slab(0), slab(N_KV), slab(0), slab(N_KV),
                  pl.BlockSpec((None, None, tq, 4 * GROUP), lambda b, h, i: (b, h, i, 0)),
                  pl.BlockSpec((nsb, ncp), lambda b, h, i: (0, 0)),
                  pl.BlockSpec((seq // ck, HEAD_DIM, ck), lambda b, h, i: (0, 0, 0))],
        out_specs=pl.BlockSpec((tq, GROUP * HEAD_DIM), lambda b, h, i: (b * nq + i, h)),
        out_shape=jax.ShapeDtypeStruct((t, N_HEADS * HEAD_DIM), BF16),
        scratch_shapes=[pltpu.VMEM((GROUP * tq, HEAD_DIM), F32)] * 3
        + [pltpu.VMEM((seq // ck, tq, ck), F32), pltpu.VMEM((3 * GROUP, tq, HEAD_DIM), F32)],
        compiler_params=_cparams(("arbitrary", "arbitrary", "arbitrary")),
        name="attn_prompt",
    )(q, kcv, kcv, kv, kv, win, win, g_re, mt, e3)


def _attn_sample_kernel(pt_ref, cache_ref, q_ref, kcv_ref, kvn_ref, wn_ref, st_ref, g_ref, ms_ref, e_ref, o_ref,
                        wso_ref, m_scr, l_scr, acc_scr, sel_scr, ocmp_scr, kv_buf, sem, *, past, ds):
    b = pl.program_id(0)
    p = pl.program_id(1)
    n_steps = pl.num_programs(1)

    def page_copies(bb, pp, slot, for_wait=False):
        cps = []
        for n in range(PAGES_PER_STEP):
            page = 0 if for_wait else pt_ref[bb, pp * PAGES_PER_STEP + n]
            for cb in range(2 * N_KV):
                cps.append(pltpu.make_async_copy(cache_ref.at[page, :, 2 * N_KV + cb, :],
                                                 kv_buf.at[slot, cb, pl.ds(n * PAGE, PAGE), :],
                                                 sem.at[slot]))
        return cps

    t = b * n_steps + p
    slot = lax.rem(t, 2)

    @pl.when(t == 0)
    def _():
        for cp in page_copies(b, p, slot):
            cp.start()

    @pl.when(t + 1 < pl.num_programs(0) * n_steps)
    def _():
        wrap = p + 1 == n_steps
        for cp in page_copies(jnp.where(wrap, b + 1, b), jnp.where(wrap, 0, p + 1), 1 - slot):
            cp.start()
    rq = GROUP * ds
    ncp = kcv_ref.shape[2]
    nc = ncp - 1
    n_sel_chunks = sel_scr.shape[1]
    lanes = ms_ref.shape[1]

    def q_rows(h):
        c0 = h * GROUP * HEAD_DIM
        return jnp.concatenate(
            [q_ref[:, c0 + g * HEAD_DIM:c0 + (g + 1) * HEAD_DIM] for g in range(GROUP)], axis=0).astype(BF16)

    def online_update(h, sc_, vals):
        m_i = m_scr[h]
        m_new = jnp.maximum(m_i, jnp.max(sc_, axis=1, keepdims=True))
        alpha = jnp.exp(m_i - m_new)
        pp = jnp.exp(sc_ - m_new)
        l_scr[h] = alpha * l_scr[h] + jnp.sum(pp, axis=1, keepdims=True)
        acc_scr[h] = alpha * acc_scr[h] + jnp.dot(pp.astype(BF16), vals, preferred_element_type=F32)
        m_scr[h] = m_new

    @pl.when(p == 0)
    def _():
        pgs = []
        for h in range(N_KV):
            qq = q_rows(h)
            kc = kcv_ref[0, h]
            vc = kcv_ref[1, h]
            s = lax.dot_general(qq, kc, _NT, preferred_element_type=F32) * SCALE
            qpos = past + (lax.broadcasted_iota(jnp.int32, (rq, ncp), 0) & (ds - 1))
            col = lax.broadcasted_iota(jnp.int32, (rq, ncp), 1)
            vis = (col * CMP_STRIDE + (CMP_LEN - 1) <= qpos) & (col < nc)
            s = jnp.where(vis, s, NEG)
            mx = jnp.max(s, axis=1, keepdims=True)
            e = jnp.where(vis, jnp.exp(s - mx), 0.0)
            den = jnp.sum(e, axis=1, keepdims=True)
            pr = e / jnp.where(den > 0.0, den, 1.0)
            ocmp_scr[h] = jnp.dot(pr.astype(BF16), vc, preferred_element_type=F32)
            pg = pr[0:ds]
            for g in range(1, GROUP):
                pg = pg + pr[g * ds:(g + 1) * ds]
            pgs.append(pg)
            m_scr[h] = jnp.full((rq, 1), NEG, F32)
            l_scr[h] = jnp.zeros((rq, 1), F32)
            acc_scr[h] = jnp.zeros((rq, HEAD_DIM), F32)
        nr = N_KV * ds
        pg_all = jnp.concatenate(pgs, axis=0)
        hi = pg_all.astype(BF16)
        r1 = pg_all - hi.astype(F32)
        mid = r1.astype(BF16)
        lo = (r1 - mid.astype(F32)).astype(BF16)
        sc3 = jnp.dot(jnp.concatenate([hi, mid, lo], axis=0), ms_ref[...], preferred_element_type=F32)
        score = (sc3[0:nr] + sc3[nr:2 * nr]) + sc3[2 * nr:3 * nr]
        blk = lax.broadcasted_iota(jnp.int32, (nr, lanes), 1)
        qrow = lax.broadcasted_iota(jnp.int32, (nr, lanes), 0) & (ds - 1)
        cur = lax.div(past + qrow, jnp.int32(SLC_LEN))
        valid = blk <= cur
        forced = (blk == 0) | (blk > cur - N_LOCAL)
        sel = _select_topk(score, valid, forced, blk, 1)
        lane_pad = jnp.zeros((ds, HEAD_DIM - BLK_PER_CHUNK), F32)
        for h in range(N_KV):
            for c in range(n_sel_chunks):
                sel_scr[h, c] = jnp.concatenate(
                    [sel[h * ds:(h + 1) * ds, c * BLK_PER_CHUNK:(c + 1) * BLK_PER_CHUNK], lane_pad], axis=1)

    for cp in page_copies(b, p, slot, for_wait=True):
        cp.wait()

    k0 = p * CHUNK_ROWS
    for h in range(N_KV):
        qq = q_rows(h)
        kch = kv_buf[slot, h].astype(BF16)
        vch = kv_buf[slot, N_KV + h].astype(BF16)
        sc_ = lax.dot_general(qq, kch, _NT, preferred_element_type=F32) * SCALE
        selq = jnp.dot(sel_scr[h, p].astype(BF16), e_ref[...], preferred_element_type=F32)
        kpos = k0 + lax.broadcasted_iota(jnp.int32, (ds, CHUNK_ROWS), 1)
        qp = past + lax.broadcasted_iota(jnp.int32, (ds, CHUNK_ROWS), 0)
        okf = jnp.where((selq > 0.5) & (kpos <= qp), 1.0, 0.0)
        ok = jnp.concatenate([okf] * GROUP, axis=0) > 0.5
        online_update(h, jnp.where(ok, sc_, NEG), vch)

    @pl.when(p == n_steps - 1)
    def _():
        gt = jax.nn.sigmoid(g_ref[...])
        zpad = jnp.zeros((HEAD_DIM - ds, HEAD_DIM), F32)
        for h in range(N_KV):
            qq = q_rows(h)
            c_k = 2 * KV_W + h * HEAD_DIM
            c_v = 3 * KV_W + h * HEAD_DIM
            kn = jnp.concatenate([kvn_ref[:, c_k:c_k + HEAD_DIM], zpad], axis=0).astype(BF16)
            vn = jnp.concatenate([kvn_ref[:, c_v:c_v + HEAD_DIM], zpad], axis=0).astype(BF16)
            sn = lax.dot_general(qq, kn, _NT, preferred_element_type=F32) * SCALE
            last_sel = sel_scr[h, n_sel_chunks - 1]
            nb_last = (past // SLC_LEN) % BLK_PER_CHUNK
            seln = jnp.sum(jnp.where(lax.broadcasted_iota(jnp.int32, (ds, HEAD_DIM), 1) == nb_last,
                                     last_sel, 0.0), axis=1, keepdims=True)
            seln = jnp.concatenate([seln] * GROUP, axis=0)
            srow = lax.broadcasted_iota(jnp.int32, (rq, HEAD_DIM), 0) & (ds - 1)
            kcol = lax.broadcasted_iota(jnp.int32, (rq, HEAD_DIM), 1)
            okn = (seln > 0.5) & (kcol <= srow) & (kcol < ds)
            online_update(h, jnp.where(okn, sn, NEG), vn)
            o_slc = acc_scr[h] / l_scr[h]
            wb = st_ref.shape[0]
            band = wb + HEAD_DIM
            st_flat = st_ref.reshape(wb * 8, HEAD_DIM)
            kw = jnp.concatenate([st_flat[pl.ds(h, wb, stride=8), :],
                                  wn_ref[:, h * HEAD_DIM:(h + 1) * HEAD_DIM], zpad], axis=0).astype(BF16)
            vw = jnp.concatenate([st_flat[pl.ds(N_KV + h, wb, stride=8), :],
                                  wn_ref[:, KV_W + h * HEAD_DIM:KV_W + (h + 1) * HEAD_DIM], zpad],
                                 axis=0).astype(BF16)
            sw = lax.dot_general(qq, kw, _NT, preferred_element_type=F32) * SCALE
            srw = lax.broadcasted_iota(jnp.int32, (rq, band), 0) & (ds - 1)
            idx = lax.broadcasted_iota(jnp.int32, (rq, band), 1)
            dlt = (wb + srw) - idx
            sw = jnp.where((dlt >= 0) & (dlt < WINDOW) & (idx < wb + ds), sw, NEG)
            ew = jnp.exp(sw - jnp.max(sw, axis=1, keepdims=True))
            o_win = jnp.dot(ew.astype(BF16), vw, preferred_element_type=F32) / jnp.sum(ew, axis=1, keepdims=True)
            o_cmp = ocmp_scr[h]
            gh = gt[h]
            for g in range(GROUP):
                r0, r1 = g * ds, (g + 1) * ds
                og = gh[:, g:g + 1] * o_cmp[r0:r1] + gh[:, GROUP + g:GROUP + g + 1] * o_slc[r0:r1]
                og = og + gh[:, 2 * GROUP + g:2 * GROUP + g + 1] * o_win[r0:r1]
                c0 = (h * GROUP + g) * HEAD_DIM
                o_ref[:, c0:c0 + HEAD_DIM] = og
        wb = st_ref.shape[0]
        wso_ref[0:wb - ds] = st_ref[ds:wb]
        wso_flat = wso_ref.reshape(wb * 8, HEAD_DIM)
        for cb in range(2 * N_KV):
            wso_flat[pl.ds((wb - ds) * 8 + cb, ds, stride=8), :] = wn_ref[:, cb * HEAD_DIM:(cb + 1) * HEAD_DIM]


def _attn_sample(q, kcv, kv_new, win_new, win_state, g_re, cache4, page_table, *, ds):
    nb, npg = page_table.shape
    past = npg * PAGE
    assert ds & (ds - 1) == 0 and ds <= SLC_LEN and past % SLC_LEN == 0 and npg % PAGES_PER_STEP == 0
    steps = npg // PAGES_PER_STEP
    ncp = kcv.shape[3]
    wb = win_state.shape[1]
    n_blocks = past // SLC_LEN + 1
    n_sel_chunks = -(-n_blocks // BLK_PER_CHUNK)
    lanes = -(-n_sel_chunks * BLK_PER_CHUNK // HEAD_DIM) * HEAD_DIM
    ms = jnp.asarray(_overlap_matrix(ncp, lanes, ncp - 1, n_blocks), dtype=BF16)
    ee = np.zeros((HEAD_DIM, CHUNK_ROWS), np.float32)
    ee[:BLK_PER_CHUNK] = np.arange(CHUNK_ROWS)[None, :] // SLC_LEN == np.arange(BLK_PER_CHUNK)[:, None]
    ee = jnp.asarray(ee, dtype=BF16)
    rq = GROUP * ds

    grid_spec = pltpu.PrefetchScalarGridSpec(
        num_scalar_prefetch=1,
        grid=(nb, steps),
        in_specs=[
            pl.BlockSpec(memory_space=pl.ANY),
            pl.BlockSpec((ds, N_HEADS * HEAD_DIM), lambda b, p, pt: (b, 0)),
            pl.BlockSpec((None, 2, N_KV, ncp, HEAD_DIM), lambda b, p, pt: (b, 0, 0, 0, 0)),
            pl.BlockSpec((ds, 4 * KV_W), lambda b, p, pt: (b, 0)),
            pl.BlockSpec((ds, 2 * KV_W), lambda b, p, pt: (b, 0)),
            pl.BlockSpec((None, wb, 8, HEAD_DIM), lambda b, p, pt: (b, 0, 0, 0)),
            pl.BlockSpec((None, N_KV, ds, 4 * GROUP), lambda b, p, pt: (b, 0, 0, 0)),
            pl.BlockSpec(ms.shape, lambda b, p, pt: (0, 0)),
            pl.BlockSpec(ee.shape, lambda b, p, pt: (0, 0))],
        out_specs=[pl.BlockSpec((ds, N_HEADS * HEAD_DIM), lambda b, p, pt: (b, 0)),
                   pl.BlockSpec((None, wb, 8, HEAD_DIM), lambda b, p, pt: (b, 0, 0, 0))],
        scratch_shapes=[pltpu.VMEM((N_KV, rq, 1), F32), pltpu.VMEM((N_KV, rq, 1), F32),
                        pltpu.VMEM((N_KV, rq, HEAD_DIM), F32),
                        pltpu.VMEM((N_KV, n_sel_chunks, ds, HEAD_DIM), F32),
                        pltpu.VMEM((N_KV, rq, HEAD_DIM), F32),
                        pltpu.VMEM((2, 2 * N_KV, CHUNK_ROWS, HEAD_DIM), F32),
                        pltpu.SemaphoreType.DMA((2,))],
    )
    return pl.pallas_call(
        functools.partial(_attn_sample_kernel, past=past, ds=ds),
        grid_spec=grid_spec,
        out_shape=[jax.ShapeDtypeStruct((nb * ds, N_HEADS * HEAD_DIM), F32),
                   jax.ShapeDtypeStruct((nb, wb, 8, HEAD_DIM), F32)],
        compiler_params=_cparams(("arbitrary", "arbitrary")),
        name="attn_sample",
    )(page_table, cache4, q, kcv, kv_new, win_new, win_state, g_re, ms, ee)


_Q0, _KV0, _WIN0, _G0 = 0, 2048, 4096, 5120
_REST0 = _G0 + 3 * N_HEADS


def _split_w_in(w_in):
    wt = w_in.T
    return wt, wt[:_G0 + HEAD_DIM].astype(BF16)


def _gate_layout(g_logits, nb, seq):
    g = g_logits[:, :3 * N_HEADS].reshape(nb, seq, 3, N_KV, GROUP)
    g = g.transpose(0, 3, 1, 2, 4).reshape(nb, N_KV, seq, 3 * GROUP)
    return jnp.pad(g, ((0, 0), (0, 0), (0, 0), (0, GROUP)))


def _cmp_weights(w_k1, w_k2, pe_k, w_v1, w_v2, pe_v):
    def by_row(w1):
        w = w1.reshape(2, CMP_STRIDE, HEAD_DIM, HEAD_DIM).transpose(1, 2, 0, 3)
        return w.reshape(CMP_STRIDE * HEAD_DIM, 2 * HEAD_DIM)
    w1pairs = jnp.concatenate([by_row(w_k1), by_row(w_v1)], axis=1).astype(BF16)
    pe8 = jnp.stack([jnp.broadcast_to(pe_k.reshape(1, -1), (8, CMP_LEN * HEAD_DIM)),
                     jnp.broadcast_to(pe_v.reshape(1, -1), (8, CMP_LEN * HEAD_DIM))])
    return w1pairs, pe8, jnp.stack([w_k1, w_v1]), jnp.stack([w_k2, w_v2])


def kernel(x_prompt, x_sample, c_prompt, c_sample, cache_nsa_kv, page_table, state_win_kv, state_conv, w_ada, b_ada, norm1_g, norm2_g, w_in, w_cmp_k1, w_cmp_k2, pe_cmp_k, w_cmp_v1, w_cmp_v2, pe_cmp_v, conv_w, conv_b, w_attn_proj, w_conv_proj, w_out, w_mlp1, w_mlp2, normf_g):
    d = D_MODEL
    nbp, seq, _ = x_prompt.shape
    nbs, ds, _ = x_sample.shape
    tp = nbp * seq
    ts = nbs * ds
    tc = 256
    depth = w_in.shape[0]
    assert depth == 1

    xp = x_prompt.reshape(tp, d)
    xs = x_sample.reshape(ts, d)
    normf = normf_g.reshape(1, d)

    l = 0
    c_all = jnp.concatenate([c_prompt, c_sample, jnp.zeros((16 - nbp - nbs, d), F32)], axis=0)
    ada = _ada(c_all, w_ada[l], b_ada[l].reshape(1, -1)).reshape(16, 6, d)
    ada_p = [ada[:nbp, k][:, None, :] for k in range(6)]
    ada_s = [jnp.repeat(ada[nbp:nbp + nbs, k], ds, axis=0)[None] for k in range(6)]

    w_t, w_head = _split_w_in(w_in[l])
    gates0 = _REST0 + 3 * d
    wa = w_attn_proj[l]
    wc = w_conv_proj[l]
    wo = w_out[l]
    g1n = norm1_g[l].reshape(1, d)
    g2n = norm2_g[l].reshape(1, d)
    cw = conv_w[l]
    cb = conv_b[l].reshape(1, d)
    w1pairs, pe8, w1s, w2s = _cmp_weights(w_cmp_k1[l], w_cmp_k2[l], pe_cmp_k[l],
                                          w_cmp_v1[l], w_cmp_v2[l], pe_cmp_v[l])

    tm = 1024
    hp, q_p = _norm_mm(xp, g1n, ada_p[1], ada_p[0], w_head, tm=tm, tn=1024, seq=seq, out_dtype=BF16,
                       col0=_Q0, n=N_HEADS * HEAD_DIM, name="norm_mm_q")
    wbp = min(WINDOW, seq)
    kv6_p, kvb_p = _mm_kv(hp, w_head, tm=tm, col0=_KV0)
    winb_p, win6_p, g_p = _mm_win(hp, w_head, tm=wbp, seq=seq, col0=_WIN0, gcol0=_G0)
    gates_p = _mm_wstat(hp, w_t, tm=tm, tn=1024, out_dtype=BF16, act="sigmoid", name="mm_gates",
                        col0=gates0, n=2 * d)
    z_p, conv_tiles, w1, w2 = _mm_conv(hp, w_t, cw, cb, w_mlp1[l], w_mlp2[l], tm=tm, tc=tc, seq=seq, col0=_REST0)

    pt_p = jnp.arange(tp // PAGE, dtype=jnp.int32).reshape(nbp, seq // PAGE)
    part_p = _cmp_part(kv6_p.reshape(tp // PAGE, PAGE, 16, HEAD_DIM), pt_p, w1pairs)
    kcv_p = _cmp_finish(part_p, pe8, w1s, w2s)
    o_p = _attn_prompt(q_p, kcv_p, kvb_p, winb_p, g_p, nb=nbp, seq=seq)

    mixed_p = _mix(o_p, z_p, wa, wc, gates_p, tm=tm, tn=512)
    x1_p, h2_p = _mm_resid_norm(mixed_p, wo, xp, ada_p[2], g2n, ada_p[4], ada_p[3], tm=512, seq=seq)
    y_p = _mlp_final(h2_p, w1, w2, x1_p, ada_p[5], normf, tm=512, tf=1024, seq=seq)

    hs, q_s = _norm_mm(xs, g1n, ada_s[1], ada_s[0], w_head, tm=ts, tn=512, seq=ds, out_dtype=F32,
                       col0=_Q0, n=N_HEADS * HEAD_DIM, name="norm_mm_q_s")
    kv_s = _mm(hs, w_head, tm=ts, tn=512, out_dtype=F32, name="mm_kv_s", col0=_KV0, n=4 * KV_W)
    win_s = _mm(hs, w_head, tm=ts, tn=512, out_dtype=F32, name="mm_win_s", col0=_WIN0, n=2 * KV_W)
    g_s = _mm(hs, w_head, tm=ts, tn=HEAD_DIM, out_dtype=F32, name="mm_g_s", col0=_G0, n=HEAD_DIM)
    gates_s = _mm_wstat(hs, w_t, tm=ts, tn=512, out_dtype=F32, act="sigmoid", name="mm_gates_s",
                        col0=gates0, n=2 * d)
    ubc_s = _mm_wstat(hs, w_t, tm=ts, tn=512, out_dtype=F32, name="mm_ubc_s", col0=_REST0, n=3 * d)
    st = state_conv[l]
    z_s, cu_s = _sample_conv(ubc_s, jnp.repeat(st[:, 0], ds, axis=0), jnp.repeat(st[:, 1], ds, axis=0),
                             cw, cb, tc=tc, seq=ds)

    cache4 = cache_nsa_kv[l].reshape(cache_nsa_kv.shape[1], PAGE, 16, HEAD_DIM)
    part_s = _cmp_part(cache4, page_table, w1pairs)
    kcv_s = _cmp_finish(part_s, pe8, w1s, w2s)
    wbs = state_win_kv.shape[2]
    wst = state_win_kv[l].reshape(nbs, wbs, 8, HEAD_DIM)
    o_s, wst_next = _attn_sample(q_s, kcv_s, kv_s, win_s, wst, _gate_layout(g_s, nbs, ds), cache4, page_table,
                                 ds=ds)

    mixed_s = _mix(o_s, z_s, wa, wc, gates_s, tm=ts, tn=512)
    x1_s, h2_s = _mm_resid_norm(mixed_s, wo, xs, ada_s[2], g2n, ada_s[4], ada_s[3], tm=ts, seq=ds)
    y_s = _mlp_final(h2_s, w1, w2, x1_s, ada_s[5], normf, tm=ts, tf=512, seq=ds)

    kv_prompt = kv6_p.reshape(1, nbp, seq, 4, N_KV, HEAD_DIM)
    kv_sample = kv_s.reshape(1, nbs, ds, 4, N_KV, HEAD_DIM)
    win_prompt = win6_p.reshape(1, nbp, wbp, 2, N_KV, HEAD_DIM)
    win_sample = wst_next.reshape(1, nbs, wbs, 2, N_KV, HEAD_DIM)
    tiles_per_seq = seq // tm
    conv_prompt = conv_tiles[tiles_per_seq - 1::tiles_per_seq][None]
    conv_sample = cu_s.reshape(nbs, ds, d)[None, :, ds - (CONV_W - 1):]
    return (y_p.reshape(nbp, seq, d), y_s.reshape(nbs, ds, d), kv_prompt, kv_sample,
            win_prompt, win_sample, conv_prompt, conv_sample)
```

```python
import functools

import numpy as np
import jax
import jax.numpy as jnp
from jax import lax
from jax.experimental import pallas as pl
from jax.experimental.pallas import tpu as pltpu

F32 = jnp.float32
BF16 = jnp.bfloat16

D_MODEL = 2048
HEAD_DIM = 128
N_HEADS = 16
N_KV = 4
GROUP = 4
KV_W = N_KV * HEAD_DIM
CMP_LEN = 32
CMP_STRIDE = 16
SLC_LEN = 64
SLC_TOP = 16
N_LOCAL = 2
WINDOW = 512
CONV_W = 3
RMS_EPS = 1e-6
NEG = -1e30
SCALE = HEAD_DIM ** -0.5
PAGE = 128
PAGES_PER_STEP = 16
CHUNK_ROWS = PAGES_PER_STEP * PAGE
BLK_PER_CHUNK = CHUNK_ROWS // SLC_LEN
V7X_VMEM_BYTES = 64 * 1024 * 1024
VMEM_LIMIT = V7X_VMEM_BYTES - 8 * 1024 * 1024

_NT = (((1,), (1,)), ((), ()))
_TN = (((0,), (0,)), ((), ()))


def _cparams(sem):
    return pltpu.CompilerParams(dimension_semantics=sem, vmem_limit_bytes=VMEM_LIMIT)


def _ada_kernel(c_ref, w_ref, b_ref, o_ref):
    c = c_ref[...]
    a = (c * jax.nn.sigmoid(c)).astype(BF16)
    o_ref[...] = jnp.dot(a, w_ref[...].astype(BF16), preferred_element_type=F32) + b_ref[...]


def _ada(c, w, b, tn=1024):
    m, k = c.shape
    n = w.shape[1]
    return pl.pallas_call(
        _ada_kernel,
        grid=(n // tn,),
        in_specs=[pl.BlockSpec((m, k), lambda j: (0, 0)),
                  pl.BlockSpec((k, tn), lambda j: (0, j)),
                  pl.BlockSpec((1, tn), lambda j: (0, j))],
        out_specs=pl.BlockSpec((m, tn), lambda j: (0, j)),
        out_shape=jax.ShapeDtypeStruct((m, n), F32),
        compiler_params=_cparams(("arbitrary",)),
        name="ada",
    )(c, w, b)


def _rowspec(p, tm, tn, seq, col=True):
    gr = p.shape[1]
    if gr == 1:
        if col:
            return pl.BlockSpec((1, 1, tn), lambda i, j: ((i * tm) // seq, 0, j))
        return pl.BlockSpec((1, 1, tn), lambda i: ((i * tm) // seq, 0, 0))
    if col:
        return pl.BlockSpec((1, tm, tn), lambda i, j: (0, i, j))
    return pl.BlockSpec((1, tm, tn), lambda i: (0, i, 0))


def _xwt(x_ref, wt_ref):
    return lax.dot_general(x_ref[...].astype(BF16), wt_ref[...], _NT, preferred_element_type=F32)


def _mm_kernel(x_ref, w_ref, o_ref, *, act):
    acc = _xwt(x_ref, w_ref)
    if act == "sigmoid":
        acc = jax.nn.sigmoid(acc)
    o_ref[...] = acc.astype(o_ref.dtype)


def _norm_mm_kernel(x_ref, g_ref, sc_ref, sh_ref, w_ref, h_ref, o_ref):
    @pl.when(pl.program_id(1) == 0)
    def _():
        x = x_ref[...]
        r = lax.rsqrt(jnp.mean(x * x, axis=-1, keepdims=True) + RMS_EPS)
        y = (x * r) * g_ref[...]
        h_ref[...] = (y * (1.0 + sc_ref[0]) + sh_ref[0]).astype(h_ref.dtype)

    o_ref[...] = _xwt(h_ref, w_ref).astype(o_ref.dtype)


def _norm_mm(x, g, sc, sh, w, *, tm, tn, seq, out_dtype, col0, n, name):
    t, d = x.shape
    assert col0 % tn == 0 and n % tn == 0
    cb0 = col0 // tn
    row = lambda p: (pl.BlockSpec((1, 1, d), lambda i, j: ((i * tm) // seq, 0, 0)) if p.shape[1] == 1
                     else pl.BlockSpec((1, tm, d), lambda i, j: (0, i, 0)))
    return pl.pallas_call(
        _norm_mm_kernel,
        grid=(t // tm, n // tn),
        in_specs=[pl.BlockSpec((tm, d), lambda i, j: (i, 0)),
                  pl.BlockSpec((1, d), lambda i, j: (0, 0)),
                  row(sc), row(sh),
                  pl.BlockSpec((tn, d), lambda i, j: (cb0 + j, 0))],
        out_specs=[pl.BlockSpec((tm, d), lambda i, j: (i, 0)),
                   pl.BlockSpec((tm, tn), lambda i, j: (i, j))],
        out_shape=[jax.ShapeDtypeStruct((t, d), BF16), jax.ShapeDtypeStruct((t, n), out_dtype)],
        compiler_params=_cparams(("arbitrary", "arbitrary")),
        name=name,
    )(x, g, sc, sh, w)


def _mm_wstat_kernel(x_ref, w_ref, o_ref, wbf_ref, *, act):
    @pl.when(pl.program_id(1) == 0)
    def _():
        wbf_ref[...] = w_ref[...].astype(BF16)

    acc = _xwt(x_ref, wbf_ref)
    if act == "sigmoid":
        acc = jax.nn.sigmoid(acc)
    o_ref[...] = acc.astype(o_ref.dtype)


def _wrows(tn, k, row0):
    return pl.BlockSpec((pl.Element(tn), pl.Element(k)), lambda j, i: (pl.multiple_of(row0 + j * tn, 8), 0))


def _mm_wstat(x, w, *, tm, tn, out_dtype, act=None, name="mm", col0, n):
    t, k = x.shape
    assert col0 % 8 == 0 and n % tn == 0
    return pl.pallas_call(
        functools.partial(_mm_wstat_kernel, act=act),
        grid=(n // tn, t // tm),
        in_specs=[pl.BlockSpec((tm, k), lambda j, i: (i, 0)),
                  _wrows(tn, k, col0)],
        out_specs=pl.BlockSpec((tm, tn), lambda j, i: (i, j)),
        out_shape=jax.ShapeDtypeStruct((t, n), out_dtype),
        scratch_shapes=[pltpu.VMEM((tn, k), BF16)],
        compiler_params=_cparams(("arbitrary", "arbitrary")),
        name=name,
    )(x, w)


def _mm(x, w, *, tm, tn, out_dtype, act=None, name="mm", col0, n):
    t, k = x.shape
    assert col0 % tn == 0 and n % tn == 0
    cb0 = col0 // tn
    return pl.pallas_call(
        functools.partial(_mm_kernel, act=act),
        grid=(t // tm, n // tn),
        in_specs=[pl.BlockSpec((tm, k), lambda i, j: (i, 0)),
                  pl.BlockSpec((tn, k), lambda i, j: (cb0 + j, 0))],
        out_specs=pl.BlockSpec((tm, tn), lambda i, j: (i, j)),
        out_shape=jax.ShapeDtypeStruct((t, n), out_dtype),
        compiler_params=_cparams(("arbitrary", "arbitrary")),
        name=name,
    )(x, w)


def _store_head_major(o_ref, acc, tm):
    flat = o_ref.reshape(tm * 8, HEAD_DIM)
    for cb in range(8):
        flat[pl.ds(cb, tm, stride=8), :] = acc[:, cb * HEAD_DIM:(cb + 1) * HEAD_DIM]


def _mm_kv_kernel(x_ref, w_ref, o6_ref, ob_ref, *, tm):
    acc = _xwt(x_ref, w_ref)
    _store_head_major(o6_ref, acc, tm)

    @pl.when(pl.program_id(1) == 1)
    def _():
        ob_ref[...] = acc.astype(BF16)


def _mm_kv(h, w, *, tm, col0):
    t, k = h.shape
    tn = 2 * KV_W
    cb0 = col0 // tn
    return pl.pallas_call(
        functools.partial(_mm_kv_kernel, tm=tm),
        grid=(t // tm, 2),
        in_specs=[pl.BlockSpec((tm, k), lambda i, j: (i, 0)),
                  pl.BlockSpec((tn, k), lambda i, j: (cb0 + j, 0))],
        out_specs=[pl.BlockSpec((tm, 8, HEAD_DIM), lambda i, j: (i, j, 0)),
                   pl.BlockSpec((tm, tn), lambda i, j: (i, 0))],
        out_shape=[jax.ShapeDtypeStruct((t, 16, HEAD_DIM), F32),
                   jax.ShapeDtypeStruct((t, tn), BF16)],
        compiler_params=_cparams(("arbitrary", "arbitrary")),
        name="mm_kv",
    )(h, w)


def _mm_win_kernel(x_ref, w_ref, wg_ref, ob_ref, o6_ref, og_ref, *, tm, seq):
    acc = _xwt(x_ref, w_ref)
    ob_ref[...] = acc.astype(BF16)
    gl = _xwt(x_ref, wg_ref)
    for hd in range(N_KV):
        og_ref[hd] = jnp.concatenate(
            [gl[:, br * N_HEADS + hd * GROUP:br * N_HEADS + (hd + 1) * GROUP] for br in range(3)]
            + [jnp.zeros((tm, GROUP), F32)], axis=1)

    @pl.when(((pl.program_id(0) + 1) * tm) % seq == 0)
    def _():
        _store_head_major(o6_ref, acc, tm)


def _mm_win(h, w, *, tm, seq, col0, gcol0):
    t, k = h.shape
    tn = 2 * KV_W
    cb0 = col0 // tn
    gb0 = gcol0 // HEAD_DIM
    return pl.pallas_call(
        functools.partial(_mm_win_kernel, tm=tm, seq=seq),
        grid=(t // tm,),
        in_specs=[pl.BlockSpec((tm, k), lambda i: (i, 0)),
                  pl.BlockSpec((tn, k), lambda i: (cb0, 0)),
                  pl.BlockSpec((HEAD_DIM, k), lambda i: (gb0, 0))],
        out_specs=[pl.BlockSpec((tm, tn), lambda i: (i, 0)),
                   pl.BlockSpec((tm, 8, HEAD_DIM), lambda i: ((i * tm) // seq, 0, 0)),
                   pl.BlockSpec((None, N_KV, tm, 4 * GROUP), lambda i: ((i * tm) // seq, 0, (i * tm % seq) // tm, 0))],
        out_shape=[jax.ShapeDtypeStruct((t, tn), BF16),
                   jax.ShapeDtypeStruct((t // seq * tm, 8, HEAD_DIM), F32),
                   jax.ShapeDtypeStruct((t // seq, N_KV, seq, 4 * GROUP), F32)],
        compiler_params=_cparams(("arbitrary",)),
        name="mm_win",
    )(h, w, w)


def _conv_taps(cu, prev1, prev2, bgate, cw_ref, cb_ref):
    y = cb_ref[...] + prev2 * cw_ref[0:1, :]
    y = y + prev1 * cw_ref[1:2, :]
    y = y + cu * cw_ref[2:3, :]
    return bgate * y


def _mm_conv_kernel(x_ref, wu_ref, wb_ref, wc_ref, cw_ref, cb_ref, ca_ref, cb2_ref, z_ref, st_ref, oa_ref, ob_ref,
                    w3_ref, carry_ref, *, tm, tc, seq):
    i = pl.program_id(1)
    oa_ref[...] = ca_ref[...].astype(BF16)
    ob_ref[...] = cb2_ref[...].astype(BF16)

    @pl.when(i == 0)
    def _():
        w3_ref[0:tc] = wu_ref[...].astype(BF16)
        w3_ref[tc:2 * tc] = wb_ref[...].astype(BF16)
        w3_ref[2 * tc:3 * tc] = wc_ref[...].astype(BF16)

    acc = _xwt(x_ref, w3_ref)
    u = acc[:, 0:tc]
    bgate = acc[:, tc:2 * tc]
    cu = acc[:, 2 * tc:3 * tc] * u

    @pl.when((i * tm) % seq == 0)
    def _():
        carry_ref[...] = jnp.zeros((8, tc), F32)

    car = carry_ref[...]
    p0 = car[0:1, :]
    p1 = car[1:2, :]
    rows = lax.broadcasted_iota(jnp.int32, (tm, tc), 0)
    r1 = pltpu.roll(cu, 1, 0)
    r2 = pltpu.roll(cu, 2, 0)
    prev1 = jnp.where(rows == 0, p1, r1)
    prev2 = jnp.where(rows == 0, p0, jnp.where(rows == 1, p1, r2))
    z_ref[...] = _conv_taps(cu, prev1, prev2, bgate, cw_ref, cb_ref).astype(z_ref.dtype)
    tail = cu[tm - 8:tm, :]
    carry_ref[...] = jnp.concatenate([tail[6:8, :], tail[0:6, :]], axis=0)
    st_ref[0] = tail[6:8, :]


def _mm_conv(h, w, conv_w, conv_b, cast_a, cast_b, *, tm, tc, seq, col0):
    t, k = h.shape
    d = conv_w.shape[1]
    nj = d // tc
    ni = t // tm
    steps = nj * ni
    ra, rb_ = cast_a.shape[0] // steps, cast_b.shape[0] // steps
    assert ra % 16 == 0 and rb_ % 16 == 0 and ra * steps == cast_a.shape[0] and rb_ * steps == cast_b.shape[0]
    slab = lambda r, m: pl.BlockSpec((r, m.shape[1]), lambda j, i: (j * ni + i, 0))
    return pl.pallas_call(
        functools.partial(_mm_conv_kernel, tm=tm, tc=tc, seq=seq),
        grid=(nj, ni),
        in_specs=[pl.BlockSpec((tm, k), lambda j, i: (i, 0)),
                  _wrows(tc, k, col0), _wrows(tc, k, col0 + d), _wrows(tc, k, col0 + 2 * d),
                  pl.BlockSpec((CONV_W, tc), lambda j, i: (0, j)),
                  pl.BlockSpec((1, tc), lambda j, i: (0, j)),
                  slab(ra, cast_a), slab(rb_, cast_b)],
        out_specs=[pl.BlockSpec((tm, tc), lambda j, i: (i, j)),
                   pl.BlockSpec((1, CONV_W - 1, tc), lambda j, i: (i, 0, j)),
                   slab(ra, cast_a), slab(rb_, cast_b)],
        out_shape=[jax.ShapeDtypeStruct((t, d), BF16),
                   jax.ShapeDtypeStruct((t // tm, CONV_W - 1, d), F32),
                   jax.ShapeDtypeStruct(cast_a.shape, BF16),
                   jax.ShapeDtypeStruct(cast_b.shape, BF16)],
        scratch_shapes=[pltpu.VMEM((3 * tc, k), BF16), pltpu.VMEM((8, tc), F32)],
        compiler_params=_cparams(("arbitrary", "arbitrary")),
        name="mm_conv",
    )(h, w, w, w, conv_w, conv_b, cast_a, cast_b)


def _sample_conv_kernel(u_ref, b_ref, c_ref, st0_ref, st1_ref, cw_ref, cb_ref, z_ref, cu_ref, *, tc, seq):
    t = u_ref.shape[0]
    bgate = b_ref[...]
    cu = c_ref[...] * u_ref[...]
    s = lax.rem(lax.broadcasted_iota(jnp.int32, (t, tc), 0), jnp.int32(seq))
    r1 = pltpu.roll(cu, 1, 0)
    r2 = pltpu.roll(cu, 2, 0)
    prev1 = jnp.where(s == 0, st1_ref[...], r1)
    prev2 = jnp.where(s == 0, st0_ref[...], jnp.where(s == 1, st1_ref[...], r2))
    z_ref[...] = _conv_taps(cu, prev1, prev2, bgate, cw_ref, cb_ref)
    cu_ref[...] = cu


def _sample_conv(ubc, st0, st1, conv_w, conv_b, *, tc, seq):
    t = ubc.shape[0]
    d = ubc.shape[1] // 3
    nj = d // tc
    return pl.pallas_call(
        functools.partial(_sample_conv_kernel, tc=tc, seq=seq),
        grid=(nj,),
        in_specs=[pl.BlockSpec((t, tc), lambda j: (0, j)),
                  pl.BlockSpec((t, tc), lambda j: (0, nj + j)),
                  pl.BlockSpec((t, tc), lambda j: (0, 2 * nj + j)),
                  pl.BlockSpec((t, tc), lambda j: (0, j)),
                  pl.BlockSpec((t, tc), lambda j: (0, j)),
                  pl.BlockSpec((CONV_W, tc), lambda j: (0, j)),
                  pl.BlockSpec((1, tc), lambda j: (0, j))],
        out_specs=[pl.BlockSpec((t, tc), lambda j: (0, j)),
                   pl.BlockSpec((t, tc), lambda j: (0, j))],
        out_shape=[jax.ShapeDtypeStruct((t, d), F32), jax.ShapeDtypeStruct((t, d), F32)],
        compiler_params=_cparams(("arbitrary",)),
        name="sample_conv",
    )(ubc, ubc, ubc, st0, st1, conv_w, conv_b)


def _mix_kernel(o_ref, z_ref, wa_ref, wc_ref, ga_ref, gc_ref, m_ref):
    a = jnp.dot(o_ref[...].astype(BF16), wa_ref[...].astype(BF16), preferred_element_type=F32)
    c = jnp.dot(z_ref[...].astype(BF16), wc_ref[...].astype(BF16), preferred_element_type=F32)
    m_ref[...] = (ga_ref[...].astype(F32) * a + gc_ref[...].astype(F32) * c).astype(m_ref.dtype)


def _mix(o, z, wa, wc, gates, *, tm, tn):
    t, k = o.shape
    n = wa.shape[1]
    nj = n // tn
    return pl.pallas_call(
        _mix_kernel,
        grid=(t // tm, nj),
        in_specs=[pl.BlockSpec((tm, k), lambda i, j: (i, 0)),
                  pl.BlockSpec((tm, k), lambda i, j: (i, 0)),
                  pl.BlockSpec((k, tn), lambda i, j: (0, j)),
                  pl.BlockSpec((k, tn), lambda i, j: (0, j)),
                  pl.BlockSpec((tm, tn), lambda i, j: (i, j)),
                  pl.BlockSpec((tm, tn), lambda i, j: (i, j + nj))],
        out_specs=pl.BlockSpec((tm, tn), lambda i, j: (i, j)),
        out_shape=jax.ShapeDtypeStruct((t, n), BF16),
        compiler_params=_cparams(("arbitrary", "arbitrary")),
        name="mix",
    )(o, z, wa, wc, gates, gates)


def _resid_norm_kernel(a_ref, w_ref, x_ref, g_ref, ng_ref, sc_ref, sh_ref, x1_ref, h2_ref, wbf_ref):
    @pl.when(pl.program_id(0) == 0)
    def _():
        wbf_ref[...] = w_ref[...].astype(BF16)

    acc = jnp.dot(a_ref[...], wbf_ref[...], preferred_element_type=F32)
    x1 = x_ref[...] + g_ref[0] * acc
    x1_ref[...] = x1
    r = lax.rsqrt(jnp.mean(x1 * x1, axis=-1, keepdims=True) + RMS_EPS)
    y = (x1 * r) * ng_ref[...]
    h2_ref[...] = (y * (1.0 + sc_ref[0]) + sh_ref[0]).astype(h2_ref.dtype)


def _mm_resid_norm(a, w, x, gate, ng, sc, sh, *, tm, seq):
    t, k = a.shape
    n = w.shape[1]
    row = lambda p: _rowspec(p, tm, n, seq, col=False)
    return pl.pallas_call(
        _resid_norm_kernel,
        grid=(t // tm,),
        in_specs=[pl.BlockSpec((tm, k), lambda i: (i, 0)),
                  pl.BlockSpec((k, n), lambda i: (0, 0), pipeline_mode=pl.Buffered(1)),
                  pl.BlockSpec((tm, n), lambda i: (i, 0)),
                  row(gate),
                  pl.BlockSpec((1, n), lambda i: (0, 0)),
                  row(sc), row(sh)],
        out_specs=[pl.BlockSpec((tm, n), lambda i: (i, 0)), pl.BlockSpec((tm, n), lambda i: (i, 0))],
        out_shape=[jax.ShapeDtypeStruct((t, n), F32), jax.ShapeDtypeStruct((t, n), BF16)],
        scratch_shapes=[pltpu.VMEM((k, n), BF16)],
        compiler_params=_cparams(("arbitrary",)),
        name="mm_resid_norm",
    )(a, w, x, gate, ng, sc, sh)


def _mlp_kernel(h_ref, w1_ref, w2_ref, x_ref, g_ref, nf_ref, o_ref, acc_ref):
    k = pl.program_id(1)

    @pl.when(k == 0)
    def _():
        acc_ref[...] = jnp.zeros_like(acc_ref)

    a = jnp.dot(h_ref[...], w1_ref[...], preferred_element_type=F32)
    a = jnp.square(jnp.maximum(a, 0.0)).astype(BF16)
    acc_ref[...] += jnp.dot(a, w2_ref[...], preferred_element_type=F32)

    @pl.when(k == pl.num_programs(1) - 1)
    def _():
        x2 = x_ref[...] + g_ref[0] * acc_ref[...]
        r = lax.rsqrt(jnp.mean(x2 * x2, axis=-1, keepdims=True) + RMS_EPS)
        o_ref[...] = (x2 * r) * nf_ref[...]


def _mlp_final(h2, w1, w2, x1, gate, normf, *, tm, tf, seq):
    t, d = h2.shape
    f = w1.shape[1]
    return pl.pallas_call(
        _mlp_kernel,
        grid=(t // tm, f // tf),
        in_specs=[pl.BlockSpec((tm, d), lambda i, k: (i, 0)),
                  pl.BlockSpec((d, tf), lambda i, k: (0, k)),
                  pl.BlockSpec((tf, d), lambda i, k: (k, 0)),
                  pl.BlockSpec((tm, d), lambda i, k: (i, 0)),
                  _rowspec_k(gate, tm, d, seq),
                  pl.BlockSpec((1, d), lambda i, k: (0, 0))],
        out_specs=pl.BlockSpec((tm, d), lambda i, k: (i, 0)),
        out_shape=jax.ShapeDtypeStruct((t, d), F32),
        scratch_shapes=[pltpu.VMEM((tm, d), F32)],
        compiler_params=_cparams(("arbitrary", "arbitrary")),
        name="mlp",
    )(h2, w1, w2, x1, gate, normf)


def _rowspec_k(p, tm, d, seq):
    if p.shape[1] == 1:
        return pl.BlockSpec((1, 1, d), lambda i, k: ((i * tm) // seq, 0, 0))
    return pl.BlockSpec((1, tm, d), lambda i, k: (0, i, 0))


def _cmp_part_kernel(pt_ref, *refs):
    pages = refs[:PAGES_PER_STEP]
    w_ref = refs[PAGES_PER_STEP]
    o_ref = refs[PAGES_PER_STEP + 1]
    stage = refs[PAGES_PER_STEP + 2]
    nchunk = CHUNK_ROWS // CMP_STRIDE
    cpp = PAGE // CMP_STRIDE
    nslab = 2 * N_KV
    tiles = [[pg[pl.ds(r, cpp, stride=CMP_STRIDE)] for pg in pages] for r in range(CMP_STRIDE)]
    for kind in range(2):
        cols = [jnp.concatenate([x[:, kind * N_KV:(kind + 1) * N_KV, :].reshape(cpp * N_KV, HEAD_DIM)
                                 for x in tiles[r]], axis=0).astype(BF16) for r in range(CMP_STRIDE)]
        lhs = jnp.concatenate(cols, axis=1)
        acc = jnp.dot(lhs, w_ref[:, kind * 2 * HEAD_DIM:(kind + 1) * 2 * HEAD_DIM],
                      preferred_element_type=F32)
        stage[0] = acc[:, :HEAD_DIM]
        stage[1] = acc[:, HEAD_DIM:]
        for h in range(N_KV):
            o_ref[kind, h] = jnp.concatenate(
                [stage[0, pl.ds(h, nchunk, stride=N_KV), :], stage[1, pl.ds(h, nchunk, stride=N_KV), :]], axis=1)


def _page_spec(n, kind_pair):
    return pl.BlockSpec((None, PAGE, 8, HEAD_DIM),
                        lambda b, p, pt, n=n: (pt[b, p * PAGES_PER_STEP + n], 0, kind_pair, 0))


def _cmp_part(pages4, page_table, w1pairs):
    nb, npg = page_table.shape
    steps = npg // PAGES_PER_STEP
    nchunk = PAGES_PER_STEP * (PAGE // CMP_STRIDE)
    grid_spec = pltpu.PrefetchScalarGridSpec(
        num_scalar_prefetch=1,
        grid=(nb, steps),
        in_specs=[_page_spec(n, 0) for n in range(PAGES_PER_STEP)]
        + [pl.BlockSpec((CMP_STRIDE * HEAD_DIM, 4 * HEAD_DIM), lambda b, p, pt: (0, 0))],
        out_specs=pl.BlockSpec((None, 2, N_KV, nchunk, 2 * HEAD_DIM), lambda b, p, pt: (b, 0, 0, p, 0)),
        scratch_shapes=[pltpu.VMEM((2, nchunk * N_KV, HEAD_DIM), F32)],
    )
    return pl.pallas_call(
        _cmp_part_kernel,
        grid_spec=grid_spec,
        out_shape=jax.ShapeDtypeStruct((nb, 2, N_KV, steps * nchunk, 2 * HEAD_DIM), F32),
        compiler_params=_cparams(("arbitrary", "arbitrary")),
        name="cmp_part",
    )(page_table, *([pages4] * PAGES_PER_STEP), w1pairs)


def _gelu_tanh(x):
    c = np.sqrt(2.0 / np.pi).astype(np.float32)
    return 0.5 * x * (1.0 + jnp.tanh(c * (x + 0.044715 * (x * x * x))))


def _cmp_finish_kernel(part_ref, pe_ref, w1_ref, w2_ref, o_ref, bias_ref):
    @pl.when(pl.program_id(1) == 0)
    def _():
        bias_ref[...] = jnp.dot(pe_ref[...], w1_ref[...], preferred_element_type=F32,
                                precision=lax.Precision.HIGHEST)

    w2 = w2_ref[...].astype(BF16)
    for hd in range(N_KV):
        part = part_ref[hd]
        n = part.shape[0]
        h = part[:, 0:HEAD_DIM] + pltpu.roll(part[:, HEAD_DIM:], n - 1, 0)
        h = h + bias_ref[0:1, :]
        o_ref[hd] = jnp.dot(_gelu_tanh(h).astype(BF16), w2, preferred_element_type=F32).astype(o_ref.dtype)


def _cmp_finish(part, pe8, w1, w2):
    nb, _, _, nchunk, _ = part.shape
    return pl.pallas_call(
        _cmp_finish_kernel,
        grid=(2, nb),
        in_specs=[pl.BlockSpec((None, None, N_KV, nchunk, 2 * HEAD_DIM), lambda k, b: (b, k, 0, 0, 0)),
                  pl.BlockSpec((None, 8, CMP_LEN * HEAD_DIM), lambda k, b: (k, 0, 0)),
                  pl.BlockSpec((None, CMP_LEN * HEAD_DIM, HEAD_DIM), lambda k, b: (k, 0, 0)),
                  pl.BlockSpec((None, HEAD_DIM, HEAD_DIM), lambda k, b: (k, 0, 0))],
        out_specs=pl.BlockSpec((None, None, N_KV, nchunk, HEAD_DIM), lambda k, b: (b, k, 0, 0, 0)),
        out_shape=jax.ShapeDtypeStruct((nb, 2, N_KV, nchunk, HEAD_DIM), BF16),
        scratch_shapes=[pltpu.VMEM((8, HEAD_DIM), F32)],
        compiler_params=_cparams(("arbitrary", "arbitrary")),
        name="cmp_finish",
    )(part, pe8, w1, w2)


def _select_topk(score, valid, forced, blk, axis):
    sc = jnp.where(valid, jnp.where(forced, jnp.inf, score), -jnp.inf)
    sel = jnp.zeros(score.shape, F32)
    big = jnp.int32(1 << 20)
    for _ in range(SLC_TOP):
        mx = jnp.max(sc, axis=axis, keepdims=True)
        idx = jnp.min(jnp.where(sc == mx, blk, big), axis=axis, keepdims=True)
        hit = blk == idx
        sel = jnp.where(hit & (mx > -jnp.inf), 1.0, sel)
        sc = jnp.where(hit, -jnp.inf, sc)
    return sel


def _overlap_matrix(nc_pad, n_blocks_pad, nc, n_blocks):
    cs = np.arange(nc_pad) * CMP_STRIDE
    sb = np.arange(n_blocks_pad) * SLC_LEN
    ov = np.clip(np.minimum(cs[:, None] + CMP_LEN, sb[None, :] + SLC_LEN)
                 - np.maximum(cs[:, None], sb[None, :]), 0, None)
    m = (ov / CMP_STRIDE).astype(np.float32)
    m[nc:, :] = 0.0
    m[:, n_blocks:] = 0.0
    return m


def _attn_prompt_kernel(q_ref, kcb, vcb, ksb, vsb, kwb, vwb, g_ref, mt_ref, e_ref, o_ref,
                        m_scr, l_scr, acc_scr, bias_scr, gate_scr, *, tq, seq, ck):
    qi = pl.program_id(2)
    nc = seq // CMP_STRIDE - 1
    nsb = seq // SLC_LEN
    rb = min(tq, 128)

    q = q_ref[...]
    qq = jnp.concatenate([q[:, g * HEAD_DIM:(g + 1) * HEAD_DIM] for g in range(GROUP)], axis=0)
    rq = GROUP * tq
    q0 = qi * tq

    gt = jax.nn.sigmoid(g_ref[...])
    ngc = 4 * GROUP
    hi = gt.astype(BF16)
    r1 = gt - hi.astype(F32)
    mid = r1.astype(BF16)
    lo = (r1 - mid.astype(F32)).astype(BF16)
    g3 = jnp.concatenate([hi, mid, lo, jnp.zeros((tq, HEAD_DIM - 3 * ngc), BF16)], axis=1)
    srow = lax.broadcasted_iota(jnp.int32, (HEAD_DIM, 3 * GROUP * HEAD_DIM), 0)
    scol = lax.broadcasted_iota(jnp.int32, (HEAD_DIM, 3 * GROUP * HEAD_DIM), 1)
    spread = ((srow < 3 * ngc) & ((srow & (ngc - 1)) == lax.div(scol, jnp.int32(HEAD_DIM)))).astype(BF16)
    gl = jnp.dot(g3, spread, preferred_element_type=F32)
    for c in range(3 * GROUP):
        gate_scr[c] = gl[:, c * HEAD_DIM:(c + 1) * HEAD_DIM]

    ncp = kcb.shape[0]
    s = lax.dot_general(qq, kcb[...], _NT, preferred_element_type=F32) * SCALE
    qpos = q0 + (lax.broadcasted_iota(jnp.int32, (rq, ncp), 0) & (tq - 1))
    col = lax.broadcasted_iota(jnp.int32, (rq, ncp), 1)
    vis = (col * CMP_STRIDE + (CMP_LEN - 1) <= qpos) & (col < nc)
    s = jnp.where(vis, s, NEG)
    mx = jnp.max(s, axis=1, keepdims=True)
    e = jnp.where(vis, jnp.exp(s - mx), 0.0)
    den = jnp.sum(e, axis=1, keepdims=True)
    p = e / jnp.where(den > 0.0, den, 1.0)
    o_cmp = jnp.dot(p.astype(BF16), vcb[...], preferred_element_type=F32)
    pg = p[0:tq]
    for g in range(1, GROUP):
        pg = pg + p[g * tq:(g + 1) * tq]

    sct = lax.dot_general(mt_ref[...], pg, _NT, preferred_element_type=F32,
                          precision=lax.Precision.HIGHEST)
    blk = lax.broadcasted_iota(jnp.int32, (nsb, tq), 0)
    cur = lax.div(q0 + lax.broadcasted_iota(jnp.int32, (nsb, tq), 1), jnp.int32(SLC_LEN))
    valid = blk <= cur
    forced = (blk == 0) | (blk > cur - N_LOCAL)
    need_topk = (q0 + tq - 1) // SLC_LEN + 1 > SLC_TOP
    sel_t = lax.cond(need_topk,
                     lambda: _select_topk(sct, valid, forced, blk, 0),
                     lambda: valid.astype(F32)).astype(BF16)

    def sweep(k_ref, v_ref, k_start, n_chunks, bias_fn):
        nl = ck // HEAD_DIM

        def scores(c):
            k0 = pl.multiple_of(k_start + c * ck, 128)
            return lax.dot_general(qq, k_ref[pl.ds(k0, ck), :], _NT, preferred_element_type=F32)

        def chunk_inputs(c):
            k0 = pl.multiple_of(k_start + c * ck, 128)
            return scores(c), bias_fn(c, k0)

        c_exp = np.float32(SCALE * np.log2(np.e))

        def reduce_chunk(c, sb, first):
            sc_, bias = sb
            k0 = pl.multiple_of(k_start + c * ck, 128)
            vch = jnp.concatenate([v_ref[pl.ds(k0, ck), :], jnp.ones((ck, HEAD_DIM), BF16)], axis=1)
            alphas, ps = [], []
            nblk = rq // rb
            for r in range(nblk):
                rows = slice(r * rb, (r + 1) * rb)
                b0 = (r * rb) % tq
                sg = [sc_[rows, j * HEAD_DIM:(j + 1) * HEAD_DIM]
                      + bias[b0:b0 + rb, j * HEAD_DIM:(j + 1) * HEAD_DIM] for j in range(nl)]
                mx = functools.reduce(jnp.maximum, sg)
                row_max = jnp.max(mx, axis=1, keepdims=True)
                if first:
                    m_new = jnp.broadcast_to(row_max, (rb, HEAD_DIM))
                else:
                    m_old = m_scr[rows]
                    m_new = jnp.maximum(m_old, row_max)
                    alphas.append(jnp.exp2((m_old - m_new) * c_exp))
                pj = [jnp.exp2((x - m_new) * c_exp) for x in sg]
                m_scr[rows] = m_new
                ps.append(jnp.concatenate([x.astype(BF16) for x in pj], axis=1))
                if (r + 1) % (nblk // 2) == 0:
                    half = slice((r + 1 - nblk // 2) * rb, (r + 1) * rb)
                    pv = jnp.dot(jnp.concatenate(ps, axis=0), vch, preferred_element_type=F32)
                    if first:
                        acc_scr[half] = pv[:, :HEAD_DIM]
                        l_scr[half] = pv[:, HEAD_DIM:]
                    else:
                        al = jnp.concatenate(alphas, axis=0)
                        acc_scr[half] = al * acc_scr[half] + pv[:, :HEAD_DIM]
                        l_scr[half] = al * l_scr[half] + pv[:, HEAD_DIM:]
                    alphas, ps = [], []

        def body(c, carry):
            reduce_chunk(c, chunk_inputs(c), False)
            return carry

        reduce_chunk(0, chunk_inputs(0), True)
        lax.fori_loop(1, n_chunks, body, 0)
        return acc_scr[...] / l_scr[...]

    qp = q0 + lax.broadcasted_iota(jnp.int32, (tq, ck), 0)
    kcol = lax.broadcasted_iota(jnp.int32, (tq, ck), 1)

    eye = (lax.broadcasted_iota(jnp.int32, (nsb, HEAD_DIM), 0)
           == lax.broadcasted_iota(jnp.int32, (nsb, HEAD_DIM), 1)).astype(BF16)
    sel_q = lax.dot_general(sel_t, eye, _TN, preferred_element_type=F32).astype(BF16)

    n_slc = (q0 + tq + ck - 1) // ck

    def put_mask(c):
        selq = jnp.dot(sel_q, e_ref[c], preferred_element_type=F32)
        bias_scr[c] = jnp.where((selq > 0.5) & (c * ck + kcol <= qp), 0.0, NEG)

    def mask_body(i, carry):
        put_mask(2 * i)
        put_mask(jnp.minimum(2 * i + 1, seq // ck - 1))
        return carry

    lax.fori_loop(0, (n_slc + 1) // 2, mask_body, 0)
    o_slc = sweep(ksb, vsb, 0, n_slc, lambda c, k0: bias_scr[c])

    span = -(-(WINDOW + tq) // ck) * ck

    def win_bias(c, k0):
        dlt = qp - (k0 + kcol)
        return jnp.where((dlt >= 0) & (dlt < WINDOW), 0.0, NEG)

    o_win = sweep(kwb, vwb, jnp.maximum(q0 + tq - span, 0), span // ck, win_bias)

    for g in range(GROUP):
        r0, r1 = g * tq, (g + 1) * tq
        og = gate_scr[g] * o_cmp[r0:r1] + gate_scr[GROUP + g] * o_slc[r0:r1]
        og = og + gate_scr[2 * GROUP + g] * o_win[r0:r1]
        o_ref[:, g * HEAD_DIM:(g + 1) * HEAD_DIM] = og.astype(o_ref.dtype)


def _attn_prompt(q, kcv, kv, win, g_re, *, nb, seq, tq=512, ck=512):
    assert tq & (tq - 1) == 0 and seq % ck == 0 and WINDOW % tq == 0
    t = q.shape[0]
    ncp = seq // CMP_STRIDE
    nsb = seq // SLC_LEN
    nq = seq // tq
    mt = jnp.asarray(_overlap_matrix(ncp, nsb, ncp - 1, nsb).T)
    kk = np.arange(seq)
    assert nsb <= HEAD_DIM
    e3 = (kk[None, :] // SLC_LEN == np.arange(HEAD_DIM)[:, None]).astype(np.float32)
    e3 = jnp.asarray(e3.reshape(HEAD_DIM, seq // ck, ck).transpose(1, 0, 2), dtype=BF16)
    slab = lambda cb: pl.BlockSpec((seq, HEAD_DIM), lambda b, h, i, cb=cb: (b, cb + h))
    return pl.pallas_call(
        functools.partial(_attn_prompt_kernel, tq=tq, seq=seq, ck=ck),
        grid=(nb, N_KV, nq),
        in_specs=[pl.BlockSpec((tq, GROUP * HEAD_DIM), lambda b, h, i: (b * nq + i, h)),
                  pl.BlockSpec((None, None, None, ncp, HEAD_DIM), lambda b, h, i: (b, 0, h, 0, 0)),
                  pl.BlockSpec((None, None, None, ncp, HEAD_DIM), lambda b, h, i: (b, 1, h, 0, 0)),
                  slab(0), slab(N_KV), slab(0), slab(N_KV),
                  pl.BlockSpec((None, None, tq, 4 * GROUP), lambda b, h, i: (b, h, i, 0)),
                  pl.BlockSpec((nsb, ncp), lambda b, h, i: (0, 0)),
                  pl.BlockSpec((seq // ck, HEAD_DIM, ck), lambda b, h, i: (0, 0, 0))],
        out_specs=pl.BlockSpec((tq, GROUP * HEAD_DIM), lambda b, h, i: (b * nq + i, h)),
        out_shape=jax.ShapeDtypeStruct((t, N_HEADS * HEAD_DIM), BF16),
        scratch_shapes=[pltpu.VMEM((GROUP * tq, HEAD_DIM), F32)] * 3
        + [pltpu.VMEM((seq // ck, tq, ck), F32), pltpu.VMEM((3 * GROUP, tq, HEAD_DIM), F32)],
        compiler_params=_cparams(("arbitrary", "arbitrary", "arbitrary")),
        name="attn_prompt",
    )(q, kcv, kcv, kv, kv, win, win, g_re, mt, e3)


def _attn_sample_kernel(pt_ref, cache_ref, q_ref, kcv_ref, kvn_ref, wn_ref, st_ref, g_ref, ms_ref, e_ref, o_ref,
                        wso_ref, m_scr, l_scr, acc_scr, sel_scr, ocmp_scr, kv_buf, sem, *, past, ds):
    b = pl.program_id(0)
    p = pl.program_id(1)
    n_steps = pl.num_programs(1)

    def page_copies(bb, pp, slot, for_wait=False):
        cps = []
        for n in range(PAGES_PER_STEP):
            page = 0 if for_wait else pt_ref[bb, pp * PAGES_PER_STEP + n]
            for cb in range(2 * N_KV):
                cps.append(pltpu.make_async_copy(cache_ref.at[page, :, 2 * N_KV + cb, :],
                                                 kv_buf.at[slot, cb, pl.ds(n * PAGE, PAGE), :],
                                                 sem.at[slot]))
        return cps

    t = b * n_steps + p
    slot = lax.rem(t, 2)

    @pl.when(t == 0)
    def _():
        for cp in page_copies(b, p, slot):
            cp.start()

    @pl.when(t + 1 < pl.num_programs(0) * n_steps)
    def _():
        wrap = p + 1 == n_steps
        for cp in page_copies(jnp.where(wrap, b + 1, b), jnp.where(wrap, 0, p + 1), 1 - slot):
            cp.start()
    rq = GROUP * ds
    ncp = kcv_ref.shape[2]
    nc = ncp - 1
    n_sel_chunks = sel_scr.shape[1]
    lanes = ms_ref.shape[1]

    def q_rows(h):
        c0 = h * GROUP * HEAD_DIM
        return jnp.concatenate(
            [q_ref[:, c0 + g * HEAD_DIM:c0 + (g + 1) * HEAD_DIM] for g in range(GROUP)], axis=0).astype(BF16)

    def online_update(h, sc_, vals):
        m_i = m_scr[h]
        m_new = jnp.maximum(m_i, jnp.max(sc_, axis=1, keepdims=True))
        alpha = jnp.exp(m_i - m_new)
        pp = jnp.exp(sc_ - m_new)
        l_scr[h] = alpha * l_scr[h] + jnp.sum(pp, axis=1, keepdims=True)
        acc_scr[h] = alpha * acc_scr[h] + jnp.dot(pp.astype(BF16), vals, preferred_element_type=F32)
        m_scr[h] = m_new

    @pl.when(p == 0)
    def _():
        pgs = []
        for h in range(N_KV):
            qq = q_rows(h)
            kc = kcv_ref[0, h]
            vc = kcv_ref[1, h]
            s = lax.dot_general(qq, kc, _NT, preferred_element_type=F32) * SCALE
            qpos = past + (lax.broadcasted_iota(jnp.int32, (rq, ncp), 0) & (ds - 1))
            col = lax.broadcasted_iota(jnp.int32, (rq, ncp), 1)
            vis = (col * CMP_STRIDE + (CMP_LEN - 1) <= qpos) & (col < nc)
            s = jnp.where(vis, s, NEG)
            mx = jnp.max(s, axis=1, keepdims=True)
            e = jnp.where(vis, jnp.exp(s - mx), 0.0)
            den = jnp.sum(e, axis=1, keepdims=True)
            pr = e / jnp.where(den > 0.0, den, 1.0)
            ocmp_scr[h] = jnp.dot(pr.astype(BF16), vc, preferred_element_type=F32)
            pg = pr[0:ds]
            for g in range(1, GROUP):
                pg = pg + pr[g * ds:(g + 1) * ds]
            pgs.append(pg)
            m_scr[h] = jnp.full((rq, 1), NEG, F32)
            l_scr[h] = jnp.zeros((rq, 1), F32)
            acc_scr[h] = jnp.zeros((rq, HEAD_DIM), F32)
        nr = N_KV * ds
        pg_all = jnp.concatenate(pgs, axis=0)
        hi = pg_all.astype(BF16)
        r1 = pg_all - hi.astype(F32)
        mid = r1.astype(BF16)
        lo = (r1 - mid.astype(F32)).astype(BF16)
        sc3 = jnp.dot(jnp.concatenate([hi, mid, lo], axis=0), ms_ref[...], preferred_element_type=F32)
        score = (sc3[0:nr] + sc3[nr:2 * nr]) + sc3[2 * nr:3 * nr]
        blk = lax.broadcasted_iota(jnp.int32, (nr, lanes), 1)
        qrow = lax.broadcasted_iota(jnp.int32, (nr, lanes), 0) & (ds - 1)
        cur = lax.div(past + qrow, jnp.int32(SLC_LEN))
        valid = blk <= cur
        forced = (blk == 0) | (blk > cur - N_LOCAL)
        sel = _select_topk(score, valid, forced, blk, 1)
        lane_pad = jnp.zeros((ds, HEAD_DIM - BLK_PER_CHUNK), F32)
        for h in range(N_KV):
            for c in range(n_sel_chunks):
                sel_scr[h, c] = jnp.concatenate(
                    [sel[h * ds:(h + 1) * ds, c * BLK_PER_CHUNK:(c + 1) * BLK_PER_CHUNK], lane_pad], axis=1)

    for cp in page_copies(b, p, slot, for_wait=True):
        cp.wait()

    k0 = p * CHUNK_ROWS
    for h in range(N_KV):
        qq = q_rows(h)
        kch = kv_buf[slot, h].astype(BF16)
        vch = kv_buf[slot, N_KV + h].astype(BF16)
        sc_ = lax.dot_general(qq, kch, _NT, preferred_element_type=F32) * SCALE
        selq = jnp.dot(sel_scr[h, p].astype(BF16), e_ref[...], preferred_element_type=F32)
        kpos = k0 + lax.broadcasted_iota(jnp.int32, (ds, CHUNK_ROWS), 1)
        qp = past + lax.broadcasted_iota(jnp.int32, (ds, CHUNK_ROWS), 0)
        okf = jnp.where((selq > 0.5) & (kpos <= qp), 1.0, 0.0)
        ok = jnp.concatenate([okf] * GROUP, axis=0) > 0.5
        online_update(h, jnp.where(ok, sc_, NEG), vch)

    @pl.when(p == n_steps - 1)
    def _():
        gt = jax.nn.sigmoid(g_ref[...])
        zpad = jnp.zeros((HEAD_DIM - ds, HEAD_DIM), F32)
        for h in range(N_KV):
            qq = q_rows(h)
            c_k = 2 * KV_W + h * HEAD_DIM
            c_v = 3 * KV_W + h * HEAD_DIM
            kn = jnp.concatenate([kvn_ref[:, c_k:c_k + HEAD_DIM], zpad], axis=0).astype(BF16)
            vn = jnp.concatenate([kvn_ref[:, c_v:c_v + HEAD_DIM], zpad], axis=0).astype(BF16)
            sn = lax.dot_general(qq, kn, _NT, preferred_element_type=F32) * SCALE
            last_sel = sel_scr[h, n_sel_chunks - 1]
            nb_last = (past // SLC_LEN) % BLK_PER_CHUNK
            seln = jnp.sum(jnp.where(lax.broadcasted_iota(jnp.int32, (ds, HEAD_DIM), 1) == nb_last,
                                     last_sel, 0.0), axis=1, keepdims=True)
            seln = jnp.concatenate([seln] * GROUP, axis=0)
            srow = lax.broadcasted_iota(jnp.int32, (rq, HEAD_DIM), 0) & (ds - 1)
            kcol = lax.broadcasted_iota(jnp.int32, (rq, HEAD_DIM), 1)
            okn = (seln > 0.5) & (kcol <= srow) & (kcol < ds)
            online_update(h, jnp.where(okn, sn, NEG), vn)
            o_slc = acc_scr[h] / l_scr[h]
            wb = st_ref.shape[0]
            band = wb + HEAD_DIM
            st_flat = st_ref.reshape(wb * 8, HEAD_DIM)
            kw = jnp.concatenate([st_flat[pl.ds(h, wb, stride=8), :],
                                  wn_ref[:, h * HEAD_DIM:(h + 1) * HEAD_DIM], zpad], axis=0).astype(BF16)
            vw = jnp.concatenate([st_flat[pl.ds(N_KV + h, wb, stride=8), :],
                                  wn_ref[:, KV_W + h * HEAD_DIM:KV_W + (h + 1) * HEAD_DIM], zpad],
                                 axis=0).astype(BF16)
            sw = lax.dot_general(qq, kw, _NT, preferred_element_type=F32) * SCALE
            srw = lax.broadcasted_iota(jnp.int32, (rq, band), 0) & (ds - 1)
            idx = lax.broadcasted_iota(jnp.int32, (rq, band), 1)
            dlt = (wb + srw) - idx
            sw = jnp.where((dlt >= 0) & (dlt < WINDOW) & (idx < wb + ds), sw, NEG)
            ew = jnp.exp(sw - jnp.max(sw, axis=1, keepdims=True))
            o_win = jnp.dot(ew.astype(BF16), vw, preferred_element_type=F32) / jnp.sum(ew, axis=1, keepdims=True)
            o_cmp = ocmp_scr[h]
            gh = gt[h]
            for g in range(GROUP):
                r0, r1 = g * ds, (g + 1) * ds
                og = gh[:, g:g + 1] * o_cmp[r0:r1] + gh[:, GROUP + g:GROUP + g + 1] * o_slc[r0:r1]
                og = og + gh[:, 2 * GROUP + g:2 * GROUP + g + 1] * o_win[r0:r1]
                c0 = (h * GROUP + g) * HEAD_DIM
                o_ref[:, c0:c0 + HEAD_DIM] = og
        wb = st_ref.shape[0]
        wso_ref[0:wb - ds] = st_ref[ds:wb]
        wso_flat = wso_ref.reshape(wb * 8, HEAD_DIM)
        for cb in range(2 * N_KV):
            wso_flat[pl.ds((wb - ds) * 8 + cb, ds, stride=8), :] = wn_ref[:, cb * HEAD_DIM:(cb + 1) * HEAD_DIM]


def _attn_sample(q, kcv, kv_new, win_new, win_state, g_re, cache4, page_table, *, ds):
    nb, npg = page_table.shape
    past = npg * PAGE
    assert ds & (ds - 1) == 0 and ds <= SLC_LEN and past % SLC_LEN == 0 and npg % PAGES_PER_STEP == 0
    steps = npg // PAGES_PER_STEP
    ncp = kcv.shape[3]
    wb = win_state.shape[1]
    n_blocks = past // SLC_LEN + 1
    n_sel_chunks = -(-n_blocks // BLK_PER_CHUNK)
    lanes = -(-n_sel_chunks * BLK_PER_CHUNK // HEAD_DIM) * HEAD_DIM
    ms = jnp.asarray(_overlap_matrix(ncp, lanes, ncp - 1, n_blocks), dtype=BF16)
    ee = np.zeros((HEAD_DIM, CHUNK_ROWS), np.float32)
    ee[:BLK_PER_CHUNK] = np.arange(CHUNK_ROWS)[None, :] // SLC_LEN == np.arange(BLK_PER_CHUNK)[:, None]
    ee = jnp.asarray(ee, dtype=BF16)
    rq = GROUP * ds

    grid_spec = pltpu.PrefetchScalarGridSpec(
        num_scalar_prefetch=1,
        grid=(nb, steps),
        in_specs=[
            pl.BlockSpec(memory_space=pl.ANY),
            pl.BlockSpec((ds, N_HEADS * HEAD_DIM), lambda b, p, pt: (b, 0)),
            pl.BlockSpec((None, 2, N_KV, ncp, HEAD_DIM), lambda b, p, pt: (b, 0, 0, 0, 0)),
            pl.BlockSpec((ds, 4 * KV_W), lambda b, p, pt: (b, 0)),
            pl.BlockSpec((ds, 2 * KV_W), lambda b, p, pt: (b, 0)),
            pl.BlockSpec((None, wb, 8, HEAD_DIM), lambda b, p, pt: (b, 0, 0, 0)),
            pl.BlockSpec((None, N_KV, ds, 4 * GROUP), lambda b, p, pt: (b, 0, 0, 0)),
            pl.BlockSpec(ms.shape, lambda b, p, pt: (0, 0)),
            pl.BlockSpec(ee.shape, lambda b, p, pt: (0, 0))],
        out_specs=[pl.BlockSpec((ds, N_HEADS * HEAD_DIM), lambda b, p, pt: (b, 0)),
                   pl.BlockSpec((None, wb, 8, HEAD_DIM), lambda b, p, pt: (b, 0, 0, 0))],
        scratch_shapes=[pltpu.VMEM((N_KV, rq, 1), F32), pltpu.VMEM((N_KV, rq, 1), F32),
                        pltpu.VMEM((N_KV, rq, HEAD_DIM), F32),
                        pltpu.VMEM((N_KV, n_sel_chunks, ds, HEAD_DIM), F32),
                        pltpu.VMEM((N_KV, rq, HEAD_DIM), F32),
                        pltpu.VMEM((2, 2 * N_KV, CHUNK_ROWS, HEAD_DIM), F32),
                        pltpu.SemaphoreType.DMA((2,))],
    )
    return pl.pallas_call(
        functools.partial(_attn_sample_kernel, past=past, ds=ds),
        grid_spec=grid_spec,
        out_shape=[jax.ShapeDtypeStruct((nb * ds, N_HEADS * HEAD_DIM), F32),
                   jax.ShapeDtypeStruct((nb, wb, 8, HEAD_DIM), F32)],
        compiler_params=_cparams(("arbitrary", "arbitrary")),
        name="attn_sample",
    )(page_table, cache4, q, kcv, kv_new, win_new, win_state, g_re, ms, ee)


_Q0, _KV0, _WIN0, _G0 = 0, 2048, 4096, 5120
_REST0 = _G0 + 3 * N_HEADS


def _split_w_in(w_in):
    wt = w_in.T
    return wt, wt[:_G0 + HEAD_DIM].astype(BF16)


def _gate_layout(g_logits, nb, seq):
    g = g_logits[:, :3 * N_HEADS].reshape(nb, seq, 3, N_KV, GROUP)
    g = g.transpose(0, 3, 1, 2, 4).reshape(nb, N_KV, seq, 3 * GROUP)
    return jnp.pad(g, ((0, 0), (0, 0), (0, 0), (0, GROUP)))


def _cmp_weights(w_k1, w_k2, pe_k, w_v1, w_v2, pe_v):
    def by_row(w1):
        w = w1.reshape(2, CMP_STRIDE, HEAD_DIM, HEAD_DIM).transpose(1, 2, 0, 3)
        return w.reshape(CMP_STRIDE * HEAD_DIM, 2 * HEAD_DIM)
    w1pairs = jnp.concatenate([by_row(w_k1), by_row(w_v1)], axis=1).astype(BF16)
    pe8 = jnp.stack([jnp.broadcast_to(pe_k.reshape(1, -1), (8, CMP_LEN * HEAD_DIM)),
                     jnp.broadcast_to(pe_v.reshape(1, -1), (8, CMP_LEN * HEAD_DIM))])
    return w1pairs, pe8, jnp.stack([w_k1, w_v1]), jnp.stack([w_k2, w_v2])


def kernel(x_prompt, x_sample, c_prompt, c_sample, cache_nsa_kv, page_table, state_win_kv, state_conv, w_ada, b_ada, norm1_g, norm2_g, w_in, w_cmp_k1, w_cmp_k2, pe_cmp_k, w_cmp_v1, w_cmp_v2, pe_cmp_v, conv_w, conv_b, w_attn_proj, w_conv_proj, w_out, w_mlp1, w_mlp2, normf_g):
    d = D_MODEL
    nbp, seq, _ = x_prompt.shape
    nbs, ds, _ = x_sample.shape
    tp = nbp * seq
    ts = nbs * ds
    tc = 256
    depth = w_in.shape[0]
    assert depth == 1

    xp = x_prompt.reshape(tp, d)
    xs = x_sample.reshape(ts, d)
    normf = normf_g.reshape(1, d)

    l = 0
    c_all = jnp.concatenate([c_prompt, c_sample, jnp.zeros((16 - nbp - nbs, d), F32)], axis=0)
    ada = _ada(c_all, w_ada[l], b_ada[l].reshape(1, -1)).reshape(16, 6, d)
    ada_p = [ada[:nbp, k][:, None, :] for k in range(6)]
    ada_s = [jnp.repeat(ada[nbp:nbp + nbs, k], ds, axis=0)[None] for k in range(6)]

    w_t, w_head = _split_w_in(w_in[l])
    gates0 = _REST0 + 3 * d
    wa = w_attn_proj[l]
    wc = w_conv_proj[l]
    wo = w_out[l]
    g1n = norm1_g[l].reshape(1, d)
    g2n = norm2_g[l].reshape(1, d)
    cw = conv_w[l]
    cb = conv_b[l].reshape(1, d)
    w1pairs, pe8, w1s, w2s = _cmp_weights(w_cmp_k1[l], w_cmp_k2[l], pe_cmp_k[l],
                                          w_cmp_v1[l], w_cmp_v2[l], pe_cmp_v[l])

    tm = 1024
    hp, q_p = _norm_mm(xp, g1n, ada_p[1], ada_p[0], w_head, tm=tm, tn=1024, seq=seq, out_dtype=BF16,
                       col0=_Q0, n=N_HEADS * HEAD_DIM, name="norm_mm_q")
    wbp = min(WINDOW, seq)
    kv6_p, kvb_p = _mm_kv(hp, w_head, tm=tm, col0=_KV0)
    winb_p, win6_p, g_p = _mm_win(hp, w_head, tm=wbp, seq=seq, col0=_WIN0, gcol0=_G0)
    gates_p = _mm_wstat(hp, w_t, tm=tm, tn=1024, out_dtype=BF16, act="sigmoid", name="mm_gates",
                        col0=gates0, n=2 * d)
    z_p, conv_tiles, w1, w2 = _mm_conv(hp, w_t, cw, cb, w_mlp1[l], w_mlp2[l], tm=tm, tc=tc, seq=seq, col0=_REST0)

    pt_p = jnp.arange(tp // PAGE, dtype=jnp.int32).reshape(nbp, seq // PAGE)
    part_p = _cmp_part(kv6_p.reshape(tp // PAGE, PAGE, 16, HEAD_DIM), pt_p, w1pairs)
    kcv_p = _cmp_finish(part_p, pe8, w1s, w2s)
    o_p = _attn_prompt(q_p, kcv_p, kvb_p, winb_p, g_p, nb=nbp, seq=seq)

    mixed_p = _mix(o_p, z_p, wa, wc, gates_p, tm=tm, tn=512)
    x1_p, h2_p = _mm_resid_norm(mixed_p, wo, xp, ada_p[2], g2n, ada_p[4], ada_p[3], tm=512, seq=seq)
    y_p = _mlp_final(h2_p, w1, w2, x1_p, ada_p[5], normf, tm=512, tf=1024, seq=seq)

    hs, q_s = _norm_mm(xs, g1n, ada_s[1], ada_s[0], w_head, tm=ts, tn=512, seq=ds, out_dtype=F32,
                       col0=_Q0, n=N_HEADS * HEAD_DIM, name="norm_mm_q_s")
    kv_s = _mm(hs, w_head, tm=ts, tn=512, out_dtype=F32, name="mm_kv_s", col0=_KV0, n=4 * KV_W)
    win_s = _mm(hs, w_head, tm=ts, tn=512, out_dtype=F32, name="mm_win_s", col0=_WIN0, n=2 * KV_W)
    g_s = _mm(hs, w_head, tm=ts, tn=HEAD_DIM, out_dtype=F32, name="mm_g_s", col0=_G0, n=HEAD_DIM)
    gates_s = _mm_wstat(hs, w_t, tm=ts, tn=512, out_dtype=F32, act="sigmoid", name="mm_gates_s",
                        col0=gates0, n=2 * d)
    ubc_s = _mm_wstat(hs, w_t, tm=ts, tn=512, out_dtype=F32, name="mm_ubc_s", col0=_REST0, n=3 * d)
    st = state_conv[l]
    z_s, cu_s = _sample_conv(ubc_s, jnp.repeat(st[:, 0], ds, axis=0), jnp.repeat(st[:, 1], ds, axis=0),
                             cw, cb, tc=tc, seq=ds)

    cache4 = cache_nsa_kv[l].reshape(cache_nsa_kv.shape[1], PAGE, 16, HEAD_DIM)
    part_s = _cmp_part(cache4, page_table, w1pairs)
    kcv_s = _cmp_finish(part_s, pe8, w1s, w2s)
    wbs = state_win_kv.shape[2]
    wst = state_win_kv[l].reshape(nbs, wbs, 8, HEAD_DIM)
    o_s, wst_next = _attn_sample(q_s, kcv_s, kv_s, win_s, wst, _gate_layout(g_s, nbs, ds), cache4, page_table,
                                 ds=ds)

    mixed_s = _mix(o_s, z_s, wa, wc, gates_s, tm=ts, tn=512)
    x1_s, h2_s = _mm_resid_norm(mixed_s, wo, xs, ada_s[2], g2n, ada_s[4], ada_s[3], tm=ts, seq=ds)
    y_s = _mlp_final(h2_s, w1, w2, x1_s, ada_s[5], normf, tm=ts, tf=512, seq=ds)

    kv_prompt = kv6_p.reshape(1, nbp, seq, 4, N_KV, HEAD_DIM)
    kv_sample = kv_s.reshape(1, nbs, ds, 4, N_KV, HEAD_DIM)
    win_prompt = win6_p.reshape(1, nbp, wbp, 2, N_KV, HEAD_DIM)
    win_sample = wst_next.reshape(1, nbs, wbs, 2, N_KV, HEAD_DIM)
    tiles_per_seq = seq // tm
    conv_prompt = conv_tiles[tiles_per_seq - 1::tiles_per_seq][None]
    conv_sample = cu_s.reshape(nbs, ds, d)[None, :, ds - (CONV_W - 1):]
    return (y_p.reshape(nbp, seq, d), y_s.reshape(nbs, ds, d), kv_prompt, kv_sample,
            win_prompt, win_sample, conv_prompt, conv_sample)
```

```python
import functools

import numpy as np
import jax
import jax.numpy as jnp
from jax import lax
from jax.experimental import pallas as pl
from jax.experimental.pallas import tpu as pltpu

F32 = jnp.float32
BF16 = jnp.bfloat16

D_MODEL = 2048
HEAD_DIM = 128
N_HEADS = 16
N_KV = 4
GROUP = 4
KV_W = N_KV * HEAD_DIM
CMP_LEN = 32
CMP_STRIDE = 16
SLC_LEN = 64
SLC_TOP = 16
N_LOCAL = 2
WINDOW = 512
CONV_W = 3
RMS_EPS = 1e-6
NEG = -1e30
SCALE = HEAD_DIM ** -0.5
PAGE = 128
PAGES_PER_STEP = 16
CHUNK_ROWS = PAGES_PER_STEP * PAGE
BLK_PER_CHUNK = CHUNK_ROWS // SLC_LEN
V7X_VMEM_BYTES = 64 * 1024 * 1024
VMEM_LIMIT = V7X_VMEM_BYTES - 8 * 1024 * 1024

_NT = (((1,), (1,)), ((), ()))
_TN = (((0,), (0,)), ((), ()))


def _cparams(sem):
    return pltpu.CompilerParams(dimension_semantics=sem, vmem_limit_bytes=VMEM_LIMIT)


def _ada_kernel(c_ref, w_ref, b_ref, o_ref):
    c = c_ref[...]
    a = (c * jax.nn.sigmoid(c)).astype(BF16)
    o_ref[...] = jnp.dot(a, w_ref[...].astype(BF16), preferred_element_type=F32) + b_ref[...]


def _ada(c, w, b, tn=1024):
    m, k = c.shape
    n = w.shape[1]
    return pl.pallas_call(
        _ada_kernel,
        grid=(n // tn,),
        in_specs=[pl.BlockSpec((m, k), lambda j: (0, 0)),
                  pl.BlockSpec((k, tn), lambda j: (0, j)),
                  pl.BlockSpec((1, tn), lambda j: (0, j))],
        out_specs=pl.BlockSpec((m, tn), lambda j: (0, j)),
        out_shape=jax.ShapeDtypeStruct((m, n), F32),
        compiler_params=_cparams(("arbitrary",)),
        name="ada",
    )(c, w, b)


def _rowspec(p, tm, tn, seq, col=True):
    gr = p.shape[1]
    if gr == 1:
        if col:
            return pl.BlockSpec((1, 1, tn), lambda i, j: ((i * tm) // seq, 0, j))
        return pl.BlockSpec((1, 1, tn), lambda i: ((i * tm) // seq, 0, 0))
    if col:
        return pl.BlockSpec((1, tm, tn), lambda i, j: (0, i, j))
    return pl.BlockSpec((1, tm, tn), lambda i: (0, i, 0))


def _xwt(x_ref, wt_ref):
    return lax.dot_general(x_ref[...].astype(BF16), wt_ref[...], _NT, preferred_element_type=F32)


def _mm_kernel(x_ref, w_ref, o_ref, *, act):
    acc = _xwt(x_ref, w_ref)
    if act == "sigmoid":
        acc = jax.nn.sigmoid(acc)
    o_ref[...] = acc.astype(o_ref.dtype)


def _norm_mm_kernel(x_ref, g_ref, sc_ref, sh_ref, w_ref, h_ref, o_ref):
    @pl.when(pl.program_id(1) == 0)
    def _():
        x = x_ref[...]
        r = lax.rsqrt(jnp.mean(x * x, axis=-1, keepdims=True) + RMS_EPS)
        y = (x * r) * g_ref[...]
        h_ref[...] = (y * (1.0 + sc_ref[0]) + sh_ref[0]).astype(h_ref.dtype)

    o_ref[...] = _xwt(h_ref, w_ref).astype(o_ref.dtype)


def _norm_mm(x, g, sc, sh, w, *, tm, tn, seq, out_dtype, col0, n, name):
    t, d = x.shape
    assert col0 % tn == 0 and n % tn == 0
    cb0 = col0 // tn
    row = lambda p: (pl.BlockSpec((1, 1, d), lambda i, j: ((i * tm) // seq, 0, 0)) if p.shape[1] == 1
                     else pl.BlockSpec((1, tm, d), lambda i, j: (0, i, 0)))
    return pl.pallas_call(
        _norm_mm_kernel,
        grid=(t // tm, n // tn),
        in_specs=[pl.BlockSpec((tm, d), lambda i, j: (i, 0)),
                  pl.BlockSpec((1, d), lambda i, j: (0, 0)),
                  row(sc), row(sh),
                  pl.BlockSpec((tn, d), lambda i, j: (cb0 + j, 0))],
        out_specs=[pl.BlockSpec((tm, d), lambda i, j: (i, 0)),
                   pl.BlockSpec((tm, tn), lambda i, j: (i, j))],
        out_shape=[jax.ShapeDtypeStruct((t, d), BF16), jax.ShapeDtypeStruct((t, n), out_dtype)],
        compiler_params=_cparams(("arbitrary", "arbitrary")),
        name=name,
    )(x, g, sc, sh, w)


def _mm_wstat_kernel(x_ref, w_ref, o_ref, wbf_ref, *, act):
    @pl.when(pl.program_id(1) == 0)
    def _():
        wbf_ref[...] = w_ref[...].astype(BF16)

    acc = _xwt(x_ref, wbf_ref)
    if act == "sigmoid":
        acc = jax.nn.sigmoid(acc)
    o_ref[...] = acc.astype(o_ref.dtype)


def _wrows(tn, k, row0):
    return pl.BlockSpec((pl.Element(tn), pl.Element(k)), lambda j, i: (pl.multiple_of(row0 + j * tn, 8), 0))


def _mm_wstat(x, w, *, tm, tn, out_dtype, act=None, name="mm", col0, n):
    t, k = x.shape
    assert col0 % 8 == 0 and n % tn == 0
    return pl.pallas_call(
        functools.partial(_mm_wstat_kernel, act=act),
        grid=(n // tn, t // tm),
        in_specs=[pl.BlockSpec((tm, k), lambda j, i: (i, 0)),
                  _wrows(tn, k, col0)],
        out_specs=pl.BlockSpec((tm, tn), lambda j, i: (i, j)),
        out_shape=jax.ShapeDtypeStruct((t, n), out_dtype),
        scratch_shapes=[pltpu.VMEM((tn, k), BF16)],
        compiler_params=_cparams(("arbitrary", "arbitrary")),
        name=name,
    )(x, w)


def _mm(x, w, *, tm, tn, out_dtype, act=None, name="mm", col0, n):
    t, k = x.shape
    assert col0 % tn == 0 and n % tn == 0
    cb0 = col0 // tn
    return pl.pallas_call(
        functools.partial(_mm_kernel, act=act),
        grid=(t // tm, n // tn),
        in_specs=[pl.BlockSpec((tm, k), lambda i, j: (i, 0)),
                  pl.BlockSpec((tn, k), lambda i, j: (cb0 + j, 0))],
        out_specs=pl.BlockSpec((tm, tn), lambda i, j: (i, j)),
        out_shape=jax.ShapeDtypeStruct((t, n), out_dtype),
        compiler_params=_cparams(("arbitrary", "arbitrary")),
        name=name,
    )(x, w)


def _store_head_major(o_ref, acc, tm):
    flat = o_ref.reshape(tm * 8, HEAD_DIM)
    for cb in range(8):
        flat[pl.ds(cb, tm, stride=8), :] = acc[:, cb * HEAD_DIM:(cb + 1) * HEAD_DIM]


def _mm_kv_kernel(x_ref, w_ref, o6_ref, ob_ref, *, tm):
    acc = _xwt(x_ref, w_ref)
    _store_head_major(o6_ref, acc, tm)

    @pl.when(pl.program_id(1) == 1)
    def _():
        ob_ref[...] = acc.astype(BF16)


def _mm_kv(h, w, *, tm, col0):
    t, k = h.shape
    tn = 2 * KV_W
    cb0 = col0 // tn
    return pl.pallas_call(
        functools.partial(_mm_kv_kernel, tm=tm),
        grid=(t // tm, 2),
        in_specs=[pl.BlockSpec((tm, k), lambda i, j: (i, 0)),
                  pl.BlockSpec((tn, k), lambda i, j: (cb0 + j, 0))],
        out_specs=[pl.BlockSpec((tm, 8, HEAD_DIM), lambda i, j: (i, j, 0)),
                   pl.BlockSpec((tm, tn), lambda i, j: (i, 0))],
        out_shape=[jax.ShapeDtypeStruct((t, 16, HEAD_DIM), F32),
                   jax.ShapeDtypeStruct((t, tn), BF16)],
        compiler_params=_cparams(("arbitrary", "arbitrary")),
        name="mm_kv",
    )(h, w)


def _mm_win_kernel(x_ref, w_ref, wg_ref, ob_ref, o6_ref, og_ref, *, tm, seq):
    acc = _xwt(x_ref, w_ref)
    ob_ref[...] = acc.astype(BF16)
    gl = _xwt(x_ref, wg_ref)
    for hd in range(N_KV):
        og_ref[hd] = jnp.concatenate(
            [gl[:, br * N_HEADS + hd * GROUP:br * N_HEADS + (hd + 1) * GROUP] for br in range(3)]
            + [jnp.zeros((tm, GROUP), F32)], axis=1)

    @pl.when(((pl.program_id(0) + 1) * tm) % seq == 0)
    def _():
        _store_head_major(o6_ref, acc, tm)


def _mm_win(h, w, *, tm, seq, col0, gcol0):
    t, k = h.shape
    tn = 2 * KV_W
    cb0 = col0 // tn
    gb0 = gcol0 // HEAD_DIM
    return pl.pallas_call(
        functools.partial(_mm_win_kernel, tm=tm, seq=seq),
        grid=(t // tm,),
        in_specs=[pl.BlockSpec((tm, k), lambda i: (i, 0)),
                  pl.BlockSpec((tn, k), lambda i: (cb0, 0)),
                  pl.BlockSpec((HEAD_DIM, k), lambda i: (gb0, 0))],
        out_specs=[pl.BlockSpec((tm, tn), lambda i: (i, 0)),
                   pl.BlockSpec((tm, 8, HEAD_DIM), lambda i: ((i * tm) // seq, 0, 0)),
                   pl.BlockSpec((None, N_KV, tm, 4 * GROUP), lambda i: ((i * tm) // seq, 0, (i * tm % seq) // tm, 0))],
        out_shape=[jax.ShapeDtypeStruct((t, tn), BF16),
                   jax.ShapeDtypeStruct((t // seq * tm, 8, HEAD_DIM), F32),
                   jax.ShapeDtypeStruct((t // seq, N_KV, seq, 4 * GROUP), F32)],
        compiler_params=_cparams(("arbitrary",)),
        name="mm_win",
    )(h, w, w)


def _conv_taps(cu, prev1, prev2, bgate, cw_ref, cb_ref):
    y = cb_ref[...] + prev2 * cw_ref[0:1, :]
    y = y + prev1 * cw_ref[1:2, :]
    y = y + cu * cw_ref[2:3, :]
    return bgate * y


def _mm_conv_kernel(x_ref, wu_ref, wb_ref, wc_ref, cw_ref, cb_ref, ca_ref, cb2_ref, z_ref, st_ref, oa_ref, ob_ref,
                    w3_ref, carry_ref, *, tm, tc, seq):
    i = pl.program_id(1)
    oa_ref[...] = ca_ref[...].astype(BF16)
    ob_ref[...] = cb2_ref[...].astype(BF16)

    @pl.when(i == 0)
    def _():
        w3_ref[0:tc] = wu_ref[...].astype(BF16)
        w3_ref[tc:2 * tc] = wb_ref[...].astype(BF16)
        w3_ref[2 * tc:3 * tc] = wc_ref[...].astype(BF16)

    acc = _xwt(x_ref, w3_ref)
    u = acc[:, 0:tc]
    bgate = acc[:, tc:2 * tc]
    cu = acc[:, 2 * tc:3 * tc] * u

    @pl.when((i * tm) % seq == 0)
    def _():
        carry_ref[...] = jnp.zeros((8, tc), F32)

    car = carry_ref[...]
    p0 = car[0:1, :]
    p1 = car[1:2, :]
    rows = lax.broadcasted_iota(jnp.int32, (tm, tc), 0)
    r1 = pltpu.roll(cu, 1, 0)
    r2 = pltpu.roll(cu, 2, 0)
    prev1 = jnp.where(rows == 0, p1, r1)
    prev2 = jnp.where(rows == 0, p0, jnp.where(rows == 1, p1, r2))
    z_ref[...] = _conv_taps(cu, prev1, prev2, bgate, cw_ref, cb_ref).astype(z_ref.dtype)
    tail = cu[tm - 8:tm, :]
    carry_ref[...] = jnp.concatenate([tail[6:8, :], tail[0:6, :]], axis=0)
    st_ref[0] = tail[6:8, :]


def _mm_conv(h, w, conv_w, conv_b, cast_a, cast_b, *, tm, tc, seq, col0):
    t, k = h.shape
    d = conv_w.shape[1]
    nj = d // tc
    ni = t // tm
    steps = nj * ni
    ra, rb_ = cast_a.shape[0] // steps, cast_b.shape[0] // steps
    assert ra % 16 == 0 and rb_ % 16 == 0 and ra * steps == cast_a.shape[0] and rb_ * steps == cast_b.shape[0]
    slab = lambda r, m: pl.BlockSpec((r, m.shape[1]), lambda j, i: (j * ni + i, 0))
    return pl.pallas_call(
        functools.partial(_mm_conv_kernel, tm=tm, tc=tc, seq=seq),
        grid=(nj, ni),
        in_specs=[pl.BlockSpec((tm, k), lambda j, i: (i, 0)),
                  _wrows(tc, k, col0), _wrows(tc, k, col0 + d), _wrows(tc, k, col0 + 2 * d),
                  pl.BlockSpec((CONV_W, tc), lambda j, i: (0, j)),
                  pl.BlockSpec((1, tc), lambda j, i: (0, j)),
                  slab(ra, cast_a), slab(rb_, cast_b)],
        out_specs=[pl.BlockSpec((tm, tc), lambda j, i: (i, j)),
                   pl.BlockSpec((1, CONV_W - 1, tc), lambda j, i: (i, 0, j)),
                   slab(ra, cast_a), slab(rb_, cast_b)],
        out_shape=[jax.ShapeDtypeStruct((t, d), BF16),
                   jax.ShapeDtypeStruct((t // tm, CONV_W - 1, d), F32),
                   jax.ShapeDtypeStruct(cast_a.shape, BF16),
                   jax.ShapeDtypeStruct(cast_b.shape, BF16)],
        scratch_shapes=[pltpu.VMEM((3 * tc, k), BF16), pltpu.VMEM((8, tc), F32)],
        compiler_params=_cparams(("arbitrary", "arbitrary")),
        name="mm_conv",
    )(h, w, w, w, conv_w, conv_b, cast_a, cast_b)


def _sample_conv_kernel(u_ref, b_ref, c_ref, st0_ref, st1_ref, cw_ref, cb_ref, z_ref, cu_ref, *, tc, seq):
    t = u_ref.shape[0]
    bgate = b_ref[...]
    cu = c_ref[...] * u_ref[...]
    s = lax.rem(lax.broadcasted_iota(jnp.int32, (t, tc), 0), jnp.int32(seq))
    r1 = pltpu.roll(cu, 1, 0)
    r2 = pltpu.roll(cu, 2, 0)
    prev1 = jnp.where(s == 0, st1_ref[...], r1)
    prev2 = jnp.where(s == 0, st0_ref[...], jnp.where(s == 1, st1_ref[...], r2))
    z_ref[...] = _conv_taps(cu, prev1, prev2, bgate, cw_ref, cb_ref)
    cu_ref[...] = cu


def _sample_conv(ubc, st0, st1, conv_w, conv_b, *, tc, seq):
    t = ubc.shape[0]
    d = ubc.shape[1] // 3
    nj = d // tc
    return pl.pallas_call(
        functools.partial(_sample_conv_kernel, tc=tc, seq=seq),
        grid=(nj,),
        in_specs=[pl.BlockSpec((t, tc), lambda j: (0, j)),
                  pl.BlockSpec((t, tc), lambda j: (0, nj + j)),
                  pl.BlockSpec((t, tc), lambda j: (0, 2 * nj + j)),
                  pl.BlockSpec((t, tc), lambda j: (0, j)),
                  pl.BlockSpec((t, tc), lambda j: (0, j)),
                  pl.BlockSpec((CONV_W, tc), lambda j: (0, j)),
                  pl.BlockSpec((1, tc), lambda j: (0, j))],
        out_specs=[pl.BlockSpec((t, tc), lambda j: (0, j)),
                   pl.BlockSpec((t, tc), lambda j: (0, j))],
        out_shape=[jax.ShapeDtypeStruct((t, d), F32), jax.ShapeDtypeStruct((t, d), F32)],
        compiler_params=_cparams(("arbitrary",)),
        name="sample_conv",
    )(ubc, ubc, ubc, st0, st1, conv_w, conv_b)


def _mix_kernel(o_ref, z_ref, wa_ref, wc_ref, ga_ref, gc_ref, m_ref, wab_ref, wcb_ref):
    @pl.when(pl.program_id(1) == 0)
    def _():
        wab_ref[...] = wa_ref[...].astype(BF16)
        wcb_ref[...] = wc_ref[...].astype(BF16)

    a = jnp.dot(o_ref[...].astype(BF16), wab_ref[...], preferred_element_type=F32)
    c = jnp.dot(z_ref[...].astype(BF16), wcb_ref[...], preferred_element_type=F32)
    m_ref[...] = (ga_ref[...].astype(F32) * a + gc_ref[...].astype(F32) * c).astype(m_ref.dtype)


def _mix(o, z, wa, wc, gates, *, tm, tn):
    t, k = o.shape
    n = wa.shape[1]
    nj = n // tn
    return pl.pallas_call(
        _mix_kernel,
        grid=(nj, t // tm),
        in_specs=[pl.BlockSpec((tm, k), lambda j, i: (i, 0)),
                  pl.BlockSpec((tm, k), lambda j, i: (i, 0)),
                  pl.BlockSpec((k, tn), lambda j, i: (0, j)),
                  pl.BlockSpec((k, tn), lambda j, i: (0, j)),
                  pl.BlockSpec((tm, tn), lambda j, i: (i, j)),
                  pl.BlockSpec((tm, tn), lambda j, i: (i, j + nj))],
        out_specs=pl.BlockSpec((tm, tn), lambda j, i: (i, j)),
        out_shape=jax.ShapeDtypeStruct((t, n), BF16),
        scratch_shapes=[pltpu.VMEM((k, tn), BF16), pltpu.VMEM((k, tn), BF16)],
        compiler_params=_cparams(("arbitrary", "arbitrary")),
        name="mix",
    )(o, z, wa, wc, gates, gates)


def _resid_norm_kernel(a_ref, w_ref, x_ref, g_ref, ng_ref, sc_ref, sh_ref, x1_ref, h2_ref, wbf_ref):
    @pl.when(pl.program_id(0) == 0)
    def _():
        wbf_ref[...] = w_ref[...].astype(BF16)

    acc = jnp.dot(a_ref[...], wbf_ref[...], preferred_element_type=F32)
    x1 = x_ref[...] + g_ref[0] * acc
    x1_ref[...] = x1
    r = lax.rsqrt(jnp.mean(x1 * x1, axis=-1, keepdims=True) + RMS_EPS)
    y = (x1 * r) * ng_ref[...]
    h2_ref[...] = (y * (1.0 + sc_ref[0]) + sh_ref[0]).astype(h2_ref.dtype)


def _mm_resid_norm(a, w, x, gate, ng, sc, sh, *, tm, seq):
    t, k = a.shape
    n = w.shape[1]
    row = lambda p: _rowspec(p, tm, n, seq, col=False)
    return pl.pallas_call(
        _resid_norm_kernel,
        grid=(t // tm,),
        in_specs=[pl.BlockSpec((tm, k), lambda i: (i, 0)),
                  pl.BlockSpec((k, n), lambda i: (0, 0), pipeline_mode=pl.Buffered(1)),
                  pl.BlockSpec((tm, n), lambda i: (i, 0)),
                  row(gate),
                  pl.BlockSpec((1, n), lambda i: (0, 0)),
                  row(sc), row(sh)],
        out_specs=[pl.BlockSpec((tm, n), lambda i: (i, 0)), pl.BlockSpec((tm, n), lambda i: (i, 0))],
        out_shape=[jax.ShapeDtypeStruct((t, n), F32), jax.ShapeDtypeStruct((t, n), BF16)],
        scratch_shapes=[pltpu.VMEM((k, n), BF16)],
        compiler_params=_cparams(("arbitrary",)),
        name="mm_resid_norm",
    )(a, w, x, gate, ng, sc, sh)


def _mlp_kernel(h_ref, w1_ref, w2_ref, x_ref, g_ref, nf_ref, o_ref, acc_ref):
    k = pl.program_id(1)

    @pl.when(k == 0)
    def _():
        acc_ref[...] = jnp.zeros_like(acc_ref)

    a = jnp.dot(h_ref[...], w1_ref[...], preferred_element_type=F32)
    a = jnp.square(jnp.maximum(a, 0.0)).astype(BF16)
    acc_ref[...] += jnp.dot(a, w2_ref[...], preferred_element_type=F32)

    @pl.when(k == pl.num_programs(1) - 1)
    def _():
        x2 = x_ref[...] + g_ref[0] * acc_ref[...]
        r = lax.rsqrt(jnp.mean(x2 * x2, axis=-1, keepdims=True) + RMS_EPS)
        o_ref[...] = (x2 * r) * nf_ref[...]


def _mlp_final(h2, w1, w2, x1, gate, normf, *, tm, tf, seq):
    t, d = h2.shape
    f = w1.shape[1]
    return pl.pallas_call(
        _mlp_kernel,
        grid=(t // tm, f // tf),
        in_specs=[pl.BlockSpec((tm, d), lambda i, k: (i, 0)),
                  pl.BlockSpec((d, tf), lambda i, k: (0, k)),
                  pl.BlockSpec((tf, d), lambda i, k: (k, 0)),
                  pl.BlockSpec((tm, d), lambda i, k: (i, 0)),
                  _rowspec_k(gate, tm, d, seq),
                  pl.BlockSpec((1, d), lambda i, k: (0, 0))],
        out_specs=pl.BlockSpec((tm, d), lambda i, k: (i, 0)),
        out_shape=jax.ShapeDtypeStruct((t, d), F32),
        scratch_shapes=[pltpu.VMEM((tm, d), F32)],
        compiler_params=_cparams(("arbitrary", "arbitrary")),
        name="mlp",
    )(h2, w1, w2, x1, gate, normf)


def _rowspec_k(p, tm, d, seq):
    if p.shape[1] == 1:
        return pl.BlockSpec((1, 1, d), lambda i, k: ((i * tm) // seq, 0, 0))
    return pl.BlockSpec((1, tm, d), lambda i, k: (0, i, 0))


def _cmp_part_kernel(pt_ref, *refs):
    pages = refs[:PAGES_PER_STEP]
    w_ref = refs[PAGES_PER_STEP]
    o_ref = refs[PAGES_PER_STEP + 1]
    stage = refs[PAGES_PER_STEP + 2]
    nchunk = CHUNK_ROWS // CMP_STRIDE
    cpp = PAGE // CMP_STRIDE
    nslab = 2 * N_KV
    tiles = [[pg[pl.ds(r, cpp, stride=CMP_STRIDE)] for pg in pages] for r in range(CMP_STRIDE)]
    for kind in range(2):
        cols = [jnp.concatenate([x[:, kind * N_KV:(kind + 1) * N_KV, :].reshape(cpp * N_KV, HEAD_DIM)
                                 for x in tiles[r]], axis=0).astype(BF16) for r in range(CMP_STRIDE)]
        lhs = jnp.concatenate(cols, axis=1)
        acc = jnp.dot(lhs, w_ref[:, kind * 2 * HEAD_DIM:(kind + 1) * 2 * HEAD_DIM],
                      preferred_element_type=F32)
        stage[0] = acc[:, :HEAD_DIM]
        stage[1] = acc[:, HEAD_DIM:]
        for h in range(N_KV):
            o_ref[kind, h] = jnp.concatenate(
                [stage[0, pl.ds(h, nchunk, stride=N_KV), :], stage[1, pl.ds(h, nchunk, stride=N_KV), :]], axis=1)


def _page_spec(n, kind_pair):
    return pl.BlockSpec((None, PAGE, 8, HEAD_DIM),
                        lambda b, p, pt, n=n: (pt[b, p * PAGES_PER_STEP + n], 0, kind_pair, 0))


def _cmp_part(pages4, page_table, w1pairs):
    nb, npg = page_table.shape
    steps = npg // PAGES_PER_STEP
    nchunk = PAGES_PER_STEP * (PAGE // CMP_STRIDE)
    grid_spec = pltpu.PrefetchScalarGridSpec(
        num_scalar_prefetch=1,
        grid=(nb, steps),
        in_specs=[_page_spec(n, 0) for n in range(PAGES_PER_STEP)]
        + [pl.BlockSpec((CMP_STRIDE * HEAD_DIM, 4 * HEAD_DIM), lambda b, p, pt: (0, 0))],
        out_specs=pl.BlockSpec((None, 2, N_KV, nchunk, 2 * HEAD_DIM), lambda b, p, pt: (b, 0, 0, p, 0)),
        scratch_shapes=[pltpu.VMEM((2, nchunk * N_KV, HEAD_DIM), F32)],
    )
    return pl.pallas_call(
        _cmp_part_kernel,
        grid_spec=grid_spec,
        out_shape=jax.ShapeDtypeStruct((nb, 2, N_KV, steps * nchunk, 2 * HEAD_DIM), F32),
        compiler_params=_cparams(("arbitrary", "arbitrary")),
        name="cmp_part",
    )(page_table, *([pages4] * PAGES_PER_STEP), w1pairs)


def _gelu_tanh(x):
    c = np.sqrt(2.0 / np.pi).astype(np.float32)
    return 0.5 * x * (1.0 + jnp.tanh(c * (x + 0.044715 * (x * x * x))))


def _cmp_finish_kernel(part_ref, pe_ref, w1_ref, w2_ref, o_ref, bias_ref):
    @pl.when(pl.program_id(1) == 0)
    def _():
        bias_ref[...] = jnp.dot(pe_ref[...], w1_ref[...], preferred_element_type=F32,
                                precision=lax.Precision.HIGHEST)

    w2 = w2_ref[...].astype(BF16)
    for hd in range(N_KV):
        part = part_ref[hd]
        n = part.shape[0]
        h = part[:, 0:HEAD_DIM] + pltpu.roll(part[:, HEAD_DIM:], n - 1, 0)
        h = h + bias_ref[0:1, :]
        o_ref[hd] = jnp.dot(_gelu_tanh(h).astype(BF16), w2, preferred_element_type=F32).astype(o_ref.dtype)


def _cmp_finish(part, pe8, w1, w2):
    nb, _, _, nchunk, _ = part.shape
    return pl.pallas_call(
        _cmp_finish_kernel,
        grid=(2, nb),
        in_specs=[pl.BlockSpec((None, None, N_KV, nchunk, 2 * HEAD_DIM), lambda k, b: (b, k, 0, 0, 0)),
                  pl.BlockSpec((None, 8, CMP_LEN * HEAD_DIM), lambda k, b: (k, 0, 0)),
                  pl.BlockSpec((None, CMP_LEN * HEAD_DIM, HEAD_DIM), lambda k, b: (k, 0, 0)),
                  pl.BlockSpec((None, HEAD_DIM, HEAD_DIM), lambda k, b: (k, 0, 0))],
        out_specs=pl.BlockSpec((None, None, N_KV, nchunk, HEAD_DIM), lambda k, b: (b, k, 0, 0, 0)),
        out_shape=jax.ShapeDtypeStruct((nb, 2, N_KV, nchunk, HEAD_DIM), BF16),
        scratch_shapes=[pltpu.VMEM((8, HEAD_DIM), F32)],
        compiler_params=_cparams(("arbitrary", "arbitrary")),
        name="cmp_finish",
    )(part, pe8, w1, w2)


def _select_topk(score, valid, forced, blk, axis):
    sc = jnp.where(valid, jnp.where(forced, jnp.inf, score), -jnp.inf)
    sel = jnp.zeros(score.shape, F32)
    big = jnp.int32(1 << 20)
    for _ in range(SLC_TOP):
        mx = jnp.max(sc, axis=axis, keepdims=True)
        idx = jnp.min(jnp.where(sc == mx, blk, big), axis=axis, keepdims=True)
        hit = blk == idx
        sel = jnp.where(hit & (mx > -jnp.inf), 1.0, sel)
        sc = jnp.where(hit, -jnp.inf, sc)
    return sel


def _overlap_matrix(nc_pad, n_blocks_pad, nc, n_blocks):
    cs = np.arange(nc_pad) * CMP_STRIDE
    sb = np.arange(n_blocks_pad) * SLC_LEN
    ov = np.clip(np.minimum(cs[:, None] + CMP_LEN, sb[None, :] + SLC_LEN)
                 - np.maximum(cs[:, None], sb[None, :]), 0, None)
    m = (ov / CMP_STRIDE).astype(np.float32)
    m[nc:, :] = 0.0
    m[:, n_blocks:] = 0.0
    return m


def _attn_prompt_kernel(q_ref, kcb, vcb, ksb, vsb, kwb, vwb, g_ref, mt_ref, e_ref, o_ref,
                        m_scr, l_scr, acc_scr, bias_scr, gate_scr, *, tq, seq, ck):
    qi = pl.program_id(2)
    nc = seq // CMP_STRIDE - 1
    nsb = seq // SLC_LEN
    rb = min(tq, 128)

    q = q_ref[...]
    qq = jnp.concatenate([q[:, g * HEAD_DIM:(g + 1) * HEAD_DIM] for g in range(GROUP)], axis=0)
    rq = GROUP * tq
    q0 = qi * tq

    gt = jax.nn.sigmoid(g_ref[...])
    ngc = 4 * GROUP
    hi = gt.astype(BF16)
    r1 = gt - hi.astype(F32)
    mid = r1.astype(BF16)
    lo = (r1 - mid.astype(F32)).astype(BF16)
    g3 = jnp.concatenate([hi, mid, lo, jnp.zeros((tq, HEAD_DIM - 3 * ngc), BF16)], axis=1)
    srow = lax.broadcasted_iota(jnp.int32, (HEAD_DIM, 3 * GROUP * HEAD_DIM), 0)
    scol = lax.broadcasted_iota(jnp.int32, (HEAD_DIM, 3 * GROUP * HEAD_DIM), 1)
    spread = ((srow < 3 * ngc) & ((srow & (ngc - 1)) == lax.div(scol, jnp.int32(HEAD_DIM)))).astype(BF16)
    gl = jnp.dot(g3, spread, preferred_element_type=F32)
    for c in range(3 * GROUP):
        gate_scr[c] = gl[:, c * HEAD_DIM:(c + 1) * HEAD_DIM]

    ncp = kcb.shape[0]
    s = lax.dot_general(qq, kcb[...], _NT, preferred_element_type=F32) * SCALE
    qpos = q0 + (lax.broadcasted_iota(jnp.int32, (rq, ncp), 0) & (tq - 1))
    col = lax.broadcasted_iota(jnp.int32, (rq, ncp), 1)
    vis = (col * CMP_STRIDE + (CMP_LEN - 1) <= qpos) & (col < nc)
    s = jnp.where(vis, s, NEG)
    mx = jnp.max(s, axis=1, keepdims=True)
    e = jnp.where(vis, jnp.exp(s - mx), 0.0)
    den = jnp.sum(e, axis=1, keepdims=True)
    p = e / jnp.where(den > 0.0, den, 1.0)
    o_cmp = jnp.dot(p.astype(BF16), vcb[...], preferred_element_type=F32)
    pg = p[0:tq]
    for g in range(1, GROUP):
        pg = pg + p[g * tq:(g + 1) * tq]

    sct = lax.dot_general(mt_ref[...], pg, _NT, preferred_element_type=F32,
                          precision=lax.Precision.HIGHEST)
    blk = lax.broadcasted_iota(jnp.int32, (nsb, tq), 0)
    cur = lax.div(q0 + lax.broadcasted_iota(jnp.int32, (nsb, tq), 1), jnp.int32(SLC_LEN))
    valid = blk <= cur
    forced = (blk == 0) | (blk > cur - N_LOCAL)
    need_topk = (q0 + tq - 1) // SLC_LEN + 1 > SLC_TOP
    sel_t = lax.cond(need_topk,
                     lambda: _select_topk(sct, valid, forced, blk, 0),
                     lambda: valid.astype(F32)).astype(BF16)

    def sweep(k_ref, v_ref, k_start, n_chunks, bias_fn):
        nl = ck // HEAD_DIM

        def scores(c):
            k0 = pl.multiple_of(k_start + c * ck, 128)
            return lax.dot_general(qq, k_ref[pl.ds(k0, ck), :], _NT, preferred_element_type=F32)

        def chunk_inputs(c):
            k0 = pl.multiple_of(k_start + c * ck, 128)
            return scores(c), bias_fn(c, k0)

        c_exp = np.float32(SCALE * np.log2(np.e))

        def reduce_chunk(c, sb, first):
            sc_, bias = sb
            k0 = pl.multiple_of(k_start + c * ck, 128)
            vch = jnp.concatenate([v_ref[pl.ds(k0, ck), :], jnp.ones((ck, HEAD_DIM), BF16)], axis=1)
            alphas, ps = [], []
            nblk = rq // rb
            for r in range(nblk):
                rows = slice(r * rb, (r + 1) * rb)
                b0 = (r * rb) % tq
                sg = [sc_[rows, j * HEAD_DIM:(j + 1) * HEAD_DIM]
                      + bias[b0:b0 + rb, j * HEAD_DIM:(j + 1) * HEAD_DIM] for j in range(nl)]
                mx = functools.reduce(jnp.maximum, sg)
                row_max = jnp.max(mx, axis=1, keepdims=True)
                if first:
                    m_new = jnp.broadcast_to(row_max, (rb, HEAD_DIM))
                else:
                    m_old = m_scr[rows]
                    m_new = jnp.maximum(m_old, row_max)
                    alphas.append(jnp.exp2((m_old - m_new) * c_exp))
                pj = [jnp.exp2((x - m_new) * c_exp) for x in sg]
                m_scr[rows] = m_new
                ps.append(jnp.concatenate([x.astype(BF16) for x in pj], axis=1))
                if (r + 1) % (nblk // 2) == 0:
                    half = slice((r + 1 - nblk // 2) * rb, (r + 1) * rb)
                    pv = jnp.dot(jnp.concatenate(ps, axis=0), vch, preferred_element_type=F32)
                    if first:
                        acc_scr[half] = pv[:, :HEAD_DIM]
                        l_scr[half] = pv[:, HEAD_DIM:]
                    else:
                        al = jnp.concatenate(alphas, axis=0)
                        acc_scr[half] = al * acc_scr[half] + pv[:, :HEAD_DIM]
                        l_scr[half] = al * l_scr[half] + pv[:, HEAD_DIM:]
                    alphas, ps = [], []

        def body(c, carry):
            reduce_chunk(c, chunk_inputs(c), False)
            return carry

        reduce_chunk(0, chunk_inputs(0), True)
        lax.fori_loop(1, n_chunks, body, 0)
        return acc_scr[...] / l_scr[...]

    qp = q0 + lax.broadcasted_iota(jnp.int32, (tq, ck), 0)
    kcol = lax.broadcasted_iota(jnp.int32, (tq, ck), 1)

    eye = (lax.broadcasted_iota(jnp.int32, (nsb, HEAD_DIM), 0)
           == lax.broadcasted_iota(jnp.int32, (nsb, HEAD_DIM), 1)).astype(BF16)
    sel_q = lax.dot_general(sel_t, eye, _TN, preferred_element_type=F32).astype(BF16)

    n_slc = (q0 + tq + ck - 1) // ck

    def put_mask(c):
        selq = jnp.dot(sel_q, e_ref[c], preferred_element_type=F32)
        bias_scr[c] = jnp.where((selq > 0.5) & (c * ck + kcol <= qp), 0.0, NEG)

    def mask_body(i, carry):
        put_mask(2 * i)
        put_mask(jnp.minimum(2 * i + 1, seq // ck - 1))
        return carry

    lax.fori_loop(0, (n_slc + 1) // 2, mask_body, 0)
    o_slc = sweep(ksb, vsb, 0, n_slc, lambda c, k0: bias_scr[c])

    span = -(-(WINDOW + tq) // ck) * ck

    def win_bias(c, k0):
        dlt = qp - (k0 + kcol)
        return jnp.where((dlt >= 0) & (dlt < WINDOW), 0.0, NEG)

    o_win = sweep(kwb, vwb, jnp.maximum(q0 + tq - span, 0), span // ck, win_bias)

    for g in range(GROUP):
        r0, r1 = g * tq, (g + 1) * tq
        og = gate_scr[g] * o_cmp[r0:r1] + gate_scr[GROUP + g] * o_slc[r0:r1]
        og = og + gate_scr[2 * GROUP + g] * o_win[r0:r1]
        o_ref[:, g * HEAD_DIM:(g + 1) * HEAD_DIM] = og.astype(o_ref.dtype)


def _attn_prompt(q, kcv, kv, win, g_re, *, nb, seq, tq=512, ck=512):
    assert tq & (tq - 1) == 0 and seq % ck == 0 and WINDOW % tq == 0
    t = q.shape[0]
    ncp = seq // CMP_STRIDE
    nsb = seq // SLC_LEN
    nq = seq // tq
    mt = jnp.asarray(_overlap_matrix(ncp, nsb, ncp - 1, nsb).T)
    kk = np.arange(seq)
    assert nsb <= HEAD_DIM
    e3 = (kk[None, :] // SLC_LEN == np.arange(HEAD_DIM)[:, None]).astype(np.float32)
    e3 = jnp.asarray(e3.reshape(HEAD_DIM, seq // ck, ck).transpose(1, 0, 2), dtype=BF16)
    slab = lambda cb: pl.BlockSpec((seq, HEAD_DIM), lambda b, h, i, cb=cb: (b, cb + h))
    return pl.pallas_call(
        functools.partial(_attn_prompt_kernel, tq=tq, seq=seq, ck=ck),
        grid=(nb, N_KV, nq),
        in_specs=[pl.BlockSpec((tq, GROUP * HEAD_DIM), lambda b, h, i: (b * nq + i, h)),
                  pl.BlockSpec((None, None, None, ncp, HEAD_DIM), lambda b, h, i: (b, 0, h, 0, 0)),
                  pl.BlockSpec((None, None, None, ncp, HEAD_DIM), lambda b, h, i: (b, 1, h, 0, 0)),
                  slab(0), slab(N_KV), slab(0), slab(N_KV),
                  pl.BlockSpec((None, None, tq, 4 * GROUP), lambda b, h, i: (b, h, i, 0)),
                  pl.BlockSpec((nsb, ncp), lambda b, h, i: (0, 0)),
                  pl.BlockSpec((seq // ck, HEAD_DIM, ck), lambda b, h, i: (0, 0, 0))],
        out_specs=pl.BlockSpec((tq, GROUP * HEAD_DIM), lambda b, h, i: (b * nq + i, h)),
        out_shape=jax.ShapeDtypeStruct((t, N_HEADS * HEAD_DIM), BF16),
        scratch_shapes=[pltpu.VMEM((GROUP * tq, HEAD_DIM), F32)] * 3
        + [pltpu.VMEM((seq // ck, tq, ck), F32), pltpu.VMEM((3 * GROUP, tq, HEAD_DIM), F32)],
        compiler_params=_cparams(("arbitrary", "arbitrary", "arbitrary")),
        name="attn_prompt",
    )(q, kcv, kcv, kv, kv, win, win, g_re, mt, e3)


def _attn_sample_kernel(pt_ref, cache_ref, q_ref, kcv_ref, kvn_ref, wn_ref, st_ref, g_ref, ms_ref, e_ref, o_ref,
                        wso_ref, m_scr, l_scr, acc_scr, sel_scr, ocmp_scr, kv_buf, sem, *, past, ds):
    b = pl.program_id(0)
    p = pl.program_id(1)
    n_steps = pl.num_programs(1)

    def page_copies(bb, pp, slot, for_wait=False):
        cps = []
        for n in range(PAGES_PER_STEP):
            page = 0 if for_wait else pt_ref[bb, pp * PAGES_PER_STEP + n]
            for cb in range(2 * N_KV):
                cps.append(pltpu.make_async_copy(cache_ref.at[page, :, 2 * N_KV + cb, :],
                                                 kv_buf.at[slot, cb, pl.ds(n * PAGE, PAGE), :],
                                                 sem.at[slot]))
        return cps

    t = b * n_steps + p
    slot = lax.rem(t, 2)

    @pl.when(t == 0)
    def _():
        for cp in page_copies(b, p, slot):
            cp.start()

    @pl.when(t + 1 < pl.num_programs(0) * n_steps)
    def _():
        wrap = p + 1 == n_steps
        for cp in page_copies(jnp.where(wrap, b + 1, b), jnp.where(wrap, 0, p + 1), 1 - slot):
            cp.start()
    rq = GROUP * ds
    ncp = kcv_ref.shape[2]
    nc = ncp - 1
    n_sel_chunks = sel_scr.shape[1]
    lanes = ms_ref.shape[1]

    def q_rows(h):
        c0 = h * GROUP * HEAD_DIM
        return jnp.concatenate(
            [q_ref[:, c0 + g * HEAD_DIM:c0 + (g + 1) * HEAD_DIM] for g in range(GROUP)], axis=0).astype(BF16)

    def online_update(h, sc_, vals):
        m_i = m_scr[h]
        m_new = jnp.maximum(m_i, jnp.max(sc_, axis=1, keepdims=True))
        alpha = jnp.exp(m_i - m_new)
        pp = jnp.exp(sc_ - m_new)
        l_scr[h] = alpha * l_scr[h] + jnp.sum(pp, axis=1, keepdims=True)
        acc_scr[h] = alpha * acc_scr[h] + jnp.dot(pp.astype(BF16), vals, preferred_element_type=F32)
        m_scr[h] = m_new

    @pl.when(p == 0)
    def _():
        pgs = []
        for h in range(N_KV):
            qq = q_rows(h)
            kc = kcv_ref[0, h]
            vc = kcv_ref[1, h]
            s = lax.dot_general(qq, kc, _NT, preferred_element_type=F32) * SCALE
            qpos = past + (lax.broadcasted_iota(jnp.int32, (rq, ncp), 0) & (ds - 1))
            col = lax.broadcasted_iota(jnp.int32, (rq, ncp), 1)
            vis = (col * CMP_STRIDE + (CMP_LEN - 1) <= qpos) & (col < nc)
            s = jnp.where(vis, s, NEG)
            mx = jnp.max(s, axis=1, keepdims=True)
            e = jnp.where(vis, jnp.exp(s - mx), 0.0)
            den = jnp.sum(e, axis=1, keepdims=True)
            pr = e / jnp.where(den > 0.0, den, 1.0)
            ocmp_scr[h] = jnp.dot(pr.astype(BF16), vc, preferred_element_type=F32)
            pg = pr[0:ds]
            for g in range(1, GROUP):
                pg = pg + pr[g * ds:(g + 1) * ds]
            pgs.append(pg)
            m_scr[h] = jnp.full((rq, 1), NEG, F32)
            l_scr[h] = jnp.zeros((rq, 1), F32)
            acc_scr[h] = jnp.zeros((rq, HEAD_DIM), F32)
        nr = N_KV * ds
        pg_all = jnp.concatenate(pgs, axis=0)
        hi = pg_all.astype(BF16)
        r1 = pg_all - hi.astype(F32)
        mid = r1.astype(BF16)
        lo = (r1 - mid.astype(F32)).astype(BF16)
        sc3 = jnp.dot(jnp.concatenate([hi, mid, lo], axis=0), ms_ref[...], preferred_element_type=F32)
        score = (sc3[0:nr] + sc3[nr:2 * nr]) + sc3[2 * nr:3 * nr]
        blk = lax.broadcasted_iota(jnp.int32, (nr, lanes), 1)
        qrow = lax.broadcasted_iota(jnp.int32, (nr, lanes), 0) & (ds - 1)
        cur = lax.div(past + qrow, jnp.int32(SLC_LEN))
        valid = blk <= cur
        forced = (blk == 0) | (blk > cur - N_LOCAL)
        sel = _select_topk(score, valid, forced, blk, 1)
        lane_pad = jnp.zeros((ds, HEAD_DIM - BLK_PER_CHUNK), F32)
        for h in range(N_KV):
            for c in range(n_sel_chunks):
                sel_scr[h, c] = jnp.concatenate(
                    [sel[h * ds:(h + 1) * ds, c * BLK_PER_CHUNK:(c + 1) * BLK_PER_CHUNK], lane_pad], axis=1)

    for cp in page_copies(b, p, slot, for_wait=True):
        cp.wait()

    k0 = p * CHUNK_ROWS
    for h in range(N_KV):
        qq = q_rows(h)
        kch = kv_buf[slot, h].astype(BF16)
        vch = kv_buf[slot, N_KV + h].astype(BF16)
        sc_ = lax.dot_general(qq, kch, _NT, preferred_element_type=F32) * SCALE
        selq = jnp.dot(sel_scr[h, p].astype(BF16), e_ref[...], preferred_element_type=F32)
        kpos = k0 + lax.broadcasted_iota(jnp.int32, (ds, CHUNK_ROWS), 1)
        qp = past + lax.broadcasted_iota(jnp.int32, (ds, CHUNK_ROWS), 0)
        okf = jnp.where((selq > 0.5) & (kpos <= qp), 1.0, 0.0)
        ok = jnp.concatenate([okf] * GROUP, axis=0) > 0.5
        online_update(h, jnp.where(ok, sc_, NEG), vch)

    @pl.when(p == n_steps - 1)
    def _():
        gt = jax.nn.sigmoid(g_ref[...])
        zpad = jnp.zeros((HEAD_DIM - ds, HEAD_DIM), F32)
        for h in range(N_KV):
            qq = q_rows(h)
            c_k = 2 * KV_W + h * HEAD_DIM
            c_v = 3 * KV_W + h * HEAD_DIM
            kn = jnp.concatenate([kvn_ref[:, c_k:c_k + HEAD_DIM], zpad], axis=0).astype(BF16)
            vn = jnp.concatenate([kvn_ref[:, c_v:c_v + HEAD_DIM], zpad], axis=0).astype(BF16)
            sn = lax.dot_general(qq, kn, _NT, preferred_element_type=F32) * SCALE
            last_sel = sel_scr[h, n_sel_chunks - 1]
            nb_last = (past // SLC_LEN) % BLK_PER_CHUNK
            seln = jnp.sum(jnp.where(lax.broadcasted_iota(jnp.int32, (ds, HEAD_DIM), 1) == nb_last,
                                     last_sel, 0.0), axis=1, keepdims=True)
            seln = jnp.concatenate([seln] * GROUP, axis=0)
            srow = lax.broadcasted_iota(jnp.int32, (rq, HEAD_DIM), 0) & (ds - 1)
            kcol = lax.broadcasted_iota(jnp.int32, (rq, HEAD_DIM), 1)
            okn = (seln > 0.5) & (kcol <= srow) & (kcol < ds)
            online_update(h, jnp.where(okn, sn, NEG), vn)
            o_slc = acc_scr[h] / l_scr[h]
            wb = st_ref.shape[0]
            band = wb + HEAD_DIM
            st_flat = st_ref.reshape(wb * 8, HEAD_DIM)
            kw = jnp.concatenate([st_flat[pl.ds(h, wb, stride=8), :],
                                  wn_ref[:, h * HEAD_DIM:(h + 1) * HEAD_DIM], zpad], axis=0).astype(BF16)
            vw = jnp.concatenate([st_flat[pl.ds(N_KV + h, wb, stride=8), :],
                                  wn_ref[:, KV_W + h * HEAD_DIM:KV_W + (h + 1) * HEAD_DIM], zpad],
                                 axis=0).astype(BF16)
            sw = lax.dot_general(qq, kw, _NT, preferred_element_type=F32) * SCALE
            srw = lax.broadcasted_iota(jnp.int32, (rq, band), 0) & (ds - 1)
            idx = lax.broadcasted_iota(jnp.int32, (rq, band), 1)
            dlt = (wb + srw) - idx
            sw = jnp.where((dlt >= 0) & (dlt < WINDOW) & (idx < wb + ds), sw, NEG)
            ew = jnp.exp(sw - jnp.max(sw, axis=1, keepdims=True))
            o_win = jnp.dot(ew.astype(BF16), vw, preferred_element_type=F32) / jnp.sum(ew, axis=1, keepdims=True)
            o_cmp = ocmp_scr[h]
            gh = gt[h]
            for g in range(GROUP):
                r0, r1 = g * ds, (g + 1) * ds
                og = gh[:, g:g + 1] * o_cmp[r0:r1] + gh[:, GROUP + g:GROUP + g + 1] * o_slc[r0:r1]
                og = og + gh[:, 2 * GROUP + g:2 * GROUP + g + 1] * o_win[r0:r1]
                c0 = (h * GROUP + g) * HEAD_DIM
                o_ref[:, c0:c0 + HEAD_DIM] = og
        wb = st_ref.shape[0]
        wso_ref[0:wb - ds] = st_ref[ds:wb]
        wso_flat = wso_ref.reshape(wb * 8, HEAD_DIM)
        for cb in range(2 * N_KV):
            wso_flat[pl.ds((wb - ds) * 8 + cb, ds, stride=8), :] = wn_ref[:, cb * HEAD_DIM:(cb + 1) * HEAD_DIM]


def _attn_sample(q, kcv, kv_new, win_new, win_state, g_re, cache4, page_table, *, ds):
    nb, npg = page_table.shape
    past = npg * PAGE
    assert ds & (ds - 1) == 0 and ds <= SLC_LEN and past % SLC_LEN == 0 and npg % PAGES_PER_STEP == 0
    steps = npg // PAGES_PER_STEP
    ncp = kcv.shape[3]
    wb = win_state.shape[1]
    n_blocks = past // SLC_LEN + 1
    n_sel_chunks = -(-n_blocks // BLK_PER_CHUNK)
    lanes = -(-n_sel_chunks * BLK_PER_CHUNK // HEAD_DIM) * HEAD_DIM
    ms = jnp.asarray(_overlap_matrix(ncp, lanes, ncp - 1, n_blocks), dtype=BF16)
    ee = np.zeros((HEAD_DIM, CHUNK_ROWS), np.float32)
    ee[:BLK_PER_CHUNK] = np.arange(CHUNK_ROWS)[None, :] // SLC_LEN == np.arange(BLK_PER_CHUNK)[:, None]
    ee = jnp.asarray(ee, dtype=BF16)
    rq = GROUP * ds

    grid_spec = pltpu.PrefetchScalarGridSpec(
        num_scalar_prefetch=1,
        grid=(nb, steps),
        in_specs=[
            pl.BlockSpec(memory_space=pl.ANY),
            pl.BlockSpec((ds, N_HEADS * HEAD_DIM), lambda b, p, pt: (b, 0)),
            pl.BlockSpec((None, 2, N_KV, ncp, HEAD_DIM), lambda b, p, pt: (b, 0, 0, 0, 0)),
            pl.BlockSpec((ds, 4 * KV_W), lambda b, p, pt: (b, 0)),
            pl.BlockSpec((ds, 2 * KV_W), lambda b, p, pt: (b, 0)),
            pl.BlockSpec((None, wb, 8, HEAD_DIM), lambda b, p, pt: (b, 0, 0, 0)),
            pl.BlockSpec((None, N_KV, ds, 4 * GROUP), lambda b, p, pt: (b, 0, 0, 0)),
            pl.BlockSpec(ms.shape, lambda b, p, pt: (0, 0)),
            pl.BlockSpec(ee.shape, lambda b, p, pt: (0, 0))],
        out_specs=[pl.BlockSpec((ds, N_HEADS * HEAD_DIM), lambda b, p, pt: (b, 0)),
                   pl.BlockSpec((None, wb, 8, HEAD_DIM), lambda b, p, pt: (b, 0, 0, 0))],
        scratch_shapes=[pltpu.VMEM((N_KV, rq, 1), F32), pltpu.VMEM((N_KV, rq, 1), F32),
                        pltpu.VMEM((N_KV, rq, HEAD_DIM), F32),
                        pltpu.VMEM((N_KV, n_sel_chunks, ds, HEAD_DIM), F32),
                        pltpu.VMEM((N_KV, rq, HEAD_DIM), F32),
                        pltpu.VMEM((2, 2 * N_KV, CHUNK_ROWS, HEAD_DIM), F32),
                        pltpu.SemaphoreType.DMA((2,))],
    )
    return pl.pallas_call(
        functools.partial(_attn_sample_kernel, past=past, ds=ds),
        grid_spec=grid_spec,
        out_shape=[jax.ShapeDtypeStruct((nb * ds, N_HEADS * HEAD_DIM), F32),
                   jax.ShapeDtypeStruct((nb, wb, 8, HEAD_DIM), F32)],
        compiler_params=_cparams(("arbitrary", "arbitrary")),
        name="attn_sample",
    )(page_table, cache4, q, kcv, kv_new, win_new, win_state, g_re, ms, ee)


_Q0, _KV0, _WIN0, _G0 = 0, 2048, 4096, 5120
_REST0 = _G0 + 3 * N_HEADS


def _split_w_in(w_in):
    wt = w_in.T
    return wt, wt[:_G0 + HEAD_DIM].astype(BF16)


def _gate_layout(g_logits, nb, seq):
    g = g_logits[:, :3 * N_HEADS].reshape(nb, seq, 3, N_KV, GROUP)
    g = g.transpose(0, 3, 1, 2, 4).reshape(nb, N_KV, seq, 3 * GROUP)
    return jnp.pad(g, ((0, 0), (0, 0), (0, 0), (0, GROUP)))


def _cmp_weights(w_k1, w_k2, pe_k, w_v1, w_v2, pe_v):
    def by_row(w1):
        w = w1.reshape(2, CMP_STRIDE, HEAD_DIM, HEAD_DIM).transpose(1, 2, 0, 3)
        return w.reshape(CMP_STRIDE * HEAD_DIM, 2 * HEAD_DIM)
    w1pairs = jnp.concatenate([by_row(w_k1), by_row(w_v1)], axis=1).astype(BF16)
    pe8 = jnp.stack([jnp.broadcast_to(pe_k.reshape(1, -1), (8, CMP_LEN * HEAD_DIM)),
                     jnp.broadcast_to(pe_v.reshape(1, -1), (8, CMP_LEN * HEAD_DIM))])
    return w1pairs, pe8, jnp.stack([w_k1, w_v1]), jnp.stack([w_k2, w_v2])


def kernel(x_prompt, x_sample, c_prompt, c_sample, cache_nsa_kv, page_table, state_win_kv, state_conv, w_ada, b_ada, norm1_g, norm2_g, w_in, w_cmp_k1, w_cmp_k2, pe_cmp_k, w_cmp_v1, w_cmp_v2, pe_cmp_v, conv_w, conv_b, w_attn_proj, w_conv_proj, w_out, w_mlp1, w_mlp2, normf_g):
    d = D_MODEL
    nbp, seq, _ = x_prompt.shape
    nbs, ds, _ = x_sample.shape
    tp = nbp * seq
    ts = nbs * ds
    tc = 256
    depth = w_in.shape[0]
    assert depth == 1

    xp = x_prompt.reshape(tp, d)
    xs = x_sample.reshape(ts, d)
    normf = normf_g.reshape(1, d)

    l = 0
    c_all = jnp.concatenate([c_prompt, c_sample, jnp.zeros((16 - nbp - nbs, d), F32)], axis=0)
    ada = _ada(c_all, w_ada[l], b_ada[l].reshape(1, -1)).reshape(16, 6, d)
    ada_p = [ada[:nbp, k][:, None, :] for k in range(6)]
    ada_s = [jnp.repeat(ada[nbp:nbp + nbs, k], ds, axis=0)[None] for k in range(6)]

    w_t, w_head = _split_w_in(w_in[l])
    gates0 = _REST0 + 3 * d
    wa = w_attn_proj[l]
    wc = w_conv_proj[l]
    wo = w_out[l]
    g1n = norm1_g[l].reshape(1, d)
    g2n = norm2_g[l].reshape(1, d)
    cw = conv_w[l]
    cb = conv_b[l].reshape(1, d)
    w1pairs, pe8, w1s, w2s = _cmp_weights(w_cmp_k1[l], w_cmp_k2[l], pe_cmp_k[l],
                                          w_cmp_v1[l], w_cmp_v2[l], pe_cmp_v[l])

    tm = 1024
    hp, q_p = _norm_mm(xp, g1n, ada_p[1], ada_p[0], w_head, tm=tm, tn=1024, seq=seq, out_dtype=BF16,
                       col0=_Q0, n=N_HEADS * HEAD_DIM, name="norm_mm_q")
    wbp = min(WINDOW, seq)
    kv6_p, kvb_p = _mm_kv(hp, w_head, tm=tm, col0=_KV0)
    winb_p, win6_p, g_p = _mm_win(hp, w_head, tm=wbp, seq=seq, col0=_WIN0, gcol0=_G0)
    gates_p = _mm_wstat(hp, w_t, tm=tm, tn=1024, out_dtype=BF16, act="sigmoid", name="mm_gates",
                        col0=gates0, n=2 * d)
    z_p, conv_tiles, w1, w2 = _mm_conv(hp, w_t, cw, cb, w_mlp1[l], w_mlp2[l], tm=tm, tc=tc, seq=seq, col0=_REST0)

    pt_p = jnp.arange(tp // PAGE, dtype=jnp.int32).reshape(nbp, seq // PAGE)
    part_p = _cmp_part(kv6_p.reshape(tp // PAGE, PAGE, 16, HEAD_DIM), pt_p, w1pairs)
    kcv_p = _cmp_finish(part_p, pe8, w1s, w2s)
    o_p = _attn_prompt(q_p, kcv_p, kvb_p, winb_p, g_p, nb=nbp, seq=seq)

    mixed_p = _mix(o_p, z_p, wa, wc, gates_p, tm=tm, tn=512)
    x1_p, h2_p = _mm_resid_norm(mixed_p, wo, xp, ada_p[2], g2n, ada_p[4], ada_p[3], tm=512, seq=seq)
    y_p = _mlp_final(h2_p, w1, w2, x1_p, ada_p[5], normf, tm=512, tf=1024, seq=seq)

    hs, q_s = _norm_mm(xs, g1n, ada_s[1], ada_s[0], w_head, tm=ts, tn=512, seq=ds, out_dtype=F32,
                       col0=_Q0, n=N_HEADS * HEAD_DIM, name="norm_mm_q_s")
    kv_s = _mm(hs, w_head, tm=ts, tn=512, out_dtype=F32, name="mm_kv_s", col0=_KV0, n=4 * KV_W)
    win_s = _mm(hs, w_head, tm=ts, tn=512, out_dtype=F32, name="mm_win_s", col0=_WIN0, n=2 * KV_W)
    g_s = _mm(hs, w_head, tm=ts, tn=HEAD_DIM, out_dtype=F32, name="mm_g_s", col0=_G0, n=HEAD_DIM)
    gates_s = _mm_wstat(hs, w_t, tm=ts, tn=512, out_dtype=F32, act="sigmoid", name="mm_gates_s",
                        col0=gates0, n=2 * d)
    ubc_s = _mm_wstat(hs, w_t, tm=ts, tn=512, out_dtype=F32, name="mm_ubc_s", col0=_REST0, n=3 * d)
    st = state_conv[l]
    z_s, cu_s = _sample_conv(ubc_s, jnp.repeat(st[:, 0], ds, axis=0), jnp.repeat(st[:, 1], ds, axis=0),
                             cw, cb, tc=tc, seq=ds)

    cache4 = cache_nsa_kv[l].reshape(cache_nsa_kv.shape[1], PAGE, 16, HEAD_DIM)
    part_s = _cmp_part(cache4, page_table, w1pairs)
    kcv_s = _cmp_finish(part_s, pe8, w1s, w2s)
    wbs = state_win_kv.shape[2]
    wst = state_win_kv[l].reshape(nbs, wbs, 8, HEAD_DIM)
    o_s, wst_next = _attn_sample(q_s, kcv_s, kv_s, win_s, wst, _gate_layout(g_s, nbs, ds), cache4, page_table,
                                 ds=ds)

    mixed_s = _mix(o_s, z_s, wa, wc, gates_s, tm=ts, tn=512)
    x1_s, h2_s = _mm_resid_norm(mixed_s, wo, xs, ada_s[2], g2n, ada_s[4], ada_s[3], tm=ts, seq=ds)
    y_s = _mlp_final(h2_s, w1, w2, x1_s, ada_s[5], normf, tm=ts, tf=512, seq=ds)

    kv_prompt = kv6_p.reshape(1, nbp, seq, 4, N_KV, HEAD_DIM)
    kv_sample = kv_s.reshape(1, nbs, ds, 4, N_KV, HEAD_DIM)
    win_prompt = win6_p.reshape(1, nbp, wbp, 2, N_KV, HEAD_DIM)
    win_sample = wst_next.reshape(1, nbs, wbs, 2, N_KV, HEAD_DIM)
    tiles_per_seq = seq // tm
    conv_prompt = conv_tiles[tiles_per_seq - 1::tiles_per_seq][None]
    conv_sample = cu_s.reshape(nbs, ds, d)[None, :, ds - (CONV_W - 1):]
    return (y_p.reshape(nbp, seq, d), y_s.reshape(nbs, ds, d), kv_prompt, kv_sample,
            win_prompt, win_sample, conv_prompt, conv_sample)
```

```python
import functools

import numpy as np
import jax
import jax.numpy as jnp
from jax import lax
from jax.experimental import pallas as pl
from jax.experimental.pallas import tpu as pltpu

F32 = jnp.float32
BF16 = jnp.bfloat16

D_MODEL = 2048
HEAD_DIM = 128
N_HEADS = 16
N_KV = 4
GROUP = 4
KV_W = N_KV * HEAD_DIM
CMP_LEN = 32
CMP_STRIDE = 16
SLC_LEN = 64
SLC_TOP = 16
N_LOCAL = 2
WINDOW = 512
CONV_W = 3
RMS_EPS = 1e-6
NEG = -1e30
SCALE = HEAD_DIM ** -0.5
PAGE = 128
PAGES_PER_STEP = 16
CHUNK_ROWS = PAGES_PER_STEP * PAGE
BLK_PER_CHUNK = CHUNK_ROWS // SLC_LEN
V7X_VMEM_BYTES = 64 * 1024 * 1024
VMEM_LIMIT = V7X_VMEM_BYTES - 8 * 1024 * 1024

_NT = (((1,), (1,)), ((), ()))
_TN = (((0,), (0,)), ((), ()))


def _cparams(sem):
    return pltpu.CompilerParams(dimension_semantics=sem, vmem_limit_bytes=VMEM_LIMIT)


def _ada_kernel(c_ref, w_ref, b_ref, o_ref):
    c = c_ref[...]
    a = (c * jax.nn.sigmoid(c)).astype(BF16)
    o_ref[...] = jnp.dot(a, w_ref[...].astype(BF16), preferred_element_type=F32) + b_ref[...]


def _ada(c, w, b, tn=1024):
    m, k = c.shape
    n = w.shape[1]
    return pl.pallas_call(
        _ada_kernel,
        grid=(n // tn,),
        in_specs=[pl.BlockSpec((m, k), lambda j: (0, 0)),
                  pl.BlockSpec((k, tn), lambda j: (0, j)),
                  pl.BlockSpec((1, tn), lambda j: (0, j))],
        out_specs=pl.BlockSpec((m, tn), lambda j: (0, j)),
        out_shape=jax.ShapeDtypeStruct((m, n), F32),
        compiler_params=_cparams(("arbitrary",)),
        name="ada",
    )(c, w, b)


def _rowspec(p, tm, tn, seq, col=True):
    gr = p.shape[1]
    if gr == 1:
        if col:
            return pl.BlockSpec((1, 1, tn), lambda i, j: ((i * tm) // seq, 0, j))
        return pl.BlockSpec((1, 1, tn), lambda i: ((i * tm) // seq, 0, 0))
    if col:
        return pl.BlockSpec((1, tm, tn), lambda i, j: (0, i, j))
    return pl.BlockSpec((1, tm, tn), lambda i: (0, i, 0))


def _xwt(x_ref, wt_ref):
    return lax.dot_general(x_ref[...].astype(BF16), wt_ref[...], _NT, preferred_element_type=F32)


def _mm_kernel(x_ref, w_ref, o_ref, *, act):
    acc = _xwt(x_ref, w_ref)
    if act == "sigmoid":
        acc = jax.nn.sigmoid(acc)
    o_ref[...] = acc.astype(o_ref.dtype)


def _norm_mm_kernel(x_ref, g_ref, sc_ref, sh_ref, w_ref, h_ref, o_ref):
    @pl.when(pl.program_id(1) == 0)
    def _():
        x = x_ref[...]
        r = lax.rsqrt(jnp.mean(x * x, axis=-1, keepdims=True) + RMS_EPS)
        y = (x * r) * g_ref[...]
        h_ref[...] = (y * (1.0 + sc_ref[0]) + sh_ref[0]).astype(h_ref.dtype)

    o_ref[...] = _xwt(h_ref, w_ref).astype(o_ref.dtype)


def _norm_mm(x, g, sc, sh, w, *, tm, tn, seq, out_dtype, col0, n, name):
    t, d = x.shape
    assert col0 % tn == 0 and n % tn == 0
    cb0 = col0 // tn
    row = lambda p: (pl.BlockSpec((1, 1, d), lambda i, j: ((i * tm) // seq, 0, 0)) if p.shape[1] == 1
                     else pl.BlockSpec((1, tm, d), lambda i, j: (0, i, 0)))
    return pl.pallas_call(
        _norm_mm_kernel,
        grid=(t // tm, n // tn),
        in_specs=[pl.BlockSpec((tm, d), lambda i, j: (i, 0)),
                  pl.BlockSpec((1, d), lambda i, j: (0, 0)),
                  row(sc), row(sh),
                  pl.BlockSpec((tn, d), lambda i, j: (cb0 + j, 0))],
        out_specs=[pl.BlockSpec((tm, d), lambda i, j: (i, 0)),
                   pl.BlockSpec((tm, tn), lambda i, j: (i, j))],
        out_shape=[jax.ShapeDtypeStruct((t, d), BF16), jax.ShapeDtypeStruct((t, n), out_dtype)],
        compiler_params=_cparams(("arbitrary", "arbitrary")),
        name=name,
    )(x, g, sc, sh, w)


def _mm_wstat_kernel(x_ref, w_ref, o_ref, wbf_ref, *, act):
    @pl.when(pl.program_id(1) == 0)
    def _():
        wbf_ref[...] = w_ref[...].astype(BF16)

    acc = _xwt(x_ref, wbf_ref)
    if act == "sigmoid":
        acc = jax.nn.sigmoid(acc)
    o_ref[...] = acc.astype(o_ref.dtype)


def _wrows(tn, k, row0):
    return pl.BlockSpec((pl.Element(tn), pl.Element(k)), lambda j, i: (pl.multiple_of(row0 + j * tn, 8), 0))


def _mm_wstat(x, w, *, tm, tn, out_dtype, act=None, name="mm", col0, n):
    t, k = x.shape
    assert col0 % 8 == 0 and n % tn == 0
    return pl.pallas_call(
        functools.partial(_mm_wstat_kernel, act=act),
        grid=(n // tn, t // tm),
        in_specs=[pl.BlockSpec((tm, k), lambda j, i: (i, 0)),
                  _wrows(tn, k, col0)],
        out_specs=pl.BlockSpec((tm, tn), lambda j, i: (i, j)),
        out_shape=jax.ShapeDtypeStruct((t, n), out_dtype),
        scratch_shapes=[pltpu.VMEM((tn, k), BF16)],
        compiler_params=_cparams(("arbitrary", "arbitrary")),
        name=name,
    )(x, w)


def _mm(x, w, *, tm, tn, out_dtype, act=None, name="mm", col0, n):
    t, k = x.shape
    assert col0 % tn == 0 and n % tn == 0
    cb0 = col0 // tn
    return pl.pallas_call(
        functools.partial(_mm_kernel, act=act),
        grid=(t // tm, n // tn),
        in_specs=[pl.BlockSpec((tm, k), lambda i, j: (i, 0)),
                  pl.BlockSpec((tn, k), lambda i, j: (cb0 + j, 0))],
        out_specs=pl.BlockSpec((tm, tn), lambda i, j: (i, j)),
        out_shape=jax.ShapeDtypeStruct((t, n), out_dtype),
        compiler_params=_cparams(("arbitrary", "arbitrary")),
        name=name,
    )(x, w)


def _store_head_major(o_ref, acc, tm):
    flat = o_ref.reshape(tm * 8, HEAD_DIM)
    for cb in range(8):
        flat[pl.ds(cb, tm, stride=8), :] = acc[:, cb * HEAD_DIM:(cb + 1) * HEAD_DIM]


def _mm_kv_kernel(x_ref, w_ref, o6_ref, ob_ref, *, tm):
    acc = _xwt(x_ref, w_ref)
    _store_head_major(o6_ref, acc, tm)

    @pl.when(pl.program_id(1) == 1)
    def _():
        ob_ref[...] = acc.astype(BF16)


def _mm_kv(h, w, *, tm, col0):
    t, k = h.shape
    tn = 2 * KV_W
    cb0 = col0 // tn
    return pl.pallas_call(
        functools.partial(_mm_kv_kernel, tm=tm),
        grid=(t // tm, 2),
        in_specs=[pl.BlockSpec((tm, k), lambda i, j: (i, 0)),
                  pl.BlockSpec((tn, k), lambda i, j: (cb0 + j, 0))],
        out_specs=[pl.BlockSpec((tm, 8, HEAD_DIM), lambda i, j: (i, j, 0)),
                   pl.BlockSpec((tm, tn), lambda i, j: (i, 0))],
        out_shape=[jax.ShapeDtypeStruct((t, 16, HEAD_DIM), F32),
                   jax.ShapeDtypeStruct((t, tn), BF16)],
        compiler_params=_cparams(("arbitrary", "arbitrary")),
        name="mm_kv",
    )(h, w)


def _mm_win_kernel(x_ref, w_ref, wg_ref, ob_ref, o6_ref, og_ref, *, tm, seq):
    acc = _xwt(x_ref, w_ref)
    ob_ref[...] = acc.astype(BF16)
    gl = _xwt(x_ref, wg_ref)
    for hd in range(N_KV):
        og_ref[hd] = jnp.concatenate(
            [gl[:, br * N_HEADS + hd * GROUP:br * N_HEADS + (hd + 1) * GROUP] for br in range(3)]
            + [jnp.zeros((tm, GROUP), F32)], axis=1)

    @pl.when(((pl.program_id(0) + 1) * tm) % seq == 0)
    def _():
        _store_head_major(o6_ref, acc, tm)


def _mm_win(h, w, *, tm, seq, col0, gcol0):
    t, k = h.shape
    tn = 2 * KV_W
    cb0 = col0 // tn
    gb0 = gcol0 // HEAD_DIM
    return pl.pallas_call(
        functools.partial(_mm_win_kernel, tm=tm, seq=seq),
        grid=(t // tm,),
        in_specs=[pl.BlockSpec((tm, k), lambda i: (i, 0)),
                  pl.BlockSpec((tn, k), lambda i: (cb0, 0)),
                  pl.BlockSpec((HEAD_DIM, k), lambda i: (gb0, 0))],
        out_specs=[pl.BlockSpec((tm, tn), lambda i: (i, 0)),
                   pl.BlockSpec((tm, 8, HEAD_DIM), lambda i: ((i * tm) // seq, 0, 0)),
                   pl.BlockSpec((None, N_KV, tm, 4 * GROUP), lambda i: ((i * tm) // seq, 0, (i * tm % seq) // tm, 0))],
        out_shape=[jax.ShapeDtypeStruct((t, tn), BF16),
                   jax.ShapeDtypeStruct((t // seq * tm, 8, HEAD_DIM), F32),
                   jax.ShapeDtypeStruct((t // seq, N_KV, seq, 4 * GROUP), F32)],
        compiler_params=_cparams(("arbitrary",)),
        name="mm_win",
    )(h, w, w)


def _conv_taps(cu, prev1, prev2, bgate, cw_ref, cb_ref):
    y = cb_ref[...] + prev2 * cw_ref[0:1, :]
    y = y + prev1 * cw_ref[1:2, :]
    y = y + cu * cw_ref[2:3, :]
    return bgate * y


def _mm_conv_kernel(x_ref, wu_ref, wb_ref, wc_ref, cw_ref, cb_ref, ca_ref, cb2_ref, z_ref, st_ref, oa_ref, ob_ref,
                    w3_ref, carry_ref, *, tm, tc, seq):
    i = pl.program_id(1)
    oa_ref[...] = ca_ref[...].astype(BF16)
    ob_ref[...] = cb2_ref[...].astype(BF16)

    @pl.when(i == 0)
    def _():
        w3_ref[0:tc] = wu_ref[...].astype(BF16)
        w3_ref[tc:2 * tc] = wb_ref[...].astype(BF16)
        w3_ref[2 * tc:3 * tc] = wc_ref[...].astype(BF16)

    acc = _xwt(x_ref, w3_ref)
    u = acc[:, 0:tc]
    bgate = acc[:, tc:2 * tc]
    cu = acc[:, 2 * tc:3 * tc] * u

    @pl.when((i * tm) % seq == 0)
    def _():
        carry_ref[...] = jnp.zeros((8, tc), F32)

    car = carry_ref[...]
    p0 = car[0:1, :]
    p1 = car[1:2, :]
    rows = lax.broadcasted_iota(jnp.int32, (tm, tc), 0)
    r1 = pltpu.roll(cu, 1, 0)
    r2 = pltpu.roll(cu, 2, 0)
    prev1 = jnp.where(rows == 0, p1, r1)
    prev2 = jnp.where(rows == 0, p0, jnp.where(rows == 1, p1, r2))
    z_ref[...] = _conv_taps(cu, prev1, prev2, bgate, cw_ref, cb_ref).astype(z_ref.dtype)
    tail = cu[tm - 8:tm, :]
    carry_ref[...] = jnp.concatenate([tail[6:8, :], tail[0:6, :]], axis=0)
    st_ref[0] = tail[6:8, :]


def _mm_conv(h, w, conv_w, conv_b, cast_a, cast_b, *, tm, tc, seq, col0):
    t, k = h.shape
    d = conv_w.shape[1]
    nj = d // tc
    ni = t // tm
    steps = nj * ni
    ra, rb_ = cast_a.shape[0] // steps, cast_b.shape[0] // steps
    assert ra % 16 == 0 and rb_ % 16 == 0 and ra * steps == cast_a.shape[0] and rb_ * steps == cast_b.shape[0]
    slab = lambda r, m: pl.BlockSpec((r, m.shape[1]), lambda j, i: (j * ni + i, 0))
    return pl.pallas_call(
        functools.partial(_mm_conv_kernel, tm=tm, tc=tc, seq=seq),
        grid=(nj, ni),
        in_specs=[pl.BlockSpec((tm, k), lambda j, i: (i, 0)),
                  _wrows(tc, k, col0), _wrows(tc, k, col0 + d), _wrows(tc, k, col0 + 2 * d),
                  pl.BlockSpec((CONV_W, tc), lambda j, i: (0, j)),
                  pl.BlockSpec((1, tc), lambda j, i: (0, j)),
                  slab(ra, cast_a), slab(rb_, cast_b)],
        out_specs=[pl.BlockSpec((tm, tc), lambda j, i: (i, j)),
                   pl.BlockSpec((1, CONV_W - 1, tc), lambda j, i: (i, 0, j)),
                   slab(ra, cast_a), slab(rb_, cast_b)],
        out_shape=[jax.ShapeDtypeStruct((t, d), BF16),
                   jax.ShapeDtypeStruct((t // tm, CONV_W - 1, d), F32),
                   jax.ShapeDtypeStruct(cast_a.shape, BF16),
                   jax.ShapeDtypeStruct(cast_b.shape, BF16)],
        scratch_shapes=[pltpu.VMEM((3 * tc, k), BF16), pltpu.VMEM((8, tc), F32)],
        compiler_params=_cparams(("arbitrary", "arbitrary")),
        name="mm_conv",
    )(h, w, w, w, conv_w, conv_b, cast_a, cast_b)


def _sample_conv_kernel(u_ref, b_ref, c_ref, st0_ref, st1_ref, cw_ref, cb_ref, z_ref, cu_ref, *, tc, seq):
    t = u_ref.shape[0]
    bgate = b_ref[...]
    cu = c_ref[...] * u_ref[...]
    s = lax.rem(lax.broadcasted_iota(jnp.int32, (t, tc), 0), jnp.int32(seq))
    r1 = pltpu.roll(cu, 1, 0)
    r2 = pltpu.roll(cu, 2, 0)
    prev1 = jnp.where(s == 0, st1_ref[...], r1)
    prev2 = jnp.where(s == 0, st0_ref[...], jnp.where(s == 1, st1_ref[...], r2))
    z_ref[...] = _conv_taps(cu, prev1, prev2, bgate, cw_ref, cb_ref)
    cu_ref[...] = cu


def _sample_conv(ubc, st0, st1, conv_w, conv_b, *, tc, seq):
    t = ubc.shape[0]
    d = ubc.shape[1] // 3
    nj = d // tc
    return pl.pallas_call(
        functools.partial(_sample_conv_kernel, tc=tc, seq=seq),
        grid=(nj,),
        in_specs=[pl.BlockSpec((t, tc), lambda j: (0, j)),
                  pl.BlockSpec((t, tc), lambda j: (0, nj + j)),
                  pl.BlockSpec((t, tc), lambda j: (0, 2 * nj + j)),
                  pl.BlockSpec((t, tc), lambda j: (0, j)),
                  pl.BlockSpec((t, tc), lambda j: (0, j)),
                  pl.BlockSpec((CONV_W, tc), lambda j: (0, j)),
                  pl.BlockSpec((1, tc), lambda j: (0, j))],
        out_specs=[pl.BlockSpec((t, tc), lambda j: (0, j)),
                   pl.BlockSpec((t, tc), lambda j: (0, j))],
        out_shape=[jax.ShapeDtypeStruct((t, d), F32), jax.ShapeDtypeStruct((t, d), F32)],
        compiler_params=_cparams(("arbitrary",)),
        name="sample_conv",
    )(ubc, ubc, ubc, st0, st1, conv_w, conv_b)


def _mix_kernel(o_ref, z_ref, wa_ref, wc_ref, ga_ref, gc_ref, m_ref, wab_ref, wcb_ref):
    @pl.when(pl.program_id(1) == 0)
    def _():
        wab_ref[...] = wa_ref[...].astype(BF16)
        wcb_ref[...] = wc_ref[...].astype(BF16)

    a = jnp.dot(o_ref[...].astype(BF16), wab_ref[...], preferred_element_type=F32)
    c = jnp.dot(z_ref[...].astype(BF16), wcb_ref[...], preferred_element_type=F32)
    m_ref[...] = (ga_ref[...].astype(F32) * a + gc_ref[...].astype(F32) * c).astype(m_ref.dtype)


def _mix(o, z, wa, wc, gates, *, tm, tn):
    t, k = o.shape
    n = wa.shape[1]
    nj = n // tn
    return pl.pallas_call(
        _mix_kernel,
        grid=(nj, t // tm),
        in_specs=[pl.BlockSpec((tm, k), lambda j, i: (i, 0)),
                  pl.BlockSpec((tm, k), lambda j, i: (i, 0)),
                  pl.BlockSpec((k, tn), lambda j, i: (0, j)),
                  pl.BlockSpec((k, tn), lambda j, i: (0, j)),
                  pl.BlockSpec((tm, tn), lambda j, i: (i, j)),
                  pl.BlockSpec((tm, tn), lambda j, i: (i, j + nj))],
        out_specs=pl.BlockSpec((tm, tn), lambda j, i: (i, j)),
        out_shape=jax.ShapeDtypeStruct((t, n), BF16),
        scratch_shapes=[pltpu.VMEM((k, tn), BF16), pltpu.VMEM((k, tn), BF16)],
        compiler_params=_cparams(("arbitrary", "arbitrary")),
        name="mix",
    )(o, z, wa, wc, gates, gates)


def _resid_norm_kernel(a_ref, w_ref, x_ref, g_ref, ng_ref, sc_ref, sh_ref, x1_ref, h2_ref, wbf_ref):
    @pl.when(pl.program_id(0) == 0)
    def _():
        wbf_ref[...] = w_ref[...].astype(BF16)

    acc = jnp.dot(a_ref[...], wbf_ref[...], preferred_element_type=F32)
    x1 = x_ref[...] + g_ref[0] * acc
    x1_ref[...] = x1
    r = lax.rsqrt(jnp.mean(x1 * x1, axis=-1, keepdims=True) + RMS_EPS)
    y = (x1 * r) * ng_ref[...]
    h2_ref[...] = (y * (1.0 + sc_ref[0]) + sh_ref[0]).astype(h2_ref.dtype)


def _mm_resid_norm(a, w, x, gate, ng, sc, sh, *, tm, seq):
    t, k = a.shape
    n = w.shape[1]
    row = lambda p: _rowspec(p, tm, n, seq, col=False)
    return pl.pallas_call(
        _resid_norm_kernel,
        grid=(t // tm,),
        in_specs=[pl.BlockSpec((tm, k), lambda i: (i, 0)),
                  pl.BlockSpec((k, n), lambda i: (0, 0), pipeline_mode=pl.Buffered(1)),
                  pl.BlockSpec((tm, n), lambda i: (i, 0)),
                  row(gate),
                  pl.BlockSpec((1, n), lambda i: (0, 0)),
                  row(sc), row(sh)],
        out_specs=[pl.BlockSpec((tm, n), lambda i: (i, 0)), pl.BlockSpec((tm, n), lambda i: (i, 0))],
        out_shape=[jax.ShapeDtypeStruct((t, n), F32), jax.ShapeDtypeStruct((t, n), BF16)],
        scratch_shapes=[pltpu.VMEM((k, n), BF16)],
        compiler_params=_cparams(("arbitrary",)),
        name="mm_resid_norm",
    )(a, w, x, gate, ng, sc, sh)


def _mlp_kernel(h_ref, w1_ref, w2_ref, x_ref, g_ref, nf_ref, o_ref, acc_ref):
    k = pl.program_id(1)

    @pl.when(k == 0)
    def _():
        acc_ref[...] = jnp.zeros_like(acc_ref)

    a = jnp.dot(h_ref[...], w1_ref[...], preferred_element_type=F32)
    a = jnp.square(jnp.maximum(a, 0.0)).astype(BF16)
    acc_ref[...] += jnp.dot(a, w2_ref[...], preferred_element_type=F32)

    @pl.when(k == pl.num_programs(1) - 1)
    def _():
        x2 = x_ref[...] + g_ref[0] * acc_ref[...]
        r = lax.rsqrt(jnp.mean(x2 * x2, axis=-1, keepdims=True) + RMS_EPS)
        o_ref[...] = (x2 * r) * nf_ref[...]


def _mlp_final(h2, w1, w2, x1, gate, normf, *, tm, tf, seq):
    t, d = h2.shape
    f = w1.shape[1]
    return pl.pallas_call(
        _mlp_kernel,
        grid=(t // tm, f // tf),
        in_specs=[pl.BlockSpec((tm, d), lambda i, k: (i, 0)),
                  pl.BlockSpec((d, tf), lambda i, k: (0, k)),
                  pl.BlockSpec((tf, d), lambda i, k: (k, 0)),
                  pl.BlockSpec((tm, d), lambda i, k: (i, 0)),
                  _rowspec_k(gate, tm, d, seq),
                  pl.BlockSpec((1, d), lambda i, k: (0, 0))],
        out_specs=pl.BlockSpec((tm, d), lambda i, k: (i, 0)),
        out_shape=jax.ShapeDtypeStruct((t, d), F32),
        scratch_shapes=[pltpu.VMEM((tm, d), F32)],
        compiler_params=_cparams(("arbitrary", "arbitrary")),
        name="mlp",
    )(h2, w1, w2, x1, gate, normf)


def _rowspec_k(p, tm, d, seq):
    if p.shape[1] == 1:
        return pl.BlockSpec((1, 1, d), lambda i, k: ((i * tm) // seq, 0, 0))
    return pl.BlockSpec((1, tm, d), lambda i, k: (0, i, 0))


def _cmp_part_kernel(pt_ref, *refs):
    pages = refs[:PAGES_PER_STEP]
    w_ref = refs[PAGES_PER_STEP]
    o_ref = refs[PAGES_PER_STEP + 1]
    stage = refs[PAGES_PER_STEP + 2]
    nchunk = CHUNK_ROWS // CMP_STRIDE
    cpp = PAGE // CMP_STRIDE
    nslab = 2 * N_KV
    tiles = [[pg[pl.ds(r, cpp, stride=CMP_STRIDE)] for pg in pages] for r in range(CMP_STRIDE)]
    for kind in range(2):
        cols = [jnp.concatenate([x[:, kind * N_KV:(kind + 1) * N_KV, :].reshape(cpp * N_KV, HEAD_DIM)
                                 for x in tiles[r]], axis=0).astype(BF16) for r in range(CMP_STRIDE)]
        lhs = jnp.concatenate(cols, axis=1)
        acc = jnp.dot(lhs, w_ref[:, kind * 2 * HEAD_DIM:(kind + 1) * 2 * HEAD_DIM],
                      preferred_element_type=F32)
        stage[0] = acc[:, :HEAD_DIM]
        stage[1] = acc[:, HEAD_DIM:]
        for h in range(N_KV):
            o_ref[kind, h] = jnp.concatenate(
                [stage[0, pl.ds(h, nchunk, stride=N_KV), :], stage[1, pl.ds(h, nchunk, stride=N_KV), :]], axis=1)


def _page_spec(n, kind_pair):
    return pl.BlockSpec((None, PAGE, 8, HEAD_DIM),
                        lambda b, p, pt, n=n: (pt[b, p * PAGES_PER_STEP + n], 0, kind_pair, 0))


def _cmp_part(pages4, page_table, w1pairs):
    nb, npg = page_table.shape
    steps = npg // PAGES_PER_STEP
    nchunk = PAGES_PER_STEP * (PAGE // CMP_STRIDE)
    grid_spec = pltpu.PrefetchScalarGridSpec(
        num_scalar_prefetch=1,
        grid=(nb, steps),
        in_specs=[_page_spec(n, 0) for n in range(PAGES_PER_STEP)]
        + [pl.BlockSpec((CMP_STRIDE * HEAD_DIM, 4 * HEAD_DIM), lambda b, p, pt: (0, 0))],
        out_specs=pl.BlockSpec((None, 2, N_KV, nchunk, 2 * HEAD_DIM), lambda b, p, pt: (b, 0, 0, p, 0)),
        scratch_shapes=[pltpu.VMEM((2, nchunk * N_KV, HEAD_DIM), F32)],
    )
    return pl.pallas_call(
        _cmp_part_kernel,
        grid_spec=grid_spec,
        out_shape=jax.ShapeDtypeStruct((nb, 2, N_KV, steps * nchunk, 2 * HEAD_DIM), F32),
        compiler_params=_cparams(("arbitrary", "arbitrary")),
        name="cmp_part",
    )(page_table, *([pages4] * PAGES_PER_STEP), w1pairs)


def _gelu_tanh(x):
    c = np.sqrt(2.0 / np.pi).astype(np.float32)
    return 0.5 * x * (1.0 + jnp.tanh(c * (x + 0.044715 * (x * x * x))))


def _cmp_finish_kernel(part_ref, pe_ref, w1_ref, w2_ref, o_ref, bias_ref):
    @pl.when(pl.program_id(1) == 0)
    def _():
        bias_ref[...] = jnp.dot(pe_ref[...], w1_ref[...], preferred_element_type=F32,
                                precision=lax.Precision.HIGHEST)

    w2 = w2_ref[...].astype(BF16)
    for hd in range(N_KV):
        part = part_ref[hd]
        n = part.shape[0]
        h = part[:, 0:HEAD_DIM] + pltpu.roll(part[:, HEAD_DIM:], n - 1, 0)
        h = h + bias_ref[0:1, :]
        o_ref[hd] = jnp.dot(_gelu_tanh(h).astype(BF16), w2, preferred_element_type=F32).astype(o_ref.dtype)


def _cmp_finish(part, pe8, w1, w2):
    nb, _, _, nchunk, _ = part.shape
    return pl.pallas_call(
        _cmp_finish_kernel,
        grid=(2, nb),
        in_specs=[pl.BlockSpec((None, None, N_KV, nchunk, 2 * HEAD_DIM), lambda k, b: (b, k, 0, 0, 0)),
                  pl.BlockSpec((None, 8, CMP_LEN * HEAD_DIM), lambda k, b: (k, 0, 0)),
                  pl.BlockSpec((None, CMP_LEN * HEAD_DIM, HEAD_DIM), lambda k, b: (k, 0, 0)),
                  pl.BlockSpec((None, HEAD_DIM, HEAD_DIM), lambda k, b: (k, 0, 0))],
        out_specs=pl.BlockSpec((None, None, N_KV, nchunk, HEAD_DIM), lambda k, b: (b, k, 0, 0, 0)),
        out_shape=jax.ShapeDtypeStruct((nb, 2, N_KV, nchunk, HEAD_DIM), BF16),
        scratch_shapes=[pltpu.VMEM((8, HEAD_DIM), F32)],
        compiler_params=_cparams(("arbitrary", "arbitrary")),
        name="cmp_finish",
    )(part, pe8, w1, w2)


def _select_topk(score, valid, forced, blk, axis):
    sc = jnp.where(valid, jnp.where(forced, jnp.inf, score), -jnp.inf)
    sel = jnp.zeros(score.shape, F32)
    big = jnp.int32(1 << 20)
    for _ in range(SLC_TOP):
        mx = jnp.max(sc, axis=axis, keepdims=True)
        idx = jnp.min(jnp.where(sc == mx, blk, big), axis=axis, keepdims=True)
        hit = blk == idx
        sel = jnp.where(hit & (mx > -jnp.inf), 1.0, sel)
        sc = jnp.where(hit, -jnp.inf, sc)
    return sel


def _overlap_matrix(nc_pad, n_blocks_pad, nc, n_blocks):
    cs = np.arange(nc_pad) * CMP_STRIDE
    sb = np.arange(n_blocks_pad) * SLC_LEN
    ov = np.clip(np.minimum(cs[:, None] + CMP_LEN, sb[None, :] + SLC_LEN)
                 - np.maximum(cs[:, None], sb[None, :]), 0, None)
    m = (ov / CMP_STRIDE).astype(np.float32)
    m[nc:, :] = 0.0
    m[:, n_blocks:] = 0.0
    return m


def _attn_prompt_kernel(q_ref, kcb, vcb, ksb, vsb, kwb, vwb, g_ref, mt_ref, e_ref, o_ref,
                        m_scr, l_scr, acc_scr, bias_scr, gate_scr, *, tq, seq, ck):
    qi = pl.program_id(2)
    nc = seq // CMP_STRIDE - 1
    nsb = seq // SLC_LEN
    rb = min(tq, 128)

    q = q_ref[...]
    qq = jnp.concatenate([q[:, g * HEAD_DIM:(g + 1) * HEAD_DIM] for g in range(GROUP)], axis=0)
    rq = GROUP * tq
    q0 = qi * tq

    gt = jax.nn.sigmoid(g_ref[...])
    ngc = 4 * GROUP
    hi = gt.astype(BF16)
    r1 = gt - hi.astype(F32)
    mid = r1.astype(BF16)
    lo = (r1 - mid.astype(F32)).astype(BF16)
    g3 = jnp.concatenate([hi, mid, lo, jnp.zeros((tq, HEAD_DIM - 3 * ngc), BF16)], axis=1)
    srow = lax.broadcasted_iota(jnp.int32, (HEAD_DIM, 3 * GROUP * HEAD_DIM), 0)
    scol = lax.broadcasted_iota(jnp.int32, (HEAD_DIM, 3 * GROUP * HEAD_DIM), 1)
    spread = ((srow < 3 * ngc) & ((srow & (ngc - 1)) == lax.div(scol, jnp.int32(HEAD_DIM)))).astype(BF16)
    gl = jnp.dot(g3, spread, preferred_element_type=F32)
    for c in range(3 * GROUP):
        gate_scr[c] = gl[:, c * HEAD_DIM:(c + 1) * HEAD_DIM]

    ncp = kcb.shape[0]
    s = lax.dot_general(qq, kcb[...], _NT, preferred_element_type=F32) * SCALE
    qpos = q0 + (lax.broadcasted_iota(jnp.int32, (rq, ncp), 0) & (tq - 1))
    col = lax.broadcasted_iota(jnp.int32, (rq, ncp), 1)
    vis = (col * CMP_STRIDE + (CMP_LEN - 1) <= qpos) & (col < nc)
    s = jnp.where(vis, s, NEG)
    mx = jnp.max(s, axis=1, keepdims=True)
    e = jnp.where(vis, jnp.exp(s - mx), 0.0)
    den = jnp.sum(e, axis=1, keepdims=True)
    p = e / jnp.where(den > 0.0, den, 1.0)
    o_cmp = jnp.dot(p.astype(BF16), vcb[...], preferred_element_type=F32)
    pg = p[0:tq]
    for g in range(1, GROUP):
        pg = pg + p[g * tq:(g + 1) * tq]

    sct = lax.dot_general(mt_ref[...], pg, _NT, preferred_element_type=F32,
                          precision=lax.Precision.HIGHEST)
    blk = lax.broadcasted_iota(jnp.int32, (nsb, tq), 0)
    cur = lax.div(q0 + lax.broadcasted_iota(jnp.int32, (nsb, tq), 1), jnp.int32(SLC_LEN))
    valid = blk <= cur
    forced = (blk == 0) | (blk > cur - N_LOCAL)
    need_topk = (q0 + tq - 1) // SLC_LEN + 1 > SLC_TOP
    sel_t = lax.cond(need_topk,
                     lambda: _select_topk(sct, valid, forced, blk, 0),
                     lambda: valid.astype(F32)).astype(BF16)

    def sweep(k_ref, v_ref, k_start, n_chunks, bias_fn):
        nl = ck // HEAD_DIM

        def scores(c):
            k0 = pl.multiple_of(k_start + c * ck, 128)
            return lax.dot_general(qq, k_ref[pl.ds(k0, ck), :], _NT, preferred_element_type=F32)

        def chunk_inputs(c):
            k0 = pl.multiple_of(k_start + c * ck, 128)
            return scores(c), bias_fn(c, k0)

        c_exp = np.float32(SCALE * np.log2(np.e))

        def reduce_chunk(c, sb, first):
            sc_, bias = sb
            k0 = pl.multiple_of(k_start + c * ck, 128)
            vch = jnp.concatenate([v_ref[pl.ds(k0, ck), :], jnp.ones((ck, HEAD_DIM), BF16)], axis=1)
            alphas, ps = [], []
            nblk = rq // rb
            for r in range(nblk):
                rows = slice(r * rb, (r + 1) * rb)
                b0 = (r * rb) % tq
                sg = [sc_[rows, j * HEAD_DIM:(j + 1) * HEAD_DIM]
                      + bias[b0:b0 + rb, j * HEAD_DIM:(j + 1) * HEAD_DIM] for j in range(nl)]
                mx = functools.reduce(jnp.maximum, sg)
                row_max = jnp.max(mx, axis=1, keepdims=True)
                if first:
                    m_new = jnp.broadcast_to(row_max, (rb, HEAD_DIM))
                else:
                    m_old = m_scr[rows]
                    m_new = jnp.maximum(m_old, row_max)
                    alphas.append(jnp.exp2((m_old - m_new) * c_exp))
                pj = [jnp.exp2((x - m_new) * c_exp) for x in sg]
                m_scr[rows] = m_new
                ps.append(jnp.concatenate([x.astype(BF16) for x in pj], axis=1))
                if (r + 1) % (nblk // 2) == 0:
                    half = slice((r + 1 - nblk // 2) * rb, (r + 1) * rb)
                    pv = jnp.dot(jnp.concatenate(ps, axis=0), vch, preferred_element_type=F32)
                    if first:
                        acc_scr[half] = pv[:, :HEAD_DIM]
                        l_scr[half] = pv[:, HEAD_DIM:]
                    else:
                        al = jnp.concatenate(alphas, axis=0)
                        acc_scr[half] = al * acc_scr[half] + pv[:, :HEAD_DIM]
                        l_scr[half] = al * l_scr[half] + pv[:, HEAD_DIM:]
                    alphas, ps = [], []

        def body(c, carry):
            reduce_chunk(c, chunk_inputs(c), False)
            return carry

        reduce_chunk(0, chunk_inputs(0), True)
        lax.fori_loop(1, n_chunks, body, 0)
        return acc_scr[...] / l_scr[...]

    qp = q0 + lax.broadcasted_iota(jnp.int32, (tq, ck), 0)
    kcol = lax.broadcasted_iota(jnp.int32, (tq, ck), 1)

    eye = (lax.broadcasted_iota(jnp.int32, (nsb, HEAD_DIM), 0)
           == lax.broadcasted_iota(jnp.int32, (nsb, HEAD_DIM), 1)).astype(BF16)
    sel_q = lax.dot_general(sel_t, eye, _TN, preferred_element_type=F32).astype(BF16)

    n_slc = (q0 + tq + ck - 1) // ck

    def put_mask(c):
        selq = jnp.dot(sel_q, e_ref[c], preferred_element_type=F32)
        bias_scr[c] = jnp.where((selq > 0.5) & (c * ck + kcol <= qp), 0.0, NEG)

    def mask_body(i, carry):
        put_mask(2 * i)
        put_mask(jnp.minimum(2 * i + 1, seq // ck - 1))
        return carry

    lax.fori_loop(0, (n_slc + 1) // 2, mask_body, 0)
    o_slc = sweep(ksb, vsb, 0, n_slc, lambda c, k0: bias_scr[c])

    span = -(-(WINDOW + tq) // ck) * ck

    def win_bias(c, k0):
        dlt = qp - (k0 + kcol)
        return jnp.where((dlt >= 0) & (dlt < WINDOW), 0.0, NEG)

    o_win = sweep(kwb, vwb, jnp.maximum(q0 + tq - span, 0), span // ck, win_bias)

    for g in range(GROUP):
        r0, r1 = g * tq, (g + 1) * tq
        og = gate_scr[g] * o_cmp[r0:r1] + gate_scr[GROUP + g] * o_slc[r0:r1]
        og = og + gate_scr[2 * GROUP + g] * o_win[r0:r1]
        o_ref[:, g * HEAD_DIM:(g + 1) * HEAD_DIM] = og.astype(o_ref.dtype)


def _attn_prompt(q, kcv, kv, win, g_re, *, nb, seq, tq=512, ck=512):
    assert tq & (tq - 1) == 0 and seq % ck == 0 and WINDOW % tq == 0
    t = q.shape[0]
    ncp = seq // CMP_STRIDE
    nsb = seq // SLC_LEN
    nq = seq // tq
    mt = jnp.asarray(_overlap_matrix(ncp, nsb, ncp - 1, nsb).T)
    kk = np.arange(seq)
    assert nsb <= HEAD_DIM
    e3 = (kk[None, :] // SLC_LEN == np.arange(HEAD_DIM)[:, None]).astype(np.float32)
    e3 = jnp.asarray(e3.reshape(HEAD_DIM, seq // ck, ck).transpose(1, 0, 2), dtype=BF16)
    slab = lambda cb: pl.BlockSpec((seq, HEAD_DIM), lambda b, h, i, cb=cb: (b, cb + h))
    return pl.pallas_call(
        functools.partial(_attn_prompt_kernel, tq=tq, seq=seq, ck=ck),
        grid=(nb, N_KV, nq),
        in_specs=[pl.BlockSpec((tq, GROUP * HEAD_DIM), lambda b, h, i: (b * nq + i, h)),
                  pl.BlockSpec((None, None, None, ncp, HEAD_DIM), lambda b, h, i: (b, 0, h, 0, 0)),
                  pl.BlockSpec((None, None, None, ncp, HEAD_DIM), lambda b, h, i: (b, 1, h, 0, 0)),
                  slab(0), slab(N_KV), slab(0), slab(N_KV),
                  pl.BlockSpec((None, None, tq, 4 * GROUP), lambda b, h, i: (b, h, i, 0)),
                  pl.BlockSpec((nsb, ncp), lambda b, h, i: (0, 0)),
                  pl.BlockSpec((seq // ck, HEAD_DIM, ck), lambda b, h, i: (0, 0, 0))],
        out_specs=pl.BlockSpec((tq, GROUP * HEAD_DIM), lambda b, h, i: (b * nq + i, h)),
        out_shape=jax.ShapeDtypeStruct((t, N_HEADS * HEAD_DIM), BF16),
        scratch_shapes=[pltpu.VMEM((GROUP * tq, HEAD_DIM), F32)] * 3
        + [pltpu.VMEM((seq // ck, tq, ck), F32), pltpu.VMEM((3 * GROUP, tq, HEAD_DIM), F32)],
        compiler_params=_cparams(("arbitrary", "arbitrary", "arbitrary")),
        name="attn_prompt",
    )(q, kcv, kcv, kv, kv, win, win, g_re, mt, e3)


def _attn_sample_kernel(pt_ref, cache_ref, q_ref, kcv_ref, kvn_ref, wn_ref, st_ref, g_ref, ms_ref, e_ref, o_ref,
                        wso_ref, m_scr, l_scr, acc_scr, sel_scr, ocmp_scr, kv_buf, sem, *, past, ds):
    b = pl.program_id(0)
    p = pl.program_id(1)
    n_steps = pl.num_programs(1)

    def page_copies(bb, pp, slot, for_wait=False):
        cps = []
        for n in range(PAGES_PER_STEP):
            page = 0 if for_wait else pt_ref[bb, pp * PAGES_PER_STEP + n]
            for cb in range(2 * N_KV):
                cps.append(pltpu.make_async_copy(cache_ref.at[page, :, 2 * N_KV + cb, :],
                                                 kv_buf.at[slot, cb, pl.ds(n * PAGE, PAGE), :],
                                                 sem.at[slot]))
        return cps

    t = b * n_steps + p
    slot = lax.rem(t, 2)

    @pl.when(t == 0)
    def _():
        for n, cp in enumerate(page_copies(b, p, slot)):
            cp.start(priority=n % 2)

    @pl.when(t + 1 < pl.num_programs(0) * n_steps)
    def _():
        wrap = p + 1 == n_steps
        for n, cp in enumerate(page_copies(jnp.where(wrap, b + 1, b), jnp.where(wrap, 0, p + 1), 1 - slot)):
            cp.start(priority=n % 2)
    rq = GROUP * ds
    ncp = kcv_ref.shape[2]
    nc = ncp - 1
    n_sel_chunks = sel_scr.shape[1]
    lanes = ms_ref.shape[1]

    def q_rows(h):
        c0 = h * GROUP * HEAD_DIM
        return jnp.concatenate(
            [q_ref[:, c0 + g * HEAD_DIM:c0 + (g + 1) * HEAD_DIM] for g in range(GROUP)], axis=0).astype(BF16)

    def online_update(h, sc_, vals):
        m_i = m_scr[h]
        m_new = jnp.maximum(m_i, jnp.max(sc_, axis=1, keepdims=True))
        alpha = jnp.exp(m_i - m_new)
        pp = jnp.exp(sc_ - m_new)
        l_scr[h] = alpha * l_scr[h] + jnp.sum(pp, axis=1, keepdims=True)
        acc_scr[h] = alpha * acc_scr[h] + jnp.dot(pp.astype(BF16), vals, preferred_element_type=F32)
        m_scr[h] = m_new

    @pl.when(p == 0)
    def _():
        pgs = []
        for h in range(N_KV):
            qq = q_rows(h)
            kc = kcv_ref[0, h]
            vc = kcv_ref[1, h]
            s = lax.dot_general(qq, kc, _NT, preferred_element_type=F32) * SCALE
            qpos = past + (lax.broadcasted_iota(jnp.int32, (rq, ncp), 0) & (ds - 1))
            col = lax.broadcasted_iota(jnp.int32, (rq, ncp), 1)
            vis = (col * CMP_STRIDE + (CMP_LEN - 1) <= qpos) & (col < nc)
            s = jnp.where(vis, s, NEG)
            mx = jnp.max(s, axis=1, keepdims=True)
            e = jnp.where(vis, jnp.exp(s - mx), 0.0)
            den = jnp.sum(e, axis=1, keepdims=True)
            pr = e / jnp.where(den > 0.0, den, 1.0)
            ocmp_scr[h] = jnp.dot(pr.astype(BF16), vc, preferred_element_type=F32)
            pg = pr[0:ds]
            for g in range(1, GROUP):
                pg = pg + pr[g * ds:(g + 1) * ds]
            pgs.append(pg)
            m_scr[h] = jnp.full((rq, 1), NEG, F32)
            l_scr[h] = jnp.zeros((rq, 1), F32)
            acc_scr[h] = jnp.zeros((rq, HEAD_DIM), F32)
        nr = N_KV * ds
        pg_all = jnp.concatenate(pgs, axis=0)
        hi = pg_all.astype(BF16)
        r1 = pg_all - hi.astype(F32)
        mid = r1.astype(BF16)
        lo = (r1 - mid.astype(F32)).astype(BF16)
        sc3 = jnp.dot(jnp.concatenate([hi, mid, lo], axis=0), ms_ref[...], preferred_element_type=F32)
        score = (sc3[0:nr] + sc3[nr:2 * nr]) + sc3[2 * nr:3 * nr]
        blk = lax.broadcasted_iota(jnp.int32, (nr, lanes), 1)
        qrow = lax.broadcasted_iota(jnp.int32, (nr, lanes), 0) & (ds - 1)
        cur = lax.div(past + qrow, jnp.int32(SLC_LEN))
        valid = blk <= cur
        forced = (blk == 0) | (blk > cur - N_LOCAL)
        sel = _select_topk(score, valid, forced, blk, 1)
        lane_pad = jnp.zeros((ds, HEAD_DIM - BLK_PER_CHUNK), F32)
        for h in range(N_KV):
            for c in range(n_sel_chunks):
                sel_scr[h, c] = jnp.concatenate(
                    [sel[h * ds:(h + 1) * ds, c * BLK_PER_CHUNK:(c + 1) * BLK_PER_CHUNK], lane_pad], axis=1)

    for cp in page_copies(b, p, slot, for_wait=True):
        cp.wait()

    k0 = p * CHUNK_ROWS
    for h in range(N_KV):
        qq = q_rows(h)
        kch = kv_buf[slot, h].astype(BF16)
        vch = kv_buf[slot, N_KV + h].astype(BF16)
        sc_ = lax.dot_general(qq, kch, _NT, preferred_element_type=F32) * SCALE
        selq = jnp.dot(sel_scr[h, p].astype(BF16), e_ref[...], preferred_element_type=F32)
        kpos = k0 + lax.broadcasted_iota(jnp.int32, (ds, CHUNK_ROWS), 1)
        qp = past + lax.broadcasted_iota(jnp.int32, (ds, CHUNK_ROWS), 0)
        okf = jnp.where((selq > 0.5) & (kpos <= qp), 1.0, 0.0)
        ok = jnp.concatenate([okf] * GROUP, axis=0) > 0.5
        online_update(h, jnp.where(ok, sc_, NEG), vch)

    @pl.when(p == n_steps - 1)
    def _():
        gt = jax.nn.sigmoid(g_ref[...])
        zpad = jnp.zeros((HEAD_DIM - ds, HEAD_DIM), F32)
        for h in range(N_KV):
            qq = q_rows(h)
            c_k = 2 * KV_W + h * HEAD_DIM
            c_v = 3 * KV_W + h * HEAD_DIM
            kn = jnp.concatenate([kvn_ref[:, c_k:c_k + HEAD_DIM], zpad], axis=0).astype(BF16)
            vn = jnp.concatenate([kvn_ref[:, c_v:c_v + HEAD_DIM], zpad], axis=0).astype(BF16)
            sn = lax.dot_general(qq, kn, _NT, preferred_element_type=F32) * SCALE
            last_sel = sel_scr[h, n_sel_chunks - 1]
            nb_last = (past // SLC_LEN) % BLK_PER_CHUNK
            seln = jnp.sum(jnp.where(lax.broadcasted_iota(jnp.int32, (ds, HEAD_DIM), 1) == nb_last,
                                     last_sel, 0.0), axis=1, keepdims=True)
            seln = jnp.concatenate([seln] * GROUP, axis=0)
            srow = lax.broadcasted_iota(jnp.int32, (rq, HEAD_DIM), 0) & (ds - 1)
            kcol = lax.broadcasted_iota(jnp.int32, (rq, HEAD_DIM), 1)
            okn = (seln > 0.5) & (kcol <= srow) & (kcol < ds)
            online_update(h, jnp.where(okn, sn, NEG), vn)
            o_slc = acc_scr[h] / l_scr[h]
            wb = st_ref.shape[0]
            band = wb + HEAD_DIM
            st_flat = st_ref.reshape(wb * 8, HEAD_DIM)
            kw = jnp.concatenate([st_flat[pl.ds(h, wb, stride=8), :],
                                  wn_ref[:, h * HEAD_DIM:(h + 1) * HEAD_DIM], zpad], axis=0).astype(BF16)
            vw = jnp.concatenate([st_flat[pl.ds(N_KV + h, wb, stride=8), :],
                                  wn_ref[:, KV_W + h * HEAD_DIM:KV_W + (h + 1) * HEAD_DIM], zpad],
                                 axis=0).astype(BF16)
            sw = lax.dot_general(qq, kw, _NT, preferred_element_type=F32) * SCALE
            srw = lax.broadcasted_iota(jnp.int32, (rq, band), 0) & (ds - 1)
            idx = lax.broadcasted_iota(jnp.int32, (rq, band), 1)
            dlt = (wb + srw) - idx
            sw = jnp.where((dlt >= 0) & (dlt < WINDOW) & (idx < wb + ds), sw, NEG)
            ew = jnp.exp(sw - jnp.max(sw, axis=1, keepdims=True))
            o_win = jnp.dot(ew.astype(BF16), vw, preferred_element_type=F32) / jnp.sum(ew, axis=1, keepdims=True)
            o_cmp = ocmp_scr[h]
            gh = gt[h]
            for g in range(GROUP):
                r0, r1 = g * ds, (g + 1) * ds
                og = gh[:, g:g + 1] * o_cmp[r0:r1] + gh[:, GROUP + g:GROUP + g + 1] * o_slc[r0:r1]
                og = og + gh[:, 2 * GROUP + g:2 * GROUP + g + 1] * o_win[r0:r1]
                c0 = (h * GROUP + g) * HEAD_DIM
                o_ref[:, c0:c0 + HEAD_DIM] = og
        wb = st_ref.shape[0]
        wso_ref[0:wb - ds] = st_ref[ds:wb]
        wso_flat = wso_ref.reshape(wb * 8, HEAD_DIM)
        for cb in range(2 * N_KV):
            wso_flat[pl.ds((wb - ds) * 8 + cb, ds, stride=8), :] = wn_ref[:, cb * HEAD_DIM:(cb + 1) * HEAD_DIM]


def _attn_sample(q, kcv, kv_new, win_new, win_state, g_re, cache4, page_table, *, ds):
    nb, npg = page_table.shape
    past = npg * PAGE
    assert ds & (ds - 1) == 0 and ds <= SLC_LEN and past % SLC_LEN == 0 and npg % PAGES_PER_STEP == 0
    steps = npg // PAGES_PER_STEP
    ncp = kcv.shape[3]
    wb = win_state.shape[1]
    n_blocks = past // SLC_LEN + 1
    n_sel_chunks = -(-n_blocks // BLK_PER_CHUNK)
    lanes = -(-n_sel_chunks * BLK_PER_CHUNK // HEAD_DIM) * HEAD_DIM
    ms = jnp.asarray(_overlap_matrix(ncp, lanes, ncp - 1, n_blocks), dtype=BF16)
    ee = np.zeros((HEAD_DIM, CHUNK_ROWS), np.float32)
    ee[:BLK_PER_CHUNK] = np.arange(CHUNK_ROWS)[None, :] // SLC_LEN == np.arange(BLK_PER_CHUNK)[:, None]
    ee = jnp.asarray(ee, dtype=BF16)
    rq = GROUP * ds

    grid_spec = pltpu.PrefetchScalarGridSpec(
        num_scalar_prefetch=1,
        grid=(nb, steps),
        in_specs=[
            pl.BlockSpec(memory_space=pl.ANY),
            pl.BlockSpec((ds, N_HEADS * HEAD_DIM), lambda b, p, pt: (b, 0)),
            pl.BlockSpec((None, 2, N_KV, ncp, HEAD_DIM), lambda b, p, pt: (b, 0, 0, 0, 0)),
            pl.BlockSpec((ds, 4 * KV_W), lambda b, p, pt: (b, 0)),
            pl.BlockSpec((ds, 2 * KV_W), lambda b, p, pt: (b, 0)),
            pl.BlockSpec((None, wb, 8, HEAD_DIM), lambda b, p, pt: (b, 0, 0, 0)),
            pl.BlockSpec((None, N_KV, ds, 4 * GROUP), lambda b, p, pt: (b, 0, 0, 0)),
            pl.BlockSpec(ms.shape, lambda b, p, pt: (0, 0)),
            pl.BlockSpec(ee.shape, lambda b, p, pt: (0, 0))],
        out_specs=[pl.BlockSpec((ds, N_HEADS * HEAD_DIM), lambda b, p, pt: (b, 0)),
                   pl.BlockSpec((None, wb, 8, HEAD_DIM), lambda b, p, pt: (b, 0, 0, 0))],
        scratch_shapes=[pltpu.VMEM((N_KV, rq, 1), F32), pltpu.VMEM((N_KV, rq, 1), F32),
                        pltpu.VMEM((N_KV, rq, HEAD_DIM), F32),
                        pltpu.VMEM((N_KV, n_sel_chunks, ds, HEAD_DIM), F32),
                        pltpu.VMEM((N_KV, rq, HEAD_DIM), F32),
                        pltpu.VMEM((2, 2 * N_KV, CHUNK_ROWS, HEAD_DIM), F32),
                        pltpu.SemaphoreType.DMA((2,))],
    )
    return pl.pallas_call(
        functools.partial(_attn_sample_kernel, past=past, ds=ds),
        grid_spec=grid_spec,
        out_shape=[jax.ShapeDtypeStruct((nb * ds, N_HEADS * HEAD_DIM), F32),
                   jax.ShapeDtypeStruct((nb, wb, 8, HEAD_DIM), F32)],
        compiler_params=_cparams(("arbitrary", "arbitrary")),
        name="attn_sample",
    )(page_table, cache4, q, kcv, kv_new, win_new, win_state, g_re, ms, ee)


_Q0, _KV0, _WIN0, _G0 = 0, 2048, 4096, 5120
_REST0 = _G0 + 3 * N_HEADS


def _split_w_in(w_in):
    wt = w_in.T
    return wt, wt[:_G0 + HEAD_DIM].astype(BF16)


def _gate_layout(g_logits, nb, seq):
    g = g_logits[:, :3 * N_HEADS].reshape(nb, seq, 3, N_KV, GROUP)
    g = g.transpose(0, 3, 1, 2, 4).reshape(nb, N_KV, seq, 3 * GROUP)
    return jnp.pad(g, ((0, 0), (0, 0), (0, 0), (0, GROUP)))


def _cmp_weights(w_k1, w_k2, pe_k, w_v1, w_v2, pe_v):
    def by_row(w1):
        w = w1.reshape(2, CMP_STRIDE, HEAD_DIM, HEAD_DIM).transpose(1, 2, 0, 3)
        return w.reshape(CMP_STRIDE * HEAD_DIM, 2 * HEAD_DIM)
    w1pairs = jnp.concatenate([by_row(w_k1), by_row(w_v1)], axis=1).astype(BF16)
    pe8 = jnp.stack([jnp.broadcast_to(pe_k.reshape(1, -1), (8, CMP_LEN * HEAD_DIM)),
                     jnp.broadcast_to(pe_v.reshape(1, -1), (8, CMP_LEN * HEAD_DIM))])
    return w1pairs, pe8, jnp.stack([w_k1, w_v1]), jnp.stack([w_k2, w_v2])


def kernel(x_prompt, x_sample, c_prompt, c_sample, cache_nsa_kv, page_table, state_win_kv, state_conv, w_ada, b_ada, norm1_g, norm2_g, w_in, w_cmp_k1, w_cmp_k2, pe_cmp_k, w_cmp_v1, w_cmp_v2, pe_cmp_v, conv_w, conv_b, w_attn_proj, w_conv_proj, w_out, w_mlp1, w_mlp2, normf_g):
    d = D_MODEL
    nbp, seq, _ = x_prompt.shape
    nbs, ds, _ = x_sample.shape
    tp = nbp * seq
    ts = nbs * ds
    tc = 256
    depth = w_in.shape[0]
    assert depth == 1

    xp = x_prompt.reshape(tp, d)
    xs = x_sample.reshape(ts, d)
    normf = normf_g.reshape(1, d)

    l = 0
    c_all = jnp.concatenate([c_prompt, c_sample, jnp.zeros((16 - nbp - nbs, d), F32)], axis=0)
    ada = _ada(c_all, w_ada[l], b_ada[l].reshape(1, -1)).reshape(16, 6, d)
    ada_p = [ada[:nbp, k][:, None, :] for k in range(6)]
    ada_s = [jnp.repeat(ada[nbp:nbp + nbs, k], ds, axis=0)[None] for k in range(6)]

    w_t, w_head = _split_w_in(w_in[l])
    gates0 = _REST0 + 3 * d
    wa = w_attn_proj[l]
    wc = w_conv_proj[l]
    wo = w_out[l]
    g1n = norm1_g[l].reshape(1, d)
    g2n = norm2_g[l].reshape(1, d)
    cw = conv_w[l]
    cb = conv_b[l].reshape(1, d)
    w1pairs, pe8, w1s, w2s = _cmp_weights(w_cmp_k1[l], w_cmp_k2[l], pe_cmp_k[l],
                                          w_cmp_v1[l], w_cmp_v2[l], pe_cmp_v[l])

    tm = 1024
    hp, q_p = _norm_mm(xp, g1n, ada_p[1], ada_p[0], w_head, tm=tm, tn=1024, seq=seq, out_dtype=BF16,
                       col0=_Q0, n=N_HEADS * HEAD_DIM, name="norm_mm_q")
    wbp = min(WINDOW, seq)
    kv6_p, kvb_p = _mm_kv(hp, w_head, tm=tm, col0=_KV0)
    winb_p, win6_p, g_p = _mm_win(hp, w_head, tm=wbp, seq=seq, col0=_WIN0, gcol0=_G0)
    gates_p = _mm_wstat(hp, w_t, tm=tm, tn=1024, out_dtype=BF16, act="sigmoid", name="mm_gates",
                        col0=gates0, n=2 * d)
    z_p, conv_tiles, w1, w2 = _mm_conv(hp, w_t, cw, cb, w_mlp1[l], w_mlp2[l], tm=tm, tc=tc, seq=seq, col0=_REST0)

    pt_p = jnp.arange(tp // PAGE, dtype=jnp.int32).reshape(nbp, seq // PAGE)
    part_p = _cmp_part(kv6_p.reshape(tp // PAGE, PAGE, 16, HEAD_DIM), pt_p, w1pairs)
    kcv_p = _cmp_finish(part_p, pe8, w1s, w2s)
    o_p = _attn_prompt(q_p, kcv_p, kvb_p, winb_p, g_p, nb=nbp, seq=seq)

    mixed_p = _mix(o_p, z_p, wa, wc, gates_p, tm=tm, tn=512)
    x1_p, h2_p = _mm_resid_norm(mixed_p, wo, xp, ada_p[2], g2n, ada_p[4], ada_p[3], tm=512, seq=seq)
    y_p = _mlp_final(h2_p, w1, w2, x1_p, ada_p[5], normf, tm=512, tf=1024, seq=seq)

    hs, q_s = _norm_mm(xs, g1n, ada_s[1], ada_s[0], w_head, tm=ts, tn=512, seq=ds, out_dtype=F32,
                       col0=_Q0, n=N_HEADS * HEAD_DIM, name="norm_mm_q_s")
    kv_s = _mm(hs, w_head, tm=ts, tn=512, out_dtype=F32, name="mm_kv_s", col0=_KV0, n=4 * KV_W)
    win_s = _mm(hs, w_head, tm=ts, tn=512, out_dtype=F32, name="mm_win_s", col0=_WIN0, n=2 * KV_W)
    g_s = _mm(hs, w_head, tm=ts, tn=HEAD_DIM, out_dtype=F32, name="mm_g_s", col0=_G0, n=HEAD_DIM)
    gates_s = _mm_wstat(hs, w_t, tm=ts, tn=512, out_dtype=F32, act="sigmoid", name="mm_gates_s",
                        col0=gates0, n=2 * d)
    ubc_s = _mm_wstat(hs, w_t, tm=ts, tn=512, out_dtype=F32, name="mm_ubc_s", col0=_REST0, n=3 * d)
    st = state_conv[l]
    z_s, cu_s = _sample_conv(ubc_s, jnp.repeat(st[:, 0], ds, axis=0), jnp.repeat(st[:, 1], ds, axis=0),
                             cw, cb, tc=tc, seq=ds)

    cache4 = cache_nsa_kv[l].reshape(cache_nsa_kv.shape[1], PAGE, 16, HEAD_DIM)
    part_s = _cmp_part(cache4, page_table, w1pairs)
    kcv_s = _cmp_finish(part_s, pe8, w1s, w2s)
    wbs = state_win_kv.shape[2]
    wst = state_win_kv[l].reshape(nbs, wbs, 8, HEAD_DIM)
    o_s, wst_next = _attn_sample(q_s, kcv_s, kv_s, win_s, wst, _gate_layout(g_s, nbs, ds), cache4, page_table,
                                 ds=ds)

    mixed_s = _mix(o_s, z_s, wa, wc, gates_s, tm=ts, tn=512)
    x1_s, h2_s = _mm_resid_norm(mixed_s, wo, xs, ada_s[2], g2n, ada_s[4], ada_s[3], tm=ts, seq=ds)
    y_s = _mlp_final(h2_s, w1, w2, x1_s, ada_s[5], normf, tm=ts, tf=512, seq=ds)

    kv_prompt = kv6_p.reshape(1, nbp, seq, 4, N_KV, HEAD_DIM)
    kv_sample = kv_s.reshape(1, nbs, ds, 4, N_KV, HEAD_DIM)
    win_prompt = win6_p.reshape(1, nbp, wbp, 2, N_KV, HEAD_DIM)
    win_sample = wst_next.reshape(1, nbs, wbs, 2, N_KV, HEAD_DIM)
    tiles_per_seq = seq // tm
    conv_prompt = conv_tiles[tiles_per_seq - 1::tiles_per_seq][None]
    conv_sample = cu_s.reshape(nbs, ds, d)[None, :, ds - (CONV_W - 1):]
    return (y_p.reshape(nbp, seq, d), y_s.reshape(nbs, ds, d), kv_prompt, kv_sample,
            win_prompt, win_sample, conv_prompt, conv_sample)
```
